```python
import math
import jax, jax.numpy as jnp
from jax import lax
import numpy as np

D_MODEL = 1024
BATCH = 32
SEQ = 2048
DEPTH = 2

GRID_W = 64
CTX_LEN = 256
EPS = 1e-6

D_MIX = D_MODEL
A_WIDTH = D_MIX // 4
A_HEADS = 4
A_HEAD_DIM = A_WIDTH // A_HEADS
CHUNK = 128
B_WIDTH = D_MIX // 2
SSM_GROUP = 16
SSM_GROUPS = B_WIDTH // SSM_GROUP
SSM_STATE = 64
C_WIDTH = D_MIX - A_WIDTH - B_WIDTH
POOL_WINDOWS = (2, 4, 8, 16)
POOL_GROUP = C_WIDTH // len(POOL_WINDOWS)
D_IN = 2 * A_WIDTH + B_WIDTH + C_WIDTH
D_FF = ((-(-8 * D_MODEL // 3) + 255) // 256) * 256

kernel_name = "hybrid_gmlp_s5_pool_dit_prefix"


def rms_norm(x, g):
    xf = x.astype(jnp.float32)
    y = xf * lax.rsqrt(jnp.mean(xf * xf, axis=-1, keepdims=True) + EPS)
    return (y * g.astype(jnp.float32)).astype(x.dtype)


def layer_norm(x):
    xf = x.astype(jnp.float32)
    mu = jnp.mean(xf, axis=-1, keepdims=True)
    var = jnp.mean(jnp.square(xf - mu), axis=-1, keepdims=True)
    return ((xf - mu) * lax.rsqrt(var + EPS)).astype(x.dtype)


def modulate(h, shift, scale):
    return h * (1 + scale) + shift


def sincos_2d(rows, cols, dim):
    quarter = dim // 4
    omega = 1.0 / (10000.0 ** (jnp.arange(quarter, dtype=jnp.float32) / quarter))
    r = jnp.arange(rows, dtype=jnp.float32)[:, None] * omega
    cc = jnp.arange(cols, dtype=jnp.float32)[:, None] * omega
    er = jnp.concatenate([jnp.sin(r), jnp.cos(r)], axis=-1)
    ec = jnp.concatenate([jnp.sin(cc), jnp.cos(cc)], axis=-1)
    pe = jnp.concatenate([jnp.broadcast_to(er[:, None, :], (rows, cols, dim // 2)),
                          jnp.broadcast_to(ec[None, :, :], (rows, cols, dim // 2))], axis=-1)
    return pe.reshape(rows * cols, dim)


def spatial_gating(z, w_s, b_s):
    bsz, n, _ = z.shape
    z = jax.nn.gelu(z)
    u, v = jnp.split(z, 2, axis=-1)
    v = layer_norm(v.reshape(bsz, n // CHUNK, CHUNK, A_HEADS, A_HEAD_DIM))
    s = jnp.einsum('hpq,bkqhd->bkphd', w_s, v) + b_s.T[None, None, :, :, None]
    return u * s.reshape(bsz, n, A_WIDTH)


def ssm_discretize(lam_re, lam_im, log_dt, b_re, b_im):
    lam = lax.complex(lam_re.astype(jnp.float32), lam_im.astype(jnp.float32))
    dt = jnp.exp(log_dt.astype(jnp.float32))[:, None]
    lam_bar = jnp.exp(lam * dt)
    b = lax.complex(b_re.astype(jnp.float32), b_im.astype(jnp.float32))
    b_bar = ((lam_bar - 1.0) / lam)[..., None] * b
    return lam_bar, b_bar


def diag_scan(lam_bar, bu, h0, reverse):
    if h0 is not None:
        edge = bu.shape[1] - 1 if reverse else 0
        bu = bu.at[:, edge].add(lam_bar * h0)
    a = jnp.broadcast_to(lam_bar, bu.shape)

    def combine(e1, e2):
        a1, b1 = e1
        a2, b2 = e2
        return a1 * a2, a2 * b1 + b2

    _, h = lax.associative_scan(combine, (a, bu), reverse=reverse, axis=1)
    return h


def ssm_mixer(u_lat, u_ctx, lam_re, lam_im, log_dt, b_re, b_im, c_re, c_im, d, glu_w, glu_b, need_ctx):
    def groups(u):
        return u.reshape(u.shape[0], u.shape[1], SSM_GROUPS, SSM_GROUP).astype(jnp.float32)

    g_lat, g_ctx = groups(u_lat), groups(u_ctx)
    df = d.astype(jnp.float32)
    y_lat = df * g_lat
    y_ctx = df * g_ctx if need_ctx else None
    for k, reverse in enumerate((False, True)):
        lam_bar, b_bar = ssm_discretize(lam_re[k], lam_im[k], log_dt[k], b_re[k], b_im[k])
        cm = lax.complex(c_re[k].astype(jnp.float32), c_im[k].astype(jnp.float32))
        bu_ctx = jnp.einsum('blgh,gph->blgp', g_ctx.astype(jnp.complex64), b_bar)
        h_ctx = diag_scan(lam_bar, bu_ctx, None, reverse)
        h_end = h_ctx[:, 0] if reverse else h_ctx[:, -1]
        bu_lat = jnp.einsum('blgh,gph->blgp', g_lat.astype(jnp.complex64), b_bar)
        h_lat = diag_scan(lam_bar, bu_lat, h_end, reverse)
        y_lat = y_lat + jnp.einsum('ghp,blgp->blgh', cm, h_lat).real
        if need_ctx:
            y_ctx = y_ctx + jnp.einsum('ghp,blgp->blgh', cm, h_ctx).real

    def glu(y, dtype):
        g = jax.nn.gelu(y.reshape(y.shape[0], y.shape[1], B_WIDTH)).astype(dtype)
        return g * jax.nn.sigmoid(g @ glu_w + glu_b)

    out_lat = glu(y_lat, u_lat.dtype)
    out_ctx = glu(y_ctx, u_ctx.dtype) if need_ctx else None
    return out_lat, out_ctx


def window_mean(x, w):
    n = x.shape[-2]
    cs = jnp.cumsum(x.astype(jnp.float32), axis=-2)
    cs = jnp.concatenate([jnp.zeros_like(cs[..., :1, :]), cs], axis=-2)
    t = np.arange(n)
    lo = np.clip(t - w // 2, 0, n)
    hi = np.clip(t - w // 2 + w, 0, n)
    cnt = (hi - lo).astype(np.float32)[:, None]
    return ((jnp.take(cs, hi, axis=-2) - jnp.take(cs, lo, axis=-2)) / cnt).astype(x.dtype)


def pool_mixer(p, pool_w, pool_scale, rows):
    bsz, n, _ = p.shape
    outs = []
    for i, w in enumerate(POOL_WINDOWS):
        pg = p[..., i * POOL_GROUP:(i + 1) * POOL_GROUP]
        if rows is None:
            m = window_mean(pg, w)
        else:
            m = window_mean(pg.reshape(bsz, rows, GRID_W, POOL_GROUP), w).reshape(bsz, n, POOL_GROUP)
        outs.append((m - pg) @ pool_w[i])
    return jnp.concatenate(outs, axis=-1) * pool_scale


def mixing_sublayer(h_lat, h_ctx, rows, need_ctx, w_in, w_out, sgu_w, sgu_b,
                    lam_re, lam_im, log_dt, b_re, b_im, c_re, c_im, d, glu_w, glu_b,
                    pool_w, pool_scale):
    b_lo, b_hi = 2 * A_WIDTH, 2 * A_WIDTH + B_WIDTH
    z_lat = h_lat @ w_in
    if need_ctx:
        z_ctx = h_ctx @ w_in
        u_ctx = z_ctx[..., b_lo:b_hi]
    else:
        u_ctx = h_ctx @ w_in[:, b_lo:b_hi]
    a_lat = spatial_gating(z_lat[..., :b_lo], sgu_w, sgu_b)
    s_lat, s_ctx = ssm_mixer(z_lat[..., b_lo:b_hi], u_ctx, lam_re, lam_im, log_dt, b_re, b_im,
                             c_re, c_im, d, glu_w, glu_b, need_ctx)
    p_lat = pool_mixer(z_lat[..., b_hi:], pool_w, pool_scale, rows)
    m_lat = jnp.concatenate([a_lat, s_lat, p_lat], axis=-1) @ w_out
    m_ctx = None
    if need_ctx:
        a_ctx = spatial_gating(z_ctx[..., :b_lo], sgu_w, sgu_b)
        p_ctx = pool_mixer(z_ctx[..., b_hi:], pool_w, pool_scale, None)
        m_ctx = jnp.concatenate([a_ctx, s_ctx, p_ctx], axis=-1) @ w_out
    return m_lat, m_ctx


def swiglu(h, w_gate, w_up, w_down):
    return (jax.nn.silu(h @ w_gate) * (h @ w_up)) @ w_down


def _fwd_setup_inputs(seed: int = 0) -> dict:
    key = jax.random.key(seed)
    ks = jax.random.split(key, 32)
    f32 = jnp.float32

    def nrm(k, shape, scale):
        return jax.random.normal(k, shape, f32) * scale

    lam_im0 = math.pi * jnp.arange(SSM_STATE, dtype=f32)
    return {
        "x": nrm(ks[0], (BATCH, SEQ, D_MODEL), 1.0),
        "c": nrm(ks[1], (BATCH, D_MODEL), 1.0),
        "ctx": nrm(ks[2], (BATCH, CTX_LEN, D_MODEL), 1.0),
        "c_ctx": nrm(ks[3], (D_MODEL,), 1.0),
        "w_mod": nrm(ks[4], (DEPTH, D_MODEL, 6 * D_MODEL), 0.5 * D_MODEL ** -0.5),
        "b_mod": nrm(ks[5], (DEPTH, 6 * D_MODEL), 0.02),
        "norm_mix_pre": 1.0 + nrm(ks[6], (DEPTH, D_MODEL), 0.1),
        "norm_mix_post": 1.0 + nrm(ks[7], (DEPTH, D_MODEL), 0.1),
        "norm_ffn_pre": 1.0 + nrm(ks[8], (DEPTH, D_MODEL), 0.1),
        "norm_ffn_post": 1.0 + nrm(ks[9], (DEPTH, D_MODEL), 0.1),
        "w_in": nrm(ks[10], (DEPTH, D_MODEL, D_IN), D_MODEL ** -0.5),
        "w_out": nrm(ks[11], (DEPTH, D_MIX, D_MODEL), D_MIX ** -0.5),
        "sgu_w": nrm(ks[12], (DEPTH, A_HEADS, CHUNK, CHUNK), CHUNK ** -0.5),
        "sgu_b": 1.0 + nrm(ks[13], (DEPTH, A_HEADS, CHUNK), 0.1),
        "ssm_lam_re": -0.5 + nrm(ks[14], (DEPTH, 2, SSM_GROUPS, SSM_STATE), 0.01),
        "ssm_lam_im": lam_im0 + nrm(ks[15], (DEPTH, 2, SSM_GROUPS, SSM_STATE), 0.01),
        "ssm_log_dt": jax.random.uniform(ks[16], (DEPTH, 2, SSM_GROUPS), f32,
                                         minval=math.log(1e-3), maxval=math.log(1e-1)),
        "ssm_b_re": nrm(ks[17], (DEPTH, 2, SSM_GROUPS, SSM_STATE, SSM_GROUP), (2 * SSM_GROUP) ** -0.5),
        "ssm_b_im": nrm(ks[18], (DEPTH, 2, SSM_GROUPS, SSM_STATE, SSM_GROUP), (2 * SSM_GROUP) ** -0.5),
        "ssm_c_re": nrm(ks[19], (DEPTH, 2, SSM_GROUPS, SSM_GROUP, SSM_STATE), SSM_STATE ** -0.5),
        "ssm_c_im": nrm(ks[20], (DEPTH, 2, SSM_GROUPS, SSM_GROUP, SSM_STATE), SSM_STATE ** -0.5),
        "ssm_d": nrm(ks[21], (DEPTH, SSM_GROUPS, SSM_GROUP), 1.0),
        "glu_w": nrm(ks[22], (DEPTH, B_WIDTH, B_WIDTH), B_WIDTH ** -0.5),
        "glu_b": nrm(ks[23], (DEPTH, B_WIDTH), 0.02),
        "pool_w": nrm(ks[24], (DEPTH, len(POOL_WINDOWS), POOL_GROUP, POOL_GROUP), POOL_GROUP ** -0.5),
        "pool_scale": 1.0 + nrm(ks[25], (DEPTH, C_WIDTH), 0.1),
        "ffn_w_gate": nrm(ks[26], (DEPTH, D_MODEL, D_FF), D_MODEL ** -0.5),
        "ffn_w_up": nrm(ks[27], (DEPTH, D_MODEL, D_FF), D_MODEL ** -0.5),
        "ffn_w_down": nrm(ks[28], (DEPTH, D_FF, D_MODEL), D_FF ** -0.5),
    }


def _fwd_reference(x, c, ctx, c_ctx, w_mod, b_mod, norm_mix_pre, norm_mix_post, norm_ffn_pre, norm_ffn_post,
              w_in, w_out, sgu_w, sgu_b, ssm_lam_re, ssm_lam_im, ssm_log_dt, ssm_b_re, ssm_b_im,
              ssm_c_re, ssm_c_im, ssm_d, glu_w, glu_b, pool_w, pool_scale,
              ffn_w_gate, ffn_w_up, ffn_w_down):
    n_lat = x.shape[1]
    ROWS = n_lat // GRID_W
    x_lat = x + sincos_2d(ROWS, GRID_W, x.shape[-1]).astype(x.dtype)[None]
    x_ctx = ctx
    for i in range(DEPTH):
        need_ctx = i < DEPTH - 1
        mod_lat = jax.nn.silu(c) @ w_mod[i] + b_mod[i]
        mod_ctx = jax.nn.silu(c_ctx) @ w_mod[i] + b_mod[i]
        sh1, sc1, g1, sh2, sc2, g2 = [m[:, None, :] for m in jnp.split(mod_lat, 6, axis=-1)]
        csh1, csc1, cg1, csh2, csc2, cg2 = jnp.split(mod_ctx, 6, axis=-1)

        h_lat = modulate(rms_norm(x_lat, norm_mix_pre[i]), sh1, sc1)
        h_ctx = modulate(rms_norm(x_ctx, norm_mix_pre[i]), csh1, csc1)
        m_lat, m_ctx = mixing_sublayer(h_lat, h_ctx, ROWS, need_ctx, w_in[i], w_out[i], sgu_w[i], sgu_b[i],
                                       ssm_lam_re[i], ssm_lam_im[i], ssm_log_dt[i], ssm_b_re[i], ssm_b_im[i],
                                       ssm_c_re[i], ssm_c_im[i], ssm_d[i], glu_w[i], glu_b[i],
                                       pool_w[i], pool_scale[i])
        x_lat = x_lat + g1 * rms_norm(m_lat, norm_mix_post[i])
        f_lat = swiglu(modulate(rms_norm(x_lat, norm_ffn_pre[i]), sh2, sc2),
                       ffn_w_gate[i], ffn_w_up[i], ffn_w_down[i])
        x_lat = x_lat + g2 * rms_norm(f_lat, norm_ffn_post[i])
        if need_ctx:
            x_ctx = x_ctx + cg1 * rms_norm(m_ctx, norm_mix_post[i])
            f_ctx = swiglu(modulate(rms_norm(x_ctx, norm_ffn_pre[i]), csh2, csc2),
                           ffn_w_gate[i], ffn_w_up[i], ffn_w_down[i])
            x_ctx = x_ctx + cg2 * rms_norm(f_ctx, norm_ffn_post[i])
    return x_lat


import jax as _jax
import jax.numpy as _jnp

TWIN_FORMAT = 'train_step'
FWD_PARAMS = ['x', 'c', 'ctx', 'c_ctx', 'w_mod', 'b_mod', 'norm_mix_pre', 'norm_mix_post', 'norm_ffn_pre', 'norm_ffn_post', 'w_in', 'w_out', 'sgu_w', 'sgu_b', 'ssm_lam_re', 'ssm_lam_im', 'ssm_log_dt', 'ssm_b_re', 'ssm_b_im', 'ssm_c_re', 'ssm_c_im', 'ssm_d', 'glu_w', 'glu_b', 'pool_w', 'pool_scale', 'ffn_w_gate', 'ffn_w_up', 'ffn_w_down']
TWIN_WEIGHTS = ['c_ctx', 'w_mod', 'b_mod', 'norm_mix_pre', 'norm_mix_post', 'norm_ffn_pre', 'norm_ffn_post', 'w_in', 'w_out', 'sgu_w', 'sgu_b', 'ssm_lam_re', 'ssm_lam_im', 'ssm_log_dt', 'ssm_b_re', 'ssm_b_im', 'ssm_c_re', 'ssm_c_im', 'ssm_d', 'glu_w', 'glu_b', 'pool_w', 'pool_scale', 'ffn_w_gate', 'ffn_w_up', 'ffn_w_down']
TWIN_DIFF_INPUT = 'x'
TWIN_INPUTS = ['x', 'c', 'ctx', 'c_ctx', 'w_mod', 'b_mod', 'norm_mix_pre', 'norm_mix_post', 'norm_ffn_pre', 'norm_ffn_post', 'w_in', 'w_out', 'sgu_w', 'sgu_b', 'ssm_lam_re', 'ssm_lam_im', 'ssm_log_dt', 'ssm_b_re', 'ssm_b_im', 'ssm_c_re', 'ssm_c_im', 'ssm_d', 'glu_w', 'glu_b', 'pool_w', 'pool_scale', 'ffn_w_gate', 'ffn_w_up', 'ffn_w_down', 'loss_target', 'm_c_ctx', 'm_w_mod', 'm_b_mod', 'm_norm_mix_pre', 'm_norm_mix_post', 'm_norm_ffn_pre', 'm_norm_ffn_post', 'm_w_in', 'm_w_out', 'm_sgu_w', 'm_sgu_b', 'm_ssm_lam_re', 'm_ssm_lam_im', 'm_ssm_log_dt', 'm_ssm_b_re', 'm_ssm_b_im', 'm_ssm_c_re', 'm_ssm_c_im', 'm_ssm_d', 'm_glu_w', 'm_glu_b', 'm_pool_w', 'm_pool_scale', 'm_ffn_w_gate', 'm_ffn_w_up', 'm_ffn_w_down', 'v_c_ctx', 'v_w_mod', 'v_b_mod', 'v_norm_mix_pre', 'v_norm_mix_post', 'v_norm_ffn_pre', 'v_norm_ffn_post', 'v_w_in', 'v_w_out', 'v_sgu_w', 'v_sgu_b', 'v_ssm_lam_re', 'v_ssm_lam_im', 'v_ssm_log_dt', 'v_ssm_b_re', 'v_ssm_b_im', 'v_ssm_c_re', 'v_ssm_c_im', 'v_ssm_d', 'v_glu_w', 'v_glu_b', 'v_pool_w', 'v_pool_scale', 'v_ffn_w_gate', 'v_ffn_w_up', 'v_ffn_w_down']
TWIN_OUTPUTS = ['loss', 'grad_x', 'grad_c_ctx', 'grad_w_mod', 'grad_b_mod', 'grad_norm_mix_pre', 'grad_norm_mix_post', 'grad_norm_ffn_pre', 'grad_norm_ffn_post', 'grad_w_in', 'grad_w_out', 'grad_sgu_w', 'grad_sgu_b', 'grad_ssm_lam_re', 'grad_ssm_lam_im', 'grad_ssm_log_dt', 'grad_ssm_b_re', 'grad_ssm_b_im', 'grad_ssm_c_re', 'grad_ssm_c_im', 'grad_ssm_d', 'grad_glu_w', 'grad_glu_b', 'grad_pool_w', 'grad_pool_scale', 'grad_ffn_w_gate', 'grad_ffn_w_up', 'grad_ffn_w_down', 'delta_c_ctx', 'delta_w_mod', 'delta_b_mod', 'delta_norm_mix_pre', 'delta_norm_mix_post', 'delta_norm_ffn_pre', 'delta_norm_ffn_post', 'delta_w_in', 'delta_w_out', 'delta_sgu_w', 'delta_sgu_b', 'delta_ssm_lam_re', 'delta_ssm_lam_im', 'delta_ssm_log_dt', 'delta_ssm_b_re', 'delta_ssm_b_im', 'delta_ssm_c_re', 'delta_ssm_c_im', 'delta_ssm_d', 'delta_glu_w', 'delta_glu_b', 'delta_pool_w', 'delta_pool_scale', 'delta_ffn_w_gate', 'delta_ffn_w_up', 'delta_ffn_w_down', 'new_m_c_ctx', 'new_m_w_mod', 'new_m_b_mod', 'new_m_norm_mix_pre', 'new_m_norm_mix_post', 'new_m_norm_ffn_pre', 'new_m_norm_ffn_post', 'new_m_w_in', 'new_m_w_out', 'new_m_sgu_w', 'new_m_sgu_b', 'new_m_ssm_lam_re', 'new_m_ssm_lam_im', 'new_m_ssm_log_dt', 'new_m_ssm_b_re', 'new_m_ssm_b_im', 'new_m_ssm_c_re', 'new_m_ssm_c_im', 'new_m_ssm_d', 'new_m_glu_w', 'new_m_glu_b', 'new_m_pool_w', 'new_m_pool_scale', 'new_m_ffn_w_gate', 'new_m_ffn_w_up', 'new_m_ffn_w_down', 'new_v_c_ctx', 'new_v_w_mod', 'new_v_b_mod', 'new_v_norm_mix_pre', 'new_v_norm_mix_post', 'new_v_norm_ffn_pre', 'new_v_norm_ffn_post', 'new_v_w_in', 'new_v_w_out', 'new_v_sgu_w', 'new_v_sgu_b', 'new_v_ssm_lam_re', 'new_v_ssm_lam_im', 'new_v_ssm_log_dt', 'new_v_ssm_b_re', 'new_v_ssm_b_im', 'new_v_ssm_c_re', 'new_v_ssm_c_im', 'new_v_ssm_d', 'new_v_glu_w', 'new_v_glu_b', 'new_v_pool_w', 'new_v_pool_scale', 'new_v_ffn_w_gate', 'new_v_ffn_w_up', 'new_v_ffn_w_down']
TWIN_LEAF_KINDS = {'loss': 'loss', 'grad_x': 'grad_x', 'grad_c_ctx': 'grad_w', 'grad_w_mod': 'grad_w', 'grad_b_mod': 'grad_w', 'grad_norm_mix_pre': 'grad_w', 'grad_norm_mix_post': 'grad_w', 'grad_norm_ffn_pre': 'grad_w', 'grad_norm_ffn_post': 'grad_w', 'grad_w_in': 'grad_w', 'grad_w_out': 'grad_w', 'grad_sgu_w': 'grad_w', 'grad_sgu_b': 'grad_w', 'grad_ssm_lam_re': 'grad_w', 'grad_ssm_lam_im': 'grad_w', 'grad_ssm_log_dt': 'grad_w', 'grad_ssm_b_re': 'grad_w', 'grad_ssm_b_im': 'grad_w', 'grad_ssm_c_re': 'grad_w', 'grad_ssm_c_im': 'grad_w', 'grad_ssm_d': 'grad_w', 'grad_glu_w': 'grad_w', 'grad_glu_b': 'grad_w', 'grad_pool_w': 'grad_w', 'grad_pool_scale': 'grad_w', 'grad_ffn_w_gate': 'grad_w', 'grad_ffn_w_up': 'grad_w', 'grad_ffn_w_down': 'grad_w', 'delta_c_ctx': 'delta_w', 'delta_w_mod': 'delta_w', 'delta_b_mod': 'delta_w', 'delta_norm_mix_pre': 'delta_w', 'delta_norm_mix_post': 'delta_w', 'delta_norm_ffn_pre': 'delta_w', 'delta_norm_ffn_post': 'delta_w', 'delta_w_in': 'delta_w', 'delta_w_out': 'delta_w', 'delta_sgu_w': 'delta_w', 'delta_sgu_b': 'delta_w', 'delta_ssm_lam_re': 'delta_w', 'delta_ssm_lam_im': 'delta_w', 'delta_ssm_log_dt': 'delta_w', 'delta_ssm_b_re': 'delta_w', 'delta_ssm_b_im': 'delta_w', 'delta_ssm_c_re': 'delta_w', 'delta_ssm_c_im': 'delta_w', 'delta_ssm_d': 'delta_w', 'delta_glu_w': 'delta_w', 'delta_glu_b': 'delta_w', 'delta_pool_w': 'delta_w', 'delta_pool_scale': 'delta_w', 'delta_ffn_w_gate': 'delta_w', 'delta_ffn_w_up': 'delta_w', 'delta_ffn_w_down': 'delta_w', 'new_m_c_ctx': 'new_m', 'new_m_w_mod': 'new_m', 'new_m_b_mod': 'new_m', 'new_m_norm_mix_pre': 'new_m', 'new_m_norm_mix_post': 'new_m', 'new_m_norm_ffn_pre': 'new_m', 'new_m_norm_ffn_post': 'new_m', 'new_m_w_in': 'new_m', 'new_m_w_out': 'new_m', 'new_m_sgu_w': 'new_m', 'new_m_sgu_b': 'new_m', 'new_m_ssm_lam_re': 'new_m', 'new_m_ssm_lam_im': 'new_m', 'new_m_ssm_log_dt': 'new_m', 'new_m_ssm_b_re': 'new_m', 'new_m_ssm_b_im': 'new_m', 'new_m_ssm_c_re': 'new_m', 'new_m_ssm_c_im': 'new_m', 'new_m_ssm_d': 'new_m', 'new_m_glu_w': 'new_m', 'new_m_glu_b': 'new_m', 'new_m_pool_w': 'new_m', 'new_m_pool_scale': 'new_m', 'new_m_ffn_w_gate': 'new_m', 'new_m_ffn_w_up': 'new_m', 'new_m_ffn_w_down': 'new_m', 'new_v_c_ctx': 'new_v', 'new_v_w_mod': 'new_v', 'new_v_b_mod': 'new_v', 'new_v_norm_mix_pre': 'new_v', 'new_v_norm_mix_post': 'new_v', 'new_v_norm_ffn_pre': 'new_v', 'new_v_norm_ffn_post': 'new_v', 'new_v_w_in': 'new_v', 'new_v_w_out': 'new_v', 'new_v_sgu_w': 'new_v', 'new_v_sgu_b': 'new_v', 'new_v_ssm_lam_re': 'new_v', 'new_v_ssm_lam_im': 'new_v', 'new_v_ssm_log_dt': 'new_v', 'new_v_ssm_b_re': 'new_v', 'new_v_ssm_b_im': 'new_v', 'new_v_ssm_c_re': 'new_v', 'new_v_ssm_c_im': 'new_v', 'new_v_ssm_d': 'new_v', 'new_v_glu_w': 'new_v', 'new_v_glu_b': 'new_v', 'new_v_pool_w': 'new_v', 'new_v_pool_scale': 'new_v', 'new_v_ffn_w_gate': 'new_v', 'new_v_ffn_w_up': 'new_v', 'new_v_ffn_w_down': 'new_v'}


def _forward(args):
    return _fwd_reference(*[args[k] for k in FWD_PARAMS])


def _output_shape():
    out = _jax.eval_shape(lambda: _forward(_fwd_setup_inputs(0)))
    return out.shape, out.dtype

N_MICROBATCH = 1
ADAM_LR = 0.001
ADAM_B1 = 0.9
ADAM_B2 = 0.999
ADAM_EPS = 1e-08
ADAM_WD = 0.01
ADAM_STEP = 10
PER_EXAMPLE_BATCH_AXIS = {'x': 0, 'c': 0, 'ctx': 0, 'loss_target': 0}
SHARED_INPUTS = []
_WEIGHT_DTYPES = {'c_ctx': _jnp.float32, 'w_mod': _jnp.float32, 'b_mod': _jnp.float32, 'norm_mix_pre': _jnp.float32, 'norm_mix_post': _jnp.float32, 'norm_ffn_pre': _jnp.float32, 'norm_ffn_post': _jnp.float32, 'w_in': _jnp.float32, 'w_out': _jnp.float32, 'sgu_w': _jnp.float32, 'sgu_b': _jnp.float32, 'ssm_lam_re': _jnp.float32, 'ssm_lam_im': _jnp.float32, 'ssm_log_dt': _jnp.float32, 'ssm_b_re': _jnp.float32, 'ssm_b_im': _jnp.float32, 'ssm_c_re': _jnp.float32, 'ssm_c_im': _jnp.float32, 'ssm_d': _jnp.float32, 'glu_w': _jnp.float32, 'glu_b': _jnp.float32, 'pool_w': _jnp.float32, 'pool_scale': _jnp.float32, 'ffn_w_gate': _jnp.float32, 'ffn_w_up': _jnp.float32, 'ffn_w_down': _jnp.float32}
MOMENT_SCALE = {'c_ctx': 1.236926e-02, 'w_mod': 4.755088e+00, 'b_mod': 9.977062e+00, 'norm_mix_pre': 1.417603e+00, 'norm_mix_post': 8.463277e+00, 'norm_ffn_pre': 1.854811e+00, 'norm_ffn_post': 8.099633e+00, 'w_in': 1.405328e+00, 'w_out': 2.701162e+00, 'sgu_w': 1.409296e-01, 'sgu_b': 2.511962e-01, 'ssm_lam_re': 2.537750e-01, 'ssm_lam_im': 2.906269e-01, 'ssm_log_dt': 2.376927e+00, 'ssm_b_re': 1.558232e-01, 'ssm_b_im': 1.707043e-01, 'ssm_c_re': 2.377697e-01, 'ssm_c_im': 2.572838e-01, 'ssm_d': 2.593394e+00, 'glu_w': 5.994630e-01, 'glu_b': 1.109246e+00, 'pool_w': 2.749924e-01, 'pool_scale': 3.050516e-01, 'ffn_w_gate': 8.654140e-01, 'ffn_w_up': 1.105990e+00, 'ffn_w_down': 1.819403e+00}


def _to_microbatches(a, axis):
    t = _jnp.moveaxis(a, axis, 0)
    t = t.reshape((N_MICROBATCH, t.shape[0] // N_MICROBATCH) + t.shape[1:])
    return _jnp.moveaxis(t, 1, axis + 1)


def setup_inputs(seed: int = 0) -> dict:
    inp = _fwd_setup_inputs(seed)
    key = _jax.random.fold_in(_jax.random.key(seed), 7919)
    shape, _ = _output_shape()
    out = dict(inp)
    out["loss_target"] = _jax.random.normal(_jax.random.fold_in(key, 0), shape, _jnp.float32)
    for i, name in enumerate(TWIN_WEIGHTS):
        w = inp[name].astype(_jnp.float32)
        if MOMENT_SCALE is None:
            s = _jnp.sqrt(_jnp.mean(_jnp.square(w)) + 1e-30)
        else:
            s = MOMENT_SCALE[name]
        km, kv = _jax.random.split(_jax.random.fold_in(key, i + 1))
        out[name] = w
        out["m_" + name] = s * _jax.random.normal(km, w.shape, _jnp.float32)
        out["v_" + name] = (s * s) * _jax.random.uniform(kv, w.shape, _jnp.float32, 0.5, 1.5)
    if N_MICROBATCH > 1:
        for name, axis in PER_EXAMPLE_BATCH_AXIS.items():
            out[name] = _to_microbatches(out[name], axis)
    return {'x': out['x'], 'c': out['c'], 'ctx': out['ctx'], 'c_ctx': out['c_ctx'], 'w_mod': out['w_mod'], 'b_mod': out['b_mod'], 'norm_mix_pre': out['norm_mix_pre'], 'norm_mix_post': out['norm_mix_post'], 'norm_ffn_pre': out['norm_ffn_pre'], 'norm_ffn_post': out['norm_ffn_post'], 'w_in': out['w_in'], 'w_out': out['w_out'], 'sgu_w': out['sgu_w'], 'sgu_b': out['sgu_b'], 'ssm_lam_re': out['ssm_lam_re'], 'ssm_lam_im': out['ssm_lam_im'], 'ssm_log_dt': out['ssm_log_dt'], 'ssm_b_re': out['ssm_b_re'], 'ssm_b_im': out['ssm_b_im'], 'ssm_c_re': out['ssm_c_re'], 'ssm_c_im': out['ssm_c_im'], 'ssm_d': out['ssm_d'], 'glu_w': out['glu_w'], 'glu_b': out['glu_b'], 'pool_w': out['pool_w'], 'pool_scale': out['pool_scale'], 'ffn_w_gate': out['ffn_w_gate'], 'ffn_w_up': out['ffn_w_up'], 'ffn_w_down': out['ffn_w_down'], 'loss_target': out['loss_target'], 'm_c_ctx': out['m_c_ctx'], 'm_w_mod': out['m_w_mod'], 'm_b_mod': out['m_b_mod'], 'm_norm_mix_pre': out['m_norm_mix_pre'], 'm_norm_mix_post': out['m_norm_mix_post'], 'm_norm_ffn_pre': out['m_norm_ffn_pre'], 'm_norm_ffn_post': out['m_norm_ffn_post'], 'm_w_in': out['m_w_in'], 'm_w_out': out['m_w_out'], 'm_sgu_w': out['m_sgu_w'], 'm_sgu_b': out['m_sgu_b'], 'm_ssm_lam_re': out['m_ssm_lam_re'], 'm_ssm_lam_im': out['m_ssm_lam_im'], 'm_ssm_log_dt': out['m_ssm_log_dt'], 'm_ssm_b_re': out['m_ssm_b_re'], 'm_ssm_b_im': out['m_ssm_b_im'], 'm_ssm_c_re': out['m_ssm_c_re'], 'm_ssm_c_im': out['m_ssm_c_im'], 'm_ssm_d': out['m_ssm_d'], 'm_glu_w': out['m_glu_w'], 'm_glu_b': out['m_glu_b'], 'm_pool_w': out['m_pool_w'], 'm_pool_scale': out['m_pool_scale'], 'm_ffn_w_gate': out['m_ffn_w_gate'], 'm_ffn_w_up': out['m_ffn_w_up'], 'm_ffn_w_down': out['m_ffn_w_down'], 'v_c_ctx': out['v_c_ctx'], 'v_w_mod': out['v_w_mod'], 'v_b_mod': out['v_b_mod'], 'v_norm_mix_pre': out['v_norm_mix_pre'], 'v_norm_mix_post': out['v_norm_mix_post'], 'v_norm_ffn_pre': out['v_norm_ffn_pre'], 'v_norm_ffn_post': out['v_norm_ffn_post'], 'v_w_in': out['v_w_in'], 'v_w_out': out['v_w_out'], 'v_sgu_w': out['v_sgu_w'], 'v_sgu_b': out['v_sgu_b'], 'v_ssm_lam_re': out['v_ssm_lam_re'], 'v_ssm_lam_im': out['v_ssm_lam_im'], 'v_ssm_log_dt': out['v_ssm_log_dt'], 'v_ssm_b_re': out['v_ssm_b_re'], 'v_ssm_b_im': out['v_ssm_b_im'], 'v_ssm_c_re': out['v_ssm_c_re'], 'v_ssm_c_im': out['v_ssm_c_im'], 'v_ssm_d': out['v_ssm_d'], 'v_glu_w': out['v_glu_w'], 'v_glu_b': out['v_glu_b'], 'v_pool_w': out['v_pool_w'], 'v_pool_scale': out['v_pool_scale'], 'v_ffn_w_gate': out['v_ffn_w_gate'], 'v_ffn_w_up': out['v_ffn_w_up'], 'v_ffn_w_down': out['v_ffn_w_down']}


def _loss(weights, diff, rest, loss_target):
    with _jax.named_scope("forward"):
        args = {**rest, TWIN_DIFF_INPUT: diff, **{k: w.astype(_WEIGHT_DTYPES[k]) for k, w in weights.items()}}
        y = _forward(args)
    with _jax.named_scope("loss_head"):
        err = _jnp.square(y.astype(_jnp.float32) - loss_target)
        return 0.5 * _jnp.sum(_jnp.mean(err, axis=-1)) if err.ndim else 0.5 * err


def _adamw(w, g, m, v):
    m = ADAM_B1 * m + (1.0 - ADAM_B1) * g
    v = ADAM_B2 * v + (1.0 - ADAM_B2) * _jnp.square(g)
    m_hat = m / (1.0 - ADAM_B1 ** ADAM_STEP)
    v_hat = v / (1.0 - ADAM_B2 ** ADAM_STEP)
    delta = -ADAM_LR * (m_hat / (_jnp.sqrt(v_hat) + ADAM_EPS) + ADAM_WD * w)
    return delta, m, v


def reference(x, c, ctx, c_ctx, w_mod, b_mod, norm_mix_pre, norm_mix_post, norm_ffn_pre, norm_ffn_post, w_in, w_out, sgu_w, sgu_b, ssm_lam_re, ssm_lam_im, ssm_log_dt, ssm_b_re, ssm_b_im, ssm_c_re, ssm_c_im, ssm_d, glu_w, glu_b, pool_w, pool_scale, ffn_w_gate, ffn_w_up, ffn_w_down, loss_target, m_c_ctx, m_w_mod, m_b_mod, m_norm_mix_pre, m_norm_mix_post, m_norm_ffn_pre, m_norm_ffn_post, m_w_in, m_w_out, m_sgu_w, m_sgu_b, m_ssm_lam_re, m_ssm_lam_im, m_ssm_log_dt, m_ssm_b_re, m_ssm_b_im, m_ssm_c_re, m_ssm_c_im, m_ssm_d, m_glu_w, m_glu_b, m_pool_w, m_pool_scale, m_ffn_w_gate, m_ffn_w_up, m_ffn_w_down, v_c_ctx, v_w_mod, v_b_mod, v_norm_mix_pre, v_norm_mix_post, v_norm_ffn_pre, v_norm_ffn_post, v_w_in, v_w_out, v_sgu_w, v_sgu_b, v_ssm_lam_re, v_ssm_lam_im, v_ssm_log_dt, v_ssm_b_re, v_ssm_b_im, v_ssm_c_re, v_ssm_c_im, v_ssm_d, v_glu_w, v_glu_b, v_pool_w, v_pool_scale, v_ffn_w_gate, v_ffn_w_up, v_ffn_w_down):
    given = dict(x=x, c=c, ctx=ctx, c_ctx=c_ctx, w_mod=w_mod, b_mod=b_mod, norm_mix_pre=norm_mix_pre, norm_mix_post=norm_mix_post, norm_ffn_pre=norm_ffn_pre, norm_ffn_post=norm_ffn_post, w_in=w_in, w_out=w_out, sgu_w=sgu_w, sgu_b=sgu_b, ssm_lam_re=ssm_lam_re, ssm_lam_im=ssm_lam_im, ssm_log_dt=ssm_log_dt, ssm_b_re=ssm_b_re, ssm_b_im=ssm_b_im, ssm_c_re=ssm_c_re, ssm_c_im=ssm_c_im, ssm_d=ssm_d, glu_w=glu_w, glu_b=glu_b, pool_w=pool_w, pool_scale=pool_scale, ffn_w_gate=ffn_w_gate, ffn_w_up=ffn_w_up, ffn_w_down=ffn_w_down, loss_target=loss_target, m_c_ctx=m_c_ctx, m_w_mod=m_w_mod, m_b_mod=m_b_mod, m_norm_mix_pre=m_norm_mix_pre, m_norm_mix_post=m_norm_mix_post, m_norm_ffn_pre=m_norm_ffn_pre, m_norm_ffn_post=m_norm_ffn_post, m_w_in=m_w_in, m_w_out=m_w_out, m_sgu_w=m_sgu_w, m_sgu_b=m_sgu_b, m_ssm_lam_re=m_ssm_lam_re, m_ssm_lam_im=m_ssm_lam_im, m_ssm_log_dt=m_ssm_log_dt, m_ssm_b_re=m_ssm_b_re, m_ssm_b_im=m_ssm_b_im, m_ssm_c_re=m_ssm_c_re, m_ssm_c_im=m_ssm_c_im, m_ssm_d=m_ssm_d, m_glu_w=m_glu_w, m_glu_b=m_glu_b, m_pool_w=m_pool_w, m_pool_scale=m_pool_scale, m_ffn_w_gate=m_ffn_w_gate, m_ffn_w_up=m_ffn_w_up, m_ffn_w_down=m_ffn_w_down, v_c_ctx=v_c_ctx, v_w_mod=v_w_mod, v_b_mod=v_b_mod, v_norm_mix_pre=v_norm_mix_pre, v_norm_mix_post=v_norm_mix_post, v_norm_ffn_pre=v_norm_ffn_pre, v_norm_ffn_post=v_norm_ffn_post, v_w_in=v_w_in, v_w_out=v_w_out, v_sgu_w=v_sgu_w, v_sgu_b=v_sgu_b, v_ssm_lam_re=v_ssm_lam_re, v_ssm_lam_im=v_ssm_lam_im, v_ssm_log_dt=v_ssm_log_dt, v_ssm_b_re=v_ssm_b_re, v_ssm_b_im=v_ssm_b_im, v_ssm_c_re=v_ssm_c_re, v_ssm_c_im=v_ssm_c_im, v_ssm_d=v_ssm_d, v_glu_w=v_glu_w, v_glu_b=v_glu_b, v_pool_w=v_pool_w, v_pool_scale=v_pool_scale, v_ffn_w_gate=v_ffn_w_gate, v_ffn_w_up=v_ffn_w_up, v_ffn_w_down=v_ffn_w_down)
    weights = {n: given[n] for n in TWIN_WEIGHTS}
    shared = {n: given[n] for n in SHARED_INPUTS}
    per_example = {n: given[n] for n in ['x', 'c', 'ctx']}
    grad_fn = _jax.value_and_grad(_loss, argnums=(0, 1))

    def one_microbatch(ex, loss_target):
        ex = dict(ex)
        diff = ex.pop(TWIN_DIFF_INPUT)
        return grad_fn(weights, diff, {**shared, **ex}, loss_target)

    if N_MICROBATCH == 1:
        loss, (grad_w, grad_x) = one_microbatch(per_example, given["loss_target"])
    else:
        def body(carry, xs):
            loss_sum, grad_sum = carry
            l_k, (gw_k, gx_k) = one_microbatch(xs[0], xs[1])
            with _jax.named_scope("update"):
                return (loss_sum + l_k, _jax.tree.map(_jnp.add, grad_sum, gw_k)), gx_k

        init = (_jnp.zeros((), _jnp.float32), _jax.tree.map(_jnp.zeros_like, weights))
        (loss, grad_w), grad_x = _jax.lax.scan(body, init, (per_example, given["loss_target"]))
    with _jax.named_scope("update"):
        delta_w, new_m, new_v = {}, {}, {}
        for n in TWIN_WEIGHTS:
            delta_w[n], new_m[n], new_v[n] = _adamw(weights[n], grad_w[n], given["m_" + n], given["v_" + n])
    return (loss, grad_x, *[grad_w[n] for n in TWIN_WEIGHTS], *[delta_w[n] for n in TWIN_WEIGHTS],
            *[new_m[n] for n in TWIN_WEIGHTS], *[new_v[n] for n in TWIN_WEIGHTS])
```

```python
import functools
import math

import numpy as np
import jax
import jax.numpy as jnp
from jax import lax
from jax.experimental import pallas as pl
from jax.experimental.pallas import tpu as pltpu

F32 = jnp.float32
BF16 = jnp.bfloat16
MXU_DTYPE = jnp.bfloat16

D = 1024
EPS = 1e-6
TB = 256
CTX = 256
CHUNK = 128
GRID_W = 64
A_W, B_W, C_W = 256, 512, 256
D_IN = 1280
D_FF = 2816
SSM_G, SSM_P, SSM_H = 32, 64, 16
ST = 64
POOL_WINDOWS = (2, 4, 8, 16)
N_DEV = 8
VMEM_LIMIT = 52 * 1024 * 1024
GELU_C = math.sqrt(2.0 / math.pi)

ADAM_LR, ADAM_B1, ADAM_B2, ADAM_EPS, ADAM_WD, ADAM_STEP = 0.001, 0.9, 0.999, 1e-08, 0.01, 10


def _cp(sem=None, vmem=VMEM_LIMIT, **kw):
    return pltpu.CompilerParams(dimension_semantics=sem, vmem_limit_bytes=vmem, **kw)


def _pick(n, cap):
    if n <= cap:
        return n
    best = None
    for t in range(128, cap + 1, 128):
        if n % t == 0:
            best = t
    assert best is not None, (n, cap)
    return best


def _gelu(x):
    return 0.5 * x * (1.0 + jnp.tanh(GELU_C * (x + 0.044715 * x * x * x)))


def _gelu_grad(x):
    t = jnp.tanh(GELU_C * (x + 0.044715 * x * x * x))
    return 0.5 * (1.0 + t) + 0.5 * x * (1.0 - t * t) * GELU_C * (1.0 + 3.0 * 0.044715 * x * x)


def _sigmoid(x):
    return 1.0 / (1.0 + jnp.exp(-x))


def _dot(a, b, dims):
    return lax.dot_general(a, b, (dims, ((), ())), preferred_element_type=F32)


def _nn(a, b):
    return _dot(a, b, ((1,), (0,)))


def _nt(a, b):
    return _dot(a, b, ((1,), (1,)))


def _tn(a, b):
    return _dot(a, b, ((0,), (0,)))


def _mm(pairs, nt, out_dtype, name, tm=512):
    m = pairs[0][0].shape[0]
    n = pairs[0][1].shape[0] if nt else pairs[0][1].shape[1]
    tn = _pick(n, 1024)
    tm = min(tm, m)
    npairs = len(pairs)

    def body(*refs):
        o_ref = refs[-1]
        acc = None
        for i in range(npairs):
            a = refs[2 * i][...].astype(MXU_DTYPE)
            b = refs[2 * i + 1][...].astype(MXU_DTYPE)
            r = _nt(a, b) if nt else _nn(a, b)
            acc = r if acc is None else acc + r
        o_ref[...] = acc.astype(o_ref.dtype)

    in_specs, flat = [], []
    for a, b in pairs:
        k = a.shape[1]
        in_specs.append(pl.BlockSpec((tm, k), lambda i, j: (i, 0)))
        in_specs.append(pl.BlockSpec((tn, k), lambda i, j: (j, 0)) if nt else pl.BlockSpec((k, tn), lambda i, j: (0, j)))
        flat += [a, b]
    return pl.pallas_call(
        body, grid=(m // tm, n // tn), in_specs=in_specs,
        out_specs=pl.BlockSpec((tm, tn), lambda i, j: (i, j)),
        out_shape=jax.ShapeDtypeStruct((m, n), out_dtype),
        compiler_params=_cp(("parallel", "parallel")), name=name)(*flat)


def _mm_tn(a, b, out_dtype, name, tm=512):
    m, k1 = a.shape
    n = b.shape[1]
    t1 = _pick(k1, 1408)
    tn = _pick(n, 1024)
    tm = min(tm, m)
    nsteps = m // tm

    def body(a_ref, b_ref, o_ref, acc_ref):
        t = pl.program_id(2)

        @pl.when(t == 0)
        def _():
            acc_ref[...] = jnp.zeros_like(acc_ref)

        acc_ref[...] += _tn(a_ref[...].astype(MXU_DTYPE), b_ref[...].astype(MXU_DTYPE))

        @pl.when(t == nsteps - 1)
        def _():
            o_ref[...] = acc_ref[...].astype(o_ref.dtype)

    return pl.pallas_call(
        body, grid=(k1 // t1, n // tn, nsteps),
        in_specs=[pl.BlockSpec((tm, t1), lambda i, j, t: (t, i)), pl.BlockSpec((tm, tn), lambda i, j, t: (t, j))],
        out_specs=pl.BlockSpec((t1, tn), lambda i, j, t: (i, j)),
        out_shape=jax.ShapeDtypeStruct((k1, n), out_dtype),
        scratch_shapes=[pltpu.VMEM((t1, tn), F32)],
        compiler_params=_cp(("parallel", "parallel", "arbitrary")), name=name)(a, b)


class _Layout:
    def __init__(self, bl, lat):
        self.bl, self.lat = bl, lat
        self.nlb = lat // TB
        self.nctx = bl
        self.nb = bl + bl * self.nlb
        self.nt = self.nb * TB
        self.ctx_row = bl

    def modrow(self, j):
        return jnp.where(j < self.nctx, self.ctx_row, (j - self.nctx) // self.nlb)

    def first_of_row(self, j):
        return jnp.logical_or(j == 0, jnp.logical_and(j >= self.nctx, (j - self.nctx) % self.nlb == 0))

    def modrows_static(self):
        return np.array([self.ctx_row] * self.nctx + [b for b in range(self.bl) for _ in range(self.nlb)], np.int32)


def _tok_spec():
    return pl.BlockSpec((TB, D), lambda j: (j, 0))


def _vec_spec():
    return pl.BlockSpec((1, D), lambda j: (0, 0))


def _mod_spec(k):
    return pl.BlockSpec((1, 1, D), lambda j: (j * 6 + k, 0, 0))


def _embed(lay, x2d, ctx2d, pe):
    nctx, nlb = lay.nctx, lay.nlb

    def body(x_ref, c_ref, pe_ref, o_ref):
        j = pl.program_id(0)

        @pl.when(j < nctx)
        def _():
            o_ref[...] = c_ref[...]

        @pl.when(j >= nctx)
        def _():
            o_ref[...] = x_ref[...] + pe_ref[...]

    return pl.pallas_call(
        body, grid=(lay.nb,),
        in_specs=[pl.BlockSpec((TB, D), lambda j: (jnp.maximum(j - nctx, 0), 0)),
                  pl.BlockSpec((TB, D), lambda j: (jnp.minimum(j, nctx - 1), 0)),
                  pl.BlockSpec((TB, D), lambda j: (jnp.maximum(j - nctx, 0) % nlb, 0))],
        out_specs=_tok_spec(), out_shape=jax.ShapeDtypeStruct((lay.nt, D), F32),
        compiler_params=_cp(("parallel",)), name="embed")(x2d, ctx2d, pe)


def _normmod_fwd(lay, x, gain, modarr, ksh, ksc, name):
    def body(x_ref, g_ref, sh_ref, sc_ref, o_ref):
        xv = x_ref[...]
        r = lax.rsqrt(jnp.mean(xv * xv, axis=-1, keepdims=True) + EPS)
        o_ref[...] = ((xv * r * g_ref[...]) * (1.0 + sc_ref[0]) + sh_ref[0]).astype(o_ref.dtype)

    return pl.pallas_call(
        body, grid=(lay.nb,), in_specs=[_tok_spec(), _vec_spec(), _mod_spec(ksh), _mod_spec(ksc)],
        out_specs=_tok_spec(), out_shape=jax.ShapeDtypeStruct((lay.nt, D), MXU_DTYPE),
        compiler_params=_cp(("parallel",)), name=name)(x, gain, modarr, modarr)


def _acc_specs(lay):
    row = pl.BlockSpec((1, 1, D), lambda j: (lay.modrow(j), 0, 0))
    return row, jax.ShapeDtypeStruct((8, 1, D), F32)


def _normmod_bwd(lay, x, dh, dx_in, gain, modarr, ksc, name):
    row_spec, row_shape = _acc_specs(lay)

    def body(x_ref, dh_ref, dxi_ref, g_ref, sc_ref, dx_ref, dsh_ref, dsc_ref, dg_ref):
        j = pl.program_id(0)
        xv = x_ref[...]
        dhv = dh_ref[...].astype(F32)
        g = g_ref[...]
        sc1 = 1.0 + sc_ref[0]
        r = lax.rsqrt(jnp.mean(xv * xv, axis=-1, keepdims=True) + EPS)
        xh = xv * r
        dxh = dhv * (g * sc1)
        dx = r * (dxh - xh * jnp.mean(dxh * xh, axis=-1, keepdims=True))
        dx_ref[...] = dxi_ref[...] + dx

        @pl.when(lay.first_of_row(j))
        def _():
            dsh_ref[...] = jnp.zeros_like(dsh_ref)
            dsc_ref[...] = jnp.zeros_like(dsc_ref)

        @pl.when(j == 0)
        def _():
            dg_ref[...] = jnp.zeros_like(dg_ref)

        dsh_ref[0] += jnp.sum(dhv, axis=0, keepdims=True)
        dsc_ref[0] += jnp.sum(dhv * (xh * g), axis=0, keepdims=True)
        dg_ref[...] += jnp.sum(dhv * sc1 * xh, axis=0, keepdims=True)

    return pl.pallas_call(
        body, grid=(lay.nb,),
        in_specs=[_tok_spec(), _tok_spec(), _tok_spec(), _vec_spec(), _mod_spec(ksc)],
        out_specs=[_tok_spec(), row_spec, row_spec, _vec_spec()],
        out_shape=[jax.ShapeDtypeStruct((lay.nt, D), F32), row_shape, row_shape, jax.ShapeDtypeStruct((1, D), F32)],
        compiler_params=_cp(("arbitrary",)), name=name)(x, dh, dx_in, gain, modarr)


def _resnorm_fwd(lay, x, m, gain, modarr, kgate, name):
    def body(x_ref, m_ref, g_ref, gate_ref, o_ref):
        mv = m_ref[...]
        r = lax.rsqrt(jnp.mean(mv * mv, axis=-1, keepdims=True) + EPS)
        o_ref[...] = x_ref[...] + gate_ref[0] * (mv * r * g_ref[...])

    return pl.pallas_call(
        body, grid=(lay.nb,), in_specs=[_tok_spec(), _tok_spec(), _vec_spec(), _mod_spec(kgate)],
        out_specs=_tok_spec(), out_shape=jax.ShapeDtypeStruct((lay.nt, D), F32),
        compiler_params=_cp(("parallel",)), name=name)(x, m, gain, modarr)


def _resnorm_bwd(lay, dxn, m, gain, modarr, kgate, name):
    row_spec, row_shape = _acc_specs(lay)

    def body(d_ref, m_ref, g_ref, gate_ref, dm_ref, dgate_ref, dg_ref):
        j = pl.program_id(0)
        dv = d_ref[...]
        mv = m_ref[...]
        g = g_ref[...]
        r = lax.rsqrt(jnp.mean(mv * mv, axis=-1, keepdims=True) + EPS)
        xh = mv * r
        dy = dv * gate_ref[0]
        dxh = dy * g
        dm_ref[...] = (r * (dxh - xh * jnp.mean(dxh * xh, axis=-1, keepdims=True))).astype(dm_ref.dtype)

        @pl.when(lay.first_of_row(j))
        def _():
            dgate_ref[...] = jnp.zeros_like(dgate_ref)

        @pl.when(j == 0)
        def _():
            dg_ref[...] = jnp.zeros_like(dg_ref)

        dgate_ref[0] += jnp.sum(dv * (xh * g), axis=0, keepdims=True)
        dg_ref[...] += jnp.sum(dy * xh, axis=0, keepdims=True)

    return pl.pallas_call(
        body, grid=(lay.nb,), in_specs=[_tok_spec(), _tok_spec(), _vec_spec(), _mod_spec(kgate)],
        out_specs=[_tok_spec(), row_spec, _vec_spec()],
        out_shape=[jax.ShapeDtypeStruct((lay.nt, D), MXU_DTYPE), row_shape, jax.ShapeDtypeStruct((1, D), F32)],
        compiler_params=_cp(("arbitrary",)), name=name)(dxn, m, gain, modarr)


def _loss_bwd(lay, xf, tgt2d):
    nctx = lay.nctx

    def body(x_ref, t_ref, dx_ref, l_ref):
        j = pl.program_id(0)

        @pl.when(j == 0)
        def _():
            l_ref[...] = jnp.zeros_like(l_ref)

        @pl.when(j < nctx)
        def _():
            dx_ref[...] = jnp.zeros_like(dx_ref)

        @pl.when(j >= nctx)
        def _():
            e = x_ref[...] - t_ref[...]
            dx_ref[...] = e * (1.0 / D)
            l_ref[...] += jnp.sum(e * e) * (0.5 / D)

    return pl.pallas_call(
        body, grid=(lay.nb,),
        in_specs=[_tok_spec(), pl.BlockSpec((TB, D), lambda j: (jnp.maximum(j - nctx, 0), 0))],
        out_specs=[_tok_spec(), pl.BlockSpec((8, 128), lambda j: (0, 0))],
        out_shape=[jax.ShapeDtypeStruct((lay.nt, D), F32), jax.ShapeDtypeStruct((8, 128), F32)],
        compiler_params=_cp(("arbitrary",)), name="loss")(xf, tgt2d)


def _ffn_up(h, wgt, wut, name):
    m = h.shape[0]
    tm, tn = min(512, m), 256

    def body(h_ref, wg_ref, wu_ref, g_ref, u_ref, a_ref):
        hv = h_ref[...]
        g = _nt(hv, wg_ref[...])
        u = _nt(hv, wu_ref[...])
        g_ref[...] = g.astype(g_ref.dtype)
        u_ref[...] = u.astype(u_ref.dtype)
        a_ref[...] = (g * _sigmoid(g) * u).astype(a_ref.dtype)

    osp = pl.BlockSpec((tm, tn), lambda i, j: (i, j))
    osh = jax.ShapeDtypeStruct((m, D_FF), MXU_DTYPE)
    return pl.pallas_call(
        body, grid=(m // tm, D_FF // tn),
        in_specs=[pl.BlockSpec((tm, D), lambda i, j: (i, 0)), pl.BlockSpec((tn, D), lambda i, j: (j, 0)),
                  pl.BlockSpec((tn, D), lambda i, j: (j, 0))],
        out_specs=[osp, osp, osp], out_shape=[osh, osh, osh],
        compiler_params=_cp(("parallel", "parallel")), name=name)(h, wgt, wut)


def _ffn_down_bwd(df, wd, g, u, name):
    m = df.shape[0]
    tm, tn = min(512, m), 256

    def body(df_ref, wd_ref, g_ref, u_ref, dg_ref, du_ref):
        da = _nt(df_ref[...], wd_ref[...])
        gv = g_ref[...].astype(F32)
        uv = u_ref[...].astype(F32)
        s = _sigmoid(gv)
        dg_ref[...] = (da * uv * (s * (1.0 + gv * (1.0 - s)))).astype(dg_ref.dtype)
        du_ref[...] = (da * gv * s).astype(du_ref.dtype)

    osp = pl.BlockSpec((tm, tn), lambda i, j: (i, j))
    osh = jax.ShapeDtypeStruct((m, D_FF), MXU_DTYPE)
    return pl.pallas_call(
        body, grid=(m // tm, D_FF // tn),
        in_specs=[pl.BlockSpec((tm, D), lambda i, j: (i, 0)), pl.BlockSpec((tn, D), lambda i, j: (j, 0)), osp, osp],
        out_specs=[osp, osp], out_shape=[osh, osh],
        compiler_params=_cp(("parallel", "parallel")), name=name)(df, wd, g, u)


def _head_masks(shape):
    lane = lax.broadcasted_iota(jnp.int32, shape, 1)
    return [jnp.logical_and(lane >= 64 * h, lane < 64 * h + 64) for h in range(4)]


def _head_mean(x, masks):
    out = jnp.zeros_like(x)
    for mk in masks:
        s = jnp.sum(jnp.where(mk, x, 0.0), axis=-1, keepdims=True) * (1.0 / 64.0)
        out = jnp.where(mk, s, out)
    return out


def _gate_common(z, masks):
    zg = _gelu(z)
    u = zg[:, :A_W]
    v = zg[:, A_W:]
    mu = _head_mean(v, masks)
    vc = v - mu
    rstd = lax.rsqrt(_head_mean(vc * vc, masks) + EPS)
    return u, vc * rstd, rstd


def _gate_s(vn, ws_ref, bias, masks):
    parts = []
    for c in range(TB // CHUNK):
        vc = vn[c * CHUNK:(c + 1) * CHUNK]
        s = bias
        for h in range(4):
            s = s + _nn(ws_ref[h], jnp.where(masks[h][:CHUNK], vc, 0.0).astype(MXU_DTYPE))
        parts.append(s)
    return jnp.concatenate(parts, axis=0)


def _gate_fwd(lay, z, ws, bias, name):
    def body(z_ref, ws_ref, b_ref, o_ref):
        masks = _head_masks((TB, A_W))
        u, vn, _ = _gate_common(z_ref[...], masks)
        o_ref[...] = (u * _gate_s(vn, ws_ref, b_ref[...], masks)).astype(o_ref.dtype)

    return pl.pallas_call(
        body, grid=(lay.nb,),
        in_specs=[pl.BlockSpec((TB, 2 * A_W), lambda j: (j, 0)), pl.BlockSpec((4, CHUNK, CHUNK), lambda j: (0, 0, 0)),
                  pl.BlockSpec((CHUNK, A_W), lambda j: (0, 0))],
        out_specs=pl.BlockSpec((TB, A_W), lambda j: (j, 0)),
        out_shape=jax.ShapeDtypeStruct((lay.nt, A_W), MXU_DTYPE),
        compiler_params=_cp(("parallel",)), name=name)(z, ws, bias)


def _gate_bwd(lay, z, da, ws, wst, bias, name):
    def body(z_ref, da_ref, ws_ref, wst_ref, b_ref, dz_ref, dws_ref, db_ref):
        j = pl.program_id(0)

        @pl.when(j == 0)
        def _():
            dws_ref[...] = jnp.zeros_like(dws_ref)
            db_ref[...] = jnp.zeros_like(db_ref)

        masks = _head_masks((TB, A_W))
        zv = z_ref[...]
        u, vn, rstd = _gate_common(zv, masks)
        s = _gate_s(vn, ws_ref, b_ref[...], masks)
        dav = da_ref[...].astype(F32)
        du = dav * s
        ds = dav * u
        dvn_parts = []
        for c in range(TB // CHUNK):
            sl = slice(c * CHUNK, (c + 1) * CHUNK)
            ds_c = ds[sl]
            vn_c = vn[sl].astype(MXU_DTYPE)
            db_ref[...] += ds_c
            ds_b = ds_c.astype(MXU_DTYPE)
            dvn_c = jnp.zeros((CHUNK, A_W), F32)
            for h in range(4):
                mk = masks[h][:CHUNK]
                dws_ref[h] += _nt(jnp.where(mk, ds_c, 0.0).astype(MXU_DTYPE), vn_c)
                dvn_c = dvn_c + jnp.where(mk, _nn(wst_ref[h], ds_b), 0.0)
            dvn_parts.append(dvn_c)
        dvn = jnp.concatenate(dvn_parts, axis=0)
        dv = rstd * (dvn - _head_mean(dvn, masks) - vn * _head_mean(dvn * vn, masks))
        gg = _gelu_grad(zv)
        dz_ref[:, :A_W] = (du * gg[:, :A_W]).astype(dz_ref.dtype)
        dz_ref[:, A_W:] = (dv * gg[:, A_W:]).astype(dz_ref.dtype)

    return pl.pallas_call(
        body, grid=(lay.nb,),
        in_specs=[pl.BlockSpec((TB, 2 * A_W), lambda j: (j, 0)), pl.BlockSpec((TB, A_W), lambda j: (j, 0)),
                  pl.BlockSpec((4, CHUNK, CHUNK), lambda j: (0, 0, 0)), pl.BlockSpec((4, CHUNK, CHUNK), lambda j: (0, 0, 0)),
                  pl.BlockSpec((CHUNK, A_W), lambda j: (0, 0))],
        out_specs=[pl.BlockSpec((TB, 2 * A_W), lambda j: (j, 0)), pl.BlockSpec((4, CHUNK, CHUNK), lambda j: (0, 0, 0)),
                   pl.BlockSpec((CHUNK, A_W), lambda j: (0, 0))],
        out_shape=[jax.ShapeDtypeStruct((lay.nt, 2 * A_W), MXU_DTYPE), jax.ShapeDtypeStruct((4, CHUNK, CHUNK), F32),
                   jax.ShapeDtypeStruct((CHUNK, A_W), F32)],
        compiler_params=_cp(("arbitrary",)), name=name)(z, da, ws, wst, bias)


def _band_constants():
    bands = np.zeros((2, 4, TB, TB), np.float32)
    inv = np.zeros((2, 4, TB, 1), np.float32)
    for kind, n in ((0, GRID_W), (1, TB)):
        for i, w in enumerate(POOL_WINDOWS):
            for t in range(TB):
                base, tt = (t // n) * n, t % n
                lo = min(max(tt - w // 2, 0), n)
                hi = min(max(tt - w // 2 + w, 0), n)
                bands[kind, i, t, base + lo:base + hi] = 1.0
                inv[kind, i, t, 0] = 1.0 / (hi - lo)
    return bands, inv


def _split3(x):
    a = x.astype(MXU_DTYPE)
    r1 = x - a.astype(F32)
    b = r1.astype(MXU_DTYPE)
    c = (r1 - b.astype(F32)).astype(MXU_DTYPE)
    return a, b, c


def _window_apply(band_ref, inv_ref, x, masks, transpose):
    out = jnp.zeros_like(x)
    for i in range(4):
        xi = x * inv_ref[0, i] if transpose else x
        acc = None
        for part in _split3(xi):
            r = _tn(band_ref[0, i], part) if transpose else _nn(band_ref[0, i], part)
            acc = r if acc is None else acc + r
        if not transpose:
            acc = acc * inv_ref[0, i]
        out = jnp.where(masks[i], acc, out)
    return out


def _pool_specs(lay):
    kind = lambda j: jnp.where(j < lay.nctx, 1, 0)
    return [pl.BlockSpec((1, 4, TB, TB), lambda j: (kind(j), 0, 0, 0)), pl.BlockSpec((1, 4, TB, 1), lambda j: (kind(j), 0, 0, 0))]


def _pool_fwd(lay, z, bands, inv, pw, scale, name):
    def body(p_ref, band_ref, inv_ref, pw_ref, sc_ref, o_ref):
        masks = _head_masks((TB, C_W))
        p = p_ref[...]
        diff = _window_apply(band_ref, inv_ref, p, masks, False) - p
        o_ref[...] = (_nn(diff.astype(MXU_DTYPE), pw_ref[...]) * sc_ref[...]).astype(o_ref.dtype)

    return pl.pallas_call(
        body, grid=(lay.nb,),
        in_specs=[pl.BlockSpec((TB, C_W), lambda j: (j, 4))] + _pool_specs(lay)
        + [pl.BlockSpec((C_W, C_W), lambda j: (0, 0)), pl.BlockSpec((1, C_W), lambda j: (0, 0))],
        out_specs=pl.BlockSpec((TB, C_W), lambda j: (j, 0)),
        out_shape=jax.ShapeDtypeStruct((lay.nt, C_W), MXU_DTYPE),
        compiler_params=_cp(("parallel",)), name=name)(z, bands, inv, pw, scale)


def _pool_bwd(lay, z, dc, bands, inv, pw, scale, name):
    def body(p_ref, dc_ref, band_ref, inv_ref, pw_ref, sc_ref, dp_ref, dpw_ref, dsc_ref):
        j = pl.program_id(0)

        @pl.when(j == 0)
        def _():
            dpw_ref[...] = jnp.zeros_like(dpw_ref)
            dsc_ref[...] = jnp.zeros_like(dsc_ref)

        masks = _head_masks((TB, C_W))
        p = p_ref[...]
        dcv = dc_ref[...].astype(F32)
        diff = _window_apply(band_ref, inv_ref, p, masks, False) - p
        diff_b = diff.astype(MXU_DTYPE)
        pre = _nn(diff_b, pw_ref[...])
        dsc_ref[...] += jnp.sum(dcv * pre, axis=0, keepdims=True)
        dpre = dcv * sc_ref[...]
        dpre_b = dpre.astype(MXU_DTYPE)
        dpw_ref[...] += _tn(diff_b, dpre_b)
        ddiff = _nt(dpre_b, pw_ref[...])
        dp_ref[...] = (_window_apply(band_ref, inv_ref, ddiff, masks, True) - ddiff).astype(dp_ref.dtype)

    return pl.pallas_call(
        body, grid=(lay.nb,),
        in_specs=[pl.BlockSpec((TB, C_W), lambda j: (j, 4)), pl.BlockSpec((TB, C_W), lambda j: (j, 0))] + _pool_specs(lay)
        + [pl.BlockSpec((C_W, C_W), lambda j: (0, 0)), pl.BlockSpec((1, C_W), lambda j: (0, 0))],
        out_specs=[pl.BlockSpec((TB, C_W), lambda j: (j, 0)), pl.BlockSpec((C_W, C_W), lambda j: (0, 0)),
                   pl.BlockSpec((1, C_W), lambda j: (0, 0))],
        out_shape=[jax.ShapeDtypeStruct((lay.nt, C_W), MXU_DTYPE), jax.ShapeDtypeStruct((C_W, C_W), F32),
                   jax.ShapeDtypeStruct((1, C_W), F32)],
        compiler_params=_cp(("arbitrary",)), name=name)(z, dc, bands, inv, pw, scale)


def _disc_math(lr, li, ldt, br, bi):
    dt = jnp.exp(ldt)
    e = jnp.exp(lr * dt)
    ar = e * jnp.cos(li * dt)
    ai = e * jnp.sin(li * dt)
    nr, ni = ar - 1.0, ai
    den = lr * lr + li * li
    qr = (nr * lr + ni * li) / den
    qi = (ni * lr - nr * li) / den
    return ar, ai, qr * br - qi * bi, qr * bi + qi * br


def _disc_fwd(lrx, lix, ldtx, brt, bit, name):
    def body(lr_ref, li_ref, ldt_ref, br_ref, bi_ref, ar_ref, ai_ref, obr_ref, obi_ref):
        ar, ai, obr, obi = _disc_math(lr_ref[...], li_ref[...], ldt_ref[...], br_ref[...], bi_ref[...])
        ar_ref[...] = ar
        ai_ref[...] = ai
        obr_ref[...] = obr
        obi_ref[...] = obi

    sh = jax.ShapeDtypeStruct(lrx.shape, F32)
    return pl.pallas_call(body, out_shape=[sh, sh, sh, sh], name=name)(lrx, lix, ldtx, brt, bit)


def _disc_bwd(lrx, lix, ldtx, brt, bit, dar, dai, dbr, dbi, name):
    nrow = lrx.shape[0] // SSM_H

    def body(lr_ref, li_ref, ldt_ref, br_ref, bi_ref, dar_ref, dai_ref, dbr_ref, dbi_ref,
             glr_ref, gli_ref, gdt_ref, gbr_ref, gbi_ref):
        _, vjp = jax.vjp(_disc_math, lr_ref[...], li_ref[...], ldt_ref[...], br_ref[...], bi_ref[...])
        glr, gli, gdt, gbr, gbi = vjp((dar_ref[...], dai_ref[...], dbr_ref[...], dbi_ref[...]))
        glr_ref[...] = jnp.sum(glr.reshape(nrow, SSM_H, SSM_P), axis=1)
        gli_ref[...] = jnp.sum(gli.reshape(nrow, SSM_H, SSM_P), axis=1)
        gdt_ref[...] = jnp.sum(jnp.sum(gdt.reshape(nrow, SSM_H, SSM_P), axis=1), axis=-1, keepdims=True)
        gbr_ref[...] = gbr
        gbi_ref[...] = gbi

    small = jax.ShapeDtypeStruct((nrow, SSM_P), F32)
    big = jax.ShapeDtypeStruct(lrx.shape, F32)
    return pl.pallas_call(body, out_shape=[small, small, jax.ShapeDtypeStruct((nrow, 1), F32), big, big],
                          name=name)(lrx, lix, ldtx, brt, bit, dar, dai, dbr, dbi)


HS = 1024
LC = 512


def _d0_rows(n):
    row = lax.broadcasted_iota(jnp.int32, (n, 1), 0)
    return jnp.bitwise_and(row, 4) == 0


def _ssm_fwd(u2, bh, ch, ar8, ai8, name):
    rows = u2.shape[0]
    rc = ST * 8
    nch = rows // rc

    def body(u_ref, bh_ref, ch_ref, ar_ref, ai_ref, y_ref, hst_ref, hs, hc):
        k = pl.program_id(1)

        @pl.when(k == 0)
        def _():
            hc[...] = jnp.zeros_like(hc)

        hst_ref[0] = hc[...]
        d0 = _d0_rows(rc)
        uv = u_ref[...]
        zero = jnp.zeros_like(uv)
        hs[...] = _nn(jnp.where(d0, uv, zero), bh_ref[0, 0]) + _nn(jnp.where(d0, zero, uv), bh_ref[1, 0])
        for q in range(HS // LC):
            cr, ci = q * LC, HS + q * LC
            ar = ar_ref[:, cr:cr + LC]
            ai = ai_ref[:, cr:cr + LC]

            def step(s, carry, cr=cr, ci=ci, ar=ar, ai=ai):
                hr, hi = carry
                base = pl.multiple_of(s * 8, 8)
                nr = ar * hr - ai * hi + hs[pl.ds(base, 8), cr:cr + LC]
                ni = ar * hi + ai * hr + hs[pl.ds(base, 8), ci:ci + LC]
                hs[pl.ds(base, 8), cr:cr + LC] = nr
                hs[pl.ds(base, 8), ci:ci + LC] = ni
                return nr, ni

            hr, hi = lax.fori_loop(0, ST, step, (hc[:, cr:cr + LC], hc[:, ci:ci + LC]), unroll=4)
            hc[:, cr:cr + LC] = hr
            hc[:, ci:ci + LC] = hi
        hb = hs[...].astype(MXU_DTYPE)
        y_ref[...] = jnp.where(d0, _nn(hb, ch_ref[0, 0]), _nn(hb, ch_ref[1, 0])).astype(y_ref.dtype)

    return pl.pallas_call(
        body, grid=(2, nch),
        in_specs=[pl.BlockSpec((rc, 256), lambda f, k: (k, f)),
                  pl.BlockSpec((2, 1, 256, 2 * HS), lambda f, k: (0, f, 0, 0)),
                  pl.BlockSpec((2, 1, 2 * HS, 256), lambda f, k: (0, f, 0, 0)),
                  pl.BlockSpec((8, HS), lambda f, k: (0, f)), pl.BlockSpec((8, HS), lambda f, k: (0, f))],
        out_specs=[pl.BlockSpec((rc, 256), lambda f, k: (k, f)), pl.BlockSpec((1, 8, 2 * HS), lambda f, k: (k, 0, f))],
        out_shape=[jax.ShapeDtypeStruct((rows, B_W), F32), jax.ShapeDtypeStruct((nch, 8, 4 * HS), F32)],
        scratch_shapes=[pltpu.VMEM((rc, 2 * HS), F32), pltpu.VMEM((8, 2 * HS), F32)],
        compiler_params=_cp(("parallel", "arbitrary")), name=name)(u2, bh, ch, ar8, ai8)


def _ssm_bwd(u2, dy2, hst, bh, ch, ar8, ai8, name):
    rows = u2.shape[0]
    rc = ST * 8
    nch = rows // rc

    def body(u_ref, dy_ref, hst_ref, bh_ref, ch_ref, ar_ref, ai_ref,
             du_ref, dbh_ref, dch_ref, dar_ref, dai_ref, hs, es, ec, accr, acci):
        k = pl.program_id(1)

        @pl.when(k == 0)
        def _():
            ec[...] = jnp.zeros_like(ec)
            accr[...] = jnp.zeros_like(accr)
            acci[...] = jnp.zeros_like(acci)
            dbh_ref[...] = jnp.zeros_like(dbh_ref)
            dch_ref[...] = jnp.zeros_like(dch_ref)

        d0 = _d0_rows(rc)
        uv = u_ref[...]
        zero = jnp.zeros_like(uv)
        u0, u1 = jnp.where(d0, uv, zero), jnp.where(d0, zero, uv)
        dyv = dy_ref[...].astype(MXU_DTYPE)
        dy0, dy1 = jnp.where(d0, dyv, jnp.zeros_like(dyv)), jnp.where(d0, jnp.zeros_like(dyv), dyv)

        hs[0:8, :] = hst_ref[0]
        hs[8:, :] = _nn(u0, bh_ref[0, 0]) + _nn(u1, bh_ref[1, 0])
        for q in range(HS // LC):
            cr, ci = q * LC, HS + q * LC
            ar = ar_ref[:, cr:cr + LC]
            ai = ai_ref[:, cr:cr + LC]

            def step(s, carry, cr=cr, ci=ci, ar=ar, ai=ai):
                hr, hi = carry
                base = pl.multiple_of(s * 8 + 8, 8)
                nr = ar * hr - ai * hi + hs[pl.ds(base, 8), cr:cr + LC]
                ni = ar * hi + ai * hr + hs[pl.ds(base, 8), ci:ci + LC]
                hs[pl.ds(base, 8), cr:cr + LC] = nr
                hs[pl.ds(base, 8), ci:ci + LC] = ni
                return nr, ni

            lax.fori_loop(0, ST, step, (hs[0:8, cr:cr + LC], hs[0:8, ci:ci + LC]), unroll=4)

        hb = hs[8:, :].astype(MXU_DTYPE)
        dch_ref[0, 0] += _tn(hb, dy0)
        dch_ref[1, 0] += _tn(hb, dy1)
        es[...] = _nt(dy0, ch_ref[0, 0]) + _nt(dy1, ch_ref[1, 0])

        for q in range(HS // LC):
            cr, ci = q * LC, HS + q * LC
            ar = ar_ref[:, cr:cr + LC]
            ai = ai_ref[:, cr:cr + LC]

            def bstep(i, carry, cr=cr, ci=ci, ar=ar, ai=ai):
                er, ei, sr, si = carry
                base = pl.multiple_of((ST - 1 - i) * 8, 8)
                ner = es[pl.ds(base, 8), cr:cr + LC] + ar * er + ai * ei
                nei = es[pl.ds(base, 8), ci:ci + LC] - ai * er + ar * ei
                es[pl.ds(base, 8), cr:cr + LC] = ner
                es[pl.ds(base, 8), ci:ci + LC] = nei
                hpr = hs[pl.ds(base, 8), cr:cr + LC]
                hpi = hs[pl.ds(base, 8), ci:ci + LC]
                return ner, nei, sr + ner * hpr + nei * hpi, si - ner * hpi + nei * hpr

            er, ei, sr, si = lax.fori_loop(
                0, ST, bstep, (ec[:, cr:cr + LC], ec[:, ci:ci + LC], accr[:, cr:cr + LC], acci[:, cr:cr + LC]), unroll=4)
            ec[:, cr:cr + LC] = er
            ec[:, ci:ci + LC] = ei
            accr[:, cr:cr + LC] = sr
            acci[:, cr:cr + LC] = si

        eb = es[...].astype(MXU_DTYPE)
        du_ref[...] = jnp.where(d0, _nt(eb, bh_ref[0, 0]), _nt(eb, bh_ref[1, 0])).astype(du_ref.dtype)
        dbh_ref[0, 0] += _tn(u0, eb)
        dbh_ref[1, 0] += _tn(u1, eb)

        @pl.when(k == nch - 1)
        def _():
            for d in range(2):
                dar_ref[d:d + 1, :] = jnp.sum(accr[4 * d:4 * d + 4, :], axis=0, keepdims=True)
                dai_ref[d:d + 1, :] = jnp.sum(acci[4 * d:4 * d + 4, :], axis=0, keepdims=True)

    rev = lambda k: nch - 1 - k
    return pl.pallas_call(
        body, grid=(2, nch),
        in_specs=[pl.BlockSpec((rc, 256), lambda f, k: (rev(k), f)), pl.BlockSpec((rc, 256), lambda f, k: (rev(k), f)),
                  pl.BlockSpec((1, 8, 2 * HS), lambda f, k: (rev(k), 0, f)),
                  pl.BlockSpec((2, 1, 256, 2 * HS), lambda f, k: (0, f, 0, 0)),
                  pl.BlockSpec((2, 1, 2 * HS, 256), lambda f, k: (0, f, 0, 0)),
                  pl.BlockSpec((8, HS), lambda f, k: (0, f)), pl.BlockSpec((8, HS), lambda f, k: (0, f))],
        out_specs=[pl.BlockSpec((rc, 256), lambda f, k: (rev(k), f)),
                   pl.BlockSpec((2, 1, 256, 2 * HS), lambda f, k: (0, f, 0, 0)),
                   pl.BlockSpec((2, 1, 2 * HS, 256), lambda f, k: (0, f, 0, 0)),
                   pl.BlockSpec((2, HS), lambda f, k: (0, f)), pl.BlockSpec((2, HS), lambda f, k: (0, f))],
        out_shape=[jax.ShapeDtypeStruct((rows, B_W), MXU_DTYPE), jax.ShapeDtypeStruct((2, 2, 256, 2 * HS), F32),
                   jax.ShapeDtypeStruct((2, 2, 2 * HS, 256), F32), jax.ShapeDtypeStruct((2, 2 * HS), F32),
                   jax.ShapeDtypeStruct((2, 2 * HS), F32)],
        scratch_shapes=[pltpu.VMEM((rc + 8, 2 * HS), F32), pltpu.VMEM((rc, 2 * HS), F32), pltpu.VMEM((8, 2 * HS), F32),
                        pltpu.VMEM((8, HS), F32), pltpu.VMEM((8, HS), F32)],
        compiler_params=_cp(("parallel", "arbitrary")), name=name)(u2, dy2, hst, bh, ch, ar8, ai8)


def _glu_fwd(lay, z, yf, yr, dvec, wglu, bglu, name):
    def body(u_ref, yf_ref, yr_ref, d_ref, w_ref, b_ref, o_ref, y_ref):
        y = yf_ref[...] + yr_ref[...] + d_ref[...] * u_ref[...]
        y_ref[...] = y
        g = _gelu(y)
        pre = _nn(g.astype(MXU_DTYPE), w_ref[...]) + b_ref[...]
        o_ref[...] = (g * _sigmoid(pre)).astype(o_ref.dtype)

    tok = pl.BlockSpec((TB, B_W), lambda j: (j, 0))
    vec = pl.BlockSpec((1, B_W), lambda j: (0, 0))
    return pl.pallas_call(
        body, grid=(lay.nb,),
        in_specs=[pl.BlockSpec((TB, B_W), lambda j: (j, 1)), tok, tok, vec, pl.BlockSpec((B_W, B_W), lambda j: (0, 0)), vec],
        out_specs=[tok, tok],
        out_shape=[jax.ShapeDtypeStruct((lay.nt, B_W), MXU_DTYPE), jax.ShapeDtypeStruct((lay.nt, B_W), F32)],
        compiler_params=_cp(("parallel",)), name=name)(z, yf, yr, dvec, wglu, bglu)


def _glu_bwd(lay, z, y, ds, dvec, wglu, bglu, name):
    def body(u_ref, y_ref, ds_ref, d_ref, w_ref, b_ref, dy_ref, dud_ref, dw_ref, db_ref, dd_ref):
        j = pl.program_id(0)

        @pl.when(j == 0)
        def _():
            dw_ref[...] = jnp.zeros_like(dw_ref)
            db_ref[...] = jnp.zeros_like(db_ref)
            dd_ref[...] = jnp.zeros_like(dd_ref)

        yv = y_ref[...]
        g = _gelu(yv)
        gb = g.astype(MXU_DTYPE)
        sg = _sigmoid(_nn(gb, w_ref[...]) + b_ref[...])
        dsv = ds_ref[...].astype(F32)
        dpre = dsv * g * sg * (1.0 - sg)
        dpre_b = dpre.astype(MXU_DTYPE)
        dg = dsv * sg + _nt(dpre_b, w_ref[...])
        dw_ref[...] += _tn(gb, dpre_b)
        db_ref[...] += jnp.sum(dpre, axis=0, keepdims=True)
        dy = dg * _gelu_grad(yv)
        dy_ref[...] = dy.astype(dy_ref.dtype)
        dd_ref[...] += jnp.sum(dy * u_ref[...], axis=0, keepdims=True)
        dud_ref[...] = (dy * d_ref[...]).astype(dud_ref.dtype)

    tok = pl.BlockSpec((TB, B_W), lambda j: (j, 0))
    vec = pl.BlockSpec((1, B_W), lambda j: (0, 0))
    mat = pl.BlockSpec((B_W, B_W), lambda j: (0, 0))
    vsh = jax.ShapeDtypeStruct((1, B_W), F32)
    return pl.pallas_call(
        body, grid=(lay.nb,),
        in_specs=[pl.BlockSpec((TB, B_W), lambda j: (j, 1)), tok, tok, vec, mat, vec],
        out_specs=[tok, tok, mat, vec, vec],
        out_shape=[jax.ShapeDtypeStruct((lay.nt, B_W), MXU_DTYPE), jax.ShapeDtypeStruct((lay.nt, B_W), F32),
                   jax.ShapeDtypeStruct((B_W, B_W), F32), vsh, vsh],
        compiler_params=_cp(("arbitrary",)), name=name)(z, y, ds, dvec, wglu, bglu)


def _to_scan_rows(lay, a):
    w = a.shape[1]
    nc = lay.nctx * TB
    actx = a[:nc].reshape(lay.bl, CTX, w)
    alat = a[nc:].reshape(lay.bl, lay.lat, w)
    fwd = jnp.concatenate([actx, alat], axis=1)
    rev = jnp.concatenate([jnp.flip(actx, 1), jnp.flip(alat, 1)], axis=1)
    s = jnp.stack([fwd, rev], axis=0)
    return jnp.transpose(s, (2, 0, 1, 3)).reshape(-1, w)


def _from_scan_rows(lay, a2):
    w = a2.shape[1]
    s = jnp.transpose(a2.reshape(CTX + lay.lat, 2, lay.bl, w), (1, 2, 0, 3))

    def tok(p, flip):
        c, l = p[:, :CTX], p[:, CTX:]
        if flip:
            c, l = jnp.flip(c, 1), jnp.flip(l, 1)
        return jnp.concatenate([c.reshape(-1, w), l.reshape(-1, w)], axis=0)

    return tok(s[0], False), tok(s[1], True)


def _expand_rows(a):
    return jnp.broadcast_to(a[:, :, None, :], (2, SSM_G, SSM_H, SSM_P)).reshape(-1, SSM_P)


def _ssm_params(lam_re, lam_im, log_dt, b_re, b_im, c_re, c_im, name):
    lrx, lix = _expand_rows(lam_re), _expand_rows(lam_im)
    ldtx = _expand_rows(jnp.broadcast_to(log_dt[:, :, None], (2, SSM_G, SSM_P)))
    brt = jnp.transpose(b_re, (0, 1, 3, 2)).reshape(-1, SSM_P)
    bit = jnp.transpose(b_im, (0, 1, 3, 2)).reshape(-1, SSM_P)
    arx, aix, bbr, bbi = _disc_fwd(lrx, lix, ldtx, brt, bit, name)
    ar = arx.reshape(2, SSM_G, SSM_H, SSM_P)[:, :, 0].reshape(2, SSM_G * SSM_P)
    ai = aix.reshape(2, SSM_G, SSM_H, SSM_P)[:, :, 0].reshape(2, SSM_G * SSM_P)
    eye = jnp.eye(16, dtype=F32)

    def bmat(bt):
        t = bt.reshape(2, 2, 16, SSM_H, SSM_P)
        return jnp.einsum('dfghp,gk->dfghkp', t, eye).reshape(2, 2, 256, HS)

    bh = jnp.concatenate([bmat(bbr), bmat(bbi)], axis=-1).astype(MXU_DTYPE)

    def cmat(c):
        t = c.reshape(2, 2, 16, SSM_H, SSM_P)
        return jnp.einsum('dfghp,gk->dfgpkh', t, eye).reshape(2, 2, HS, 256)

    ch = jnp.concatenate([cmat(c_re), -cmat(c_im)], axis=2).astype(MXU_DTYPE)

    def rows8(a):
        return jnp.repeat(a, 4, axis=0)

    return dict(lrx=lrx, lix=lix, ldtx=ldtx, brt=brt, bit=bit, bh=bh, ch=ch, ar8=rows8(ar), ai8=rows8(ai))


def _ssm_param_grads(sp, dbh, dch, dar, dai, name):
    def bdiag(m):
        t = m.reshape(2, 2, 16, SSM_H, 16, SSM_P)
        return jnp.einsum('dfghgp->dfghp', t).reshape(-1, SSM_P)

    dbr, dbi = bdiag(dbh[..., :HS]), bdiag(dbh[..., HS:])

    def cdiag(m):
        t = m.reshape(2, 2, 16, SSM_P, 16, SSM_H)
        return jnp.einsum('dfgpgh->dfghp', t).reshape(2, SSM_G, SSM_H, SSM_P)

    dc_re, dc_im = cdiag(dch[:, :, :HS]), -cdiag(dch[:, :, HS:])

    def hrow(a):
        t = a.reshape(2, SSM_G, 1, SSM_P)
        return jnp.concatenate([t, jnp.zeros((2, SSM_G, SSM_H - 1, SSM_P), F32)], axis=2).reshape(-1, SSM_P)

    glr, gli, gdt, gbr, gbi = _disc_bwd(sp["lrx"], sp["lix"], sp["ldtx"], sp["brt"], sp["bit"],
                                        hrow(dar), hrow(dai), dbr, dbi, name)
    to_b = lambda g: jnp.transpose(g.reshape(2, SSM_G, SSM_H, SSM_P), (0, 1, 3, 2))
    return dict(ssm_lam_re=glr.reshape(2, SSM_G, SSM_P), ssm_lam_im=gli.reshape(2, SSM_G, SSM_P),
                ssm_log_dt=gdt.reshape(2, SSM_G), ssm_b_re=to_b(gbr), ssm_b_im=to_b(gbi),
                ssm_c_re=dc_re, ssm_c_im=dc_im)


def _layer_consts(p):
    c = {}
    c["ws"] = p["sgu_w"].astype(MXU_DTYPE)
    c["wst"] = jnp.transpose(p["sgu_w"], (0, 2, 1)).astype(MXU_DTYPE)
    c["gbias"] = jnp.repeat(p["sgu_b"].T, 64, axis=1)
    pw = jnp.zeros((C_W, C_W), F32)
    for i in range(4):
        pw = pw.at[64 * i:64 * i + 64, 64 * i:64 * i + 64].set(p["pool_w"][i])
    c["pw"] = pw.astype(MXU_DTYPE)
    c["pscale"] = p["pool_scale"].reshape(1, C_W)
    c["dvec"] = p["ssm_d"].reshape(1, B_W)
    c["bglu"] = p["glu_b"].reshape(1, B_W)
    return c


def _layer_fwd(lay, i, x, modarr, p, w, cst, sp, bands, inv):
    n = f"l{i}_"
    res = {"x0": x}
    h = _normmod_fwd(lay, x, p["norm_mix_pre"].reshape(1, D), modarr, 0, 1, n + "nm1")
    z = _mm([(h, w["win_t"])], True, F32, n + "win")
    a = _gate_fwd(lay, z, cst["ws"], cst["gbias"], n + "gate")
    u2 = _to_scan_rows(lay, z[:, 2 * A_W:2 * A_W + B_W].astype(MXU_DTYPE))
    y2, hst = _ssm_fwd(u2, sp["bh"], sp["ch"], sp["ar8"], sp["ai8"], n + "ssm")
    yf, yr = _from_scan_rows(lay, y2)
    s, y = _glu_fwd(lay, z, yf, yr, cst["dvec"], w["wglu"], cst["bglu"], n + "glu")
    c = _pool_fwd(lay, z, bands, inv, cst["pw"], cst["pscale"], n + "pool")
    mcat = jnp.concatenate([a, s, c], axis=1)
    m = _mm([(mcat, w["wout"])], False, F32, n + "wout")
    x1 = _resnorm_fwd(lay, x, m, p["norm_mix_post"].reshape(1, D), modarr, 2, n + "rn1")
    h2 = _normmod_fwd(lay, x1, p["norm_ffn_pre"].reshape(1, D), modarr, 3, 4, n + "nm2")
    g, u, act = _ffn_up(h2, w["wg_t"], w["wu_t"], n + "ffn_up")
    f = _mm([(act, w["wd"])], False, F32, n + "ffn_down")
    x2 = _resnorm_fwd(lay, x1, f, p["norm_ffn_post"].reshape(1, D), modarr, 5, n + "rn2")
    res.update(h=h, z=z, u2=u2, hst=hst, y=y, mcat=mcat, m=m, x1=x1, h2=h2, g=g, u=u, act=act, f=f)
    return x2, res


def _layer_bwd(lay, i, dx2, modarr, p, w, cst, sp, bands, inv, res):
    n = f"l{i}b_"
    big, small = {}, {}
    df, dg2, gpost2 = _resnorm_bwd(lay, dx2, res["f"], p["norm_ffn_post"].reshape(1, D), modarr, 5, n + "rn2")
    big["wd"] = _mm_tn(res["act"], df, MXU_DTYPE, n + "dwd")
    dg, du = _ffn_down_bwd(df, w["wd"], res["g"], res["u"], n + "ffn_down")
    dh2 = _mm([(dg, w["wg_t"]), (du, w["wu_t"])], False, F32, n + "dh2")
    big["wg_t"] = _mm_tn(dg, res["h2"], MXU_DTYPE, n + "dwg")
    big["wu_t"] = _mm_tn(du, res["h2"], MXU_DTYPE, n + "dwu")
    dx1, dsh2, dsc2, gpre2 = _normmod_bwd(lay, res["x1"], dh2, dx2, p["norm_ffn_pre"].reshape(1, D), modarr, 4, n + "nm2")
    dm, dg1, gpost1 = _resnorm_bwd(lay, dx1, res["m"], p["norm_mix_post"].reshape(1, D), modarr, 2, n + "rn1")
    big["wout"] = _mm_tn(res["mcat"], dm, MXU_DTYPE, n + "dwout")
    dmcat = _mm([(dm, w["wout"])], True, F32, n + "dmcat")
    da, ds, dc = dmcat[:, :A_W], dmcat[:, A_W:A_W + B_W], dmcat[:, A_W + B_W:]
    z = res["z"]
    dz_a, dws, dgb = _gate_bwd(lay, z, da, cst["ws"], cst["wst"], cst["gbias"], n + "gate")
    dy, dud, dwglu, dbglu, ddvec = _glu_bwd(lay, z, res["y"], ds, cst["dvec"], w["wglu"], cst["bglu"], n + "glu")
    dy2 = _to_scan_rows(lay, dy)
    du2, dbh, dch, dar, dai = _ssm_bwd(res["u2"], dy2, res["hst"], sp["bh"], sp["ch"], sp["ar8"], sp["ai8"], n + "ssm")
    duf, dur = _from_scan_rows(lay, du2)
    dz_p, dpw, dpsc = _pool_bwd(lay, z, dc, bands, inv, cst["pw"], cst["pscale"], n + "pool")
    dz_s = (duf.astype(F32) + dur.astype(F32) + dud).astype(MXU_DTYPE)
    dz = jnp.concatenate([dz_a, dz_s, dz_p], axis=1)
    big["wglu"] = dwglu.astype(MXU_DTYPE)
    big["win_t"] = _mm_tn(dz, res["h"], MXU_DTYPE, n + "dwin")
    dh = _mm([(dz, w["win_t"])], False, F32, n + "dh")
    dx, dsh1, dsc1, gpre1 = _normmod_bwd(lay, res["x0"], dh, dx1, p["norm_mix_pre"].reshape(1, D), modarr, 1, n + "nm1")

    small.update(norm_mix_pre=gpre1[0], norm_mix_post=gpost1[0], norm_ffn_pre=gpre2[0], norm_ffn_post=gpost2[0])
    small["sgu_w"] = dws
    small["sgu_b"] = jnp.sum(dgb.reshape(CHUNK, 4, 64), axis=-1).T
    small.update(_ssm_param_grads(sp, dbh, dch, dar, dai, n + "disc"))
    small["ssm_d"] = ddvec.reshape(SSM_G, SSM_H)
    small["glu_b"] = dbglu[0]
    small["pool_w"] = jnp.stack([dpw[64 * k:64 * k + 64, 64 * k:64 * k + 64] for k in range(4)])
    small["pool_scale"] = dpsc[0]
    dmod = jnp.concatenate([dsh1, dsc1, dg1, dsh2, dsc2, dg2], axis=1)[:lay.bl + 1]
    dmod = jnp.concatenate([dmod, jnp.zeros((8 - lay.bl - 1, 6, D), F32)], axis=0)
    return dx, big, small, dmod


SMALL_NAMES = ["norm_mix_pre", "norm_mix_post", "norm_ffn_pre", "norm_ffn_post", "sgu_w", "sgu_b", "ssm_lam_re",
               "ssm_lam_im", "ssm_log_dt", "ssm_b_re", "ssm_b_im", "ssm_c_re", "ssm_c_im", "ssm_d", "glu_b", "pool_w",
               "pool_scale"]
BIG_NAMES = ["win_t", "wout", "wglu", "wg_t", "wu_t", "wd"]


def _sincos_2d(rows, cols, dim):
    quarter = dim // 4
    omega = 1.0 / (10000.0 ** (jnp.arange(quarter, dtype=F32) / quarter))
    r = jnp.arange(rows, dtype=F32)[:, None] * omega
    cc = jnp.arange(cols, dtype=F32)[:, None] * omega
    er = jnp.concatenate([jnp.sin(r), jnp.cos(r)], axis=-1)
    ec = jnp.concatenate([jnp.sin(cc), jnp.cos(cc)], axis=-1)
    pe = jnp.concatenate([jnp.broadcast_to(er[:, None, :], (rows, cols, dim // 2)),
                          jnp.broadcast_to(ec[None, :, :], (rows, cols, dim // 2))], axis=-1)
    return pe.reshape(rows * cols, dim)


def _core(x, ctx, target, mods_local, params, weights):
    bl, lat, _ = x.shape
    lay = _Layout(bl, lat)
    pe = _sincos_2d(lat // GRID_W, GRID_W, D)
    xt = _embed(lay, x.reshape(bl * lat, D), ctx.reshape(bl * CTX, D), pe)
    bands_np, inv_np = _band_constants()
    bands, inv = jnp.asarray(bands_np, MXU_DTYPE), jnp.asarray(inv_np, F32)
    rows = lay.modrows_static()
    modarrs, csts, sps, ress = [], [], [], []
    for i in range(2):
        modarrs.append(mods_local[i][rows].reshape(lay.nb * 6, 1, D))
        csts.append(_layer_consts(params[i]))
        p = params[i]
        sps.append(_ssm_params(p["ssm_lam_re"], p["ssm_lam_im"], p["ssm_log_dt"], p["ssm_b_re"], p["ssm_b_im"],
                               p["ssm_c_re"], p["ssm_c_im"], f"l{i}_disc"))
    for i in range(2):
        xt, res = _layer_fwd(lay, i, xt, modarrs[i], params[i], weights[i], csts[i], sps[i], bands, inv)
        ress.append(res)
    dx, lossv = _loss_bwd(lay, xt, target.reshape(bl * lat, D))
    bigs, smalls, dmods = [None, None], [None, None], [None, None]
    for i in (1, 0):
        dx, bigs[i], smalls[i], dmods[i] = _layer_bwd(lay, i, dx, modarrs[i], params[i], weights[i], csts[i], sps[i],
                                                       bands, inv, ress[i])
    grad_x = dx[lay.nctx * TB:].reshape(bl, lat, D)
    return lossv[0, 0], grad_x, bigs, smalls, dmods


def _my_index():
    return 4 * lax.axis_index("x") + 2 * lax.axis_index("y") + lax.axis_index("c")


def _peer(k):
    x, y, c = lax.axis_index("x"), lax.axis_index("y"), lax.axis_index("c")
    kx, ky, kc = (k >> 2) & 1, (k >> 1) & 1, k & 1
    px = 1 - x if kx else x
    py = 1 - y if ky else y
    pc = 1 - c if kc else c
    return (px, py, pc), 4 * px + 2 * py + pc


def _comm(items, name):
    n = len(items)
    ncopies = sum(len(it[2]) for it in items)

    def slot_of(idx, slots):
        return idx if slots == 8 else (idx // 2 if slots == 4 else idx % 2)

    def body(*refs):
        ins, outs = refs[:n], refs[n:2 * n]
        send_sems, recv_sems, local_sems = refs[2 * n:]
        me = _my_index()
        local, sends, recvs = [], [], []
        q = 0
        for t, (arr, mode, ks, slots) in enumerate(items):
            src_own = ins[t] if mode == "gather" else ins[t].at[me]
            cp = pltpu.make_async_copy(src_own, outs[t].at[slot_of(me, slots)], local_sems.at[t])
            cp.start()
            local.append(cp)
            for k in ks:
                peer, pidx = _peer(k)
                src = ins[t] if mode == "gather" else ins[t].at[pidx]
                sends.append(pltpu.make_async_remote_copy(
                    src_ref=src, dst_ref=outs[t].at[slot_of(me, slots)], send_sem=send_sems.at[q], recv_sem=recv_sems.at[q],
                    device_id=peer, device_id_type=pl.DeviceIdType.MESH))
                recvs.append(pltpu.make_async_remote_copy(
                    src_ref=src, dst_ref=outs[t].at[slot_of(pidx, slots)], send_sem=send_sems.at[q], recv_sem=recv_sems.at[q],
                    device_id=peer, device_id_type=pl.DeviceIdType.MESH))
                q += 1
        for cp in sends:
            cp.start()
        for cp in recvs:
            cp.wait_recv()
        for cp in sends:
            cp.wait_send()
        for cp in local:
            cp.wait()

    out_shape = []
    for arr, mode, ks, slots in items:
        shp = (slots,) + tuple(arr.shape) if mode == "gather" else tuple(arr.shape)
        out_shape.append(jax.ShapeDtypeStruct(shp, arr.dtype))
    anyspec = pl.BlockSpec(memory_space=pl.ANY)
    return pl.pallas_call(
        body, in_specs=[anyspec] * n, out_specs=[anyspec] * n, out_shape=out_shape,
        scratch_shapes=[pltpu.SemaphoreType.DMA((ncopies,)), pltpu.SemaphoreType.DMA((ncopies,)),
                        pltpu.SemaphoreType.DMA((n,))],
        compiler_params=pltpu.CompilerParams(has_side_effects=True), name=name)(*[it[0] for it in items])


ALL7 = (1, 2, 3, 4, 5, 6, 7)
CHIPS3 = (2, 4, 6)


def _sum8(parts, name):
    def one(a, nm):
        _, r, c = a.shape
        tr = r if r <= 512 else _pick_rows(r)

        def body(a_ref, o_ref):
            acc = a_ref[0].astype(F32)
            for q in range(1, a_ref.shape[0]):
                acc = acc + a_ref[q].astype(F32)
            o_ref[...] = acc

        return pl.pallas_call(
            body, grid=(r // tr,), in_specs=[pl.BlockSpec((a.shape[0], tr, c), lambda i: (0, i, 0))],
            out_specs=pl.BlockSpec((tr, c), lambda i: (i, 0)), out_shape=jax.ShapeDtypeStruct((r, c), F32),
            compiler_params=_cp(("parallel",)), name=nm)(a)

    return [one(a, f"{name}{i}") for i, a in enumerate(parts)]


def _pick_rows(r):
    for t in (512, 352, 256, 176, 128, 64, 32, 16, 8):
        if r % t == 0:
            return t
    return r


def _adam(w, g, m, v, name):
    shape = w.shape
    nel = int(np.prod(shape))
    lanes = 512 if nel % 512 == 0 else 128
    r = nel // lanes
    tr = r if r <= 1024 else _pick_rows(r)
    c1 = 1.0 / (1.0 - ADAM_B1 ** ADAM_STEP)
    c2 = 1.0 / (1.0 - ADAM_B2 ** ADAM_STEP)

    def body(w_ref, g_ref, m_ref, v_ref, d_ref, nm_ref, nv_ref):
        gv = g_ref[...]
        nm = ADAM_B1 * m_ref[...] + (1.0 - ADAM_B1) * gv
        nv = ADAM_B2 * v_ref[...] + (1.0 - ADAM_B2) * (gv * gv)
        d_ref[...] = -ADAM_LR * ((nm * c1) / (jnp.sqrt(nv * c2) + ADAM_EPS) + ADAM_WD * w_ref[...])
        nm_ref[...] = nm
        nv_ref[...] = nv

    spec = pl.BlockSpec((tr, lanes), lambda i: (i, 0))
    sh = jax.ShapeDtypeStruct((r, lanes), F32)
    outs = pl.pallas_call(
        body, grid=(r // tr,), in_specs=[spec] * 4, out_specs=[spec] * 3, out_shape=[sh] * 3,
        compiler_params=_cp(("parallel",)), name=name)(*[a.reshape(r, lanes) for a in (w, g, m, v)])
    return [o.reshape(shape) for o in outs]


def _silu(x):
    return x * _sigmoid(x)


def _mod_fwd(c_rows, w_mod, b_cols, name):
    def body(c_ref, w_ref, b_ref, o_ref):
        s = _silu(c_ref[...])
        for l in range(2):
            o_ref[l] = jnp.dot(s, w_ref[l], preferred_element_type=F32, precision=lax.Precision.HIGHEST) + b_ref[l]

    nc = w_mod.shape[2]
    return pl.pallas_call(body, out_shape=jax.ShapeDtypeStruct((2, c_rows.shape[0], nc), F32),
                          compiler_params=_cp(None), name=name)(c_rows, w_mod, b_cols)


def _mod_bwd(c_rows, w_mod, dlat, dctx8, name):
    nrow = c_rows.shape[0]
    nb = nrow - 8

    def body(c_ref, w_ref, dl_ref, dc_ref, gw_ref, gc_ref):
        s = _silu(c_ref[...])
        ctx_row = lax.broadcasted_iota(jnp.int32, (nrow, 1), 0) == nb
        gc = jnp.zeros((1, D), F32)
        for l in range(2):
            dctx = dc_ref[0, l]
            for q in range(1, 8):
                dctx = dctx + dc_ref[q, l]
            dm = dl_ref[l] + jnp.where(ctx_row, dctx, 0.0)
            gw_ref[l] = lax.dot_general(s, dm, (((0,), (0,)), ((), ())), preferred_element_type=F32,
                                        precision=lax.Precision.HIGHEST)
            gc = gc + lax.dot_general(dctx, w_ref[l], (((1,), (1,)), ((), ())), preferred_element_type=F32,
                                      precision=lax.Precision.HIGHEST)
        gc_ref[...] = gc

    nc = w_mod.shape[2]
    return pl.pallas_call(body, out_shape=[jax.ShapeDtypeStruct((2, D, nc), F32), jax.ShapeDtypeStruct((1, D), F32)],
                          compiler_params=_cp(None), name=name)(c_rows, w_mod, dlat, dctx8)


def _bmod_cctx(dmod_all, gc4, c_ctx, name):
    def body(dm_ref, gc_ref, cc_ref, gb_ref, gcc_ref):
        for l in range(2):
            acc = jnp.sum(dm_ref[0, l], axis=0, keepdims=True)
            for q in range(1, 8):
                acc = acc + jnp.sum(dm_ref[q, l], axis=0, keepdims=True)
            gb_ref[l:l + 1, :] = acc
        g = gc_ref[0] + gc_ref[1] + gc_ref[2] + gc_ref[3]
        cv = cc_ref[...]
        sg = _sigmoid(cv)
        gcc_ref[...] = g * (sg * (1.0 + cv * (1.0 - sg)))

    return pl.pallas_call(body, out_shape=[jax.ShapeDtypeStruct((2, 6 * D), F32), jax.ShapeDtypeStruct((1, D), F32)],
                          compiler_params=_cp(None), name=name)(dmod_all, gc4, c_ctx)


def kernel(x, c, ctx, c_ctx, w_mod, b_mod, norm_mix_pre, norm_mix_post, norm_ffn_pre, norm_ffn_post, w_in, w_out, sgu_w, sgu_b, ssm_lam_re, ssm_lam_im, ssm_log_dt, ssm_b_re, ssm_b_im, ssm_c_re, ssm_c_im, ssm_d, glu_w, glu_b, pool_w, pool_scale, ffn_w_gate, ffn_w_up, ffn_w_down, loss_target, m_c_ctx, m_w_mod, m_b_mod, m_norm_mix_pre, m_norm_mix_post, m_norm_ffn_pre, m_norm_ffn_post, m_w_in, m_w_out, m_sgu_w, m_sgu_b, m_ssm_lam_re, m_ssm_lam_im, m_ssm_log_dt, m_ssm_b_re, m_ssm_b_im, m_ssm_c_re, m_ssm_c_im, m_ssm_d, m_glu_w, m_glu_b, m_pool_w, m_pool_scale, m_ffn_w_gate, m_ffn_w_up, m_ffn_w_down, v_c_ctx, v_w_mod, v_b_mod, v_norm_mix_pre, v_norm_mix_post, v_norm_ffn_pre, v_norm_ffn_post, v_w_in, v_w_out, v_sgu_w, v_sgu_b, v_ssm_lam_re, v_ssm_lam_im, v_ssm_log_dt, v_ssm_b_re, v_ssm_b_im, v_ssm_c_re, v_ssm_c_im, v_ssm_d, v_glu_w, v_glu_b, v_pool_w, v_pool_scale, v_ffn_w_gate, v_ffn_w_up, v_ffn_w_down):
    wts = dict(c_ctx=c_ctx, w_mod=w_mod, b_mod=b_mod, norm_mix_pre=norm_mix_pre, norm_mix_post=norm_mix_post,
               norm_ffn_pre=norm_ffn_pre, norm_ffn_post=norm_ffn_post, w_in=w_in, w_out=w_out, sgu_w=sgu_w, sgu_b=sgu_b,
               ssm_lam_re=ssm_lam_re, ssm_lam_im=ssm_lam_im, ssm_log_dt=ssm_log_dt, ssm_b_re=ssm_b_re, ssm_b_im=ssm_b_im,
               ssm_c_re=ssm_c_re, ssm_c_im=ssm_c_im, ssm_d=ssm_d, glu_w=glu_w, glu_b=glu_b, pool_w=pool_w,
               pool_scale=pool_scale, ffn_w_gate=ffn_w_gate, ffn_w_up=ffn_w_up, ffn_w_down=ffn_w_down)
    ms = dict(c_ctx=m_c_ctx, w_mod=m_w_mod, b_mod=m_b_mod, norm_mix_pre=m_norm_mix_pre, norm_mix_post=m_norm_mix_post,
              norm_ffn_pre=m_norm_ffn_pre, norm_ffn_post=m_norm_ffn_post, w_in=m_w_in, w_out=m_w_out, sgu_w=m_sgu_w,
              sgu_b=m_sgu_b, ssm_lam_re=m_ssm_lam_re, ssm_lam_im=m_ssm_lam_im, ssm_log_dt=m_ssm_log_dt,
              ssm_b_re=m_ssm_b_re, ssm_b_im=m_ssm_b_im, ssm_c_re=m_ssm_c_re, ssm_c_im=m_ssm_c_im, ssm_d=m_ssm_d,
              glu_w=m_glu_w, glu_b=m_glu_b, pool_w=m_pool_w, pool_scale=m_pool_scale, ffn_w_gate=m_ffn_w_gate,
              ffn_w_up=m_ffn_w_up, ffn_w_down=m_ffn_w_down)
    vs = dict(c_ctx=v_c_ctx, w_mod=v_w_mod, b_mod=v_b_mod, norm_mix_pre=v_norm_mix_pre, norm_mix_post=v_norm_mix_post,
              norm_ffn_pre=v_norm_ffn_pre, norm_ffn_post=v_norm_ffn_post, w_in=v_w_in, w_out=v_w_out, sgu_w=v_sgu_w,
              sgu_b=v_sgu_b, ssm_lam_re=v_ssm_lam_re, ssm_lam_im=v_ssm_lam_im, ssm_log_dt=v_ssm_log_dt,
              ssm_b_re=v_ssm_b_re, ssm_b_im=v_ssm_b_im, ssm_c_re=v_ssm_c_re, ssm_c_im=v_ssm_c_im, ssm_d=v_ssm_d,
              glu_w=v_glu_w, glu_b=v_glu_b, pool_w=v_pool_w, pool_scale=v_pool_scale, ffn_w_gate=v_ffn_w_gate,
              ffn_w_up=v_ffn_w_up, ffn_w_down=v_ffn_w_down)
    order = list(wts.keys())
    bl = x.shape[0]
    nseq = bl * N_DEV
    me = _my_index()
    chip = me // 2
    ncol = w_mod.shape[2]

    (c_all,) = _comm([(c, "gather", ALL7, 8)], "ag_c")
    nrow = nseq + 8
    c_rows = jnp.concatenate([c_all.reshape(nseq, D), c_ctx[None], jnp.zeros((7, D), F32)], axis=0)
    b_cols = lax.dynamic_slice_in_dim(b_mod, chip * ncol, ncol, axis=1)[:, None, :]
    mod_cols = _mod_fwd(c_rows, w_mod, b_cols, "mod_fwd")
    (mod4,) = _comm([(mod_cols, "gather", CHIPS3, 4)], "ag_mod")
    mods = jnp.transpose(mod4, (1, 2, 0, 3)).reshape(2, nrow, 6 * D)
    mods_local = jnp.concatenate([lax.dynamic_slice_in_dim(mods, me * bl, bl, axis=1), mods[:, nseq:nseq + 1],
                                  jnp.zeros((2, 8 - bl - 1, 6 * D), F32)], axis=1)

    shards = []
    for i in range(2):
        shards += [w_in[i].T, w_out[i], glu_w[i], ffn_w_gate[i].T, ffn_w_up[i].T, ffn_w_down[i]]
    gathered = _comm([(s.astype(MXU_DTYPE), "gather", CHIPS3, 4) for s in shards], "ag_weights")
    weights = []
    for i in range(2):
        weights.append({nme: g.reshape(-1, g.shape[-1]) for nme, g in zip(BIG_NAMES, gathered[6 * i:6 * i + 6])})
    params = [{k: wts[k][i] for k in SMALL_NAMES} for i in range(2)]

    loss_part, grad_x, bigs, smalls, dmods = _core(x, ctx, loss_target, mods_local, params, weights)
    loss = lax.psum(loss_part, ("x", "y", "c"))

    dmod_local = jnp.stack([dmods[i].reshape(8, 6 * D) for i in range(2)])
    (dmod_all,) = _comm([(dmod_local, "gather", ALL7, 8)], "ag_dmod")
    dcols = lax.dynamic_slice_in_dim(dmod_all, chip * ncol, ncol, axis=3)
    dlat = jnp.transpose(dcols[:, :, :bl], (1, 0, 2, 3)).reshape(2, nseq, ncol)
    dlat = jnp.concatenate([dlat, jnp.zeros((2, 8, ncol), F32)], axis=1)
    dctx8 = dcols[:, :, bl:bl + 1]
    g_w_mod, gc_part = _mod_bwd(c_rows, w_mod, dlat, dctx8, "mod_bwd")
    (gc4,) = _comm([(gc_part, "gather", CHIPS3, 4)], "ag_cctx")
    g_b_mod, g_c_ctx = _bmod_cctx(dmod_all, gc4, c_ctx[None], "bmod_cctx")

    small_flat = jnp.concatenate([jnp.stack([smalls[i][k] for i in range(2)]).reshape(-1) for k in SMALL_NAMES])
    npad = (-small_flat.shape[0]) % (8 * 1024)
    small_flat = jnp.concatenate([small_flat, jnp.zeros((npad,), F32)])
    a2a_items = []
    for i in range(2):
        for k in BIG_NAMES:
            g = bigs[i][k]
            a2a_items.append((g.reshape(8, g.shape[0] // 8, g.shape[1]), "a2a", ALL7, 8))
    a2a_items.append((small_flat.reshape(8, -1, 1024), "a2a", ALL7, 8))
    pieces = _comm(a2a_items, "a2a_grads")
    sums = _sum8(pieces, "gsum")
    items = [(s, "gather", (1,), 2) for s in sums[:-1]] + [(sums[-1], "gather", ALL7, 8)]
    fin = _comm(items, "ag_grads")
    big_g = [{k: fin[6 * i + j].reshape(-1, fin[6 * i + j].shape[-1]) for j, k in enumerate(BIG_NAMES)} for i in range(2)]
    small_red = fin[-1].reshape(-1)

    grads = {}
    off = 0
    for k in SMALL_NAMES:
        shp = wts[k].shape
        nel = int(np.prod(shp))
        grads[k] = small_red[off:off + nel].reshape(shp)
        off += nel
    grads["c_ctx"] = g_c_ctx[0]
    grads["w_mod"] = g_w_mod
    grads["b_mod"] = g_b_mod
    grads["w_in"] = jnp.stack([big_g[i]["win_t"].T for i in range(2)])
    grads["w_out"] = jnp.stack([big_g[i]["wout"] for i in range(2)])
    grads["glu_w"] = jnp.stack([big_g[i]["wglu"] for i in range(2)])
    grads["ffn_w_gate"] = jnp.stack([big_g[i]["wg_t"].T for i in range(2)])
    grads["ffn_w_up"] = jnp.stack([big_g[i]["wu_t"].T for i in range(2)])
    grads["ffn_w_down"] = jnp.stack([big_g[i]["wd"] for i in range(2)])

    deltas, new_m, new_v = {}, {}, {}
    for k in order:
        deltas[k], new_m[k], new_v[k] = _adam(wts[k], grads[k], ms[k], vs[k], "adam_" + k)
    return (loss, grad_x, *[grads[k] for k in order], *[deltas[k] for k in order],
            *[new_m[k] for k in order], *[new_v[k] for k in order])
```

```python
import functools
import math

import numpy as np
import jax
import jax.numpy as jnp
from jax import lax
from jax.experimental import pallas as pl
from jax.experimental.pallas import tpu as pltpu

F32 = jnp.float32
BF16 = jnp.bfloat16
MXU_DTYPE = jnp.bfloat16
MCAT_A, MCAT_C = 2, 3
WOUT_PERM, WOUT_INV = (1, 2, 0, 3), (2, 0, 1, 3)

D = 1024
EPS = 1e-6
TB = 256
CTX = 256
CHUNK = 128
GRID_W = 64
A_W, B_W, C_W = 256, 512, 256
D_IN = 1280
D_FF = 2816
SSM_G, SSM_P, SSM_H = 32, 64, 16
ST = 64
POOL_WINDOWS = (2, 4, 8, 16)
N_DEV = 8
VMEM_LIMIT = 52 * 1024 * 1024
GELU_C = math.sqrt(2.0 / math.pi)

ADAM_LR, ADAM_B1, ADAM_B2, ADAM_EPS, ADAM_WD, ADAM_STEP = 0.001, 0.9, 0.999, 1e-08, 0.01, 10


def _cp(sem=None, vmem=VMEM_LIMIT, **kw):
    return pltpu.CompilerParams(dimension_semantics=sem, vmem_limit_bytes=vmem, **kw)


def _pick(n, cap):
    if n <= cap:
        return n
    best = None
    for t in range(128, cap + 1, 128):
        if n % t == 0:
            best = t
    assert best is not None, (n, cap)
    return best


def _gelu(x):
    return 0.5 * x * (1.0 + jnp.tanh(GELU_C * (x + 0.044715 * x * x * x)))


def _gelu_grad(x):
    t = jnp.tanh(GELU_C * (x + 0.044715 * x * x * x))
    return 0.5 * (1.0 + t) + 0.5 * x * (1.0 - t * t) * GELU_C * (1.0 + 3.0 * 0.044715 * x * x)


def _sigmoid(x):
    return 1.0 / (1.0 + jnp.exp(-x))


def _dot(a, b, dims):
    return lax.dot_general(a, b, (dims, ((), ())), preferred_element_type=F32)


def _nn(a, b):
    return _dot(a, b, ((1,), (0,)))


def _nt(a, b):
    return _dot(a, b, ((1,), (1,)))


def _tn(a, b):
    return _dot(a, b, ((0,), (0,)))


def _mm(pairs, nt, out_dtype, name, tm=512):
    m = pairs[0][0].shape[0]
    n = pairs[0][1].shape[0] if nt else pairs[0][1].shape[1]
    tn = _pick(n, 1408)
    tm = min(tm, m)
    npairs = len(pairs)

    def body(*refs):
        o_ref = refs[-1]
        acc = None
        for i in range(npairs):
            a = refs[2 * i][...].astype(MXU_DTYPE)
            b = refs[2 * i + 1][...].astype(MXU_DTYPE)
            r = _nt(a, b) if nt else _nn(a, b)
            acc = r if acc is None else acc + r
        o_ref[...] = acc.astype(o_ref.dtype)

    in_specs, flat = [], []
    for a, b in pairs:
        k = a.shape[1]
        in_specs.append(pl.BlockSpec((tm, k), lambda i, j: (i, 0)))
        in_specs.append(pl.BlockSpec((tn, k), lambda i, j: (j, 0)) if nt else pl.BlockSpec((k, tn), lambda i, j: (0, j)))
        flat += [a, b]
    return pl.pallas_call(
        body, grid=(m // tm, n // tn), in_specs=in_specs,
        out_specs=pl.BlockSpec((tm, tn), lambda i, j: (i, j)),
        out_shape=jax.ShapeDtypeStruct((m, n), out_dtype),
        compiler_params=_cp(("parallel", "parallel")), name=name)(*flat)


def _mm_tn(a, b, out_dtype, name, tm=512):
    m, k1 = a.shape
    n = b.shape[1]
    t1 = _pick(k1, 1408)
    tn = _pick(n, 1024)
    tm = min(tm, m)
    nsteps = m // tm

    def body(a_ref, b_ref, o_ref, acc_ref):
        t = pl.program_id(2)

        @pl.when(t == 0)
        def _():
            acc_ref[...] = jnp.zeros_like(acc_ref)

        acc_ref[...] += _tn(a_ref[...].astype(MXU_DTYPE), b_ref[...].astype(MXU_DTYPE))

        @pl.when(t == nsteps - 1)
        def _():
            o_ref[...] = acc_ref[...].astype(o_ref.dtype)

    return pl.pallas_call(
        body, grid=(k1 // t1, n // tn, nsteps),
        in_specs=[pl.BlockSpec((tm, t1), lambda i, j, t: (t, i)), pl.BlockSpec((tm, tn), lambda i, j, t: (t, j))],
        out_specs=pl.BlockSpec((t1, tn), lambda i, j, t: (i, j)),
        out_shape=jax.ShapeDtypeStruct((k1, n), out_dtype),
        scratch_shapes=[pltpu.VMEM((t1, tn), F32)],
        compiler_params=_cp(("parallel", "parallel", "arbitrary")), name=name)(a, b)


class _Layout:
    def __init__(self, bl, lat):
        self.bl, self.lat = bl, lat
        self.nlb = lat // TB
        self.nr = 1 + self.nlb
        self.nctx = bl
        self.nb = self.nr * bl
        self.nt = self.nb * TB
        self.ctx_row = bl

    def blk(self, g):
        gg = g - self.bl
        return jnp.where(g < self.bl, g, (gg % self.nlb + 1) * self.bl + gg // self.nlb)

    def modrow(self, g):
        return jnp.where(g < self.bl, self.ctx_row, (g - self.bl) // self.nlb)

    def first_of_row(self, g):
        return jnp.logical_or(g == 0, jnp.logical_and(g >= self.bl, (g - self.bl) % self.nlb == 0))

    def modrows_static(self):
        return np.array([self.ctx_row if j < self.bl else j % self.bl for j in range(self.nb)], np.int32)


def _tok_spec(lay):
    return pl.BlockSpec((TB, D), lambda g: (lay.blk(g), 0))


def _vec_spec():
    return pl.BlockSpec((1, D), lambda j: (0, 0))


def _mod_spec(lay, k):
    return pl.BlockSpec((1, 1, D), lambda g: (lay.blk(g) * 6 + k, 0, 0))


def _embed(lay, x2d, ctx2d, pe):
    bl, nlb = lay.bl, lay.nlb

    def body(x_ref, c_ref, pe_ref, o_ref):
        j = pl.program_id(0)

        @pl.when(j < bl)
        def _():
            o_ref[...] = c_ref[...]

        @pl.when(j >= bl)
        def _():
            o_ref[...] = x_ref[...] + pe_ref[...]

    pos = lambda j: jnp.maximum(j // bl - 1, 0)
    return pl.pallas_call(
        body, grid=(lay.nb,),
        in_specs=[pl.BlockSpec((TB, D), lambda j: ((j % bl) * nlb + pos(j), 0)),
                  pl.BlockSpec((TB, D), lambda j: (jnp.minimum(j, bl - 1), 0)),
                  pl.BlockSpec((TB, D), lambda j: (pos(j), 0))],
        out_specs=pl.BlockSpec((TB, D), lambda j: (j, 0)), out_shape=jax.ShapeDtypeStruct((lay.nt, D), F32),
        compiler_params=_cp(("parallel",)), name="embed")(x2d, ctx2d, pe)


def _normmod_fwd(lay, x, gain, modarr, ksh, ksc, name):
    def body(x_ref, g_ref, sh_ref, sc_ref, o_ref):
        xv = x_ref[...]
        r = lax.rsqrt(jnp.mean(xv * xv, axis=-1, keepdims=True) + EPS)
        o_ref[...] = ((xv * r * g_ref[...]) * (1.0 + sc_ref[0]) + sh_ref[0]).astype(o_ref.dtype)

    return pl.pallas_call(
        body, grid=(lay.nb,), in_specs=[_tok_spec(lay), _vec_spec(), _mod_spec(lay, ksh), _mod_spec(lay, ksc)],
        out_specs=_tok_spec(lay), out_shape=jax.ShapeDtypeStruct((lay.nt, D), MXU_DTYPE),
        compiler_params=_cp(("parallel",)), name=name)(x, gain, modarr, modarr)


def _acc_specs(lay):
    row = pl.BlockSpec((1, 1, D), lambda j: (lay.modrow(j), 0, 0))
    return row, jax.ShapeDtypeStruct((8, 1, D), F32)


def _normmod_bwd(lay, x, dh, dx_in, gain, modarr, ksc, name, latent_only=False):
    row_spec, row_shape = _acc_specs(lay)
    if latent_only:
        dx_spec = pl.BlockSpec((TB, D), lambda g: (jnp.maximum(g - lay.bl, 0), 0))
        dx_shape = jax.ShapeDtypeStruct((lay.bl * lay.lat, D), F32)
    else:
        dx_spec, dx_shape = _tok_spec(lay), jax.ShapeDtypeStruct((lay.nt, D), F32)

    def body(x_ref, dh_ref, dxi_ref, g_ref, sc_ref, dx_ref, dsh_ref, dsc_ref, dg_ref):
        j = pl.program_id(0)
        xv = x_ref[...]
        dhv = dh_ref[...].astype(F32)
        g = g_ref[...]
        sc1 = 1.0 + sc_ref[0]
        r = lax.rsqrt(jnp.mean(xv * xv, axis=-1, keepdims=True) + EPS)
        xh = xv * r
        dxh = dhv * (g * sc1)
        dx = r * (dxh - xh * jnp.mean(dxh * xh, axis=-1, keepdims=True))
        dx_ref[...] = dxi_ref[...] + dx

        @pl.when(lay.first_of_row(j))
        def _():
            dsh_ref[...] = jnp.zeros_like(dsh_ref)
            dsc_ref[...] = jnp.zeros_like(dsc_ref)

        @pl.when(j == 0)
        def _():
            dg_ref[...] = jnp.zeros_like(dg_ref)

        dsh_ref[0] += jnp.sum(dhv, axis=0, keepdims=True)
        dsc_ref[0] += jnp.sum(dhv * (xh * g), axis=0, keepdims=True)
        dg_ref[...] += jnp.sum(dhv * sc1 * xh, axis=0, keepdims=True)

    return pl.pallas_call(
        body, grid=(lay.nb,),
        in_specs=[_tok_spec(lay), _tok_spec(lay), _tok_spec(lay), _vec_spec(), _mod_spec(lay, ksc)],
        out_specs=[dx_spec, row_spec, row_spec, _vec_spec()],
        out_shape=[dx_shape, row_shape, row_shape, jax.ShapeDtypeStruct((1, D), F32)],
        compiler_params=_cp(("arbitrary",)), name=name)(x, dh, dx_in, gain, modarr)


def _resnorm_fwd(lay, x, m, gain, modarr, kgate, name):
    def body(x_ref, m_ref, g_ref, gate_ref, o_ref):
        mv = m_ref[...]
        r = lax.rsqrt(jnp.mean(mv * mv, axis=-1, keepdims=True) + EPS)
        o_ref[...] = x_ref[...] + gate_ref[0] * (mv * r * g_ref[...])

    return pl.pallas_call(
        body, grid=(lay.nb,), in_specs=[_tok_spec(lay), _tok_spec(lay), _vec_spec(), _mod_spec(lay, kgate)],
        out_specs=_tok_spec(lay), out_shape=jax.ShapeDtypeStruct((lay.nt, D), F32),
        compiler_params=_cp(("parallel",)), name=name)(x, m, gain, modarr)


def _resnorm_bwd(lay, dxn, m, gain, modarr, kgate, name):
    row_spec, row_shape = _acc_specs(lay)

    def body(d_ref, m_ref, g_ref, gate_ref, dm_ref, dgate_ref, dg_ref):
        j = pl.program_id(0)
        dv = d_ref[...]
        mv = m_ref[...]
        g = g_ref[...]
        r = lax.rsqrt(jnp.mean(mv * mv, axis=-1, keepdims=True) + EPS)
        xh = mv * r
        dy = dv * gate_ref[0]
        dxh = dy * g
        dm_ref[...] = (r * (dxh - xh * jnp.mean(dxh * xh, axis=-1, keepdims=True))).astype(dm_ref.dtype)

        @pl.when(lay.first_of_row(j))
        def _():
            dgate_ref[...] = jnp.zeros_like(dgate_ref)

        @pl.when(j == 0)
        def _():
            dg_ref[...] = jnp.zeros_like(dg_ref)

        dgate_ref[0] += jnp.sum(dv * (xh * g), axis=0, keepdims=True)
        dg_ref[...] += jnp.sum(dy * xh, axis=0, keepdims=True)

    return pl.pallas_call(
        body, grid=(lay.nb,), in_specs=[_tok_spec(lay), _tok_spec(lay), _vec_spec(), _mod_spec(lay, kgate)],
        out_specs=[_tok_spec(lay), row_spec, _vec_spec()],
        out_shape=[jax.ShapeDtypeStruct((lay.nt, D), MXU_DTYPE), row_shape, jax.ShapeDtypeStruct((1, D), F32)],
        compiler_params=_cp(("arbitrary",)), name=name)(dxn, m, gain, modarr)


def _loss_bwd(lay, xf, tgt2d):
    bl, nlb = lay.bl, lay.nlb

    def body(x_ref, t_ref, dx_ref, l_ref):
        j = pl.program_id(0)

        @pl.when(j == 0)
        def _():
            l_ref[...] = jnp.zeros_like(l_ref)

        @pl.when(j < bl)
        def _():
            dx_ref[...] = jnp.zeros_like(dx_ref)

        @pl.when(j >= bl)
        def _():
            e = x_ref[...] - t_ref[...]
            dx_ref[...] = e * (1.0 / D)
            l_ref[...] += jnp.sum(e * e) * (0.5 / D)

    tok = pl.BlockSpec((TB, D), lambda j: (j, 0))
    return pl.pallas_call(
        body, grid=(lay.nb,),
        in_specs=[tok, pl.BlockSpec((TB, D), lambda j: ((j % bl) * nlb + jnp.maximum(j // bl - 1, 0), 0))],
        out_specs=[tok, pl.BlockSpec((8, 128), lambda j: (0, 0))],
        out_shape=[jax.ShapeDtypeStruct((lay.nt, D), F32), jax.ShapeDtypeStruct((8, 128), F32)],
        compiler_params=_cp(("arbitrary",)), name="loss")(xf, tgt2d)


def _ffn_up(h, wgt, wut, name):
    m = h.shape[0]
    tm, tn = min(512, m), D_FF // 2

    def body(h_ref, wg_ref, wu_ref, g_ref, u_ref, a_ref):
        hv = h_ref[...]
        g = _nt(hv, wg_ref[...])
        u = _nt(hv, wu_ref[...])
        g_ref[...] = g.astype(g_ref.dtype)
        u_ref[...] = u.astype(u_ref.dtype)
        a_ref[...] = (g * _sigmoid(g) * u).astype(a_ref.dtype)

    osp = pl.BlockSpec((tm, tn), lambda i, j: (i, j))
    osh = jax.ShapeDtypeStruct((m, D_FF), MXU_DTYPE)
    return pl.pallas_call(
        body, grid=(m // tm, D_FF // tn),
        in_specs=[pl.BlockSpec((tm, D), lambda i, j: (i, 0)), pl.BlockSpec((tn, D), lambda i, j: (j, 0)),
                  pl.BlockSpec((tn, D), lambda i, j: (j, 0))],
        out_specs=[osp, osp, osp], out_shape=[osh, osh, osh],
        compiler_params=_cp(("parallel", "parallel")), name=name)(h, wgt, wut)


def _ffn_down_bwd(df, wd, g, u, name):
    m = df.shape[0]
    tm, tn = min(512, m), D_FF // 2

    def body(df_ref, wd_ref, g_ref, u_ref, dg_ref, du_ref):
        da = _nt(df_ref[...], wd_ref[...])
        gv = g_ref[...].astype(F32)
        uv = u_ref[...].astype(F32)
        s = _sigmoid(gv)
        dg_ref[...] = (da * uv * (s * (1.0 + gv * (1.0 - s)))).astype(dg_ref.dtype)
        du_ref[...] = (da * gv * s).astype(du_ref.dtype)

    osp = pl.BlockSpec((tm, tn), lambda i, j: (i, j))
    osh = jax.ShapeDtypeStruct((m, D_FF), MXU_DTYPE)
    return pl.pallas_call(
        body, grid=(m // tm, D_FF // tn),
        in_specs=[pl.BlockSpec((tm, D), lambda i, j: (i, 0)), pl.BlockSpec((tn, D), lambda i, j: (j, 0)), osp, osp],
        out_specs=[osp, osp], out_shape=[osh, osh],
        compiler_params=_cp(("parallel", "parallel")), name=name)(df, wd, g, u)


def _head_masks(shape):
    lane = lax.broadcasted_iota(jnp.int32, shape, 1)
    return [jnp.logical_and(lane >= 64 * h, lane < 64 * h + 64) for h in range(4)]


def _head_mean(x, masks):
    out = jnp.zeros_like(x)
    for mk in masks:
        s = jnp.sum(jnp.where(mk, x, 0.0), axis=-1, keepdims=True) * (1.0 / 64.0)
        out = jnp.where(mk, s, out)
    return out


def _gate_common(z, masks):
    zg = _gelu(z)
    u = zg[:, :A_W]
    v = zg[:, A_W:]
    mu = _head_mean(v, masks)
    vc = v - mu
    rstd = lax.rsqrt(_head_mean(vc * vc, masks) + EPS)
    return u, vc * rstd, rstd


def _gate_s(vn, ws_ref, bias, masks):
    parts = []
    for c in range(TB // CHUNK):
        vc = vn[c * CHUNK:(c + 1) * CHUNK]
        s = bias
        for h in range(4):
            s = s + _nn(ws_ref[h], jnp.where(masks[h][:CHUNK], vc, 0.0).astype(MXU_DTYPE))
        parts.append(s)
    return jnp.concatenate(parts, axis=0)


def _gate_fwd(lay, z, ws, bias, name):
    def body(z_ref, ws_ref, b_ref, o_ref):
        masks = _head_masks((TB, A_W))
        u, vn, _ = _gate_common(z_ref[...], masks)
        o_ref[...] = (u * _gate_s(vn, ws_ref, b_ref[...], masks)).astype(o_ref.dtype)

    return pl.pallas_call(
        body, grid=(lay.nb,),
        in_specs=[pl.BlockSpec((TB, 2 * A_W), lambda j: (j, 0)), pl.BlockSpec((4, CHUNK, CHUNK), lambda j: (0, 0, 0)),
                  pl.BlockSpec((CHUNK, A_W), lambda j: (0, 0))],
        out_specs=pl.BlockSpec((TB, A_W), lambda j: (j, 0)),
        out_shape=jax.ShapeDtypeStruct((lay.nt, A_W), MXU_DTYPE),
        compiler_params=_cp(("parallel",)), name=name)(z, ws, bias)


def _gate_bwd(lay, z, da, ws, wst, bias, name):
    def body(z_ref, da_ref, ws_ref, wst_ref, b_ref, dz_ref, dws_ref, db_ref):
        j = pl.program_id(0)

        @pl.when(j == 0)
        def _():
            dws_ref[...] = jnp.zeros_like(dws_ref)
            db_ref[...] = jnp.zeros_like(db_ref)

        masks = _head_masks((TB, A_W))
        zv = z_ref[...]
        u, vn, rstd = _gate_common(zv, masks)
        s = _gate_s(vn, ws_ref, b_ref[...], masks)
        dav = da_ref[...].astype(F32)
        du = dav * s
        ds = dav * u
        dvn_parts = []
        for c in range(TB // CHUNK):
            sl = slice(c * CHUNK, (c + 1) * CHUNK)
            ds_c = ds[sl]
            vn_c = vn[sl].astype(MXU_DTYPE)
            db_ref[...] += ds_c
            ds_b = ds_c.astype(MXU_DTYPE)
            dvn_c = jnp.zeros((CHUNK, A_W), F32)
            for h in range(4):
                mk = masks[h][:CHUNK]
                dws_ref[h] += _nt(jnp.where(mk, ds_c, 0.0).astype(MXU_DTYPE), vn_c)
                dvn_c = dvn_c + jnp.where(mk, _nn(wst_ref[h], ds_b), 0.0)
            dvn_parts.append(dvn_c)
        dvn = jnp.concatenate(dvn_parts, axis=0)
        dv = rstd * (dvn - _head_mean(dvn, masks) - vn * _head_mean(dvn * vn, masks))
        gg = _gelu_grad(zv)
        dz_ref[:, :A_W] = (du * gg[:, :A_W]).astype(dz_ref.dtype)
        dz_ref[:, A_W:] = (dv * gg[:, A_W:]).astype(dz_ref.dtype)

    return pl.pallas_call(
        body, grid=(lay.nb,),
        in_specs=[pl.BlockSpec((TB, 2 * A_W), lambda j: (j, 0)), pl.BlockSpec((TB, A_W), lambda j: (j, MCAT_A)),
                  pl.BlockSpec((4, CHUNK, CHUNK), lambda j: (0, 0, 0)), pl.BlockSpec((4, CHUNK, CHUNK), lambda j: (0, 0, 0)),
                  pl.BlockSpec((CHUNK, A_W), lambda j: (0, 0))],
        out_specs=[pl.BlockSpec((TB, 2 * A_W), lambda j: (j, 0)), pl.BlockSpec((4, CHUNK, CHUNK), lambda j: (0, 0, 0)),
                   pl.BlockSpec((CHUNK, A_W), lambda j: (0, 0))],
        out_shape=[jax.ShapeDtypeStruct((lay.nt, 2 * A_W), MXU_DTYPE), jax.ShapeDtypeStruct((4, CHUNK, CHUNK), F32),
                   jax.ShapeDtypeStruct((CHUNK, A_W), F32)],
        compiler_params=_cp(("arbitrary",)), name=name)(z, da, ws, wst, bias)


def _band_constants():
    bands = np.zeros((2, 4, TB, TB), np.float32)
    inv = np.zeros((2, 4, TB, 1), np.float32)
    for kind, n in ((0, GRID_W), (1, TB)):
        for i, w in enumerate(POOL_WINDOWS):
            for t in range(TB):
                base, tt = (t // n) * n, t % n
                lo = min(max(tt - w // 2, 0), n)
                hi = min(max(tt - w // 2 + w, 0), n)
                bands[kind, i, t, base + lo:base + hi] = 1.0
                inv[kind, i, t, 0] = 1.0 / (hi - lo)
    return bands, inv


def _split3(x):
    a = x.astype(MXU_DTYPE)
    r1 = x - a.astype(F32)
    b = r1.astype(MXU_DTYPE)
    c = (r1 - b.astype(F32)).astype(MXU_DTYPE)
    return a, b, c


def _window_apply(band_ref, inv_ref, x, masks, transpose):
    out = jnp.zeros_like(x)
    for i in range(4):
        xi = x * inv_ref[0, i] if transpose else x
        acc = None
        for part in _split3(xi):
            r = _tn(band_ref[0, i], part) if transpose else _nn(band_ref[0, i], part)
            acc = r if acc is None else acc + r
        if not transpose:
            acc = acc * inv_ref[0, i]
        out = jnp.where(masks[i], acc, out)
    return out


def _pool_specs(lay):
    kind = lambda j: jnp.where(j < lay.nctx, 1, 0)
    return [pl.BlockSpec((1, 4, TB, TB), lambda j: (kind(j), 0, 0, 0)), pl.BlockSpec((1, 4, TB, 1), lambda j: (kind(j), 0, 0, 0))]


def _pool_fwd(lay, z, bands, inv, pw, scale, name):
    def body(p_ref, band_ref, inv_ref, pw_ref, sc_ref, o_ref):
        masks = _head_masks((TB, C_W))
        p = p_ref[...]
        diff = _window_apply(band_ref, inv_ref, p, masks, False) - p
        o_ref[...] = (_nn(diff.astype(MXU_DTYPE), pw_ref[...]) * sc_ref[...]).astype(o_ref.dtype)

    return pl.pallas_call(
        body, grid=(lay.nb,),
        in_specs=[pl.BlockSpec((TB, C_W), lambda j: (j, 4))] + _pool_specs(lay)
        + [pl.BlockSpec((C_W, C_W), lambda j: (0, 0)), pl.BlockSpec((1, C_W), lambda j: (0, 0))],
        out_specs=pl.BlockSpec((TB, C_W), lambda j: (j, 0)),
        out_shape=jax.ShapeDtypeStruct((lay.nt, C_W), MXU_DTYPE),
        compiler_params=_cp(("parallel",)), name=name)(z, bands, inv, pw, scale)


def _pool_bwd(lay, z, dc, bands, inv, pw, scale, name):
    def body(p_ref, dc_ref, band_ref, inv_ref, pw_ref, sc_ref, dp_ref, dpw_ref, dsc_ref):
        j = pl.program_id(0)

        @pl.when(j == 0)
        def _():
            dpw_ref[...] = jnp.zeros_like(dpw_ref)
            dsc_ref[...] = jnp.zeros_like(dsc_ref)

        masks = _head_masks((TB, C_W))
        p = p_ref[...]
        dcv = dc_ref[...].astype(F32)
        diff = _window_apply(band_ref, inv_ref, p, masks, False) - p
        diff_b = diff.astype(MXU_DTYPE)
        pre = _nn(diff_b, pw_ref[...])
        dsc_ref[...] += jnp.sum(dcv * pre, axis=0, keepdims=True)
        dpre = dcv * sc_ref[...]
        dpre_b = dpre.astype(MXU_DTYPE)
        dpw_ref[...] += _tn(diff_b, dpre_b)
        ddiff = _nt(dpre_b, pw_ref[...])
        dp_ref[...] = (_window_apply(band_ref, inv_ref, ddiff, masks, True) - ddiff).astype(dp_ref.dtype)

    return pl.pallas_call(
        body, grid=(lay.nb,),
        in_specs=[pl.BlockSpec((TB, C_W), lambda j: (j, 4)), pl.BlockSpec((TB, C_W), lambda j: (j, MCAT_C))] + _pool_specs(lay)
        + [pl.BlockSpec((C_W, C_W), lambda j: (0, 0)), pl.BlockSpec((1, C_W), lambda j: (0, 0))],
        out_specs=[pl.BlockSpec((TB, C_W), lambda j: (j, 0)), pl.BlockSpec((C_W, C_W), lambda j: (0, 0)),
                   pl.BlockSpec((1, C_W), lambda j: (0, 0))],
        out_shape=[jax.ShapeDtypeStruct((lay.nt, C_W), MXU_DTYPE), jax.ShapeDtypeStruct((C_W, C_W), F32),
                   jax.ShapeDtypeStruct((1, C_W), F32)],
        compiler_params=_cp(("arbitrary",)), name=name)(z, dc, bands, inv, pw, scale)


def _disc_math(lr, li, ldt, br, bi):
    dt = jnp.exp(ldt)
    e = jnp.exp(lr * dt)
    ar = e * jnp.cos(li * dt)
    ai = e * jnp.sin(li * dt)
    nr, ni = ar - 1.0, ai
    den = lr * lr + li * li
    qr = (nr * lr + ni * li) / den
    qi = (ni * lr - nr * li) / den
    return ar, ai, qr * br - qi * bi, qr * bi + qi * br


def _disc_fwd(lrx, lix, ldtx, brt, bit, name):
    def body(lr_ref, li_ref, ldt_ref, br_ref, bi_ref, ar_ref, ai_ref, obr_ref, obi_ref):
        ar, ai, obr, obi = _disc_math(lr_ref[...], li_ref[...], ldt_ref[...], br_ref[...], bi_ref[...])
        ar_ref[...] = ar
        ai_ref[...] = ai
        obr_ref[...] = obr
        obi_ref[...] = obi

    sh = jax.ShapeDtypeStruct(lrx.shape, F32)
    return pl.pallas_call(body, out_shape=[sh, sh, sh, sh], name=name)(lrx, lix, ldtx, brt, bit)


def _disc_bwd(lrx, lix, ldtx, brt, bit, dar, dai, dbr, dbi, name):
    nrow = lrx.shape[0] // SSM_H

    def body(lr_ref, li_ref, ldt_ref, br_ref, bi_ref, dar_ref, dai_ref, dbr_ref, dbi_ref,
             glr_ref, gli_ref, gdt_ref, gbr_ref, gbi_ref):
        _, vjp = jax.vjp(_disc_math, lr_ref[...], li_ref[...], ldt_ref[...], br_ref[...], bi_ref[...])
        glr, gli, gdt, gbr, gbi = vjp((dar_ref[...], dai_ref[...], dbr_ref[...], dbi_ref[...]))
        glr_ref[...] = jnp.sum(glr.reshape(nrow, SSM_H, SSM_P), axis=1)
        gli_ref[...] = jnp.sum(gli.reshape(nrow, SSM_H, SSM_P), axis=1)
        gdt_ref[...] = jnp.sum(jnp.sum(gdt.reshape(nrow, SSM_H, SSM_P), axis=1), axis=-1, keepdims=True)
        gbr_ref[...] = gbr
        gbi_ref[...] = gbi

    small = jax.ShapeDtypeStruct((nrow, SSM_P), F32)
    big = jax.ShapeDtypeStruct(lrx.shape, F32)
    return pl.pallas_call(body, out_shape=[small, small, jax.ShapeDtypeStruct((nrow, 1), F32), big, big],
                          name=name)(lrx, lix, ldtx, brt, bit, dar, dai, dbr, dbi)


HS = 1024
LC = 512


def _d0_rows(n):
    row = lax.broadcasted_iota(jnp.int32, (n, 1), 0)
    return jnp.bitwise_and(row, 4) == 0


def _scan_perm(bl):
    n = 2 * bl * ST
    p = np.zeros((n, n), np.float32)
    for s in range(ST):
        for d in range(2):
            for b in range(bl):
                t = s if d == 0 else ST - 1 - s
                p[s * 2 * bl + d * bl + b, d * bl * ST + b * ST + t] = 1.0
    return p


def _scan_maps(lay):
    spc = TB // ST
    nlc = lay.nlb * spc

    def fwd(k):
        return k // spc, k % spc

    def rev(k):
        cpos = nlc - 1 - jnp.maximum(k - spc, 0)
        return jnp.where(k < spc, 0, 1 + cpos // spc), jnp.where(k < spc, spc - 1 - k, cpos % spc)

    return fwd, rev


def _pack_rows(f_ref, r_ref, p_ref, rc):
    st = jnp.concatenate([f_ref[0].reshape(rc // 2, 256), r_ref[0].reshape(rc // 2, 256)], axis=0).astype(MXU_DTYPE)
    return _nn(p_ref[...], st).astype(MXU_DTYPE)


def _ssm_fwd(lay, z, perm, bh, ch, ar8, ai8, name):
    bl = lay.bl
    rc = ST * 2 * bl
    nch = lay.nr * (TB // ST)
    fwd, rev = _scan_maps(lay)
    z4 = z.reshape(lay.nr, bl, TB, z.shape[1])

    def body(uf_ref, ur_ref, p_ref, bh_ref, ch_ref, ar_ref, ai_ref, yf_ref, yr_ref, hst_ref, hs, hc):
        k = pl.program_id(1)

        @pl.when(k == 0)
        def _():
            hc[...] = jnp.zeros_like(hc)

        hst_ref[0] = hc[...]
        d0 = _d0_rows(rc)
        uv = _pack_rows(uf_ref, ur_ref, p_ref, rc)
        zero = jnp.zeros_like(uv)
        hs[...] = _nn(jnp.where(d0, uv, zero), bh_ref[0, 0]) + _nn(jnp.where(d0, zero, uv), bh_ref[1, 0])
        for q in range(HS // LC):
            cr, ci = q * LC, HS + q * LC
            ar = ar_ref[:, cr:cr + LC]
            ai = ai_ref[:, cr:cr + LC]

            def step(s, carry, cr=cr, ci=ci, ar=ar, ai=ai):
                hr, hi = carry
                base = pl.multiple_of(s * 8, 8)
                nr = ar * hr - ai * hi + hs[pl.ds(base, 8), cr:cr + LC]
                ni = ar * hi + ai * hr + hs[pl.ds(base, 8), ci:ci + LC]
                hs[pl.ds(base, 8), cr:cr + LC] = nr
                hs[pl.ds(base, 8), ci:ci + LC] = ni
                return nr, ni

            hr, hi = lax.fori_loop(0, ST, step, (hc[:, cr:cr + LC], hc[:, ci:ci + LC]), unroll=4)
            hc[:, cr:cr + LC] = hr
            hc[:, ci:ci + LC] = hi
        hb = hs[...].astype(MXU_DTYPE)
        yi = jnp.where(d0, _nn(hb, ch_ref[0, 0]), _nn(hb, ch_ref[1, 0]))
        yhi = yi.astype(MXU_DTYPE)
        ylo = (yi - yhi.astype(F32)).astype(MXU_DTYPE)
        yd = _tn(p_ref[...], yhi) + _tn(p_ref[...], ylo)
        yf_ref[0] = yd[:rc // 2].reshape(bl, ST, 256)
        yr_ref[0] = yd[rc // 2:].reshape(bl, ST, 256)

    blk = (1, bl, ST, 256)
    ysh = jax.ShapeDtypeStruct((lay.nr, bl, TB, B_W), F32)
    yf, yr, hst = pl.pallas_call(
        body, grid=(2, nch),
        in_specs=[pl.BlockSpec(blk, lambda f, k: (fwd(k)[0], 0, fwd(k)[1], 2 + f)),
                  pl.BlockSpec(blk, lambda f, k: (rev(k)[0], 0, rev(k)[1], 2 + f)),
                  pl.BlockSpec((rc, rc), lambda f, k: (0, 0)),
                  pl.BlockSpec((2, 1, 256, 2 * HS), lambda f, k: (0, f, 0, 0)),
                  pl.BlockSpec((2, 1, 2 * HS, 256), lambda f, k: (0, f, 0, 0)),
                  pl.BlockSpec((8, HS), lambda f, k: (0, f)), pl.BlockSpec((8, HS), lambda f, k: (0, f))],
        out_specs=[pl.BlockSpec(blk, lambda f, k: (fwd(k)[0], 0, fwd(k)[1], f)),
                   pl.BlockSpec(blk, lambda f, k: (rev(k)[0], 0, rev(k)[1], f)),
                   pl.BlockSpec((1, 8, 2 * HS), lambda f, k: (k, 0, f))],
        out_shape=[ysh, ysh, jax.ShapeDtypeStruct((nch, 8, 4 * HS), F32)],
        scratch_shapes=[pltpu.VMEM((rc, 2 * HS), F32), pltpu.VMEM((8, 2 * HS), F32)],
        compiler_params=_cp(("parallel", "arbitrary")), name=name)(z4, z4, perm, bh, ch, ar8, ai8)
    return yf.reshape(lay.nt, B_W), yr.reshape(lay.nt, B_W), hst


def _ssm_bwd(lay, z, dy, perm, hst, bh, ch, ar8, ai8, name):
    bl = lay.bl
    rc = ST * 2 * bl
    nch = lay.nr * (TB // ST)
    fwd, rev = _scan_maps(lay)
    z4 = z.reshape(lay.nr, bl, TB, z.shape[1])
    dy4 = dy.reshape(lay.nr, bl, TB, B_W)

    def body(uf_ref, ur_ref, dyf_ref, dyr_ref, p_ref, hst_ref, bh_ref, ch_ref, ar_ref, ai_ref,
             duf_ref, dur_ref, dbh_ref, dch_ref, dar_ref, dai_ref, hs, es, ec, accr, acci):
        k = pl.program_id(1)

        @pl.when(k == 0)
        def _():
            ec[...] = jnp.zeros_like(ec)
            accr[...] = jnp.zeros_like(accr)
            acci[...] = jnp.zeros_like(acci)
            dbh_ref[...] = jnp.zeros_like(dbh_ref)
            dch_ref[...] = jnp.zeros_like(dch_ref)

        d0 = _d0_rows(rc)
        uv = _pack_rows(uf_ref, ur_ref, p_ref, rc)
        zero = jnp.zeros_like(uv)
        u0, u1 = jnp.where(d0, uv, zero), jnp.where(d0, zero, uv)
        dyv = _pack_rows(dyf_ref, dyr_ref, p_ref, rc)
        dy0, dy1 = jnp.where(d0, dyv, zero), jnp.where(d0, zero, dyv)

        hs[0:8, :] = hst_ref[0]
        hs[8:, :] = _nn(u0, bh_ref[0, 0]) + _nn(u1, bh_ref[1, 0])
        for q in range(HS // LC):
            cr, ci = q * LC, HS + q * LC
            ar = ar_ref[:, cr:cr + LC]
            ai = ai_ref[:, cr:cr + LC]

            def step(s, carry, cr=cr, ci=ci, ar=ar, ai=ai):
                hr, hi = carry
                base = pl.multiple_of(s * 8 + 8, 8)
                nr = ar * hr - ai * hi + hs[pl.ds(base, 8), cr:cr + LC]
                ni = ar * hi + ai * hr + hs[pl.ds(base, 8), ci:ci + LC]
                hs[pl.ds(base, 8), cr:cr + LC] = nr
                hs[pl.ds(base, 8), ci:ci + LC] = ni
                return nr, ni

            lax.fori_loop(0, ST, step, (hs[0:8, cr:cr + LC], hs[0:8, ci:ci + LC]), unroll=4)

        hb = hs[8:, :].astype(MXU_DTYPE)
        dch_ref[0, 0] += _tn(hb, dy0)
        dch_ref[1, 0] += _tn(hb, dy1)
        es[...] = _nt(dy0, ch_ref[0, 0]) + _nt(dy1, ch_ref[1, 0])

        for q in range(HS // LC):
            cr, ci = q * LC, HS + q * LC
            ar = ar_ref[:, cr:cr + LC]
            ai = ai_ref[:, cr:cr + LC]

            def bstep(i, carry, cr=cr, ci=ci, ar=ar, ai=ai):
                er, ei, sr, si = carry
                base = pl.multiple_of((ST - 1 - i) * 8, 8)
                ner = es[pl.ds(base, 8), cr:cr + LC] + ar * er + ai * ei
                nei = es[pl.ds(base, 8), ci:ci + LC] - ai * er + ar * ei
                es[pl.ds(base, 8), cr:cr + LC] = ner
                es[pl.ds(base, 8), ci:ci + LC] = nei
                hpr = hs[pl.ds(base, 8), cr:cr + LC]
                hpi = hs[pl.ds(base, 8), ci:ci + LC]
                return ner, nei, sr + ner * hpr + nei * hpi, si - ner * hpi + nei * hpr

            er, ei, sr, si = lax.fori_loop(
                0, ST, bstep, (ec[:, cr:cr + LC], ec[:, ci:ci + LC], accr[:, cr:cr + LC], acci[:, cr:cr + LC]), unroll=4)
            ec[:, cr:cr + LC] = er
            ec[:, ci:ci + LC] = ei
            accr[:, cr:cr + LC] = sr
            acci[:, cr:cr + LC] = si

        eb = es[...].astype(MXU_DTYPE)
        dui = jnp.where(d0, _nt(eb, bh_ref[0, 0]), _nt(eb, bh_ref[1, 0])).astype(MXU_DTYPE)
        dud = _tn(p_ref[...], dui)
        duf_ref[0] = dud[:rc // 2].reshape(bl, ST, 256).astype(duf_ref.dtype)
        dur_ref[0] = dud[rc // 2:].reshape(bl, ST, 256).astype(dur_ref.dtype)
        dbh_ref[0, 0] += _tn(u0, eb)
        dbh_ref[1, 0] += _tn(u1, eb)

        @pl.when(k == nch - 1)
        def _():
            for d in range(2):
                dar_ref[d:d + 1, :] = jnp.sum(accr[4 * d:4 * d + 4, :], axis=0, keepdims=True)
                dai_ref[d:d + 1, :] = jnp.sum(acci[4 * d:4 * d + 4, :], axis=0, keepdims=True)

    last = lambda k: nch - 1 - k
    blk = (1, bl, ST, 256)
    fspec = lambda c0: pl.BlockSpec(blk, lambda f, k: (fwd(last(k))[0], 0, fwd(last(k))[1], c0 + f))
    rspec = lambda c0: pl.BlockSpec(blk, lambda f, k: (rev(last(k))[0], 0, rev(last(k))[1], c0 + f))
    dush = jax.ShapeDtypeStruct((lay.nr, bl, TB, B_W), MXU_DTYPE)
    duf, dur, dbh, dch, dar, dai = pl.pallas_call(
        body, grid=(2, nch),
        in_specs=[fspec(2), rspec(2), fspec(0), rspec(0),
                  pl.BlockSpec((rc, rc), lambda f, k: (0, 0)),
                  pl.BlockSpec((1, 8, 2 * HS), lambda f, k: (last(k), 0, f)),
                  pl.BlockSpec((2, 1, 256, 2 * HS), lambda f, k: (0, f, 0, 0)),
                  pl.BlockSpec((2, 1, 2 * HS, 256), lambda f, k: (0, f, 0, 0)),
                  pl.BlockSpec((8, HS), lambda f, k: (0, f)), pl.BlockSpec((8, HS), lambda f, k: (0, f))],
        out_specs=[fspec(0), rspec(0),
                   pl.BlockSpec((2, 1, 256, 2 * HS), lambda f, k: (0, f, 0, 0)),
                   pl.BlockSpec((2, 1, 2 * HS, 256), lambda f, k: (0, f, 0, 0)),
                   pl.BlockSpec((2, HS), lambda f, k: (0, f)), pl.BlockSpec((2, HS), lambda f, k: (0, f))],
        out_shape=[dush, dush, jax.ShapeDtypeStruct((2, 2, 256, 2 * HS), F32),
                   jax.ShapeDtypeStruct((2, 2, 2 * HS, 256), F32), jax.ShapeDtypeStruct((2, 2 * HS), F32),
                   jax.ShapeDtypeStruct((2, 2 * HS), F32)],
        scratch_shapes=[pltpu.VMEM((rc + 8, 2 * HS), F32), pltpu.VMEM((rc, 2 * HS), F32), pltpu.VMEM((8, 2 * HS), F32),
                        pltpu.VMEM((8, HS), F32), pltpu.VMEM((8, HS), F32)],
        compiler_params=_cp(("parallel", "arbitrary")), name=name)(z4, z4, dy4, dy4, perm, hst, bh, ch, ar8, ai8)
    return duf.reshape(lay.nt, B_W), dur.reshape(lay.nt, B_W), dbh, dch, dar, dai


def _glu_fwd(lay, z, yf, yr, dvec, wglu, bglu, name):
    def body(u_ref, yf_ref, yr_ref, d_ref, w_ref, b_ref, o_ref, y_ref):
        y = yf_ref[...] + yr_ref[...] + d_ref[...] * u_ref[...]
        y_ref[...] = y
        g = _gelu(y)
        pre = _nn(g.astype(MXU_DTYPE), w_ref[...]) + b_ref[...]
        o_ref[...] = (g * _sigmoid(pre)).astype(o_ref.dtype)

    tok = pl.BlockSpec((TB, B_W), lambda j: (j, 0))
    vec = pl.BlockSpec((1, B_W), lambda j: (0, 0))
    return pl.pallas_call(
        body, grid=(lay.nb,),
        in_specs=[pl.BlockSpec((TB, B_W), lambda j: (j, 1)), tok, tok, vec, pl.BlockSpec((B_W, B_W), lambda j: (0, 0)), vec],
        out_specs=[tok, tok],
        out_shape=[jax.ShapeDtypeStruct((lay.nt, B_W), MXU_DTYPE), jax.ShapeDtypeStruct((lay.nt, B_W), F32)],
        compiler_params=_cp(("parallel",)), name=name)(z, yf, yr, dvec, wglu, bglu)


def _glu_bwd(lay, z, y, ds, dvec, wglu, bglu, name):
    def body(u_ref, y_ref, ds_ref, d_ref, w_ref, b_ref, dy_ref, dud_ref, dw_ref, db_ref, dd_ref):
        j = pl.program_id(0)

        @pl.when(j == 0)
        def _():
            dw_ref[...] = jnp.zeros_like(dw_ref)
            db_ref[...] = jnp.zeros_like(db_ref)
            dd_ref[...] = jnp.zeros_like(dd_ref)

        yv = y_ref[...]
        g = _gelu(yv)
        gb = g.astype(MXU_DTYPE)
        sg = _sigmoid(_nn(gb, w_ref[...]) + b_ref[...])
        dsv = ds_ref[...].astype(F32)
        dpre = dsv * g * sg * (1.0 - sg)
        dpre_b = dpre.astype(MXU_DTYPE)
        dg = dsv * sg + _nt(dpre_b, w_ref[...])
        dw_ref[...] += _tn(gb, dpre_b)
        db_ref[...] += jnp.sum(dpre, axis=0, keepdims=True)
        dy = dg * _gelu_grad(yv)
        dy_ref[...] = dy.astype(dy_ref.dtype)
        dd_ref[...] += jnp.sum(dy * u_ref[...], axis=0, keepdims=True)
        dud_ref[...] = (dy * d_ref[...]).astype(dud_ref.dtype)

    tok = pl.BlockSpec((TB, B_W), lambda j: (j, 0))
    vec = pl.BlockSpec((1, B_W), lambda j: (0, 0))
    mat = pl.BlockSpec((B_W, B_W), lambda j: (0, 0))
    vsh = jax.ShapeDtypeStruct((1, B_W), F32)
    return pl.pallas_call(
        body, grid=(lay.nb,),
        in_specs=[pl.BlockSpec((TB, B_W), lambda j: (j, 1)), tok, tok, vec, mat, vec],
        out_specs=[tok, tok, mat, vec, vec],
        out_shape=[jax.ShapeDtypeStruct((lay.nt, B_W), MXU_DTYPE), jax.ShapeDtypeStruct((lay.nt, B_W), F32),
                   jax.ShapeDtypeStruct((B_W, B_W), F32), vsh, vsh],
        compiler_params=_cp(("arbitrary",)), name=name)(z, y, ds, dvec, wglu, bglu)


def _dz_assemble(lay, dz_a, duf, dur, dud, dz_p, name):
    def body(a_ref, f_ref, r_ref, d_ref, p_ref, o_ref):
        o_ref[:, :2 * A_W] = a_ref[...].astype(o_ref.dtype)
        o_ref[:, 2 * A_W:2 * A_W + B_W] = (f_ref[...].astype(F32) + r_ref[...].astype(F32) + d_ref[...]).astype(o_ref.dtype)
        o_ref[:, 2 * A_W + B_W:] = p_ref[...].astype(o_ref.dtype)

    spec = lambda w: pl.BlockSpec((TB, w), lambda j: (j, 0))
    return pl.pallas_call(
        body, grid=(lay.nb,), in_specs=[spec(2 * A_W), spec(B_W), spec(B_W), spec(B_W), spec(C_W)],
        out_specs=spec(D_IN), out_shape=jax.ShapeDtypeStruct((lay.nt, D_IN), MXU_DTYPE),
        compiler_params=_cp(("parallel",)), name=name)(dz_a, duf, dur, dud, dz_p)


def _expand_rows(a):
    return jnp.broadcast_to(a[:, :, None, :], (2, SSM_G, SSM_H, SSM_P)).reshape(-1, SSM_P)


def _ssm_params(lam_re, lam_im, log_dt, b_re, b_im, c_re, c_im, name):
    lrx, lix = _expand_rows(lam_re), _expand_rows(lam_im)
    ldtx = _expand_rows(jnp.broadcast_to(log_dt[:, :, None], (2, SSM_G, SSM_P)))
    brt = jnp.transpose(b_re, (0, 1, 3, 2)).reshape(-1, SSM_P)
    bit = jnp.transpose(b_im, (0, 1, 3, 2)).reshape(-1, SSM_P)
    arx, aix, bbr, bbi = _disc_fwd(lrx, lix, ldtx, brt, bit, name)
    ar = arx.reshape(2, SSM_G, SSM_H, SSM_P)[:, :, 0].reshape(2, SSM_G * SSM_P)
    ai = aix.reshape(2, SSM_G, SSM_H, SSM_P)[:, :, 0].reshape(2, SSM_G * SSM_P)
    eye = jnp.eye(16, dtype=F32)

    def bmat(bt):
        t = bt.reshape(2, 2, 16, SSM_H, SSM_P)
        return jnp.einsum('dfghp,gk->dfghkp', t, eye).reshape(2, 2, 256, HS)

    bh = jnp.concatenate([bmat(bbr), bmat(bbi)], axis=-1).astype(MXU_DTYPE)

    def cmat(c):
        t = c.reshape(2, 2, 16, SSM_H, SSM_P)
        return jnp.einsum('dfghp,gk->dfgpkh', t, eye).reshape(2, 2, HS, 256)

    ch = jnp.concatenate([cmat(c_re), -cmat(c_im)], axis=2).astype(MXU_DTYPE)

    def rows8(a):
        return jnp.repeat(a, 4, axis=0)

    return dict(lrx=lrx, lix=lix, ldtx=ldtx, brt=brt, bit=bit, bh=bh, ch=ch, ar8=rows8(ar), ai8=rows8(ai))


def _ssm_param_grads(sp, dbh, dch, dar, dai, name):
    def bdiag(m):
        t = m.reshape(2, 2, 16, SSM_H, 16, SSM_P)
        return jnp.einsum('dfghgp->dfghp', t).reshape(-1, SSM_P)

    dbr, dbi = bdiag(dbh[..., :HS]), bdiag(dbh[..., HS:])

    def cdiag(m):
        t = m.reshape(2, 2, 16, SSM_P, 16, SSM_H)
        return jnp.einsum('dfgpgh->dfghp', t).reshape(2, SSM_G, SSM_H, SSM_P)

    dc_re, dc_im = cdiag(dch[:, :, :HS]), -cdiag(dch[:, :, HS:])

    def hrow(a):
        t = a.reshape(2, SSM_G, 1, SSM_P)
        return jnp.concatenate([t, jnp.zeros((2, SSM_G, SSM_H - 1, SSM_P), F32)], axis=2).reshape(-1, SSM_P)

    glr, gli, gdt, gbr, gbi = _disc_bwd(sp["lrx"], sp["lix"], sp["ldtx"], sp["brt"], sp["bit"],
                                        hrow(dar), hrow(dai), dbr, dbi, name)
    to_b = lambda g: jnp.transpose(g.reshape(2, SSM_G, SSM_H, SSM_P), (0, 1, 3, 2))
    return dict(ssm_lam_re=glr.reshape(2, SSM_G, SSM_P), ssm_lam_im=gli.reshape(2, SSM_G, SSM_P),
                ssm_log_dt=gdt.reshape(2, SSM_G), ssm_b_re=to_b(gbr), ssm_b_im=to_b(gbi),
                ssm_c_re=dc_re, ssm_c_im=dc_im)


def _layer_consts(p):
    c = {}
    c["ws"] = p["sgu_w"].astype(MXU_DTYPE)
    c["wst"] = jnp.transpose(p["sgu_w"], (0, 2, 1)).astype(MXU_DTYPE)
    c["gbias"] = jnp.repeat(p["sgu_b"].T, 64, axis=1)
    pw = jnp.zeros((C_W, C_W), F32)
    for i in range(4):
        pw = pw.at[64 * i:64 * i + 64, 64 * i:64 * i + 64].set(p["pool_w"][i])
    c["pw"] = pw.astype(MXU_DTYPE)
    c["pscale"] = p["pool_scale"].reshape(1, C_W)
    c["dvec"] = p["ssm_d"].reshape(1, B_W)
    c["bglu"] = p["glu_b"].reshape(1, B_W)
    return c


def _layer_fwd(lay, i, x, modarr, p, w, cst, sp, bands, inv, perm):
    n = f"l{i}_"
    res = {"x0": x}
    h = _normmod_fwd(lay, x, p["norm_mix_pre"].reshape(1, D), modarr, 0, 1, n + "nm1")
    z = _mm([(h, w["win_t"])], True, F32, n + "win")
    a = _gate_fwd(lay, z, cst["ws"], cst["gbias"], n + "gate")
    yf, yr, hst = _ssm_fwd(lay, z, perm, sp["bh"], sp["ch"], sp["ar8"], sp["ai8"], n + "ssm")
    s, y = _glu_fwd(lay, z, yf, yr, cst["dvec"], w["wglu"], cst["bglu"], n + "glu")
    c = _pool_fwd(lay, z, bands, inv, cst["pw"], cst["pscale"], n + "pool")
    mcat = jnp.concatenate([s, a, c], axis=1)
    m = _mm([(mcat, w["wout"])], False, F32, n + "wout")
    x1 = _resnorm_fwd(lay, x, m, p["norm_mix_post"].reshape(1, D), modarr, 2, n + "rn1")
    h2 = _normmod_fwd(lay, x1, p["norm_ffn_pre"].reshape(1, D), modarr, 3, 4, n + "nm2")
    g, u, act = _ffn_up(h2, w["wg_t"], w["wu_t"], n + "ffn_up")
    f = _mm([(act, w["wd"])], False, F32, n + "ffn_down")
    x2 = _resnorm_fwd(lay, x1, f, p["norm_ffn_post"].reshape(1, D), modarr, 5, n + "rn2")
    res.update(h=h, z=z, hst=hst, y=y, mcat=mcat, m=m, x1=x1, h2=h2, g=g, u=u, act=act, f=f)
    return x2, res


def _layer_bwd(lay, i, dx2, modarr, p, w, cst, sp, bands, inv, perm, res):
    n = f"l{i}b_"
    big, small = {}, {}
    df, dg2, gpost2 = _resnorm_bwd(lay, dx2, res["f"], p["norm_ffn_post"].reshape(1, D), modarr, 5, n + "rn2")
    big["wd"] = _mm_tn(res["act"], df, MXU_DTYPE, n + "dwd")
    dg, du = _ffn_down_bwd(df, w["wd"], res["g"], res["u"], n + "ffn_down")
    dh2 = _mm([(dg, w["wg_t"]), (du, w["wu_t"])], False, F32, n + "dh2")
    big["wg_t"] = _mm_tn(dg, res["h2"], MXU_DTYPE, n + "dwg")
    big["wu_t"] = _mm_tn(du, res["h2"], MXU_DTYPE, n + "dwu")
    dx1, dsh2, dsc2, gpre2 = _normmod_bwd(lay, res["x1"], dh2, dx2, p["norm_ffn_pre"].reshape(1, D), modarr, 4, n + "nm2")
    dm, dg1, gpost1 = _resnorm_bwd(lay, dx1, res["m"], p["norm_mix_post"].reshape(1, D), modarr, 2, n + "rn1")
    big["wout"] = _mm_tn(res["mcat"], dm, MXU_DTYPE, n + "dwout")
    dmcat = _mm([(dm, w["wout"])], True, F32, n + "dmcat")
    z = res["z"]
    dz_a, dws, dgb = _gate_bwd(lay, z, dmcat, cst["ws"], cst["wst"], cst["gbias"], n + "gate")
    dy, dud, dwglu, dbglu, ddvec = _glu_bwd(lay, z, res["y"], dmcat, cst["dvec"], w["wglu"], cst["bglu"], n + "glu")
    duf, dur, dbh, dch, dar, dai = _ssm_bwd(lay, z, dy, perm, res["hst"], sp["bh"], sp["ch"], sp["ar8"], sp["ai8"],
                                            n + "ssm")
    dz_p, dpw, dpsc = _pool_bwd(lay, z, dmcat, bands, inv, cst["pw"], cst["pscale"], n + "pool")
    dz = _dz_assemble(lay, dz_a, duf, dur, dud, dz_p, n + "dz")
    big["wglu"] = dwglu.astype(MXU_DTYPE)
    big["win_t"] = _mm_tn(dz, res["h"], MXU_DTYPE, n + "dwin")
    dh = _mm([(dz, w["win_t"])], False, F32, n + "dh")
    dx, dsh1, dsc1, gpre1 = _normmod_bwd(lay, res["x0"], dh, dx1, p["norm_mix_pre"].reshape(1, D), modarr, 1, n + "nm1",
                                         latent_only=(i == 0))

    small.update(norm_mix_pre=gpre1[0], norm_mix_post=gpost1[0], norm_ffn_pre=gpre2[0], norm_ffn_post=gpost2[0])
    small["sgu_w"] = dws
    small["sgu_b"] = jnp.sum(dgb.reshape(CHUNK, 4, 64), axis=-1).T
    small.update(_ssm_param_grads(sp, dbh, dch, dar, dai, n + "disc"))
    small["ssm_d"] = ddvec.reshape(SSM_G, SSM_H)
    small["glu_b"] = dbglu[0]
    small["pool_w"] = jnp.stack([dpw[64 * k:64 * k + 64, 64 * k:64 * k + 64] for k in range(4)])
    small["pool_scale"] = dpsc[0]
    dmod = jnp.concatenate([dsh1, dsc1, dg1, dsh2, dsc2, dg2], axis=1)[:lay.bl + 1]
    dmod = jnp.concatenate([dmod, jnp.zeros((8 - lay.bl - 1, 6, D), F32)], axis=0)
    return dx, big, small, dmod


SMALL_NAMES = ["norm_mix_pre", "norm_mix_post", "norm_ffn_pre", "norm_ffn_post", "sgu_w", "sgu_b", "ssm_lam_re",
               "ssm_lam_im", "ssm_log_dt", "ssm_b_re", "ssm_b_im", "ssm_c_re", "ssm_c_im", "ssm_d", "glu_b", "pool_w",
               "pool_scale"]
BIG_NAMES = ["win_t", "wout", "wglu", "wg_t", "wu_t", "wd"]


def _sincos_2d(rows, cols, dim):
    quarter = dim // 4
    omega = 1.0 / (10000.0 ** (jnp.arange(quarter, dtype=F32) / quarter))
    r = jnp.arange(rows, dtype=F32)[:, None] * omega
    cc = jnp.arange(cols, dtype=F32)[:, None] * omega
    er = jnp.concatenate([jnp.sin(r), jnp.cos(r)], axis=-1)
    ec = jnp.concatenate([jnp.sin(cc), jnp.cos(cc)], axis=-1)
    pe = jnp.concatenate([jnp.broadcast_to(er[:, None, :], (rows, cols, dim // 2)),
                          jnp.broadcast_to(ec[None, :, :], (rows, cols, dim // 2))], axis=-1)
    return pe.reshape(rows * cols, dim)


def _core(x, ctx, target, mods_local, params, weights):
    bl, lat, _ = x.shape
    assert bl == 4 and lat % TB == 0, "the scan fills 8 sublanes with 2 directions x 4 sequences"
    lay = _Layout(bl, lat)
    pe = _sincos_2d(lat // GRID_W, GRID_W, D)
    xt = _embed(lay, x.reshape(bl * lat, D), ctx.reshape(bl * CTX, D), pe)
    bands_np, inv_np = _band_constants()
    bands, inv = jnp.asarray(bands_np, MXU_DTYPE), jnp.asarray(inv_np, F32)
    perm = jnp.asarray(_scan_perm(bl), MXU_DTYPE)
    rows = lay.modrows_static()
    modarrs, csts, sps, ress, wls = [], [], [], [], []
    for i in range(2):
        modarrs.append(mods_local[i][rows].reshape(lay.nb * 6, 1, D))
        csts.append(_layer_consts(params[i]))
        p = params[i]
        sps.append(_ssm_params(p["ssm_lam_re"], p["ssm_lam_im"], p["ssm_log_dt"], p["ssm_b_re"], p["ssm_b_im"],
                               p["ssm_c_re"], p["ssm_c_im"], f"l{i}_disc"))
        w = dict(weights[i])
        w["wout"] = w["wout"].reshape(4, D // 4, D)[np.array(WOUT_PERM)].reshape(D, D)
        wls.append(w)
    for i in range(2):
        xt, res = _layer_fwd(lay, i, xt, modarrs[i], params[i], wls[i], csts[i], sps[i], bands, inv, perm)
        ress.append(res)
    dx, lossv = _loss_bwd(lay, xt, target.reshape(bl * lat, D))
    bigs, smalls, dmods = [None, None], [None, None], [None, None]
    for i in (1, 0):
        dx, bigs[i], smalls[i], dmods[i] = _layer_bwd(lay, i, dx, modarrs[i], params[i], wls[i], csts[i], sps[i],
                                                       bands, inv, perm, ress[i])
        bigs[i]["wout"] = bigs[i]["wout"].reshape(4, D // 4, D)[np.array(WOUT_INV)].reshape(D, D)
    return lossv[0, 0], dx.reshape(bl, lat, D), bigs, smalls, dmods


def _my_index():
    return 4 * lax.axis_index("x") + 2 * lax.axis_index("y") + lax.axis_index("c")


def _peer(k):
    x, y, c = lax.axis_index("x"), lax.axis_index("y"), lax.axis_index("c")
    kx, ky, kc = (k >> 2) & 1, (k >> 1) & 1, k & 1
    px = 1 - x if kx else x
    py = 1 - y if ky else y
    pc = 1 - c if kc else c
    return (px, py, pc), 4 * px + 2 * py + pc


def _comm(items, name):
    n = len(items)
    ncopies = sum(len(it[2]) for it in items)

    def slot_of(idx, slots):
        return idx if slots == 8 else (idx // 2 if slots == 4 else idx % 2)

    def body(*refs):
        ins, outs = refs[:n], refs[n:2 * n]
        send_sems, recv_sems, local_sems = refs[2 * n:]
        me = _my_index()
        local, sends, recvs = [], [], []
        q = 0
        for t, (arr, mode, ks, slots) in enumerate(items):
            src_own = ins[t] if mode == "gather" else ins[t].at[me]
            cp = pltpu.make_async_copy(src_own, outs[t].at[slot_of(me, slots)], local_sems.at[t])
            cp.start()
            local.append(cp)
            for k in ks:
                peer, pidx = _peer(k)
                src = ins[t] if mode == "gather" else ins[t].at[pidx]
                sends.append(pltpu.make_async_remote_copy(
                    src_ref=src, dst_ref=outs[t].at[slot_of(me, slots)], send_sem=send_sems.at[q], recv_sem=recv_sems.at[q],
                    device_id=peer, device_id_type=pl.DeviceIdType.MESH))
                recvs.append(pltpu.make_async_remote_copy(
                    src_ref=src, dst_ref=outs[t].at[slot_of(pidx, slots)], send_sem=send_sems.at[q], recv_sem=recv_sems.at[q],
                    device_id=peer, device_id_type=pl.DeviceIdType.MESH))
                q += 1
        for cp in sends:
            cp.start()
        for cp in recvs:
            cp.wait_recv()
        for cp in sends:
            cp.wait_send()
        for cp in local:
            cp.wait()

    out_shape = []
    for arr, mode, ks, slots in items:
        shp = (slots,) + tuple(arr.shape) if mode == "gather" else tuple(arr.shape)
        out_shape.append(jax.ShapeDtypeStruct(shp, arr.dtype))
    anyspec = pl.BlockSpec(memory_space=pl.ANY)
    return pl.pallas_call(
        body, in_specs=[anyspec] * n, out_specs=[anyspec] * n, out_shape=out_shape,
        scratch_shapes=[pltpu.SemaphoreType.DMA((ncopies,)), pltpu.SemaphoreType.DMA((ncopies,)),
                        pltpu.SemaphoreType.DMA((n,))],
        compiler_params=pltpu.CompilerParams(has_side_effects=True), name=name)(*[it[0] for it in items])


ALL7 = (1, 2, 3, 4, 5, 6, 7)
CHIPS3 = (2, 4, 6)


def _sum8(parts, name):
    def one(a, nm):
        _, r, c = a.shape
        tr = r if r <= 512 else _pick_rows(r)

        def body(a_ref, o_ref):
            acc = a_ref[0].astype(F32)
            for q in range(1, a_ref.shape[0]):
                acc = acc + a_ref[q].astype(F32)
            o_ref[...] = acc

        return pl.pallas_call(
            body, grid=(r // tr,), in_specs=[pl.BlockSpec((a.shape[0], tr, c), lambda i: (0, i, 0))],
            out_specs=pl.BlockSpec((tr, c), lambda i: (i, 0)), out_shape=jax.ShapeDtypeStruct((r, c), F32),
            compiler_params=_cp(("parallel",)), name=nm)(a)

    return [one(a, f"{name}{i}") for i, a in enumerate(parts)]


def _pick_rows(r, cap=512):
    for t in (512, 352, 256, 176, 128, 64, 32, 16, 8):
        if r % t == 0 and t <= cap:
            return t
    return r


def _adam(w, g, m, v, name):
    shape = w.shape
    nel = int(np.prod(shape))
    if len(shape) >= 2 and shape[-1] >= 128:
        lanes = shape[-1]
    else:
        lanes = 512 if nel % 512 == 0 else 128
    r = nel // lanes
    tr = r if r * lanes <= 384 * 1024 else _pick_rows(r, 384 * 1024 // lanes)
    c1 = 1.0 / (1.0 - ADAM_B1 ** ADAM_STEP)
    c2 = 1.0 / (1.0 - ADAM_B2 ** ADAM_STEP)

    def body(w_ref, g_ref, m_ref, v_ref, d_ref, nm_ref, nv_ref):
        gv = g_ref[...]
        nm = ADAM_B1 * m_ref[...] + (1.0 - ADAM_B1) * gv
        nv = ADAM_B2 * v_ref[...] + (1.0 - ADAM_B2) * (gv * gv)
        d_ref[...] = -ADAM_LR * ((nm * c1) / (jnp.sqrt(nv * c2) + ADAM_EPS) + ADAM_WD * w_ref[...])
        nm_ref[...] = nm
        nv_ref[...] = nv

    spec = pl.BlockSpec((tr, lanes), lambda i: (i, 0))
    sh = jax.ShapeDtypeStruct((r, lanes), F32)
    outs = pl.pallas_call(
        body, grid=(r // tr,), in_specs=[spec] * 4, out_specs=[spec] * 3, out_shape=[sh] * 3,
        compiler_params=_cp(("parallel",)), name=name)(*[a.reshape(r, lanes) for a in (w, g, m, v)])
    return [o.reshape(shape) for o in outs]


def _silu(x):
    return x * _sigmoid(x)


def _mod_fwd(c_rows, w_mod, b_cols, name):
    def body(c_ref, w_ref, b_ref, o_ref):
        s = _silu(c_ref[...])
        for l in range(2):
            o_ref[l] = jnp.dot(s, w_ref[l], preferred_element_type=F32, precision=lax.Precision.HIGHEST) + b_ref[l]

    nc = w_mod.shape[2]
    return pl.pallas_call(body, out_shape=jax.ShapeDtypeStruct((2, c_rows.shape[0], nc), F32),
                          compiler_params=_cp(None), name=name)(c_rows, w_mod, b_cols)


def _mod_bwd(c_rows, w_mod, dlat, dctx8, name):
    nrow = c_rows.shape[0]
    nb = nrow - 8

    def body(c_ref, w_ref, dl_ref, dc_ref, gw_ref, gc_ref):
        s = _silu(c_ref[...])
        ctx_row = lax.broadcasted_iota(jnp.int32, (nrow, 1), 0) == nb
        gc = jnp.zeros((1, D), F32)
        for l in range(2):
            dctx = dc_ref[0, l]
            for q in range(1, 8):
                dctx = dctx + dc_ref[q, l]
            dm = dl_ref[l] + jnp.where(ctx_row, dctx, 0.0)
            gw_ref[l] = lax.dot_general(s, dm, (((0,), (0,)), ((), ())), preferred_element_type=F32,
                                        precision=lax.Precision.HIGHEST)
            gc = gc + lax.dot_general(dctx, w_ref[l], (((1,), (1,)), ((), ())), preferred_element_type=F32,
                                      precision=lax.Precision.HIGHEST)
        gc_ref[...] = gc

    nc = w_mod.shape[2]
    return pl.pallas_call(body, out_shape=[jax.ShapeDtypeStruct((2, D, nc), F32), jax.ShapeDtypeStruct((1, D), F32)],
                          compiler_params=_cp(None), name=name)(c_rows, w_mod, dlat, dctx8)


def _bmod_cctx(dmod_all, gc4, c_ctx, name):
    def body(dm_ref, gc_ref, cc_ref, gb_ref, gcc_ref):
        for l in range(2):
            acc = jnp.sum(dm_ref[0, l], axis=0, keepdims=True)
            for q in range(1, 8):
                acc = acc + jnp.sum(dm_ref[q, l], axis=0, keepdims=True)
            gb_ref[l:l + 1, :] = acc
        g = gc_ref[0] + gc_ref[1] + gc_ref[2] + gc_ref[3]
        cv = cc_ref[...]
        sg = _sigmoid(cv)
        gcc_ref[...] = g * (sg * (1.0 + cv * (1.0 - sg)))

    return pl.pallas_call(body, out_shape=[jax.ShapeDtypeStruct((2, 6 * D), F32), jax.ShapeDtypeStruct((1, D), F32)],
                          compiler_params=_cp(None), name=name)(dmod_all, gc4, c_ctx)


def kernel(x, c, ctx, c_ctx, w_mod, b_mod, norm_mix_pre, norm_mix_post, norm_ffn_pre, norm_ffn_post, w_in, w_out, sgu_w, sgu_b, ssm_lam_re, ssm_lam_im, ssm_log_dt, ssm_b_re, ssm_b_im, ssm_c_re, ssm_c_im, ssm_d, glu_w, glu_b, pool_w, pool_scale, ffn_w_gate, ffn_w_up, ffn_w_down, loss_target, m_c_ctx, m_w_mod, m_b_mod, m_norm_mix_pre, m_norm_mix_post, m_norm_ffn_pre, m_norm_ffn_post, m_w_in, m_w_out, m_sgu_w, m_sgu_b, m_ssm_lam_re, m_ssm_lam_im, m_ssm_log_dt, m_ssm_b_re, m_ssm_b_im, m_ssm_c_re, m_ssm_c_im, m_ssm_d, m_glu_w, m_glu_b, m_pool_w, m_pool_scale, m_ffn_w_gate, m_ffn_w_up, m_ffn_w_down, v_c_ctx, v_w_mod, v_b_mod, v_norm_mix_pre, v_norm_mix_post, v_norm_ffn_pre, v_norm_ffn_post, v_w_in, v_w_out, v_sgu_w, v_sgu_b, v_ssm_lam_re, v_ssm_lam_im, v_ssm_log_dt, v_ssm_b_re, v_ssm_b_im, v_ssm_c_re, v_ssm_c_im, v_ssm_d, v_glu_w, v_glu_b, v_pool_w, v_pool_scale, v_ffn_w_gate, v_ffn_w_up, v_ffn_w_down):
    wts = dict(c_ctx=c_ctx, w_mod=w_mod, b_mod=b_mod, norm_mix_pre=norm_mix_pre, norm_mix_post=norm_mix_post,
               norm_ffn_pre=norm_ffn_pre, norm_ffn_post=norm_ffn_post, w_in=w_in, w_out=w_out, sgu_w=sgu_w, sgu_b=sgu_b,
               ssm_lam_re=ssm_lam_re, ssm_lam_im=ssm_lam_im, ssm_log_dt=ssm_log_dt, ssm_b_re=ssm_b_re, ssm_b_im=ssm_b_im,
               ssm_c_re=ssm_c_re, ssm_c_im=ssm_c_im, ssm_d=ssm_d, glu_w=glu_w, glu_b=glu_b, pool_w=pool_w,
               pool_scale=pool_scale, ffn_w_gate=ffn_w_gate, ffn_w_up=ffn_w_up, ffn_w_down=ffn_w_down)
    ms = dict(c_ctx=m_c_ctx, w_mod=m_w_mod, b_mod=m_b_mod, norm_mix_pre=m_norm_mix_pre, norm_mix_post=m_norm_mix_post,
              norm_ffn_pre=m_norm_ffn_pre, norm_ffn_post=m_norm_ffn_post, w_in=m_w_in, w_out=m_w_out, sgu_w=m_sgu_w,
              sgu_b=m_sgu_b, ssm_lam_re=m_ssm_lam_re, ssm_lam_im=m_ssm_lam_im, ssm_log_dt=m_ssm_log_dt,
              ssm_b_re=m_ssm_b_re, ssm_b_im=m_ssm_b_im, ssm_c_re=m_ssm_c_re, ssm_c_im=m_ssm_c_im, ssm_d=m_ssm_d,
              glu_w=m_glu_w, glu_b=m_glu_b, pool_w=m_pool_w, pool_scale=m_pool_scale, ffn_w_gate=m_ffn_w_gate,
              ffn_w_up=m_ffn_w_up, ffn_w_down=m_ffn_w_down)
    vs = dict(c_ctx=v_c_ctx, w_mod=v_w_mod, b_mod=v_b_mod, norm_mix_pre=v_norm_mix_pre, norm_mix_post=v_norm_mix_post,
              norm_ffn_pre=v_norm_ffn_pre, norm_ffn_post=v_norm_ffn_post, w_in=v_w_in, w_out=v_w_out, sgu_w=v_sgu_w,
              sgu_b=v_sgu_b, ssm_lam_re=v_ssm_lam_re, ssm_lam_im=v_ssm_lam_im, ssm_log_dt=v_ssm_log_dt,
              ssm_b_re=v_ssm_b_re, ssm_b_im=v_ssm_b_im, ssm_c_re=v_ssm_c_re, ssm_c_im=v_ssm_c_im, ssm_d=v_ssm_d,
              glu_w=v_glu_w, glu_b=v_glu_b, pool_w=v_pool_w, pool_scale=v_pool_scale, ffn_w_gate=v_ffn_w_gate,
              ffn_w_up=v_ffn_w_up, ffn_w_down=v_ffn_w_down)
    order = list(wts.keys())
    bl = x.shape[0]
    nseq = bl * N_DEV
    me = _my_index()
    chip = me // 2
    ncol = w_mod.shape[2]

    (c_all,) = _comm([(c, "gather", ALL7, 8)], "ag_c")
    nrow = nseq + 8
    c_rows = jnp.concatenate([c_all.reshape(nseq, D), c_ctx[None], jnp.zeros((7, D), F32)], axis=0)
    b_cols = lax.dynamic_slice_in_dim(b_mod, chip * ncol, ncol, axis=1)[:, None, :]
    mod_cols = _mod_fwd(c_rows, w_mod, b_cols, "mod_fwd")
    (mod4,) = _comm([(mod_cols, "gather", CHIPS3, 4)], "ag_mod")
    mods = jnp.transpose(mod4, (1, 2, 0, 3)).reshape(2, nrow, 6 * D)
    mods_local = jnp.concatenate([lax.dynamic_slice_in_dim(mods, me * bl, bl, axis=1), mods[:, nseq:nseq + 1],
                                  jnp.zeros((2, 8 - bl - 1, 6 * D), F32)], axis=1)

    shards = []
    for i in range(2):
        shards += [w_in[i].T, w_out[i], glu_w[i], ffn_w_gate[i].T, ffn_w_up[i].T, ffn_w_down[i]]
    gathered = _comm([(s.astype(MXU_DTYPE), "gather", CHIPS3, 4) for s in shards], "ag_weights")
    weights = []
    for i in range(2):
        weights.append({nme: g.reshape(-1, g.shape[-1]) for nme, g in zip(BIG_NAMES, gathered[6 * i:6 * i + 6])})
    params = [{k: wts[k][i] for k in SMALL_NAMES} for i in range(2)]

    loss_part, grad_x, bigs, smalls, dmods = _core(x, ctx, loss_target, mods_local, params, weights)
    loss = lax.psum(loss_part, ("x", "y", "c"))

    dmod_local = jnp.stack([dmods[i].reshape(8, 6 * D) for i in range(2)])
    (dmod_all,) = _comm([(dmod_local, "gather", ALL7, 8)], "ag_dmod")
    dcols = lax.dynamic_slice_in_dim(dmod_all, chip * ncol, ncol, axis=3)
    dlat = jnp.transpose(dcols[:, :, :bl], (1, 0, 2, 3)).reshape(2, nseq, ncol)
    dlat = jnp.concatenate([dlat, jnp.zeros((2, 8, ncol), F32)], axis=1)
    dctx8 = dcols[:, :, bl:bl + 1]
    g_w_mod, gc_part = _mod_bwd(c_rows, w_mod, dlat, dctx8, "mod_bwd")
    (gc4,) = _comm([(gc_part, "gather", CHIPS3, 4)], "ag_cctx")
    g_b_mod, g_c_ctx = _bmod_cctx(dmod_all, gc4, c_ctx[None], "bmod_cctx")

    small_flat = jnp.concatenate([jnp.stack([smalls[i][k] for i in range(2)]).reshape(-1) for k in SMALL_NAMES])
    npad = (-small_flat.shape[0]) % (8 * 1024)
    small_flat = jnp.concatenate([small_flat, jnp.zeros((npad,), F32)])
    a2a_items = []
    for i in range(2):
        for k in BIG_NAMES:
            g = bigs[i][k]
            a2a_items.append((g.reshape(8, g.shape[0] // 8, g.shape[1]), "a2a", ALL7, 8))
    a2a_items.append((small_flat.reshape(8, -1, 1024), "a2a", ALL7, 8))
    pieces = _comm(a2a_items, "a2a_grads")
    sums = _sum8(pieces, "gsum")
    items = [(s, "gather", (1,), 2) for s in sums[:-1]] + [(sums[-1], "gather", ALL7, 8)]
    fin = _comm(items, "ag_grads")
    big_g = [{k: fin[6 * i + j].reshape(-1, fin[6 * i + j].shape[-1]) for j, k in enumerate(BIG_NAMES)} for i in range(2)]
    small_red = fin[-1].reshape(-1)

    grads = {}
    off = 0
    for k in SMALL_NAMES:
        shp = wts[k].shape
        nel = int(np.prod(shp))
        grads[k] = small_red[off:off + nel].reshape(shp)
        off += nel
    grads["c_ctx"] = g_c_ctx[0]
    grads["w_mod"] = g_w_mod
    grads["b_mod"] = g_b_mod
    grads["w_in"] = jnp.stack([big_g[i]["win_t"].T for i in range(2)])
    grads["w_out"] = jnp.stack([big_g[i]["wout"] for i in range(2)])
    grads["glu_w"] = jnp.stack([big_g[i]["wglu"] for i in range(2)])
    grads["ffn_w_gate"] = jnp.stack([big_g[i]["wg_t"].T for i in range(2)])
    grads["ffn_w_up"] = jnp.stack([big_g[i]["wu_t"].T for i in range(2)])
    grads["ffn_w_down"] = jnp.stack([big_g[i]["wd"] for i in range(2)])

    deltas, new_m, new_v = {}, {}, {}
    for k in order:
        deltas[k], new_m[k], new_v[k] = _adam(wts[k], grads[k], ms[k], vs[k], "adam_" + k)
    return (loss, grad_x, *[grads[k] for k in order], *[deltas[k] for k in order],
            *[new_m[k] for k in order], *[new_v[k] for k in order])
```

```python
import functools
import math

import numpy as np
import jax
import jax.numpy as jnp
from jax import lax
from jax.experimental import pallas as pl
from jax.experimental.pallas import tpu as pltpu

F32 = jnp.float32
BF16 = jnp.bfloat16
MXU_DTYPE = jnp.bfloat16
MCAT_A, MCAT_C = 2, 3
WOUT_PERM, WOUT_INV = (1, 2, 0, 3), (2, 0, 1, 3)

D = 1024
EPS = 1e-6
TB = 256
CTX = 256
CHUNK = 128
GRID_W = 64
A_W, B_W, C_W = 256, 512, 256
D_IN = 1280
D_FF = 2816
SSM_G, SSM_P, SSM_H = 32, 64, 16
ST = 64
POOL_WINDOWS = (2, 4, 8, 16)
N_DEV = 8
VMEM_LIMIT = 52 * 1024 * 1024
GELU_C = math.sqrt(2.0 / math.pi)

ADAM_LR, ADAM_B1, ADAM_B2, ADAM_EPS, ADAM_WD, ADAM_STEP = 0.001, 0.9, 0.999, 1e-08, 0.01, 10


def _cp(sem=None, vmem=VMEM_LIMIT, **kw):
    return pltpu.CompilerParams(dimension_semantics=sem, vmem_limit_bytes=vmem, **kw)


def _pick(n, cap):
    if n <= cap:
        return n
    best = None
    for t in range(128, cap + 1, 128):
        if n % t == 0:
            best = t
    assert best is not None, (n, cap)
    return best


def _gelu(x):
    return 0.5 * x * (1.0 + jnp.tanh(GELU_C * (x + 0.044715 * x * x * x)))


def _gelu_grad(x):
    t = jnp.tanh(GELU_C * (x + 0.044715 * x * x * x))
    return 0.5 * (1.0 + t) + 0.5 * x * (1.0 - t * t) * GELU_C * (1.0 + 3.0 * 0.044715 * x * x)


def _sigmoid(x):
    return 1.0 / (1.0 + jnp.exp(-x))


def _dot(a, b, dims):
    return lax.dot_general(a, b, (dims, ((), ())), preferred_element_type=F32)


def _nn(a, b):
    return _dot(a, b, ((1,), (0,)))


def _nt(a, b):
    return _dot(a, b, ((1,), (1,)))


def _tn(a, b):
    return _dot(a, b, ((0,), (0,)))


def _mm(pairs, nt, out_dtype, name, tm=512):
    m = pairs[0][0].shape[0]
    n = pairs[0][1].shape[0] if nt else pairs[0][1].shape[1]
    tn = _pick(n, 1408)
    tm = min(tm, m)
    npairs = len(pairs)

    def body(*refs):
        o_ref = refs[-1]
        acc = None
        for i in range(npairs):
            a = refs[2 * i][...].astype(MXU_DTYPE)
            b = refs[2 * i + 1][...].astype(MXU_DTYPE)
            r = _nt(a, b) if nt else _nn(a, b)
            acc = r if acc is None else acc + r
        o_ref[...] = acc.astype(o_ref.dtype)

    in_specs, flat = [], []
    for a, b in pairs:
        k = a.shape[1]
        in_specs.append(pl.BlockSpec((tm, k), lambda i, j: (i, 0)))
        in_specs.append(pl.BlockSpec((tn, k), lambda i, j: (j, 0)) if nt else pl.BlockSpec((k, tn), lambda i, j: (0, j)))
        flat += [a, b]
    return pl.pallas_call(
        body, grid=(m // tm, n // tn), in_specs=in_specs,
        out_specs=pl.BlockSpec((tm, tn), lambda i, j: (i, j)),
        out_shape=jax.ShapeDtypeStruct((m, n), out_dtype),
        compiler_params=_cp(("parallel", "parallel")), name=name)(*flat)


def _mm_tn(a, b, out_dtype, name, tm=512):
    m, k1 = a.shape
    n = b.shape[1]
    t1 = _pick(k1, 1408)
    tn = _pick(n, 1024)
    tm = min(tm, m)
    nsteps = m // tm

    def body(a_ref, b_ref, o_ref, acc_ref):
        t = pl.program_id(2)

        @pl.when(t == 0)
        def _():
            acc_ref[...] = jnp.zeros_like(acc_ref)

        acc_ref[...] += _tn(a_ref[...].astype(MXU_DTYPE), b_ref[...].astype(MXU_DTYPE))

        @pl.when(t == nsteps - 1)
        def _():
            o_ref[...] = acc_ref[...].astype(o_ref.dtype)

    return pl.pallas_call(
        body, grid=(k1 // t1, n // tn, nsteps),
        in_specs=[pl.BlockSpec((tm, t1), lambda i, j, t: (t, i)), pl.BlockSpec((tm, tn), lambda i, j, t: (t, j))],
        out_specs=pl.BlockSpec((t1, tn), lambda i, j, t: (i, j)),
        out_shape=jax.ShapeDtypeStruct((k1, n), out_dtype),
        scratch_shapes=[pltpu.VMEM((t1, tn), F32)],
        compiler_params=_cp(("parallel", "parallel", "arbitrary")), name=name)(a, b)


class _Layout:
    def __init__(self, bl, lat):
        self.bl, self.lat = bl, lat
        self.nlb = lat // TB
        self.nr = 1 + self.nlb
        self.nctx = bl
        self.nb = self.nr * bl
        self.nt = self.nb * TB
        self.ctx_row = bl

    def blk(self, g):
        gg = g - self.bl
        return jnp.where(g < self.bl, g, (gg % self.nlb + 1) * self.bl + gg // self.nlb)

    def modrow(self, g):
        return jnp.where(g < self.bl, self.ctx_row, (g - self.bl) // self.nlb)

    def first_of_row(self, g):
        return jnp.logical_or(g == 0, jnp.logical_and(g >= self.bl, (g - self.bl) % self.nlb == 0))

    def modrows_static(self):
        return np.array([self.ctx_row if j < self.bl else j % self.bl for j in range(self.nb)], np.int32)


def _tok_spec(lay):
    return pl.BlockSpec((TB, D), lambda g: (lay.blk(g), 0))


def _vec_spec():
    return pl.BlockSpec((1, D), lambda j: (0, 0))


def _mod_spec(lay, k):
    return pl.BlockSpec((1, 1, D), lambda g: (lay.blk(g) * 6 + k, 0, 0))


def _embed(lay, x2d, ctx2d, pe):
    bl, nlb = lay.bl, lay.nlb

    def body(x_ref, c_ref, pe_ref, o_ref):
        j = pl.program_id(0)

        @pl.when(j < bl)
        def _():
            o_ref[...] = c_ref[...]

        @pl.when(j >= bl)
        def _():
            o_ref[...] = x_ref[...] + pe_ref[...]

    pos = lambda j: jnp.maximum(j // bl - 1, 0)
    return pl.pallas_call(
        body, grid=(lay.nb,),
        in_specs=[pl.BlockSpec((TB, D), lambda j: ((j % bl) * nlb + pos(j), 0)),
                  pl.BlockSpec((TB, D), lambda j: (jnp.minimum(j, bl - 1), 0)),
                  pl.BlockSpec((TB, D), lambda j: (pos(j), 0))],
        out_specs=pl.BlockSpec((TB, D), lambda j: (j, 0)), out_shape=jax.ShapeDtypeStruct((lay.nt, D), F32),
        compiler_params=_cp(("parallel",)), name="embed")(x2d, ctx2d, pe)


def _normmod_fwd(lay, x, gain, modarr, ksh, ksc, name):
    def body(x_ref, g_ref, sh_ref, sc_ref, o_ref):
        xv = x_ref[...]
        r = lax.rsqrt(jnp.mean(xv * xv, axis=-1, keepdims=True) + EPS)
        o_ref[...] = ((xv * r * g_ref[...]) * (1.0 + sc_ref[0]) + sh_ref[0]).astype(o_ref.dtype)

    return pl.pallas_call(
        body, grid=(lay.nb,), in_specs=[_tok_spec(lay), _vec_spec(), _mod_spec(lay, ksh), _mod_spec(lay, ksc)],
        out_specs=_tok_spec(lay), out_shape=jax.ShapeDtypeStruct((lay.nt, D), MXU_DTYPE),
        compiler_params=_cp(("parallel",)), name=name)(x, gain, modarr, modarr)


def _acc_specs(lay):
    row = pl.BlockSpec((1, 1, D), lambda j: (lay.modrow(j), 0, 0))
    return row, jax.ShapeDtypeStruct((8, 1, D), F32)


def _normmod_bwd(lay, x, dh, dx_in, gain, modarr, ksc, name, latent_only=False):
    row_spec, row_shape = _acc_specs(lay)
    if latent_only:
        dx_spec = pl.BlockSpec((TB, D), lambda g: (jnp.maximum(g - lay.bl, 0), 0))
        dx_shape = jax.ShapeDtypeStruct((lay.bl * lay.lat, D), F32)
    else:
        dx_spec, dx_shape = _tok_spec(lay), jax.ShapeDtypeStruct((lay.nt, D), F32)

    def body(x_ref, dh_ref, dxi_ref, g_ref, sc_ref, dx_ref, dsh_ref, dsc_ref, dg_ref):
        j = pl.program_id(0)
        xv = x_ref[...]
        dhv = dh_ref[...].astype(F32)
        g = g_ref[...]
        sc1 = 1.0 + sc_ref[0]
        r = lax.rsqrt(jnp.mean(xv * xv, axis=-1, keepdims=True) + EPS)
        xh = xv * r
        dxh = dhv * (g * sc1)
        dx = r * (dxh - xh * jnp.mean(dxh * xh, axis=-1, keepdims=True))
        dx_ref[...] = dxi_ref[...] + dx

        @pl.when(lay.first_of_row(j))
        def _():
            dsh_ref[...] = jnp.zeros_like(dsh_ref)
            dsc_ref[...] = jnp.zeros_like(dsc_ref)

        @pl.when(j == 0)
        def _():
            dg_ref[...] = jnp.zeros_like(dg_ref)

        dsh_ref[0] += jnp.sum(dhv, axis=0, keepdims=True)
        dsc_ref[0] += jnp.sum(dhv * (xh * g), axis=0, keepdims=True)
        dg_ref[...] += jnp.sum(dhv * sc1 * xh, axis=0, keepdims=True)

    return pl.pallas_call(
        body, grid=(lay.nb,),
        in_specs=[_tok_spec(lay), _tok_spec(lay), _tok_spec(lay), _vec_spec(), _mod_spec(lay, ksc)],
        out_specs=[dx_spec, row_spec, row_spec, _vec_spec()],
        out_shape=[dx_shape, row_shape, row_shape, jax.ShapeDtypeStruct((1, D), F32)],
        compiler_params=_cp(("arbitrary",)), name=name)(x, dh, dx_in, gain, modarr)


def _resnorm_fwd(lay, x, m, gain, modarr, kgate, name):
    def body(x_ref, m_ref, g_ref, gate_ref, o_ref):
        mv = m_ref[...]
        r = lax.rsqrt(jnp.mean(mv * mv, axis=-1, keepdims=True) + EPS)
        o_ref[...] = x_ref[...] + gate_ref[0] * (mv * r * g_ref[...])

    return pl.pallas_call(
        body, grid=(lay.nb,), in_specs=[_tok_spec(lay), _tok_spec(lay), _vec_spec(), _mod_spec(lay, kgate)],
        out_specs=_tok_spec(lay), out_shape=jax.ShapeDtypeStruct((lay.nt, D), F32),
        compiler_params=_cp(("parallel",)), name=name)(x, m, gain, modarr)


def _resnorm_bwd(lay, dxn, m, gain, modarr, kgate, name):
    row_spec, row_shape = _acc_specs(lay)

    def body(d_ref, m_ref, g_ref, gate_ref, dm_ref, dgate_ref, dg_ref):
        j = pl.program_id(0)
        dv = d_ref[...]
        mv = m_ref[...]
        g = g_ref[...]
        r = lax.rsqrt(jnp.mean(mv * mv, axis=-1, keepdims=True) + EPS)
        xh = mv * r
        dy = dv * gate_ref[0]
        dxh = dy * g
        dm_ref[...] = (r * (dxh - xh * jnp.mean(dxh * xh, axis=-1, keepdims=True))).astype(dm_ref.dtype)

        @pl.when(lay.first_of_row(j))
        def _():
            dgate_ref[...] = jnp.zeros_like(dgate_ref)

        @pl.when(j == 0)
        def _():
            dg_ref[...] = jnp.zeros_like(dg_ref)

        dgate_ref[0] += jnp.sum(dv * (xh * g), axis=0, keepdims=True)
        dg_ref[...] += jnp.sum(dy * xh, axis=0, keepdims=True)

    return pl.pallas_call(
        body, grid=(lay.nb,), in_specs=[_tok_spec(lay), _tok_spec(lay), _vec_spec(), _mod_spec(lay, kgate)],
        out_specs=[_tok_spec(lay), row_spec, _vec_spec()],
        out_shape=[jax.ShapeDtypeStruct((lay.nt, D), MXU_DTYPE), row_shape, jax.ShapeDtypeStruct((1, D), F32)],
        compiler_params=_cp(("arbitrary",)), name=name)(dxn, m, gain, modarr)


def _loss_bwd(lay, xf, tgt2d):
    bl, nlb = lay.bl, lay.nlb

    def body(x_ref, t_ref, dx_ref, l_ref):
        j = pl.program_id(0)

        @pl.when(j == 0)
        def _():
            l_ref[...] = jnp.zeros_like(l_ref)

        @pl.when(j < bl)
        def _():
            dx_ref[...] = jnp.zeros_like(dx_ref)

        @pl.when(j >= bl)
        def _():
            e = x_ref[...] - t_ref[...]
            dx_ref[...] = e * (1.0 / D)
            l_ref[...] += jnp.sum(e * e) * (0.5 / D)

    tok = pl.BlockSpec((TB, D), lambda j: (j, 0))
    return pl.pallas_call(
        body, grid=(lay.nb,),
        in_specs=[tok, pl.BlockSpec((TB, D), lambda j: ((j % bl) * nlb + jnp.maximum(j // bl - 1, 0), 0))],
        out_specs=[tok, pl.BlockSpec((8, 128), lambda j: (0, 0))],
        out_shape=[jax.ShapeDtypeStruct((lay.nt, D), F32), jax.ShapeDtypeStruct((8, 128), F32)],
        compiler_params=_cp(("arbitrary",)), name="loss")(xf, tgt2d)


def _ffn_up(h, wgt, wut, name):
    m = h.shape[0]
    tm, tn = min(512, m), D_FF // 2

    def body(h_ref, wg_ref, wu_ref, g_ref, u_ref, a_ref):
        hv = h_ref[...]
        g = _nt(hv, wg_ref[...])
        u = _nt(hv, wu_ref[...])
        g_ref[...] = g.astype(g_ref.dtype)
        u_ref[...] = u.astype(u_ref.dtype)
        a_ref[...] = (g * _sigmoid(g) * u).astype(a_ref.dtype)

    osp = pl.BlockSpec((tm, tn), lambda i, j: (i, j))
    osh = jax.ShapeDtypeStruct((m, D_FF), MXU_DTYPE)
    return pl.pallas_call(
        body, grid=(m // tm, D_FF // tn),
        in_specs=[pl.BlockSpec((tm, D), lambda i, j: (i, 0)), pl.BlockSpec((tn, D), lambda i, j: (j, 0)),
                  pl.BlockSpec((tn, D), lambda i, j: (j, 0))],
        out_specs=[osp, osp, osp], out_shape=[osh, osh, osh],
        compiler_params=_cp(("parallel", "parallel")), name=name)(h, wgt, wut)


def _ffn_down_bwd(df, wd, g, u, name):
    m = df.shape[0]
    tm, tn = min(512, m), D_FF // 2

    def body(df_ref, wd_ref, g_ref, u_ref, dg_ref, du_ref):
        da = _nt(df_ref[...], wd_ref[...])
        gv = g_ref[...].astype(F32)
        uv = u_ref[...].astype(F32)
        s = _sigmoid(gv)
        dg_ref[...] = (da * uv * (s * (1.0 + gv * (1.0 - s)))).astype(dg_ref.dtype)
        du_ref[...] = (da * gv * s).astype(du_ref.dtype)

    osp = pl.BlockSpec((tm, tn), lambda i, j: (i, j))
    osh = jax.ShapeDtypeStruct((m, D_FF), MXU_DTYPE)
    return pl.pallas_call(
        body, grid=(m // tm, D_FF // tn),
        in_specs=[pl.BlockSpec((tm, D), lambda i, j: (i, 0)), pl.BlockSpec((tn, D), lambda i, j: (j, 0)), osp, osp],
        out_specs=[osp, osp], out_shape=[osh, osh],
        compiler_params=_cp(("parallel", "parallel")), name=name)(df, wd, g, u)


def _head_masks(shape):
    lane = lax.broadcasted_iota(jnp.int32, shape, 1)
    return [jnp.logical_and(lane >= 64 * h, lane < 64 * h + 64) for h in range(4)]


def _head_mean(x, masks):
    out = jnp.zeros_like(x)
    for mk in masks:
        s = jnp.sum(jnp.where(mk, x, 0.0), axis=-1, keepdims=True) * (1.0 / 64.0)
        out = jnp.where(mk, s, out)
    return out


def _gate_common(z, masks):
    zg = _gelu(z)
    u = zg[:, :A_W]
    v = zg[:, A_W:]
    mu = _head_mean(v, masks)
    vc = v - mu
    rstd = lax.rsqrt(_head_mean(vc * vc, masks) + EPS)
    return u, vc * rstd, rstd


def _gate_s(vn, ws_ref, bias, masks):
    parts = []
    for c in range(TB // CHUNK):
        vc = vn[c * CHUNK:(c + 1) * CHUNK]
        s = bias
        for h in range(4):
            s = s + _nn(ws_ref[h], jnp.where(masks[h][:CHUNK], vc, 0.0).astype(MXU_DTYPE))
        parts.append(s)
    return jnp.concatenate(parts, axis=0)


def _gate_fwd(lay, z, ws, bias, name):
    def body(z_ref, ws_ref, b_ref, o_ref):
        masks = _head_masks((TB, A_W))
        u, vn, _ = _gate_common(z_ref[...], masks)
        o_ref[...] = (u * _gate_s(vn, ws_ref, b_ref[...], masks)).astype(o_ref.dtype)

    return pl.pallas_call(
        body, grid=(lay.nb,),
        in_specs=[pl.BlockSpec((TB, 2 * A_W), lambda j: (j, 0)), pl.BlockSpec((4, CHUNK, CHUNK), lambda j: (0, 0, 0)),
                  pl.BlockSpec((CHUNK, A_W), lambda j: (0, 0))],
        out_specs=pl.BlockSpec((TB, A_W), lambda j: (j, 0)),
        out_shape=jax.ShapeDtypeStruct((lay.nt, A_W), MXU_DTYPE),
        compiler_params=_cp(("parallel",)), name=name)(z, ws, bias)


def _gate_bwd(lay, z, da, ws, wst, bias, name):
    def body(z_ref, da_ref, ws_ref, wst_ref, b_ref, dz_ref, dws_ref, db_ref):
        j = pl.program_id(0)

        @pl.when(j == 0)
        def _():
            dws_ref[...] = jnp.zeros_like(dws_ref)
            db_ref[...] = jnp.zeros_like(db_ref)

        masks = _head_masks((TB, A_W))
        zv = z_ref[...]
        u, vn, rstd = _gate_common(zv, masks)
        s = _gate_s(vn, ws_ref, b_ref[...], masks)
        dav = da_ref[...].astype(F32)
        du = dav * s
        ds = dav * u
        dvn_parts = []
        for c in range(TB // CHUNK):
            sl = slice(c * CHUNK, (c + 1) * CHUNK)
            ds_c = ds[sl]
            vn_c = vn[sl].astype(MXU_DTYPE)
            db_ref[...] += ds_c
            ds_b = ds_c.astype(MXU_DTYPE)
            dvn_c = jnp.zeros((CHUNK, A_W), F32)
            for h in range(4):
                mk = masks[h][:CHUNK]
                dws_ref[h] += _nt(jnp.where(mk, ds_c, 0.0).astype(MXU_DTYPE), vn_c)
                dvn_c = dvn_c + jnp.where(mk, _nn(wst_ref[h], ds_b), 0.0)
            dvn_parts.append(dvn_c)
        dvn = jnp.concatenate(dvn_parts, axis=0)
        dv = rstd * (dvn - _head_mean(dvn, masks) - vn * _head_mean(dvn * vn, masks))
        gg = _gelu_grad(zv)
        dz_ref[:, :A_W] = (du * gg[:, :A_W]).astype(dz_ref.dtype)
        dz_ref[:, A_W:] = (dv * gg[:, A_W:]).astype(dz_ref.dtype)

    return pl.pallas_call(
        body, grid=(lay.nb,),
        in_specs=[pl.BlockSpec((TB, 2 * A_W), lambda j: (j, 0)), pl.BlockSpec((TB, A_W), lambda j: (j, MCAT_A)),
                  pl.BlockSpec((4, CHUNK, CHUNK), lambda j: (0, 0, 0)), pl.BlockSpec((4, CHUNK, CHUNK), lambda j: (0, 0, 0)),
                  pl.BlockSpec((CHUNK, A_W), lambda j: (0, 0))],
        out_specs=[pl.BlockSpec((TB, 2 * A_W), lambda j: (j, 0)), pl.BlockSpec((4, CHUNK, CHUNK), lambda j: (0, 0, 0)),
                   pl.BlockSpec((CHUNK, A_W), lambda j: (0, 0))],
        out_shape=[jax.ShapeDtypeStruct((lay.nt, 2 * A_W), MXU_DTYPE), jax.ShapeDtypeStruct((4, CHUNK, CHUNK), F32),
                   jax.ShapeDtypeStruct((CHUNK, A_W), F32)],
        compiler_params=_cp(("arbitrary",)), name=name)(z, da, ws, wst, bias)


def _band_constants():
    bands = np.zeros((2, 4, TB, TB), np.float32)
    inv = np.zeros((2, 4, TB, 1), np.float32)
    for kind, n in ((0, GRID_W), (1, TB)):
        for i, w in enumerate(POOL_WINDOWS):
            for t in range(TB):
                base, tt = (t // n) * n, t % n
                lo = min(max(tt - w // 2, 0), n)
                hi = min(max(tt - w // 2 + w, 0), n)
                bands[kind, i, t, base + lo:base + hi] = 1.0
                inv[kind, i, t, 0] = 1.0 / (hi - lo)
    return bands, inv


def _split3(x):
    a = x.astype(MXU_DTYPE)
    r1 = x - a.astype(F32)
    b = r1.astype(MXU_DTYPE)
    c = (r1 - b.astype(F32)).astype(MXU_DTYPE)
    return a, b, c


def _window_apply(band_ref, inv_ref, x, masks, transpose):
    out = jnp.zeros_like(x)
    for i in range(4):
        xi = x * inv_ref[0, i] if transpose else x
        acc = None
        for part in _split3(xi):
            r = _tn(band_ref[0, i], part) if transpose else _nn(band_ref[0, i], part)
            acc = r if acc is None else acc + r
        if not transpose:
            acc = acc * inv_ref[0, i]
        out = jnp.where(masks[i], acc, out)
    return out


def _pool_specs(lay):
    kind = lambda j: jnp.where(j < lay.nctx, 1, 0)
    return [pl.BlockSpec((1, 4, TB, TB), lambda j: (kind(j), 0, 0, 0)), pl.BlockSpec((1, 4, TB, 1), lambda j: (kind(j), 0, 0, 0))]


def _pool_fwd(lay, z, bands, inv, pw, scale, name):
    def body(p_ref, band_ref, inv_ref, pw_ref, sc_ref, o_ref):
        masks = _head_masks((TB, C_W))
        p = p_ref[...]
        diff = _window_apply(band_ref, inv_ref, p, masks, False) - p
        o_ref[...] = (_nn(diff.astype(MXU_DTYPE), pw_ref[...]) * sc_ref[...]).astype(o_ref.dtype)

    return pl.pallas_call(
        body, grid=(lay.nb,),
        in_specs=[pl.BlockSpec((TB, C_W), lambda j: (j, 4))] + _pool_specs(lay)
        + [pl.BlockSpec((C_W, C_W), lambda j: (0, 0)), pl.BlockSpec((1, C_W), lambda j: (0, 0))],
        out_specs=pl.BlockSpec((TB, C_W), lambda j: (j, 0)),
        out_shape=jax.ShapeDtypeStruct((lay.nt, C_W), MXU_DTYPE),
        compiler_params=_cp(("parallel",)), name=name)(z, bands, inv, pw, scale)


def _pool_bwd(lay, z, dc, bands, inv, pw, scale, name):
    def body(p_ref, dc_ref, band_ref, inv_ref, pw_ref, sc_ref, dp_ref, dpw_ref, dsc_ref):
        j = pl.program_id(0)

        @pl.when(j == 0)
        def _():
            dpw_ref[...] = jnp.zeros_like(dpw_ref)
            dsc_ref[...] = jnp.zeros_like(dsc_ref)

        masks = _head_masks((TB, C_W))
        p = p_ref[...]
        dcv = dc_ref[...].astype(F32)
        diff = _window_apply(band_ref, inv_ref, p, masks, False) - p
        diff_b = diff.astype(MXU_DTYPE)
        pre = _nn(diff_b, pw_ref[...])
        dsc_ref[...] += jnp.sum(dcv * pre, axis=0, keepdims=True)
        dpre = dcv * sc_ref[...]
        dpre_b = dpre.astype(MXU_DTYPE)
        dpw_ref[...] += _tn(diff_b, dpre_b)
        ddiff = _nt(dpre_b, pw_ref[...])
        dp_ref[...] = (_window_apply(band_ref, inv_ref, ddiff, masks, True) - ddiff).astype(dp_ref.dtype)

    return pl.pallas_call(
        body, grid=(lay.nb,),
        in_specs=[pl.BlockSpec((TB, C_W), lambda j: (j, 4)), pl.BlockSpec((TB, C_W), lambda j: (j, MCAT_C))] + _pool_specs(lay)
        + [pl.BlockSpec((C_W, C_W), lambda j: (0, 0)), pl.BlockSpec((1, C_W), lambda j: (0, 0))],
        out_specs=[pl.BlockSpec((TB, C_W), lambda j: (j, 0)), pl.BlockSpec((C_W, C_W), lambda j: (0, 0)),
                   pl.BlockSpec((1, C_W), lambda j: (0, 0))],
        out_shape=[jax.ShapeDtypeStruct((lay.nt, C_W), MXU_DTYPE), jax.ShapeDtypeStruct((C_W, C_W), F32),
                   jax.ShapeDtypeStruct((1, C_W), F32)],
        compiler_params=_cp(("arbitrary",)), name=name)(z, dc, bands, inv, pw, scale)


def _disc_math(lr, li, ldt, br, bi):
    dt = jnp.exp(ldt)
    e = jnp.exp(lr * dt)
    ar = e * jnp.cos(li * dt)
    ai = e * jnp.sin(li * dt)
    nr, ni = ar - 1.0, ai
    den = lr * lr + li * li
    qr = (nr * lr + ni * li) / den
    qi = (ni * lr - nr * li) / den
    return ar, ai, qr * br - qi * bi, qr * bi + qi * br


def _disc_fwd(lrx, lix, ldtx, brt, bit, name):
    def body(lr_ref, li_ref, ldt_ref, br_ref, bi_ref, ar_ref, ai_ref, obr_ref, obi_ref):
        ar, ai, obr, obi = _disc_math(lr_ref[...], li_ref[...], ldt_ref[...], br_ref[...], bi_ref[...])
        ar_ref[...] = ar
        ai_ref[...] = ai
        obr_ref[...] = obr
        obi_ref[...] = obi

    sh = jax.ShapeDtypeStruct(lrx.shape, F32)
    return pl.pallas_call(body, out_shape=[sh, sh, sh, sh], name=name)(lrx, lix, ldtx, brt, bit)


def _disc_bwd(lrx, lix, ldtx, brt, bit, dar, dai, dbr, dbi, name):
    nrow = lrx.shape[0] // SSM_H

    def body(lr_ref, li_ref, ldt_ref, br_ref, bi_ref, dar_ref, dai_ref, dbr_ref, dbi_ref,
             glr_ref, gli_ref, gdt_ref, gbr_ref, gbi_ref):
        _, vjp = jax.vjp(_disc_math, lr_ref[...], li_ref[...], ldt_ref[...], br_ref[...], bi_ref[...])
        glr, gli, gdt, gbr, gbi = vjp((dar_ref[...], dai_ref[...], dbr_ref[...], dbi_ref[...]))
        glr_ref[...] = jnp.sum(glr.reshape(nrow, SSM_H, SSM_P), axis=1)
        gli_ref[...] = jnp.sum(gli.reshape(nrow, SSM_H, SSM_P), axis=1)
        gdt_ref[...] = jnp.sum(jnp.sum(gdt.reshape(nrow, SSM_H, SSM_P), axis=1), axis=-1, keepdims=True)
        gbr_ref[...] = gbr
        gbi_ref[...] = gbi

    small = jax.ShapeDtypeStruct((nrow, SSM_P), F32)
    big = jax.ShapeDtypeStruct(lrx.shape, F32)
    return pl.pallas_call(body, out_shape=[small, small, jax.ShapeDtypeStruct((nrow, 1), F32), big, big],
                          name=name)(lrx, lix, ldtx, brt, bit, dar, dai, dbr, dbi)


HS = 1024
GQ, QC, QS = 8, 128, 512
LC = QS


def _dir_cat(x, d0, qq):
    xq = x[:, QC * qq:QC * qq + QC]
    zero = jnp.zeros_like(xq)
    return jnp.concatenate([jnp.where(d0, xq, zero), jnp.where(d0, zero, xq)], axis=1)


def _dir_pick(x, d0):
    return jnp.where(d0, x[:, :QC], x[:, QC:])


def _d0_rows(n):
    row = lax.broadcasted_iota(jnp.int32, (n, 1), 0)
    return jnp.bitwise_and(row, 4) == 0


def _scan_perm(bl):
    n = 2 * bl * ST
    p = np.zeros((n, n), np.float32)
    for s in range(ST):
        for d in range(2):
            for b in range(bl):
                t = s if d == 0 else ST - 1 - s
                p[s * 2 * bl + d * bl + b, d * bl * ST + b * ST + t] = 1.0
    return p


def _scan_maps(lay):
    spc = TB // ST
    nlc = lay.nlb * spc

    def fwd(k):
        return k // spc, k % spc

    def rev(k):
        cpos = nlc - 1 - jnp.maximum(k - spc, 0)
        return jnp.where(k < spc, 0, 1 + cpos // spc), jnp.where(k < spc, spc - 1 - k, cpos % spc)

    return fwd, rev


def _pack_rows(f_ref, r_ref, p_ref, rc):
    st = jnp.concatenate([f_ref[0].reshape(rc // 2, 256), r_ref[0].reshape(rc // 2, 256)], axis=0).astype(MXU_DTYPE)
    return _nn(p_ref[...], st).astype(MXU_DTYPE)


def _ssm_fwd(lay, z, perm, bh, ch, ar8, ai8, name):
    bl = lay.bl
    rc = ST * 2 * bl
    nch = lay.nr * (TB // ST)
    fwd, rev = _scan_maps(lay)
    z4 = z.reshape(lay.nr, bl, TB, z.shape[1])

    def body(uf_ref, ur_ref, p_ref, bh_ref, ch_ref, ar_ref, ai_ref, yf_ref, yr_ref, hst_ref, hs, hc):
        k = pl.program_id(1)

        @pl.when(k == 0)
        def _():
            hc[...] = jnp.zeros_like(hc)

        hst_ref[0] = hc[...]
        d0 = _d0_rows(rc)
        uv = _pack_rows(uf_ref, ur_ref, p_ref, rc)
        for q in range(2):
            cr, ci = 2 * QS * q, 2 * QS * q + QS
            hs[:, cr:cr + 2 * QS] = _nn(_dir_cat(uv, d0, q), bh_ref[q])
            ar = ar_ref[:, QS * q:QS * q + QS]
            ai = ai_ref[:, QS * q:QS * q + QS]

            def step(s, carry, cr=cr, ci=ci, ar=ar, ai=ai):
                hr, hi = carry
                base = pl.multiple_of(s * 8, 8)
                nr = ar * hr - ai * hi + hs[pl.ds(base, 8), cr:cr + LC]
                ni = ar * hi + ai * hr + hs[pl.ds(base, 8), ci:ci + LC]
                hs[pl.ds(base, 8), cr:cr + LC] = nr
                hs[pl.ds(base, 8), ci:ci + LC] = ni
                return nr, ni

            hr, hi = lax.fori_loop(0, ST, step, (hc[:, cr:cr + LC], hc[:, ci:ci + LC]), unroll=4)
            hc[:, cr:cr + LC] = hr
            hc[:, ci:ci + LC] = hi
        yi = jnp.concatenate(
            [_dir_pick(_nn(hs[:, 2 * QS * q:2 * QS * (q + 1)].astype(MXU_DTYPE), ch_ref[q]), d0) for q in range(2)], axis=1)
        yhi = yi.astype(MXU_DTYPE)
        ylo = (yi - yhi.astype(F32)).astype(MXU_DTYPE)
        yd = _tn(p_ref[...], yhi) + _tn(p_ref[...], ylo)
        yf_ref[0] = yd[:rc // 2].reshape(bl, ST, 256)
        yr_ref[0] = yd[rc // 2:].reshape(bl, ST, 256)

    blk = (1, bl, ST, 256)
    ysh = jax.ShapeDtypeStruct((lay.nr, bl, TB, B_W), F32)
    yf, yr, hst = pl.pallas_call(
        body, grid=(2, nch),
        in_specs=[pl.BlockSpec(blk, lambda f, k: (fwd(k)[0], 0, fwd(k)[1], 2 + f)),
                  pl.BlockSpec(blk, lambda f, k: (rev(k)[0], 0, rev(k)[1], 2 + f)),
                  pl.BlockSpec((rc, rc), lambda f, k: (0, 0)),
                  pl.BlockSpec((2, 2 * QC, 2 * QS), lambda f, k: (f, 0, 0)),
                  pl.BlockSpec((2, 2 * QS, 2 * QC), lambda f, k: (f, 0, 0)),
                  pl.BlockSpec((8, HS), lambda f, k: (0, f)), pl.BlockSpec((8, HS), lambda f, k: (0, f))],
        out_specs=[pl.BlockSpec(blk, lambda f, k: (fwd(k)[0], 0, fwd(k)[1], f)),
                   pl.BlockSpec(blk, lambda f, k: (rev(k)[0], 0, rev(k)[1], f)),
                   pl.BlockSpec((1, 8, 2 * HS), lambda f, k: (k, 0, f))],
        out_shape=[ysh, ysh, jax.ShapeDtypeStruct((nch, 8, 4 * HS), F32)],
        scratch_shapes=[pltpu.VMEM((rc, 2 * HS), F32), pltpu.VMEM((8, 2 * HS), F32)],
        compiler_params=_cp(("parallel", "arbitrary")), name=name)(z4, z4, perm, bh, ch, ar8, ai8)
    return yf.reshape(lay.nt, B_W), yr.reshape(lay.nt, B_W), hst


def _ssm_bwd(lay, z, dy, perm, hst, bh, ch, ar8, ai8, name):
    bl = lay.bl
    rc = ST * 2 * bl
    nch = lay.nr * (TB // ST)
    fwd, rev = _scan_maps(lay)
    z4 = z.reshape(lay.nr, bl, TB, z.shape[1])
    dy4 = dy.reshape(lay.nr, bl, TB, B_W)

    def body(uf_ref, ur_ref, dyf_ref, dyr_ref, p_ref, hst_ref, bh_ref, ch_ref, ar_ref, ai_ref,
             duf_ref, dur_ref, dbh_ref, dch_ref, dar_ref, dai_ref, hs, es, ec, accr, acci):
        k = pl.program_id(1)

        @pl.when(k == 0)
        def _():
            ec[...] = jnp.zeros_like(ec)
            accr[...] = jnp.zeros_like(accr)
            acci[...] = jnp.zeros_like(acci)
            dbh_ref[...] = jnp.zeros_like(dbh_ref)
            dch_ref[...] = jnp.zeros_like(dch_ref)

        d0 = _d0_rows(rc)
        uv = _pack_rows(uf_ref, ur_ref, p_ref, rc)
        dyv = _pack_rows(dyf_ref, dyr_ref, p_ref, rc)

        hs[0:8, :] = hst_ref[0]
        ucat, dycat = [], []
        for q in range(2):
            cr, ci = 2 * QS * q, 2 * QS * q + QS
            ucat.append(_dir_cat(uv, d0, q))
            dycat.append(_dir_cat(dyv, d0, q))
            hs[8:, cr:cr + 2 * QS] = _nn(ucat[q], bh_ref[q])
            ar = ar_ref[:, QS * q:QS * q + QS]
            ai = ai_ref[:, QS * q:QS * q + QS]

            def step(s, carry, cr=cr, ci=ci, ar=ar, ai=ai):
                hr, hi = carry
                base = pl.multiple_of(s * 8 + 8, 8)
                nr = ar * hr - ai * hi + hs[pl.ds(base, 8), cr:cr + LC]
                ni = ar * hi + ai * hr + hs[pl.ds(base, 8), ci:ci + LC]
                hs[pl.ds(base, 8), cr:cr + LC] = nr
                hs[pl.ds(base, 8), ci:ci + LC] = ni
                return nr, ni

            lax.fori_loop(0, ST, step, (hs[0:8, cr:cr + LC], hs[0:8, ci:ci + LC]), unroll=4)
            dch_ref[q] += _tn(hs[8:, cr:cr + 2 * QS].astype(MXU_DTYPE), dycat[q])
            es[:, cr:cr + 2 * QS] = _nt(dycat[q], ch_ref[q])

        dui = []
        for q in range(2):
            cr, ci = 2 * QS * q, 2 * QS * q + QS
            ar = ar_ref[:, QS * q:QS * q + QS]
            ai = ai_ref[:, QS * q:QS * q + QS]

            def bstep(i, carry, cr=cr, ci=ci, ar=ar, ai=ai):
                er, ei, sr, si = carry
                base = pl.multiple_of((ST - 1 - i) * 8, 8)
                ner = es[pl.ds(base, 8), cr:cr + LC] + ar * er + ai * ei
                nei = es[pl.ds(base, 8), ci:ci + LC] - ai * er + ar * ei
                es[pl.ds(base, 8), cr:cr + LC] = ner
                es[pl.ds(base, 8), ci:ci + LC] = nei
                hpr = hs[pl.ds(base, 8), cr:cr + LC]
                hpi = hs[pl.ds(base, 8), ci:ci + LC]
                return ner, nei, sr + ner * hpr + nei * hpi, si - ner * hpi + nei * hpr

            lo = QS * q
            er, ei, sr, si = lax.fori_loop(
                0, ST, bstep, (ec[:, cr:cr + LC], ec[:, ci:ci + LC], accr[:, lo:lo + LC], acci[:, lo:lo + LC]), unroll=4)
            ec[:, cr:cr + LC] = er
            ec[:, ci:ci + LC] = ei
            accr[:, lo:lo + LC] = sr
            acci[:, lo:lo + LC] = si
            eb = es[:, cr:cr + 2 * QS].astype(MXU_DTYPE)
            dui.append(_dir_pick(_nt(eb, bh_ref[q]), d0))
            dbh_ref[q] += _tn(ucat[q], eb)

        dud = _tn(p_ref[...], jnp.concatenate(dui, axis=1).astype(MXU_DTYPE))
        duf_ref[0] = dud[:rc // 2].reshape(bl, ST, 256).astype(duf_ref.dtype)
        dur_ref[0] = dud[rc // 2:].reshape(bl, ST, 256).astype(dur_ref.dtype)

        @pl.when(k == nch - 1)
        def _():
            for d in range(2):
                dar_ref[d:d + 1, :] = jnp.sum(accr[4 * d:4 * d + 4, :], axis=0, keepdims=True)
                dai_ref[d:d + 1, :] = jnp.sum(acci[4 * d:4 * d + 4, :], axis=0, keepdims=True)

    last = lambda k: nch - 1 - k
    blk = (1, bl, ST, 256)
    fspec = lambda c0: pl.BlockSpec(blk, lambda f, k: (fwd(last(k))[0], 0, fwd(last(k))[1], c0 + f))
    rspec = lambda c0: pl.BlockSpec(blk, lambda f, k: (rev(last(k))[0], 0, rev(last(k))[1], c0 + f))
    dush = jax.ShapeDtypeStruct((lay.nr, bl, TB, B_W), MXU_DTYPE)
    duf, dur, dbh, dch, dar, dai = pl.pallas_call(
        body, grid=(2, nch),
        in_specs=[fspec(2), rspec(2), fspec(0), rspec(0),
                  pl.BlockSpec((rc, rc), lambda f, k: (0, 0)),
                  pl.BlockSpec((1, 8, 2 * HS), lambda f, k: (last(k), 0, f)),
                  pl.BlockSpec((2, 2 * QC, 2 * QS), lambda f, k: (f, 0, 0)),
                  pl.BlockSpec((2, 2 * QS, 2 * QC), lambda f, k: (f, 0, 0)),
                  pl.BlockSpec((8, HS), lambda f, k: (0, f)), pl.BlockSpec((8, HS), lambda f, k: (0, f))],
        out_specs=[fspec(0), rspec(0),
                   pl.BlockSpec((2, 2 * QC, 2 * QS), lambda f, k: (f, 0, 0)),
                   pl.BlockSpec((2, 2 * QS, 2 * QC), lambda f, k: (f, 0, 0)),
                   pl.BlockSpec((2, HS), lambda f, k: (0, f)), pl.BlockSpec((2, HS), lambda f, k: (0, f))],
        out_shape=[dush, dush, jax.ShapeDtypeStruct((4, 2 * QC, 2 * QS), F32),
                   jax.ShapeDtypeStruct((4, 2 * QS, 2 * QC), F32), jax.ShapeDtypeStruct((2, 2 * HS), F32),
                   jax.ShapeDtypeStruct((2, 2 * HS), F32)],
        scratch_shapes=[pltpu.VMEM((rc + 8, 2 * HS), F32), pltpu.VMEM((rc, 2 * HS), F32), pltpu.VMEM((8, 2 * HS), F32),
                        pltpu.VMEM((8, HS), F32), pltpu.VMEM((8, HS), F32)],
        compiler_params=_cp(("parallel", "arbitrary")), name=name)(z4, z4, dy4, dy4, perm, hst, bh, ch, ar8, ai8)
    return duf.reshape(lay.nt, B_W), dur.reshape(lay.nt, B_W), dbh, dch, dar, dai


def _glu_fwd(lay, z, yf, yr, dvec, wglu, bglu, name):
    def body(u_ref, yf_ref, yr_ref, d_ref, w_ref, b_ref, o_ref, y_ref):
        y = yf_ref[...] + yr_ref[...] + d_ref[...] * u_ref[...]
        y_ref[...] = y
        g = _gelu(y)
        pre = _nn(g.astype(MXU_DTYPE), w_ref[...]) + b_ref[...]
        o_ref[...] = (g * _sigmoid(pre)).astype(o_ref.dtype)

    tok = pl.BlockSpec((TB, B_W), lambda j: (j, 0))
    vec = pl.BlockSpec((1, B_W), lambda j: (0, 0))
    return pl.pallas_call(
        body, grid=(lay.nb,),
        in_specs=[pl.BlockSpec((TB, B_W), lambda j: (j, 1)), tok, tok, vec, pl.BlockSpec((B_W, B_W), lambda j: (0, 0)), vec],
        out_specs=[tok, tok],
        out_shape=[jax.ShapeDtypeStruct((lay.nt, B_W), MXU_DTYPE), jax.ShapeDtypeStruct((lay.nt, B_W), F32)],
        compiler_params=_cp(("parallel",)), name=name)(z, yf, yr, dvec, wglu, bglu)


def _glu_bwd(lay, z, y, ds, dvec, wglu, bglu, name):
    def body(u_ref, y_ref, ds_ref, d_ref, w_ref, b_ref, dy_ref, dud_ref, dw_ref, db_ref, dd_ref):
        j = pl.program_id(0)

        @pl.when(j == 0)
        def _():
            dw_ref[...] = jnp.zeros_like(dw_ref)
            db_ref[...] = jnp.zeros_like(db_ref)
            dd_ref[...] = jnp.zeros_like(dd_ref)

        yv = y_ref[...]
        g = _gelu(yv)
        gb = g.astype(MXU_DTYPE)
        sg = _sigmoid(_nn(gb, w_ref[...]) + b_ref[...])
        dsv = ds_ref[...].astype(F32)
        dpre = dsv * g * sg * (1.0 - sg)
        dpre_b = dpre.astype(MXU_DTYPE)
        dg = dsv * sg + _nt(dpre_b, w_ref[...])
        dw_ref[...] += _tn(gb, dpre_b)
        db_ref[...] += jnp.sum(dpre, axis=0, keepdims=True)
        dy = dg * _gelu_grad(yv)
        dy_ref[...] = dy.astype(dy_ref.dtype)
        dd_ref[...] += jnp.sum(dy * u_ref[...], axis=0, keepdims=True)
        dud_ref[...] = (dy * d_ref[...]).astype(dud_ref.dtype)

    tok = pl.BlockSpec((TB, B_W), lambda j: (j, 0))
    vec = pl.BlockSpec((1, B_W), lambda j: (0, 0))
    mat = pl.BlockSpec((B_W, B_W), lambda j: (0, 0))
    vsh = jax.ShapeDtypeStruct((1, B_W), F32)
    return pl.pallas_call(
        body, grid=(lay.nb,),
        in_specs=[pl.BlockSpec((TB, B_W), lambda j: (j, 1)), tok, tok, vec, mat, vec],
        out_specs=[tok, tok, mat, vec, vec],
        out_shape=[jax.ShapeDtypeStruct((lay.nt, B_W), MXU_DTYPE), jax.ShapeDtypeStruct((lay.nt, B_W), F32),
                   jax.ShapeDtypeStruct((B_W, B_W), F32), vsh, vsh],
        compiler_params=_cp(("arbitrary",)), name=name)(z, y, ds, dvec, wglu, bglu)


def _dz_assemble(lay, dz_a, duf, dur, dud, dz_p, name):
    def body(a_ref, f_ref, r_ref, d_ref, p_ref, o_ref):
        o_ref[:, :2 * A_W] = a_ref[...].astype(o_ref.dtype)
        o_ref[:, 2 * A_W:2 * A_W + B_W] = (f_ref[...].astype(F32) + r_ref[...].astype(F32) + d_ref[...]).astype(o_ref.dtype)
        o_ref[:, 2 * A_W + B_W:] = p_ref[...].astype(o_ref.dtype)

    spec = lambda w: pl.BlockSpec((TB, w), lambda j: (j, 0))
    return pl.pallas_call(
        body, grid=(lay.nb,), in_specs=[spec(2 * A_W), spec(B_W), spec(B_W), spec(B_W), spec(C_W)],
        out_specs=spec(D_IN), out_shape=jax.ShapeDtypeStruct((lay.nt, D_IN), MXU_DTYPE),
        compiler_params=_cp(("parallel",)), name=name)(dz_a, duf, dur, dud, dz_p)


def _expand_rows(a):
    return jnp.broadcast_to(a[:, :, None, :], (2, SSM_G, SSM_H, SSM_P)).reshape(-1, SSM_P)


def _ssm_params(lam_re, lam_im, log_dt, b_re, b_im, c_re, c_im, name):
    lrx, lix = _expand_rows(lam_re), _expand_rows(lam_im)
    ldtx = _expand_rows(jnp.broadcast_to(log_dt[:, :, None], (2, SSM_G, SSM_P)))
    brt = jnp.transpose(b_re, (0, 1, 3, 2)).reshape(-1, SSM_P)
    bit = jnp.transpose(b_im, (0, 1, 3, 2)).reshape(-1, SSM_P)
    arx, aix, bbr, bbi = _disc_fwd(lrx, lix, ldtx, brt, bit, name)
    ar = arx.reshape(2, SSM_G, SSM_H, SSM_P)[:, :, 0].reshape(2, SSM_G * SSM_P)
    ai = aix.reshape(2, SSM_G, SSM_H, SSM_P)[:, :, 0].reshape(2, SSM_G * SSM_P)
    eye = jnp.eye(GQ, dtype=F32)

    def bmat(bt):
        t = bt.reshape(2, 4, GQ, SSM_H, SSM_P)
        return jnp.einsum('dqghp,gk->qdghkp', t, eye).reshape(4, 2 * QC, QS)

    bh = jnp.concatenate([bmat(bbr), bmat(bbi)], axis=-1).astype(MXU_DTYPE)

    def cmat(c):
        t = c.reshape(2, 4, GQ, SSM_H, SSM_P)
        return jnp.einsum('dqghp,gk->qgpdkh', t, eye).reshape(4, QS, 2 * QC)

    ch = jnp.concatenate([cmat(c_re), -cmat(c_im)], axis=1).astype(MXU_DTYPE)

    def rows8(a):
        return jnp.repeat(a, 4, axis=0)

    return dict(lrx=lrx, lix=lix, ldtx=ldtx, brt=brt, bit=bit, bh=bh, ch=ch, ar8=rows8(ar), ai8=rows8(ai))


def _ssm_param_grads(sp, dbh, dch, dar, dai, name):
    def bdiag(m):
        t = m.reshape(4, 2, GQ, SSM_H, GQ, SSM_P)
        return jnp.einsum('qdghgp->dqghp', t).reshape(-1, SSM_P)

    dbr, dbi = bdiag(dbh[..., :QS]), bdiag(dbh[..., QS:])

    def cdiag(m):
        t = m.reshape(4, GQ, SSM_P, 2, GQ, SSM_H)
        return jnp.einsum('qgpdgh->dqghp', t).reshape(2, SSM_G, SSM_H, SSM_P)

    dc_re, dc_im = cdiag(dch[:, :QS]), -cdiag(dch[:, QS:])

    def hrow(a):
        t = a.reshape(2, SSM_G, 1, SSM_P)
        return jnp.concatenate([t, jnp.zeros((2, SSM_G, SSM_H - 1, SSM_P), F32)], axis=2).reshape(-1, SSM_P)

    glr, gli, gdt, gbr, gbi = _disc_bwd(sp["lrx"], sp["lix"], sp["ldtx"], sp["brt"], sp["bit"],
                                        hrow(dar), hrow(dai), dbr, dbi, name)
    to_b = lambda g: jnp.transpose(g.reshape(2, SSM_G, SSM_H, SSM_P), (0, 1, 3, 2))
    return dict(ssm_lam_re=glr.reshape(2, SSM_G, SSM_P), ssm_lam_im=gli.reshape(2, SSM_G, SSM_P),
                ssm_log_dt=gdt.reshape(2, SSM_G), ssm_b_re=to_b(gbr), ssm_b_im=to_b(gbi),
                ssm_c_re=dc_re, ssm_c_im=dc_im)


def _layer_consts(p):
    c = {}
    c["ws"] = p["sgu_w"].astype(MXU_DTYPE)
    c["wst"] = jnp.transpose(p["sgu_w"], (0, 2, 1)).astype(MXU_DTYPE)
    c["gbias"] = jnp.repeat(p["sgu_b"].T, 64, axis=1)
    pw = jnp.zeros((C_W, C_W), F32)
    for i in range(4):
        pw = pw.at[64 * i:64 * i + 64, 64 * i:64 * i + 64].set(p["pool_w"][i])
    c["pw"] = pw.astype(MXU_DTYPE)
    c["pscale"] = p["pool_scale"].reshape(1, C_W)
    c["dvec"] = p["ssm_d"].reshape(1, B_W)
    c["bglu"] = p["glu_b"].reshape(1, B_W)
    return c


def _layer_fwd(lay, i, x, modarr, p, w, cst, sp, bands, inv, perm):
    n = f"l{i}_"
    res = {"x0": x}
    h = _normmod_fwd(lay, x, p["norm_mix_pre"].reshape(1, D), modarr, 0, 1, n + "nm1")
    z = _mm([(h, w["win_t"])], True, F32, n + "win")
    a = _gate_fwd(lay, z, cst["ws"], cst["gbias"], n + "gate")
    yf, yr, hst = _ssm_fwd(lay, z, perm, sp["bh"], sp["ch"], sp["ar8"], sp["ai8"], n + "ssm")
    s, y = _glu_fwd(lay, z, yf, yr, cst["dvec"], w["wglu"], cst["bglu"], n + "glu")
    c = _pool_fwd(lay, z, bands, inv, cst["pw"], cst["pscale"], n + "pool")
    mcat = jnp.concatenate([s, a, c], axis=1)
    m = _mm([(mcat, w["wout"])], False, F32, n + "wout")
    x1 = _resnorm_fwd(lay, x, m, p["norm_mix_post"].reshape(1, D), modarr, 2, n + "rn1")
    h2 = _normmod_fwd(lay, x1, p["norm_ffn_pre"].reshape(1, D), modarr, 3, 4, n + "nm2")
    g, u, act = _ffn_up(h2, w["wg_t"], w["wu_t"], n + "ffn_up")
    f = _mm([(act, w["wd"])], False, F32, n + "ffn_down")
    x2 = _resnorm_fwd(lay, x1, f, p["norm_ffn_post"].reshape(1, D), modarr, 5, n + "rn2")
    res.update(h=h, z=z, hst=hst, y=y, mcat=mcat, m=m, x1=x1, h2=h2, g=g, u=u, act=act, f=f)
    return x2, res


def _layer_bwd(lay, i, dx2, modarr, p, w, cst, sp, bands, inv, perm, res):
    n = f"l{i}b_"
    big, small = {}, {}
    df, dg2, gpost2 = _resnorm_bwd(lay, dx2, res["f"], p["norm_ffn_post"].reshape(1, D), modarr, 5, n + "rn2")
    big["wd"] = _mm_tn(res["act"], df, MXU_DTYPE, n + "dwd")
    dg, du = _ffn_down_bwd(df, w["wd"], res["g"], res["u"], n + "ffn_down")
    dh2 = _mm([(dg, w["wg_t"]), (du, w["wu_t"])], False, F32, n + "dh2")
    big["wg_t"] = _mm_tn(dg, res["h2"], MXU_DTYPE, n + "dwg")
    big["wu_t"] = _mm_tn(du, res["h2"], MXU_DTYPE, n + "dwu")
    dx1, dsh2, dsc2, gpre2 = _normmod_bwd(lay, res["x1"], dh2, dx2, p["norm_ffn_pre"].reshape(1, D), modarr, 4, n + "nm2")
    dm, dg1, gpost1 = _resnorm_bwd(lay, dx1, res["m"], p["norm_mix_post"].reshape(1, D), modarr, 2, n + "rn1")
    big["wout"] = _mm_tn(res["mcat"], dm, MXU_DTYPE, n + "dwout")
    dmcat = _mm([(dm, w["wout"])], True, F32, n + "dmcat")
    z = res["z"]
    dz_a, dws, dgb = _gate_bwd(lay, z, dmcat, cst["ws"], cst["wst"], cst["gbias"], n + "gate")
    dy, dud, dwglu, dbglu, ddvec = _glu_bwd(lay, z, res["y"], dmcat, cst["dvec"], w["wglu"], cst["bglu"], n + "glu")
    duf, dur, dbh, dch, dar, dai = _ssm_bwd(lay, z, dy, perm, res["hst"], sp["bh"], sp["ch"], sp["ar8"], sp["ai8"],
                                            n + "ssm")
    dz_p, dpw, dpsc = _pool_bwd(lay, z, dmcat, bands, inv, cst["pw"], cst["pscale"], n + "pool")
    dz = _dz_assemble(lay, dz_a, duf, dur, dud, dz_p, n + "dz")
    big["wglu"] = dwglu.astype(MXU_DTYPE)
    big["win_t"] = _mm_tn(dz, res["h"], MXU_DTYPE, n + "dwin")
    dh = _mm([(dz, w["win_t"])], False, F32, n + "dh")
    dx, dsh1, dsc1, gpre1 = _normmod_bwd(lay, res["x0"], dh, dx1, p["norm_mix_pre"].reshape(1, D), modarr, 1, n + "nm1",
                                         latent_only=(i == 0))

    small.update(norm_mix_pre=gpre1[0], norm_mix_post=gpost1[0], norm_ffn_pre=gpre2[0], norm_ffn_post=gpost2[0])
    small["sgu_w"] = dws
    small["sgu_b"] = jnp.sum(dgb.reshape(CHUNK, 4, 64), axis=-1).T
    small.update(_ssm_param_grads(sp, dbh, dch, dar, dai, n + "disc"))
    small["ssm_d"] = ddvec.reshape(SSM_G, SSM_H)
    small["glu_b"] = dbglu[0]
    small["pool_w"] = jnp.stack([dpw[64 * k:64 * k + 64, 64 * k:64 * k + 64] for k in range(4)])
    small["pool_scale"] = dpsc[0]
    dmod = jnp.concatenate([dsh1, dsc1, dg1, dsh2, dsc2, dg2], axis=1)[:lay.bl + 1]
    dmod = jnp.concatenate([dmod, jnp.zeros((8 - lay.bl - 1, 6, D), F32)], axis=0)
    return dx, big, small, dmod


SMALL_NAMES = ["norm_mix_pre", "norm_mix_post", "norm_ffn_pre", "norm_ffn_post", "sgu_w", "sgu_b", "ssm_lam_re",
               "ssm_lam_im", "ssm_log_dt", "ssm_b_re", "ssm_b_im", "ssm_c_re", "ssm_c_im", "ssm_d", "glu_b", "pool_w",
               "pool_scale"]
BIG_NAMES = ["win_t", "wout", "wglu", "wg_t", "wu_t", "wd"]


def _sincos_2d(rows, cols, dim):
    quarter = dim // 4
    omega = 1.0 / (10000.0 ** (jnp.arange(quarter, dtype=F32) / quarter))
    r = jnp.arange(rows, dtype=F32)[:, None] * omega
    cc = jnp.arange(cols, dtype=F32)[:, None] * omega
    er = jnp.concatenate([jnp.sin(r), jnp.cos(r)], axis=-1)
    ec = jnp.concatenate([jnp.sin(cc), jnp.cos(cc)], axis=-1)
    pe = jnp.concatenate([jnp.broadcast_to(er[:, None, :], (rows, cols, dim // 2)),
                          jnp.broadcast_to(ec[None, :, :], (rows, cols, dim // 2))], axis=-1)
    return pe.reshape(rows * cols, dim)


def _core(x, ctx, target, mods_local, params, weights):
    bl, lat, _ = x.shape
    assert bl == 4 and lat % TB == 0, "the scan fills 8 sublanes with 2 directions x 4 sequences"
    lay = _Layout(bl, lat)
    pe = _sincos_2d(lat // GRID_W, GRID_W, D)
    xt = _embed(lay, x.reshape(bl * lat, D), ctx.reshape(bl * CTX, D), pe)
    bands_np, inv_np = _band_constants()
    bands, inv = jnp.asarray(bands_np, MXU_DTYPE), jnp.asarray(inv_np, F32)
    perm = jnp.asarray(_scan_perm(bl), MXU_DTYPE)
    rows = lay.modrows_static()
    modarrs, csts, sps, ress, wls = [], [], [], [], []
    for i in range(2):
        modarrs.append(mods_local[i][rows].reshape(lay.nb * 6, 1, D))
        csts.append(_layer_consts(params[i]))
        p = params[i]
        sps.append(_ssm_params(p["ssm_lam_re"], p["ssm_lam_im"], p["ssm_log_dt"], p["ssm_b_re"], p["ssm_b_im"],
                               p["ssm_c_re"], p["ssm_c_im"], f"l{i}_disc"))
        w = dict(weights[i])
        w["wout"] = w["wout"].reshape(4, D // 4, D)[np.array(WOUT_PERM)].reshape(D, D)
        wls.append(w)
    for i in range(2):
        xt, res = _layer_fwd(lay, i, xt, modarrs[i], params[i], wls[i], csts[i], sps[i], bands, inv, perm)
        ress.append(res)
    dx, lossv = _loss_bwd(lay, xt, target.reshape(bl * lat, D))
    bigs, smalls, dmods = [None, None], [None, None], [None, None]
    for i in (1, 0):
        dx, bigs[i], smalls[i], dmods[i] = _layer_bwd(lay, i, dx, modarrs[i], params[i], wls[i], csts[i], sps[i],
                                                       bands, inv, perm, ress[i])
        bigs[i]["wout"] = bigs[i]["wout"].reshape(4, D // 4, D)[np.array(WOUT_INV)].reshape(D, D)
    return lossv[0, 0], dx.reshape(bl, lat, D), bigs, smalls, dmods


def _my_index():
    return 4 * lax.axis_index("x") + 2 * lax.axis_index("y") + lax.axis_index("c")


def _peer(k):
    x, y, c = lax.axis_index("x"), lax.axis_index("y"), lax.axis_index("c")
    kx, ky, kc = (k >> 2) & 1, (k >> 1) & 1, k & 1
    px = 1 - x if kx else x
    py = 1 - y if ky else y
    pc = 1 - c if kc else c
    return (px, py, pc), 4 * px + 2 * py + pc


def _comm(items, name):
    n = len(items)
    ncopies = sum(len(it[2]) for it in items)

    def slot_of(idx, slots):
        return idx if slots == 8 else (idx // 2 if slots == 4 else idx % 2)

    def body(*refs):
        ins, outs = refs[:n], refs[n:2 * n]
        send_sems, recv_sems, local_sems = refs[2 * n:]
        me = _my_index()
        local, sends, recvs = [], [], []
        q = 0
        for t, (arr, mode, ks, slots) in enumerate(items):
            src_own = ins[t] if mode == "gather" else ins[t].at[me]
            cp = pltpu.make_async_copy(src_own, outs[t].at[slot_of(me, slots)], local_sems.at[t])
            cp.start()
            local.append(cp)
            for k in ks:
                peer, pidx = _peer(k)
                src = ins[t] if mode == "gather" else ins[t].at[pidx]
                sends.append(pltpu.make_async_remote_copy(
                    src_ref=src, dst_ref=outs[t].at[slot_of(me, slots)], send_sem=send_sems.at[q], recv_sem=recv_sems.at[q],
                    device_id=peer, device_id_type=pl.DeviceIdType.MESH))
                recvs.append(pltpu.make_async_remote_copy(
                    src_ref=src, dst_ref=outs[t].at[slot_of(pidx, slots)], send_sem=send_sems.at[q], recv_sem=recv_sems.at[q],
                    device_id=peer, device_id_type=pl.DeviceIdType.MESH))
                q += 1
        for cp in sends:
            cp.start()
        for cp in recvs:
            cp.wait_recv()
        for cp in sends:
            cp.wait_send()
        for cp in local:
            cp.wait()

    out_shape = []
    for arr, mode, ks, slots in items:
        shp = (slots,) + tuple(arr.shape) if mode == "gather" else tuple(arr.shape)
        out_shape.append(jax.ShapeDtypeStruct(shp, arr.dtype))
    anyspec = pl.BlockSpec(memory_space=pl.ANY)
    return pl.pallas_call(
        body, in_specs=[anyspec] * n, out_specs=[anyspec] * n, out_shape=out_shape,
        scratch_shapes=[pltpu.SemaphoreType.DMA((ncopies,)), pltpu.SemaphoreType.DMA((ncopies,)),
                        pltpu.SemaphoreType.DMA((n,))],
        compiler_params=pltpu.CompilerParams(has_side_effects=True), name=name)(*[it[0] for it in items])


def _spread(items, name):
    n = len(items)
    ncopies = sum(len(it[1]) for it in items)

    def slot_of(idx, slots):
        return idx if slots == 8 else (idx // 2 if slots == 4 else idx % 2)

    def body(*refs):
        ins, outs, bufs = refs[:n], refs[n:2 * n], refs[2 * n:3 * n]
        load_sems, store_sems, send_sems, recv_sems = refs[3 * n:]
        me = _my_index()
        loads = [pltpu.make_async_copy(ins[t], bufs[t], load_sems.at[t]) for t in range(n)]
        for cp in loads:
            cp.start()
        stores, sends, recvs = [], [], []
        q = 0
        for t, (arr, ks, slots) in enumerate(items):
            loads[t].wait()
            own = outs[t].at[slot_of(me, slots)]
            stores.append(pltpu.make_async_copy(bufs[t], own, store_sems.at[t]))
            stores[-1].start()
            for k in ks:
                peer, pidx = _peer(k)
                sends.append(pltpu.make_async_remote_copy(
                    src_ref=bufs[t], dst_ref=own, send_sem=send_sems.at[q], recv_sem=recv_sems.at[q],
                    device_id=peer, device_id_type=pl.DeviceIdType.MESH))
                recvs.append(pltpu.make_async_remote_copy(
                    src_ref=bufs[t], dst_ref=outs[t].at[slot_of(pidx, slots)], send_sem=send_sems.at[q],
                    recv_sem=recv_sems.at[q], device_id=peer, device_id_type=pl.DeviceIdType.MESH))
                sends[-1].start()
                q += 1
        for cp in recvs:
            cp.wait_recv()
        for cp in sends:
            cp.wait_send()
        for cp in stores:
            cp.wait()

    anyspec = pl.BlockSpec(memory_space=pl.ANY)
    return pl.pallas_call(
        body, in_specs=[anyspec] * n, out_specs=[anyspec] * n,
        out_shape=[jax.ShapeDtypeStruct((slots,) + tuple(arr.shape), arr.dtype) for arr, ks, slots in items],
        scratch_shapes=[pltpu.VMEM(tuple(arr.shape), arr.dtype) for arr, ks, slots in items]
        + [pltpu.SemaphoreType.DMA((n,)), pltpu.SemaphoreType.DMA((n,)), pltpu.SemaphoreType.DMA((ncopies,)),
           pltpu.SemaphoreType.DMA((ncopies,))],
        compiler_params=pltpu.CompilerParams(has_side_effects=True, vmem_limit_bytes=VMEM_LIMIT),
        name=name)(*[it[0] for it in items])


ALL7 = (1, 2, 3, 4, 5, 6, 7)
CHIPS3 = (2, 4, 6)


def _sum8(parts, name):
    def one(a, nm):
        _, r, c = a.shape
        tr = r if r <= 512 else _pick_rows(r)

        def body(a_ref, o_ref):
            acc = a_ref[0].astype(F32)
            for q in range(1, a_ref.shape[0]):
                acc = acc + a_ref[q].astype(F32)
            o_ref[...] = acc

        return pl.pallas_call(
            body, grid=(r // tr,), in_specs=[pl.BlockSpec((a.shape[0], tr, c), lambda i: (0, i, 0))],
            out_specs=pl.BlockSpec((tr, c), lambda i: (i, 0)), out_shape=jax.ShapeDtypeStruct((r, c), F32),
            compiler_params=_cp(("parallel",)), name=nm)(a)

    return [one(a, f"{name}{i}") for i, a in enumerate(parts)]


def _pick_rows(r, cap=512):
    for t in (512, 352, 256, 176, 128, 64, 32, 16, 8):
        if r % t == 0 and t <= cap:
            return t
    return r


def _adam(w, g, m, v, name):
    shape = w.shape
    nel = int(np.prod(shape))
    if len(shape) >= 2 and shape[-1] >= 128:
        lanes = shape[-1]
    else:
        lanes = 512 if nel % 512 == 0 else 128
    r = nel // lanes
    tr = r if r * lanes <= 384 * 1024 else _pick_rows(r, 384 * 1024 // lanes)
    c1 = 1.0 / (1.0 - ADAM_B1 ** ADAM_STEP)
    c2 = 1.0 / (1.0 - ADAM_B2 ** ADAM_STEP)

    def body(w_ref, g_ref, m_ref, v_ref, d_ref, nm_ref, nv_ref):
        gv = g_ref[...]
        nm = ADAM_B1 * m_ref[...] + (1.0 - ADAM_B1) * gv
        nv = ADAM_B2 * v_ref[...] + (1.0 - ADAM_B2) * (gv * gv)
        d_ref[...] = -ADAM_LR * ((nm * c1) / (jnp.sqrt(nv * c2) + ADAM_EPS) + ADAM_WD * w_ref[...])
        nm_ref[...] = nm
        nv_ref[...] = nv

    spec = pl.BlockSpec((tr, lanes), lambda i: (i, 0))
    sh = jax.ShapeDtypeStruct((r, lanes), F32)
    outs = pl.pallas_call(
        body, grid=(r // tr,), in_specs=[spec] * 4, out_specs=[spec] * 3, out_shape=[sh] * 3,
        compiler_params=_cp(("parallel",)), name=name)(*[a.reshape(r, lanes) for a in (w, g, m, v)])
    return [o.reshape(shape) for o in outs]


def _silu(x):
    return x * _sigmoid(x)


def _mod_fwd(c_rows, w_mod, b_cols, name):
    def body(c_ref, w_ref, b_ref, o_ref):
        s = _silu(c_ref[...])
        for l in range(2):
            o_ref[l] = jnp.dot(s, w_ref[l], preferred_element_type=F32, precision=lax.Precision.HIGHEST) + b_ref[l]

    nc = w_mod.shape[2]
    return pl.pallas_call(body, out_shape=jax.ShapeDtypeStruct((2, c_rows.shape[0], nc), F32),
                          compiler_params=_cp(None), name=name)(c_rows, w_mod, b_cols)


def _mod_bwd(c_rows, w_mod, dlat, dctx8, name):
    nrow = c_rows.shape[0]
    nb = nrow - 8

    def body(c_ref, w_ref, dl_ref, dc_ref, gw_ref, gc_ref):
        s = _silu(c_ref[...])
        ctx_row = lax.broadcasted_iota(jnp.int32, (nrow, 1), 0) == nb
        gc = jnp.zeros((1, D), F32)
        for l in range(2):
            dctx = dc_ref[0, l]
            for q in range(1, 8):
                dctx = dctx + dc_ref[q, l]
            dm = dl_ref[l] + jnp.where(ctx_row, dctx, 0.0)
            gw_ref[l] = lax.dot_general(s, dm, (((0,), (0,)), ((), ())), preferred_element_type=F32,
                                        precision=lax.Precision.HIGHEST)
            gc = gc + lax.dot_general(dctx, w_ref[l], (((1,), (1,)), ((), ())), preferred_element_type=F32,
                                      precision=lax.Precision.HIGHEST)
        gc_ref[...] = gc

    nc = w_mod.shape[2]
    return pl.pallas_call(body, out_shape=[jax.ShapeDtypeStruct((2, D, nc), F32), jax.ShapeDtypeStruct((1, D), F32)],
                          compiler_params=_cp(None), name=name)(c_rows, w_mod, dlat, dctx8)


def _bmod_cctx(dmod_all, gc4, c_ctx, name):
    def body(dm_ref, gc_ref, cc_ref, gb_ref, gcc_ref):
        for l in range(2):
            acc = jnp.sum(dm_ref[0, l], axis=0, keepdims=True)
            for q in range(1, 8):
                acc = acc + jnp.sum(dm_ref[q, l], axis=0, keepdims=True)
            gb_ref[l:l + 1, :] = acc
        g = gc_ref[0] + gc_ref[1] + gc_ref[2] + gc_ref[3]
        cv = cc_ref[...]
        sg = _sigmoid(cv)
        gcc_ref[...] = g * (sg * (1.0 + cv * (1.0 - sg)))

    return pl.pallas_call(body, out_shape=[jax.ShapeDtypeStruct((2, 6 * D), F32), jax.ShapeDtypeStruct((1, D), F32)],
                          compiler_params=_cp(None), name=name)(dmod_all, gc4, c_ctx)


def kernel(x, c, ctx, c_ctx, w_mod, b_mod, norm_mix_pre, norm_mix_post, norm_ffn_pre, norm_ffn_post, w_in, w_out, sgu_w, sgu_b, ssm_lam_re, ssm_lam_im, ssm_log_dt, ssm_b_re, ssm_b_im, ssm_c_re, ssm_c_im, ssm_d, glu_w, glu_b, pool_w, pool_scale, ffn_w_gate, ffn_w_up, ffn_w_down, loss_target, m_c_ctx, m_w_mod, m_b_mod, m_norm_mix_pre, m_norm_mix_post, m_norm_ffn_pre, m_norm_ffn_post, m_w_in, m_w_out, m_sgu_w, m_sgu_b, m_ssm_lam_re, m_ssm_lam_im, m_ssm_log_dt, m_ssm_b_re, m_ssm_b_im, m_ssm_c_re, m_ssm_c_im, m_ssm_d, m_glu_w, m_glu_b, m_pool_w, m_pool_scale, m_ffn_w_gate, m_ffn_w_up, m_ffn_w_down, v_c_ctx, v_w_mod, v_b_mod, v_norm_mix_pre, v_norm_mix_post, v_norm_ffn_pre, v_norm_ffn_post, v_w_in, v_w_out, v_sgu_w, v_sgu_b, v_ssm_lam_re, v_ssm_lam_im, v_ssm_log_dt, v_ssm_b_re, v_ssm_b_im, v_ssm_c_re, v_ssm_c_im, v_ssm_d, v_glu_w, v_glu_b, v_pool_w, v_pool_scale, v_ffn_w_gate, v_ffn_w_up, v_ffn_w_down):
    wts = dict(c_ctx=c_ctx, w_mod=w_mod, b_mod=b_mod, norm_mix_pre=norm_mix_pre, norm_mix_post=norm_mix_post,
               norm_ffn_pre=norm_ffn_pre, norm_ffn_post=norm_ffn_post, w_in=w_in, w_out=w_out, sgu_w=sgu_w, sgu_b=sgu_b,
               ssm_lam_re=ssm_lam_re, ssm_lam_im=ssm_lam_im, ssm_log_dt=ssm_log_dt, ssm_b_re=ssm_b_re, ssm_b_im=ssm_b_im,
               ssm_c_re=ssm_c_re, ssm_c_im=ssm_c_im, ssm_d=ssm_d, glu_w=glu_w, glu_b=glu_b, pool_w=pool_w,
               pool_scale=pool_scale, ffn_w_gate=ffn_w_gate, ffn_w_up=ffn_w_up, ffn_w_down=ffn_w_down)
    ms = dict(c_ctx=m_c_ctx, w_mod=m_w_mod, b_mod=m_b_mod, norm_mix_pre=m_norm_mix_pre, norm_mix_post=m_norm_mix_post,
              norm_ffn_pre=m_norm_ffn_pre, norm_ffn_post=m_norm_ffn_post, w_in=m_w_in, w_out=m_w_out, sgu_w=m_sgu_w,
              sgu_b=m_sgu_b, ssm_lam_re=m_ssm_lam_re, ssm_lam_im=m_ssm_lam_im, ssm_log_dt=m_ssm_log_dt,
              ssm_b_re=m_ssm_b_re, ssm_b_im=m_ssm_b_im, ssm_c_re=m_ssm_c_re, ssm_c_im=m_ssm_c_im, ssm_d=m_ssm_d,
              glu_w=m_glu_w, glu_b=m_glu_b, pool_w=m_pool_w, pool_scale=m_pool_scale, ffn_w_gate=m_ffn_w_gate,
              ffn_w_up=m_ffn_w_up, ffn_w_down=m_ffn_w_down)
    vs = dict(c_ctx=v_c_ctx, w_mod=v_w_mod, b_mod=v_b_mod, norm_mix_pre=v_norm_mix_pre, norm_mix_post=v_norm_mix_post,
              norm_ffn_pre=v_norm_ffn_pre, norm_ffn_post=v_norm_ffn_post, w_in=v_w_in, w_out=v_w_out, sgu_w=v_sgu_w,
              sgu_b=v_sgu_b, ssm_lam_re=v_ssm_lam_re, ssm_lam_im=v_ssm_lam_im, ssm_log_dt=v_ssm_log_dt,
              ssm_b_re=v_ssm_b_re, ssm_b_im=v_ssm_b_im, ssm_c_re=v_ssm_c_re, ssm_c_im=v_ssm_c_im, ssm_d=v_ssm_d,
              glu_w=v_glu_w, glu_b=v_glu_b, pool_w=v_pool_w, pool_scale=v_pool_scale, ffn_w_gate=v_ffn_w_gate,
              ffn_w_up=v_ffn_w_up, ffn_w_down=v_ffn_w_down)
    order = list(wts.keys())
    bl = x.shape[0]
    nseq = bl * N_DEV
    me = _my_index()
    chip = me // 2
    ncol = w_mod.shape[2]

    (c_all,) = _spread([(c, ALL7, 8)], "ag_c")
    nrow = nseq + 8
    c_rows = jnp.concatenate([c_all.reshape(nseq, D), c_ctx[None], jnp.zeros((7, D), F32)], axis=0)
    b_cols = lax.dynamic_slice_in_dim(b_mod, chip * ncol, ncol, axis=1)[:, None, :]
    mod_cols = _mod_fwd(c_rows, w_mod, b_cols, "mod_fwd")
    (mod4,) = _spread([(mod_cols, CHIPS3, 4)], "ag_mod")
    mods = jnp.transpose(mod4, (1, 2, 0, 3)).reshape(2, nrow, 6 * D)
    mods_local = jnp.concatenate([lax.dynamic_slice_in_dim(mods, me * bl, bl, axis=1), mods[:, nseq:nseq + 1],
                                  jnp.zeros((2, 8 - bl - 1, 6 * D), F32)], axis=1)

    shards = []
    for i in range(2):
        shards += [w_in[i].T, w_out[i], glu_w[i], ffn_w_gate[i].T, ffn_w_up[i].T, ffn_w_down[i]]
    gathered = _comm([(s.astype(MXU_DTYPE), "gather", CHIPS3, 4) for s in shards], "ag_weights")
    weights = []
    for i in range(2):
        weights.append({nme: g.reshape(-1, g.shape[-1]) for nme, g in zip(BIG_NAMES, gathered[6 * i:6 * i + 6])})
    params = [{k: wts[k][i] for k in SMALL_NAMES} for i in range(2)]

    loss_part, grad_x, bigs, smalls, dmods = _core(x, ctx, loss_target, mods_local, params, weights)
    loss = lax.psum(loss_part, ("x", "y", "c"))

    dmod_local = jnp.stack([dmods[i].reshape(8, 6 * D) for i in range(2)])
    (dmod_all,) = _spread([(dmod_local, ALL7, 8)], "ag_dmod")
    dcols = lax.dynamic_slice_in_dim(dmod_all, chip * ncol, ncol, axis=3)
    dlat = jnp.transpose(dcols[:, :, :bl], (1, 0, 2, 3)).reshape(2, nseq, ncol)
    dlat = jnp.concatenate([dlat, jnp.zeros((2, 8, ncol), F32)], axis=1)
    dctx8 = dcols[:, :, bl:bl + 1]
    g_w_mod, gc_part = _mod_bwd(c_rows, w_mod, dlat, dctx8, "mod_bwd")
    (gc4,) = _spread([(gc_part, CHIPS3, 4)], "ag_cctx")
    g_b_mod, g_c_ctx = _bmod_cctx(dmod_all, gc4, c_ctx[None], "bmod_cctx")

    small_flat = jnp.concatenate([jnp.stack([smalls[i][k] for i in range(2)]).reshape(-1) for k in SMALL_NAMES])
    npad = (-small_flat.shape[0]) % (8 * 1024)
    small_flat = jnp.concatenate([small_flat, jnp.zeros((npad,), F32)])
    a2a_items = []
    for i in range(2):
        for k in BIG_NAMES:
            g = bigs[i][k]
            a2a_items.append((g.reshape(8, g.shape[0] // 8, g.shape[1]), "a2a", ALL7, 8))
    a2a_items.append((small_flat.reshape(8, -1, 1024), "a2a", ALL7, 8))
    pieces = _comm(a2a_items, "a2a_grads")
    sums = _sum8(pieces, "gsum")
    fin = _spread([(s, (1,), 2) for s in sums[:-1]] + [(sums[-1], ALL7, 8)], "ag_grads")
    big_g = [{k: fin[6 * i + j].reshape(-1, fin[6 * i + j].shape[-1]) for j, k in enumerate(BIG_NAMES)} for i in range(2)]
    small_red = fin[-1].reshape(-1)

    grads = {}
    off = 0
    for k in SMALL_NAMES:
        shp = wts[k].shape
        nel = int(np.prod(shp))
        grads[k] = small_red[off:off + nel].reshape(shp)
        off += nel
    grads["c_ctx"] = g_c_ctx[0]
    grads["w_mod"] = g_w_mod
    grads["b_mod"] = g_b_mod
    grads["w_in"] = jnp.stack([big_g[i]["win_t"].T for i in range(2)])
    grads["w_out"] = jnp.stack([big_g[i]["wout"] for i in range(2)])
    grads["glu_w"] = jnp.stack([big_g[i]["wglu"] for i in range(2)])
    grads["ffn_w_gate"] = jnp.stack([big_g[i]["wg_t"].T for i in range(2)])
    grads["ffn_w_up"] = jnp.stack([big_g[i]["wu_t"].T for i in range(2)])
    grads["ffn_w_down"] = jnp.stack([big_g[i]["wd"] for i in range(2)])

    deltas, new_m, new_v = {}, {}, {}
    for k in order:
        deltas[k], new_m[k], new_v[k] = _adam(wts[k], grads[k], ms[k], vs[k], "adam_" + k)
    return (loss, grad_x, *[grads[k] for k in order], *[deltas[k] for k in order],
            *[new_m[k] for k in order], *[new_v[k] for k in order])
```

```python
import functools
import math

import numpy as np
import jax
import jax.numpy as jnp
from jax import lax
from jax.experimental import pallas as pl
from jax.experimental.pallas import tpu as pltpu

F32 = jnp.float32
BF16 = jnp.bfloat16
MXU_DTYPE = jnp.bfloat16
MCAT_A, MCAT_C = 2, 3
WOUT_PERM, WOUT_INV = (1, 2, 0, 3), (2, 0, 1, 3)

D = 1024
EPS = 1e-6
TB = 256
CTX = 256
CHUNK = 128
GRID_W = 64
A_W, B_W, C_W = 256, 512, 256
D_IN = 1280
D_FF = 2816
SSM_G, SSM_P, SSM_H = 32, 64, 16
ST = 64
POOL_WINDOWS = (2, 4, 8, 16)
N_DEV = 8
VMEM_LIMIT = 52 * 1024 * 1024
GELU_C = math.sqrt(2.0 / math.pi)

ADAM_LR, ADAM_B1, ADAM_B2, ADAM_EPS, ADAM_WD, ADAM_STEP = 0.001, 0.9, 0.999, 1e-08, 0.01, 10


def _cp(sem=None, vmem=VMEM_LIMIT, **kw):
    return pltpu.CompilerParams(dimension_semantics=sem, vmem_limit_bytes=vmem, **kw)


def _pick(n, cap):
    if n <= cap:
        return n
    best = None
    for t in range(128, cap + 1, 128):
        if n % t == 0:
            best = t
    assert best is not None, (n, cap)
    return best


def _gelu(x):
    return 0.5 * x * (1.0 + jnp.tanh(GELU_C * (x + 0.044715 * x * x * x)))


def _gelu_grad(x):
    t = jnp.tanh(GELU_C * (x + 0.044715 * x * x * x))
    return 0.5 * (1.0 + t) + 0.5 * x * (1.0 - t * t) * GELU_C * (1.0 + 3.0 * 0.044715 * x * x)


def _sigmoid(x):
    return 1.0 / (1.0 + jnp.exp(-x))


def _dot(a, b, dims):
    return lax.dot_general(a, b, (dims, ((), ())), preferred_element_type=F32)


def _nn(a, b):
    return _dot(a, b, ((1,), (0,)))


def _nt(a, b):
    return _dot(a, b, ((1,), (1,)))


def _tn(a, b):
    return _dot(a, b, ((0,), (0,)))


def _mm(pairs, nt, out_dtype, name, tm=512):
    m = pairs[0][0].shape[0]
    n = pairs[0][1].shape[0] if nt else pairs[0][1].shape[1]
    tn = _pick(n, 1408)
    tm = min(tm, m)
    npairs = len(pairs)

    def body(*refs):
        o_ref = refs[-1]
        acc = None
        for i in range(npairs):
            a = refs[2 * i][...].astype(MXU_DTYPE)
            b = refs[2 * i + 1][...].astype(MXU_DTYPE)
            r = _nt(a, b) if nt else _nn(a, b)
            acc = r if acc is None else acc + r
        o_ref[...] = acc.astype(o_ref.dtype)

    in_specs, flat = [], []
    for a, b in pairs:
        k = a.shape[1]
        in_specs.append(pl.BlockSpec((tm, k), lambda i, j: (i, 0)))
        in_specs.append(pl.BlockSpec((tn, k), lambda i, j: (j, 0)) if nt else pl.BlockSpec((k, tn), lambda i, j: (0, j)))
        flat += [a, b]
    return pl.pallas_call(
        body, grid=(m // tm, n // tn), in_specs=in_specs,
        out_specs=pl.BlockSpec((tm, tn), lambda i, j: (i, j)),
        out_shape=jax.ShapeDtypeStruct((m, n), out_dtype),
        compiler_params=_cp(("parallel", "parallel")), name=name)(*flat)


def _mm_tn(a, b, out_dtype, name, tm=512):
    m, k1 = a.shape
    n = b.shape[1]
    t1 = _pick(k1, 1408)
    tn = _pick(n, 1024)
    tm = min(tm, m)
    nsteps = m // tm

    def body(a_ref, b_ref, o_ref, acc_ref):
        t = pl.program_id(2)

        @pl.when(t == 0)
        def _():
            acc_ref[...] = jnp.zeros_like(acc_ref)

        acc_ref[...] += _tn(a_ref[...].astype(MXU_DTYPE), b_ref[...].astype(MXU_DTYPE))

        @pl.when(t == nsteps - 1)
        def _():
            o_ref[...] = acc_ref[...].astype(o_ref.dtype)

    return pl.pallas_call(
        body, grid=(k1 // t1, n // tn, nsteps),
        in_specs=[pl.BlockSpec((tm, t1), lambda i, j, t: (t, i)), pl.BlockSpec((tm, tn), lambda i, j, t: (t, j))],
        out_specs=pl.BlockSpec((t1, tn), lambda i, j, t: (i, j)),
        out_shape=jax.ShapeDtypeStruct((k1, n), out_dtype),
        scratch_shapes=[pltpu.VMEM((t1, tn), F32)],
        compiler_params=_cp(("parallel", "parallel", "arbitrary")), name=name)(a, b)


class _Layout:
    def __init__(self, bl, lat):
        self.bl, self.lat = bl, lat
        self.nlb = lat // TB
        self.nr = 1 + self.nlb
        self.nctx = bl
        self.nb = self.nr * bl
        self.nt = self.nb * TB
        self.ctx_row = bl

    def blk(self, g):
        gg = g - self.bl
        return jnp.where(g < self.bl, g, (gg % self.nlb + 1) * self.bl + gg // self.nlb)

    def modrow(self, g):
        return jnp.where(g < self.bl, self.ctx_row, (g - self.bl) // self.nlb)

    def first_of_row(self, g):
        return jnp.logical_or(g == 0, jnp.logical_and(g >= self.bl, (g - self.bl) % self.nlb == 0))

    def modrows_static(self):
        return np.array([self.ctx_row if j < self.bl else j % self.bl for j in range(self.nb)], np.int32)


def _tok_spec(lay):
    return pl.BlockSpec((TB, D), lambda g: (lay.blk(g), 0))


def _vec_spec():
    return pl.BlockSpec((1, D), lambda j: (0, 0))


def _mod_spec(lay, k):
    return pl.BlockSpec((1, 1, D), lambda g: (lay.blk(g) * 6 + k, 0, 0))


def _embed(lay, x2d, ctx2d, pe):
    bl, nlb = lay.bl, lay.nlb

    def body(x_ref, c_ref, pe_ref, o_ref):
        j = pl.program_id(0)

        @pl.when(j < bl)
        def _():
            o_ref[...] = c_ref[...]

        @pl.when(j >= bl)
        def _():
            o_ref[...] = x_ref[...] + pe_ref[...]

    pos = lambda j: jnp.maximum(j // bl - 1, 0)
    return pl.pallas_call(
        body, grid=(lay.nb,),
        in_specs=[pl.BlockSpec((TB, D), lambda j: ((j % bl) * nlb + pos(j), 0)),
                  pl.BlockSpec((TB, D), lambda j: (jnp.minimum(j, bl - 1), 0)),
                  pl.BlockSpec((TB, D), lambda j: (pos(j), 0))],
        out_specs=pl.BlockSpec((TB, D), lambda j: (j, 0)), out_shape=jax.ShapeDtypeStruct((lay.nt, D), F32),
        compiler_params=_cp(("parallel",)), name="embed")(x2d, ctx2d, pe)


def _normmod_fwd(lay, x, gain, modarr, ksh, ksc, name):
    def body(x_ref, g_ref, sh_ref, sc_ref, o_ref):
        xv = x_ref[...]
        r = lax.rsqrt(jnp.mean(xv * xv, axis=-1, keepdims=True) + EPS)
        o_ref[...] = ((xv * r * g_ref[...]) * (1.0 + sc_ref[0]) + sh_ref[0]).astype(o_ref.dtype)

    return pl.pallas_call(
        body, grid=(lay.nb,), in_specs=[_tok_spec(lay), _vec_spec(), _mod_spec(lay, ksh), _mod_spec(lay, ksc)],
        out_specs=_tok_spec(lay), out_shape=jax.ShapeDtypeStruct((lay.nt, D), MXU_DTYPE),
        compiler_params=_cp(("parallel",)), name=name)(x, gain, modarr, modarr)


def _acc_specs(lay):
    row = pl.BlockSpec((1, 1, D), lambda j: (lay.modrow(j), 0, 0))
    return row, jax.ShapeDtypeStruct((8, 1, D), F32)


def _normmod_bwd(lay, x, dh, dx_in, gain, modarr, ksc, name, latent_only=False):
    row_spec, row_shape = _acc_specs(lay)
    if latent_only:
        dx_spec = pl.BlockSpec((TB, D), lambda g: (jnp.maximum(g - lay.bl, 0), 0))
        dx_shape = jax.ShapeDtypeStruct((lay.bl * lay.lat, D), F32)
    else:
        dx_spec, dx_shape = _tok_spec(lay), jax.ShapeDtypeStruct((lay.nt, D), F32)

    def body(x_ref, dh_ref, dxi_ref, g_ref, sc_ref, dx_ref, dsh_ref, dsc_ref, dg_ref):
        j = pl.program_id(0)
        xv = x_ref[...]
        dhv = dh_ref[...].astype(F32)
        g = g_ref[...]
        sc1 = 1.0 + sc_ref[0]
        r = lax.rsqrt(jnp.mean(xv * xv, axis=-1, keepdims=True) + EPS)
        xh = xv * r
        dxh = dhv * (g * sc1)
        dx = r * (dxh - xh * jnp.mean(dxh * xh, axis=-1, keepdims=True))
        dx_ref[...] = dxi_ref[...] + dx

        @pl.when(lay.first_of_row(j))
        def _():
            dsh_ref[...] = jnp.zeros_like(dsh_ref)
            dsc_ref[...] = jnp.zeros_like(dsc_ref)

        @pl.when(j == 0)
        def _():
            dg_ref[...] = jnp.zeros_like(dg_ref)

        dsh_ref[0] += jnp.sum(dhv, axis=0, keepdims=True)
        dsc_ref[0] += jnp.sum(dhv * (xh * g), axis=0, keepdims=True)
        dg_ref[...] += jnp.sum(dhv * sc1 * xh, axis=0, keepdims=True)

    return pl.pallas_call(
        body, grid=(lay.nb,),
        in_specs=[_tok_spec(lay), _tok_spec(lay), _tok_spec(lay), _vec_spec(), _mod_spec(lay, ksc)],
        out_specs=[dx_spec, row_spec, row_spec, _vec_spec()],
        out_shape=[dx_shape, row_shape, row_shape, jax.ShapeDtypeStruct((1, D), F32)],
        compiler_params=_cp(("arbitrary",)), name=name)(x, dh, dx_in, gain, modarr)


def _resnorm_fwd(lay, x, m, gain, modarr, kgate, name):
    def body(x_ref, m_ref, g_ref, gate_ref, o_ref):
        mv = m_ref[...]
        r = lax.rsqrt(jnp.mean(mv * mv, axis=-1, keepdims=True) + EPS)
        o_ref[...] = x_ref[...] + gate_ref[0] * (mv * r * g_ref[...])

    return pl.pallas_call(
        body, grid=(lay.nb,), in_specs=[_tok_spec(lay), _tok_spec(lay), _vec_spec(), _mod_spec(lay, kgate)],
        out_specs=_tok_spec(lay), out_shape=jax.ShapeDtypeStruct((lay.nt, D), F32),
        compiler_params=_cp(("parallel",)), name=name)(x, m, gain, modarr)


def _resnorm_bwd(lay, dxn, m, gain, modarr, kgate, name):
    row_spec, row_shape = _acc_specs(lay)

    def body(d_ref, m_ref, g_ref, gate_ref, dm_ref, dgate_ref, dg_ref):
        j = pl.program_id(0)
        dv = d_ref[...]
        mv = m_ref[...]
        g = g_ref[...]
        r = lax.rsqrt(jnp.mean(mv * mv, axis=-1, keepdims=True) + EPS)
        xh = mv * r
        dy = dv * gate_ref[0]
        dxh = dy * g
        dm_ref[...] = (r * (dxh - xh * jnp.mean(dxh * xh, axis=-1, keepdims=True))).astype(dm_ref.dtype)

        @pl.when(lay.first_of_row(j))
        def _():
            dgate_ref[...] = jnp.zeros_like(dgate_ref)

        @pl.when(j == 0)
        def _():
            dg_ref[...] = jnp.zeros_like(dg_ref)

        dgate_ref[0] += jnp.sum(dv * (xh * g), axis=0, keepdims=True)
        dg_ref[...] += jnp.sum(dy * xh, axis=0, keepdims=True)

    return pl.pallas_call(
        body, grid=(lay.nb,), in_specs=[_tok_spec(lay), _tok_spec(lay), _vec_spec(), _mod_spec(lay, kgate)],
        out_specs=[_tok_spec(lay), row_spec, _vec_spec()],
        out_shape=[jax.ShapeDtypeStruct((lay.nt, D), MXU_DTYPE), row_shape, jax.ShapeDtypeStruct((1, D), F32)],
        compiler_params=_cp(("arbitrary",)), name=name)(dxn, m, gain, modarr)


def _loss_bwd(lay, xf, tgt2d):
    bl, nlb = lay.bl, lay.nlb

    def body(x_ref, t_ref, dx_ref, l_ref):
        j = pl.program_id(0)

        @pl.when(j == 0)
        def _():
            l_ref[...] = jnp.zeros_like(l_ref)

        @pl.when(j < bl)
        def _():
            dx_ref[...] = jnp.zeros_like(dx_ref)

        @pl.when(j >= bl)
        def _():
            e = x_ref[...] - t_ref[...]
            dx_ref[...] = e * (1.0 / D)
            l_ref[...] += jnp.sum(e * e) * (0.5 / D)

    tok = pl.BlockSpec((TB, D), lambda j: (j, 0))
    return pl.pallas_call(
        body, grid=(lay.nb,),
        in_specs=[tok, pl.BlockSpec((TB, D), lambda j: ((j % bl) * nlb + jnp.maximum(j // bl - 1, 0), 0))],
        out_specs=[tok, pl.BlockSpec((8, 128), lambda j: (0, 0))],
        out_shape=[jax.ShapeDtypeStruct((lay.nt, D), F32), jax.ShapeDtypeStruct((8, 128), F32)],
        compiler_params=_cp(("arbitrary",)), name="loss")(xf, tgt2d)


def _ffn_up(h, wgt, wut, name):
    m = h.shape[0]
    tm, tn = min(512, m), D_FF // 2

    def body(h_ref, wg_ref, wu_ref, g_ref, u_ref, a_ref):
        hv = h_ref[...]
        g = _nt(hv, wg_ref[...])
        u = _nt(hv, wu_ref[...])
        g_ref[...] = g.astype(g_ref.dtype)
        u_ref[...] = u.astype(u_ref.dtype)
        a_ref[...] = (g * _sigmoid(g) * u).astype(a_ref.dtype)

    osp = pl.BlockSpec((tm, tn), lambda i, j: (i, j))
    osh = jax.ShapeDtypeStruct((m, D_FF), MXU_DTYPE)
    return pl.pallas_call(
        body, grid=(m // tm, D_FF // tn),
        in_specs=[pl.BlockSpec((tm, D), lambda i, j: (i, 0)), pl.BlockSpec((tn, D), lambda i, j: (j, 0)),
                  pl.BlockSpec((tn, D), lambda i, j: (j, 0))],
        out_specs=[osp, osp, osp], out_shape=[osh, osh, osh],
        compiler_params=_cp(("parallel", "parallel")), name=name)(h, wgt, wut)


def _ffn_down_bwd(df, wd, g, u, name):
    m = df.shape[0]
    tm, tn = min(512, m), D_FF // 2

    def body(df_ref, wd_ref, g_ref, u_ref, dg_ref, du_ref):
        da = _nt(df_ref[...], wd_ref[...])
        gv = g_ref[...].astype(F32)
        uv = u_ref[...].astype(F32)
        s = _sigmoid(gv)
        dg_ref[...] = (da * uv * (s * (1.0 + gv * (1.0 - s)))).astype(dg_ref.dtype)
        du_ref[...] = (da * gv * s).astype(du_ref.dtype)

    osp = pl.BlockSpec((tm, tn), lambda i, j: (i, j))
    osh = jax.ShapeDtypeStruct((m, D_FF), MXU_DTYPE)
    return pl.pallas_call(
        body, grid=(m // tm, D_FF // tn),
        in_specs=[pl.BlockSpec((tm, D), lambda i, j: (i, 0)), pl.BlockSpec((tn, D), lambda i, j: (j, 0)), osp, osp],
        out_specs=[osp, osp], out_shape=[osh, osh],
        compiler_params=_cp(("parallel", "parallel")), name=name)(df, wd, g, u)


def _head_masks(shape):
    lane = lax.broadcasted_iota(jnp.int32, shape, 1)
    return [jnp.logical_and(lane >= 64 * h, lane < 64 * h + 64) for h in range(4)]


def _head_mean(x, masks):
    out = jnp.zeros_like(x)
    for mk in masks:
        s = jnp.sum(jnp.where(mk, x, 0.0), axis=-1, keepdims=True) * (1.0 / 64.0)
        out = jnp.where(mk, s, out)
    return out


def _gate_common(z, masks):
    zg = _gelu(z)
    u = zg[:, :A_W]
    v = zg[:, A_W:]
    mu = _head_mean(v, masks)
    vc = v - mu
    rstd = lax.rsqrt(_head_mean(vc * vc, masks) + EPS)
    return u, vc * rstd, rstd


def _gate_s(vn, ws_ref, bias, masks):
    parts = []
    for c in range(TB // CHUNK):
        vc = vn[c * CHUNK:(c + 1) * CHUNK]
        s = bias
        for h in range(4):
            s = s + _nn(ws_ref[h], jnp.where(masks[h][:CHUNK], vc, 0.0).astype(MXU_DTYPE))
        parts.append(s)
    return jnp.concatenate(parts, axis=0)


def _gate_fwd(lay, z, ws, bias, name):
    def body(z_ref, ws_ref, b_ref, o_ref):
        masks = _head_masks((TB, A_W))
        u, vn, _ = _gate_common(z_ref[...], masks)
        o_ref[...] = (u * _gate_s(vn, ws_ref, b_ref[...], masks)).astype(o_ref.dtype)

    return pl.pallas_call(
        body, grid=(lay.nb,),
        in_specs=[pl.BlockSpec((TB, 2 * A_W), lambda j: (j, 0)), pl.BlockSpec((4, CHUNK, CHUNK), lambda j: (0, 0, 0)),
                  pl.BlockSpec((CHUNK, A_W), lambda j: (0, 0))],
        out_specs=pl.BlockSpec((TB, A_W), lambda j: (j, 0)),
        out_shape=jax.ShapeDtypeStruct((lay.nt, A_W), MXU_DTYPE),
        compiler_params=_cp(("parallel",)), name=name)(z, ws, bias)


def _gate_bwd(lay, z, da, ws, wst, bias, name):
    def body(z_ref, da_ref, ws_ref, wst_ref, b_ref, dz_ref, dws_ref, db_ref):
        j = pl.program_id(0)

        @pl.when(j == 0)
        def _():
            dws_ref[...] = jnp.zeros_like(dws_ref)
            db_ref[...] = jnp.zeros_like(db_ref)

        masks = _head_masks((TB, A_W))
        zv = z_ref[...]
        u, vn, rstd = _gate_common(zv, masks)
        s = _gate_s(vn, ws_ref, b_ref[...], masks)
        dav = da_ref[...].astype(F32)
        du = dav * s
        ds = dav * u
        dvn_parts = []
        for c in range(TB // CHUNK):
            sl = slice(c * CHUNK, (c + 1) * CHUNK)
            ds_c = ds[sl]
            vn_c = vn[sl].astype(MXU_DTYPE)
            db_ref[...] += ds_c
            ds_b = ds_c.astype(MXU_DTYPE)
            dvn_c = jnp.zeros((CHUNK, A_W), F32)
            for h in range(4):
                mk = masks[h][:CHUNK]
                dws_ref[h] += _nt(jnp.where(mk, ds_c, 0.0).astype(MXU_DTYPE), vn_c)
                dvn_c = dvn_c + jnp.where(mk, _nn(wst_ref[h], ds_b), 0.0)
            dvn_parts.append(dvn_c)
        dvn = jnp.concatenate(dvn_parts, axis=0)
        dv = rstd * (dvn - _head_mean(dvn, masks) - vn * _head_mean(dvn * vn, masks))
        gg = _gelu_grad(zv)
        dz_ref[:, :A_W] = (du * gg[:, :A_W]).astype(dz_ref.dtype)
        dz_ref[:, A_W:] = (dv * gg[:, A_W:]).astype(dz_ref.dtype)

    return pl.pallas_call(
        body, grid=(lay.nb,),
        in_specs=[pl.BlockSpec((TB, 2 * A_W), lambda j: (j, 0)), pl.BlockSpec((TB, A_W), lambda j: (j, MCAT_A)),
                  pl.BlockSpec((4, CHUNK, CHUNK), lambda j: (0, 0, 0)), pl.BlockSpec((4, CHUNK, CHUNK), lambda j: (0, 0, 0)),
                  pl.BlockSpec((CHUNK, A_W), lambda j: (0, 0))],
        out_specs=[pl.BlockSpec((TB, 2 * A_W), lambda j: (j, 0)), pl.BlockSpec((4, CHUNK, CHUNK), lambda j: (0, 0, 0)),
                   pl.BlockSpec((CHUNK, A_W), lambda j: (0, 0))],
        out_shape=[jax.ShapeDtypeStruct((lay.nt, 2 * A_W), MXU_DTYPE), jax.ShapeDtypeStruct((4, CHUNK, CHUNK), F32),
                   jax.ShapeDtypeStruct((CHUNK, A_W), F32)],
        compiler_params=_cp(("arbitrary",)), name=name)(z, da, ws, wst, bias)


def _band_constants():
    bands = np.zeros((2, 4, TB, TB), np.float32)
    inv = np.zeros((2, 4, TB, 1), np.float32)
    for kind, n in ((0, GRID_W), (1, TB)):
        for i, w in enumerate(POOL_WINDOWS):
            for t in range(TB):
                base, tt = (t // n) * n, t % n
                lo = min(max(tt - w // 2, 0), n)
                hi = min(max(tt - w // 2 + w, 0), n)
                bands[kind, i, t, base + lo:base + hi] = 1.0
                inv[kind, i, t, 0] = 1.0 / (hi - lo)
    return bands, inv


def _split3(x):
    a = x.astype(MXU_DTYPE)
    r1 = x - a.astype(F32)
    b = r1.astype(MXU_DTYPE)
    c = (r1 - b.astype(F32)).astype(MXU_DTYPE)
    return a, b, c


def _window_apply(band_ref, inv_ref, x, masks, transpose):
    out = jnp.zeros_like(x)
    for i in range(4):
        xi = x * inv_ref[0, i] if transpose else x
        acc = None
        for part in _split3(xi):
            r = _tn(band_ref[0, i], part) if transpose else _nn(band_ref[0, i], part)
            acc = r if acc is None else acc + r
        if not transpose:
            acc = acc * inv_ref[0, i]
        out = jnp.where(masks[i], acc, out)
    return out


def _pool_specs(lay):
    kind = lambda j: jnp.where(j < lay.nctx, 1, 0)
    return [pl.BlockSpec((1, 4, TB, TB), lambda j: (kind(j), 0, 0, 0)), pl.BlockSpec((1, 4, TB, 1), lambda j: (kind(j), 0, 0, 0))]


def _pool_fwd(lay, z, bands, inv, pw, scale, name):
    def body(p_ref, band_ref, inv_ref, pw_ref, sc_ref, o_ref):
        masks = _head_masks((TB, C_W))
        p = p_ref[...]
        diff = _window_apply(band_ref, inv_ref, p, masks, False) - p
        o_ref[...] = (_nn(diff.astype(MXU_DTYPE), pw_ref[...]) * sc_ref[...]).astype(o_ref.dtype)

    return pl.pallas_call(
        body, grid=(lay.nb,),
        in_specs=[pl.BlockSpec((TB, C_W), lambda j: (j, 4))] + _pool_specs(lay)
        + [pl.BlockSpec((C_W, C_W), lambda j: (0, 0)), pl.BlockSpec((1, C_W), lambda j: (0, 0))],
        out_specs=pl.BlockSpec((TB, C_W), lambda j: (j, 0)),
        out_shape=jax.ShapeDtypeStruct((lay.nt, C_W), MXU_DTYPE),
        compiler_params=_cp(("parallel",)), name=name)(z, bands, inv, pw, scale)


def _pool_bwd(lay, z, dc, bands, inv, pw, scale, name):
    def body(p_ref, dc_ref, band_ref, inv_ref, pw_ref, sc_ref, dp_ref, dpw_ref, dsc_ref):
        j = pl.program_id(0)

        @pl.when(j == 0)
        def _():
            dpw_ref[...] = jnp.zeros_like(dpw_ref)
            dsc_ref[...] = jnp.zeros_like(dsc_ref)

        masks = _head_masks((TB, C_W))
        p = p_ref[...]
        dcv = dc_ref[...].astype(F32)
        diff = _window_apply(band_ref, inv_ref, p, masks, False) - p
        diff_b = diff.astype(MXU_DTYPE)
        pre = _nn(diff_b, pw_ref[...])
        dsc_ref[...] += jnp.sum(dcv * pre, axis=0, keepdims=True)
        dpre = dcv * sc_ref[...]
        dpre_b = dpre.astype(MXU_DTYPE)
        dpw_ref[...] += _tn(diff_b, dpre_b)
        ddiff = _nt(dpre_b, pw_ref[...])
        dp_ref[...] = (_window_apply(band_ref, inv_ref, ddiff, masks, True) - ddiff).astype(dp_ref.dtype)

    return pl.pallas_call(
        body, grid=(lay.nb,),
        in_specs=[pl.BlockSpec((TB, C_W), lambda j: (j, 4)), pl.BlockSpec((TB, C_W), lambda j: (j, MCAT_C))] + _pool_specs(lay)
        + [pl.BlockSpec((C_W, C_W), lambda j: (0, 0)), pl.BlockSpec((1, C_W), lambda j: (0, 0))],
        out_specs=[pl.BlockSpec((TB, C_W), lambda j: (j, 0)), pl.BlockSpec((C_W, C_W), lambda j: (0, 0)),
                   pl.BlockSpec((1, C_W), lambda j: (0, 0))],
        out_shape=[jax.ShapeDtypeStruct((lay.nt, C_W), MXU_DTYPE), jax.ShapeDtypeStruct((C_W, C_W), F32),
                   jax.ShapeDtypeStruct((1, C_W), F32)],
        compiler_params=_cp(("arbitrary",)), name=name)(z, dc, bands, inv, pw, scale)


def _disc_math(lr, li, ldt, br, bi):
    dt = jnp.exp(ldt)
    e = jnp.exp(lr * dt)
    ar = e * jnp.cos(li * dt)
    ai = e * jnp.sin(li * dt)
    nr, ni = ar - 1.0, ai
    den = lr * lr + li * li
    qr = (nr * lr + ni * li) / den
    qi = (ni * lr - nr * li) / den
    return ar, ai, qr * br - qi * bi, qr * bi + qi * br


def _disc_fwd(lrx, lix, ldtx, brt, bit, name):
    def body(lr_ref, li_ref, ldt_ref, br_ref, bi_ref, ar_ref, ai_ref, obr_ref, obi_ref):
        ar, ai, obr, obi = _disc_math(lr_ref[...], li_ref[...], ldt_ref[...], br_ref[...], bi_ref[...])
        ar_ref[...] = ar
        ai_ref[...] = ai
        obr_ref[...] = obr
        obi_ref[...] = obi

    sh = jax.ShapeDtypeStruct(lrx.shape, F32)
    return pl.pallas_call(body, out_shape=[sh, sh, sh, sh], name=name)(lrx, lix, ldtx, brt, bit)


def _disc_bwd(lrx, lix, ldtx, brt, bit, dar, dai, dbr, dbi, name):
    nrow = lrx.shape[0] // SSM_H

    def body(lr_ref, li_ref, ldt_ref, br_ref, bi_ref, dar_ref, dai_ref, dbr_ref, dbi_ref,
             glr_ref, gli_ref, gdt_ref, gbr_ref, gbi_ref):
        _, vjp = jax.vjp(_disc_math, lr_ref[...], li_ref[...], ldt_ref[...], br_ref[...], bi_ref[...])
        glr, gli, gdt, gbr, gbi = vjp((dar_ref[...], dai_ref[...], dbr_ref[...], dbi_ref[...]))
        glr_ref[...] = jnp.sum(glr.reshape(nrow, SSM_H, SSM_P), axis=1)
        gli_ref[...] = jnp.sum(gli.reshape(nrow, SSM_H, SSM_P), axis=1)
        gdt_ref[...] = jnp.sum(jnp.sum(gdt.reshape(nrow, SSM_H, SSM_P), axis=1), axis=-1, keepdims=True)
        gbr_ref[...] = gbr
        gbi_ref[...] = gbi

    small = jax.ShapeDtypeStruct((nrow, SSM_P), F32)
    big = jax.ShapeDtypeStruct(lrx.shape, F32)
    return pl.pallas_call(body, out_shape=[small, small, jax.ShapeDtypeStruct((nrow, 1), F32), big, big],
                          name=name)(lrx, lix, ldtx, brt, bit, dar, dai, dbr, dbi)


HS = 1024
GQ, QC, QS = 8, 128, 512
LC = QS


def _dir_cat(x, d0, qq):
    xq = x[:, QC * qq:QC * qq + QC]
    zero = jnp.zeros_like(xq)
    return jnp.concatenate([jnp.where(d0, xq, zero), jnp.where(d0, zero, xq)], axis=1)


def _dir_pick(x, d0):
    return jnp.where(d0, x[:, :QC], x[:, QC:])


def _d0_rows(n):
    row = lax.broadcasted_iota(jnp.int32, (n, 1), 0)
    return jnp.bitwise_and(row, 4) == 0


def _scan_perm(bl):
    n = 2 * bl * ST
    p = np.zeros((n, n), np.float32)
    for s in range(ST):
        for d in range(2):
            for b in range(bl):
                t = s if d == 0 else ST - 1 - s
                p[s * 2 * bl + d * bl + b, d * bl * ST + b * ST + t] = 1.0
    return p


def _scan_maps(lay):
    spc = TB // ST
    nlc = lay.nlb * spc

    def fwd(k):
        return k // spc, k % spc

    def rev(k):
        cpos = nlc - 1 - jnp.maximum(k - spc, 0)
        return jnp.where(k < spc, 0, 1 + cpos // spc), jnp.where(k < spc, spc - 1 - k, cpos % spc)

    return fwd, rev


def _pack_rows(f_ref, r_ref, p_ref, rc):
    st = jnp.concatenate([f_ref[0].reshape(rc // 2, 256), r_ref[0].reshape(rc // 2, 256)], axis=0).astype(MXU_DTYPE)
    return _nn(p_ref[...], st).astype(MXU_DTYPE)


def _side_split(refs, n_in, n_out, n_scr, side):
    ns = side.n if side is not None else 0
    ins, sin = refs[:n_in], refs[n_in:n_in + ns]
    o0 = n_in + ns
    outs, sout = refs[o0:o0 + n_out], refs[o0 + n_out:o0 + n_out + ns]
    s0 = o0 + n_out + ns
    return ins + outs + refs[s0:s0 + n_scr], (sin, sout, refs[s0 + n_scr:])


def _side_start(side, srefs, first):
    if side is not None:
        @pl.when(first)
        def _():
            side.start(*srefs)


def _side_wait(side, srefs, last):
    if side is not None:
        @pl.when(last)
        def _():
            side.wait(*srefs)


def _ssm_fwd(lay, z, perm, bh, ch, ar8, ai8, name, side=None):
    bl = lay.bl
    rc = ST * 2 * bl
    nch = lay.nr * (TB // ST)
    fwd, rev = _scan_maps(lay)
    z4 = z.reshape(lay.nr, bl, TB, z.shape[1])

    def body(*refs):
        own, srefs = _side_split(refs, 7, 3, 2, side)
        uf_ref, ur_ref, p_ref, bh_ref, ch_ref, ar_ref, ai_ref, yf_ref, yr_ref, hst_ref, hs, hc = own
        f, k = pl.program_id(0), pl.program_id(1)
        _side_start(side, srefs, jnp.logical_and(f == 0, k == 0))

        @pl.when(k == 0)
        def _():
            hc[...] = jnp.zeros_like(hc)

        hst_ref[0] = hc[...]
        d0 = _d0_rows(rc)
        uv = _pack_rows(uf_ref, ur_ref, p_ref, rc)
        for q in range(2):
            cr, ci = 2 * QS * q, 2 * QS * q + QS
            hs[:, cr:cr + 2 * QS] = _nn(_dir_cat(uv, d0, q), bh_ref[q])
            ar = ar_ref[:, QS * q:QS * q + QS]
            ai = ai_ref[:, QS * q:QS * q + QS]

            def step(s, carry, cr=cr, ci=ci, ar=ar, ai=ai):
                hr, hi = carry
                base = pl.multiple_of(s * 8, 8)
                nr = ar * hr - ai * hi + hs[pl.ds(base, 8), cr:cr + LC]
                ni = ar * hi + ai * hr + hs[pl.ds(base, 8), ci:ci + LC]
                hs[pl.ds(base, 8), cr:cr + LC] = nr
                hs[pl.ds(base, 8), ci:ci + LC] = ni
                return nr, ni

            hr, hi = lax.fori_loop(0, ST, step, (hc[:, cr:cr + LC], hc[:, ci:ci + LC]), unroll=4)
            hc[:, cr:cr + LC] = hr
            hc[:, ci:ci + LC] = hi
        yi = jnp.concatenate(
            [_dir_pick(_nn(hs[:, 2 * QS * q:2 * QS * (q + 1)].astype(MXU_DTYPE), ch_ref[q]), d0) for q in range(2)], axis=1)
        yhi = yi.astype(MXU_DTYPE)
        ylo = (yi - yhi.astype(F32)).astype(MXU_DTYPE)
        yd = _tn(p_ref[...], yhi) + _tn(p_ref[...], ylo)
        yf_ref[0] = yd[:rc // 2].reshape(bl, ST, 256)
        yr_ref[0] = yd[rc // 2:].reshape(bl, ST, 256)
        _side_wait(side, srefs, jnp.logical_and(f == 1, k == nch - 1))

    sd = side if side is not None else _Side([])
    blk = (1, bl, ST, 256)
    ysh = jax.ShapeDtypeStruct((lay.nr, bl, TB, B_W), F32)
    outs = pl.pallas_call(
        body, grid=(2, nch),
        in_specs=[pl.BlockSpec(blk, lambda f, k: (fwd(k)[0], 0, fwd(k)[1], 2 + f)),
                  pl.BlockSpec(blk, lambda f, k: (rev(k)[0], 0, rev(k)[1], 2 + f)),
                  pl.BlockSpec((rc, rc), lambda f, k: (0, 0)),
                  pl.BlockSpec((2, 2 * QC, 2 * QS), lambda f, k: (f, 0, 0)),
                  pl.BlockSpec((2, 2 * QS, 2 * QC), lambda f, k: (f, 0, 0)),
                  pl.BlockSpec((8, HS), lambda f, k: (0, f)), pl.BlockSpec((8, HS), lambda f, k: (0, f))] + sd.in_specs,
        out_specs=[pl.BlockSpec(blk, lambda f, k: (fwd(k)[0], 0, fwd(k)[1], f)),
                   pl.BlockSpec(blk, lambda f, k: (rev(k)[0], 0, rev(k)[1], f)),
                   pl.BlockSpec((1, 8, 2 * HS), lambda f, k: (k, 0, f))] + sd.out_specs,
        out_shape=[ysh, ysh, jax.ShapeDtypeStruct((nch, 8, 4 * HS), F32)] + sd.out_shape,
        scratch_shapes=[pltpu.VMEM((rc, 2 * HS), F32), pltpu.VMEM((8, 2 * HS), F32)] + (sd.scratch if side is not None else []),
        compiler_params=_cp(("arbitrary", "arbitrary")), name=name)(z4, z4, perm, bh, ch, ar8, ai8, *sd.arrays)
    yf, yr, hst = outs[:3]
    return yf.reshape(lay.nt, B_W), yr.reshape(lay.nt, B_W), hst, list(outs[3:])


def _ssm_bwd(lay, z, dy, perm, hst, bh, ch, ar8, ai8, name, side=None):
    bl = lay.bl
    rc = ST * 2 * bl
    nch = lay.nr * (TB // ST)
    fwd, rev = _scan_maps(lay)
    z4 = z.reshape(lay.nr, bl, TB, z.shape[1])
    dy4 = dy.reshape(lay.nr, bl, TB, B_W)

    def body(*refs):
        own, srefs = _side_split(refs, 10, 6, 5, side)
        (uf_ref, ur_ref, dyf_ref, dyr_ref, p_ref, hst_ref, bh_ref, ch_ref, ar_ref, ai_ref,
         duf_ref, dur_ref, dbh_ref, dch_ref, dar_ref, dai_ref, hs, es, ec, accr, acci) = own
        f, k = pl.program_id(0), pl.program_id(1)
        _side_start(side, srefs, jnp.logical_and(f == 0, k == 0))

        @pl.when(k == 0)
        def _():
            ec[...] = jnp.zeros_like(ec)
            accr[...] = jnp.zeros_like(accr)
            acci[...] = jnp.zeros_like(acci)
            dbh_ref[...] = jnp.zeros_like(dbh_ref)
            dch_ref[...] = jnp.zeros_like(dch_ref)

        d0 = _d0_rows(rc)
        uv = _pack_rows(uf_ref, ur_ref, p_ref, rc)
        dyv = _pack_rows(dyf_ref, dyr_ref, p_ref, rc)

        hs[0:8, :] = hst_ref[0]
        ucat, dycat = [], []
        for q in range(2):
            cr, ci = 2 * QS * q, 2 * QS * q + QS
            ucat.append(_dir_cat(uv, d0, q))
            dycat.append(_dir_cat(dyv, d0, q))
            hs[8:, cr:cr + 2 * QS] = _nn(ucat[q], bh_ref[q])
            ar = ar_ref[:, QS * q:QS * q + QS]
            ai = ai_ref[:, QS * q:QS * q + QS]

            def step(s, carry, cr=cr, ci=ci, ar=ar, ai=ai):
                hr, hi = carry
                base = pl.multiple_of(s * 8 + 8, 8)
                nr = ar * hr - ai * hi + hs[pl.ds(base, 8), cr:cr + LC]
                ni = ar * hi + ai * hr + hs[pl.ds(base, 8), ci:ci + LC]
                hs[pl.ds(base, 8), cr:cr + LC] = nr
                hs[pl.ds(base, 8), ci:ci + LC] = ni
                return nr, ni

            lax.fori_loop(0, ST, step, (hs[0:8, cr:cr + LC], hs[0:8, ci:ci + LC]), unroll=4)
            dch_ref[q] += _tn(hs[8:, cr:cr + 2 * QS].astype(MXU_DTYPE), dycat[q])
            es[:, cr:cr + 2 * QS] = _nt(dycat[q], ch_ref[q])

        dui = []
        for q in range(2):
            cr, ci = 2 * QS * q, 2 * QS * q + QS
            ar = ar_ref[:, QS * q:QS * q + QS]
            ai = ai_ref[:, QS * q:QS * q + QS]

            def bstep(i, carry, cr=cr, ci=ci, ar=ar, ai=ai):
                er, ei, sr, si = carry
                base = pl.multiple_of((ST - 1 - i) * 8, 8)
                ner = es[pl.ds(base, 8), cr:cr + LC] + ar * er + ai * ei
                nei = es[pl.ds(base, 8), ci:ci + LC] - ai * er + ar * ei
                es[pl.ds(base, 8), cr:cr + LC] = ner
                es[pl.ds(base, 8), ci:ci + LC] = nei
                hpr = hs[pl.ds(base, 8), cr:cr + LC]
                hpi = hs[pl.ds(base, 8), ci:ci + LC]
                return ner, nei, sr + ner * hpr + nei * hpi, si - ner * hpi + nei * hpr

            lo = QS * q
            er, ei, sr, si = lax.fori_loop(
                0, ST, bstep, (ec[:, cr:cr + LC], ec[:, ci:ci + LC], accr[:, lo:lo + LC], acci[:, lo:lo + LC]), unroll=4)
            ec[:, cr:cr + LC] = er
            ec[:, ci:ci + LC] = ei
            accr[:, lo:lo + LC] = sr
            acci[:, lo:lo + LC] = si
            eb = es[:, cr:cr + 2 * QS].astype(MXU_DTYPE)
            dui.append(_dir_pick(_nt(eb, bh_ref[q]), d0))
            dbh_ref[q] += _tn(ucat[q], eb)

        dud = _tn(p_ref[...], jnp.concatenate(dui, axis=1).astype(MXU_DTYPE))
        duf_ref[0] = dud[:rc // 2].reshape(bl, ST, 256).astype(duf_ref.dtype)
        dur_ref[0] = dud[rc // 2:].reshape(bl, ST, 256).astype(dur_ref.dtype)

        @pl.when(k == nch - 1)
        def _():
            for d in range(2):
                dar_ref[d:d + 1, :] = jnp.sum(accr[4 * d:4 * d + 4, :], axis=0, keepdims=True)
                dai_ref[d:d + 1, :] = jnp.sum(acci[4 * d:4 * d + 4, :], axis=0, keepdims=True)

        _side_wait(side, srefs, jnp.logical_and(f == 1, k == nch - 1))

    sd = side if side is not None else _Side([])
    last = lambda k: nch - 1 - k
    blk = (1, bl, ST, 256)
    fspec = lambda c0: pl.BlockSpec(blk, lambda f, k: (fwd(last(k))[0], 0, fwd(last(k))[1], c0 + f))
    rspec = lambda c0: pl.BlockSpec(blk, lambda f, k: (rev(last(k))[0], 0, rev(last(k))[1], c0 + f))
    dush = jax.ShapeDtypeStruct((lay.nr, bl, TB, B_W), MXU_DTYPE)
    outs = pl.pallas_call(
        body, grid=(2, nch),
        in_specs=[fspec(2), rspec(2), fspec(0), rspec(0),
                  pl.BlockSpec((rc, rc), lambda f, k: (0, 0)),
                  pl.BlockSpec((1, 8, 2 * HS), lambda f, k: (last(k), 0, f)),
                  pl.BlockSpec((2, 2 * QC, 2 * QS), lambda f, k: (f, 0, 0)),
                  pl.BlockSpec((2, 2 * QS, 2 * QC), lambda f, k: (f, 0, 0)),
                  pl.BlockSpec((8, HS), lambda f, k: (0, f)), pl.BlockSpec((8, HS), lambda f, k: (0, f))] + sd.in_specs,
        out_specs=[fspec(0), rspec(0),
                   pl.BlockSpec((2, 2 * QC, 2 * QS), lambda f, k: (f, 0, 0)),
                   pl.BlockSpec((2, 2 * QS, 2 * QC), lambda f, k: (f, 0, 0)),
                   pl.BlockSpec((2, HS), lambda f, k: (0, f)), pl.BlockSpec((2, HS), lambda f, k: (0, f))] + sd.out_specs,
        out_shape=[dush, dush, jax.ShapeDtypeStruct((4, 2 * QC, 2 * QS), F32),
                   jax.ShapeDtypeStruct((4, 2 * QS, 2 * QC), F32), jax.ShapeDtypeStruct((2, 2 * HS), F32),
                   jax.ShapeDtypeStruct((2, 2 * HS), F32)] + sd.out_shape,
        scratch_shapes=[pltpu.VMEM((rc + 8, 2 * HS), F32), pltpu.VMEM((rc, 2 * HS), F32), pltpu.VMEM((8, 2 * HS), F32),
                        pltpu.VMEM((8, HS), F32), pltpu.VMEM((8, HS), F32)] + (sd.scratch if side is not None else []),
        compiler_params=_cp(("arbitrary", "arbitrary")), name=name)(z4, z4, dy4, dy4, perm, hst, bh, ch, ar8, ai8, *sd.arrays)
    duf, dur, dbh, dch, dar, dai = outs[:6]
    return duf.reshape(lay.nt, B_W), dur.reshape(lay.nt, B_W), dbh, dch, dar, dai, list(outs[6:])


def _glu_fwd(lay, z, yf, yr, dvec, wglu, bglu, name):
    def body(u_ref, yf_ref, yr_ref, d_ref, w_ref, b_ref, o_ref, y_ref):
        y = yf_ref[...] + yr_ref[...] + d_ref[...] * u_ref[...]
        y_ref[...] = y
        g = _gelu(y)
        pre = _nn(g.astype(MXU_DTYPE), w_ref[...]) + b_ref[...]
        o_ref[...] = (g * _sigmoid(pre)).astype(o_ref.dtype)

    tok = pl.BlockSpec((TB, B_W), lambda j: (j, 0))
    vec = pl.BlockSpec((1, B_W), lambda j: (0, 0))
    return pl.pallas_call(
        body, grid=(lay.nb,),
        in_specs=[pl.BlockSpec((TB, B_W), lambda j: (j, 1)), tok, tok, vec, pl.BlockSpec((B_W, B_W), lambda j: (0, 0)), vec],
        out_specs=[tok, tok],
        out_shape=[jax.ShapeDtypeStruct((lay.nt, B_W), MXU_DTYPE), jax.ShapeDtypeStruct((lay.nt, B_W), F32)],
        compiler_params=_cp(("parallel",)), name=name)(z, yf, yr, dvec, wglu, bglu)


def _glu_bwd(lay, z, y, ds, dvec, wglu, bglu, name):
    def body(u_ref, y_ref, ds_ref, d_ref, w_ref, b_ref, dy_ref, dud_ref, dw_ref, db_ref, dd_ref):
        j = pl.program_id(0)

        @pl.when(j == 0)
        def _():
            dw_ref[...] = jnp.zeros_like(dw_ref)
            db_ref[...] = jnp.zeros_like(db_ref)
            dd_ref[...] = jnp.zeros_like(dd_ref)

        yv = y_ref[...]
        g = _gelu(yv)
        gb = g.astype(MXU_DTYPE)
        sg = _sigmoid(_nn(gb, w_ref[...]) + b_ref[...])
        dsv = ds_ref[...].astype(F32)
        dpre = dsv * g * sg * (1.0 - sg)
        dpre_b = dpre.astype(MXU_DTYPE)
        dg = dsv * sg + _nt(dpre_b, w_ref[...])
        dw_ref[...] += _tn(gb, dpre_b)
        db_ref[...] += jnp.sum(dpre, axis=0, keepdims=True)
        dy = dg * _gelu_grad(yv)
        dy_ref[...] = dy.astype(dy_ref.dtype)
        dd_ref[...] += jnp.sum(dy * u_ref[...], axis=0, keepdims=True)
        dud_ref[...] = (dy * d_ref[...]).astype(dud_ref.dtype)

    tok = pl.BlockSpec((TB, B_W), lambda j: (j, 0))
    vec = pl.BlockSpec((1, B_W), lambda j: (0, 0))
    mat = pl.BlockSpec((B_W, B_W), lambda j: (0, 0))
    vsh = jax.ShapeDtypeStruct((1, B_W), F32)
    return pl.pallas_call(
        body, grid=(lay.nb,),
        in_specs=[pl.BlockSpec((TB, B_W), lambda j: (j, 1)), tok, tok, vec, mat, vec],
        out_specs=[tok, tok, mat, vec, vec],
        out_shape=[jax.ShapeDtypeStruct((lay.nt, B_W), MXU_DTYPE), jax.ShapeDtypeStruct((lay.nt, B_W), F32),
                   jax.ShapeDtypeStruct((B_W, B_W), F32), vsh, vsh],
        compiler_params=_cp(("arbitrary",)), name=name)(z, y, ds, dvec, wglu, bglu)


def _dz_assemble(lay, dz_a, duf, dur, dud, dz_p, name):
    def body(a_ref, f_ref, r_ref, d_ref, p_ref, o_ref):
        o_ref[:, :2 * A_W] = a_ref[...].astype(o_ref.dtype)
        o_ref[:, 2 * A_W:2 * A_W + B_W] = (f_ref[...].astype(F32) + r_ref[...].astype(F32) + d_ref[...]).astype(o_ref.dtype)
        o_ref[:, 2 * A_W + B_W:] = p_ref[...].astype(o_ref.dtype)

    spec = lambda w: pl.BlockSpec((TB, w), lambda j: (j, 0))
    return pl.pallas_call(
        body, grid=(lay.nb,), in_specs=[spec(2 * A_W), spec(B_W), spec(B_W), spec(B_W), spec(C_W)],
        out_specs=spec(D_IN), out_shape=jax.ShapeDtypeStruct((lay.nt, D_IN), MXU_DTYPE),
        compiler_params=_cp(("parallel",)), name=name)(dz_a, duf, dur, dud, dz_p)


def _expand_rows(a):
    return jnp.broadcast_to(a[:, :, None, :], (2, SSM_G, SSM_H, SSM_P)).reshape(-1, SSM_P)


def _ssm_params(lam_re, lam_im, log_dt, b_re, b_im, c_re, c_im, name):
    lrx, lix = _expand_rows(lam_re), _expand_rows(lam_im)
    ldtx = _expand_rows(jnp.broadcast_to(log_dt[:, :, None], (2, SSM_G, SSM_P)))
    brt = jnp.transpose(b_re, (0, 1, 3, 2)).reshape(-1, SSM_P)
    bit = jnp.transpose(b_im, (0, 1, 3, 2)).reshape(-1, SSM_P)
    arx, aix, bbr, bbi = _disc_fwd(lrx, lix, ldtx, brt, bit, name)
    ar = arx.reshape(2, SSM_G, SSM_H, SSM_P)[:, :, 0].reshape(2, SSM_G * SSM_P)
    ai = aix.reshape(2, SSM_G, SSM_H, SSM_P)[:, :, 0].reshape(2, SSM_G * SSM_P)
    eye = jnp.eye(GQ, dtype=F32)

    def bmat(bt):
        t = bt.reshape(2, 4, GQ, SSM_H, SSM_P)
        return jnp.einsum('dqghp,gk->qdghkp', t, eye).reshape(4, 2 * QC, QS)

    bh = jnp.concatenate([bmat(bbr), bmat(bbi)], axis=-1).astype(MXU_DTYPE)

    def cmat(c):
        t = c.reshape(2, 4, GQ, SSM_H, SSM_P)
        return jnp.einsum('dqghp,gk->qgpdkh', t, eye).reshape(4, QS, 2 * QC)

    ch = jnp.concatenate([cmat(c_re), -cmat(c_im)], axis=1).astype(MXU_DTYPE)

    def rows8(a):
        return jnp.repeat(a, 4, axis=0)

    return dict(lrx=lrx, lix=lix, ldtx=ldtx, brt=brt, bit=bit, bh=bh, ch=ch, ar8=rows8(ar), ai8=rows8(ai))


def _ssm_param_grads(sp, dbh, dch, dar, dai, name):
    def bdiag(m):
        t = m.reshape(4, 2, GQ, SSM_H, GQ, SSM_P)
        return jnp.einsum('qdghgp->dqghp', t).reshape(-1, SSM_P)

    dbr, dbi = bdiag(dbh[..., :QS]), bdiag(dbh[..., QS:])

    def cdiag(m):
        t = m.reshape(4, GQ, SSM_P, 2, GQ, SSM_H)
        return jnp.einsum('qgpdgh->dqghp', t).reshape(2, SSM_G, SSM_H, SSM_P)

    dc_re, dc_im = cdiag(dch[:, :QS]), -cdiag(dch[:, QS:])

    def hrow(a):
        t = a.reshape(2, SSM_G, 1, SSM_P)
        return jnp.concatenate([t, jnp.zeros((2, SSM_G, SSM_H - 1, SSM_P), F32)], axis=2).reshape(-1, SSM_P)

    glr, gli, gdt, gbr, gbi = _disc_bwd(sp["lrx"], sp["lix"], sp["ldtx"], sp["brt"], sp["bit"],
                                        hrow(dar), hrow(dai), dbr, dbi, name)
    to_b = lambda g: jnp.transpose(g.reshape(2, SSM_G, SSM_H, SSM_P), (0, 1, 3, 2))
    return dict(ssm_lam_re=glr.reshape(2, SSM_G, SSM_P), ssm_lam_im=gli.reshape(2, SSM_G, SSM_P),
                ssm_log_dt=gdt.reshape(2, SSM_G), ssm_b_re=to_b(gbr), ssm_b_im=to_b(gbi),
                ssm_c_re=dc_re, ssm_c_im=dc_im)


def _layer_consts(p):
    c = {}
    c["ws"] = p["sgu_w"].astype(MXU_DTYPE)
    c["wst"] = jnp.transpose(p["sgu_w"], (0, 2, 1)).astype(MXU_DTYPE)
    c["gbias"] = jnp.repeat(p["sgu_b"].T, 64, axis=1)
    pw = jnp.zeros((C_W, C_W), F32)
    for i in range(4):
        pw = pw.at[64 * i:64 * i + 64, 64 * i:64 * i + 64].set(p["pool_w"][i])
    c["pw"] = pw.astype(MXU_DTYPE)
    c["pscale"] = p["pool_scale"].reshape(1, C_W)
    c["dvec"] = p["ssm_d"].reshape(1, B_W)
    c["bglu"] = p["glu_b"].reshape(1, B_W)
    return c


def _layer_fwd(lay, i, x, modarr, p, w, cst, sp, bands, inv, perm, side=None, fill=None):
    n = f"l{i}_"
    res = {"x0": x}
    h = _normmod_fwd(lay, x, p["norm_mix_pre"].reshape(1, D), modarr, 0, 1, n + "nm1")
    z = _mm([(h, w["win_t"])], True, F32, n + "win")
    a = _gate_fwd(lay, z, cst["ws"], cst["gbias"], n + "gate")
    yf, yr, hst, extra = _ssm_fwd(lay, z, perm, sp["bh"], sp["ch"], sp["ar8"], sp["ai8"], n + "ssm", side)
    if fill is not None:
        fill(extra)
    s, y = _glu_fwd(lay, z, yf, yr, cst["dvec"], w["wglu"], cst["bglu"], n + "glu")
    c = _pool_fwd(lay, z, bands, inv, cst["pw"], cst["pscale"], n + "pool")
    mcat = jnp.concatenate([s, a, c], axis=1)
    m = _mm([(mcat, w["wout"])], False, F32, n + "wout")
    x1 = _resnorm_fwd(lay, x, m, p["norm_mix_post"].reshape(1, D), modarr, 2, n + "rn1")
    h2 = _normmod_fwd(lay, x1, p["norm_ffn_pre"].reshape(1, D), modarr, 3, 4, n + "nm2")
    g, u, act = _ffn_up(h2, w["wg_t"], w["wu_t"], n + "ffn_up")
    f = _mm([(act, w["wd"])], False, F32, n + "ffn_down")
    x2 = _resnorm_fwd(lay, x1, f, p["norm_ffn_post"].reshape(1, D), modarr, 5, n + "rn2")
    res.update(h=h, z=z, hst=hst, y=y, mcat=mcat, m=m, x1=x1, h2=h2, g=g, u=u, act=act, f=f)
    return x2, res


def _layer_bwd(lay, i, dx2, modarr, p, w, cst, sp, bands, inv, perm, res, side_fn=None):
    n = f"l{i}b_"
    big, small = {}, {}
    df, dg2, gpost2 = _resnorm_bwd(lay, dx2, res["f"], p["norm_ffn_post"].reshape(1, D), modarr, 5, n + "rn2")
    big["wd"] = _mm_tn(res["act"], df, MXU_DTYPE, n + "dwd")
    dg, du = _ffn_down_bwd(df, w["wd"], res["g"], res["u"], n + "ffn_down")
    dh2 = _mm([(dg, w["wg_t"]), (du, w["wu_t"])], False, F32, n + "dh2")
    big["wg_t"] = _mm_tn(dg, res["h2"], MXU_DTYPE, n + "dwg")
    big["wu_t"] = _mm_tn(du, res["h2"], MXU_DTYPE, n + "dwu")
    dx1, dsh2, dsc2, gpre2 = _normmod_bwd(lay, res["x1"], dh2, dx2, p["norm_ffn_pre"].reshape(1, D), modarr, 4, n + "nm2")
    dm, dg1, gpost1 = _resnorm_bwd(lay, dx1, res["m"], p["norm_mix_post"].reshape(1, D), modarr, 2, n + "rn1")
    big["wout"] = _unperm_wout(_mm_tn(res["mcat"], dm, MXU_DTYPE, n + "dwout"))
    dmcat = _mm([(dm, w["wout"])], True, F32, n + "dmcat")
    z = res["z"]
    dz_a, dws, dgb = _gate_bwd(lay, z, dmcat, cst["ws"], cst["wst"], cst["gbias"], n + "gate")
    dy, dud, dwglu, dbglu, ddvec = _glu_bwd(lay, z, res["y"], dmcat, cst["dvec"], w["wglu"], cst["bglu"], n + "glu")
    big["wglu"] = dwglu.astype(MXU_DTYPE)
    side = side_fn(big) if side_fn is not None else None
    duf, dur, dbh, dch, dar, dai, early = _ssm_bwd(lay, z, dy, perm, res["hst"], sp["bh"], sp["ch"], sp["ar8"],
                                                   sp["ai8"], n + "ssm", side)
    dz_p, dpw, dpsc = _pool_bwd(lay, z, dmcat, bands, inv, cst["pw"], cst["pscale"], n + "pool")
    dz = _dz_assemble(lay, dz_a, duf, dur, dud, dz_p, n + "dz")
    big["win_t"] = _mm_tn(dz, res["h"], MXU_DTYPE, n + "dwin")
    dh = _mm([(dz, w["win_t"])], False, F32, n + "dh")
    dx, dsh1, dsc1, gpre1 = _normmod_bwd(lay, res["x0"], dh, dx1, p["norm_mix_pre"].reshape(1, D), modarr, 1, n + "nm1",
                                         latent_only=(i == 0))

    small.update(norm_mix_pre=gpre1[0], norm_mix_post=gpost1[0], norm_ffn_pre=gpre2[0], norm_ffn_post=gpost2[0])
    small["sgu_w"] = dws
    small["sgu_b"] = jnp.sum(dgb.reshape(CHUNK, 4, 64), axis=-1).T
    small.update(_ssm_param_grads(sp, dbh, dch, dar, dai, n + "disc"))
    small["ssm_d"] = ddvec.reshape(SSM_G, SSM_H)
    small["glu_b"] = dbglu[0]
    small["pool_w"] = jnp.stack([dpw[64 * k:64 * k + 64, 64 * k:64 * k + 64] for k in range(4)])
    small["pool_scale"] = dpsc[0]
    dmod = jnp.concatenate([dsh1, dsc1, dg1, dsh2, dsc2, dg2], axis=1)[:lay.bl + 1]
    dmod = jnp.concatenate([dmod, jnp.zeros((8 - lay.bl - 1, 6, D), F32)], axis=0)
    return dx, big, small, dmod, early


def _perm_wout(w):
    return w.reshape(4, D // 4, D)[np.array(WOUT_PERM)].reshape(D, D)


def _unperm_wout(g):
    return g.reshape(4, D // 4, D)[np.array(WOUT_INV)].reshape(D, D)


SMALL_NAMES = ["norm_mix_pre", "norm_mix_post", "norm_ffn_pre", "norm_ffn_post", "sgu_w", "sgu_b", "ssm_lam_re",
               "ssm_lam_im", "ssm_log_dt", "ssm_b_re", "ssm_b_im", "ssm_c_re", "ssm_c_im", "ssm_d", "glu_b", "pool_w",
               "pool_scale"]
BIG_NAMES = ["win_t", "wout", "wglu", "wg_t", "wu_t", "wd"]


def _sincos_2d(rows, cols, dim):
    quarter = dim // 4
    omega = 1.0 / (10000.0 ** (jnp.arange(quarter, dtype=F32) / quarter))
    r = jnp.arange(rows, dtype=F32)[:, None] * omega
    cc = jnp.arange(cols, dtype=F32)[:, None] * omega
    er = jnp.concatenate([jnp.sin(r), jnp.cos(r)], axis=-1)
    ec = jnp.concatenate([jnp.sin(cc), jnp.cos(cc)], axis=-1)
    pe = jnp.concatenate([jnp.broadcast_to(er[:, None, :], (rows, cols, dim // 2)),
                          jnp.broadcast_to(ec[None, :, :], (rows, cols, dim // 2))], axis=-1)
    return pe.reshape(rows * cols, dim)


def _core(x, ctx, target, mods_local, params, weights, w_side=None, w_fill=None, g_side_fn=None):
    bl, lat, _ = x.shape
    assert bl == 4 and lat % TB == 0, "the scan fills 8 sublanes with 2 directions x 4 sequences"
    lay = _Layout(bl, lat)
    pe = _sincos_2d(lat // GRID_W, GRID_W, D)
    xt = _embed(lay, x.reshape(bl * lat, D), ctx.reshape(bl * CTX, D), pe)
    bands_np, inv_np = _band_constants()
    bands, inv = jnp.asarray(bands_np, MXU_DTYPE), jnp.asarray(inv_np, F32)
    perm = jnp.asarray(_scan_perm(bl), MXU_DTYPE)
    rows = lay.modrows_static()
    modarrs, csts, sps, ress, wls = [], [], [], [], []
    for i in range(2):
        modarrs.append(mods_local[i][rows].reshape(lay.nb * 6, 1, D))
        csts.append(_layer_consts(params[i]))
        p = params[i]
        sps.append(_ssm_params(p["ssm_lam_re"], p["ssm_lam_im"], p["ssm_log_dt"], p["ssm_b_re"], p["ssm_b_im"],
                               p["ssm_c_re"], p["ssm_c_im"], f"l{i}_disc"))
        wls.append(dict(weights[i]))

    def fill(extra):
        if w_fill is not None:
            w_fill(wls, extra)
        for w in wls:
            w["wout"] = _perm_wout(w["wout"])

    for i in range(2):
        first = i == 0
        xt, res = _layer_fwd(lay, i, xt, modarrs[i], params[i], wls[i], csts[i], sps[i], bands, inv, perm,
                             w_side if first else None, fill if first else None)
        ress.append(res)
    dx, lossv = _loss_bwd(lay, xt, target.reshape(bl * lat, D))
    bigs, smalls, dmods, early = [None, None], [None, None], [None, None], []
    for i in (1, 0):
        side_fn = (lambda big0: g_side_fn(bigs[1], big0)) if (i == 0 and g_side_fn is not None) else None
        dx, bigs[i], smalls[i], dmods[i], ex = _layer_bwd(lay, i, dx, modarrs[i], params[i], wls[i], csts[i], sps[i],
                                                           bands, inv, perm, ress[i], side_fn)
        early += ex
    return lossv[0, 0], dx.reshape(bl, lat, D), bigs, smalls, dmods, early


def _my_index():
    return 4 * lax.axis_index("x") + 2 * lax.axis_index("y") + lax.axis_index("c")


def _peer(k):
    x, y, c = lax.axis_index("x"), lax.axis_index("y"), lax.axis_index("c")
    kx, ky, kc = (k >> 2) & 1, (k >> 1) & 1, k & 1
    px = 1 - x if kx else x
    py = 1 - y if ky else y
    pc = 1 - c if kc else c
    return (px, py, pc), 4 * px + 2 * py + pc


class _Side:
    def __init__(self, items):
        self.items = items
        self.n = len(items)
        self.ncopies = sum(len(it[2]) for it in items)
        self.arrays = [it[0] for it in items]
        anyspec = pl.BlockSpec(memory_space=pl.ANY)
        self.in_specs = [anyspec] * self.n
        self.out_specs = [anyspec] * self.n
        self.out_shape = [jax.ShapeDtypeStruct((slots,) + tuple(a.shape) if mode == "gather" else tuple(a.shape), a.dtype)
                          for a, mode, ks, slots in items]
        self.scratch = [pltpu.SemaphoreType.DMA((self.ncopies,)), pltpu.SemaphoreType.DMA((self.ncopies,)),
                        pltpu.SemaphoreType.DMA((self.n,))]

    def _copies(self, ins, outs, sems):
        send_sems, recv_sems, local_sems = sems
        slot_of = lambda idx, slots: idx if slots == 8 else (idx // 2 if slots == 4 else idx % 2)
        me = _my_index()
        local, sends, recvs = [], [], []
        q = 0
        for t, (arr, mode, ks, slots) in enumerate(self.items):
            src_own = ins[t] if mode == "gather" else ins[t].at[me]
            local.append(pltpu.make_async_copy(src_own, outs[t].at[slot_of(me, slots)], local_sems.at[t]))
            for k in ks:
                peer, pidx = _peer(k)
                src = ins[t] if mode == "gather" else ins[t].at[pidx]
                sends.append(pltpu.make_async_remote_copy(
                    src_ref=src, dst_ref=outs[t].at[slot_of(me, slots)], send_sem=send_sems.at[q], recv_sem=recv_sems.at[q],
                    device_id=peer, device_id_type=pl.DeviceIdType.MESH))
                recvs.append(pltpu.make_async_remote_copy(
                    src_ref=src, dst_ref=outs[t].at[slot_of(pidx, slots)], send_sem=send_sems.at[q], recv_sem=recv_sems.at[q],
                    device_id=peer, device_id_type=pl.DeviceIdType.MESH))
                q += 1
        return local, sends, recvs

    def start(self, ins, outs, sems):
        local, sends, _ = self._copies(ins, outs, sems)
        for cp in sends + local:
            cp.start()

    def wait(self, ins, outs, sems):
        local, sends, recvs = self._copies(ins, outs, sems)
        for cp in recvs:
            cp.wait_recv()
        for cp in sends:
            cp.wait_send()
        for cp in local:
            cp.wait()


def _comm(items, name):
    side = _Side(items)
    n = side.n

    def body(*refs):
        ins, outs, sems = refs[:n], refs[n:2 * n], refs[2 * n:]
        side.start(ins, outs, sems)
        side.wait(ins, outs, sems)

    return pl.pallas_call(
        body, in_specs=side.in_specs, out_specs=side.out_specs, out_shape=side.out_shape, scratch_shapes=side.scratch,
        compiler_params=pltpu.CompilerParams(has_side_effects=True), name=name)(*side.arrays)


def _spread(items, name):
    n = len(items)
    ncopies = sum(len(it[1]) for it in items)

    def slot_of(idx, slots):
        return idx if slots == 8 else (idx // 2 if slots == 4 else idx % 2)

    def body(*refs):
        ins, outs, bufs = refs[:n], refs[n:2 * n], refs[2 * n:3 * n]
        load_sems, store_sems, send_sems, recv_sems = refs[3 * n:]
        me = _my_index()
        loads = [pltpu.make_async_copy(ins[t], bufs[t], load_sems.at[t]) for t in range(n)]
        for cp in loads:
            cp.start()
        stores, sends, recvs = [], [], []
        q = 0
        for t, (arr, ks, slots) in enumerate(items):
            loads[t].wait()
            own = outs[t].at[slot_of(me, slots)]
            stores.append(pltpu.make_async_copy(bufs[t], own, store_sems.at[t]))
            stores[-1].start()
            for k in ks:
                peer, pidx = _peer(k)
                sends.append(pltpu.make_async_remote_copy(
                    src_ref=bufs[t], dst_ref=own, send_sem=send_sems.at[q], recv_sem=recv_sems.at[q],
                    device_id=peer, device_id_type=pl.DeviceIdType.MESH))
                recvs.append(pltpu.make_async_remote_copy(
                    src_ref=bufs[t], dst_ref=outs[t].at[slot_of(pidx, slots)], send_sem=send_sems.at[q],
                    recv_sem=recv_sems.at[q], device_id=peer, device_id_type=pl.DeviceIdType.MESH))
                sends[-1].start()
                q += 1
        for cp in recvs:
            cp.wait_recv()
        for cp in sends:
            cp.wait_send()
        for cp in stores:
            cp.wait()

    anyspec = pl.BlockSpec(memory_space=pl.ANY)
    return pl.pallas_call(
        body, in_specs=[anyspec] * n, out_specs=[anyspec] * n,
        out_shape=[jax.ShapeDtypeStruct((slots,) + tuple(arr.shape), arr.dtype) for arr, ks, slots in items],
        scratch_shapes=[pltpu.VMEM(tuple(arr.shape), arr.dtype) for arr, ks, slots in items]
        + [pltpu.SemaphoreType.DMA((n,)), pltpu.SemaphoreType.DMA((n,)), pltpu.SemaphoreType.DMA((ncopies,)),
           pltpu.SemaphoreType.DMA((ncopies,))],
        compiler_params=pltpu.CompilerParams(has_side_effects=True, vmem_limit_bytes=VMEM_LIMIT),
        name=name)(*[it[0] for it in items])


ALL7 = (1, 2, 3, 4, 5, 6, 7)
CHIPS3 = (2, 4, 6)


def _sum8(parts, name):
    def one(a, nm):
        _, r, c = a.shape
        tr = r if r <= 512 else _pick_rows(r)

        def body(a_ref, o_ref):
            acc = a_ref[0].astype(F32)
            for q in range(1, a_ref.shape[0]):
                acc = acc + a_ref[q].astype(F32)
            o_ref[...] = acc

        return pl.pallas_call(
            body, grid=(r // tr,), in_specs=[pl.BlockSpec((a.shape[0], tr, c), lambda i: (0, i, 0))],
            out_specs=pl.BlockSpec((tr, c), lambda i: (i, 0)), out_shape=jax.ShapeDtypeStruct((r, c), F32),
            compiler_params=_cp(("parallel",)), name=nm)(a)

    return [one(a, f"{name}{i}") for i, a in enumerate(parts)]


def _pick_rows(r, cap=512):
    for t in (512, 352, 256, 176, 128, 64, 32, 16, 8):
        if r % t == 0 and t <= cap:
            return t
    return r


def _adam(w, g, m, v, name):
    shape = w.shape
    nel = int(np.prod(shape))
    if len(shape) >= 2 and shape[-1] >= 128:
        lanes = shape[-1]
    else:
        lanes = 512 if nel % 512 == 0 else 128
    r = nel // lanes
    tr = r if r * lanes <= 384 * 1024 else _pick_rows(r, 384 * 1024 // lanes)
    c1 = 1.0 / (1.0 - ADAM_B1 ** ADAM_STEP)
    c2 = 1.0 / (1.0 - ADAM_B2 ** ADAM_STEP)

    def body(w_ref, g_ref, m_ref, v_ref, d_ref, nm_ref, nv_ref):
        gv = g_ref[...]
        nm = ADAM_B1 * m_ref[...] + (1.0 - ADAM_B1) * gv
        nv = ADAM_B2 * v_ref[...] + (1.0 - ADAM_B2) * (gv * gv)
        d_ref[...] = -ADAM_LR * ((nm * c1) / (jnp.sqrt(nv * c2) + ADAM_EPS) + ADAM_WD * w_ref[...])
        nm_ref[...] = nm
        nv_ref[...] = nv

    spec = pl.BlockSpec((tr, lanes), lambda i: (i, 0))
    sh = jax.ShapeDtypeStruct((r, lanes), F32)
    outs = pl.pallas_call(
        body, grid=(r // tr,), in_specs=[spec] * 4, out_specs=[spec] * 3, out_shape=[sh] * 3,
        compiler_params=_cp(("parallel",)), name=name)(*[a.reshape(r, lanes) for a in (w, g, m, v)])
    return [o.reshape(shape) for o in outs]


def _silu(x):
    return x * _sigmoid(x)


def _mod_fwd(c_rows, w_mod, b_cols, name):
    def body(c_ref, w_ref, b_ref, o_ref):
        s = _silu(c_ref[...])
        for l in range(2):
            o_ref[l] = jnp.dot(s, w_ref[l], preferred_element_type=F32, precision=lax.Precision.HIGHEST) + b_ref[l]

    nc = w_mod.shape[2]
    return pl.pallas_call(body, out_shape=jax.ShapeDtypeStruct((2, c_rows.shape[0], nc), F32),
                          compiler_params=_cp(None), name=name)(c_rows, w_mod, b_cols)


def _mod_bwd(c_rows, w_mod, dlat, dctx8, name):
    nrow = c_rows.shape[0]
    nb = nrow - 8

    def body(c_ref, w_ref, dl_ref, dc_ref, gw_ref, gc_ref):
        s = _silu(c_ref[...])
        ctx_row = lax.broadcasted_iota(jnp.int32, (nrow, 1), 0) == nb
        gc = jnp.zeros((1, D), F32)
        for l in range(2):
            dctx = dc_ref[0, l]
            for q in range(1, 8):
                dctx = dctx + dc_ref[q, l]
            dm = dl_ref[l] + jnp.where(ctx_row, dctx, 0.0)
            gw_ref[l] = lax.dot_general(s, dm, (((0,), (0,)), ((), ())), preferred_element_type=F32,
                                        precision=lax.Precision.HIGHEST)
            gc = gc + lax.dot_general(dctx, w_ref[l], (((1,), (1,)), ((), ())), preferred_element_type=F32,
                                      precision=lax.Precision.HIGHEST)
        gc_ref[...] = gc

    nc = w_mod.shape[2]
    return pl.pallas_call(body, out_shape=[jax.ShapeDtypeStruct((2, D, nc), F32), jax.ShapeDtypeStruct((1, D), F32)],
                          compiler_params=_cp(None), name=name)(c_rows, w_mod, dlat, dctx8)


def _bmod_cctx(dmod_all, gc4, c_ctx, name):
    def body(dm_ref, gc_ref, cc_ref, gb_ref, gcc_ref):
        for l in range(2):
            acc = jnp.sum(dm_ref[0, l], axis=0, keepdims=True)
            for q in range(1, 8):
                acc = acc + jnp.sum(dm_ref[q, l], axis=0, keepdims=True)
            gb_ref[l:l + 1, :] = acc
        g = gc_ref[0] + gc_ref[1] + gc_ref[2] + gc_ref[3]
        cv = cc_ref[...]
        sg = _sigmoid(cv)
        gcc_ref[...] = g * (sg * (1.0 + cv * (1.0 - sg)))

    return pl.pallas_call(body, out_shape=[jax.ShapeDtypeStruct((2, 6 * D), F32), jax.ShapeDtypeStruct((1, D), F32)],
                          compiler_params=_cp(None), name=name)(dmod_all, gc4, c_ctx)


def kernel(x, c, ctx, c_ctx, w_mod, b_mod, norm_mix_pre, norm_mix_post, norm_ffn_pre, norm_ffn_post, w_in, w_out, sgu_w, sgu_b, ssm_lam_re, ssm_lam_im, ssm_log_dt, ssm_b_re, ssm_b_im, ssm_c_re, ssm_c_im, ssm_d, glu_w, glu_b, pool_w, pool_scale, ffn_w_gate, ffn_w_up, ffn_w_down, loss_target, m_c_ctx, m_w_mod, m_b_mod, m_norm_mix_pre, m_norm_mix_post, m_norm_ffn_pre, m_norm_ffn_post, m_w_in, m_w_out, m_sgu_w, m_sgu_b, m_ssm_lam_re, m_ssm_lam_im, m_ssm_log_dt, m_ssm_b_re, m_ssm_b_im, m_ssm_c_re, m_ssm_c_im, m_ssm_d, m_glu_w, m_glu_b, m_pool_w, m_pool_scale, m_ffn_w_gate, m_ffn_w_up, m_ffn_w_down, v_c_ctx, v_w_mod, v_b_mod, v_norm_mix_pre, v_norm_mix_post, v_norm_ffn_pre, v_norm_ffn_post, v_w_in, v_w_out, v_sgu_w, v_sgu_b, v_ssm_lam_re, v_ssm_lam_im, v_ssm_log_dt, v_ssm_b_re, v_ssm_b_im, v_ssm_c_re, v_ssm_c_im, v_ssm_d, v_glu_w, v_glu_b, v_pool_w, v_pool_scale, v_ffn_w_gate, v_ffn_w_up, v_ffn_w_down):
    wts = dict(c_ctx=c_ctx, w_mod=w_mod, b_mod=b_mod, norm_mix_pre=norm_mix_pre, norm_mix_post=norm_mix_post,
               norm_ffn_pre=norm_ffn_pre, norm_ffn_post=norm_ffn_post, w_in=w_in, w_out=w_out, sgu_w=sgu_w, sgu_b=sgu_b,
               ssm_lam_re=ssm_lam_re, ssm_lam_im=ssm_lam_im, ssm_log_dt=ssm_log_dt, ssm_b_re=ssm_b_re, ssm_b_im=ssm_b_im,
               ssm_c_re=ssm_c_re, ssm_c_im=ssm_c_im, ssm_d=ssm_d, glu_w=glu_w, glu_b=glu_b, pool_w=pool_w,
               pool_scale=pool_scale, ffn_w_gate=ffn_w_gate, ffn_w_up=ffn_w_up, ffn_w_down=ffn_w_down)
    ms = dict(c_ctx=m_c_ctx, w_mod=m_w_mod, b_mod=m_b_mod, norm_mix_pre=m_norm_mix_pre, norm_mix_post=m_norm_mix_post,
              norm_ffn_pre=m_norm_ffn_pre, norm_ffn_post=m_norm_ffn_post, w_in=m_w_in, w_out=m_w_out, sgu_w=m_sgu_w,
              sgu_b=m_sgu_b, ssm_lam_re=m_ssm_lam_re, ssm_lam_im=m_ssm_lam_im, ssm_log_dt=m_ssm_log_dt,
              ssm_b_re=m_ssm_b_re, ssm_b_im=m_ssm_b_im, ssm_c_re=m_ssm_c_re, ssm_c_im=m_ssm_c_im, ssm_d=m_ssm_d,
              glu_w=m_glu_w, glu_b=m_glu_b, pool_w=m_pool_w, pool_scale=m_pool_scale, ffn_w_gate=m_ffn_w_gate,
              ffn_w_up=m_ffn_w_up, ffn_w_down=m_ffn_w_down)
    vs = dict(c_ctx=v_c_ctx, w_mod=v_w_mod, b_mod=v_b_mod, norm_mix_pre=v_norm_mix_pre, norm_mix_post=v_norm_mix_post,
              norm_ffn_pre=v_norm_ffn_pre, norm_ffn_post=v_norm_ffn_post, w_in=v_w_in, w_out=v_w_out, sgu_w=v_sgu_w,
              sgu_b=v_sgu_b, ssm_lam_re=v_ssm_lam_re, ssm_lam_im=v_ssm_lam_im, ssm_log_dt=v_ssm_log_dt,
              ssm_b_re=v_ssm_b_re, ssm_b_im=v_ssm_b_im, ssm_c_re=v_ssm_c_re, ssm_c_im=v_ssm_c_im, ssm_d=v_ssm_d,
              glu_w=v_glu_w, glu_b=v_glu_b, pool_w=v_pool_w, pool_scale=v_pool_scale, ffn_w_gate=v_ffn_w_gate,
              ffn_w_up=v_ffn_w_up, ffn_w_down=v_ffn_w_down)
    order = list(wts.keys())
    bl = x.shape[0]
    nseq = bl * N_DEV
    me = _my_index()
    chip = me // 2
    ncol = w_mod.shape[2]

    (c_all,) = _spread([(c, ALL7, 8)], "ag_c")
    nrow = nseq + 8
    c_rows = jnp.concatenate([c_all.reshape(nseq, D), c_ctx[None], jnp.zeros((7, D), F32)], axis=0)
    b_cols = lax.dynamic_slice_in_dim(b_mod, chip * ncol, ncol, axis=1)[:, None, :]
    mod_cols = _mod_fwd(c_rows, w_mod, b_cols, "mod_fwd")
    (mod4,) = _spread([(mod_cols, CHIPS3, 4)], "ag_mod")
    mods = jnp.transpose(mod4, (1, 2, 0, 3)).reshape(2, nrow, 6 * D)
    mods_local = jnp.concatenate([lax.dynamic_slice_in_dim(mods, me * bl, bl, axis=1), mods[:, nseq:nseq + 1],
                                  jnp.zeros((2, 8 - bl - 1, 6 * D), F32)], axis=1)

    shards = {}
    for i in range(2):
        for nme, s in zip(BIG_NAMES, [w_in[i].T, w_out[i], glu_w[i], ffn_w_gate[i].T, ffn_w_up[i].T, ffn_w_down[i]]):
            shards[(i, nme)] = s.astype(MXU_DTYPE)
    (win0,) = _comm([(shards[(0, "win_t")], "gather", CHIPS3, 4)], "ag_win0")
    weights = [{"win_t": win0.reshape(-1, D)}, {}]
    late_w = [key for key in shards if key != (0, "win_t")]
    w_side = _Side([(shards[key], "gather", CHIPS3, 4) for key in late_w])

    def w_fill(wls, gathered):
        for (i, nme), g in zip(late_w, gathered):
            wls[i][nme] = g.reshape(-1, g.shape[-1])

    eighths = lambda g: g.reshape(8, g.shape[0] // 8, g.shape[1])
    early_g = [(1, k) for k in BIG_NAMES] + [(0, k) for k in BIG_NAMES if k != "win_t"]

    def g_side_fn(big1, big0):
        return _Side([(eighths((big1 if i == 1 else big0)[k]), "a2a", ALL7, 8) for i, k in early_g])

    params = [{k: wts[k][i] for k in SMALL_NAMES} for i in range(2)]
    loss_part, grad_x, bigs, smalls, dmods, early = _core(x, ctx, loss_target, mods_local, params, weights,
                                                           w_side, w_fill, g_side_fn)
    loss = lax.psum(loss_part, ("x", "y", "c"))

    dmod_local = jnp.stack([dmods[i].reshape(8, 6 * D) for i in range(2)])
    (dmod_all,) = _spread([(dmod_local, ALL7, 8)], "ag_dmod")
    dcols = lax.dynamic_slice_in_dim(dmod_all, chip * ncol, ncol, axis=3)
    dlat = jnp.transpose(dcols[:, :, :bl], (1, 0, 2, 3)).reshape(2, nseq, ncol)
    dlat = jnp.concatenate([dlat, jnp.zeros((2, 8, ncol), F32)], axis=1)
    dctx8 = dcols[:, :, bl:bl + 1]
    g_w_mod, gc_part = _mod_bwd(c_rows, w_mod, dlat, dctx8, "mod_bwd")
    (gc4,) = _spread([(gc_part, CHIPS3, 4)], "ag_cctx")
    g_b_mod, g_c_ctx = _bmod_cctx(dmod_all, gc4, c_ctx[None], "bmod_cctx")

    small_flat = jnp.concatenate([jnp.stack([smalls[i][k] for i in range(2)]).reshape(-1) for k in SMALL_NAMES])
    npad = (-small_flat.shape[0]) % (8 * 1024)
    small_flat = jnp.concatenate([small_flat, jnp.zeros((npad,), F32)])
    late = _comm([(eighths(bigs[0]["win_t"]), "a2a", ALL7, 8), (small_flat.reshape(8, -1, 1024), "a2a", ALL7, 8)],
                 "a2a_grads")
    sums = _sum8(list(early) + list(late), "gsum")
    fin = _spread([(s, (1,), 2) for s in sums[:-1]] + [(sums[-1], ALL7, 8)], "ag_grads")
    big_g = [{}, {}]
    for (i, k), g in zip(early_g + [(0, "win_t")], fin[:-1]):
        big_g[i][k] = g.reshape(-1, g.shape[-1])
    small_red = fin[-1].reshape(-1)

    grads = {}
    off = 0
    for k in SMALL_NAMES:
        shp = wts[k].shape
        nel = int(np.prod(shp))
        grads[k] = small_red[off:off + nel].reshape(shp)
        off += nel
    grads["c_ctx"] = g_c_ctx[0]
    grads["w_mod"] = g_w_mod
    grads["b_mod"] = g_b_mod
    grads["w_in"] = jnp.stack([big_g[i]["win_t"].T for i in range(2)])
    grads["w_out"] = jnp.stack([big_g[i]["wout"] for i in range(2)])
    grads["glu_w"] = jnp.stack([big_g[i]["wglu"] for i in range(2)])
    grads["ffn_w_gate"] = jnp.stack([big_g[i]["wg_t"].T for i in range(2)])
    grads["ffn_w_up"] = jnp.stack([big_g[i]["wu_t"].T for i in range(2)])
    grads["ffn_w_down"] = jnp.stack([big_g[i]["wd"] for i in range(2)])

    deltas, new_m, new_v = {}, {}, {}
    for k in order:
        deltas[k], new_m[k], new_v[k] = _adam(wts[k], grads[k], ms[k], vs[k], "adam_" + k)
    return (loss, grad_x, *[grads[k] for k in order], *[deltas[k] for k in order],
            *[new_m[k] for k in order], *[new_v[k] for k in order])
```

```python
import functools
import math

import numpy as np
import jax
import jax.numpy as jnp
from jax import lax
from jax.experimental import pallas as pl
from jax.experimental.pallas import tpu as pltpu

F32 = jnp.float32
BF16 = jnp.bfloat16
MXU_DTYPE = jnp.bfloat16
MCAT_A, MCAT_C = 2, 3
WOUT_PERM, WOUT_INV = (1, 2, 0, 3), (2, 0, 1, 3)

D = 1024
EPS = 1e-6
TB = 256
CTX = 256
CHUNK = 128
GRID_W = 64
A_W, B_W, C_W = 256, 512, 256
D_IN = 1280
D_FF = 2816
SSM_G, SSM_P, SSM_H = 32, 64, 16
ST = 64
POOL_WINDOWS = (2, 4, 8, 16)
N_DEV = 8
VMEM_LIMIT = 52 * 1024 * 1024
GELU_C = math.sqrt(2.0 / math.pi)

ADAM_LR, ADAM_B1, ADAM_B2, ADAM_EPS, ADAM_WD, ADAM_STEP = 0.001, 0.9, 0.999, 1e-08, 0.01, 10


def _cp(sem=None, vmem=VMEM_LIMIT, **kw):
    return pltpu.CompilerParams(dimension_semantics=sem, vmem_limit_bytes=vmem, **kw)


def _pick(n, cap):
    if n <= cap:
        return n
    best = None
    for t in range(128, cap + 1, 128):
        if n % t == 0:
            best = t
    assert best is not None, (n, cap)
    return best


def _gelu(x):
    return 0.5 * x * (1.0 + jnp.tanh(GELU_C * (x + 0.044715 * x * x * x)))


def _gelu_grad(x):
    t = jnp.tanh(GELU_C * (x + 0.044715 * x * x * x))
    return 0.5 * (1.0 + t) + 0.5 * x * (1.0 - t * t) * GELU_C * (1.0 + 3.0 * 0.044715 * x * x)


def _sigmoid(x):
    return 1.0 / (1.0 + jnp.exp(-x))


def _dot(a, b, dims):
    return lax.dot_general(a, b, (dims, ((), ())), preferred_element_type=F32)


def _nn(a, b):
    return _dot(a, b, ((1,), (0,)))


def _nt(a, b):
    return _dot(a, b, ((1,), (1,)))


def _tn(a, b):
    return _dot(a, b, ((0,), (0,)))


def _mm(pairs, nt, out_dtype, name, tm=512):
    m = pairs[0][0].shape[0]
    n = pairs[0][1].shape[0] if nt else pairs[0][1].shape[1]
    tn = _pick(n, 1408)
    tm = min(tm, m)
    npairs = len(pairs)

    def body(*refs):
        o_ref = refs[-1]
        acc = None
        for i in range(npairs):
            a = refs[2 * i][...].astype(MXU_DTYPE)
            b = refs[2 * i + 1][...].astype(MXU_DTYPE)
            r = _nt(a, b) if nt else _nn(a, b)
            acc = r if acc is None else acc + r
        o_ref[...] = acc.astype(o_ref.dtype)

    in_specs, flat = [], []
    for a, b in pairs:
        k = a.shape[1]
        in_specs.append(pl.BlockSpec((tm, k), lambda i, j: (i, 0)))
        in_specs.append(pl.BlockSpec((tn, k), lambda i, j: (j, 0)) if nt else pl.BlockSpec((k, tn), lambda i, j: (0, j)))
        flat += [a, b]
    return pl.pallas_call(
        body, grid=(m // tm, n // tn), in_specs=in_specs,
        out_specs=pl.BlockSpec((tm, tn), lambda i, j: (i, j)),
        out_shape=jax.ShapeDtypeStruct((m, n), out_dtype),
        compiler_params=_cp(("parallel", "parallel")), name=name)(*flat)


def _mm_tn(a, b, out_dtype, name, tm=512):
    m, k1 = a.shape
    n = b.shape[1]
    t1 = _pick(k1, 1408)
    tn = _pick(n, 1024)
    tm = min(tm, m)
    nsteps = m // tm

    def body(a_ref, b_ref, o_ref, acc_ref):
        t = pl.program_id(2)

        @pl.when(t == 0)
        def _():
            acc_ref[...] = jnp.zeros_like(acc_ref)

        acc_ref[...] += _tn(a_ref[...].astype(MXU_DTYPE), b_ref[...].astype(MXU_DTYPE))

        @pl.when(t == nsteps - 1)
        def _():
            o_ref[...] = acc_ref[...].astype(o_ref.dtype)

    return pl.pallas_call(
        body, grid=(k1 // t1, n // tn, nsteps),
        in_specs=[pl.BlockSpec((tm, t1), lambda i, j, t: (t, i)), pl.BlockSpec((tm, tn), lambda i, j, t: (t, j))],
        out_specs=pl.BlockSpec((t1, tn), lambda i, j, t: (i, j)),
        out_shape=jax.ShapeDtypeStruct((k1, n), out_dtype),
        scratch_shapes=[pltpu.VMEM((t1, tn), F32)],
        compiler_params=_cp(("parallel", "parallel", "arbitrary")), name=name)(a, b)


class _Layout:
    def __init__(self, bl, lat):
        self.bl, self.lat = bl, lat
        self.nlb = lat // TB
        self.nr = 1 + self.nlb
        self.nctx = bl
        self.nb = self.nr * bl
        self.nt = self.nb * TB
        self.ctx_row = bl

    def blk(self, g):
        gg = g - self.bl
        return jnp.where(g < self.bl, g, (gg % self.nlb + 1) * self.bl + gg // self.nlb)

    def modrow(self, g):
        return jnp.where(g < self.bl, self.ctx_row, (g - self.bl) // self.nlb)

    def first_of_row(self, g):
        return jnp.logical_or(g == 0, jnp.logical_and(g >= self.bl, (g - self.bl) % self.nlb == 0))

    def modrows_static(self):
        return np.array([self.ctx_row if j < self.bl else j % self.bl for j in range(self.nb)], np.int32)


def _tok_spec(lay):
    return pl.BlockSpec((TB, D), lambda g: (lay.blk(g), 0))


def _vec_spec():
    return pl.BlockSpec((1, D), lambda j: (0, 0))


def _mod_spec(lay, k):
    return pl.BlockSpec((1, 1, D), lambda g: (lay.blk(g) * 6 + k, 0, 0))


def _embed(lay, x2d, ctx2d, pe):
    bl, nlb = lay.bl, lay.nlb

    def body(x_ref, c_ref, pe_ref, o_ref):
        j = pl.program_id(0)

        @pl.when(j < bl)
        def _():
            o_ref[...] = c_ref[...]

        @pl.when(j >= bl)
        def _():
            o_ref[...] = x_ref[...] + pe_ref[...]

    pos = lambda j: jnp.maximum(j // bl - 1, 0)
    return pl.pallas_call(
        body, grid=(lay.nb,),
        in_specs=[pl.BlockSpec((TB, D), lambda j: ((j % bl) * nlb + pos(j), 0)),
                  pl.BlockSpec((TB, D), lambda j: (jnp.minimum(j, bl - 1), 0)),
                  pl.BlockSpec((TB, D), lambda j: (pos(j), 0))],
        out_specs=pl.BlockSpec((TB, D), lambda j: (j, 0)), out_shape=jax.ShapeDtypeStruct((lay.nt, D), F32),
        compiler_params=_cp(("parallel",)), name="embed")(x2d, ctx2d, pe)


def _normmod_fwd(lay, x, gain, modarr, ksh, ksc, name):
    def body(x_ref, g_ref, sh_ref, sc_ref, o_ref):
        xv = x_ref[...]
        r = lax.rsqrt(jnp.mean(xv * xv, axis=-1, keepdims=True) + EPS)
        o_ref[...] = ((xv * r * g_ref[...]) * (1.0 + sc_ref[0]) + sh_ref[0]).astype(o_ref.dtype)

    return pl.pallas_call(
        body, grid=(lay.nb,), in_specs=[_tok_spec(lay), _vec_spec(), _mod_spec(lay, ksh), _mod_spec(lay, ksc)],
        out_specs=_tok_spec(lay), out_shape=jax.ShapeDtypeStruct((lay.nt, D), MXU_DTYPE),
        compiler_params=_cp(("parallel",)), name=name)(x, gain, modarr, modarr)


def _acc_specs(lay):
    row = pl.BlockSpec((1, 1, D), lambda j: (lay.modrow(j), 0, 0))
    return row, jax.ShapeDtypeStruct((8, 1, D), F32)


def _normmod_bwd(lay, x, dh, dx_in, gain, modarr, ksc, name, latent_only=False):
    row_spec, row_shape = _acc_specs(lay)
    if latent_only:
        dx_spec = pl.BlockSpec((TB, D), lambda g: (jnp.maximum(g - lay.bl, 0), 0))
        dx_shape = jax.ShapeDtypeStruct((lay.bl * lay.lat, D), F32)
    else:
        dx_spec, dx_shape = _tok_spec(lay), jax.ShapeDtypeStruct((lay.nt, D), F32)

    def body(x_ref, dh_ref, dxi_ref, g_ref, sc_ref, dx_ref, dsh_ref, dsc_ref, dg_ref):
        j = pl.program_id(0)
        xv = x_ref[...]
        dhv = dh_ref[...].astype(F32)
        g = g_ref[...]
        sc1 = 1.0 + sc_ref[0]
        r = lax.rsqrt(jnp.mean(xv * xv, axis=-1, keepdims=True) + EPS)
        xh = xv * r
        dxh = dhv * (g * sc1)
        dx = r * (dxh - xh * jnp.mean(dxh * xh, axis=-1, keepdims=True))
        dx_ref[...] = dxi_ref[...] + dx

        @pl.when(lay.first_of_row(j))
        def _():
            dsh_ref[...] = jnp.zeros_like(dsh_ref)
            dsc_ref[...] = jnp.zeros_like(dsc_ref)

        @pl.when(j == 0)
        def _():
            dg_ref[...] = jnp.zeros_like(dg_ref)

        dsh_ref[0] += jnp.sum(dhv, axis=0, keepdims=True)
        dsc_ref[0] += jnp.sum(dhv * (xh * g), axis=0, keepdims=True)
        dg_ref[...] += jnp.sum(dhv * sc1 * xh, axis=0, keepdims=True)

    return pl.pallas_call(
        body, grid=(lay.nb,),
        in_specs=[_tok_spec(lay), _tok_spec(lay), _tok_spec(lay), _vec_spec(), _mod_spec(lay, ksc)],
        out_specs=[dx_spec, row_spec, row_spec, _vec_spec()],
        out_shape=[dx_shape, row_shape, row_shape, jax.ShapeDtypeStruct((1, D), F32)],
        compiler_params=_cp(("arbitrary",)), name=name)(x, dh, dx_in, gain, modarr)


def _resnorm_fwd(lay, x, m, gain, modarr, kgate, name):
    def body(x_ref, m_ref, g_ref, gate_ref, o_ref):
        mv = m_ref[...]
        r = lax.rsqrt(jnp.mean(mv * mv, axis=-1, keepdims=True) + EPS)
        o_ref[...] = x_ref[...] + gate_ref[0] * (mv * r * g_ref[...])

    return pl.pallas_call(
        body, grid=(lay.nb,), in_specs=[_tok_spec(lay), _tok_spec(lay), _vec_spec(), _mod_spec(lay, kgate)],
        out_specs=_tok_spec(lay), out_shape=jax.ShapeDtypeStruct((lay.nt, D), F32),
        compiler_params=_cp(("parallel",)), name=name)(x, m, gain, modarr)


def _resnorm_bwd(lay, dxn, m, gain, modarr, kgate, name):
    row_spec, row_shape = _acc_specs(lay)

    def body(d_ref, m_ref, g_ref, gate_ref, dm_ref, dgate_ref, dg_ref):
        j = pl.program_id(0)
        dv = d_ref[...]
        mv = m_ref[...]
        g = g_ref[...]
        r = lax.rsqrt(jnp.mean(mv * mv, axis=-1, keepdims=True) + EPS)
        xh = mv * r
        dy = dv * gate_ref[0]
        dxh = dy * g
        dm_ref[...] = (r * (dxh - xh * jnp.mean(dxh * xh, axis=-1, keepdims=True))).astype(dm_ref.dtype)

        @pl.when(lay.first_of_row(j))
        def _():
            dgate_ref[...] = jnp.zeros_like(dgate_ref)

        @pl.when(j == 0)
        def _():
            dg_ref[...] = jnp.zeros_like(dg_ref)

        dgate_ref[0] += jnp.sum(dv * (xh * g), axis=0, keepdims=True)
        dg_ref[...] += jnp.sum(dy * xh, axis=0, keepdims=True)

    return pl.pallas_call(
        body, grid=(lay.nb,), in_specs=[_tok_spec(lay), _tok_spec(lay), _vec_spec(), _mod_spec(lay, kgate)],
        out_specs=[_tok_spec(lay), row_spec, _vec_spec()],
        out_shape=[jax.ShapeDtypeStruct((lay.nt, D), MXU_DTYPE), row_shape, jax.ShapeDtypeStruct((1, D), F32)],
        compiler_params=_cp(("arbitrary",)), name=name)(dxn, m, gain, modarr)


def _loss_bwd(lay, xf, tgt2d):
    bl, nlb = lay.bl, lay.nlb

    def body(x_ref, t_ref, dx_ref, l_ref):
        j = pl.program_id(0)

        @pl.when(j == 0)
        def _():
            l_ref[...] = jnp.zeros_like(l_ref)

        @pl.when(j < bl)
        def _():
            dx_ref[...] = jnp.zeros_like(dx_ref)

        @pl.when(j >= bl)
        def _():
            e = x_ref[...] - t_ref[...]
            dx_ref[...] = e * (1.0 / D)
            l_ref[...] += jnp.sum(e * e) * (0.5 / D)

    tok = pl.BlockSpec((TB, D), lambda j: (j, 0))
    return pl.pallas_call(
        body, grid=(lay.nb,),
        in_specs=[tok, pl.BlockSpec((TB, D), lambda j: ((j % bl) * nlb + jnp.maximum(j // bl - 1, 0), 0))],
        out_specs=[tok, pl.BlockSpec((8, 128), lambda j: (0, 0))],
        out_shape=[jax.ShapeDtypeStruct((lay.nt, D), F32), jax.ShapeDtypeStruct((8, 128), F32)],
        compiler_params=_cp(("arbitrary",)), name="loss")(xf, tgt2d)


FF_TN = D_FF // 2
FF_CHUNKS = ((0, 512), (512, 512), (1024, 384))


def _ffn_up(h, wgt, wut, name, side=None):
    m = h.shape[0]
    tm, tn = min(512, m), FF_TN
    ni, nj = m // tm, D_FF // tn

    def body(*refs):
        (h_ref, wg_ref, wu_ref, g_ref, u_ref, a_ref), srefs = _side_split(refs, 3, 3, 0, side)
        i, j = pl.program_id(0), pl.program_id(1)
        _side_start(side, srefs, jnp.logical_and(i == 0, j == 0))
        hv = h_ref[...]
        for c0, cw in FF_CHUNKS:
            g = _nt(hv, wg_ref[c0:c0 + cw, :])
            u = _nt(hv, wu_ref[c0:c0 + cw, :])
            g_ref[:, c0:c0 + cw] = g.astype(g_ref.dtype)
            u_ref[:, c0:c0 + cw] = u.astype(u_ref.dtype)
            a_ref[:, c0:c0 + cw] = (g * _sigmoid(g) * u).astype(a_ref.dtype)
        _side_wait(side, srefs, jnp.logical_and(i == ni - 1, j == nj - 1))

    sd = side if side is not None else _Side([])
    osp = pl.BlockSpec((tm, tn), lambda i, j: (i, j))
    osh = jax.ShapeDtypeStruct((m, D_FF), MXU_DTYPE)
    outs = pl.pallas_call(
        body, grid=(ni, nj),
        in_specs=[pl.BlockSpec((tm, D), lambda i, j: (i, 0)), pl.BlockSpec((tn, D), lambda i, j: (j, 0)),
                  pl.BlockSpec((tn, D), lambda i, j: (j, 0))] + sd.in_specs,
        out_specs=[osp, osp, osp] + sd.out_specs, out_shape=[osh, osh, osh] + sd.out_shape,
        scratch_shapes=sd.scratch if side is not None else [],
        compiler_params=_cp(("arbitrary", "arbitrary") if side is not None else ("parallel", "parallel")),
        name=name)(h, wgt, wut, *sd.arrays)
    return outs[0], outs[1], outs[2], list(outs[3:])


def _ffn_down_bwd(df, wd, g, u, name):
    m = df.shape[0]
    tm, tn = min(512, m), FF_TN

    def body(df_ref, wd_ref, g_ref, u_ref, dg_ref, du_ref):
        dfv = df_ref[...]
        for c0, cw in FF_CHUNKS:
            da = _nt(dfv, wd_ref[c0:c0 + cw, :])
            gv = g_ref[:, c0:c0 + cw].astype(F32)
            uv = u_ref[:, c0:c0 + cw].astype(F32)
            s = _sigmoid(gv)
            dg_ref[:, c0:c0 + cw] = (da * uv * (s * (1.0 + gv * (1.0 - s)))).astype(dg_ref.dtype)
            du_ref[:, c0:c0 + cw] = (da * gv * s).astype(du_ref.dtype)

    osp = pl.BlockSpec((tm, tn), lambda i, j: (i, j))
    osh = jax.ShapeDtypeStruct((m, D_FF), MXU_DTYPE)
    return pl.pallas_call(
        body, grid=(m // tm, D_FF // tn),
        in_specs=[pl.BlockSpec((tm, D), lambda i, j: (i, 0)), pl.BlockSpec((tn, D), lambda i, j: (j, 0)), osp, osp],
        out_specs=[osp, osp], out_shape=[osh, osh],
        compiler_params=_cp(("parallel", "parallel")), name=name)(df, wd, g, u)


def _head_masks(shape):
    lane = lax.broadcasted_iota(jnp.int32, shape, 1)
    return [jnp.logical_and(lane >= 64 * h, lane < 64 * h + 64) for h in range(4)]


def _head_mean(x, masks):
    out = jnp.zeros_like(x)
    for mk in masks:
        s = jnp.sum(jnp.where(mk, x, 0.0), axis=-1, keepdims=True) * (1.0 / 64.0)
        out = jnp.where(mk, s, out)
    return out


def _gate_common(z, masks):
    zg = _gelu(z)
    u = zg[:, :A_W]
    v = zg[:, A_W:]
    mu = _head_mean(v, masks)
    vc = v - mu
    rstd = lax.rsqrt(_head_mean(vc * vc, masks) + EPS)
    return u, vc * rstd, rstd


def _gate_s(vn, ws_ref, bias, masks):
    parts = []
    for c in range(TB // CHUNK):
        vc = vn[c * CHUNK:(c + 1) * CHUNK]
        s = bias
        for h in range(4):
            s = s + _nn(ws_ref[h], jnp.where(masks[h][:CHUNK], vc, 0.0).astype(MXU_DTYPE))
        parts.append(s)
    return jnp.concatenate(parts, axis=0)


def _gate_fwd(lay, z, ws, bias, name):
    def body(z_ref, ws_ref, b_ref, o_ref):
        masks = _head_masks((TB, A_W))
        u, vn, _ = _gate_common(z_ref[...], masks)
        o_ref[...] = (u * _gate_s(vn, ws_ref, b_ref[...], masks)).astype(o_ref.dtype)

    return pl.pallas_call(
        body, grid=(lay.nb,),
        in_specs=[pl.BlockSpec((TB, 2 * A_W), lambda j: (j, 0)), pl.BlockSpec((4, CHUNK, CHUNK), lambda j: (0, 0, 0)),
                  pl.BlockSpec((CHUNK, A_W), lambda j: (0, 0))],
        out_specs=pl.BlockSpec((TB, A_W), lambda j: (j, 0)),
        out_shape=jax.ShapeDtypeStruct((lay.nt, A_W), MXU_DTYPE),
        compiler_params=_cp(("parallel",)), name=name)(z, ws, bias)


def _gate_bwd(lay, z, da, ws, wst, bias, name):
    def body(z_ref, da_ref, ws_ref, wst_ref, b_ref, dz_ref, dws_ref, db_ref):
        j = pl.program_id(0)

        @pl.when(j == 0)
        def _():
            dws_ref[...] = jnp.zeros_like(dws_ref)
            db_ref[...] = jnp.zeros_like(db_ref)

        masks = _head_masks((TB, A_W))
        zv = z_ref[...]
        u, vn, rstd = _gate_common(zv, masks)
        s = _gate_s(vn, ws_ref, b_ref[...], masks)
        dav = da_ref[...].astype(F32)
        du = dav * s
        ds = dav * u
        dvn_parts = []
        for c in range(TB // CHUNK):
            sl = slice(c * CHUNK, (c + 1) * CHUNK)
            ds_c = ds[sl]
            vn_c = vn[sl].astype(MXU_DTYPE)
            db_ref[...] += ds_c
            ds_b = ds_c.astype(MXU_DTYPE)
            dvn_c = jnp.zeros((CHUNK, A_W), F32)
            for h in range(4):
                mk = masks[h][:CHUNK]
                dws_ref[h] += _nt(jnp.where(mk, ds_c, 0.0).astype(MXU_DTYPE), vn_c)
                dvn_c = dvn_c + jnp.where(mk, _nn(wst_ref[h], ds_b), 0.0)
            dvn_parts.append(dvn_c)
        dvn = jnp.concatenate(dvn_parts, axis=0)
        dv = rstd * (dvn - _head_mean(dvn, masks) - vn * _head_mean(dvn * vn, masks))
        gg = _gelu_grad(zv)
        dz_ref[:, :A_W] = (du * gg[:, :A_W]).astype(dz_ref.dtype)
        dz_ref[:, A_W:] = (dv * gg[:, A_W:]).astype(dz_ref.dtype)

    return pl.pallas_call(
        body, grid=(lay.nb,),
        in_specs=[pl.BlockSpec((TB, 2 * A_W), lambda j: (j, 0)), pl.BlockSpec((TB, A_W), lambda j: (j, MCAT_A)),
                  pl.BlockSpec((4, CHUNK, CHUNK), lambda j: (0, 0, 0)), pl.BlockSpec((4, CHUNK, CHUNK), lambda j: (0, 0, 0)),
                  pl.BlockSpec((CHUNK, A_W), lambda j: (0, 0))],
        out_specs=[pl.BlockSpec((TB, 2 * A_W), lambda j: (j, 0)), pl.BlockSpec((4, CHUNK, CHUNK), lambda j: (0, 0, 0)),
                   pl.BlockSpec((CHUNK, A_W), lambda j: (0, 0))],
        out_shape=[jax.ShapeDtypeStruct((lay.nt, 2 * A_W), MXU_DTYPE), jax.ShapeDtypeStruct((4, CHUNK, CHUNK), F32),
                   jax.ShapeDtypeStruct((CHUNK, A_W), F32)],
        compiler_params=_cp(("arbitrary",)), name=name)(z, da, ws, wst, bias)


def _band_constants():
    bands = np.zeros((2, 4, TB, TB), np.float32)
    inv = np.zeros((2, 4, TB, 1), np.float32)
    for kind, n in ((0, GRID_W), (1, TB)):
        for i, w in enumerate(POOL_WINDOWS):
            for t in range(TB):
                base, tt = (t // n) * n, t % n
                lo = min(max(tt - w // 2, 0), n)
                hi = min(max(tt - w // 2 + w, 0), n)
                bands[kind, i, t, base + lo:base + hi] = 1.0
                inv[kind, i, t, 0] = 1.0 / (hi - lo)
    return bands, inv


def _split3(x):
    a = x.astype(MXU_DTYPE)
    r1 = x - a.astype(F32)
    b = r1.astype(MXU_DTYPE)
    c = (r1 - b.astype(F32)).astype(MXU_DTYPE)
    return a, b, c


def _window_apply(band_ref, inv_ref, x, masks, transpose):
    out = jnp.zeros_like(x)
    for i in range(4):
        xi = x * inv_ref[0, i] if transpose else x
        acc = None
        for part in _split3(xi):
            r = _tn(band_ref[0, i], part) if transpose else _nn(band_ref[0, i], part)
            acc = r if acc is None else acc + r
        if not transpose:
            acc = acc * inv_ref[0, i]
        out = jnp.where(masks[i], acc, out)
    return out


def _pool_specs(lay):
    kind = lambda j: jnp.where(j < lay.nctx, 1, 0)
    return [pl.BlockSpec((1, 4, TB, TB), lambda j: (kind(j), 0, 0, 0)), pl.BlockSpec((1, 4, TB, 1), lambda j: (kind(j), 0, 0, 0))]


def _pool_fwd(lay, z, bands, inv, pw, scale, name):
    def body(p_ref, band_ref, inv_ref, pw_ref, sc_ref, o_ref):
        masks = _head_masks((TB, C_W))
        p = p_ref[...]
        diff = _window_apply(band_ref, inv_ref, p, masks, False) - p
        o_ref[...] = (_nn(diff.astype(MXU_DTYPE), pw_ref[...]) * sc_ref[...]).astype(o_ref.dtype)

    return pl.pallas_call(
        body, grid=(lay.nb,),
        in_specs=[pl.BlockSpec((TB, C_W), lambda j: (j, 4))] + _pool_specs(lay)
        + [pl.BlockSpec((C_W, C_W), lambda j: (0, 0)), pl.BlockSpec((1, C_W), lambda j: (0, 0))],
        out_specs=pl.BlockSpec((TB, C_W), lambda j: (j, 0)),
        out_shape=jax.ShapeDtypeStruct((lay.nt, C_W), MXU_DTYPE),
        compiler_params=_cp(("parallel",)), name=name)(z, bands, inv, pw, scale)


def _pool_bwd(lay, z, dc, bands, inv, pw, scale, name):
    def body(p_ref, dc_ref, band_ref, inv_ref, pw_ref, sc_ref, dp_ref, dpw_ref, dsc_ref):
        j = pl.program_id(0)

        @pl.when(j == 0)
        def _():
            dpw_ref[...] = jnp.zeros_like(dpw_ref)
            dsc_ref[...] = jnp.zeros_like(dsc_ref)

        masks = _head_masks((TB, C_W))
        p = p_ref[...]
        dcv = dc_ref[...].astype(F32)
        diff = _window_apply(band_ref, inv_ref, p, masks, False) - p
        diff_b = diff.astype(MXU_DTYPE)
        pre = _nn(diff_b, pw_ref[...])
        dsc_ref[...] += jnp.sum(dcv * pre, axis=0, keepdims=True)
        dpre = dcv * sc_ref[...]
        dpre_b = dpre.astype(MXU_DTYPE)
        dpw_ref[...] += _tn(diff_b, dpre_b)
        ddiff = _nt(dpre_b, pw_ref[...])
        dp_ref[...] = (_window_apply(band_ref, inv_ref, ddiff, masks, True) - ddiff).astype(dp_ref.dtype)

    return pl.pallas_call(
        body, grid=(lay.nb,),
        in_specs=[pl.BlockSpec((TB, C_W), lambda j: (j, 4)), pl.BlockSpec((TB, C_W), lambda j: (j, MCAT_C))] + _pool_specs(lay)
        + [pl.BlockSpec((C_W, C_W), lambda j: (0, 0)), pl.BlockSpec((1, C_W), lambda j: (0, 0))],
        out_specs=[pl.BlockSpec((TB, C_W), lambda j: (j, 0)), pl.BlockSpec((C_W, C_W), lambda j: (0, 0)),
                   pl.BlockSpec((1, C_W), lambda j: (0, 0))],
        out_shape=[jax.ShapeDtypeStruct((lay.nt, C_W), MXU_DTYPE), jax.ShapeDtypeStruct((C_W, C_W), F32),
                   jax.ShapeDtypeStruct((1, C_W), F32)],
        compiler_params=_cp(("arbitrary",)), name=name)(z, dc, bands, inv, pw, scale)


def _disc_math(lr, li, ldt, br, bi):
    dt = jnp.exp(ldt)
    e = jnp.exp(lr * dt)
    ar = e * jnp.cos(li * dt)
    ai = e * jnp.sin(li * dt)
    nr, ni = ar - 1.0, ai
    den = lr * lr + li * li
    qr = (nr * lr + ni * li) / den
    qi = (ni * lr - nr * li) / den
    return ar, ai, qr * br - qi * bi, qr * bi + qi * br


def _disc_fwd(lrx, lix, ldtx, brt, bit, name):
    def body(lr_ref, li_ref, ldt_ref, br_ref, bi_ref, ar_ref, ai_ref, obr_ref, obi_ref):
        ar, ai, obr, obi = _disc_math(lr_ref[...], li_ref[...], ldt_ref[...], br_ref[...], bi_ref[...])
        ar_ref[...] = ar
        ai_ref[...] = ai
        obr_ref[...] = obr
        obi_ref[...] = obi

    sh = jax.ShapeDtypeStruct(lrx.shape, F32)
    return pl.pallas_call(body, out_shape=[sh, sh, sh, sh], name=name)(lrx, lix, ldtx, brt, bit)


def _disc_bwd(lrx, lix, ldtx, brt, bit, dar, dai, dbr, dbi, name):
    nrow = lrx.shape[0] // SSM_H

    def body(lr_ref, li_ref, ldt_ref, br_ref, bi_ref, dar_ref, dai_ref, dbr_ref, dbi_ref,
             glr_ref, gli_ref, gdt_ref, gbr_ref, gbi_ref):
        _, vjp = jax.vjp(_disc_math, lr_ref[...], li_ref[...], ldt_ref[...], br_ref[...], bi_ref[...])
        glr, gli, gdt, gbr, gbi = vjp((dar_ref[...], dai_ref[...], dbr_ref[...], dbi_ref[...]))
        glr_ref[...] = jnp.sum(glr.reshape(nrow, SSM_H, SSM_P), axis=1)
        gli_ref[...] = jnp.sum(gli.reshape(nrow, SSM_H, SSM_P), axis=1)
        gdt_ref[...] = jnp.sum(jnp.sum(gdt.reshape(nrow, SSM_H, SSM_P), axis=1), axis=-1, keepdims=True)
        gbr_ref[...] = gbr
        gbi_ref[...] = gbi

    small = jax.ShapeDtypeStruct((nrow, SSM_P), F32)
    big = jax.ShapeDtypeStruct(lrx.shape, F32)
    return pl.pallas_call(body, out_shape=[small, small, jax.ShapeDtypeStruct((nrow, 1), F32), big, big],
                          name=name)(lrx, lix, ldtx, brt, bit, dar, dai, dbr, dbi)


HS = 1024
GQ, QC, QS = 8, 128, 512
LC = QS
SCAN_UNROLL = 2


def _unrolled(step):
    def body(i, carry):
        for j in range(SCAN_UNROLL):
            carry = step(i * SCAN_UNROLL + j, carry)
        return carry
    return body


def _dir_cat(x, d0, qq):
    xq = x[:, QC * qq:QC * qq + QC]
    zero = jnp.zeros_like(xq)
    return jnp.concatenate([jnp.where(d0, xq, zero), jnp.where(d0, zero, xq)], axis=1)


def _dir_pick(x, d0):
    return jnp.where(d0, x[:, :QC], x[:, QC:])


def _d0_rows(n):
    row = lax.broadcasted_iota(jnp.int32, (n, 1), 0)
    return jnp.bitwise_and(row, 4) == 0


def _scan_perm(bl):
    n = 2 * bl * ST
    p = np.zeros((n, n), np.float32)
    for s in range(ST):
        for d in range(2):
            for b in range(bl):
                t = s if d == 0 else ST - 1 - s
                p[s * 2 * bl + d * bl + b, d * bl * ST + b * ST + t] = 1.0
    return p


def _scan_maps(lay):
    spc = TB // ST
    nlc = lay.nlb * spc

    def fwd(k):
        return k // spc, k % spc

    def rev(k):
        cpos = nlc - 1 - jnp.maximum(k - spc, 0)
        return jnp.where(k < spc, 0, 1 + cpos // spc), jnp.where(k < spc, spc - 1 - k, cpos % spc)

    return fwd, rev


def _pack_rows(f_ref, r_ref, p_ref, rc):
    st = jnp.concatenate([f_ref[0].reshape(rc // 2, 256), r_ref[0].reshape(rc // 2, 256)], axis=0).astype(MXU_DTYPE)
    return _nn(p_ref[...], st).astype(MXU_DTYPE)


def _side_split(refs, n_in, n_out, n_scr, side):
    ns = side.n if side is not None else 0
    ins, sin = refs[:n_in], refs[n_in:n_in + ns]
    o0 = n_in + ns
    outs, sout = refs[o0:o0 + n_out], refs[o0 + n_out:o0 + n_out + ns]
    s0 = o0 + n_out + ns
    return ins + outs + refs[s0:s0 + n_scr], (sin, sout, refs[s0 + n_scr:])


def _side_start(side, srefs, first):
    if side is not None:
        @pl.when(first)
        def _():
            side.start(*srefs)


def _side_wait(side, srefs, last):
    if side is not None:
        @pl.when(last)
        def _():
            side.wait(*srefs)


def _ssm_fwd(lay, z, perm, bh, ch, ar8, ai8, name, side=None):
    bl = lay.bl
    rc = ST * 2 * bl
    nch = lay.nr * (TB // ST)
    fwd, rev = _scan_maps(lay)
    z4 = z.reshape(lay.nr, bl, TB, z.shape[1])

    def body(*refs):
        own, srefs = _side_split(refs, 7, 3, 2, side)
        uf_ref, ur_ref, p_ref, bh_ref, ch_ref, ar_ref, ai_ref, yf_ref, yr_ref, hst_ref, hs, hc = own
        f, k = pl.program_id(0), pl.program_id(1)
        _side_start(side, srefs, jnp.logical_and(f == 0, k == 0))

        @pl.when(k == 0)
        def _():
            hc[...] = jnp.zeros_like(hc)

        hst_ref[0] = hc[...]
        d0 = _d0_rows(rc)
        uv = _pack_rows(uf_ref, ur_ref, p_ref, rc)
        for q in range(2):
            cr, ci = 2 * QS * q, 2 * QS * q + QS
            hs[:, cr:cr + 2 * QS] = _nn(_dir_cat(uv, d0, q), bh_ref[q])
            ar = ar_ref[:, QS * q:QS * q + QS]
            ai = ai_ref[:, QS * q:QS * q + QS]

            def step(s, carry, cr=cr, ci=ci, ar=ar, ai=ai):
                hr, hi = carry
                base = pl.multiple_of(s * 8, 8)
                nr = ar * hr - ai * hi + hs[pl.ds(base, 8), cr:cr + LC]
                ni = ar * hi + ai * hr + hs[pl.ds(base, 8), ci:ci + LC]
                hs[pl.ds(base, 8), cr:cr + LC] = nr
                hs[pl.ds(base, 8), ci:ci + LC] = ni
                return nr, ni

            hr, hi = lax.fori_loop(0, ST // SCAN_UNROLL, _unrolled(step), (hc[:, cr:cr + LC], hc[:, ci:ci + LC]))
            hc[:, cr:cr + LC] = hr
            hc[:, ci:ci + LC] = hi
        yi = jnp.concatenate(
            [_dir_pick(_nn(hs[:, 2 * QS * q:2 * QS * (q + 1)].astype(MXU_DTYPE), ch_ref[q]), d0) for q in range(2)], axis=1)
        yd = _tn(p_ref[...], yi.astype(MXU_DTYPE))
        yf_ref[0] = yd[:rc // 2].reshape(bl, ST, 256).astype(yf_ref.dtype)
        yr_ref[0] = yd[rc // 2:].reshape(bl, ST, 256).astype(yr_ref.dtype)
        _side_wait(side, srefs, jnp.logical_and(f == 1, k == nch - 1))

    sd = side if side is not None else _Side([])
    blk = (1, bl, ST, 256)
    ysh = jax.ShapeDtypeStruct((lay.nr, bl, TB, B_W), MXU_DTYPE)
    outs = pl.pallas_call(
        body, grid=(2, nch),
        in_specs=[pl.BlockSpec(blk, lambda f, k: (fwd(k)[0], 0, fwd(k)[1], 2 + f)),
                  pl.BlockSpec(blk, lambda f, k: (rev(k)[0], 0, rev(k)[1], 2 + f)),
                  pl.BlockSpec((rc, rc), lambda f, k: (0, 0)),
                  pl.BlockSpec((2, 2 * QC, 2 * QS), lambda f, k: (f, 0, 0)),
                  pl.BlockSpec((2, 2 * QS, 2 * QC), lambda f, k: (f, 0, 0)),
                  pl.BlockSpec((8, HS), lambda f, k: (0, f)), pl.BlockSpec((8, HS), lambda f, k: (0, f))] + sd.in_specs,
        out_specs=[pl.BlockSpec(blk, lambda f, k: (fwd(k)[0], 0, fwd(k)[1], f)),
                   pl.BlockSpec(blk, lambda f, k: (rev(k)[0], 0, rev(k)[1], f)),
                   pl.BlockSpec((1, 8, 2 * HS), lambda f, k: (k, 0, f))] + sd.out_specs,
        out_shape=[ysh, ysh, jax.ShapeDtypeStruct((nch, 8, 4 * HS), F32)] + sd.out_shape,
        scratch_shapes=[pltpu.VMEM((rc, 2 * HS), F32), pltpu.VMEM((8, 2 * HS), F32)] + (sd.scratch if side is not None else []),
        compiler_params=_cp(("arbitrary", "arbitrary")), name=name)(z4, z4, perm, bh, ch, ar8, ai8, *sd.arrays)
    yf, yr, hst = outs[:3]
    return yf.reshape(lay.nt, B_W), yr.reshape(lay.nt, B_W), hst, list(outs[3:])


def _ssm_bwd(lay, z, dy, perm, hst, bh, ch, ar8, ai8, name, side=None):
    bl = lay.bl
    rc = ST * 2 * bl
    nch = lay.nr * (TB // ST)
    fwd, rev = _scan_maps(lay)
    z4 = z.reshape(lay.nr, bl, TB, z.shape[1])
    dy4 = dy.reshape(lay.nr, bl, TB, B_W)

    def body(*refs):
        own, srefs = _side_split(refs, 10, 6, 5, side)
        (uf_ref, ur_ref, dyf_ref, dyr_ref, p_ref, hst_ref, bh_ref, ch_ref, ar_ref, ai_ref,
         duf_ref, dur_ref, dbh_ref, dch_ref, dar_ref, dai_ref, hs, es, ec, accr, acci) = own
        f, k = pl.program_id(0), pl.program_id(1)
        _side_start(side, srefs, jnp.logical_and(f == 0, k == 0))

        @pl.when(k == 0)
        def _():
            ec[...] = jnp.zeros_like(ec)
            accr[...] = jnp.zeros_like(accr)
            acci[...] = jnp.zeros_like(acci)
            dbh_ref[...] = jnp.zeros_like(dbh_ref)
            dch_ref[...] = jnp.zeros_like(dch_ref)

        d0 = _d0_rows(rc)
        uv = _pack_rows(uf_ref, ur_ref, p_ref, rc)
        dyv = _pack_rows(dyf_ref, dyr_ref, p_ref, rc)

        hs[0:8, :] = hst_ref[0]
        ucat, dycat = [], []
        for q in range(2):
            cr, ci = 2 * QS * q, 2 * QS * q + QS
            ucat.append(_dir_cat(uv, d0, q))
            dycat.append(_dir_cat(dyv, d0, q))
            hs[8:, cr:cr + 2 * QS] = _nn(ucat[q], bh_ref[q])
            ar = ar_ref[:, QS * q:QS * q + QS]
            ai = ai_ref[:, QS * q:QS * q + QS]

            def step(s, carry, cr=cr, ci=ci, ar=ar, ai=ai):
                hr, hi = carry
                base = pl.multiple_of(s * 8 + 8, 8)
                nr = ar * hr - ai * hi + hs[pl.ds(base, 8), cr:cr + LC]
                ni = ar * hi + ai * hr + hs[pl.ds(base, 8), ci:ci + LC]
                hs[pl.ds(base, 8), cr:cr + LC] = nr
                hs[pl.ds(base, 8), ci:ci + LC] = ni
                return nr, ni

            lax.fori_loop(0, ST // SCAN_UNROLL, _unrolled(step), (hs[0:8, cr:cr + LC], hs[0:8, ci:ci + LC]))
            dch_ref[q] += _tn(hs[8:, cr:cr + 2 * QS].astype(MXU_DTYPE), dycat[q])
            es[:, cr:cr + 2 * QS] = _nt(dycat[q], ch_ref[q])

        dui = []
        for q in range(2):
            cr, ci = 2 * QS * q, 2 * QS * q + QS
            ar = ar_ref[:, QS * q:QS * q + QS]
            ai = ai_ref[:, QS * q:QS * q + QS]

            def bstep(i, carry, cr=cr, ci=ci, ar=ar, ai=ai):
                er, ei, sr, si = carry
                base = pl.multiple_of((ST - 1 - i) * 8, 8)
                ner = es[pl.ds(base, 8), cr:cr + LC] + ar * er + ai * ei
                nei = es[pl.ds(base, 8), ci:ci + LC] - ai * er + ar * ei
                es[pl.ds(base, 8), cr:cr + LC] = ner
                es[pl.ds(base, 8), ci:ci + LC] = nei
                hpr = hs[pl.ds(base, 8), cr:cr + LC]
                hpi = hs[pl.ds(base, 8), ci:ci + LC]
                return ner, nei, sr + ner * hpr + nei * hpi, si - ner * hpi + nei * hpr

            lo = QS * q
            er, ei, sr, si = lax.fori_loop(
                0, ST // SCAN_UNROLL, _unrolled(bstep),
                (ec[:, cr:cr + LC], ec[:, ci:ci + LC], accr[:, lo:lo + LC], acci[:, lo:lo + LC]))
            ec[:, cr:cr + LC] = er
            ec[:, ci:ci + LC] = ei
            accr[:, lo:lo + LC] = sr
            acci[:, lo:lo + LC] = si
            eb = es[:, cr:cr + 2 * QS].astype(MXU_DTYPE)
            dui.append(_dir_pick(_nt(eb, bh_ref[q]), d0))
            dbh_ref[q] += _tn(ucat[q], eb)

        dud = _tn(p_ref[...], jnp.concatenate(dui, axis=1).astype(MXU_DTYPE))
        duf_ref[0] = dud[:rc // 2].reshape(bl, ST, 256).astype(duf_ref.dtype)
        dur_ref[0] = dud[rc // 2:].reshape(bl, ST, 256).astype(dur_ref.dtype)

        @pl.when(k == nch - 1)
        def _():
            for d in range(2):
                dar_ref[d:d + 1, :] = jnp.sum(accr[4 * d:4 * d + 4, :], axis=0, keepdims=True)
                dai_ref[d:d + 1, :] = jnp.sum(acci[4 * d:4 * d + 4, :], axis=0, keepdims=True)

        _side_wait(side, srefs, jnp.logical_and(f == 1, k == nch - 1))

    sd = side if side is not None else _Side([])
    last = lambda k: nch - 1 - k
    blk = (1, bl, ST, 256)
    fspec = lambda c0: pl.BlockSpec(blk, lambda f, k: (fwd(last(k))[0], 0, fwd(last(k))[1], c0 + f))
    rspec = lambda c0: pl.BlockSpec(blk, lambda f, k: (rev(last(k))[0], 0, rev(last(k))[1], c0 + f))
    dush = jax.ShapeDtypeStruct((lay.nr, bl, TB, B_W), MXU_DTYPE)
    outs = pl.pallas_call(
        body, grid=(2, nch),
        in_specs=[fspec(2), rspec(2), fspec(0), rspec(0),
                  pl.BlockSpec((rc, rc), lambda f, k: (0, 0)),
                  pl.BlockSpec((1, 8, 2 * HS), lambda f, k: (last(k), 0, f)),
                  pl.BlockSpec((2, 2 * QC, 2 * QS), lambda f, k: (f, 0, 0)),
                  pl.BlockSpec((2, 2 * QS, 2 * QC), lambda f, k: (f, 0, 0)),
                  pl.BlockSpec((8, HS), lambda f, k: (0, f)), pl.BlockSpec((8, HS), lambda f, k: (0, f))] + sd.in_specs,
        out_specs=[fspec(0), rspec(0),
                   pl.BlockSpec((2, 2 * QC, 2 * QS), lambda f, k: (f, 0, 0)),
                   pl.BlockSpec((2, 2 * QS, 2 * QC), lambda f, k: (f, 0, 0)),
                   pl.BlockSpec((2, HS), lambda f, k: (0, f)), pl.BlockSpec((2, HS), lambda f, k: (0, f))] + sd.out_specs,
        out_shape=[dush, dush, jax.ShapeDtypeStruct((4, 2 * QC, 2 * QS), F32),
                   jax.ShapeDtypeStruct((4, 2 * QS, 2 * QC), F32), jax.ShapeDtypeStruct((2, 2 * HS), F32),
                   jax.ShapeDtypeStruct((2, 2 * HS), F32)] + sd.out_shape,
        scratch_shapes=[pltpu.VMEM((rc + 8, 2 * HS), F32), pltpu.VMEM((rc, 2 * HS), F32), pltpu.VMEM((8, 2 * HS), F32),
                        pltpu.VMEM((8, HS), F32), pltpu.VMEM((8, HS), F32)] + (sd.scratch if side is not None else []),
        compiler_params=_cp(("arbitrary", "arbitrary")), name=name)(z4, z4, dy4, dy4, perm, hst, bh, ch, ar8, ai8, *sd.arrays)
    duf, dur, dbh, dch, dar, dai = outs[:6]
    return duf.reshape(lay.nt, B_W), dur.reshape(lay.nt, B_W), dbh, dch, dar, dai, list(outs[6:])


def _glu_fwd(lay, z, yf, yr, dvec, wglu, bglu, name):
    def body(u_ref, yf_ref, yr_ref, d_ref, w_ref, b_ref, o_ref, y_ref):
        y = yf_ref[...].astype(F32) + yr_ref[...].astype(F32) + d_ref[...] * u_ref[...]
        y_ref[...] = y
        g = _gelu(y)
        pre = _nn(g.astype(MXU_DTYPE), w_ref[...]) + b_ref[...]
        o_ref[...] = (g * _sigmoid(pre)).astype(o_ref.dtype)

    tok = pl.BlockSpec((TB, B_W), lambda j: (j, 0))
    vec = pl.BlockSpec((1, B_W), lambda j: (0, 0))
    return pl.pallas_call(
        body, grid=(lay.nb,),
        in_specs=[pl.BlockSpec((TB, B_W), lambda j: (j, 1)), tok, tok, vec, pl.BlockSpec((B_W, B_W), lambda j: (0, 0)), vec],
        out_specs=[tok, tok],
        out_shape=[jax.ShapeDtypeStruct((lay.nt, B_W), MXU_DTYPE), jax.ShapeDtypeStruct((lay.nt, B_W), F32)],
        compiler_params=_cp(("parallel",)), name=name)(z, yf, yr, dvec, wglu, bglu)


def _glu_bwd(lay, z, y, ds, dvec, wglu, bglu, name):
    def body(u_ref, y_ref, ds_ref, d_ref, w_ref, b_ref, dy_ref, dud_ref, dw_ref, db_ref, dd_ref):
        j = pl.program_id(0)

        @pl.when(j == 0)
        def _():
            dw_ref[...] = jnp.zeros_like(dw_ref)
            db_ref[...] = jnp.zeros_like(db_ref)
            dd_ref[...] = jnp.zeros_like(dd_ref)

        yv = y_ref[...]
        g = _gelu(yv)
        gb = g.astype(MXU_DTYPE)
        sg = _sigmoid(_nn(gb, w_ref[...]) + b_ref[...])
        dsv = ds_ref[...].astype(F32)
        dpre = dsv * g * sg * (1.0 - sg)
        dpre_b = dpre.astype(MXU_DTYPE)
        dg = dsv * sg + _nt(dpre_b, w_ref[...])
        dw_ref[...] += _tn(gb, dpre_b)
        db_ref[...] += jnp.sum(dpre, axis=0, keepdims=True)
        dy = dg * _gelu_grad(yv)
        dy_ref[...] = dy.astype(dy_ref.dtype)
        dd_ref[...] += jnp.sum(dy * u_ref[...], axis=0, keepdims=True)
        dud_ref[...] = (dy * d_ref[...]).astype(dud_ref.dtype)

    tok = pl.BlockSpec((TB, B_W), lambda j: (j, 0))
    vec = pl.BlockSpec((1, B_W), lambda j: (0, 0))
    mat = pl.BlockSpec((B_W, B_W), lambda j: (0, 0))
    vsh = jax.ShapeDtypeStruct((1, B_W), F32)
    return pl.pallas_call(
        body, grid=(lay.nb,),
        in_specs=[pl.BlockSpec((TB, B_W), lambda j: (j, 1)), tok, tok, vec, mat, vec],
        out_specs=[tok, tok, mat, vec, vec],
        out_shape=[jax.ShapeDtypeStruct((lay.nt, B_W), MXU_DTYPE), jax.ShapeDtypeStruct((lay.nt, B_W), F32),
                   jax.ShapeDtypeStruct((B_W, B_W), F32), vsh, vsh],
        compiler_params=_cp(("arbitrary",)), name=name)(z, y, ds, dvec, wglu, bglu)


def _dz_assemble(lay, dz_a, duf, dur, dud, dz_p, name):
    def body(a_ref, f_ref, r_ref, d_ref, p_ref, o_ref):
        o_ref[:, :2 * A_W] = a_ref[...].astype(o_ref.dtype)
        o_ref[:, 2 * A_W:2 * A_W + B_W] = (f_ref[...].astype(F32) + r_ref[...].astype(F32) + d_ref[...]).astype(o_ref.dtype)
        o_ref[:, 2 * A_W + B_W:] = p_ref[...].astype(o_ref.dtype)

    spec = lambda w: pl.BlockSpec((TB, w), lambda j: (j, 0))
    return pl.pallas_call(
        body, grid=(lay.nb,), in_specs=[spec(2 * A_W), spec(B_W), spec(B_W), spec(B_W), spec(C_W)],
        out_specs=spec(D_IN), out_shape=jax.ShapeDtypeStruct((lay.nt, D_IN), MXU_DTYPE),
        compiler_params=_cp(("parallel",)), name=name)(dz_a, duf, dur, dud, dz_p)


def _expand_rows(a):
    return jnp.broadcast_to(a[:, :, None, :], (2, SSM_G, SSM_H, SSM_P)).reshape(-1, SSM_P)


def _ssm_params(lam_re, lam_im, log_dt, b_re, b_im, c_re, c_im, name):
    lrx, lix = _expand_rows(lam_re), _expand_rows(lam_im)
    ldtx = _expand_rows(jnp.broadcast_to(log_dt[:, :, None], (2, SSM_G, SSM_P)))
    brt = jnp.transpose(b_re, (0, 1, 3, 2)).reshape(-1, SSM_P)
    bit = jnp.transpose(b_im, (0, 1, 3, 2)).reshape(-1, SSM_P)
    arx, aix, bbr, bbi = _disc_fwd(lrx, lix, ldtx, brt, bit, name)
    ar = arx.reshape(2, SSM_G, SSM_H, SSM_P)[:, :, 0].reshape(2, SSM_G * SSM_P)
    ai = aix.reshape(2, SSM_G, SSM_H, SSM_P)[:, :, 0].reshape(2, SSM_G * SSM_P)
    eye = jnp.eye(GQ, dtype=F32)

    def bmat(bt):
        t = bt.reshape(2, 4, GQ, SSM_H, SSM_P)
        return jnp.einsum('dqghp,gk->qdghkp', t, eye).reshape(4, 2 * QC, QS)

    bh = jnp.concatenate([bmat(bbr), bmat(bbi)], axis=-1).astype(MXU_DTYPE)

    def cmat(c):
        t = c.reshape(2, 4, GQ, SSM_H, SSM_P)
        return jnp.einsum('dqghp,gk->qgpdkh', t, eye).reshape(4, QS, 2 * QC)

    ch = jnp.concatenate([cmat(c_re), -cmat(c_im)], axis=1).astype(MXU_DTYPE)

    def rows8(a):
        return jnp.repeat(a, 4, axis=0)

    return dict(lrx=lrx, lix=lix, ldtx=ldtx, brt=brt, bit=bit, bh=bh, ch=ch, ar8=rows8(ar), ai8=rows8(ai))


def _ssm_param_grads(sp, dbh, dch, dar, dai, name):
    def bdiag(m):
        t = m.reshape(4, 2, GQ, SSM_H, GQ, SSM_P)
        return jnp.einsum('qdghgp->dqghp', t).reshape(-1, SSM_P)

    dbr, dbi = bdiag(dbh[..., :QS]), bdiag(dbh[..., QS:])

    def cdiag(m):
        t = m.reshape(4, GQ, SSM_P, 2, GQ, SSM_H)
        return jnp.einsum('qgpdgh->dqghp', t).reshape(2, SSM_G, SSM_H, SSM_P)

    dc_re, dc_im = cdiag(dch[:, :QS]), -cdiag(dch[:, QS:])

    def hrow(a):
        t = a.reshape(2, SSM_G, 1, SSM_P)
        return jnp.concatenate([t, jnp.zeros((2, SSM_G, SSM_H - 1, SSM_P), F32)], axis=2).reshape(-1, SSM_P)

    glr, gli, gdt, gbr, gbi = _disc_bwd(sp["lrx"], sp["lix"], sp["ldtx"], sp["brt"], sp["bit"],
                                        hrow(dar), hrow(dai), dbr, dbi, name)
    to_b = lambda g: jnp.transpose(g.reshape(2, SSM_G, SSM_H, SSM_P), (0, 1, 3, 2))
    return dict(ssm_lam_re=glr.reshape(2, SSM_G, SSM_P), ssm_lam_im=gli.reshape(2, SSM_G, SSM_P),
                ssm_log_dt=gdt.reshape(2, SSM_G), ssm_b_re=to_b(gbr), ssm_b_im=to_b(gbi),
                ssm_c_re=dc_re, ssm_c_im=dc_im)


def _layer_consts(p):
    c = {}
    c["ws"] = p["sgu_w"].astype(MXU_DTYPE)
    c["wst"] = jnp.transpose(p["sgu_w"], (0, 2, 1)).astype(MXU_DTYPE)
    c["gbias"] = jnp.repeat(p["sgu_b"].T, 64, axis=1)
    pw = jnp.zeros((C_W, C_W), F32)
    for i in range(4):
        pw = pw.at[64 * i:64 * i + 64, 64 * i:64 * i + 64].set(p["pool_w"][i])
    c["pw"] = pw.astype(MXU_DTYPE)
    c["pscale"] = p["pool_scale"].reshape(1, C_W)
    c["dvec"] = p["ssm_d"].reshape(1, B_W)
    c["bglu"] = p["glu_b"].reshape(1, B_W)
    return c


def _layer_fwd(lay, i, x, modarr, p, w, cst, sp, bands, inv, perm, sides=None):
    n = f"l{i}_"
    sides = sides or {}
    ssm_side, ssm_fill = sides.get("ssm", (None, None))
    ffn_side, ffn_fill = sides.get("ffn", (None, None))
    res = {"x0": x}
    h = _normmod_fwd(lay, x, p["norm_mix_pre"].reshape(1, D), modarr, 0, 1, n + "nm1")
    z = _mm([(h, w["win_t"])], True, F32, n + "win")
    a = _gate_fwd(lay, z, cst["ws"], cst["gbias"], n + "gate")
    yf, yr, hst, extra = _ssm_fwd(lay, z, perm, sp["bh"], sp["ch"], sp["ar8"], sp["ai8"], n + "ssm", ssm_side)
    if ssm_fill is not None:
        ssm_fill(extra)
    s, y = _glu_fwd(lay, z, yf, yr, cst["dvec"], w["wglu"], cst["bglu"], n + "glu")
    c = _pool_fwd(lay, z, bands, inv, cst["pw"], cst["pscale"], n + "pool")
    mcat = jnp.concatenate([s, a, c], axis=1)
    m = _mm([(mcat, w["wout"])], False, F32, n + "wout")
    x1 = _resnorm_fwd(lay, x, m, p["norm_mix_post"].reshape(1, D), modarr, 2, n + "rn1")
    h2 = _normmod_fwd(lay, x1, p["norm_ffn_pre"].reshape(1, D), modarr, 3, 4, n + "nm2")
    g, u, act, extra = _ffn_up(h2, w["wg_t"], w["wu_t"], n + "ffn_up", ffn_side)
    if ffn_fill is not None:
        ffn_fill(extra)
    f = _mm([(act, w["wd"])], False, F32, n + "ffn_down")
    x2 = _resnorm_fwd(lay, x1, f, p["norm_ffn_post"].reshape(1, D), modarr, 5, n + "rn2")
    res.update(h=h, z=z, hst=hst, y=y, mcat=mcat, m=m, x1=x1, h2=h2, g=g, u=u, act=act, f=f)
    return x2, res


def _layer_bwd(lay, i, dx2, modarr, p, w, cst, sp, bands, inv, perm, res, side_fn=None):
    n = f"l{i}b_"
    big, small = {}, {}
    df, dg2, gpost2 = _resnorm_bwd(lay, dx2, res["f"], p["norm_ffn_post"].reshape(1, D), modarr, 5, n + "rn2")
    big["wd"] = _mm_tn(res["act"], df, MXU_DTYPE, n + "dwd")
    dg, du = _ffn_down_bwd(df, w["wd"], res["g"], res["u"], n + "ffn_down")
    dh2 = _mm([(dg, w["wg_t"]), (du, w["wu_t"])], False, MXU_DTYPE, n + "dh2")
    big["wg_t"] = _mm_tn(dg, res["h2"], MXU_DTYPE, n + "dwg")
    big["wu_t"] = _mm_tn(du, res["h2"], MXU_DTYPE, n + "dwu")
    dx1, dsh2, dsc2, gpre2 = _normmod_bwd(lay, res["x1"], dh2, dx2, p["norm_ffn_pre"].reshape(1, D), modarr, 4, n + "nm2")
    dm, dg1, gpost1 = _resnorm_bwd(lay, dx1, res["m"], p["norm_mix_post"].reshape(1, D), modarr, 2, n + "rn1")
    big["wout"] = _unperm_wout(_mm_tn(res["mcat"], dm, MXU_DTYPE, n + "dwout"))
    dmcat = _mm([(dm, w["wout"])], True, MXU_DTYPE, n + "dmcat")
    z = res["z"]
    dz_a, dws, dgb = _gate_bwd(lay, z, dmcat, cst["ws"], cst["wst"], cst["gbias"], n + "gate")
    dy, dud, dwglu, dbglu, ddvec = _glu_bwd(lay, z, res["y"], dmcat, cst["dvec"], w["wglu"], cst["bglu"], n + "glu")
    big["wglu"] = dwglu.astype(MXU_DTYPE)
    side = side_fn(big) if side_fn is not None else None
    duf, dur, dbh, dch, dar, dai, early = _ssm_bwd(lay, z, dy, perm, res["hst"], sp["bh"], sp["ch"], sp["ar8"],
                                                   sp["ai8"], n + "ssm", side)
    dz_p, dpw, dpsc = _pool_bwd(lay, z, dmcat, bands, inv, cst["pw"], cst["pscale"], n + "pool")
    dz = _dz_assemble(lay, dz_a, duf, dur, dud, dz_p, n + "dz")
    big["win_t"] = _mm_tn(dz, res["h"], MXU_DTYPE, n + "dwin")
    dh = _mm([(dz, w["win_t"])], False, MXU_DTYPE, n + "dh")
    dx, dsh1, dsc1, gpre1 = _normmod_bwd(lay, res["x0"], dh, dx1, p["norm_mix_pre"].reshape(1, D), modarr, 1, n + "nm1",
                                         latent_only=(i == 0))

    small.update(norm_mix_pre=gpre1[0], norm_mix_post=gpost1[0], norm_ffn_pre=gpre2[0], norm_ffn_post=gpost2[0])
    small["sgu_w"] = dws
    small["sgu_b"] = jnp.sum(dgb.reshape(CHUNK, 4, 64), axis=-1).T
    small.update(_ssm_param_grads(sp, dbh, dch, dar, dai, n + "disc"))
    small["ssm_d"] = ddvec.reshape(SSM_G, SSM_H)
    small["glu_b"] = dbglu[0]
    small["pool_w"] = jnp.stack([dpw[64 * k:64 * k + 64, 64 * k:64 * k + 64] for k in range(4)])
    small["pool_scale"] = dpsc[0]
    dmod = jnp.concatenate([dsh1, dsc1, dg1, dsh2, dsc2, dg2], axis=1)[:lay.bl + 1]
    dmod = jnp.concatenate([dmod, jnp.zeros((8 - lay.bl - 1, 6, D), F32)], axis=0)
    return dx, big, small, dmod, early


def _perm_wout(w):
    return w.reshape(4, D // 4, D)[np.array(WOUT_PERM)].reshape(D, D)


def _unperm_wout(g):
    return g.reshape(4, D // 4, D)[np.array(WOUT_INV)].reshape(D, D)


SMALL_NAMES = ["norm_mix_pre", "norm_mix_post", "norm_ffn_pre", "norm_ffn_post", "sgu_w", "sgu_b", "ssm_lam_re",
               "ssm_lam_im", "ssm_log_dt", "ssm_b_re", "ssm_b_im", "ssm_c_re", "ssm_c_im", "ssm_d", "glu_b", "pool_w",
               "pool_scale"]
BIG_NAMES = ["win_t", "wout", "wglu", "wg_t", "wu_t", "wd"]


def _sincos_2d(rows, cols, dim):
    quarter = dim // 4
    omega = 1.0 / (10000.0 ** (jnp.arange(quarter, dtype=F32) / quarter))
    r = jnp.arange(rows, dtype=F32)[:, None] * omega
    cc = jnp.arange(cols, dtype=F32)[:, None] * omega
    er = jnp.concatenate([jnp.sin(r), jnp.cos(r)], axis=-1)
    ec = jnp.concatenate([jnp.sin(cc), jnp.cos(cc)], axis=-1)
    pe = jnp.concatenate([jnp.broadcast_to(er[:, None, :], (rows, cols, dim // 2)),
                          jnp.broadcast_to(ec[None, :, :], (rows, cols, dim // 2))], axis=-1)
    return pe.reshape(rows * cols, dim)


def _core(x, ctx, target, mods_local, params, weights, w_side=None, w_fill=None, g_side_fn=None):
    bl, lat, _ = x.shape
    assert bl == 4 and lat % TB == 0, "the scan fills 8 sublanes with 2 directions x 4 sequences"
    lay = _Layout(bl, lat)
    pe = _sincos_2d(lat // GRID_W, GRID_W, D)
    xt = _embed(lay, x.reshape(bl * lat, D), ctx.reshape(bl * CTX, D), pe)
    bands_np, inv_np = _band_constants()
    bands, inv = jnp.asarray(bands_np, MXU_DTYPE), jnp.asarray(inv_np, F32)
    perm = jnp.asarray(_scan_perm(bl), MXU_DTYPE)
    rows = lay.modrows_static()
    modarrs, csts, sps, ress, wls = [], [], [], [], []
    for i in range(2):
        modarrs.append(mods_local[i][rows].reshape(lay.nb * 6, 1, D))
        csts.append(_layer_consts(params[i]))
        p = params[i]
        sps.append(_ssm_params(p["ssm_lam_re"], p["ssm_lam_im"], p["ssm_log_dt"], p["ssm_b_re"], p["ssm_b_im"],
                               p["ssm_c_re"], p["ssm_c_im"], f"l{i}_disc"))
        wls.append(dict(weights[i]))

    w_side = w_side or {}

    def fill_of(key):
        def fill(extra):
            if key in w_side:
                w_fill[key](wls, extra)
            if key == "ssm":
                for w in wls:
                    w["wout"] = _perm_wout(w["wout"])
        return fill

    sides0 = {key: (w_side.get(key), fill_of(key)) for key in ("ssm", "ffn")}
    for i in range(2):
        xt, res = _layer_fwd(lay, i, xt, modarrs[i], params[i], wls[i], csts[i], sps[i], bands, inv, perm,
                             sides0 if i == 0 else None)
        ress.append(res)
    dx, lossv = _loss_bwd(lay, xt, target.reshape(bl * lat, D))
    bigs, smalls, dmods, early = [None, None], [None, None], [None, None], []
    for i in (1, 0):
        side_fn = (lambda big0: g_side_fn(bigs[1], big0)) if (i == 0 and g_side_fn is not None) else None
        dx, bigs[i], smalls[i], dmods[i], ex = _layer_bwd(lay, i, dx, modarrs[i], params[i], wls[i], csts[i], sps[i],
                                                           bands, inv, perm, ress[i], side_fn)
        early += ex
    return lossv[0, 0], dx.reshape(bl, lat, D), bigs, smalls, dmods, early


def _my_index():
    return 4 * lax.axis_index("x") + 2 * lax.axis_index("y") + lax.axis_index("c")


def _peer(k):
    x, y, c = lax.axis_index("x"), lax.axis_index("y"), lax.axis_index("c")
    kx, ky, kc = (k >> 2) & 1, (k >> 1) & 1, k & 1
    px = 1 - x if kx else x
    py = 1 - y if ky else y
    pc = 1 - c if kc else c
    return (px, py, pc), 4 * px + 2 * py + pc


class _Side:
    def __init__(self, items):
        self.items = items
        self.n = len(items)
        self.ncopies = sum(len(it[2]) for it in items)
        self.arrays = [it[0] for it in items]
        anyspec = pl.BlockSpec(memory_space=pl.ANY)
        self.in_specs = [anyspec] * self.n
        self.out_specs = [anyspec] * self.n
        self.out_shape = [jax.ShapeDtypeStruct((slots,) + tuple(a.shape) if mode == "gather" else tuple(a.shape), a.dtype)
                          for a, mode, ks, slots in items]
        self.scratch = [pltpu.SemaphoreType.DMA((self.ncopies,)), pltpu.SemaphoreType.DMA((self.ncopies,)),
                        pltpu.SemaphoreType.DMA((self.n,))]

    def _copies(self, ins, outs, sems):
        send_sems, recv_sems, local_sems = sems
        slot_of = lambda idx, slots: idx if slots == 8 else (idx // 2 if slots == 4 else idx % 2)
        me = _my_index()
        local, sends, recvs = [], [], []
        q = 0
        for t, (arr, mode, ks, slots) in enumerate(self.items):
            src_own = ins[t] if mode == "gather" else ins[t].at[me]
            local.append(pltpu.make_async_copy(src_own, outs[t].at[slot_of(me, slots)], local_sems.at[t]))
            for k in ks:
                peer, pidx = _peer(k)
                src = ins[t] if mode == "gather" else ins[t].at[pidx]
                sends.append(pltpu.make_async_remote_copy(
                    src_ref=src, dst_ref=outs[t].at[slot_of(me, slots)], send_sem=send_sems.at[q], recv_sem=recv_sems.at[q],
                    device_id=peer, device_id_type=pl.DeviceIdType.MESH))
                recvs.append(pltpu.make_async_remote_copy(
                    src_ref=src, dst_ref=outs[t].at[slot_of(pidx, slots)], send_sem=send_sems.at[q], recv_sem=recv_sems.at[q],
                    device_id=peer, device_id_type=pl.DeviceIdType.MESH))
                q += 1
        return local, sends, recvs

    def start(self, ins, outs, sems):
        local, sends, _ = self._copies(ins, outs, sems)
        for cp in sends + local:
            cp.start()

    def wait(self, ins, outs, sems):
        local, sends, recvs = self._copies(ins, outs, sems)
        for cp in recvs:
            cp.wait_recv()
        for cp in sends:
            cp.wait_send()
        for cp in local:
            cp.wait()


def _comm(items, name):
    side = _Side(items)
    n = side.n

    def body(*refs):
        ins, outs, sems = refs[:n], refs[n:2 * n], refs[2 * n:]
        side.start(ins, outs, sems)
        side.wait(ins, outs, sems)

    return pl.pallas_call(
        body, in_specs=side.in_specs, out_specs=side.out_specs, out_shape=side.out_shape, scratch_shapes=side.scratch,
        compiler_params=pltpu.CompilerParams(has_side_effects=True), name=name)(*side.arrays)


def _spread(items, name):
    n = len(items)
    ncopies = sum(len(it[1]) for it in items)

    def slot_of(idx, slots):
        return idx if slots == 8 else (idx // 2 if slots == 4 else idx % 2)

    def body(*refs):
        ins, outs, bufs = refs[:n], refs[n:2 * n], refs[2 * n:3 * n]
        load_sems, store_sems, send_sems, recv_sems = refs[3 * n:]
        me = _my_index()
        loads = [pltpu.make_async_copy(ins[t], bufs[t], load_sems.at[t]) for t in range(n)]
        for cp in loads:
            cp.start()
        stores, sends, recvs = [], [], []
        q = 0
        for t, (arr, ks, slots) in enumerate(items):
            loads[t].wait()
            own = outs[t].at[slot_of(me, slots)]
            stores.append(pltpu.make_async_copy(bufs[t], own, store_sems.at[t]))
            stores[-1].start()
            for k in ks:
                peer, pidx = _peer(k)
                sends.append(pltpu.make_async_remote_copy(
                    src_ref=bufs[t], dst_ref=own, send_sem=send_sems.at[q], recv_sem=recv_sems.at[q],
                    device_id=peer, device_id_type=pl.DeviceIdType.MESH))
                recvs.append(pltpu.make_async_remote_copy(
                    src_ref=bufs[t], dst_ref=outs[t].at[slot_of(pidx, slots)], send_sem=send_sems.at[q],
                    recv_sem=recv_sems.at[q], device_id=peer, device_id_type=pl.DeviceIdType.MESH))
                sends[-1].start()
                q += 1
        for cp in recvs:
            cp.wait_recv()
        for cp in sends:
            cp.wait_send()
        for cp in stores:
            cp.wait()

    anyspec = pl.BlockSpec(memory_space=pl.ANY)
    return pl.pallas_call(
        body, in_specs=[anyspec] * n, out_specs=[anyspec] * n,
        out_shape=[jax.ShapeDtypeStruct((slots,) + tuple(arr.shape), arr.dtype) for arr, ks, slots in items],
        scratch_shapes=[pltpu.VMEM(tuple(arr.shape), arr.dtype) for arr, ks, slots in items]
        + [pltpu.SemaphoreType.DMA((n,)), pltpu.SemaphoreType.DMA((n,)), pltpu.SemaphoreType.DMA((ncopies,)),
           pltpu.SemaphoreType.DMA((ncopies,))],
        compiler_params=pltpu.CompilerParams(has_side_effects=True, vmem_limit_bytes=VMEM_LIMIT),
        name=name)(*[it[0] for it in items])


ALL7 = (1, 2, 3, 4, 5, 6, 7)
CHIPS3 = (2, 4, 6)


def _sum8(parts, name):
    def one(a, nm):
        _, r, c = a.shape
        tr = r if r <= 512 else _pick_rows(r)

        def body(a_ref, o_ref):
            acc = a_ref[0].astype(F32)
            for q in range(1, a_ref.shape[0]):
                acc = acc + a_ref[q].astype(F32)
            o_ref[...] = acc

        return pl.pallas_call(
            body, grid=(r // tr,), in_specs=[pl.BlockSpec((a.shape[0], tr, c), lambda i: (0, i, 0))],
            out_specs=pl.BlockSpec((tr, c), lambda i: (i, 0)), out_shape=jax.ShapeDtypeStruct((r, c), F32),
            compiler_params=_cp(("parallel",)), name=nm)(a)

    return [one(a, f"{name}{i}") for i, a in enumerate(parts)]


def _pick_rows(r, cap=512):
    for t in (512, 352, 256, 176, 128, 64, 32, 16, 8):
        if r % t == 0 and t <= cap:
            return t
    return r


def _adam(w, g, m, v, name):
    shape = w.shape
    nel = int(np.prod(shape))
    if len(shape) >= 2 and shape[-1] >= 128:
        lanes = shape[-1]
    else:
        lanes = 512 if nel % 512 == 0 else 128
    r = nel // lanes
    tr = r if r * lanes <= 384 * 1024 else _pick_rows(r, 384 * 1024 // lanes)
    c1 = 1.0 / (1.0 - ADAM_B1 ** ADAM_STEP)
    c2 = 1.0 / (1.0 - ADAM_B2 ** ADAM_STEP)

    def body(w_ref, g_ref, m_ref, v_ref, d_ref, nm_ref, nv_ref):
        gv = g_ref[...]
        nm = ADAM_B1 * m_ref[...] + (1.0 - ADAM_B1) * gv
        nv = ADAM_B2 * v_ref[...] + (1.0 - ADAM_B2) * (gv * gv)
        d_ref[...] = -ADAM_LR * ((nm * c1) / (jnp.sqrt(nv * c2) + ADAM_EPS) + ADAM_WD * w_ref[...])
        nm_ref[...] = nm
        nv_ref[...] = nv

    spec = pl.BlockSpec((tr, lanes), lambda i: (i, 0))
    sh = jax.ShapeDtypeStruct((r, lanes), F32)
    outs = pl.pallas_call(
        body, grid=(r // tr,), in_specs=[spec] * 4, out_specs=[spec] * 3, out_shape=[sh] * 3,
        compiler_params=_cp(("parallel",)), name=name)(*[a.reshape(r, lanes) for a in (w, g, m, v)])
    return [o.reshape(shape) for o in outs]


def _silu(x):
    return x * _sigmoid(x)


def _mod_fwd(c_rows, w_mod, b_cols, name):
    def body(c_ref, w_ref, b_ref, o_ref):
        s = _silu(c_ref[...])
        for l in range(2):
            o_ref[l] = jnp.dot(s, w_ref[l], preferred_element_type=F32, precision=lax.Precision.HIGHEST) + b_ref[l]

    nc = w_mod.shape[2]
    return pl.pallas_call(body, out_shape=jax.ShapeDtypeStruct((2, c_rows.shape[0], nc), F32),
                          compiler_params=_cp(None), name=name)(c_rows, w_mod, b_cols)


def _mod_bwd(c_rows, w_mod, dlat, dctx8, name):
    nrow = c_rows.shape[0]
    nb = nrow - 8

    def body(c_ref, w_ref, dl_ref, dc_ref, gw_ref, gc_ref):
        s = _silu(c_ref[...])
        ctx_row = lax.broadcasted_iota(jnp.int32, (nrow, 1), 0) == nb
        gc = jnp.zeros((1, D), F32)
        for l in range(2):
            dctx = dc_ref[0, l]
            for q in range(1, 8):
                dctx = dctx + dc_ref[q, l]
            dm = dl_ref[l] + jnp.where(ctx_row, dctx, 0.0)
            gw_ref[l] = lax.dot_general(s, dm, (((0,), (0,)), ((), ())), preferred_element_type=F32,
                                        precision=lax.Precision.HIGHEST)
            gc = gc + lax.dot_general(dctx, w_ref[l], (((1,), (1,)), ((), ())), preferred_element_type=F32,
                                      precision=lax.Precision.HIGHEST)
        gc_ref[...] = gc

    nc = w_mod.shape[2]
    return pl.pallas_call(body, out_shape=[jax.ShapeDtypeStruct((2, D, nc), F32), jax.ShapeDtypeStruct((1, D), F32)],
                          compiler_params=_cp(None), name=name)(c_rows, w_mod, dlat, dctx8)


def _bmod_cctx(dmod_all, gc4, c_ctx, name):
    def body(dm_ref, gc_ref, cc_ref, gb_ref, gcc_ref):
        for l in range(2):
            acc = jnp.sum(dm_ref[0, l], axis=0, keepdims=True)
            for q in range(1, 8):
                acc = acc + jnp.sum(dm_ref[q, l], axis=0, keepdims=True)
            gb_ref[l:l + 1, :] = acc
        g = gc_ref[0] + gc_ref[1] + gc_ref[2] + gc_ref[3]
        cv = cc_ref[...]
        sg = _sigmoid(cv)
        gcc_ref[...] = g * (sg * (1.0 + cv * (1.0 - sg)))

    return pl.pallas_call(body, out_shape=[jax.ShapeDtypeStruct((2, 6 * D), F32), jax.ShapeDtypeStruct((1, D), F32)],
                          compiler_params=_cp(None), name=name)(dmod_all, gc4, c_ctx)


def kernel(x, c, ctx, c_ctx, w_mod, b_mod, norm_mix_pre, norm_mix_post, norm_ffn_pre, norm_ffn_post, w_in, w_out, sgu_w, sgu_b, ssm_lam_re, ssm_lam_im, ssm_log_dt, ssm_b_re, ssm_b_im, ssm_c_re, ssm_c_im, ssm_d, glu_w, glu_b, pool_w, pool_scale, ffn_w_gate, ffn_w_up, ffn_w_down, loss_target, m_c_ctx, m_w_mod, m_b_mod, m_norm_mix_pre, m_norm_mix_post, m_norm_ffn_pre, m_norm_ffn_post, m_w_in, m_w_out, m_sgu_w, m_sgu_b, m_ssm_lam_re, m_ssm_lam_im, m_ssm_log_dt, m_ssm_b_re, m_ssm_b_im, m_ssm_c_re, m_ssm_c_im, m_ssm_d, m_glu_w, m_glu_b, m_pool_w, m_pool_scale, m_ffn_w_gate, m_ffn_w_up, m_ffn_w_down, v_c_ctx, v_w_mod, v_b_mod, v_norm_mix_pre, v_norm_mix_post, v_norm_ffn_pre, v_norm_ffn_post, v_w_in, v_w_out, v_sgu_w, v_sgu_b, v_ssm_lam_re, v_ssm_lam_im, v_ssm_log_dt, v_ssm_b_re, v_ssm_b_im, v_ssm_c_re, v_ssm_c_im, v_ssm_d, v_glu_w, v_glu_b, v_pool_w, v_pool_scale, v_ffn_w_gate, v_ffn_w_up, v_ffn_w_down):
    wts = dict(c_ctx=c_ctx, w_mod=w_mod, b_mod=b_mod, norm_mix_pre=norm_mix_pre, norm_mix_post=norm_mix_post,
               norm_ffn_pre=norm_ffn_pre, norm_ffn_post=norm_ffn_post, w_in=w_in, w_out=w_out, sgu_w=sgu_w, sgu_b=sgu_b,
               ssm_lam_re=ssm_lam_re, ssm_lam_im=ssm_lam_im, ssm_log_dt=ssm_log_dt, ssm_b_re=ssm_b_re, ssm_b_im=ssm_b_im,
               ssm_c_re=ssm_c_re, ssm_c_im=ssm_c_im, ssm_d=ssm_d, glu_w=glu_w, glu_b=glu_b, pool_w=pool_w,
               pool_scale=pool_scale, ffn_w_gate=ffn_w_gate, ffn_w_up=ffn_w_up, ffn_w_down=ffn_w_down)
    ms = dict(c_ctx=m_c_ctx, w_mod=m_w_mod, b_mod=m_b_mod, norm_mix_pre=m_norm_mix_pre, norm_mix_post=m_norm_mix_post,
              norm_ffn_pre=m_norm_ffn_pre, norm_ffn_post=m_norm_ffn_post, w_in=m_w_in, w_out=m_w_out, sgu_w=m_sgu_w,
              sgu_b=m_sgu_b, ssm_lam_re=m_ssm_lam_re, ssm_lam_im=m_ssm_lam_im, ssm_log_dt=m_ssm_log_dt,
              ssm_b_re=m_ssm_b_re, ssm_b_im=m_ssm_b_im, ssm_c_re=m_ssm_c_re, ssm_c_im=m_ssm_c_im, ssm_d=m_ssm_d,
              glu_w=m_glu_w, glu_b=m_glu_b, pool_w=m_pool_w, pool_scale=m_pool_scale, ffn_w_gate=m_ffn_w_gate,
              ffn_w_up=m_ffn_w_up, ffn_w_down=m_ffn_w_down)
    vs = dict(c_ctx=v_c_ctx, w_mod=v_w_mod, b_mod=v_b_mod, norm_mix_pre=v_norm_mix_pre, norm_mix_post=v_norm_mix_post,
              norm_ffn_pre=v_norm_ffn_pre, norm_ffn_post=v_norm_ffn_post, w_in=v_w_in, w_out=v_w_out, sgu_w=v_sgu_w,
              sgu_b=v_sgu_b, ssm_lam_re=v_ssm_lam_re, ssm_lam_im=v_ssm_lam_im, ssm_log_dt=v_ssm_log_dt,
              ssm_b_re=v_ssm_b_re, ssm_b_im=v_ssm_b_im, ssm_c_re=v_ssm_c_re, ssm_c_im=v_ssm_c_im, ssm_d=v_ssm_d,
              glu_w=v_glu_w, glu_b=v_glu_b, pool_w=v_pool_w, pool_scale=v_pool_scale, ffn_w_gate=v_ffn_w_gate,
              ffn_w_up=v_ffn_w_up, ffn_w_down=v_ffn_w_down)
    order = list(wts.keys())
    bl = x.shape[0]
    nseq = bl * N_DEV
    me = _my_index()
    chip = me // 2
    ncol = w_mod.shape[2]

    (c_all,) = _spread([(c, ALL7, 8)], "ag_c")
    nrow = nseq + 8
    c_rows = jnp.concatenate([c_all.reshape(nseq, D), c_ctx[None], jnp.zeros((7, D), F32)], axis=0)
    b_cols = lax.dynamic_slice_in_dim(b_mod, chip * ncol, ncol, axis=1)[:, None, :]
    mod_cols = _mod_fwd(c_rows, w_mod, b_cols, "mod_fwd")
    (mod4,) = _spread([(mod_cols, CHIPS3, 4)], "ag_mod")
    mods = jnp.transpose(mod4, (1, 2, 0, 3)).reshape(2, nrow, 6 * D)
    mods_local = jnp.concatenate([lax.dynamic_slice_in_dim(mods, me * bl, bl, axis=1), mods[:, nseq:nseq + 1],
                                  jnp.zeros((2, 8 - bl - 1, 6 * D), F32)], axis=1)

    shards = {}
    for i in range(2):
        for nme, s in zip(BIG_NAMES, [w_in[i].T, w_out[i], glu_w[i], ffn_w_gate[i].T, ffn_w_up[i].T, ffn_w_down[i]]):
            shards[(i, nme)] = s.astype(MXU_DTYPE)
    (win0,) = _comm([(shards[(0, "win_t")], "gather", CHIPS3, 4)], "ag_win0")
    weights = [{"win_t": win0.reshape(-1, D)}, {}]
    ffn_names = ("wg_t", "wu_t", "wd")
    late_w = {"ssm": [key for key in shards if key != (0, "win_t") and not (key[0] == 1 and key[1] in ffn_names)],
              "ffn": [(1, nme) for nme in ffn_names]}
    w_side = {key: _Side([(shards[k2], "gather", CHIPS3, 4) for k2 in late_w[key]]) for key in late_w}

    def filler(key):
        def w_fill(wls, gathered):
            for (i, nme), g in zip(late_w[key], gathered):
                wls[i][nme] = g.reshape(-1, g.shape[-1])
        return w_fill

    w_fill = {key: filler(key) for key in late_w}

    eighths = lambda g: g.reshape(8, g.shape[0] // 8, g.shape[1])
    early_g = [(1, k) for k in BIG_NAMES] + [(0, k) for k in BIG_NAMES if k != "win_t"]

    def g_side_fn(big1, big0):
        return _Side([(eighths((big1 if i == 1 else big0)[k]), "a2a", ALL7, 8) for i, k in early_g])

    params = [{k: wts[k][i] for k in SMALL_NAMES} for i in range(2)]
    loss_part, grad_x, bigs, smalls, dmods, early = _core(x, ctx, loss_target, mods_local, params, weights,
                                                           w_side, w_fill, g_side_fn)
    loss = lax.psum(loss_part, ("x", "y", "c"))

    dmod_local = jnp.stack([dmods[i].reshape(8, 6 * D) for i in range(2)])
    (dmod_all,) = _spread([(dmod_local, ALL7, 8)], "ag_dmod")
    dcols = lax.dynamic_slice_in_dim(dmod_all, chip * ncol, ncol, axis=3)
    dlat = jnp.transpose(dcols[:, :, :bl], (1, 0, 2, 3)).reshape(2, nseq, ncol)
    dlat = jnp.concatenate([dlat, jnp.zeros((2, 8, ncol), F32)], axis=1)
    dctx8 = dcols[:, :, bl:bl + 1]
    g_w_mod, gc_part = _mod_bwd(c_rows, w_mod, dlat, dctx8, "mod_bwd")
    (gc4,) = _spread([(gc_part, CHIPS3, 4)], "ag_cctx")
    g_b_mod, g_c_ctx = _bmod_cctx(dmod_all, gc4, c_ctx[None], "bmod_cctx")

    small_flat = jnp.concatenate([jnp.stack([smalls[i][k] for i in range(2)]).reshape(-1) for k in SMALL_NAMES])
    npad = (-small_flat.shape[0]) % (8 * 1024)
    small_flat = jnp.concatenate([small_flat, jnp.zeros((npad,), F32)])
    late = _comm([(eighths(bigs[0]["win_t"]), "a2a", ALL7, 8), (small_flat.reshape(8, -1, 1024), "a2a", ALL7, 8)],
                 "a2a_grads")
    sums = _sum8(list(early) + list(late), "gsum")
    fin = _spread([(s, (1,), 2) for s in sums[:-1]] + [(sums[-1], ALL7, 8)], "ag_grads")
    big_g = [{}, {}]
    for (i, k), g in zip(early_g + [(0, "win_t")], fin[:-1]):
        big_g[i][k] = g.reshape(-1, g.shape[-1])
    small_red = fin[-1].reshape(-1)

    grads = {}
    off = 0
    for k in SMALL_NAMES:
        shp = wts[k].shape
        nel = int(np.prod(shp))
        grads[k] = small_red[off:off + nel].reshape(shp)
        off += nel
    grads["c_ctx"] = g_c_ctx[0]
    grads["w_mod"] = g_w_mod
    grads["b_mod"] = g_b_mod
    grads["w_in"] = jnp.stack([big_g[i]["win_t"].T for i in range(2)])
    grads["w_out"] = jnp.stack([big_g[i]["wout"] for i in range(2)])
    grads["glu_w"] = jnp.stack([big_g[i]["wglu"] for i in range(2)])
    grads["ffn_w_gate"] = jnp.stack([big_g[i]["wg_t"].T for i in range(2)])
    grads["ffn_w_up"] = jnp.stack([big_g[i]["wu_t"].T for i in range(2)])
    grads["ffn_w_down"] = jnp.stack([big_g[i]["wd"] for i in range(2)])

    deltas, new_m, new_v = {}, {}, {}
    for k in order:
        deltas[k], new_m[k], new_v[k] = _adam(wts[k], grads[k], ms[k], vs[k], "adam_" + k)
    return (loss, grad_x, *[grads[k] for k in order], *[deltas[k] for k in order],
            *[new_m[k] for k in order], *[new_v[k] for k in order])
```

```python
import functools
import math

import numpy as np
import jax
import jax.numpy as jnp
from jax import lax
from jax.experimental import pallas as pl
from jax.experimental.pallas import tpu as pltpu

F32 = jnp.float32
BF16 = jnp.bfloat16
MXU_DTYPE = jnp.bfloat16
MCAT_A, MCAT_C = 2, 3
WOUT_PERM, WOUT_INV = (1, 2, 0, 3), (2, 0, 1, 3)

D = 1024
EPS = 1e-6
TB = 256
CTX = 256
CHUNK = 128
GRID_W = 64
A_W, B_W, C_W = 256, 512, 256
D_IN = 1280
D_FF = 2816
SSM_G, SSM_P, SSM_H = 32, 64, 16
ST = 64
POOL_WINDOWS = (2, 4, 8, 16)
N_DEV = 8
VMEM_LIMIT = 52 * 1024 * 1024
GELU_C = math.sqrt(2.0 / math.pi)

ADAM_LR, ADAM_B1, ADAM_B2, ADAM_EPS, ADAM_WD, ADAM_STEP = 0.001, 0.9, 0.999, 1e-08, 0.01, 10


def _cp(sem=None, vmem=VMEM_LIMIT, **kw):
    return pltpu.CompilerParams(dimension_semantics=sem, vmem_limit_bytes=vmem, **kw)


def _pick(n, cap):
    if n <= cap:
        return n
    best = None
    for t in range(128, cap + 1, 128):
        if n % t == 0:
            best = t
    assert best is not None, (n, cap)
    return best


def _gelu(x):
    return 0.5 * x * (1.0 + jnp.tanh(GELU_C * (x + 0.044715 * x * x * x)))


def _gelu_grad(x):
    t = jnp.tanh(GELU_C * (x + 0.044715 * x * x * x))
    return 0.5 * (1.0 + t) + 0.5 * x * (1.0 - t * t) * GELU_C * (1.0 + 3.0 * 0.044715 * x * x)


def _sigmoid(x):
    return 1.0 / (1.0 + jnp.exp(-x))


def _dot(a, b, dims):
    return lax.dot_general(a, b, (dims, ((), ())), preferred_element_type=F32)


def _nn(a, b):
    return _dot(a, b, ((1,), (0,)))


def _nt(a, b):
    return _dot(a, b, ((1,), (1,)))


def _tn(a, b):
    return _dot(a, b, ((0,), (0,)))


def _mm(pairs, nt, out_dtype, name, tm=512):
    m = pairs[0][0].shape[0]
    n = pairs[0][1].shape[0] if nt else pairs[0][1].shape[1]
    tn = _pick(n, 1408)
    tm = min(tm, m)
    npairs = len(pairs)

    def body(*refs):
        o_ref = refs[-1]
        acc = None
        for i in range(npairs):
            a = refs[2 * i][...].astype(MXU_DTYPE)
            b = refs[2 * i + 1][...].astype(MXU_DTYPE)
            r = _nt(a, b) if nt else _nn(a, b)
            acc = r if acc is None else acc + r
        o_ref[...] = acc.astype(o_ref.dtype)

    in_specs, flat = [], []
    for a, b in pairs:
        k = a.shape[1]
        in_specs.append(pl.BlockSpec((tm, k), lambda i, j: (i, 0)))
        in_specs.append(pl.BlockSpec((tn, k), lambda i, j: (j, 0)) if nt else pl.BlockSpec((k, tn), lambda i, j: (0, j)))
        flat += [a, b]
    return pl.pallas_call(
        body, grid=(m // tm, n // tn), in_specs=in_specs,
        out_specs=pl.BlockSpec((tm, tn), lambda i, j: (i, j)),
        out_shape=jax.ShapeDtypeStruct((m, n), out_dtype),
        compiler_params=_cp(("parallel", "parallel")), name=name)(*flat)


def _mm_tn(a, b, out_dtype, name, tm=512):
    m, k1 = a.shape
    n = b.shape[1]
    t1 = _pick(k1, 1408)
    tn = _pick(n, 1024)
    tm = min(tm, m)
    nsteps = m // tm

    def body(a_ref, b_ref, o_ref, acc_ref):
        t = pl.program_id(2)

        @pl.when(t == 0)
        def _():
            acc_ref[...] = jnp.zeros_like(acc_ref)

        acc_ref[...] += _tn(a_ref[...].astype(MXU_DTYPE), b_ref[...].astype(MXU_DTYPE))

        @pl.when(t == nsteps - 1)
        def _():
            o_ref[...] = acc_ref[...].astype(o_ref.dtype)

    return pl.pallas_call(
        body, grid=(k1 // t1, n // tn, nsteps),
        in_specs=[pl.BlockSpec((tm, t1), lambda i, j, t: (t, i)), pl.BlockSpec((tm, tn), lambda i, j, t: (t, j))],
        out_specs=pl.BlockSpec((t1, tn), lambda i, j, t: (i, j)),
        out_shape=jax.ShapeDtypeStruct((k1, n), out_dtype),
        scratch_shapes=[pltpu.VMEM((t1, tn), F32)],
        compiler_params=_cp(("parallel", "parallel", "arbitrary")), name=name)(a, b)


class _Layout:
    def __init__(self, bl, lat):
        self.bl, self.lat = bl, lat
        self.nlb = lat // TB
        self.nr = 1 + self.nlb
        self.nctx = bl
        self.nb = self.nr * bl
        self.nt = self.nb * TB
        self.ctx_row = bl

    def blk(self, g):
        gg = g - self.bl
        return jnp.where(g < self.bl, g, (gg % self.nlb + 1) * self.bl + gg // self.nlb)

    def modrow(self, g):
        return jnp.where(g < self.bl, self.ctx_row, (g - self.bl) // self.nlb)

    def first_of_row(self, g):
        return jnp.logical_or(g == 0, jnp.logical_and(g >= self.bl, (g - self.bl) % self.nlb == 0))

    def modrows_static(self):
        return np.array([self.ctx_row if j < self.bl else j % self.bl for j in range(self.nb)], np.int32)


def _tok_spec(lay):
    return pl.BlockSpec((TB, D), lambda g: (lay.blk(g), 0))


def _vec_spec():
    return pl.BlockSpec((1, D), lambda j: (0, 0))


def _mod_spec(lay, k):
    return pl.BlockSpec((1, 1, D), lambda g: (lay.blk(g) * 6 + k, 0, 0))


def _embed(lay, x2d, ctx2d, pe):
    bl, nlb = lay.bl, lay.nlb

    def body(x_ref, c_ref, pe_ref, o_ref):
        j = pl.program_id(0)

        @pl.when(j < bl)
        def _():
            o_ref[...] = c_ref[...]

        @pl.when(j >= bl)
        def _():
            o_ref[...] = x_ref[...] + pe_ref[...]

    pos = lambda j: jnp.maximum(j // bl - 1, 0)
    return pl.pallas_call(
        body, grid=(lay.nb,),
        in_specs=[pl.BlockSpec((TB, D), lambda j: ((j % bl) * nlb + pos(j), 0)),
                  pl.BlockSpec((TB, D), lambda j: (jnp.minimum(j, bl - 1), 0)),
                  pl.BlockSpec((TB, D), lambda j: (pos(j), 0))],
        out_specs=pl.BlockSpec((TB, D), lambda j: (j, 0)), out_shape=jax.ShapeDtypeStruct((lay.nt, D), F32),
        compiler_params=_cp(("parallel",)), name="embed")(x2d, ctx2d, pe)


def _normmod_fwd(lay, x, gain, modarr, ksh, ksc, name):
    def body(x_ref, g_ref, sh_ref, sc_ref, o_ref):
        xv = x_ref[...]
        r = lax.rsqrt(jnp.mean(xv * xv, axis=-1, keepdims=True) + EPS)
        o_ref[...] = ((xv * r * g_ref[...]) * (1.0 + sc_ref[0]) + sh_ref[0]).astype(o_ref.dtype)

    return pl.pallas_call(
        body, grid=(lay.nb,), in_specs=[_tok_spec(lay), _vec_spec(), _mod_spec(lay, ksh), _mod_spec(lay, ksc)],
        out_specs=_tok_spec(lay), out_shape=jax.ShapeDtypeStruct((lay.nt, D), MXU_DTYPE),
        compiler_params=_cp(("parallel",)), name=name)(x, gain, modarr, modarr)


def _acc_specs(lay):
    row = pl.BlockSpec((1, 1, D), lambda j: (lay.modrow(j), 0, 0))
    return row, jax.ShapeDtypeStruct((8, 1, D), F32)


def _normmod_bwd(lay, x, dh, dx_in, gain, modarr, ksc, name, latent_only=False):
    row_spec, row_shape = _acc_specs(lay)
    if latent_only:
        dx_spec = pl.BlockSpec((TB, D), lambda g: (jnp.maximum(g - lay.bl, 0), 0))
        dx_shape = jax.ShapeDtypeStruct((lay.bl * lay.lat, D), F32)
    else:
        dx_spec, dx_shape = _tok_spec(lay), jax.ShapeDtypeStruct((lay.nt, D), F32)

    def body(x_ref, dh_ref, dxi_ref, g_ref, sc_ref, dx_ref, dsh_ref, dsc_ref, dg_ref):
        j = pl.program_id(0)
        xv = x_ref[...]
        dhv = dh_ref[...].astype(F32)
        g = g_ref[...]
        sc1 = 1.0 + sc_ref[0]
        r = lax.rsqrt(jnp.mean(xv * xv, axis=-1, keepdims=True) + EPS)
        xh = xv * r
        dxh = dhv * (g * sc1)
        dx = r * (dxh - xh * jnp.mean(dxh * xh, axis=-1, keepdims=True))
        dx_ref[...] = dxi_ref[...] + dx

        @pl.when(lay.first_of_row(j))
        def _():
            dsh_ref[...] = jnp.zeros_like(dsh_ref)
            dsc_ref[...] = jnp.zeros_like(dsc_ref)

        @pl.when(j == 0)
        def _():
            dg_ref[...] = jnp.zeros_like(dg_ref)

        dsh_ref[0] += jnp.sum(dhv, axis=0, keepdims=True)
        dsc_ref[0] += jnp.sum(dhv * (xh * g), axis=0, keepdims=True)
        dg_ref[...] += jnp.sum(dhv * sc1 * xh, axis=0, keepdims=True)

    return pl.pallas_call(
        body, grid=(lay.nb,),
        in_specs=[_tok_spec(lay), _tok_spec(lay), _tok_spec(lay), _vec_spec(), _mod_spec(lay, ksc)],
        out_specs=[dx_spec, row_spec, row_spec, _vec_spec()],
        out_shape=[dx_shape, row_shape, row_shape, jax.ShapeDtypeStruct((1, D), F32)],
        compiler_params=_cp(("arbitrary",)), name=name)(x, dh, dx_in, gain, modarr)


def _resnorm_fwd(lay, x, m, gain, modarr, kgate, name):
    def body(x_ref, m_ref, g_ref, gate_ref, o_ref):
        mv = m_ref[...].astype(F32)
        r = lax.rsqrt(jnp.mean(mv * mv, axis=-1, keepdims=True) + EPS)
        o_ref[...] = x_ref[...] + gate_ref[0] * (mv * r * g_ref[...])

    return pl.pallas_call(
        body, grid=(lay.nb,), in_specs=[_tok_spec(lay), _tok_spec(lay), _vec_spec(), _mod_spec(lay, kgate)],
        out_specs=_tok_spec(lay), out_shape=jax.ShapeDtypeStruct((lay.nt, D), F32),
        compiler_params=_cp(("parallel",)), name=name)(x, m, gain, modarr)


def _resnorm_bwd(lay, dxn, m, gain, modarr, kgate, name):
    row_spec, row_shape = _acc_specs(lay)

    def body(d_ref, m_ref, g_ref, gate_ref, dm_ref, dgate_ref, dg_ref):
        j = pl.program_id(0)
        dv = d_ref[...]
        mv = m_ref[...].astype(F32)
        g = g_ref[...]
        r = lax.rsqrt(jnp.mean(mv * mv, axis=-1, keepdims=True) + EPS)
        xh = mv * r
        dy = dv * gate_ref[0]
        dxh = dy * g
        dm_ref[...] = (r * (dxh - xh * jnp.mean(dxh * xh, axis=-1, keepdims=True))).astype(dm_ref.dtype)

        @pl.when(lay.first_of_row(j))
        def _():
            dgate_ref[...] = jnp.zeros_like(dgate_ref)

        @pl.when(j == 0)
        def _():
            dg_ref[...] = jnp.zeros_like(dg_ref)

        dgate_ref[0] += jnp.sum(dv * (xh * g), axis=0, keepdims=True)
        dg_ref[...] += jnp.sum(dy * xh, axis=0, keepdims=True)

    return pl.pallas_call(
        body, grid=(lay.nb,), in_specs=[_tok_spec(lay), _tok_spec(lay), _vec_spec(), _mod_spec(lay, kgate)],
        out_specs=[_tok_spec(lay), row_spec, _vec_spec()],
        out_shape=[jax.ShapeDtypeStruct((lay.nt, D), MXU_DTYPE), row_shape, jax.ShapeDtypeStruct((1, D), F32)],
        compiler_params=_cp(("arbitrary",)), name=name)(dxn, m, gain, modarr)


def _loss_bwd(lay, xf, tgt2d):
    bl, nlb = lay.bl, lay.nlb

    def body(x_ref, t_ref, dx_ref, l_ref):
        j = pl.program_id(0)

        @pl.when(j == 0)
        def _():
            l_ref[...] = jnp.zeros_like(l_ref)

        @pl.when(j < bl)
        def _():
            dx_ref[...] = jnp.zeros_like(dx_ref)

        @pl.when(j >= bl)
        def _():
            e = x_ref[...] - t_ref[...]
            dx_ref[...] = e * (1.0 / D)
            l_ref[...] += jnp.sum(e * e) * (0.5 / D)

    tok = pl.BlockSpec((TB, D), lambda j: (j, 0))
    return pl.pallas_call(
        body, grid=(lay.nb,),
        in_specs=[tok, pl.BlockSpec((TB, D), lambda j: ((j % bl) * nlb + jnp.maximum(j // bl - 1, 0), 0))],
        out_specs=[tok, pl.BlockSpec((8, 128), lambda j: (0, 0))],
        out_shape=[jax.ShapeDtypeStruct((lay.nt, D), F32), jax.ShapeDtypeStruct((8, 128), F32)],
        compiler_params=_cp(("arbitrary",)), name="loss")(xf, tgt2d)


FF_TN = D_FF // 2
FF_CHUNKS = ((0, 512), (512, 512), (1024, 384))


def _ffn_up(h, wgt, wut, name, side=None):
    m = h.shape[0]
    tm, tn = min(512, m), FF_TN
    ni, nj = m // tm, D_FF // tn

    def body(*refs):
        (h_ref, wg_ref, wu_ref, g_ref, u_ref, a_ref), srefs = _side_split(refs, 3, 3, 0, side)
        j, i = pl.program_id(0), pl.program_id(1)
        _side_start(side, srefs, jnp.logical_and(i == 0, j == 0))
        hv = h_ref[...]
        for c0, cw in FF_CHUNKS:
            g = _nt(hv, wg_ref[c0:c0 + cw, :])
            u = _nt(hv, wu_ref[c0:c0 + cw, :])
            g_ref[:, c0:c0 + cw] = g.astype(g_ref.dtype)
            u_ref[:, c0:c0 + cw] = u.astype(u_ref.dtype)
            a_ref[:, c0:c0 + cw] = (g * _sigmoid(g) * u).astype(a_ref.dtype)
        _side_wait(side, srefs, jnp.logical_and(i == ni - 1, j == nj - 1))

    sd = side if side is not None else _Side([])
    osp = pl.BlockSpec((tm, tn), lambda j, i: (i, j))
    osh = jax.ShapeDtypeStruct((m, D_FF), MXU_DTYPE)
    outs = pl.pallas_call(
        body, grid=(nj, ni),
        in_specs=[pl.BlockSpec((tm, D), lambda j, i: (i, 0)), pl.BlockSpec((tn, D), lambda j, i: (j, 0)),
                  pl.BlockSpec((tn, D), lambda j, i: (j, 0))] + sd.in_specs,
        out_specs=[osp, osp, osp] + sd.out_specs, out_shape=[osh, osh, osh] + sd.out_shape,
        scratch_shapes=sd.scratch if side is not None else [],
        compiler_params=_cp(("arbitrary", "arbitrary") if side is not None else ("parallel", "parallel")),
        name=name)(h, wgt, wut, *sd.arrays)
    return outs[0], outs[1], outs[2], list(outs[3:])


def _ffn_down_bwd(df, wd, g, u, name):
    m = df.shape[0]
    tm, tn = min(512, m), FF_TN

    def body(df_ref, wd_ref, g_ref, u_ref, dg_ref, du_ref):
        dfv = df_ref[...]
        for c0, cw in FF_CHUNKS:
            da = _nt(dfv, wd_ref[c0:c0 + cw, :])
            gv = g_ref[:, c0:c0 + cw].astype(F32)
            uv = u_ref[:, c0:c0 + cw].astype(F32)
            s = _sigmoid(gv)
            dg_ref[:, c0:c0 + cw] = (da * uv * (s * (1.0 + gv * (1.0 - s)))).astype(dg_ref.dtype)
            du_ref[:, c0:c0 + cw] = (da * gv * s).astype(du_ref.dtype)

    osp = pl.BlockSpec((tm, tn), lambda j, i: (i, j))
    osh = jax.ShapeDtypeStruct((m, D_FF), MXU_DTYPE)
    return pl.pallas_call(
        body, grid=(D_FF // tn, m // tm),
        in_specs=[pl.BlockSpec((tm, D), lambda j, i: (i, 0)), pl.BlockSpec((tn, D), lambda j, i: (j, 0)), osp, osp],
        out_specs=[osp, osp], out_shape=[osh, osh],
        compiler_params=_cp(("parallel", "parallel")), name=name)(df, wd, g, u)


def _head_masks(shape):
    lane = lax.broadcasted_iota(jnp.int32, shape, 1)
    return [jnp.logical_and(lane >= 64 * h, lane < 64 * h + 64) for h in range(4)]


def _head_mean(x, masks):
    out = jnp.zeros_like(x)
    for mk in masks:
        s = jnp.sum(jnp.where(mk, x, 0.0), axis=-1, keepdims=True) * (1.0 / 64.0)
        out = jnp.where(mk, s, out)
    return out


def _gate_common(z, masks):
    zg = _gelu(z)
    u = zg[:, :A_W]
    v = zg[:, A_W:]
    mu = _head_mean(v, masks)
    vc = v - mu
    rstd = lax.rsqrt(_head_mean(vc * vc, masks) + EPS)
    return u, vc * rstd, rstd


def _gate_s(vn, ws_ref, bias, masks):
    parts = []
    for c in range(TB // CHUNK):
        vc = vn[c * CHUNK:(c + 1) * CHUNK]
        s = bias
        for h in range(4):
            s = s + _nn(ws_ref[h], jnp.where(masks[h][:CHUNK], vc, 0.0).astype(MXU_DTYPE))
        parts.append(s)
    return jnp.concatenate(parts, axis=0)


def _gate_fwd(lay, z, ws, bias, name):
    def body(z_ref, ws_ref, b_ref, o_ref):
        masks = _head_masks((TB, A_W))
        u, vn, _ = _gate_common(z_ref[...].astype(F32), masks)
        o_ref[...] = (u * _gate_s(vn, ws_ref, b_ref[...], masks)).astype(o_ref.dtype)

    return pl.pallas_call(
        body, grid=(lay.nb,),
        in_specs=[pl.BlockSpec((TB, 2 * A_W), lambda j: (j, 0)), pl.BlockSpec((4, CHUNK, CHUNK), lambda j: (0, 0, 0)),
                  pl.BlockSpec((CHUNK, A_W), lambda j: (0, 0))],
        out_specs=pl.BlockSpec((TB, A_W), lambda j: (j, 0)),
        out_shape=jax.ShapeDtypeStruct((lay.nt, A_W), MXU_DTYPE),
        compiler_params=_cp(("parallel",)), name=name)(z, ws, bias)


def _gate_bwd(lay, z, da, ws, wst, bias, name):
    def body(z_ref, da_ref, ws_ref, wst_ref, b_ref, dz_ref, dws_ref, db_ref):
        j = pl.program_id(0)

        @pl.when(j == 0)
        def _():
            dws_ref[...] = jnp.zeros_like(dws_ref)
            db_ref[...] = jnp.zeros_like(db_ref)

        masks = _head_masks((TB, A_W))
        zv = z_ref[...].astype(F32)
        u, vn, rstd = _gate_common(zv, masks)
        s = _gate_s(vn, ws_ref, b_ref[...], masks)
        dav = da_ref[...].astype(F32)
        du = dav * s
        ds = dav * u
        dvn_parts = []
        for c in range(TB // CHUNK):
            sl = slice(c * CHUNK, (c + 1) * CHUNK)
            ds_c = ds[sl]
            vn_c = vn[sl].astype(MXU_DTYPE)
            db_ref[...] += ds_c
            ds_b = ds_c.astype(MXU_DTYPE)
            dvn_c = jnp.zeros((CHUNK, A_W), F32)
            for h in range(4):
                mk = masks[h][:CHUNK]
                dws_ref[h] += _nt(jnp.where(mk, ds_c, 0.0).astype(MXU_DTYPE), vn_c)
                dvn_c = dvn_c + jnp.where(mk, _nn(wst_ref[h], ds_b), 0.0)
            dvn_parts.append(dvn_c)
        dvn = jnp.concatenate(dvn_parts, axis=0)
        dv = rstd * (dvn - _head_mean(dvn, masks) - vn * _head_mean(dvn * vn, masks))
        gg = _gelu_grad(zv)
        dz_ref[:, :A_W] = (du * gg[:, :A_W]).astype(dz_ref.dtype)
        dz_ref[:, A_W:] = (dv * gg[:, A_W:]).astype(dz_ref.dtype)

    return pl.pallas_call(
        body, grid=(lay.nb,),
        in_specs=[pl.BlockSpec((TB, 2 * A_W), lambda j: (j, 0)), pl.BlockSpec((TB, A_W), lambda j: (j, MCAT_A)),
                  pl.BlockSpec((4, CHUNK, CHUNK), lambda j: (0, 0, 0)), pl.BlockSpec((4, CHUNK, CHUNK), lambda j: (0, 0, 0)),
                  pl.BlockSpec((CHUNK, A_W), lambda j: (0, 0))],
        out_specs=[pl.BlockSpec((TB, 2 * A_W), lambda j: (j, 0)), pl.BlockSpec((4, CHUNK, CHUNK), lambda j: (0, 0, 0)),
                   pl.BlockSpec((CHUNK, A_W), lambda j: (0, 0))],
        out_shape=[jax.ShapeDtypeStruct((lay.nt, 2 * A_W), MXU_DTYPE), jax.ShapeDtypeStruct((4, CHUNK, CHUNK), F32),
                   jax.ShapeDtypeStruct((CHUNK, A_W), F32)],
        compiler_params=_cp(("arbitrary",)), name=name)(z, da, ws, wst, bias)


def _band_constants():
    bands = np.zeros((2, 4, TB, TB), np.float32)
    inv = np.zeros((2, 4, TB, 1), np.float32)
    for kind, n in ((0, GRID_W), (1, TB)):
        for i, w in enumerate(POOL_WINDOWS):
            for t in range(TB):
                base, tt = (t // n) * n, t % n
                lo = min(max(tt - w // 2, 0), n)
                hi = min(max(tt - w // 2 + w, 0), n)
                bands[kind, i, t, base + lo:base + hi] = 1.0
                inv[kind, i, t, 0] = 1.0 / (hi - lo)
    return bands, inv


def _split3(x):
    a = x.astype(MXU_DTYPE)
    r1 = x - a.astype(F32)
    b = r1.astype(MXU_DTYPE)
    c = (r1 - b.astype(F32)).astype(MXU_DTYPE)
    return a, b, c


def _window_apply(band_ref, inv_ref, x, masks, transpose, mxu_exact=False):
    out = jnp.zeros_like(x)
    for i in range(4):
        xi = x * inv_ref[0, i] if transpose else x
        acc = None
        for part in ((xi.astype(MXU_DTYPE),) if mxu_exact else _split3(xi)):
            r = _tn(band_ref[0, i], part) if transpose else _nn(band_ref[0, i], part)
            acc = r if acc is None else acc + r
        if not transpose:
            acc = acc * inv_ref[0, i]
        out = jnp.where(masks[i], acc, out)
    return out


def _pool_specs(lay):
    kind = lambda j: jnp.where(j < lay.nctx, 1, 0)
    return [pl.BlockSpec((1, 4, TB, TB), lambda j: (kind(j), 0, 0, 0)), pl.BlockSpec((1, 4, TB, 1), lambda j: (kind(j), 0, 0, 0))]


def _pool_fwd(lay, z, bands, inv, pw, scale, name):
    def body(p_ref, band_ref, inv_ref, pw_ref, sc_ref, o_ref):
        masks = _head_masks((TB, C_W))
        p = p_ref[...].astype(F32)
        diff = _window_apply(band_ref, inv_ref, p, masks, False, mxu_exact=True) - p
        o_ref[...] = (_nn(diff.astype(MXU_DTYPE), pw_ref[...]) * sc_ref[...]).astype(o_ref.dtype)

    return pl.pallas_call(
        body, grid=(lay.nb,),
        in_specs=[pl.BlockSpec((TB, C_W), lambda j: (j, 4))] + _pool_specs(lay)
        + [pl.BlockSpec((C_W, C_W), lambda j: (0, 0)), pl.BlockSpec((1, C_W), lambda j: (0, 0))],
        out_specs=pl.BlockSpec((TB, C_W), lambda j: (j, 0)),
        out_shape=jax.ShapeDtypeStruct((lay.nt, C_W), MXU_DTYPE),
        compiler_params=_cp(("parallel",)), name=name)(z, bands, inv, pw, scale)


def _pool_bwd(lay, z, dc, bands, inv, pw, scale, name):
    def body(p_ref, dc_ref, band_ref, inv_ref, pw_ref, sc_ref, dp_ref, dpw_ref, dsc_ref):
        j = pl.program_id(0)

        @pl.when(j == 0)
        def _():
            dpw_ref[...] = jnp.zeros_like(dpw_ref)
            dsc_ref[...] = jnp.zeros_like(dsc_ref)

        masks = _head_masks((TB, C_W))
        p = p_ref[...].astype(F32)
        dcv = dc_ref[...].astype(F32)
        diff = _window_apply(band_ref, inv_ref, p, masks, False, mxu_exact=True) - p
        diff_b = diff.astype(MXU_DTYPE)
        pre = _nn(diff_b, pw_ref[...])
        dsc_ref[...] += jnp.sum(dcv * pre, axis=0, keepdims=True)
        dpre = dcv * sc_ref[...]
        dpre_b = dpre.astype(MXU_DTYPE)
        dpw_ref[...] += _tn(diff_b, dpre_b)
        ddiff = _nt(dpre_b, pw_ref[...])
        dp_ref[...] = (_window_apply(band_ref, inv_ref, ddiff, masks, True) - ddiff).astype(dp_ref.dtype)

    return pl.pallas_call(
        body, grid=(lay.nb,),
        in_specs=[pl.BlockSpec((TB, C_W), lambda j: (j, 4)), pl.BlockSpec((TB, C_W), lambda j: (j, MCAT_C))] + _pool_specs(lay)
        + [pl.BlockSpec((C_W, C_W), lambda j: (0, 0)), pl.BlockSpec((1, C_W), lambda j: (0, 0))],
        out_specs=[pl.BlockSpec((TB, C_W), lambda j: (j, 0)), pl.BlockSpec((C_W, C_W), lambda j: (0, 0)),
                   pl.BlockSpec((1, C_W), lambda j: (0, 0))],
        out_shape=[jax.ShapeDtypeStruct((lay.nt, C_W), MXU_DTYPE), jax.ShapeDtypeStruct((C_W, C_W), F32),
                   jax.ShapeDtypeStruct((1, C_W), F32)],
        compiler_params=_cp(("arbitrary",)), name=name)(z, dc, bands, inv, pw, scale)


def _disc_math(lr, li, ldt, br, bi):
    dt = jnp.exp(ldt)
    e = jnp.exp(lr * dt)
    ar = e * jnp.cos(li * dt)
    ai = e * jnp.sin(li * dt)
    nr, ni = ar - 1.0, ai
    den = lr * lr + li * li
    qr = (nr * lr + ni * li) / den
    qi = (ni * lr - nr * li) / den
    return ar, ai, qr * br - qi * bi, qr * bi + qi * br


def _disc_fwd(lrx, lix, ldtx, brt, bit, name):
    def body(lr_ref, li_ref, ldt_ref, br_ref, bi_ref, ar_ref, ai_ref, obr_ref, obi_ref):
        ar, ai, obr, obi = _disc_math(lr_ref[...], li_ref[...], ldt_ref[...], br_ref[...], bi_ref[...])
        ar_ref[...] = ar
        ai_ref[...] = ai
        obr_ref[...] = obr
        obi_ref[...] = obi

    sh = jax.ShapeDtypeStruct(lrx.shape, F32)
    return pl.pallas_call(body, out_shape=[sh, sh, sh, sh], name=name)(lrx, lix, ldtx, brt, bit)


def _disc_bwd(lrx, lix, ldtx, brt, bit, dar, dai, dbr, dbi, name):
    nrow = lrx.shape[0] // SSM_H

    def body(lr_ref, li_ref, ldt_ref, br_ref, bi_ref, dar_ref, dai_ref, dbr_ref, dbi_ref,
             glr_ref, gli_ref, gdt_ref, gbr_ref, gbi_ref):
        _, vjp = jax.vjp(_disc_math, lr_ref[...], li_ref[...], ldt_ref[...], br_ref[...], bi_ref[...])
        glr, gli, gdt, gbr, gbi = vjp((dar_ref[...], dai_ref[...], dbr_ref[...], dbi_ref[...]))
        glr_ref[...] = jnp.sum(glr.reshape(nrow, SSM_H, SSM_P), axis=1)
        gli_ref[...] = jnp.sum(gli.reshape(nrow, SSM_H, SSM_P), axis=1)
        gdt_ref[...] = jnp.sum(jnp.sum(gdt.reshape(nrow, SSM_H, SSM_P), axis=1), axis=-1, keepdims=True)
        gbr_ref[...] = gbr
        gbi_ref[...] = gbi

    small = jax.ShapeDtypeStruct((nrow, SSM_P), F32)
    big = jax.ShapeDtypeStruct(lrx.shape, F32)
    return pl.pallas_call(body, out_shape=[small, small, jax.ShapeDtypeStruct((nrow, 1), F32), big, big],
                          name=name)(lrx, lix, ldtx, brt, bit, dar, dai, dbr, dbi)


HS = 1024
GQ, QC, QS = 8, 128, 512
LC = QS
SCAN_UNROLL = 2


def _unrolled(step):
    def body(i, carry):
        for j in range(SCAN_UNROLL):
            carry = step(i * SCAN_UNROLL + j, carry)
        return carry
    return body


def _dir_cat(x, d0, qq):
    xq = x[:, QC * qq:QC * qq + QC]
    zero = jnp.zeros_like(xq)
    return jnp.concatenate([jnp.where(d0, xq, zero), jnp.where(d0, zero, xq)], axis=1)


def _dir_pick(x, d0):
    return jnp.where(d0, x[:, :QC], x[:, QC:])


def _d0_rows(n):
    row = lax.broadcasted_iota(jnp.int32, (n, 1), 0)
    return jnp.bitwise_and(row, 4) == 0


def _scan_perm(bl):
    n = 2 * bl * ST
    p = np.zeros((n, n), np.float32)
    for s in range(ST):
        for d in range(2):
            for b in range(bl):
                t = s if d == 0 else ST - 1 - s
                p[s * 2 * bl + d * bl + b, d * bl * ST + b * ST + t] = 1.0
    return p


def _scan_maps(lay):
    spc = TB // ST
    nlc = lay.nlb * spc

    def fwd(k):
        return k // spc, k % spc

    def rev(k):
        cpos = nlc - 1 - jnp.maximum(k - spc, 0)
        return jnp.where(k < spc, 0, 1 + cpos // spc), jnp.where(k < spc, spc - 1 - k, cpos % spc)

    return fwd, rev


def _pack_rows(f_ref, r_ref, p_ref, rc):
    st = jnp.concatenate([f_ref[0].reshape(rc // 2, 256), r_ref[0].reshape(rc // 2, 256)], axis=0).astype(MXU_DTYPE)
    return _nn(p_ref[...], st).astype(MXU_DTYPE)


def _side_split(refs, n_in, n_out, n_scr, side):
    ns = side.n if side is not None else 0
    ins, sin = refs[:n_in], refs[n_in:n_in + ns]
    o0 = n_in + ns
    outs, sout = refs[o0:o0 + n_out], refs[o0 + n_out:o0 + n_out + ns]
    s0 = o0 + n_out + ns
    return ins + outs + refs[s0:s0 + n_scr], (sin, sout, refs[s0 + n_scr:])


def _side_start(side, srefs, first):
    if side is not None:
        @pl.when(first)
        def _():
            side.start(*srefs)


def _side_wait(side, srefs, last):
    if side is not None:
        @pl.when(last)
        def _():
            side.wait(*srefs)


def _ssm_fwd(lay, z, perm, bh, ch, ar8, ai8, name, side=None):
    bl = lay.bl
    rc = ST * 2 * bl
    nch = lay.nr * (TB // ST)
    fwd, rev = _scan_maps(lay)
    z4 = z.reshape(lay.nr, bl, TB, z.shape[1])

    def body(*refs):
        own, srefs = _side_split(refs, 7, 3, 2, side)
        uf_ref, ur_ref, p_ref, bh_ref, ch_ref, ar_ref, ai_ref, yf_ref, yr_ref, hst_ref, hs, hc = own
        f, k = pl.program_id(0), pl.program_id(1)
        _side_start(side, srefs, jnp.logical_and(f == 0, k == 0))

        @pl.when(k == 0)
        def _():
            hc[...] = jnp.zeros_like(hc)

        hst_ref[0] = hc[...]
        d0 = _d0_rows(rc)
        uv = _pack_rows(uf_ref, ur_ref, p_ref, rc)
        for q in range(2):
            cr, ci = 2 * QS * q, 2 * QS * q + QS
            hs[:, cr:cr + 2 * QS] = _nn(_dir_cat(uv, d0, q), bh_ref[q])
            ar = ar_ref[:, QS * q:QS * q + QS]
            ai = ai_ref[:, QS * q:QS * q + QS]

            def step(s, carry, cr=cr, ci=ci, ar=ar, ai=ai):
                hr, hi = carry
                base = pl.multiple_of(s * 8, 8)
                nr = ar * hr - ai * hi + hs[pl.ds(base, 8), cr:cr + LC]
                ni = ar * hi + ai * hr + hs[pl.ds(base, 8), ci:ci + LC]
                hs[pl.ds(base, 8), cr:cr + LC] = nr
                hs[pl.ds(base, 8), ci:ci + LC] = ni
                return nr, ni

            hr, hi = lax.fori_loop(0, ST // SCAN_UNROLL, _unrolled(step), (hc[:, cr:cr + LC], hc[:, ci:ci + LC]))
            hc[:, cr:cr + LC] = hr
            hc[:, ci:ci + LC] = hi
        yi = jnp.concatenate(
            [_dir_pick(_nn(hs[:, 2 * QS * q:2 * QS * (q + 1)].astype(MXU_DTYPE), ch_ref[q]), d0) for q in range(2)], axis=1)
        yd = _tn(p_ref[...], yi.astype(MXU_DTYPE))
        yf_ref[0] = yd[:rc // 2].reshape(bl, ST, 256).astype(yf_ref.dtype)
        yr_ref[0] = yd[rc // 2:].reshape(bl, ST, 256).astype(yr_ref.dtype)
        _side_wait(side, srefs, jnp.logical_and(f == 1, k == nch - 1))

    sd = side if side is not None else _Side([])
    blk = (1, bl, ST, 256)
    ysh = jax.ShapeDtypeStruct((lay.nr, bl, TB, B_W), MXU_DTYPE)
    outs = pl.pallas_call(
        body, grid=(2, nch),
        in_specs=[pl.BlockSpec(blk, lambda f, k: (fwd(k)[0], 0, fwd(k)[1], 2 + f)),
                  pl.BlockSpec(blk, lambda f, k: (rev(k)[0], 0, rev(k)[1], 2 + f)),
                  pl.BlockSpec((rc, rc), lambda f, k: (0, 0)),
                  pl.BlockSpec((2, 2 * QC, 2 * QS), lambda f, k: (f, 0, 0)),
                  pl.BlockSpec((2, 2 * QS, 2 * QC), lambda f, k: (f, 0, 0)),
                  pl.BlockSpec((8, HS), lambda f, k: (0, f)), pl.BlockSpec((8, HS), lambda f, k: (0, f))] + sd.in_specs,
        out_specs=[pl.BlockSpec(blk, lambda f, k: (fwd(k)[0], 0, fwd(k)[1], f)),
                   pl.BlockSpec(blk, lambda f, k: (rev(k)[0], 0, rev(k)[1], f)),
                   pl.BlockSpec((1, 8, 2 * HS), lambda f, k: (k, 0, f))] + sd.out_specs,
        out_shape=[ysh, ysh, jax.ShapeDtypeStruct((nch, 8, 4 * HS), F32)] + sd.out_shape,
        scratch_shapes=[pltpu.VMEM((rc, 2 * HS), F32), pltpu.VMEM((8, 2 * HS), F32)] + (sd.scratch if side is not None else []),
        compiler_params=_cp(("arbitrary", "arbitrary")), name=name)(z4, z4, perm, bh, ch, ar8, ai8, *sd.arrays)
    yf, yr, hst = outs[:3]
    return yf.reshape(lay.nt, B_W), yr.reshape(lay.nt, B_W), hst, list(outs[3:])


def _ssm_bwd(lay, z, dy, perm, hst, bh, ch, ar8, ai8, name, side=None):
    bl = lay.bl
    rc = ST * 2 * bl
    nch = lay.nr * (TB // ST)
    fwd, rev = _scan_maps(lay)
    z4 = z.reshape(lay.nr, bl, TB, z.shape[1])
    dy4 = dy.reshape(lay.nr, bl, TB, B_W)

    def body(*refs):
        own, srefs = _side_split(refs, 10, 6, 5, side)
        (uf_ref, ur_ref, dyf_ref, dyr_ref, p_ref, hst_ref, bh_ref, ch_ref, ar_ref, ai_ref,
         duf_ref, dur_ref, dbh_ref, dch_ref, dar_ref, dai_ref, hs, es, ec, accr, acci) = own
        f, k = pl.program_id(0), pl.program_id(1)
        _side_start(side, srefs, jnp.logical_and(f == 0, k == 0))

        @pl.when(k == 0)
        def _():
            ec[...] = jnp.zeros_like(ec)
            accr[...] = jnp.zeros_like(accr)
            acci[...] = jnp.zeros_like(acci)
            dbh_ref[...] = jnp.zeros_like(dbh_ref)
            dch_ref[...] = jnp.zeros_like(dch_ref)

        d0 = _d0_rows(rc)
        uv = _pack_rows(uf_ref, ur_ref, p_ref, rc)
        dyv = _pack_rows(dyf_ref, dyr_ref, p_ref, rc)

        hs[0:8, :] = hst_ref[0]
        ucat, dycat = [], []
        for q in range(2):
            cr, ci = 2 * QS * q, 2 * QS * q + QS
            ucat.append(_dir_cat(uv, d0, q))
            dycat.append(_dir_cat(dyv, d0, q))
            hs[8:, cr:cr + 2 * QS] = _nn(ucat[q], bh_ref[q])
            ar = ar_ref[:, QS * q:QS * q + QS]
            ai = ai_ref[:, QS * q:QS * q + QS]

            def step(s, carry, cr=cr, ci=ci, ar=ar, ai=ai):
                hr, hi = carry
                base = pl.multiple_of(s * 8 + 8, 8)
                nr = ar * hr - ai * hi + hs[pl.ds(base, 8), cr:cr + LC]
                ni = ar * hi + ai * hr + hs[pl.ds(base, 8), ci:ci + LC]
                hs[pl.ds(base, 8), cr:cr + LC] = nr
                hs[pl.ds(base, 8), ci:ci + LC] = ni
                return nr, ni

            lax.fori_loop(0, ST // SCAN_UNROLL, _unrolled(step), (hs[0:8, cr:cr + LC], hs[0:8, ci:ci + LC]))
            dch_ref[q] += _tn(hs[8:, cr:cr + 2 * QS].astype(MXU_DTYPE), dycat[q])
            es[:, cr:cr + 2 * QS] = _nt(dycat[q], ch_ref[q])

        dui = []
        for q in range(2):
            cr, ci = 2 * QS * q, 2 * QS * q + QS
            ar = ar_ref[:, QS * q:QS * q + QS]
            ai = ai_ref[:, QS * q:QS * q + QS]

            def bstep(i, carry, cr=cr, ci=ci, ar=ar, ai=ai):
                er, ei, sr, si = carry
                base = pl.multiple_of((ST - 1 - i) * 8, 8)
                ner = es[pl.ds(base, 8), cr:cr + LC] + ar * er + ai * ei
                nei = es[pl.ds(base, 8), ci:ci + LC] - ai * er + ar * ei
                es[pl.ds(base, 8), cr:cr + LC] = ner
                es[pl.ds(base, 8), ci:ci + LC] = nei
                hpr = hs[pl.ds(base, 8), cr:cr + LC]
                hpi = hs[pl.ds(base, 8), ci:ci + LC]
                return ner, nei, sr + ner * hpr + nei * hpi, si - ner * hpi + nei * hpr

            lo = QS * q
            er, ei, sr, si = lax.fori_loop(
                0, ST // SCAN_UNROLL, _unrolled(bstep),
                (ec[:, cr:cr + LC], ec[:, ci:ci + LC], accr[:, lo:lo + LC], acci[:, lo:lo + LC]))
            ec[:, cr:cr + LC] = er
            ec[:, ci:ci + LC] = ei
            accr[:, lo:lo + LC] = sr
            acci[:, lo:lo + LC] = si
            eb = es[:, cr:cr + 2 * QS].astype(MXU_DTYPE)
            dui.append(_dir_pick(_nt(eb, bh_ref[q]), d0))
            dbh_ref[q] += _tn(ucat[q], eb)

        dud = _tn(p_ref[...], jnp.concatenate(dui, axis=1).astype(MXU_DTYPE))
        duf_ref[0] = dud[:rc // 2].reshape(bl, ST, 256).astype(duf_ref.dtype)
        dur_ref[0] = dud[rc // 2:].reshape(bl, ST, 256).astype(dur_ref.dtype)

        @pl.when(k == nch - 1)
        def _():
            for d in range(2):
                dar_ref[d:d + 1, :] = jnp.sum(accr[4 * d:4 * d + 4, :], axis=0, keepdims=True)
                dai_ref[d:d + 1, :] = jnp.sum(acci[4 * d:4 * d + 4, :], axis=0, keepdims=True)

        _side_wait(side, srefs, jnp.logical_and(f == 1, k == nch - 1))

    sd = side if side is not None else _Side([])
    last = lambda k: nch - 1 - k
    blk = (1, bl, ST, 256)
    fspec = lambda c0: pl.BlockSpec(blk, lambda f, k: (fwd(last(k))[0], 0, fwd(last(k))[1], c0 + f))
    rspec = lambda c0: pl.BlockSpec(blk, lambda f, k: (rev(last(k))[0], 0, rev(last(k))[1], c0 + f))
    dush = jax.ShapeDtypeStruct((lay.nr, bl, TB, B_W), MXU_DTYPE)
    outs = pl.pallas_call(
        body, grid=(2, nch),
        in_specs=[fspec(2), rspec(2), fspec(0), rspec(0),
                  pl.BlockSpec((rc, rc), lambda f, k: (0, 0)),
                  pl.BlockSpec((1, 8, 2 * HS), lambda f, k: (last(k), 0, f)),
                  pl.BlockSpec((2, 2 * QC, 2 * QS), lambda f, k: (f, 0, 0)),
                  pl.BlockSpec((2, 2 * QS, 2 * QC), lambda f, k: (f, 0, 0)),
                  pl.BlockSpec((8, HS), lambda f, k: (0, f)), pl.BlockSpec((8, HS), lambda f, k: (0, f))] + sd.in_specs,
        out_specs=[fspec(0), rspec(0),
                   pl.BlockSpec((2, 2 * QC, 2 * QS), lambda f, k: (f, 0, 0)),
                   pl.BlockSpec((2, 2 * QS, 2 * QC), lambda f, k: (f, 0, 0)),
                   pl.BlockSpec((2, HS), lambda f, k: (0, f)), pl.BlockSpec((2, HS), lambda f, k: (0, f))] + sd.out_specs,
        out_shape=[dush, dush, jax.ShapeDtypeStruct((4, 2 * QC, 2 * QS), F32),
                   jax.ShapeDtypeStruct((4, 2 * QS, 2 * QC), F32), jax.ShapeDtypeStruct((2, 2 * HS), F32),
                   jax.ShapeDtypeStruct((2, 2 * HS), F32)] + sd.out_shape,
        scratch_shapes=[pltpu.VMEM((rc + 8, 2 * HS), F32), pltpu.VMEM((rc, 2 * HS), F32), pltpu.VMEM((8, 2 * HS), F32),
                        pltpu.VMEM((8, HS), F32), pltpu.VMEM((8, HS), F32)] + (sd.scratch if side is not None else []),
        compiler_params=_cp(("arbitrary", "arbitrary")), name=name)(z4, z4, dy4, dy4, perm, hst, bh, ch, ar8, ai8, *sd.arrays)
    duf, dur, dbh, dch, dar, dai = outs[:6]
    return duf.reshape(lay.nt, B_W), dur.reshape(lay.nt, B_W), dbh, dch, dar, dai, list(outs[6:])


def _glu_fwd(lay, z, yf, yr, dvec, wglu, bglu, name):
    def body(u_ref, yf_ref, yr_ref, d_ref, w_ref, b_ref, o_ref, y_ref):
        y = yf_ref[...].astype(F32) + yr_ref[...].astype(F32) + d_ref[...] * u_ref[...].astype(F32)
        y_ref[...] = y
        g = _gelu(y)
        pre = _nn(g.astype(MXU_DTYPE), w_ref[...]) + b_ref[...]
        o_ref[...] = (g * _sigmoid(pre)).astype(o_ref.dtype)

    tok = pl.BlockSpec((TB, B_W), lambda j: (j, 0))
    vec = pl.BlockSpec((1, B_W), lambda j: (0, 0))
    return pl.pallas_call(
        body, grid=(lay.nb,),
        in_specs=[pl.BlockSpec((TB, B_W), lambda j: (j, 1)), tok, tok, vec, pl.BlockSpec((B_W, B_W), lambda j: (0, 0)), vec],
        out_specs=[tok, tok],
        out_shape=[jax.ShapeDtypeStruct((lay.nt, B_W), MXU_DTYPE), jax.ShapeDtypeStruct((lay.nt, B_W), F32)],
        compiler_params=_cp(("parallel",)), name=name)(z, yf, yr, dvec, wglu, bglu)


def _glu_bwd(lay, z, y, ds, dvec, wglu, bglu, name):
    def body(u_ref, y_ref, ds_ref, d_ref, w_ref, b_ref, dy_ref, dud_ref, dw_ref, db_ref, dd_ref):
        j = pl.program_id(0)

        @pl.when(j == 0)
        def _():
            dw_ref[...] = jnp.zeros_like(dw_ref)
            db_ref[...] = jnp.zeros_like(db_ref)
            dd_ref[...] = jnp.zeros_like(dd_ref)

        yv = y_ref[...]
        g = _gelu(yv)
        gb = g.astype(MXU_DTYPE)
        sg = _sigmoid(_nn(gb, w_ref[...]) + b_ref[...])
        dsv = ds_ref[...].astype(F32)
        dpre = dsv * g * sg * (1.0 - sg)
        dpre_b = dpre.astype(MXU_DTYPE)
        dg = dsv * sg + _nt(dpre_b, w_ref[...])
        dw_ref[...] += _tn(gb, dpre_b)
        db_ref[...] += jnp.sum(dpre, axis=0, keepdims=True)
        dy = dg * _gelu_grad(yv)
        dy_ref[...] = dy.astype(dy_ref.dtype)
        dd_ref[...] += jnp.sum(dy * u_ref[...].astype(F32), axis=0, keepdims=True)
        dud_ref[...] = (dy * d_ref[...]).astype(dud_ref.dtype)

    tok = pl.BlockSpec((TB, B_W), lambda j: (j, 0))
    vec = pl.BlockSpec((1, B_W), lambda j: (0, 0))
    mat = pl.BlockSpec((B_W, B_W), lambda j: (0, 0))
    vsh = jax.ShapeDtypeStruct((1, B_W), F32)
    return pl.pallas_call(
        body, grid=(lay.nb,),
        in_specs=[pl.BlockSpec((TB, B_W), lambda j: (j, 1)), tok, tok, vec, mat, vec],
        out_specs=[tok, tok, mat, vec, vec],
        out_shape=[jax.ShapeDtypeStruct((lay.nt, B_W), MXU_DTYPE), jax.ShapeDtypeStruct((lay.nt, B_W), F32),
                   jax.ShapeDtypeStruct((B_W, B_W), F32), vsh, vsh],
        compiler_params=_cp(("arbitrary",)), name=name)(z, y, ds, dvec, wglu, bglu)


def _dz_assemble(lay, dz_a, duf, dur, dud, dz_p, name):
    def body(a_ref, f_ref, r_ref, d_ref, p_ref, o_ref):
        o_ref[:, :2 * A_W] = a_ref[...].astype(o_ref.dtype)
        o_ref[:, 2 * A_W:2 * A_W + B_W] = (f_ref[...].astype(F32) + r_ref[...].astype(F32) + d_ref[...]).astype(o_ref.dtype)
        o_ref[:, 2 * A_W + B_W:] = p_ref[...].astype(o_ref.dtype)

    spec = lambda w: pl.BlockSpec((TB, w), lambda j: (j, 0))
    return pl.pallas_call(
        body, grid=(lay.nb,), in_specs=[spec(2 * A_W), spec(B_W), spec(B_W), spec(B_W), spec(C_W)],
        out_specs=spec(D_IN), out_shape=jax.ShapeDtypeStruct((lay.nt, D_IN), MXU_DTYPE),
        compiler_params=_cp(("parallel",)), name=name)(dz_a, duf, dur, dud, dz_p)


def _expand_rows(a):
    return jnp.broadcast_to(a[:, :, None, :], (2, SSM_G, SSM_H, SSM_P)).reshape(-1, SSM_P)


def _ssm_params(lam_re, lam_im, log_dt, b_re, b_im, c_re, c_im, name):
    lrx, lix = _expand_rows(lam_re), _expand_rows(lam_im)
    ldtx = _expand_rows(jnp.broadcast_to(log_dt[:, :, None], (2, SSM_G, SSM_P)))
    brt = jnp.transpose(b_re, (0, 1, 3, 2)).reshape(-1, SSM_P)
    bit = jnp.transpose(b_im, (0, 1, 3, 2)).reshape(-1, SSM_P)
    arx, aix, bbr, bbi = _disc_fwd(lrx, lix, ldtx, brt, bit, name)
    ar = arx.reshape(2, SSM_G, SSM_H, SSM_P)[:, :, 0].reshape(2, SSM_G * SSM_P)
    ai = aix.reshape(2, SSM_G, SSM_H, SSM_P)[:, :, 0].reshape(2, SSM_G * SSM_P)
    eye = jnp.eye(GQ, dtype=F32)

    def bmat(bt):
        t = bt.reshape(2, 4, GQ, SSM_H, SSM_P)
        return jnp.einsum('dqghp,gk->qdghkp', t, eye).reshape(4, 2 * QC, QS)

    bh = jnp.concatenate([bmat(bbr), bmat(bbi)], axis=-1).astype(MXU_DTYPE)

    def cmat(c):
        t = c.reshape(2, 4, GQ, SSM_H, SSM_P)
        return jnp.einsum('dqghp,gk->qgpdkh', t, eye).reshape(4, QS, 2 * QC)

    ch = jnp.concatenate([cmat(c_re), -cmat(c_im)], axis=1).astype(MXU_DTYPE)

    def rows8(a):
        return jnp.repeat(a, 4, axis=0)

    return dict(lrx=lrx, lix=lix, ldtx=ldtx, brt=brt, bit=bit, bh=bh, ch=ch, ar8=rows8(ar), ai8=rows8(ai))


def _ssm_param_grads(sp, dbh, dch, dar, dai, name):
    def bdiag(m):
        t = m.reshape(4, 2, GQ, SSM_H, GQ, SSM_P)
        return jnp.einsum('qdghgp->dqghp', t).reshape(-1, SSM_P)

    dbr, dbi = bdiag(dbh[..., :QS]), bdiag(dbh[..., QS:])

    def cdiag(m):
        t = m.reshape(4, GQ, SSM_P, 2, GQ, SSM_H)
        return jnp.einsum('qgpdgh->dqghp', t).reshape(2, SSM_G, SSM_H, SSM_P)

    dc_re, dc_im = cdiag(dch[:, :QS]), -cdiag(dch[:, QS:])

    def hrow(a):
        t = a.reshape(2, SSM_G, 1, SSM_P)
        return jnp.concatenate([t, jnp.zeros((2, SSM_G, SSM_H - 1, SSM_P), F32)], axis=2).reshape(-1, SSM_P)

    glr, gli, gdt, gbr, gbi = _disc_bwd(sp["lrx"], sp["lix"], sp["ldtx"], sp["brt"], sp["bit"],
                                        hrow(dar), hrow(dai), dbr, dbi, name)
    to_b = lambda g: jnp.transpose(g.reshape(2, SSM_G, SSM_H, SSM_P), (0, 1, 3, 2))
    return dict(ssm_lam_re=glr.reshape(2, SSM_G, SSM_P), ssm_lam_im=gli.reshape(2, SSM_G, SSM_P),
                ssm_log_dt=gdt.reshape(2, SSM_G), ssm_b_re=to_b(gbr), ssm_b_im=to_b(gbi),
                ssm_c_re=dc_re, ssm_c_im=dc_im)


def _layer_consts(p):
    c = {}
    c["ws"] = p["sgu_w"].astype(MXU_DTYPE)
    c["wst"] = jnp.transpose(p["sgu_w"], (0, 2, 1)).astype(MXU_DTYPE)
    c["gbias"] = jnp.repeat(p["sgu_b"].T, 64, axis=1)
    pw = jnp.zeros((C_W, C_W), F32)
    for i in range(4):
        pw = pw.at[64 * i:64 * i + 64, 64 * i:64 * i + 64].set(p["pool_w"][i])
    c["pw"] = pw.astype(MXU_DTYPE)
    c["pscale"] = p["pool_scale"].reshape(1, C_W)
    c["dvec"] = p["ssm_d"].reshape(1, B_W)
    c["bglu"] = p["glu_b"].reshape(1, B_W)
    return c


def _layer_fwd(lay, i, x, modarr, p, w, cst, sp, bands, inv, perm, sides=None):
    n = f"l{i}_"
    sides = sides or {}
    ssm_side, ssm_fill = sides.get("ssm", (None, None))
    ffn_side, ffn_fill = sides.get("ffn", (None, None))
    res = {"x0": x}
    h = _normmod_fwd(lay, x, p["norm_mix_pre"].reshape(1, D), modarr, 0, 1, n + "nm1")
    z = _mm([(h, w["win_t"])], True, MXU_DTYPE, n + "win")
    a = _gate_fwd(lay, z, cst["ws"], cst["gbias"], n + "gate")
    yf, yr, hst, extra = _ssm_fwd(lay, z, perm, sp["bh"], sp["ch"], sp["ar8"], sp["ai8"], n + "ssm", ssm_side)
    if ssm_fill is not None:
        ssm_fill(extra)
    s, y = _glu_fwd(lay, z, yf, yr, cst["dvec"], w["wglu"], cst["bglu"], n + "glu")
    c = _pool_fwd(lay, z, bands, inv, cst["pw"], cst["pscale"], n + "pool")
    mcat = jnp.concatenate([s, a, c], axis=1)
    m = _mm([(mcat, w["wout"])], False, MXU_DTYPE, n + "wout")
    x1 = _resnorm_fwd(lay, x, m, p["norm_mix_post"].reshape(1, D), modarr, 2, n + "rn1")
    h2 = _normmod_fwd(lay, x1, p["norm_ffn_pre"].reshape(1, D), modarr, 3, 4, n + "nm2")
    g, u, act, extra = _ffn_up(h2, w["wg_t"], w["wu_t"], n + "ffn_up", ffn_side)
    if ffn_fill is not None:
        ffn_fill(extra)
    f = _mm([(act, w["wd"])], False, MXU_DTYPE, n + "ffn_down")
    x2 = _resnorm_fwd(lay, x1, f, p["norm_ffn_post"].reshape(1, D), modarr, 5, n + "rn2")
    res.update(h=h, z=z, hst=hst, y=y, mcat=mcat, m=m, x1=x1, h2=h2, g=g, u=u, act=act, f=f)
    return x2, res


def _layer_bwd(lay, i, dx2, modarr, p, w, cst, sp, bands, inv, perm, res, side_fn=None):
    n = f"l{i}b_"
    big, small = {}, {}
    df, dg2, gpost2 = _resnorm_bwd(lay, dx2, res["f"], p["norm_ffn_post"].reshape(1, D), modarr, 5, n + "rn2")
    big["wd"] = _mm_tn(res["act"], df, MXU_DTYPE, n + "dwd")
    dg, du = _ffn_down_bwd(df, w["wd"], res["g"], res["u"], n + "ffn_down")
    dh2 = _mm([(dg, w["wg_t"]), (du, w["wu_t"])], False, MXU_DTYPE, n + "dh2")
    big["wg_t"] = _mm_tn(dg, res["h2"], MXU_DTYPE, n + "dwg")
    big["wu_t"] = _mm_tn(du, res["h2"], MXU_DTYPE, n + "dwu")
    dx1, dsh2, dsc2, gpre2 = _normmod_bwd(lay, res["x1"], dh2, dx2, p["norm_ffn_pre"].reshape(1, D), modarr, 4, n + "nm2")
    dm, dg1, gpost1 = _resnorm_bwd(lay, dx1, res["m"], p["norm_mix_post"].reshape(1, D), modarr, 2, n + "rn1")
    big["wout"] = _unperm_wout(_mm_tn(res["mcat"], dm, MXU_DTYPE, n + "dwout"))
    dmcat = _mm([(dm, w["wout"])], True, MXU_DTYPE, n + "dmcat")
    z = res["z"]
    dz_a, dws, dgb = _gate_bwd(lay, z, dmcat, cst["ws"], cst["wst"], cst["gbias"], n + "gate")
    dy, dud, dwglu, dbglu, ddvec = _glu_bwd(lay, z, res["y"], dmcat, cst["dvec"], w["wglu"], cst["bglu"], n + "glu")
    big["wglu"] = dwglu.astype(MXU_DTYPE)
    side = side_fn(big) if side_fn is not None else None
    duf, dur, dbh, dch, dar, dai, early = _ssm_bwd(lay, z, dy, perm, res["hst"], sp["bh"], sp["ch"], sp["ar8"],
                                                   sp["ai8"], n + "ssm", side)
    dz_p, dpw, dpsc = _pool_bwd(lay, z, dmcat, bands, inv, cst["pw"], cst["pscale"], n + "pool")
    dz = _dz_assemble(lay, dz_a, duf, dur, dud, dz_p, n + "dz")
    big["win_t"] = _mm_tn(dz, res["h"], MXU_DTYPE, n + "dwin")
    dh = _mm([(dz, w["win_t"])], False, MXU_DTYPE, n + "dh")
    dx, dsh1, dsc1, gpre1 = _normmod_bwd(lay, res["x0"], dh, dx1, p["norm_mix_pre"].reshape(1, D), modarr, 1, n + "nm1",
                                         latent_only=(i == 0))

    small.update(norm_mix_pre=gpre1[0], norm_mix_post=gpost1[0], norm_ffn_pre=gpre2[0], norm_ffn_post=gpost2[0])
    small["sgu_w"] = dws
    small["sgu_b"] = jnp.sum(dgb.reshape(CHUNK, 4, 64), axis=-1).T
    small.update(_ssm_param_grads(sp, dbh, dch, dar, dai, n + "disc"))
    small["ssm_d"] = ddvec.reshape(SSM_G, SSM_H)
    small["glu_b"] = dbglu[0]
    small["pool_w"] = jnp.stack([dpw[64 * k:64 * k + 64, 64 * k:64 * k + 64] for k in range(4)])
    small["pool_scale"] = dpsc[0]
    dmod = jnp.concatenate([dsh1, dsc1, dg1, dsh2, dsc2, dg2], axis=1)[:lay.bl + 1]
    dmod = jnp.concatenate([dmod, jnp.zeros((8 - lay.bl - 1, 6, D), F32)], axis=0)
    return dx, big, small, dmod, early


def _perm_wout(w):
    return w.reshape(4, D // 4, D)[np.array(WOUT_PERM)].reshape(D, D)


def _unperm_wout(g):
    return g.reshape(4, D // 4, D)[np.array(WOUT_INV)].reshape(D, D)


SMALL_NAMES = ["norm_mix_pre", "norm_mix_post", "norm_ffn_pre", "norm_ffn_post", "sgu_w", "sgu_b", "ssm_lam_re",
               "ssm_lam_im", "ssm_log_dt", "ssm_b_re", "ssm_b_im", "ssm_c_re", "ssm_c_im", "ssm_d", "glu_b", "pool_w",
               "pool_scale"]
BIG_NAMES = ["win_t", "wout", "wglu", "wg_t", "wu_t", "wd"]


def _sincos_2d(rows, cols, dim):
    quarter = dim // 4
    omega = 1.0 / (10000.0 ** (jnp.arange(quarter, dtype=F32) / quarter))
    r = jnp.arange(rows, dtype=F32)[:, None] * omega
    cc = jnp.arange(cols, dtype=F32)[:, None] * omega
    er = jnp.concatenate([jnp.sin(r), jnp.cos(r)], axis=-1)
    ec = jnp.concatenate([jnp.sin(cc), jnp.cos(cc)], axis=-1)
    pe = jnp.concatenate([jnp.broadcast_to(er[:, None, :], (rows, cols, dim // 2)),
                          jnp.broadcast_to(ec[None, :, :], (rows, cols, dim // 2))], axis=-1)
    return pe.reshape(rows * cols, dim)


def _core(x, ctx, target, mods_local, params, weights, w_side=None, w_fill=None, g_side_fn=None):
    bl, lat, _ = x.shape
    assert bl == 4 and lat % TB == 0, "the scan fills 8 sublanes with 2 directions x 4 sequences"
    lay = _Layout(bl, lat)
    pe = _sincos_2d(lat // GRID_W, GRID_W, D)
    xt = _embed(lay, x.reshape(bl * lat, D), ctx.reshape(bl * CTX, D), pe)
    bands_np, inv_np = _band_constants()
    bands, inv = jnp.asarray(bands_np, MXU_DTYPE), jnp.asarray(inv_np, F32)
    perm = jnp.asarray(_scan_perm(bl), MXU_DTYPE)
    rows = lay.modrows_static()
    modarrs, csts, sps, ress, wls = [], [], [], [], []
    for i in range(2):
        modarrs.append(mods_local[i][rows].reshape(lay.nb * 6, 1, D))
        csts.append(_layer_consts(params[i]))
        p = params[i]
        sps.append(_ssm_params(p["ssm_lam_re"], p["ssm_lam_im"], p["ssm_log_dt"], p["ssm_b_re"], p["ssm_b_im"],
                               p["ssm_c_re"], p["ssm_c_im"], f"l{i}_disc"))
        wls.append(dict(weights[i]))

    w_side = w_side or {}

    def fill_of(key):
        def fill(extra):
            if key in w_side:
                w_fill[key](wls, extra)
            if key == "ssm":
                for w in wls:
                    w["wout"] = _perm_wout(w["wout"])
        return fill

    sides0 = {key: (w_side.get(key), fill_of(key)) for key in ("ssm", "ffn")}
    for i in range(2):
        xt, res = _layer_fwd(lay, i, xt, modarrs[i], params[i], wls[i], csts[i], sps[i], bands, inv, perm,
                             sides0 if i == 0 else None)
        ress.append(res)
    dx, lossv = _loss_bwd(lay, xt, target.reshape(bl * lat, D))
    bigs, smalls, dmods, early = [None, None], [None, None], [None, None], []
    for i in (1, 0):
        side_fn = (lambda big0: g_side_fn(bigs[1], big0)) if (i == 0 and g_side_fn is not None) else None
        dx, bigs[i], smalls[i], dmods[i], ex = _layer_bwd(lay, i, dx, modarrs[i], params[i], wls[i], csts[i], sps[i],
                                                           bands, inv, perm, ress[i], side_fn)
        early += ex
    return lossv[0, 0], dx.reshape(bl, lat, D), bigs, smalls, dmods, early


def _my_index():
    return 4 * lax.axis_index("x") + 2 * lax.axis_index("y") + lax.axis_index("c")


def _peer(k):
    x, y, c = lax.axis_index("x"), lax.axis_index("y"), lax.axis_index("c")
    kx, ky, kc = (k >> 2) & 1, (k >> 1) & 1, k & 1
    px = 1 - x if kx else x
    py = 1 - y if ky else y
    pc = 1 - c if kc else c
    return (px, py, pc), 4 * px + 2 * py + pc


class _Side:
    def __init__(self, items):
        self.items = items
        self.n = len(items)
        self.ncopies = sum(len(it[2]) for it in items)
        self.arrays = [it[0] for it in items]
        anyspec = pl.BlockSpec(memory_space=pl.ANY)
        self.in_specs = [anyspec] * self.n
        self.out_specs = [anyspec] * self.n
        self.out_shape = [jax.ShapeDtypeStruct((slots,) + tuple(a.shape) if mode == "gather" else tuple(a.shape), a.dtype)
                          for a, mode, ks, slots in items]
        self.scratch = [pltpu.SemaphoreType.DMA((self.ncopies,)), pltpu.SemaphoreType.DMA((self.ncopies,)),
                        pltpu.SemaphoreType.DMA((self.n,))]

    def _copies(self, ins, outs, sems):
        send_sems, recv_sems, local_sems = sems
        slot_of = lambda idx, slots: idx if slots == 8 else (idx // 2 if slots == 4 else idx % 2)
        me = _my_index()
        local, sends, recvs = [], [], []
        q = 0
        for t, (arr, mode, ks, slots) in enumerate(self.items):
            src_own = ins[t] if mode == "gather" else ins[t].at[me]
            local.append(pltpu.make_async_copy(src_own, outs[t].at[slot_of(me, slots)], local_sems.at[t]))
            for k in ks:
                peer, pidx = _peer(k)
                src = ins[t] if mode == "gather" else ins[t].at[pidx]
                sends.append(pltpu.make_async_remote_copy(
                    src_ref=src, dst_ref=outs[t].at[slot_of(me, slots)], send_sem=send_sems.at[q], recv_sem=recv_sems.at[q],
                    device_id=peer, device_id_type=pl.DeviceIdType.MESH))
                recvs.append(pltpu.make_async_remote_copy(
                    src_ref=src, dst_ref=outs[t].at[slot_of(pidx, slots)], send_sem=send_sems.at[q], recv_sem=recv_sems.at[q],
                    device_id=peer, device_id_type=pl.DeviceIdType.MESH))
                q += 1
        return local, sends, recvs

    def start(self, ins, outs, sems):
        local, sends, _ = self._copies(ins, outs, sems)
        for cp in sends + local:
            cp.start()

    def wait(self, ins, outs, sems):
        local, sends, recvs = self._copies(ins, outs, sems)
        for cp in recvs:
            cp.wait_recv()
        for cp in sends:
            cp.wait_send()
        for cp in local:
            cp.wait()


def _comm(items, name):
    side = _Side(items)
    n = side.n

    def body(*refs):
        ins, outs, sems = refs[:n], refs[n:2 * n], refs[2 * n:]
        side.start(ins, outs, sems)
        side.wait(ins, outs, sems)

    return pl.pallas_call(
        body, in_specs=side.in_specs, out_specs=side.out_specs, out_shape=side.out_shape, scratch_shapes=side.scratch,
        compiler_params=pltpu.CompilerParams(has_side_effects=True), name=name)(*side.arrays)


def _spread(items, name):
    n = len(items)
    ncopies = sum(len(it[1]) for it in items)

    def slot_of(idx, slots):
        return idx if slots == 8 else (idx // 2 if slots == 4 else idx % 2)

    def body(*refs):
        ins, outs, bufs = refs[:n], refs[n:2 * n], refs[2 * n:3 * n]
        load_sems, store_sems, send_sems, recv_sems = refs[3 * n:]
        me = _my_index()
        loads = [pltpu.make_async_copy(ins[t], bufs[t], load_sems.at[t]) for t in range(n)]
        for cp in loads:
            cp.start()
        stores, sends, recvs = [], [], []
        q = 0
        for t, (arr, ks, slots) in enumerate(items):
            loads[t].wait()
            own = outs[t].at[slot_of(me, slots)]
            stores.append(pltpu.make_async_copy(bufs[t], own, store_sems.at[t]))
            stores[-1].start()
            for k in ks:
                peer, pidx = _peer(k)
                sends.append(pltpu.make_async_remote_copy(
                    src_ref=bufs[t], dst_ref=own, send_sem=send_sems.at[q], recv_sem=recv_sems.at[q],
                    device_id=peer, device_id_type=pl.DeviceIdType.MESH))
                recvs.append(pltpu.make_async_remote_copy(
                    src_ref=bufs[t], dst_ref=outs[t].at[slot_of(pidx, slots)], send_sem=send_sems.at[q],
                    recv_sem=recv_sems.at[q], device_id=peer, device_id_type=pl.DeviceIdType.MESH))
                sends[-1].start()
                q += 1
        for cp in recvs:
            cp.wait_recv()
        for cp in sends:
            cp.wait_send()
        for cp in stores:
            cp.wait()

    anyspec = pl.BlockSpec(memory_space=pl.ANY)
    return pl.pallas_call(
        body, in_specs=[anyspec] * n, out_specs=[anyspec] * n,
        out_shape=[jax.ShapeDtypeStruct((slots,) + tuple(arr.shape), arr.dtype) for arr, ks, slots in items],
        scratch_shapes=[pltpu.VMEM(tuple(arr.shape), arr.dtype) for arr, ks, slots in items]
        + [pltpu.SemaphoreType.DMA((n,)), pltpu.SemaphoreType.DMA((n,)), pltpu.SemaphoreType.DMA((ncopies,)),
           pltpu.SemaphoreType.DMA((ncopies,))],
        compiler_params=pltpu.CompilerParams(has_side_effects=True, vmem_limit_bytes=VMEM_LIMIT),
        name=name)(*[it[0] for it in items])


ALL7 = (1, 2, 3, 4, 5, 6, 7)
CHIPS3 = (2, 4, 6)


def _sum8(parts, name):
    def one(a, nm):
        _, r, c = a.shape
        tr = r if r <= 512 else _pick_rows(r)

        def body(a_ref, o_ref):
            acc = a_ref[0].astype(F32)
            for q in range(1, a_ref.shape[0]):
                acc = acc + a_ref[q].astype(F32)
            o_ref[...] = acc

        return pl.pallas_call(
            body, grid=(r // tr,), in_specs=[pl.BlockSpec((a.shape[0], tr, c), lambda i: (0, i, 0))],
            out_specs=pl.BlockSpec((tr, c), lambda i: (i, 0)), out_shape=jax.ShapeDtypeStruct((r, c), F32),
            compiler_params=_cp(("parallel",)), name=nm)(a)

    return [one(a, f"{name}{i}") for i, a in enumerate(parts)]


def _pick_rows(r, cap=512):
    for t in (512, 352, 256, 176, 128, 64, 32, 16, 8):
        if r % t == 0 and t <= cap:
            return t
    return r


def _adam(w, g, m, v, name):
    shape = w.shape
    nel = int(np.prod(shape))
    if len(shape) >= 2 and shape[-1] >= 128:
        lanes = shape[-1]
    else:
        lanes = 512 if nel % 512 == 0 else 128
    r = nel // lanes
    tr = r if r * lanes <= 384 * 1024 else _pick_rows(r, 384 * 1024 // lanes)
    c1 = 1.0 / (1.0 - ADAM_B1 ** ADAM_STEP)
    c2 = 1.0 / (1.0 - ADAM_B2 ** ADAM_STEP)

    def body(w_ref, g_ref, m_ref, v_ref, d_ref, nm_ref, nv_ref):
        gv = g_ref[...]
        nm = ADAM_B1 * m_ref[...] + (1.0 - ADAM_B1) * gv
        nv = ADAM_B2 * v_ref[...] + (1.0 - ADAM_B2) * (gv * gv)
        d_ref[...] = -ADAM_LR * ((nm * c1) / (jnp.sqrt(nv * c2) + ADAM_EPS) + ADAM_WD * w_ref[...])
        nm_ref[...] = nm
        nv_ref[...] = nv

    spec = pl.BlockSpec((tr, lanes), lambda i: (i, 0))
    sh = jax.ShapeDtypeStruct((r, lanes), F32)
    outs = pl.pallas_call(
        body, grid=(r // tr,), in_specs=[spec] * 4, out_specs=[spec] * 3, out_shape=[sh] * 3,
        compiler_params=_cp(("parallel",)), name=name)(*[a.reshape(r, lanes) for a in (w, g, m, v)])
    return [o.reshape(shape) for o in outs]


def _silu(x):
    return x * _sigmoid(x)


def _mod_fwd(c_rows, w_mod, b_cols, name):
    def body(c_ref, w_ref, b_ref, o_ref):
        s = _silu(c_ref[...])
        for l in range(2):
            o_ref[l] = jnp.dot(s, w_ref[l], preferred_element_type=F32, precision=lax.Precision.HIGHEST) + b_ref[l]

    nc = w_mod.shape[2]
    return pl.pallas_call(body, out_shape=jax.ShapeDtypeStruct((2, c_rows.shape[0], nc), F32),
                          compiler_params=_cp(None), name=name)(c_rows, w_mod, b_cols)


def _mod_bwd(c_rows, w_mod, dlat, dctx8, name):
    nrow = c_rows.shape[0]
    nb = nrow - 8

    def body(c_ref, w_ref, dl_ref, dc_ref, gw_ref, gc_ref):
        s = _silu(c_ref[...])
        ctx_row = lax.broadcasted_iota(jnp.int32, (nrow, 1), 0) == nb
        gc = jnp.zeros((1, D), F32)
        for l in range(2):
            dctx = dc_ref[0, l]
            for q in range(1, 8):
                dctx = dctx + dc_ref[q, l]
            dm = dl_ref[l] + jnp.where(ctx_row, dctx, 0.0)
            gw_ref[l] = lax.dot_general(s, dm, (((0,), (0,)), ((), ())), preferred_element_type=F32,
                                        precision=lax.Precision.HIGHEST)
            gc = gc + lax.dot_general(dctx, w_ref[l], (((1,), (1,)), ((), ())), preferred_element_type=F32,
                                      precision=lax.Precision.HIGHEST)
        gc_ref[...] = gc

    nc = w_mod.shape[2]
    return pl.pallas_call(body, out_shape=[jax.ShapeDtypeStruct((2, D, nc), F32), jax.ShapeDtypeStruct((1, D), F32)],
                          compiler_params=_cp(None), name=name)(c_rows, w_mod, dlat, dctx8)


def _bmod_cctx(dmod_all, gc4, c_ctx, name):
    def body(dm_ref, gc_ref, cc_ref, gb_ref, gcc_ref):
        for l in range(2):
            acc = jnp.sum(dm_ref[0, l], axis=0, keepdims=True)
            for q in range(1, 8):
                acc = acc + jnp.sum(dm_ref[q, l], axis=0, keepdims=True)
            gb_ref[l:l + 1, :] = acc
        g = gc_ref[0] + gc_ref[1] + gc_ref[2] + gc_ref[3]
        cv = cc_ref[...]
        sg = _sigmoid(cv)
        gcc_ref[...] = g * (sg * (1.0 + cv * (1.0 - sg)))

    return pl.pallas_call(body, out_shape=[jax.ShapeDtypeStruct((2, 6 * D), F32), jax.ShapeDtypeStruct((1, D), F32)],
                          compiler_params=_cp(None), name=name)(dmod_all, gc4, c_ctx)


def kernel(x, c, ctx, c_ctx, w_mod, b_mod, norm_mix_pre, norm_mix_post, norm_ffn_pre, norm_ffn_post, w_in, w_out, sgu_w, sgu_b, ssm_lam_re, ssm_lam_im, ssm_log_dt, ssm_b_re, ssm_b_im, ssm_c_re, ssm_c_im, ssm_d, glu_w, glu_b, pool_w, pool_scale, ffn_w_gate, ffn_w_up, ffn_w_down, loss_target, m_c_ctx, m_w_mod, m_b_mod, m_norm_mix_pre, m_norm_mix_post, m_norm_ffn_pre, m_norm_ffn_post, m_w_in, m_w_out, m_sgu_w, m_sgu_b, m_ssm_lam_re, m_ssm_lam_im, m_ssm_log_dt, m_ssm_b_re, m_ssm_b_im, m_ssm_c_re, m_ssm_c_im, m_ssm_d, m_glu_w, m_glu_b, m_pool_w, m_pool_scale, m_ffn_w_gate, m_ffn_w_up, m_ffn_w_down, v_c_ctx, v_w_mod, v_b_mod, v_norm_mix_pre, v_norm_mix_post, v_norm_ffn_pre, v_norm_ffn_post, v_w_in, v_w_out, v_sgu_w, v_sgu_b, v_ssm_lam_re, v_ssm_lam_im, v_ssm_log_dt, v_ssm_b_re, v_ssm_b_im, v_ssm_c_re, v_ssm_c_im, v_ssm_d, v_glu_w, v_glu_b, v_pool_w, v_pool_scale, v_ffn_w_gate, v_ffn_w_up, v_ffn_w_down):
    wts = dict(c_ctx=c_ctx, w_mod=w_mod, b_mod=b_mod, norm_mix_pre=norm_mix_pre, norm_mix_post=norm_mix_post,
               norm_ffn_pre=norm_ffn_pre, norm_ffn_post=norm_ffn_post, w_in=w_in, w_out=w_out, sgu_w=sgu_w, sgu_b=sgu_b,
               ssm_lam_re=ssm_lam_re, ssm_lam_im=ssm_lam_im, ssm_log_dt=ssm_log_dt, ssm_b_re=ssm_b_re, ssm_b_im=ssm_b_im,
               ssm_c_re=ssm_c_re, ssm_c_im=ssm_c_im, ssm_d=ssm_d, glu_w=glu_w, glu_b=glu_b, pool_w=pool_w,
               pool_scale=pool_scale, ffn_w_gate=ffn_w_gate, ffn_w_up=ffn_w_up, ffn_w_down=ffn_w_down)
    ms = dict(c_ctx=m_c_ctx, w_mod=m_w_mod, b_mod=m_b_mod, norm_mix_pre=m_norm_mix_pre, norm_mix_post=m_norm_mix_post,
              norm_ffn_pre=m_norm_ffn_pre, norm_ffn_post=m_norm_ffn_post, w_in=m_w_in, w_out=m_w_out, sgu_w=m_sgu_w,
              sgu_b=m_sgu_b, ssm_lam_re=m_ssm_lam_re, ssm_lam_im=m_ssm_lam_im, ssm_log_dt=m_ssm_log_dt,
              ssm_b_re=m_ssm_b_re, ssm_b_im=m_ssm_b_im, ssm_c_re=m_ssm_c_re, ssm_c_im=m_ssm_c_im, ssm_d=m_ssm_d,
              glu_w=m_glu_w, glu_b=m_glu_b, pool_w=m_pool_w, pool_scale=m_pool_scale, ffn_w_gate=m_ffn_w_gate,
              ffn_w_up=m_ffn_w_up, ffn_w_down=m_ffn_w_down)
    vs = dict(c_ctx=v_c_ctx, w_mod=v_w_mod, b_mod=v_b_mod, norm_mix_pre=v_norm_mix_pre, norm_mix_post=v_norm_mix_post,
              norm_ffn_pre=v_norm_ffn_pre, norm_ffn_post=v_norm_ffn_post, w_in=v_w_in, w_out=v_w_out, sgu_w=v_sgu_w,
              sgu_b=v_sgu_b, ssm_lam_re=v_ssm_lam_re, ssm_lam_im=v_ssm_lam_im, ssm_log_dt=v_ssm_log_dt,
              ssm_b_re=v_ssm_b_re, ssm_b_im=v_ssm_b_im, ssm_c_re=v_ssm_c_re, ssm_c_im=v_ssm_c_im, ssm_d=v_ssm_d,
              glu_w=v_glu_w, glu_b=v_glu_b, pool_w=v_pool_w, pool_scale=v_pool_scale, ffn_w_gate=v_ffn_w_gate,
              ffn_w_up=v_ffn_w_up, ffn_w_down=v_ffn_w_down)
    order = list(wts.keys())
    bl = x.shape[0]
    nseq = bl * N_DEV
    me = _my_index()
    chip = me // 2
    ncol = w_mod.shape[2]

    (c_all,) = _spread([(c, ALL7, 8)], "ag_c")
    nrow = nseq + 8
    c_rows = jnp.concatenate([c_all.reshape(nseq, D), c_ctx[None], jnp.zeros((7, D), F32)], axis=0)
    b_cols = lax.dynamic_slice_in_dim(b_mod, chip * ncol, ncol, axis=1)[:, None, :]
    mod_cols = _mod_fwd(c_rows, w_mod, b_cols, "mod_fwd")
    (mod4,) = _spread([(mod_cols, CHIPS3, 4)], "ag_mod")
    mods = jnp.transpose(mod4, (1, 2, 0, 3)).reshape(2, nrow, 6 * D)
    mods_local = jnp.concatenate([lax.dynamic_slice_in_dim(mods, me * bl, bl, axis=1), mods[:, nseq:nseq + 1],
                                  jnp.zeros((2, 8 - bl - 1, 6 * D), F32)], axis=1)

    shards = {}
    for i in range(2):
        for nme, s in zip(BIG_NAMES, [w_in[i].T, w_out[i], glu_w[i], ffn_w_gate[i].T, ffn_w_up[i].T, ffn_w_down[i]]):
            shards[(i, nme)] = s.astype(MXU_DTYPE)
    (win0,) = _comm([(shards[(0, "win_t")], "gather", CHIPS3, 4)], "ag_win0")
    weights = [{"win_t": win0.reshape(-1, D)}, {}]
    ffn_names = ("wg_t", "wu_t", "wd")
    late_w = {"ssm": [key for key in shards if key != (0, "win_t") and not (key[0] == 1 and key[1] in ffn_names)],
              "ffn": [(1, nme) for nme in ffn_names]}
    w_side = {key: _Side([(shards[k2], "gather", CHIPS3, 4) for k2 in late_w[key]]) for key in late_w}

    def filler(key):
        def w_fill(wls, gathered):
            for (i, nme), g in zip(late_w[key], gathered):
                wls[i][nme] = g.reshape(-1, g.shape[-1])
        return w_fill

    w_fill = {key: filler(key) for key in late_w}

    eighths = lambda g: g.reshape(8, g.shape[0] // 8, g.shape[1])
    early_g = [(1, k) for k in BIG_NAMES] + [(0, k) for k in BIG_NAMES if k != "win_t"]

    def g_side_fn(big1, big0):
        return _Side([(eighths((big1 if i == 1 else big0)[k]), "a2a", ALL7, 8) for i, k in early_g])

    params = [{k: wts[k][i] for k in SMALL_NAMES} for i in range(2)]
    loss_part, grad_x, bigs, smalls, dmods, early = _core(x, ctx, loss_target, mods_local, params, weights,
                                                           w_side, w_fill, g_side_fn)
    loss = lax.psum(loss_part, ("x", "y", "c"))

    dmod_local = jnp.stack([dmods[i].reshape(8, 6 * D) for i in range(2)])
    (dmod_all,) = _spread([(dmod_local, ALL7, 8)], "ag_dmod")
    dcols = lax.dynamic_slice_in_dim(dmod_all, chip * ncol, ncol, axis=3)
    dlat = jnp.transpose(dcols[:, :, :bl], (1, 0, 2, 3)).reshape(2, nseq, ncol)
    dlat = jnp.concatenate([dlat, jnp.zeros((2, 8, ncol), F32)], axis=1)
    dctx8 = dcols[:, :, bl:bl + 1]
    g_w_mod, gc_part = _mod_bwd(c_rows, w_mod, dlat, dctx8, "mod_bwd")
    (gc4,) = _spread([(gc_part, CHIPS3, 4)], "ag_cctx")
    g_b_mod, g_c_ctx = _bmod_cctx(dmod_all, gc4, c_ctx[None], "bmod_cctx")

    small_flat = jnp.concatenate([jnp.stack([smalls[i][k] for i in range(2)]).reshape(-1) for k in SMALL_NAMES])
    npad = (-small_flat.shape[0]) % (8 * 1024)
    small_flat = jnp.concatenate([small_flat, jnp.zeros((npad,), F32)])
    late = _comm([(eighths(bigs[0]["win_t"]), "a2a", ALL7, 8), (small_flat.reshape(8, -1, 1024), "a2a", ALL7, 8)],
                 "a2a_grads")
    sums = _sum8(list(early) + list(late), "gsum")
    fin = _spread([(s, (1,), 2) for s in sums[:-1]] + [(sums[-1], ALL7, 8)], "ag_grads")
    big_g = [{}, {}]
    for (i, k), g in zip(early_g + [(0, "win_t")], fin[:-1]):
        big_g[i][k] = g.reshape(-1, g.shape[-1])
    small_red = fin[-1].reshape(-1)

    grads = {}
    off = 0
    for k in SMALL_NAMES:
        shp = wts[k].shape
        nel = int(np.prod(shp))
        grads[k] = small_red[off:off + nel].reshape(shp)
        off += nel
    grads["c_ctx"] = g_c_ctx[0]
    grads["w_mod"] = g_w_mod
    grads["b_mod"] = g_b_mod
    grads["w_in"] = jnp.stack([big_g[i]["win_t"].T for i in range(2)])
    grads["w_out"] = jnp.stack([big_g[i]["wout"] for i in range(2)])
    grads["glu_w"] = jnp.stack([big_g[i]["wglu"] for i in range(2)])
    grads["ffn_w_gate"] = jnp.stack([big_g[i]["wg_t"].T for i in range(2)])
    grads["ffn_w_up"] = jnp.stack([big_g[i]["wu_t"].T for i in range(2)])
    grads["ffn_w_down"] = jnp.stack([big_g[i]["wd"] for i in range(2)])

    deltas, new_m, new_v = {}, {}, {}
    for k in order:
        deltas[k], new_m[k], new_v[k] = _adam(wts[k], grads[k], ms[k], vs[k], "adam_" + k)
    return (loss, grad_x, *[grads[k] for k in order], *[deltas[k] for k in order],
            *[new_m[k] for k in order], *[new_v[k] for k in order])
```

```python
import functools
import math

import numpy as np
import jax
import jax.numpy as jnp
from jax import lax
from jax.experimental import pallas as pl
from jax.experimental.pallas import tpu as pltpu

F32 = jnp.float32
BF16 = jnp.bfloat16
MXU_DTYPE = jnp.bfloat16
MCAT_A, MCAT_C = 2, 3
WOUT_PERM, WOUT_INV = (1, 2, 0, 3), (2, 0, 1, 3)

D = 1024
EPS = 1e-6
TB = 256
CTX = 256
CHUNK = 128
GRID_W = 64
A_W, B_W, C_W = 256, 512, 256
D_IN = 1280
D_FF = 2816
SSM_G, SSM_P, SSM_H = 32, 64, 16
ST = 64
POOL_WINDOWS = (2, 4, 8, 16)
N_DEV = 8
VMEM_LIMIT = 52 * 1024 * 1024
GELU_C = math.sqrt(2.0 / math.pi)

ADAM_LR, ADAM_B1, ADAM_B2, ADAM_EPS, ADAM_WD, ADAM_STEP = 0.001, 0.9, 0.999, 1e-08, 0.01, 10


def _cp(sem=None, vmem=VMEM_LIMIT, **kw):
    return pltpu.CompilerParams(dimension_semantics=sem, vmem_limit_bytes=vmem, **kw)


def _pick(n, cap):
    if n <= cap:
        return n
    best = None
    for t in range(128, cap + 1, 128):
        if n % t == 0:
            best = t
    assert best is not None, (n, cap)
    return best


def _gelu(x):
    return 0.5 * x * (1.0 + jnp.tanh(GELU_C * (x + 0.044715 * x * x * x)))


def _gelu_grad(x):
    t = jnp.tanh(GELU_C * (x + 0.044715 * x * x * x))
    return 0.5 * (1.0 + t) + 0.5 * x * (1.0 - t * t) * GELU_C * (1.0 + 3.0 * 0.044715 * x * x)


def _sigmoid(x):
    return 1.0 / (1.0 + jnp.exp(-x))


def _dot(a, b, dims):
    return lax.dot_general(a, b, (dims, ((), ())), preferred_element_type=F32)


def _nn(a, b):
    return _dot(a, b, ((1,), (0,)))


def _nt(a, b):
    return _dot(a, b, ((1,), (1,)))


def _tn(a, b):
    return _dot(a, b, ((0,), (0,)))


def _mm(pairs, nt, out_dtype, name, tm=512):
    m = pairs[0][0].shape[0]
    n = pairs[0][1].shape[0] if nt else pairs[0][1].shape[1]
    tn = _pick(n, 1408)
    tm = min(tm, m)
    npairs = len(pairs)

    def body(*refs):
        o_ref = refs[-1]
        acc = None
        for i in range(npairs):
            a = refs[2 * i][...].astype(MXU_DTYPE)
            b = refs[2 * i + 1][...].astype(MXU_DTYPE)
            r = _nt(a, b) if nt else _nn(a, b)
            acc = r if acc is None else acc + r
        o_ref[...] = acc.astype(o_ref.dtype)

    in_specs, flat = [], []
    for a, b in pairs:
        k = a.shape[1]
        in_specs.append(pl.BlockSpec((tm, k), lambda i, j: (i, 0)))
        in_specs.append(pl.BlockSpec((tn, k), lambda i, j: (j, 0)) if nt else pl.BlockSpec((k, tn), lambda i, j: (0, j)))
        flat += [a, b]
    return pl.pallas_call(
        body, grid=(m // tm, n // tn), in_specs=in_specs,
        out_specs=pl.BlockSpec((tm, tn), lambda i, j: (i, j)),
        out_shape=jax.ShapeDtypeStruct((m, n), out_dtype),
        compiler_params=_cp(("parallel", "parallel")), name=name)(*flat)


def _mm_tn(a, b, out_dtype, name, tm=512):
    m, k1 = a.shape
    n = b.shape[1]
    t1 = _pick(k1, 1408)
    tn = _pick(n, 1024)
    tm = min(tm, m)
    nsteps = m // tm

    def body(a_ref, b_ref, o_ref, acc_ref):
        t = pl.program_id(2)

        @pl.when(t == 0)
        def _():
            acc_ref[...] = jnp.zeros_like(acc_ref)

        acc_ref[...] += _tn(a_ref[...].astype(MXU_DTYPE), b_ref[...].astype(MXU_DTYPE))

        @pl.when(t == nsteps - 1)
        def _():
            o_ref[...] = acc_ref[...].astype(o_ref.dtype)

    return pl.pallas_call(
        body, grid=(k1 // t1, n // tn, nsteps),
        in_specs=[pl.BlockSpec((tm, t1), lambda i, j, t: (t, i)), pl.BlockSpec((tm, tn), lambda i, j, t: (t, j))],
        out_specs=pl.BlockSpec((t1, tn), lambda i, j, t: (i, j)),
        out_shape=jax.ShapeDtypeStruct((k1, n), out_dtype),
        scratch_shapes=[pltpu.VMEM((t1, tn), F32)],
        compiler_params=_cp(("parallel", "parallel", "arbitrary")), name=name)(a, b)


class _Layout:
    def __init__(self, bl, lat):
        self.bl, self.lat = bl, lat
        self.nlb = lat // TB
        self.nr = 1 + self.nlb
        self.nctx = bl
        self.nb = self.nr * bl
        self.nt = self.nb * TB
        self.ctx_row = bl

    def blk(self, g):
        gg = g - self.bl
        return jnp.where(g < self.bl, g, (gg % self.nlb + 1) * self.bl + gg // self.nlb)

    def modrow(self, g):
        return jnp.where(g < self.bl, self.ctx_row, (g - self.bl) // self.nlb)

    def first_of_row(self, g):
        return jnp.logical_or(g == 0, jnp.logical_and(g >= self.bl, (g - self.bl) % self.nlb == 0))

    def modrows_static(self):
        return np.array([self.ctx_row if j < self.bl else j % self.bl for j in range(self.nb)], np.int32)


def _tok_spec(lay):
    return pl.BlockSpec((TB, D), lambda g: (lay.blk(g), 0))


def _vec_spec():
    return pl.BlockSpec((1, D), lambda j: (0, 0))


def _mod_spec(lay, k):
    return pl.BlockSpec((1, 1, D), lambda g: (lay.blk(g) * 6 + k, 0, 0))


def _embed(lay, x2d, ctx2d, pe):
    bl, nlb = lay.bl, lay.nlb

    def body(x_ref, c_ref, pe_ref, o_ref):
        j = pl.program_id(0)

        @pl.when(j < bl)
        def _():
            o_ref[...] = c_ref[...]

        @pl.when(j >= bl)
        def _():
            o_ref[...] = x_ref[...] + pe_ref[...]

    pos = lambda j: jnp.maximum(j // bl - 1, 0)
    return pl.pallas_call(
        body, grid=(lay.nb,),
        in_specs=[pl.BlockSpec((TB, D), lambda j: ((j % bl) * nlb + pos(j), 0)),
                  pl.BlockSpec((TB, D), lambda j: (jnp.minimum(j, bl - 1), 0)),
                  pl.BlockSpec((TB, D), lambda j: (pos(j), 0))],
        out_specs=pl.BlockSpec((TB, D), lambda j: (j, 0)), out_shape=jax.ShapeDtypeStruct((lay.nt, D), F32),
        compiler_params=_cp(("parallel",)), name="embed")(x2d, ctx2d, pe)


def _normmod_fwd(lay, x, gain, modarr, ksh, ksc, name):
    def body(x_ref, g_ref, sh_ref, sc_ref, o_ref):
        xv = x_ref[...]
        r = lax.rsqrt(jnp.mean(xv * xv, axis=-1, keepdims=True) + EPS)
        o_ref[...] = ((xv * r * g_ref[...]) * (1.0 + sc_ref[0]) + sh_ref[0]).astype(o_ref.dtype)

    return pl.pallas_call(
        body, grid=(lay.nb,), in_specs=[_tok_spec(lay), _vec_spec(), _mod_spec(lay, ksh), _mod_spec(lay, ksc)],
        out_specs=_tok_spec(lay), out_shape=jax.ShapeDtypeStruct((lay.nt, D), MXU_DTYPE),
        compiler_params=_cp(("parallel",)), name=name)(x, gain, modarr, modarr)


def _acc_specs(lay):
    row = pl.BlockSpec((1, 1, D), lambda j: (lay.modrow(j), 0, 0))
    return row, jax.ShapeDtypeStruct((8, 1, D), F32)


def _normmod_bwd(lay, x, dh, dx_in, gain, modarr, ksc, name, latent_only=False):
    row_spec, row_shape = _acc_specs(lay)
    if latent_only:
        dx_spec = pl.BlockSpec((TB, D), lambda g: (jnp.maximum(g - lay.bl, 0), 0))
        dx_shape = jax.ShapeDtypeStruct((lay.bl * lay.lat, D), F32)
    else:
        dx_spec, dx_shape = _tok_spec(lay), jax.ShapeDtypeStruct((lay.nt, D), F32)

    def body(x_ref, dh_ref, dxi_ref, g_ref, sc_ref, dx_ref, dsh_ref, dsc_ref, dg_ref):
        j = pl.program_id(0)
        xv = x_ref[...]
        dhv = dh_ref[...].astype(F32)
        g = g_ref[...]
        sc1 = 1.0 + sc_ref[0]
        r = lax.rsqrt(jnp.mean(xv * xv, axis=-1, keepdims=True) + EPS)
        xh = xv * r
        dxh = dhv * (g * sc1)
        dx = r * (dxh - xh * jnp.mean(dxh * xh, axis=-1, keepdims=True))
        dx_ref[...] = dxi_ref[...] + dx

        @pl.when(lay.first_of_row(j))
        def _():
            dsh_ref[...] = jnp.zeros_like(dsh_ref)
            dsc_ref[...] = jnp.zeros_like(dsc_ref)

        @pl.when(j == 0)
        def _():
            dg_ref[...] = jnp.zeros_like(dg_ref)

        dsh_ref[0] += jnp.sum(dhv, axis=0, keepdims=True)
        dsc_ref[0] += jnp.sum(dhv * (xh * g), axis=0, keepdims=True)
        dg_ref[...] += jnp.sum(dhv * sc1 * xh, axis=0, keepdims=True)

    return pl.pallas_call(
        body, grid=(lay.nb,),
        in_specs=[_tok_spec(lay), _tok_spec(lay), _tok_spec(lay), _vec_spec(), _mod_spec(lay, ksc)],
        out_specs=[dx_spec, row_spec, row_spec, _vec_spec()],
        out_shape=[dx_shape, row_shape, row_shape, jax.ShapeDtypeStruct((1, D), F32)],
        compiler_params=_cp(("arbitrary",)), name=name)(x, dh, dx_in, gain, modarr)


def _resnorm_fwd(lay, x, m, gain, modarr, kgate, name):
    def body(x_ref, m_ref, g_ref, gate_ref, o_ref):
        mv = m_ref[...].astype(F32)
        r = lax.rsqrt(jnp.mean(mv * mv, axis=-1, keepdims=True) + EPS)
        o_ref[...] = x_ref[...] + gate_ref[0] * (mv * r * g_ref[...])

    return pl.pallas_call(
        body, grid=(lay.nb,), in_specs=[_tok_spec(lay), _tok_spec(lay), _vec_spec(), _mod_spec(lay, kgate)],
        out_specs=_tok_spec(lay), out_shape=jax.ShapeDtypeStruct((lay.nt, D), F32),
        compiler_params=_cp(("parallel",)), name=name)(x, m, gain, modarr)


def _resnorm_bwd(lay, dxn, m, gain, modarr, kgate, name):
    row_spec, row_shape = _acc_specs(lay)

    def body(d_ref, m_ref, g_ref, gate_ref, dm_ref, dgate_ref, dg_ref):
        j = pl.program_id(0)
        dv = d_ref[...]
        mv = m_ref[...].astype(F32)
        g = g_ref[...]
        r = lax.rsqrt(jnp.mean(mv * mv, axis=-1, keepdims=True) + EPS)
        xh = mv * r
        dy = dv * gate_ref[0]
        dxh = dy * g
        dm_ref[...] = (r * (dxh - xh * jnp.mean(dxh * xh, axis=-1, keepdims=True))).astype(dm_ref.dtype)

        @pl.when(lay.first_of_row(j))
        def _():
            dgate_ref[...] = jnp.zeros_like(dgate_ref)

        @pl.when(j == 0)
        def _():
            dg_ref[...] = jnp.zeros_like(dg_ref)

        dgate_ref[0] += jnp.sum(dv * (xh * g), axis=0, keepdims=True)
        dg_ref[...] += jnp.sum(dy * xh, axis=0, keepdims=True)

    return pl.pallas_call(
        body, grid=(lay.nb,), in_specs=[_tok_spec(lay), _tok_spec(lay), _vec_spec(), _mod_spec(lay, kgate)],
        out_specs=[_tok_spec(lay), row_spec, _vec_spec()],
        out_shape=[jax.ShapeDtypeStruct((lay.nt, D), MXU_DTYPE), row_shape, jax.ShapeDtypeStruct((1, D), F32)],
        compiler_params=_cp(("arbitrary",)), name=name)(dxn, m, gain, modarr)


def _loss_bwd(lay, xf, tgt2d):
    bl, nlb = lay.bl, lay.nlb

    def body(x_ref, t_ref, dx_ref, l_ref):
        j = pl.program_id(0)

        @pl.when(j == 0)
        def _():
            l_ref[...] = jnp.zeros_like(l_ref)

        @pl.when(j < bl)
        def _():
            dx_ref[...] = jnp.zeros_like(dx_ref)

        @pl.when(j >= bl)
        def _():
            e = x_ref[...] - t_ref[...]
            dx_ref[...] = e * (1.0 / D)
            l_ref[...] += jnp.sum(e * e) * (0.5 / D)

    tok = pl.BlockSpec((TB, D), lambda j: (j, 0))
    return pl.pallas_call(
        body, grid=(lay.nb,),
        in_specs=[tok, pl.BlockSpec((TB, D), lambda j: ((j % bl) * nlb + jnp.maximum(j // bl - 1, 0), 0))],
        out_specs=[tok, pl.BlockSpec((8, 128), lambda j: (0, 0))],
        out_shape=[jax.ShapeDtypeStruct((lay.nt, D), F32), jax.ShapeDtypeStruct((8, 128), F32)],
        compiler_params=_cp(("arbitrary",)), name="loss")(xf, tgt2d)


FF_TN = D_FF // 2
FF_CHUNKS = ((0, 512), (512, 512), (1024, 384))


def _ffn_up(h, wgt, wut, name, side=None):
    m = h.shape[0]
    tm, tn = min(512, m), FF_TN
    ni, nj = m // tm, D_FF // tn

    def body(*refs):
        (h_ref, wg_ref, wu_ref, g_ref, u_ref, a_ref), srefs = _side_split(refs, 3, 3, 0, side)
        j, i = pl.program_id(0), pl.program_id(1)
        _side_start(side, srefs, jnp.logical_and(i == 0, j == 0))
        hv = h_ref[...]
        for c0, cw in FF_CHUNKS:
            g = _nt(hv, wg_ref[c0:c0 + cw, :])
            u = _nt(hv, wu_ref[c0:c0 + cw, :])
            g_ref[:, c0:c0 + cw] = g.astype(g_ref.dtype)
            u_ref[:, c0:c0 + cw] = u.astype(u_ref.dtype)
            a_ref[:, c0:c0 + cw] = (g * _sigmoid(g) * u).astype(a_ref.dtype)
        _side_wait(side, srefs, jnp.logical_and(i == ni - 1, j == nj - 1))

    sd = side if side is not None else _Side([])
    osp = pl.BlockSpec((tm, tn), lambda j, i: (i, j))
    osh = jax.ShapeDtypeStruct((m, D_FF), MXU_DTYPE)
    outs = pl.pallas_call(
        body, grid=(nj, ni),
        in_specs=[pl.BlockSpec((tm, D), lambda j, i: (i, 0)), pl.BlockSpec((tn, D), lambda j, i: (j, 0)),
                  pl.BlockSpec((tn, D), lambda j, i: (j, 0))] + sd.in_specs,
        out_specs=[osp, osp, osp] + sd.out_specs, out_shape=[osh, osh, osh] + sd.out_shape,
        scratch_shapes=sd.scratch if side is not None else [],
        compiler_params=_cp(("arbitrary", "arbitrary") if side is not None else ("parallel", "parallel")),
        name=name)(h, wgt, wut, *sd.arrays)
    return outs[0], outs[1], outs[2], list(outs[3:])


def _ffn_down_bwd(df, wd, g, u, name):
    m = df.shape[0]
    tm, tn = min(512, m), FF_TN

    def body(df_ref, wd_ref, g_ref, u_ref, dg_ref, du_ref):
        dfv = df_ref[...]
        for c0, cw in FF_CHUNKS:
            da = _nt(dfv, wd_ref[c0:c0 + cw, :])
            gv = g_ref[:, c0:c0 + cw].astype(F32)
            uv = u_ref[:, c0:c0 + cw].astype(F32)
            s = _sigmoid(gv)
            dg_ref[:, c0:c0 + cw] = (da * uv * (s * (1.0 + gv * (1.0 - s)))).astype(dg_ref.dtype)
            du_ref[:, c0:c0 + cw] = (da * gv * s).astype(du_ref.dtype)

    osp = pl.BlockSpec((tm, tn), lambda j, i: (i, j))
    osh = jax.ShapeDtypeStruct((m, D_FF), MXU_DTYPE)
    return pl.pallas_call(
        body, grid=(D_FF // tn, m // tm),
        in_specs=[pl.BlockSpec((tm, D), lambda j, i: (i, 0)), pl.BlockSpec((tn, D), lambda j, i: (j, 0)), osp, osp],
        out_specs=[osp, osp], out_shape=[osh, osh],
        compiler_params=_cp(("parallel", "parallel")), name=name)(df, wd, g, u)


def _head_masks(shape):
    lane = lax.broadcasted_iota(jnp.int32, shape, 1)
    return [jnp.logical_and(lane >= 64 * h, lane < 64 * h + 64) for h in range(4)]


def _head_mean(x, masks):
    out = jnp.zeros_like(x)
    for mk in masks:
        s = jnp.sum(jnp.where(mk, x, 0.0), axis=-1, keepdims=True) * (1.0 / 64.0)
        out = jnp.where(mk, s, out)
    return out


def _gate_common(z, masks):
    zg = _gelu(z)
    u = zg[:, :A_W]
    v = zg[:, A_W:]
    mu = _head_mean(v, masks)
    vc = v - mu
    rstd = lax.rsqrt(_head_mean(vc * vc, masks) + EPS)
    return u, vc * rstd, rstd


def _gate_s(vn, ws_ref, bias, masks):
    parts = []
    for c in range(TB // CHUNK):
        vc = vn[c * CHUNK:(c + 1) * CHUNK]
        s = bias
        for h in range(4):
            s = s + _nn(ws_ref[h], jnp.where(masks[h][:CHUNK], vc, 0.0).astype(MXU_DTYPE))
        parts.append(s)
    return jnp.concatenate(parts, axis=0)


def _gate_fwd(lay, z, ws, bias, name):
    def body(z_ref, ws_ref, b_ref, o_ref):
        masks = _head_masks((TB, A_W))
        u, vn, _ = _gate_common(z_ref[...].astype(F32), masks)
        o_ref[...] = (u * _gate_s(vn, ws_ref, b_ref[...], masks)).astype(o_ref.dtype)

    return pl.pallas_call(
        body, grid=(lay.nb,),
        in_specs=[pl.BlockSpec((TB, 2 * A_W), lambda j: (j, 0)), pl.BlockSpec((4, CHUNK, CHUNK), lambda j: (0, 0, 0)),
                  pl.BlockSpec((CHUNK, A_W), lambda j: (0, 0))],
        out_specs=pl.BlockSpec((TB, A_W), lambda j: (j, 0)),
        out_shape=jax.ShapeDtypeStruct((lay.nt, A_W), MXU_DTYPE),
        compiler_params=_cp(("parallel",)), name=name)(z, ws, bias)


def _gate_bwd(lay, z, da, ws, wst, bias, name):
    def body(z_ref, da_ref, ws_ref, wst_ref, b_ref, dz_ref, dws_ref, db_ref):
        j = pl.program_id(0)

        @pl.when(j == 0)
        def _():
            dws_ref[...] = jnp.zeros_like(dws_ref)
            db_ref[...] = jnp.zeros_like(db_ref)

        masks = _head_masks((TB, A_W))
        zv = z_ref[...].astype(F32)
        u, vn, rstd = _gate_common(zv, masks)
        s = _gate_s(vn, ws_ref, b_ref[...], masks)
        dav = da_ref[...].astype(F32)
        du = dav * s
        ds = dav * u
        dvn_parts = []
        for c in range(TB // CHUNK):
            sl = slice(c * CHUNK, (c + 1) * CHUNK)
            ds_c = ds[sl]
            vn_c = vn[sl].astype(MXU_DTYPE)
            db_ref[...] += ds_c
            ds_b = ds_c.astype(MXU_DTYPE)
            dvn_c = jnp.zeros((CHUNK, A_W), F32)
            for h in range(4):
                mk = masks[h][:CHUNK]
                dws_ref[h] += _nt(jnp.where(mk, ds_c, 0.0).astype(MXU_DTYPE), vn_c)
                dvn_c = dvn_c + jnp.where(mk, _nn(wst_ref[h], ds_b), 0.0)
            dvn_parts.append(dvn_c)
        dvn = jnp.concatenate(dvn_parts, axis=0)
        dv = rstd * (dvn - _head_mean(dvn, masks) - vn * _head_mean(dvn * vn, masks))
        gg = _gelu_grad(zv)
        dz_ref[:, :A_W] = (du * gg[:, :A_W]).astype(dz_ref.dtype)
        dz_ref[:, A_W:] = (dv * gg[:, A_W:]).astype(dz_ref.dtype)

    return pl.pallas_call(
        body, grid=(lay.nb,),
        in_specs=[pl.BlockSpec((TB, 2 * A_W), lambda j: (j, 0)), pl.BlockSpec((TB, A_W), lambda j: (j, MCAT_A)),
                  pl.BlockSpec((4, CHUNK, CHUNK), lambda j: (0, 0, 0)), pl.BlockSpec((4, CHUNK, CHUNK), lambda j: (0, 0, 0)),
                  pl.BlockSpec((CHUNK, A_W), lambda j: (0, 0))],
        out_specs=[pl.BlockSpec((TB, 2 * A_W), lambda j: (j, 0)), pl.BlockSpec((4, CHUNK, CHUNK), lambda j: (0, 0, 0)),
                   pl.BlockSpec((CHUNK, A_W), lambda j: (0, 0))],
        out_shape=[jax.ShapeDtypeStruct((lay.nt, 2 * A_W), MXU_DTYPE), jax.ShapeDtypeStruct((4, CHUNK, CHUNK), F32),
                   jax.ShapeDtypeStruct((CHUNK, A_W), F32)],
        compiler_params=_cp(("arbitrary",)), name=name)(z, da, ws, wst, bias)


def _band_constants():
    bands = np.zeros((2, 4, TB, TB), np.float32)
    inv = np.zeros((2, 4, TB, 1), np.float32)
    for kind, n in ((0, GRID_W), (1, TB)):
        for i, w in enumerate(POOL_WINDOWS):
            for t in range(TB):
                base, tt = (t // n) * n, t % n
                lo = min(max(tt - w // 2, 0), n)
                hi = min(max(tt - w // 2 + w, 0), n)
                bands[kind, i, t, base + lo:base + hi] = 1.0
                inv[kind, i, t, 0] = 1.0 / (hi - lo)
    return bands, inv


def _split3(x):
    a = x.astype(MXU_DTYPE)
    r1 = x - a.astype(F32)
    b = r1.astype(MXU_DTYPE)
    c = (r1 - b.astype(F32)).astype(MXU_DTYPE)
    return a, b, c


def _window_apply(band_ref, inv_ref, x, masks, transpose, mxu_exact=False):
    out = jnp.zeros_like(x)
    for i in range(4):
        xi = x * inv_ref[0, i] if transpose else x
        acc = None
        for part in ((xi.astype(MXU_DTYPE),) if mxu_exact else _split3(xi)):
            r = _tn(band_ref[0, i], part) if transpose else _nn(band_ref[0, i], part)
            acc = r if acc is None else acc + r
        if not transpose:
            acc = acc * inv_ref[0, i]
        out = jnp.where(masks[i], acc, out)
    return out


def _pool_specs(lay):
    kind = lambda j: jnp.where(j < lay.nctx, 1, 0)
    return [pl.BlockSpec((1, 4, TB, TB), lambda j: (kind(j), 0, 0, 0)), pl.BlockSpec((1, 4, TB, 1), lambda j: (kind(j), 0, 0, 0))]


def _pool_fwd(lay, z, bands, inv, pw, scale, name):
    def body(p_ref, band_ref, inv_ref, pw_ref, sc_ref, o_ref):
        masks = _head_masks((TB, C_W))
        p = p_ref[...].astype(F32)
        diff = _window_apply(band_ref, inv_ref, p, masks, False, mxu_exact=True) - p
        o_ref[...] = (_nn(diff.astype(MXU_DTYPE), pw_ref[...]) * sc_ref[...]).astype(o_ref.dtype)

    return pl.pallas_call(
        body, grid=(lay.nb,),
        in_specs=[pl.BlockSpec((TB, C_W), lambda j: (j, 4))] + _pool_specs(lay)
        + [pl.BlockSpec((C_W, C_W), lambda j: (0, 0)), pl.BlockSpec((1, C_W), lambda j: (0, 0))],
        out_specs=pl.BlockSpec((TB, C_W), lambda j: (j, 0)),
        out_shape=jax.ShapeDtypeStruct((lay.nt, C_W), MXU_DTYPE),
        compiler_params=_cp(("parallel",)), name=name)(z, bands, inv, pw, scale)


def _pool_bwd(lay, z, dc, bands, inv, pw, scale, name):
    def body(p_ref, dc_ref, band_ref, inv_ref, pw_ref, sc_ref, dp_ref, dpw_ref, dsc_ref):
        j = pl.program_id(0)

        @pl.when(j == 0)
        def _():
            dpw_ref[...] = jnp.zeros_like(dpw_ref)
            dsc_ref[...] = jnp.zeros_like(dsc_ref)

        masks = _head_masks((TB, C_W))
        p = p_ref[...].astype(F32)
        dcv = dc_ref[...].astype(F32)
        diff = _window_apply(band_ref, inv_ref, p, masks, False, mxu_exact=True) - p
        diff_b = diff.astype(MXU_DTYPE)
        pre = _nn(diff_b, pw_ref[...])
        dsc_ref[...] += jnp.sum(dcv * pre, axis=0, keepdims=True)
        dpre = dcv * sc_ref[...]
        dpre_b = dpre.astype(MXU_DTYPE)
        dpw_ref[...] += _tn(diff_b, dpre_b)
        ddiff = _nt(dpre_b, pw_ref[...])
        dp_ref[...] = (_window_apply(band_ref, inv_ref, ddiff, masks, True) - ddiff).astype(dp_ref.dtype)

    return pl.pallas_call(
        body, grid=(lay.nb,),
        in_specs=[pl.BlockSpec((TB, C_W), lambda j: (j, 4)), pl.BlockSpec((TB, C_W), lambda j: (j, MCAT_C))] + _pool_specs(lay)
        + [pl.BlockSpec((C_W, C_W), lambda j: (0, 0)), pl.BlockSpec((1, C_W), lambda j: (0, 0))],
        out_specs=[pl.BlockSpec((TB, C_W), lambda j: (j, 0)), pl.BlockSpec((C_W, C_W), lambda j: (0, 0)),
                   pl.BlockSpec((1, C_W), lambda j: (0, 0))],
        out_shape=[jax.ShapeDtypeStruct((lay.nt, C_W), MXU_DTYPE), jax.ShapeDtypeStruct((C_W, C_W), F32),
                   jax.ShapeDtypeStruct((1, C_W), F32)],
        compiler_params=_cp(("arbitrary",)), name=name)(z, dc, bands, inv, pw, scale)


def _disc_math(lr, li, ldt, br, bi):
    dt = jnp.exp(ldt)
    e = jnp.exp(lr * dt)
    ar = e * jnp.cos(li * dt)
    ai = e * jnp.sin(li * dt)
    nr, ni = ar - 1.0, ai
    den = lr * lr + li * li
    qr = (nr * lr + ni * li) / den
    qi = (ni * lr - nr * li) / den
    return ar, ai, qr * br - qi * bi, qr * bi + qi * br


def _disc_fwd(lrx, lix, ldtx, brt, bit, name):
    def body(lr_ref, li_ref, ldt_ref, br_ref, bi_ref, ar_ref, ai_ref, obr_ref, obi_ref):
        ar, ai, obr, obi = _disc_math(lr_ref[...], li_ref[...], ldt_ref[...], br_ref[...], bi_ref[...])
        ar_ref[...] = ar
        ai_ref[...] = ai
        obr_ref[...] = obr
        obi_ref[...] = obi

    sh = jax.ShapeDtypeStruct(lrx.shape, F32)
    return pl.pallas_call(body, out_shape=[sh, sh, sh, sh], name=name)(lrx, lix, ldtx, brt, bit)


def _disc_bwd(lrx, lix, ldtx, brt, bit, dar, dai, dbr, dbi, name):
    nrow = lrx.shape[0] // SSM_H

    def body(lr_ref, li_ref, ldt_ref, br_ref, bi_ref, dar_ref, dai_ref, dbr_ref, dbi_ref,
             glr_ref, gli_ref, gdt_ref, gbr_ref, gbi_ref):
        _, vjp = jax.vjp(_disc_math, lr_ref[...], li_ref[...], ldt_ref[...], br_ref[...], bi_ref[...])
        glr, gli, gdt, gbr, gbi = vjp((dar_ref[...], dai_ref[...], dbr_ref[...], dbi_ref[...]))
        glr_ref[...] = jnp.sum(glr.reshape(nrow, SSM_H, SSM_P), axis=1)
        gli_ref[...] = jnp.sum(gli.reshape(nrow, SSM_H, SSM_P), axis=1)
        gdt_ref[...] = jnp.sum(jnp.sum(gdt.reshape(nrow, SSM_H, SSM_P), axis=1), axis=-1, keepdims=True)
        gbr_ref[...] = gbr
        gbi_ref[...] = gbi

    small = jax.ShapeDtypeStruct((nrow, SSM_P), F32)
    big = jax.ShapeDtypeStruct(lrx.shape, F32)
    return pl.pallas_call(body, out_shape=[small, small, jax.ShapeDtypeStruct((nrow, 1), F32), big, big],
                          name=name)(lrx, lix, ldtx, brt, bit, dar, dai, dbr, dbi)


HS = 1024
GQ, QC, QS = 8, 128, 512
LC = QS
SCAN_UNROLL = ST


def _scan_steps(step, carry):
    if SCAN_UNROLL >= ST:
        for s in range(ST):
            carry = step(s, carry)
        return carry

    def body(i, c):
        for j in range(SCAN_UNROLL):
            c = step(i * SCAN_UNROLL + j, c)
        return c

    return lax.fori_loop(0, ST // SCAN_UNROLL, body, carry)


def _tile_row(s):
    return s * 8 if isinstance(s, int) else pl.multiple_of(s * 8, 8)


def _dir_cat(x, d0, qq):
    xq = x[:, QC * qq:QC * qq + QC]
    zero = jnp.zeros_like(xq)
    return jnp.concatenate([jnp.where(d0, xq, zero), jnp.where(d0, zero, xq)], axis=1)


def _dir_pick(x, d0):
    return jnp.where(d0, x[:, :QC], x[:, QC:])


def _d0_rows(n):
    row = lax.broadcasted_iota(jnp.int32, (n, 1), 0)
    return jnp.bitwise_and(row, 4) == 0


def _scan_perm(bl):
    n = 2 * bl * ST
    p = np.zeros((n, n), np.float32)
    for s in range(ST):
        for d in range(2):
            for b in range(bl):
                t = s if d == 0 else ST - 1 - s
                p[s * 2 * bl + d * bl + b, d * bl * ST + b * ST + t] = 1.0
    return p


def _scan_maps(lay):
    spc = TB // ST
    nlc = lay.nlb * spc

    def fwd(k):
        return k // spc, k % spc

    def rev(k):
        cpos = nlc - 1 - jnp.maximum(k - spc, 0)
        return jnp.where(k < spc, 0, 1 + cpos // spc), jnp.where(k < spc, spc - 1 - k, cpos % spc)

    return fwd, rev


def _pack_rows(f_ref, r_ref, p_ref, rc):
    st = jnp.concatenate([f_ref[0].reshape(rc // 2, 256), r_ref[0].reshape(rc // 2, 256)], axis=0).astype(MXU_DTYPE)
    return _nn(p_ref[...], st).astype(MXU_DTYPE)


def _side_split(refs, n_in, n_out, n_scr, side):
    ns = side.n if side is not None else 0
    ins, sin = refs[:n_in], refs[n_in:n_in + ns]
    o0 = n_in + ns
    outs, sout = refs[o0:o0 + n_out], refs[o0 + n_out:o0 + n_out + ns]
    s0 = o0 + n_out + ns
    return ins + outs + refs[s0:s0 + n_scr], (sin, sout, refs[s0 + n_scr:])


def _side_start(side, srefs, first):
    if side is not None:
        @pl.when(first)
        def _():
            side.start(*srefs)


def _side_wait(side, srefs, last):
    if side is not None:
        @pl.when(last)
        def _():
            side.wait(*srefs)


def _ssm_fwd(lay, z, perm, bh, ch, ar8, ai8, name, side=None):
    bl = lay.bl
    rc = ST * 2 * bl
    nch = lay.nr * (TB // ST)
    fwd, rev = _scan_maps(lay)
    z4 = z.reshape(lay.nr, bl, TB, z.shape[1])

    def body(*refs):
        own, srefs = _side_split(refs, 7, 3, 2, side)
        uf_ref, ur_ref, p_ref, bh_ref, ch_ref, ar_ref, ai_ref, yf_ref, yr_ref, hst_ref, hs, hc = own
        f, k = pl.program_id(0), pl.program_id(1)
        _side_start(side, srefs, jnp.logical_and(f == 0, k == 0))

        @pl.when(k == 0)
        def _():
            hc[...] = jnp.zeros_like(hc)

        hst_ref[0] = hc[...]
        d0 = _d0_rows(rc)
        uv = _pack_rows(uf_ref, ur_ref, p_ref, rc)
        for q in range(2):
            cr, ci = 2 * QS * q, 2 * QS * q + QS
            hs[:, cr:cr + 2 * QS] = _nn(_dir_cat(uv, d0, q), bh_ref[q])
            ar = ar_ref[:, QS * q:QS * q + QS]
            ai = ai_ref[:, QS * q:QS * q + QS]

            def step(s, carry, cr=cr, ci=ci, ar=ar, ai=ai):
                hr, hi = carry
                base = _tile_row(s)
                nr = ar * hr - ai * hi + hs[pl.ds(base, 8), cr:cr + LC]
                ni = ar * hi + ai * hr + hs[pl.ds(base, 8), ci:ci + LC]
                hs[pl.ds(base, 8), cr:cr + LC] = nr
                hs[pl.ds(base, 8), ci:ci + LC] = ni
                return nr, ni

            hr, hi = _scan_steps(step, (hc[:, cr:cr + LC], hc[:, ci:ci + LC]))
            hc[:, cr:cr + LC] = hr
            hc[:, ci:ci + LC] = hi
        yi = jnp.concatenate(
            [_dir_pick(_nn(hs[:, 2 * QS * q:2 * QS * (q + 1)].astype(MXU_DTYPE), ch_ref[q]), d0) for q in range(2)], axis=1)
        yd = _tn(p_ref[...], yi.astype(MXU_DTYPE))
        yf_ref[0] = yd[:rc // 2].reshape(bl, ST, 256).astype(yf_ref.dtype)
        yr_ref[0] = yd[rc // 2:].reshape(bl, ST, 256).astype(yr_ref.dtype)
        _side_wait(side, srefs, jnp.logical_and(f == 1, k == nch - 1))

    sd = side if side is not None else _Side([])
    blk = (1, bl, ST, 256)
    ysh = jax.ShapeDtypeStruct((lay.nr, bl, TB, B_W), MXU_DTYPE)
    outs = pl.pallas_call(
        body, grid=(2, nch),
        in_specs=[pl.BlockSpec(blk, lambda f, k: (fwd(k)[0], 0, fwd(k)[1], 2 + f)),
                  pl.BlockSpec(blk, lambda f, k: (rev(k)[0], 0, rev(k)[1], 2 + f)),
                  pl.BlockSpec((rc, rc), lambda f, k: (0, 0)),
                  pl.BlockSpec((2, 2 * QC, 2 * QS), lambda f, k: (f, 0, 0)),
                  pl.BlockSpec((2, 2 * QS, 2 * QC), lambda f, k: (f, 0, 0)),
                  pl.BlockSpec((8, HS), lambda f, k: (0, f)), pl.BlockSpec((8, HS), lambda f, k: (0, f))] + sd.in_specs,
        out_specs=[pl.BlockSpec(blk, lambda f, k: (fwd(k)[0], 0, fwd(k)[1], f)),
                   pl.BlockSpec(blk, lambda f, k: (rev(k)[0], 0, rev(k)[1], f)),
                   pl.BlockSpec((1, 8, 2 * HS), lambda f, k: (k, 0, f))] + sd.out_specs,
        out_shape=[ysh, ysh, jax.ShapeDtypeStruct((nch, 8, 4 * HS), F32)] + sd.out_shape,
        scratch_shapes=[pltpu.VMEM((rc, 2 * HS), F32), pltpu.VMEM((8, 2 * HS), F32)] + (sd.scratch if side is not None else []),
        compiler_params=_cp(("arbitrary", "arbitrary")), name=name)(z4, z4, perm, bh, ch, ar8, ai8, *sd.arrays)
    yf, yr, hst = outs[:3]
    return yf.reshape(lay.nt, B_W), yr.reshape(lay.nt, B_W), hst, list(outs[3:])


def _ssm_bwd(lay, z, dy, perm, hst, bh, ch, ar8, ai8, name, side=None):
    bl = lay.bl
    rc = ST * 2 * bl
    nch = lay.nr * (TB // ST)
    fwd, rev = _scan_maps(lay)
    z4 = z.reshape(lay.nr, bl, TB, z.shape[1])
    dy4 = dy.reshape(lay.nr, bl, TB, B_W)

    def body(*refs):
        own, srefs = _side_split(refs, 10, 6, 5, side)
        (uf_ref, ur_ref, dyf_ref, dyr_ref, p_ref, hst_ref, bh_ref, ch_ref, ar_ref, ai_ref,
         duf_ref, dur_ref, dbh_ref, dch_ref, dar_ref, dai_ref, hs, es, ec, accr, acci) = own
        f, k = pl.program_id(0), pl.program_id(1)
        _side_start(side, srefs, jnp.logical_and(f == 0, k == 0))

        @pl.when(k == 0)
        def _():
            ec[...] = jnp.zeros_like(ec)
            accr[...] = jnp.zeros_like(accr)
            acci[...] = jnp.zeros_like(acci)
            dbh_ref[...] = jnp.zeros_like(dbh_ref)
            dch_ref[...] = jnp.zeros_like(dch_ref)

        d0 = _d0_rows(rc)
        uv = _pack_rows(uf_ref, ur_ref, p_ref, rc)
        dyv = _pack_rows(dyf_ref, dyr_ref, p_ref, rc)

        hs[0:8, :] = hst_ref[0]
        ucat, dycat = [], []
        for q in range(2):
            cr, ci = 2 * QS * q, 2 * QS * q + QS
            ucat.append(_dir_cat(uv, d0, q))
            dycat.append(_dir_cat(dyv, d0, q))
            hs[8:, cr:cr + 2 * QS] = _nn(ucat[q], bh_ref[q])
            ar = ar_ref[:, QS * q:QS * q + QS]
            ai = ai_ref[:, QS * q:QS * q + QS]

            def step(s, carry, cr=cr, ci=ci, ar=ar, ai=ai):
                hr, hi = carry
                base = _tile_row(s + 1)
                nr = ar * hr - ai * hi + hs[pl.ds(base, 8), cr:cr + LC]
                ni = ar * hi + ai * hr + hs[pl.ds(base, 8), ci:ci + LC]
                hs[pl.ds(base, 8), cr:cr + LC] = nr
                hs[pl.ds(base, 8), ci:ci + LC] = ni
                return nr, ni

            _scan_steps(step, (hs[0:8, cr:cr + LC], hs[0:8, ci:ci + LC]))
            dch_ref[q] += _tn(hs[8:, cr:cr + 2 * QS].astype(MXU_DTYPE), dycat[q])
            es[:, cr:cr + 2 * QS] = _nt(dycat[q], ch_ref[q])

        dui = []
        for q in range(2):
            cr, ci = 2 * QS * q, 2 * QS * q + QS
            ar = ar_ref[:, QS * q:QS * q + QS]
            ai = ai_ref[:, QS * q:QS * q + QS]

            def bstep(i, carry, cr=cr, ci=ci, ar=ar, ai=ai):
                er, ei, sr, si = carry
                base = _tile_row(ST - 1 - i)
                ner = es[pl.ds(base, 8), cr:cr + LC] + ar * er + ai * ei
                nei = es[pl.ds(base, 8), ci:ci + LC] - ai * er + ar * ei
                es[pl.ds(base, 8), cr:cr + LC] = ner
                es[pl.ds(base, 8), ci:ci + LC] = nei
                hpr = hs[pl.ds(base, 8), cr:cr + LC]
                hpi = hs[pl.ds(base, 8), ci:ci + LC]
                return ner, nei, sr + ner * hpr + nei * hpi, si - ner * hpi + nei * hpr

            lo = QS * q
            er, ei, sr, si = _scan_steps(
                bstep, (ec[:, cr:cr + LC], ec[:, ci:ci + LC], accr[:, lo:lo + LC], acci[:, lo:lo + LC]))
            ec[:, cr:cr + LC] = er
            ec[:, ci:ci + LC] = ei
            accr[:, lo:lo + LC] = sr
            acci[:, lo:lo + LC] = si
            eb = es[:, cr:cr + 2 * QS].astype(MXU_DTYPE)
            dui.append(_dir_pick(_nt(eb, bh_ref[q]), d0))
            dbh_ref[q] += _tn(ucat[q], eb)

        dud = _tn(p_ref[...], jnp.concatenate(dui, axis=1).astype(MXU_DTYPE))
        duf_ref[0] = dud[:rc // 2].reshape(bl, ST, 256).astype(duf_ref.dtype)
        dur_ref[0] = dud[rc // 2:].reshape(bl, ST, 256).astype(dur_ref.dtype)

        @pl.when(k == nch - 1)
        def _():
            for d in range(2):
                dar_ref[d:d + 1, :] = jnp.sum(accr[4 * d:4 * d + 4, :], axis=0, keepdims=True)
                dai_ref[d:d + 1, :] = jnp.sum(acci[4 * d:4 * d + 4, :], axis=0, keepdims=True)

        _side_wait(side, srefs, jnp.logical_and(f == 1, k == nch - 1))

    sd = side if side is not None else _Side([])
    last = lambda k: nch - 1 - k
    blk = (1, bl, ST, 256)
    fspec = lambda c0: pl.BlockSpec(blk, lambda f, k: (fwd(last(k))[0], 0, fwd(last(k))[1], c0 + f))
    rspec = lambda c0: pl.BlockSpec(blk, lambda f, k: (rev(last(k))[0], 0, rev(last(k))[1], c0 + f))
    dush = jax.ShapeDtypeStruct((lay.nr, bl, TB, B_W), MXU_DTYPE)
    outs = pl.pallas_call(
        body, grid=(2, nch),
        in_specs=[fspec(2), rspec(2), fspec(0), rspec(0),
                  pl.BlockSpec((rc, rc), lambda f, k: (0, 0)),
                  pl.BlockSpec((1, 8, 2 * HS), lambda f, k: (last(k), 0, f)),
                  pl.BlockSpec((2, 2 * QC, 2 * QS), lambda f, k: (f, 0, 0)),
                  pl.BlockSpec((2, 2 * QS, 2 * QC), lambda f, k: (f, 0, 0)),
                  pl.BlockSpec((8, HS), lambda f, k: (0, f)), pl.BlockSpec((8, HS), lambda f, k: (0, f))] + sd.in_specs,
        out_specs=[fspec(0), rspec(0),
                   pl.BlockSpec((2, 2 * QC, 2 * QS), lambda f, k: (f, 0, 0)),
                   pl.BlockSpec((2, 2 * QS, 2 * QC), lambda f, k: (f, 0, 0)),
                   pl.BlockSpec((2, HS), lambda f, k: (0, f)), pl.BlockSpec((2, HS), lambda f, k: (0, f))] + sd.out_specs,
        out_shape=[dush, dush, jax.ShapeDtypeStruct((4, 2 * QC, 2 * QS), F32),
                   jax.ShapeDtypeStruct((4, 2 * QS, 2 * QC), F32), jax.ShapeDtypeStruct((2, 2 * HS), F32),
                   jax.ShapeDtypeStruct((2, 2 * HS), F32)] + sd.out_shape,
        scratch_shapes=[pltpu.VMEM((rc + 8, 2 * HS), F32), pltpu.VMEM((rc, 2 * HS), F32), pltpu.VMEM((8, 2 * HS), F32),
                        pltpu.VMEM((8, HS), F32), pltpu.VMEM((8, HS), F32)] + (sd.scratch if side is not None else []),
        compiler_params=_cp(("arbitrary", "arbitrary")), name=name)(z4, z4, dy4, dy4, perm, hst, bh, ch, ar8, ai8, *sd.arrays)
    duf, dur, dbh, dch, dar, dai = outs[:6]
    return duf.reshape(lay.nt, B_W), dur.reshape(lay.nt, B_W), dbh, dch, dar, dai, list(outs[6:])


def _glu_fwd(lay, z, yf, yr, dvec, wglu, bglu, name):
    def body(u_ref, yf_ref, yr_ref, d_ref, w_ref, b_ref, o_ref, y_ref):
        y = yf_ref[...].astype(F32) + yr_ref[...].astype(F32) + d_ref[...] * u_ref[...].astype(F32)
        y_ref[...] = y
        g = _gelu(y)
        pre = _nn(g.astype(MXU_DTYPE), w_ref[...]) + b_ref[...]
        o_ref[...] = (g * _sigmoid(pre)).astype(o_ref.dtype)

    tok = pl.BlockSpec((TB, B_W), lambda j: (j, 0))
    vec = pl.BlockSpec((1, B_W), lambda j: (0, 0))
    return pl.pallas_call(
        body, grid=(lay.nb,),
        in_specs=[pl.BlockSpec((TB, B_W), lambda j: (j, 1)), tok, tok, vec, pl.BlockSpec((B_W, B_W), lambda j: (0, 0)), vec],
        out_specs=[tok, tok],
        out_shape=[jax.ShapeDtypeStruct((lay.nt, B_W), MXU_DTYPE), jax.ShapeDtypeStruct((lay.nt, B_W), F32)],
        compiler_params=_cp(("parallel",)), name=name)(z, yf, yr, dvec, wglu, bglu)


def _glu_bwd(lay, z, y, ds, dvec, wglu, bglu, name):
    def body(u_ref, y_ref, ds_ref, d_ref, w_ref, b_ref, dy_ref, dud_ref, dw_ref, db_ref, dd_ref):
        j = pl.program_id(0)

        @pl.when(j == 0)
        def _():
            dw_ref[...] = jnp.zeros_like(dw_ref)
            db_ref[...] = jnp.zeros_like(db_ref)
            dd_ref[...] = jnp.zeros_like(dd_ref)

        yv = y_ref[...]
        g = _gelu(yv)
        gb = g.astype(MXU_DTYPE)
        sg = _sigmoid(_nn(gb, w_ref[...]) + b_ref[...])
        dsv = ds_ref[...].astype(F32)
        dpre = dsv * g * sg * (1.0 - sg)
        dpre_b = dpre.astype(MXU_DTYPE)
        dg = dsv * sg + _nt(dpre_b, w_ref[...])
        dw_ref[...] += _tn(gb, dpre_b)
        db_ref[...] += jnp.sum(dpre, axis=0, keepdims=True)
        dy = dg * _gelu_grad(yv)
        dy_ref[...] = dy.astype(dy_ref.dtype)
        dd_ref[...] += jnp.sum(dy * u_ref[...].astype(F32), axis=0, keepdims=True)
        dud_ref[...] = (dy * d_ref[...]).astype(dud_ref.dtype)

    tok = pl.BlockSpec((TB, B_W), lambda j: (j, 0))
    vec = pl.BlockSpec((1, B_W), lambda j: (0, 0))
    mat = pl.BlockSpec((B_W, B_W), lambda j: (0, 0))
    vsh = jax.ShapeDtypeStruct((1, B_W), F32)
    return pl.pallas_call(
        body, grid=(lay.nb,),
        in_specs=[pl.BlockSpec((TB, B_W), lambda j: (j, 1)), tok, tok, vec, mat, vec],
        out_specs=[tok, tok, mat, vec, vec],
        out_shape=[jax.ShapeDtypeStruct((lay.nt, B_W), MXU_DTYPE), jax.ShapeDtypeStruct((lay.nt, B_W), F32),
                   jax.ShapeDtypeStruct((B_W, B_W), F32), vsh, vsh],
        compiler_params=_cp(("arbitrary",)), name=name)(z, y, ds, dvec, wglu, bglu)


def _dz_assemble(lay, dz_a, duf, dur, dud, dz_p, name):
    def body(a_ref, f_ref, r_ref, d_ref, p_ref, o_ref):
        o_ref[:, :2 * A_W] = a_ref[...].astype(o_ref.dtype)
        o_ref[:, 2 * A_W:2 * A_W + B_W] = (f_ref[...].astype(F32) + r_ref[...].astype(F32) + d_ref[...]).astype(o_ref.dtype)
        o_ref[:, 2 * A_W + B_W:] = p_ref[...].astype(o_ref.dtype)

    spec = lambda w: pl.BlockSpec((TB, w), lambda j: (j, 0))
    return pl.pallas_call(
        body, grid=(lay.nb,), in_specs=[spec(2 * A_W), spec(B_W), spec(B_W), spec(B_W), spec(C_W)],
        out_specs=spec(D_IN), out_shape=jax.ShapeDtypeStruct((lay.nt, D_IN), MXU_DTYPE),
        compiler_params=_cp(("parallel",)), name=name)(dz_a, duf, dur, dud, dz_p)


def _expand_rows(a):
    return jnp.broadcast_to(a[:, :, None, :], (2, SSM_G, SSM_H, SSM_P)).reshape(-1, SSM_P)


def _ssm_params(lam_re, lam_im, log_dt, b_re, b_im, c_re, c_im, name):
    lrx, lix = _expand_rows(lam_re), _expand_rows(lam_im)
    ldtx = _expand_rows(jnp.broadcast_to(log_dt[:, :, None], (2, SSM_G, SSM_P)))
    brt = jnp.transpose(b_re, (0, 1, 3, 2)).reshape(-1, SSM_P)
    bit = jnp.transpose(b_im, (0, 1, 3, 2)).reshape(-1, SSM_P)
    arx, aix, bbr, bbi = _disc_fwd(lrx, lix, ldtx, brt, bit, name)
    ar = arx.reshape(2, SSM_G, SSM_H, SSM_P)[:, :, 0].reshape(2, SSM_G * SSM_P)
    ai = aix.reshape(2, SSM_G, SSM_H, SSM_P)[:, :, 0].reshape(2, SSM_G * SSM_P)
    eye = jnp.eye(GQ, dtype=F32)

    def bmat(bt):
        t = bt.reshape(2, 4, GQ, SSM_H, SSM_P)
        return jnp.einsum('dqghp,gk->qdghkp', t, eye).reshape(4, 2 * QC, QS)

    bh = jnp.concatenate([bmat(bbr), bmat(bbi)], axis=-1).astype(MXU_DTYPE)

    def cmat(c):
        t = c.reshape(2, 4, GQ, SSM_H, SSM_P)
        return jnp.einsum('dqghp,gk->qgpdkh', t, eye).reshape(4, QS, 2 * QC)

    ch = jnp.concatenate([cmat(c_re), -cmat(c_im)], axis=1).astype(MXU_DTYPE)

    def rows8(a):
        return jnp.repeat(a, 4, axis=0)

    return dict(lrx=lrx, lix=lix, ldtx=ldtx, brt=brt, bit=bit, bh=bh, ch=ch, ar8=rows8(ar), ai8=rows8(ai))


def _ssm_param_grads(sp, dbh, dch, dar, dai, name):
    def bdiag(m):
        t = m.reshape(4, 2, GQ, SSM_H, GQ, SSM_P)
        return jnp.einsum('qdghgp->dqghp', t).reshape(-1, SSM_P)

    dbr, dbi = bdiag(dbh[..., :QS]), bdiag(dbh[..., QS:])

    def cdiag(m):
        t = m.reshape(4, GQ, SSM_P, 2, GQ, SSM_H)
        return jnp.einsum('qgpdgh->dqghp', t).reshape(2, SSM_G, SSM_H, SSM_P)

    dc_re, dc_im = cdiag(dch[:, :QS]), -cdiag(dch[:, QS:])

    def hrow(a):
        t = a.reshape(2, SSM_G, 1, SSM_P)
        return jnp.concatenate([t, jnp.zeros((2, SSM_G, SSM_H - 1, SSM_P), F32)], axis=2).reshape(-1, SSM_P)

    glr, gli, gdt, gbr, gbi = _disc_bwd(sp["lrx"], sp["lix"], sp["ldtx"], sp["brt"], sp["bit"],
                                        hrow(dar), hrow(dai), dbr, dbi, name)
    to_b = lambda g: jnp.transpose(g.reshape(2, SSM_G, SSM_H, SSM_P), (0, 1, 3, 2))
    return dict(ssm_lam_re=glr.reshape(2, SSM_G, SSM_P), ssm_lam_im=gli.reshape(2, SSM_G, SSM_P),
                ssm_log_dt=gdt.reshape(2, SSM_G), ssm_b_re=to_b(gbr), ssm_b_im=to_b(gbi),
                ssm_c_re=dc_re, ssm_c_im=dc_im)


def _layer_consts(p):
    c = {}
    c["ws"] = p["sgu_w"].astype(MXU_DTYPE)
    c["wst"] = jnp.transpose(p["sgu_w"], (0, 2, 1)).astype(MXU_DTYPE)
    c["gbias"] = jnp.repeat(p["sgu_b"].T, 64, axis=1)
    pw = jnp.zeros((C_W, C_W), F32)
    for i in range(4):
        pw = pw.at[64 * i:64 * i + 64, 64 * i:64 * i + 64].set(p["pool_w"][i])
    c["pw"] = pw.astype(MXU_DTYPE)
    c["pscale"] = p["pool_scale"].reshape(1, C_W)
    c["dvec"] = p["ssm_d"].reshape(1, B_W)
    c["bglu"] = p["glu_b"].reshape(1, B_W)
    return c


def _layer_fwd(lay, i, x, modarr, p, w, cst, sp, bands, inv, perm, sides=None):
    n = f"l{i}_"
    sides = sides or {}
    ssm_side, ssm_fill = sides.get("ssm", (None, None))
    ffn_side, ffn_fill = sides.get("ffn", (None, None))
    res = {"x0": x}
    h = _normmod_fwd(lay, x, p["norm_mix_pre"].reshape(1, D), modarr, 0, 1, n + "nm1")
    z = _mm([(h, w["win_t"])], True, MXU_DTYPE, n + "win")
    a = _gate_fwd(lay, z, cst["ws"], cst["gbias"], n + "gate")
    yf, yr, hst, extra = _ssm_fwd(lay, z, perm, sp["bh"], sp["ch"], sp["ar8"], sp["ai8"], n + "ssm", ssm_side)
    if ssm_fill is not None:
        ssm_fill(extra)
    s, y = _glu_fwd(lay, z, yf, yr, cst["dvec"], w["wglu"], cst["bglu"], n + "glu")
    c = _pool_fwd(lay, z, bands, inv, cst["pw"], cst["pscale"], n + "pool")
    mcat = jnp.concatenate([s, a, c], axis=1)
    m = _mm([(mcat, w["wout"])], False, MXU_DTYPE, n + "wout")
    x1 = _resnorm_fwd(lay, x, m, p["norm_mix_post"].reshape(1, D), modarr, 2, n + "rn1")
    h2 = _normmod_fwd(lay, x1, p["norm_ffn_pre"].reshape(1, D), modarr, 3, 4, n + "nm2")
    g, u, act, extra = _ffn_up(h2, w["wg_t"], w["wu_t"], n + "ffn_up", ffn_side)
    if ffn_fill is not None:
        ffn_fill(extra)
    f = _mm([(act, w["wd"])], False, MXU_DTYPE, n + "ffn_down")
    x2 = _resnorm_fwd(lay, x1, f, p["norm_ffn_post"].reshape(1, D), modarr, 5, n + "rn2")
    res.update(h=h, z=z, hst=hst, y=y, mcat=mcat, m=m, x1=x1, h2=h2, g=g, u=u, act=act, f=f)
    return x2, res


def _layer_bwd(lay, i, dx2, modarr, p, w, cst, sp, bands, inv, perm, res, side_fn=None):
    n = f"l{i}b_"
    big, small = {}, {}
    df, dg2, gpost2 = _resnorm_bwd(lay, dx2, res["f"], p["norm_ffn_post"].reshape(1, D), modarr, 5, n + "rn2")
    big["wd"] = _mm_tn(res["act"], df, MXU_DTYPE, n + "dwd")
    dg, du = _ffn_down_bwd(df, w["wd"], res["g"], res["u"], n + "ffn_down")
    dh2 = _mm([(dg, w["wg_t"]), (du, w["wu_t"])], False, MXU_DTYPE, n + "dh2")
    big["wg_t"] = _mm_tn(dg, res["h2"], MXU_DTYPE, n + "dwg")
    big["wu_t"] = _mm_tn(du, res["h2"], MXU_DTYPE, n + "dwu")
    dx1, dsh2, dsc2, gpre2 = _normmod_bwd(lay, res["x1"], dh2, dx2, p["norm_ffn_pre"].reshape(1, D), modarr, 4, n + "nm2")
    dm, dg1, gpost1 = _resnorm_bwd(lay, dx1, res["m"], p["norm_mix_post"].reshape(1, D), modarr, 2, n + "rn1")
    big["wout"] = _unperm_wout(_mm_tn(res["mcat"], dm, MXU_DTYPE, n + "dwout"))
    dmcat = _mm([(dm, w["wout"])], True, MXU_DTYPE, n + "dmcat")
    z = res["z"]
    dz_a, dws, dgb = _gate_bwd(lay, z, dmcat, cst["ws"], cst["wst"], cst["gbias"], n + "gate")
    dy, dud, dwglu, dbglu, ddvec = _glu_bwd(lay, z, res["y"], dmcat, cst["dvec"], w["wglu"], cst["bglu"], n + "glu")
    big["wglu"] = dwglu.astype(MXU_DTYPE)
    side = side_fn(big) if side_fn is not None else None
    duf, dur, dbh, dch, dar, dai, early = _ssm_bwd(lay, z, dy, perm, res["hst"], sp["bh"], sp["ch"], sp["ar8"],
                                                   sp["ai8"], n + "ssm", side)
    dz_p, dpw, dpsc = _pool_bwd(lay, z, dmcat, bands, inv, cst["pw"], cst["pscale"], n + "pool")
    dz = _dz_assemble(lay, dz_a, duf, dur, dud, dz_p, n + "dz")
    big["win_t"] = _mm_tn(dz, res["h"], MXU_DTYPE, n + "dwin")
    dh = _mm([(dz, w["win_t"])], False, MXU_DTYPE, n + "dh")
    dx, dsh1, dsc1, gpre1 = _normmod_bwd(lay, res["x0"], dh, dx1, p["norm_mix_pre"].reshape(1, D), modarr, 1, n + "nm1",
                                         latent_only=(i == 0))

    small.update(norm_mix_pre=gpre1[0], norm_mix_post=gpost1[0], norm_ffn_pre=gpre2[0], norm_ffn_post=gpost2[0])
    small["sgu_w"] = dws
    small["sgu_b"] = jnp.sum(dgb.reshape(CHUNK, 4, 64), axis=-1).T
    small.update(_ssm_param_grads(sp, dbh, dch, dar, dai, n + "disc"))
    small["ssm_d"] = ddvec.reshape(SSM_G, SSM_H)
    small["glu_b"] = dbglu[0]
    small["pool_w"] = jnp.stack([dpw[64 * k:64 * k + 64, 64 * k:64 * k + 64] for k in range(4)])
    small["pool_scale"] = dpsc[0]
    dmod = jnp.concatenate([dsh1, dsc1, dg1, dsh2, dsc2, dg2], axis=1)[:lay.bl + 1]
    dmod = jnp.concatenate([dmod, jnp.zeros((8 - lay.bl - 1, 6, D), F32)], axis=0)
    return dx, big, small, dmod, early


def _perm_wout(w):
    return w.reshape(4, D // 4, D)[np.array(WOUT_PERM)].reshape(D, D)


def _unperm_wout(g):
    return g.reshape(4, D // 4, D)[np.array(WOUT_INV)].reshape(D, D)


SMALL_NAMES = ["norm_mix_pre", "norm_mix_post", "norm_ffn_pre", "norm_ffn_post", "sgu_w", "sgu_b", "ssm_lam_re",
               "ssm_lam_im", "ssm_log_dt", "ssm_b_re", "ssm_b_im", "ssm_c_re", "ssm_c_im", "ssm_d", "glu_b", "pool_w",
               "pool_scale"]
BIG_NAMES = ["win_t", "wout", "wglu", "wg_t", "wu_t", "wd"]


def _sincos_2d(rows, cols, dim):
    quarter = dim // 4
    omega = 1.0 / (10000.0 ** (jnp.arange(quarter, dtype=F32) / quarter))
    r = jnp.arange(rows, dtype=F32)[:, None] * omega
    cc = jnp.arange(cols, dtype=F32)[:, None] * omega
    er = jnp.concatenate([jnp.sin(r), jnp.cos(r)], axis=-1)
    ec = jnp.concatenate([jnp.sin(cc), jnp.cos(cc)], axis=-1)
    pe = jnp.concatenate([jnp.broadcast_to(er[:, None, :], (rows, cols, dim // 2)),
                          jnp.broadcast_to(ec[None, :, :], (rows, cols, dim // 2))], axis=-1)
    return pe.reshape(rows * cols, dim)


def _core(x, ctx, target, mods_local, params, weights, w_side=None, w_fill=None, g_side_fn=None):
    bl, lat, _ = x.shape
    assert bl == 4 and lat % TB == 0, "the scan fills 8 sublanes with 2 directions x 4 sequences"
    lay = _Layout(bl, lat)
    pe = _sincos_2d(lat // GRID_W, GRID_W, D)
    xt = _embed(lay, x.reshape(bl * lat, D), ctx.reshape(bl * CTX, D), pe)
    bands_np, inv_np = _band_constants()
    bands, inv = jnp.asarray(bands_np, MXU_DTYPE), jnp.asarray(inv_np, F32)
    perm = jnp.asarray(_scan_perm(bl), MXU_DTYPE)
    rows = lay.modrows_static()
    modarrs, csts, sps, ress, wls = [], [], [], [], []
    for i in range(2):
        modarrs.append(mods_local[i][rows].reshape(lay.nb * 6, 1, D))
        csts.append(_layer_consts(params[i]))
        p = params[i]
        sps.append(_ssm_params(p["ssm_lam_re"], p["ssm_lam_im"], p["ssm_log_dt"], p["ssm_b_re"], p["ssm_b_im"],
                               p["ssm_c_re"], p["ssm_c_im"], f"l{i}_disc"))
        wls.append(dict(weights[i]))

    w_side = w_side or {}

    def fill_of(key):
        def fill(extra):
            if key in w_side:
                w_fill[key](wls, extra)
            if key == "ssm":
                for w in wls:
                    w["wout"] = _perm_wout(w["wout"])
        return fill

    sides0 = {key: (w_side.get(key), fill_of(key)) for key in ("ssm", "ffn")}
    for i in range(2):
        xt, res = _layer_fwd(lay, i, xt, modarrs[i], params[i], wls[i], csts[i], sps[i], bands, inv, perm,
                             sides0 if i == 0 else None)
        ress.append(res)
    dx, lossv = _loss_bwd(lay, xt, target.reshape(bl * lat, D))
    bigs, smalls, dmods, early = [None, None], [None, None], [None, None], []
    for i in (1, 0):
        side_fn = (lambda big0: g_side_fn(bigs[1], big0)) if (i == 0 and g_side_fn is not None) else None
        dx, bigs[i], smalls[i], dmods[i], ex = _layer_bwd(lay, i, dx, modarrs[i], params[i], wls[i], csts[i], sps[i],
                                                           bands, inv, perm, ress[i], side_fn)
        early += ex
    return lossv[0, 0], dx.reshape(bl, lat, D), bigs, smalls, dmods, early


def _my_index():
    return 4 * lax.axis_index("x") + 2 * lax.axis_index("y") + lax.axis_index("c")


def _peer(k):
    x, y, c = lax.axis_index("x"), lax.axis_index("y"), lax.axis_index("c")
    kx, ky, kc = (k >> 2) & 1, (k >> 1) & 1, k & 1
    px = 1 - x if kx else x
    py = 1 - y if ky else y
    pc = 1 - c if kc else c
    return (px, py, pc), 4 * px + 2 * py + pc


class _Side:
    def __init__(self, items):
        self.items = items
        self.n = len(items)
        self.ncopies = sum(len(it[2]) for it in items)
        self.arrays = [it[0] for it in items]
        anyspec = pl.BlockSpec(memory_space=pl.ANY)
        self.in_specs = [anyspec] * self.n
        self.out_specs = [anyspec] * self.n
        self.out_shape = [jax.ShapeDtypeStruct((slots,) + tuple(a.shape) if mode == "gather" else tuple(a.shape), a.dtype)
                          for a, mode, ks, slots in items]
        self.scratch = [pltpu.SemaphoreType.DMA((self.ncopies,)), pltpu.SemaphoreType.DMA((self.ncopies,)),
                        pltpu.SemaphoreType.DMA((self.n,))]

    def _copies(self, ins, outs, sems):
        send_sems, recv_sems, local_sems = sems
        slot_of = lambda idx, slots: idx if slots == 8 else (idx // 2 if slots == 4 else idx % 2)
        me = _my_index()
        local, sends, recvs = [], [], []
        q = 0
        for t, (arr, mode, ks, slots) in enumerate(self.items):
            src_own = ins[t] if mode == "gather" else ins[t].at[me]
            local.append(pltpu.make_async_copy(src_own, outs[t].at[slot_of(me, slots)], local_sems.at[t]))
            for k in ks:
                peer, pidx = _peer(k)
                src = ins[t] if mode == "gather" else ins[t].at[pidx]
                sends.append(pltpu.make_async_remote_copy(
                    src_ref=src, dst_ref=outs[t].at[slot_of(me, slots)], send_sem=send_sems.at[q], recv_sem=recv_sems.at[q],
                    device_id=peer, device_id_type=pl.DeviceIdType.MESH))
                recvs.append(pltpu.make_async_remote_copy(
                    src_ref=src, dst_ref=outs[t].at[slot_of(pidx, slots)], send_sem=send_sems.at[q], recv_sem=recv_sems.at[q],
                    device_id=peer, device_id_type=pl.DeviceIdType.MESH))
                q += 1
        return local, sends, recvs

    def start(self, ins, outs, sems):
        local, sends, _ = self._copies(ins, outs, sems)
        for cp in sends + local:
            cp.start()

    def wait(self, ins, outs, sems):
        local, sends, recvs = self._copies(ins, outs, sems)
        for cp in recvs:
            cp.wait_recv()
        for cp in sends:
            cp.wait_send()
        for cp in local:
            cp.wait()


def _comm(items, name):
    side = _Side(items)
    n = side.n

    def body(*refs):
        ins, outs, sems = refs[:n], refs[n:2 * n], refs[2 * n:]
        side.start(ins, outs, sems)
        side.wait(ins, outs, sems)

    return pl.pallas_call(
        body, in_specs=side.in_specs, out_specs=side.out_specs, out_shape=side.out_shape, scratch_shapes=side.scratch,
        compiler_params=pltpu.CompilerParams(has_side_effects=True), name=name)(*side.arrays)


def _spread(items, name):
    n = len(items)
    ncopies = sum(len(it[1]) for it in items)

    def slot_of(idx, slots):
        return idx if slots == 8 else (idx // 2 if slots == 4 else idx % 2)

    def body(*refs):
        ins, outs, bufs = refs[:n], refs[n:2 * n], refs[2 * n:3 * n]
        load_sems, store_sems, send_sems, recv_sems = refs[3 * n:]
        me = _my_index()
        loads = [pltpu.make_async_copy(ins[t], bufs[t], load_sems.at[t]) for t in range(n)]
        for cp in loads:
            cp.start()
        stores, sends, recvs = [], [], []
        q = 0
        for t, (arr, ks, slots) in enumerate(items):
            loads[t].wait()
            own = outs[t].at[slot_of(me, slots)]
            stores.append(pltpu.make_async_copy(bufs[t], own, store_sems.at[t]))
            stores[-1].start()
            for k in ks:
                peer, pidx = _peer(k)
                sends.append(pltpu.make_async_remote_copy(
                    src_ref=bufs[t], dst_ref=own, send_sem=send_sems.at[q], recv_sem=recv_sems.at[q],
                    device_id=peer, device_id_type=pl.DeviceIdType.MESH))
                recvs.append(pltpu.make_async_remote_copy(
                    src_ref=bufs[t], dst_ref=outs[t].at[slot_of(pidx, slots)], send_sem=send_sems.at[q],
                    recv_sem=recv_sems.at[q], device_id=peer, device_id_type=pl.DeviceIdType.MESH))
                sends[-1].start()
                q += 1
        for cp in recvs:
            cp.wait_recv()
        for cp in sends:
            cp.wait_send()
        for cp in stores:
            cp.wait()

    anyspec = pl.BlockSpec(memory_space=pl.ANY)
    return pl.pallas_call(
        body, in_specs=[anyspec] * n, out_specs=[anyspec] * n,
        out_shape=[jax.ShapeDtypeStruct((slots,) + tuple(arr.shape), arr.dtype) for arr, ks, slots in items],
        scratch_shapes=[pltpu.VMEM(tuple(arr.shape), arr.dtype) for arr, ks, slots in items]
        + [pltpu.SemaphoreType.DMA((n,)), pltpu.SemaphoreType.DMA((n,)), pltpu.SemaphoreType.DMA((ncopies,)),
           pltpu.SemaphoreType.DMA((ncopies,))],
        compiler_params=pltpu.CompilerParams(has_side_effects=True, vmem_limit_bytes=VMEM_LIMIT),
        name=name)(*[it[0] for it in items])


ALL7 = (1, 2, 3, 4, 5, 6, 7)
CHIPS3 = (2, 4, 6)


def _sum8(parts, name):
    def one(a, nm):
        _, r, c = a.shape
        tr = r if r <= 512 else _pick_rows(r)

        def body(a_ref, o_ref):
            acc = a_ref[0].astype(F32)
            for q in range(1, a_ref.shape[0]):
                acc = acc + a_ref[q].astype(F32)
            o_ref[...] = acc

        return pl.pallas_call(
            body, grid=(r // tr,), in_specs=[pl.BlockSpec((a.shape[0], tr, c), lambda i: (0, i, 0))],
            out_specs=pl.BlockSpec((tr, c), lambda i: (i, 0)), out_shape=jax.ShapeDtypeStruct((r, c), F32),
            compiler_params=_cp(("parallel",)), name=nm)(a)

    return [one(a, f"{name}{i}") for i, a in enumerate(parts)]


def _pick_rows(r, cap=512):
    for t in (512, 352, 256, 176, 128, 64, 32, 16, 8):
        if r % t == 0 and t <= cap:
            return t
    return r


def _adam(w, g, m, v, name):
    shape = w.shape
    nel = int(np.prod(shape))
    c1 = 1.0 / (1.0 - ADAM_B1 ** ADAM_STEP)
    c2 = 1.0 / (1.0 - ADAM_B2 ** ADAM_STEP)

    def body(w_ref, g_ref, m_ref, v_ref, d_ref, nm_ref, nv_ref):
        gv = g_ref[...]
        nm = ADAM_B1 * m_ref[...] + (1.0 - ADAM_B1) * gv
        nv = ADAM_B2 * v_ref[...] + (1.0 - ADAM_B2) * (gv * gv)
        d_ref[...] = -ADAM_LR * ((nm * c1) / (jnp.sqrt(nv * c2) + ADAM_EPS) + ADAM_WD * w_ref[...])
        nm_ref[...] = nm
        nv_ref[...] = nv

    padded = int(np.prod(shape[:-2])) * (-(-shape[-2] // 8) * 8) * (-(-shape[-1] // 128) * 128) if len(shape) >= 2 else nel
    if len(shape) >= 2 and padded <= 1024 * 1024:
        sh = jax.ShapeDtypeStruct(shape, F32)
        return pl.pallas_call(body, out_shape=[sh] * 3, compiler_params=_cp(None), name=name)(w, g, m, v)

    if len(shape) >= 2 and shape[-1] >= 128:
        lanes = shape[-1]
    else:
        lanes = 512 if nel % 512 == 0 else 128
    r = nel // lanes
    tr = r if r * lanes <= 384 * 1024 else _pick_rows(r, 384 * 1024 // lanes)

    spec = pl.BlockSpec((tr, lanes), lambda i: (i, 0))
    sh = jax.ShapeDtypeStruct((r, lanes), F32)
    outs = pl.pallas_call(
        body, grid=(r // tr,), in_specs=[spec] * 4, out_specs=[spec] * 3, out_shape=[sh] * 3,
        compiler_params=_cp(("parallel",)), name=name)(*[a.reshape(r, lanes) for a in (w, g, m, v)])
    return [o.reshape(shape) for o in outs]


def _silu(x):
    return x * _sigmoid(x)


def _mod_fwd(c_rows, w_mod, b_cols, name):
    def body(c_ref, w_ref, b_ref, o_ref):
        s = _silu(c_ref[...])
        for l in range(2):
            o_ref[l] = jnp.dot(s, w_ref[l], preferred_element_type=F32, precision=lax.Precision.HIGHEST) + b_ref[l]

    nc = w_mod.shape[2]
    return pl.pallas_call(body, out_shape=jax.ShapeDtypeStruct((2, c_rows.shape[0], nc), F32),
                          compiler_params=_cp(None), name=name)(c_rows, w_mod, b_cols)


def _mod_bwd(c_rows, w_mod, dlat, dctx8, name):
    nrow = c_rows.shape[0]
    nb = nrow - 8

    def body(c_ref, w_ref, dl_ref, dc_ref, gw_ref, gc_ref):
        s = _silu(c_ref[...])
        ctx_row = lax.broadcasted_iota(jnp.int32, (nrow, 1), 0) == nb
        gc = jnp.zeros((1, D), F32)
        for l in range(2):
            dctx = dc_ref[0, l]
            for q in range(1, 8):
                dctx = dctx + dc_ref[q, l]
            dm = dl_ref[l] + jnp.where(ctx_row, dctx, 0.0)
            gw_ref[l] = lax.dot_general(s, dm, (((0,), (0,)), ((), ())), preferred_element_type=F32,
                                        precision=lax.Precision.HIGHEST)
            gc = gc + lax.dot_general(dctx, w_ref[l], (((1,), (1,)), ((), ())), preferred_element_type=F32,
                                      precision=lax.Precision.HIGHEST)
        gc_ref[...] = gc

    nc = w_mod.shape[2]
    return pl.pallas_call(body, out_shape=[jax.ShapeDtypeStruct((2, D, nc), F32), jax.ShapeDtypeStruct((1, D), F32)],
                          compiler_params=_cp(None), name=name)(c_rows, w_mod, dlat, dctx8)


def _bmod_cctx(dmod_all, gc4, c_ctx, name):
    def body(dm_ref, gc_ref, cc_ref, gb_ref, gcc_ref):
        for l in range(2):
            acc = jnp.sum(dm_ref[0, l], axis=0, keepdims=True)
            for q in range(1, 8):
                acc = acc + jnp.sum(dm_ref[q, l], axis=0, keepdims=True)
            gb_ref[l:l + 1, :] = acc
        g = gc_ref[0] + gc_ref[1] + gc_ref[2] + gc_ref[3]
        cv = cc_ref[...]
        sg = _sigmoid(cv)
        gcc_ref[...] = g * (sg * (1.0 + cv * (1.0 - sg)))

    return pl.pallas_call(body, out_shape=[jax.ShapeDtypeStruct((2, 6 * D), F32), jax.ShapeDtypeStruct((1, D), F32)],
                          compiler_params=_cp(None), name=name)(dmod_all, gc4, c_ctx)


def kernel(x, c, ctx, c_ctx, w_mod, b_mod, norm_mix_pre, norm_mix_post, norm_ffn_pre, norm_ffn_post, w_in, w_out, sgu_w, sgu_b, ssm_lam_re, ssm_lam_im, ssm_log_dt, ssm_b_re, ssm_b_im, ssm_c_re, ssm_c_im, ssm_d, glu_w, glu_b, pool_w, pool_scale, ffn_w_gate, ffn_w_up, ffn_w_down, loss_target, m_c_ctx, m_w_mod, m_b_mod, m_norm_mix_pre, m_norm_mix_post, m_norm_ffn_pre, m_norm_ffn_post, m_w_in, m_w_out, m_sgu_w, m_sgu_b, m_ssm_lam_re, m_ssm_lam_im, m_ssm_log_dt, m_ssm_b_re, m_ssm_b_im, m_ssm_c_re, m_ssm_c_im, m_ssm_d, m_glu_w, m_glu_b, m_pool_w, m_pool_scale, m_ffn_w_gate, m_ffn_w_up, m_ffn_w_down, v_c_ctx, v_w_mod, v_b_mod, v_norm_mix_pre, v_norm_mix_post, v_norm_ffn_pre, v_norm_ffn_post, v_w_in, v_w_out, v_sgu_w, v_sgu_b, v_ssm_lam_re, v_ssm_lam_im, v_ssm_log_dt, v_ssm_b_re, v_ssm_b_im, v_ssm_c_re, v_ssm_c_im, v_ssm_d, v_glu_w, v_glu_b, v_pool_w, v_pool_scale, v_ffn_w_gate, v_ffn_w_up, v_ffn_w_down):
    wts = dict(c_ctx=c_ctx, w_mod=w_mod, b_mod=b_mod, norm_mix_pre=norm_mix_pre, norm_mix_post=norm_mix_post,
               norm_ffn_pre=norm_ffn_pre, norm_ffn_post=norm_ffn_post, w_in=w_in, w_out=w_out, sgu_w=sgu_w, sgu_b=sgu_b,
               ssm_lam_re=ssm_lam_re, ssm_lam_im=ssm_lam_im, ssm_log_dt=ssm_log_dt, ssm_b_re=ssm_b_re, ssm_b_im=ssm_b_im,
               ssm_c_re=ssm_c_re, ssm_c_im=ssm_c_im, ssm_d=ssm_d, glu_w=glu_w, glu_b=glu_b, pool_w=pool_w,
               pool_scale=pool_scale, ffn_w_gate=ffn_w_gate, ffn_w_up=ffn_w_up, ffn_w_down=ffn_w_down)
    ms = dict(c_ctx=m_c_ctx, w_mod=m_w_mod, b_mod=m_b_mod, norm_mix_pre=m_norm_mix_pre, norm_mix_post=m_norm_mix_post,
              norm_ffn_pre=m_norm_ffn_pre, norm_ffn_post=m_norm_ffn_post, w_in=m_w_in, w_out=m_w_out, sgu_w=m_sgu_w,
              sgu_b=m_sgu_b, ssm_lam_re=m_ssm_lam_re, ssm_lam_im=m_ssm_lam_im, ssm_log_dt=m_ssm_log_dt,
              ssm_b_re=m_ssm_b_re, ssm_b_im=m_ssm_b_im, ssm_c_re=m_ssm_c_re, ssm_c_im=m_ssm_c_im, ssm_d=m_ssm_d,
              glu_w=m_glu_w, glu_b=m_glu_b, pool_w=m_pool_w, pool_scale=m_pool_scale, ffn_w_gate=m_ffn_w_gate,
              ffn_w_up=m_ffn_w_up, ffn_w_down=m_ffn_w_down)
    vs = dict(c_ctx=v_c_ctx, w_mod=v_w_mod, b_mod=v_b_mod, norm_mix_pre=v_norm_mix_pre, norm_mix_post=v_norm_mix_post,
              norm_ffn_pre=v_norm_ffn_pre, norm_ffn_post=v_norm_ffn_post, w_in=v_w_in, w_out=v_w_out, sgu_w=v_sgu_w,
              sgu_b=v_sgu_b, ssm_lam_re=v_ssm_lam_re, ssm_lam_im=v_ssm_lam_im, ssm_log_dt=v_ssm_log_dt,
              ssm_b_re=v_ssm_b_re, ssm_b_im=v_ssm_b_im, ssm_c_re=v_ssm_c_re, ssm_c_im=v_ssm_c_im, ssm_d=v_ssm_d,
              glu_w=v_glu_w, glu_b=v_glu_b, pool_w=v_pool_w, pool_scale=v_pool_scale, ffn_w_gate=v_ffn_w_gate,
              ffn_w_up=v_ffn_w_up, ffn_w_down=v_ffn_w_down)
    order = list(wts.keys())
    bl = x.shape[0]
    nseq = bl * N_DEV
    me = _my_index()
    chip = me // 2
    ncol = w_mod.shape[2]

    (c_all,) = _spread([(c, ALL7, 8)], "ag_c")
    nrow = nseq + 8
    c_rows = jnp.concatenate([c_all.reshape(nseq, D), c_ctx[None], jnp.zeros((7, D), F32)], axis=0)
    b_cols = lax.dynamic_slice_in_dim(b_mod, chip * ncol, ncol, axis=1)[:, None, :]
    mod_cols = _mod_fwd(c_rows, w_mod, b_cols, "mod_fwd")
    (mod4,) = _spread([(mod_cols, CHIPS3, 4)], "ag_mod")
    mods = jnp.transpose(mod4, (1, 2, 0, 3)).reshape(2, nrow, 6 * D)
    mods_local = jnp.concatenate([lax.dynamic_slice_in_dim(mods, me * bl, bl, axis=1), mods[:, nseq:nseq + 1],
                                  jnp.zeros((2, 8 - bl - 1, 6 * D), F32)], axis=1)

    shards = {}
    for i in range(2):
        for nme, s in zip(BIG_NAMES, [w_in[i].T, w_out[i], glu_w[i], ffn_w_gate[i].T, ffn_w_up[i].T, ffn_w_down[i]]):
            shards[(i, nme)] = s.astype(MXU_DTYPE)
    (win0,) = _comm([(shards[(0, "win_t")], "gather", CHIPS3, 4)], "ag_win0")
    weights = [{"win_t": win0.reshape(-1, D)}, {}]
    ffn_names = ("wg_t", "wu_t", "wd")
    late_w = {"ssm": [key for key in shards if key != (0, "win_t") and not (key[0] == 1 and key[1] in ffn_names)],
              "ffn": [(1, nme) for nme in ffn_names]}
    w_side = {key: _Side([(shards[k2], "gather", CHIPS3, 4) for k2 in late_w[key]]) for key in late_w}

    def filler(key):
        def w_fill(wls, gathered):
            for (i, nme), g in zip(late_w[key], gathered):
                wls[i][nme] = g.reshape(-1, g.shape[-1])
        return w_fill

    w_fill = {key: filler(key) for key in late_w}

    eighths = lambda g: g.reshape(8, g.shape[0] // 8, g.shape[1])
    early_g = [(1, k) for k in BIG_NAMES] + [(0, k) for k in BIG_NAMES if k != "win_t"]

    def g_side_fn(big1, big0):
        return _Side([(eighths((big1 if i == 1 else big0)[k]), "a2a", ALL7, 8) for i, k in early_g])

    params = [{k: wts[k][i] for k in SMALL_NAMES} for i in range(2)]
    loss_part, grad_x, bigs, smalls, dmods, early = _core(x, ctx, loss_target, mods_local, params, weights,
                                                           w_side, w_fill, g_side_fn)
    loss = lax.psum(loss_part, ("x", "y", "c"))

    dmod_local = jnp.stack([dmods[i].reshape(8, 6 * D) for i in range(2)])
    (dmod_all,) = _spread([(dmod_local, ALL7, 8)], "ag_dmod")
    dcols = lax.dynamic_slice_in_dim(dmod_all, chip * ncol, ncol, axis=3)
    dlat = jnp.transpose(dcols[:, :, :bl], (1, 0, 2, 3)).reshape(2, nseq, ncol)
    dlat = jnp.concatenate([dlat, jnp.zeros((2, 8, ncol), F32)], axis=1)
    dctx8 = dcols[:, :, bl:bl + 1]
    g_w_mod, gc_part = _mod_bwd(c_rows, w_mod, dlat, dctx8, "mod_bwd")
    (gc4,) = _spread([(gc_part, CHIPS3, 4)], "ag_cctx")
    g_b_mod, g_c_ctx = _bmod_cctx(dmod_all, gc4, c_ctx[None], "bmod_cctx")

    small_flat = jnp.concatenate([jnp.stack([smalls[i][k] for i in range(2)]).reshape(-1) for k in SMALL_NAMES])
    npad = (-small_flat.shape[0]) % (8 * 1024)
    small_flat = jnp.concatenate([small_flat, jnp.zeros((npad,), F32)])
    late = _comm([(eighths(bigs[0]["win_t"]), "a2a", ALL7, 8), (small_flat.reshape(8, -1, 1024), "a2a", ALL7, 8)],
                 "a2a_grads")
    sums = _sum8(list(early) + list(late), "gsum")
    fin = _spread([(s, (1,), 2) for s in sums[:-1]] + [(sums[-1], ALL7, 8)], "ag_grads")
    big_g = [{}, {}]
    for (i, k), g in zip(early_g + [(0, "win_t")], fin[:-1]):
        big_g[i][k] = g.reshape(-1, g.shape[-1])
    small_red = fin[-1].reshape(-1)

    grads = {}
    off = 0
    for k in SMALL_NAMES:
        shp = wts[k].shape
        nel = int(np.prod(shp))
        grads[k] = small_red[off:off + nel].reshape(shp)
        off += nel
    grads["c_ctx"] = g_c_ctx[0]
    grads["w_mod"] = g_w_mod
    grads["b_mod"] = g_b_mod
    grads["w_in"] = jnp.stack([big_g[i]["win_t"].T for i in range(2)])
    grads["w_out"] = jnp.stack([big_g[i]["wout"] for i in range(2)])
    grads["glu_w"] = jnp.stack([big_g[i]["wglu"] for i in range(2)])
    grads["ffn_w_gate"] = jnp.stack([big_g[i]["wg_t"].T for i in range(2)])
    grads["ffn_w_up"] = jnp.stack([big_g[i]["wu_t"].T for i in range(2)])
    grads["ffn_w_down"] = jnp.stack([big_g[i]["wd"] for i in range(2)])

    deltas, new_m, new_v = {}, {}, {}
    for k in order:
        deltas[k], new_m[k], new_v[k] = _adam(wts[k], grads[k], ms[k], vs[k], "adam_" + k)
    return (loss, grad_x, *[grads[k] for k in order], *[deltas[k] for k in order],
            *[new_m[k] for k in order], *[new_v[k] for k in order])
```

```python
import functools
import math

import numpy as np
import jax
import jax.numpy as jnp
from jax import lax
from jax.experimental import pallas as pl
from jax.experimental.pallas import tpu as pltpu

F32 = jnp.float32
BF16 = jnp.bfloat16
MXU_DTYPE = jnp.bfloat16
MCAT_A, MCAT_C = 2, 3
WOUT_PERM, WOUT_INV = (1, 2, 0, 3), (2, 0, 1, 3)

D = 1024
EPS = 1e-6
TB = 256
CTX = 256
CHUNK = 128
GRID_W = 64
A_W, B_W, C_W = 256, 512, 256
D_IN = 1280
D_FF = 2816
SSM_G, SSM_P, SSM_H = 32, 64, 16
ST = 64
POOL_WINDOWS = (2, 4, 8, 16)
N_DEV = 8
VMEM_LIMIT = 52 * 1024 * 1024
GELU_C = math.sqrt(2.0 / math.pi)

ADAM_LR, ADAM_B1, ADAM_B2, ADAM_EPS, ADAM_WD, ADAM_STEP = 0.001, 0.9, 0.999, 1e-08, 0.01, 10


def _cp(sem=None, vmem=VMEM_LIMIT, **kw):
    return pltpu.CompilerParams(dimension_semantics=sem, vmem_limit_bytes=vmem, **kw)


def _pick(n, cap):
    if n <= cap:
        return n
    best = None
    for t in range(128, cap + 1, 128):
        if n % t == 0:
            best = t
    assert best is not None, (n, cap)
    return best


def _gelu(x):
    return 0.5 * x * (1.0 + jnp.tanh(GELU_C * (x + 0.044715 * x * x * x)))


def _gelu_grad(x):
    t = jnp.tanh(GELU_C * (x + 0.044715 * x * x * x))
    return 0.5 * (1.0 + t) + 0.5 * x * (1.0 - t * t) * GELU_C * (1.0 + 3.0 * 0.044715 * x * x)


def _sigmoid(x):
    return 1.0 / (1.0 + jnp.exp(-x))


def _dot(a, b, dims):
    return lax.dot_general(a, b, (dims, ((), ())), preferred_element_type=F32)


def _nn(a, b):
    return _dot(a, b, ((1,), (0,)))


def _nt(a, b):
    return _dot(a, b, ((1,), (1,)))


def _tn(a, b):
    return _dot(a, b, ((0,), (0,)))


def _mm(pairs, nt, out_dtype, name, tm=512, side=None):
    m = pairs[0][0].shape[0]
    n = pairs[0][1].shape[0] if nt else pairs[0][1].shape[1]
    tn = _pick(n, 1408)
    tm = min(tm, m)
    npairs = len(pairs)
    ni, nj = m // tm, n // tn

    def body(*refs):
        own, srefs = _side_split(refs, 2 * npairs, 1, 0, side)
        o_ref = own[-1]
        i, j = pl.program_id(0), pl.program_id(1)
        _side_start(side, srefs, jnp.logical_and(i == 0, j == 0))
        acc = None
        for t in range(npairs):
            a = own[2 * t][...].astype(MXU_DTYPE)
            b = own[2 * t + 1][...].astype(MXU_DTYPE)
            r = _nt(a, b) if nt else _nn(a, b)
            acc = r if acc is None else acc + r
        o_ref[...] = acc.astype(o_ref.dtype)
        _side_wait(side, srefs, jnp.logical_and(i == ni - 1, j == nj - 1))

    sd = side if side is not None else _Side([])
    in_specs, flat = [], []
    for a, b in pairs:
        k = a.shape[1]
        in_specs.append(pl.BlockSpec((tm, k), lambda i, j: (i, 0)))
        in_specs.append(pl.BlockSpec((tn, k), lambda i, j: (j, 0)) if nt else pl.BlockSpec((k, tn), lambda i, j: (0, j)))
        flat += [a, b]
    outs = pl.pallas_call(
        body, grid=(ni, nj), in_specs=in_specs + sd.in_specs,
        out_specs=[pl.BlockSpec((tm, tn), lambda i, j: (i, j))] + sd.out_specs,
        out_shape=[jax.ShapeDtypeStruct((m, n), out_dtype)] + sd.out_shape,
        scratch_shapes=sd.scratch if side is not None else [],
        compiler_params=_cp(("arbitrary", "arbitrary") if side is not None else ("parallel", "parallel")),
        name=name)(*flat, *sd.arrays)
    return outs[0] if side is None else (outs[0], list(outs[1:]))


def _mm_tn(a, b, out_dtype, name, tm=512):
    m, k1 = a.shape
    n = b.shape[1]
    t1 = _pick(k1, 1408)
    tn = _pick(n, 1024)
    tm = min(tm, m)
    nsteps = m // tm

    def body(a_ref, b_ref, o_ref, acc_ref):
        t = pl.program_id(2)

        @pl.when(t == 0)
        def _():
            acc_ref[...] = jnp.zeros_like(acc_ref)

        acc_ref[...] += _tn(a_ref[...].astype(MXU_DTYPE), b_ref[...].astype(MXU_DTYPE))

        @pl.when(t == nsteps - 1)
        def _():
            o_ref[...] = acc_ref[...].astype(o_ref.dtype)

    return pl.pallas_call(
        body, grid=(k1 // t1, n // tn, nsteps),
        in_specs=[pl.BlockSpec((tm, t1), lambda i, j, t: (t, i)), pl.BlockSpec((tm, tn), lambda i, j, t: (t, j))],
        out_specs=pl.BlockSpec((t1, tn), lambda i, j, t: (i, j)),
        out_shape=jax.ShapeDtypeStruct((k1, n), out_dtype),
        scratch_shapes=[pltpu.VMEM((t1, tn), F32)],
        compiler_params=_cp(("parallel", "parallel", "arbitrary")), name=name)(a, b)


class _Layout:
    def __init__(self, bl, lat):
        self.bl, self.lat = bl, lat
        self.nlb = lat // TB
        self.nr = 1 + self.nlb
        self.nctx = bl
        self.nb = self.nr * bl
        self.nt = self.nb * TB
        self.ctx_row = bl

    def blk(self, g):
        gg = g - self.bl
        return jnp.where(g < self.bl, g, (gg % self.nlb + 1) * self.bl + gg // self.nlb)

    def modrow(self, g):
        return jnp.where(g < self.bl, self.ctx_row, (g - self.bl) // self.nlb)

    def first_of_row(self, g):
        return jnp.logical_or(g == 0, jnp.logical_and(g >= self.bl, (g - self.bl) % self.nlb == 0))

    def modrows_static(self):
        return np.array([self.ctx_row if j < self.bl else j % self.bl for j in range(self.nb)], np.int32)


def _tok_spec(lay):
    return pl.BlockSpec((TB, D), lambda g: (lay.blk(g), 0))


def _vec_spec():
    return pl.BlockSpec((1, D), lambda j: (0, 0))


def _mod_spec(lay, k):
    return pl.BlockSpec((1, 1, D), lambda g: (lay.blk(g) * 6 + k, 0, 0))


def _embed(lay, x2d, ctx2d, pe):
    bl, nlb = lay.bl, lay.nlb

    def body(x_ref, c_ref, pe_ref, o_ref):
        j = pl.program_id(0)

        @pl.when(j < bl)
        def _():
            o_ref[...] = c_ref[...]

        @pl.when(j >= bl)
        def _():
            o_ref[...] = x_ref[...] + pe_ref[...]

    pos = lambda j: jnp.maximum(j // bl - 1, 0)
    return pl.pallas_call(
        body, grid=(lay.nb,),
        in_specs=[pl.BlockSpec((TB, D), lambda j: ((j % bl) * nlb + pos(j), 0)),
                  pl.BlockSpec((TB, D), lambda j: (jnp.minimum(j, bl - 1), 0)),
                  pl.BlockSpec((TB, D), lambda j: (pos(j), 0))],
        out_specs=pl.BlockSpec((TB, D), lambda j: (j, 0)), out_shape=jax.ShapeDtypeStruct((lay.nt, D), F32),
        compiler_params=_cp(("parallel",)), name="embed")(x2d, ctx2d, pe)


def _normmod_fwd(lay, x, gain, modarr, ksh, ksc, name):
    def body(x_ref, g_ref, sh_ref, sc_ref, o_ref):
        xv = x_ref[...]
        r = lax.rsqrt(jnp.mean(xv * xv, axis=-1, keepdims=True) + EPS)
        o_ref[...] = ((xv * r * g_ref[...]) * (1.0 + sc_ref[0]) + sh_ref[0]).astype(o_ref.dtype)

    return pl.pallas_call(
        body, grid=(lay.nb,), in_specs=[_tok_spec(lay), _vec_spec(), _mod_spec(lay, ksh), _mod_spec(lay, ksc)],
        out_specs=_tok_spec(lay), out_shape=jax.ShapeDtypeStruct((lay.nt, D), MXU_DTYPE),
        compiler_params=_cp(("parallel",)), name=name)(x, gain, modarr, modarr)


def _acc_specs(lay):
    row = pl.BlockSpec((1, 1, D), lambda j: (lay.modrow(j), 0, 0))
    return row, jax.ShapeDtypeStruct((8, 1, D), F32)


def _normmod_bwd(lay, x, dh, dx_in, gain, modarr, ksc, name, latent_only=False):
    row_spec, row_shape = _acc_specs(lay)
    if latent_only:
        dx_spec = pl.BlockSpec((TB, D), lambda g: (jnp.maximum(g - lay.bl, 0), 0))
        dx_shape = jax.ShapeDtypeStruct((lay.bl * lay.lat, D), F32)
    else:
        dx_spec, dx_shape = _tok_spec(lay), jax.ShapeDtypeStruct((lay.nt, D), F32)

    def body(x_ref, dh_ref, dxi_ref, g_ref, sc_ref, dx_ref, dsh_ref, dsc_ref, dg_ref):
        j = pl.program_id(0)
        xv = x_ref[...]
        dhv = dh_ref[...].astype(F32)
        g = g_ref[...]
        sc1 = 1.0 + sc_ref[0]
        r = lax.rsqrt(jnp.mean(xv * xv, axis=-1, keepdims=True) + EPS)
        xh = xv * r
        dxh = dhv * (g * sc1)
        dx = r * (dxh - xh * jnp.mean(dxh * xh, axis=-1, keepdims=True))
        dx_ref[...] = dxi_ref[...] + dx

        @pl.when(lay.first_of_row(j))
        def _():
            dsh_ref[...] = jnp.zeros_like(dsh_ref)
            dsc_ref[...] = jnp.zeros_like(dsc_ref)

        @pl.when(j == 0)
        def _():
            dg_ref[...] = jnp.zeros_like(dg_ref)

        dsh_ref[0] += jnp.sum(dhv, axis=0, keepdims=True)
        dsc_ref[0] += jnp.sum(dhv * (xh * g), axis=0, keepdims=True)
        dg_ref[...] += jnp.sum(dhv * sc1 * xh, axis=0, keepdims=True)

    return pl.pallas_call(
        body, grid=(lay.nb,),
        in_specs=[_tok_spec(lay), _tok_spec(lay), _tok_spec(lay), _vec_spec(), _mod_spec(lay, ksc)],
        out_specs=[dx_spec, row_spec, row_spec, _vec_spec()],
        out_shape=[dx_shape, row_shape, row_shape, jax.ShapeDtypeStruct((1, D), F32)],
        compiler_params=_cp(("arbitrary",)), name=name)(x, dh, dx_in, gain, modarr)


def _resnorm_fwd(lay, x, m, gain, modarr, kgate, name):
    def body(x_ref, m_ref, g_ref, gate_ref, o_ref):
        mv = m_ref[...].astype(F32)
        r = lax.rsqrt(jnp.mean(mv * mv, axis=-1, keepdims=True) + EPS)
        o_ref[...] = x_ref[...] + gate_ref[0] * (mv * r * g_ref[...])

    return pl.pallas_call(
        body, grid=(lay.nb,), in_specs=[_tok_spec(lay), _tok_spec(lay), _vec_spec(), _mod_spec(lay, kgate)],
        out_specs=_tok_spec(lay), out_shape=jax.ShapeDtypeStruct((lay.nt, D), F32),
        compiler_params=_cp(("parallel",)), name=name)(x, m, gain, modarr)


def _resnorm_bwd(lay, dxn, m, gain, modarr, kgate, name):
    row_spec, row_shape = _acc_specs(lay)

    def body(d_ref, m_ref, g_ref, gate_ref, dm_ref, dgate_ref, dg_ref):
        j = pl.program_id(0)
        dv = d_ref[...]
        mv = m_ref[...].astype(F32)
        g = g_ref[...]
        r = lax.rsqrt(jnp.mean(mv * mv, axis=-1, keepdims=True) + EPS)
        xh = mv * r
        dy = dv * gate_ref[0]
        dxh = dy * g
        dm_ref[...] = (r * (dxh - xh * jnp.mean(dxh * xh, axis=-1, keepdims=True))).astype(dm_ref.dtype)

        @pl.when(lay.first_of_row(j))
        def _():
            dgate_ref[...] = jnp.zeros_like(dgate_ref)

        @pl.when(j == 0)
        def _():
            dg_ref[...] = jnp.zeros_like(dg_ref)

        dgate_ref[0] += jnp.sum(dv * (xh * g), axis=0, keepdims=True)
        dg_ref[...] += jnp.sum(dy * xh, axis=0, keepdims=True)

    return pl.pallas_call(
        body, grid=(lay.nb,), in_specs=[_tok_spec(lay), _tok_spec(lay), _vec_spec(), _mod_spec(lay, kgate)],
        out_specs=[_tok_spec(lay), row_spec, _vec_spec()],
        out_shape=[jax.ShapeDtypeStruct((lay.nt, D), MXU_DTYPE), row_shape, jax.ShapeDtypeStruct((1, D), F32)],
        compiler_params=_cp(("arbitrary",)), name=name)(dxn, m, gain, modarr)


def _loss_bwd(lay, xf, tgt2d):
    bl, nlb = lay.bl, lay.nlb

    def body(x_ref, t_ref, dx_ref, l_ref):
        j = pl.program_id(0)

        @pl.when(j == 0)
        def _():
            l_ref[...] = jnp.zeros_like(l_ref)

        @pl.when(j < bl)
        def _():
            dx_ref[...] = jnp.zeros_like(dx_ref)

        @pl.when(j >= bl)
        def _():
            e = x_ref[...] - t_ref[...]
            dx_ref[...] = e * (1.0 / D)
            l_ref[...] += jnp.sum(e * e) * (0.5 / D)

    tok = pl.BlockSpec((TB, D), lambda j: (j, 0))
    return pl.pallas_call(
        body, grid=(lay.nb,),
        in_specs=[tok, pl.BlockSpec((TB, D), lambda j: ((j % bl) * nlb + jnp.maximum(j // bl - 1, 0), 0))],
        out_specs=[tok, pl.BlockSpec((8, 128), lambda j: (0, 0))],
        out_shape=[jax.ShapeDtypeStruct((lay.nt, D), F32), jax.ShapeDtypeStruct((8, 128), F32)],
        compiler_params=_cp(("arbitrary",)), name="loss")(xf, tgt2d)


FF_TN = D_FF // 2
FF_CHUNKS = ((0, 512), (512, 512), (1024, 384))


def _ffn_up(h, wgt, wut, name, side=None):
    m = h.shape[0]
    tm, tn = min(512, m), FF_TN
    ni, nj = m // tm, D_FF // tn

    def body(*refs):
        (h_ref, wg_ref, wu_ref, g_ref, u_ref, a_ref), srefs = _side_split(refs, 3, 3, 0, side)
        j, i = pl.program_id(0), pl.program_id(1)
        _side_start(side, srefs, jnp.logical_and(i == 0, j == 0))
        hv = h_ref[...]
        for c0, cw in FF_CHUNKS:
            g = _nt(hv, wg_ref[c0:c0 + cw, :])
            u = _nt(hv, wu_ref[c0:c0 + cw, :])
            g_ref[:, c0:c0 + cw] = g.astype(g_ref.dtype)
            u_ref[:, c0:c0 + cw] = u.astype(u_ref.dtype)
            a_ref[:, c0:c0 + cw] = (g * _sigmoid(g) * u).astype(a_ref.dtype)
        _side_wait(side, srefs, jnp.logical_and(i == ni - 1, j == nj - 1))

    sd = side if side is not None else _Side([])
    osp = pl.BlockSpec((tm, tn), lambda j, i: (i, j))
    osh = jax.ShapeDtypeStruct((m, D_FF), MXU_DTYPE)
    outs = pl.pallas_call(
        body, grid=(nj, ni),
        in_specs=[pl.BlockSpec((tm, D), lambda j, i: (i, 0)), pl.BlockSpec((tn, D), lambda j, i: (j, 0)),
                  pl.BlockSpec((tn, D), lambda j, i: (j, 0))] + sd.in_specs,
        out_specs=[osp, osp, osp] + sd.out_specs, out_shape=[osh, osh, osh] + sd.out_shape,
        scratch_shapes=sd.scratch if side is not None else [],
        compiler_params=_cp(("arbitrary", "arbitrary") if side is not None else ("parallel", "parallel")),
        name=name)(h, wgt, wut, *sd.arrays)
    return outs[0], outs[1], outs[2], list(outs[3:])


def _ffn_down_bwd(df, wd, g, u, name, side=None):
    m = df.shape[0]
    tm, tn = min(512, m), FF_TN
    ni, nj = m // tm, D_FF // tn

    def body(*refs):
        (df_ref, wd_ref, g_ref, u_ref, dg_ref, du_ref), srefs = _side_split(refs, 4, 2, 0, side)
        j, i = pl.program_id(0), pl.program_id(1)
        _side_start(side, srefs, jnp.logical_and(i == 0, j == 0))
        dfv = df_ref[...]
        for c0, cw in FF_CHUNKS:
            da = _nt(dfv, wd_ref[c0:c0 + cw, :])
            gv = g_ref[:, c0:c0 + cw].astype(F32)
            uv = u_ref[:, c0:c0 + cw].astype(F32)
            s = _sigmoid(gv)
            dg_ref[:, c0:c0 + cw] = (da * uv * (s * (1.0 + gv * (1.0 - s)))).astype(dg_ref.dtype)
            du_ref[:, c0:c0 + cw] = (da * gv * s).astype(du_ref.dtype)
        _side_wait(side, srefs, jnp.logical_and(i == ni - 1, j == nj - 1))

    sd = side if side is not None else _Side([])
    osp = pl.BlockSpec((tm, tn), lambda j, i: (i, j))
    osh = jax.ShapeDtypeStruct((m, D_FF), MXU_DTYPE)
    outs = pl.pallas_call(
        body, grid=(nj, ni),
        in_specs=[pl.BlockSpec((tm, D), lambda j, i: (i, 0)), pl.BlockSpec((tn, D), lambda j, i: (j, 0)), osp, osp]
        + sd.in_specs,
        out_specs=[osp, osp] + sd.out_specs, out_shape=[osh, osh] + sd.out_shape,
        scratch_shapes=sd.scratch if side is not None else [],
        compiler_params=_cp(("arbitrary", "arbitrary") if side is not None else ("parallel", "parallel")),
        name=name)(df, wd, g, u, *sd.arrays)
    return outs[0], outs[1], list(outs[2:])


def _head_masks(shape):
    lane = lax.broadcasted_iota(jnp.int32, shape, 1)
    return [jnp.logical_and(lane >= 64 * h, lane < 64 * h + 64) for h in range(4)]


def _head_mean(x, masks):
    out = jnp.zeros_like(x)
    for mk in masks:
        s = jnp.sum(jnp.where(mk, x, 0.0), axis=-1, keepdims=True) * (1.0 / 64.0)
        out = jnp.where(mk, s, out)
    return out


def _gate_common(z, masks):
    zg = _gelu(z)
    u = zg[:, :A_W]
    v = zg[:, A_W:]
    mu = _head_mean(v, masks)
    vc = v - mu
    rstd = lax.rsqrt(_head_mean(vc * vc, masks) + EPS)
    return u, vc * rstd, rstd


def _gate_s(vn, ws_ref, bias, masks):
    parts = []
    for c in range(TB // CHUNK):
        vc = vn[c * CHUNK:(c + 1) * CHUNK]
        s = bias
        for h in range(4):
            s = s + _nn(ws_ref[h], jnp.where(masks[h][:CHUNK], vc, 0.0).astype(MXU_DTYPE))
        parts.append(s)
    return jnp.concatenate(parts, axis=0)


def _gate_fwd(lay, z, ws, bias, name):
    def body(z_ref, ws_ref, b_ref, o_ref):
        masks = _head_masks((TB, A_W))
        u, vn, _ = _gate_common(z_ref[...].astype(F32), masks)
        o_ref[...] = (u * _gate_s(vn, ws_ref, b_ref[...], masks)).astype(o_ref.dtype)

    return pl.pallas_call(
        body, grid=(lay.nb,),
        in_specs=[pl.BlockSpec((TB, 2 * A_W), lambda j: (j, 0)), pl.BlockSpec((4, CHUNK, CHUNK), lambda j: (0, 0, 0)),
                  pl.BlockSpec((CHUNK, A_W), lambda j: (0, 0))],
        out_specs=pl.BlockSpec((TB, A_W), lambda j: (j, 0)),
        out_shape=jax.ShapeDtypeStruct((lay.nt, A_W), MXU_DTYPE),
        compiler_params=_cp(("parallel",)), name=name)(z, ws, bias)


def _gate_bwd(lay, z, da, ws, wst, bias, name):
    def body(z_ref, da_ref, ws_ref, wst_ref, b_ref, dz_ref, dws_ref, db_ref):
        j = pl.program_id(0)

        @pl.when(j == 0)
        def _():
            dws_ref[...] = jnp.zeros_like(dws_ref)
            db_ref[...] = jnp.zeros_like(db_ref)

        masks = _head_masks((TB, A_W))
        zv = z_ref[...].astype(F32)
        u, vn, rstd = _gate_common(zv, masks)
        s = _gate_s(vn, ws_ref, b_ref[...], masks)
        dav = da_ref[...].astype(F32)
        du = dav * s
        ds = dav * u
        dvn_parts = []
        for c in range(TB // CHUNK):
            sl = slice(c * CHUNK, (c + 1) * CHUNK)
            ds_c = ds[sl]
            vn_c = vn[sl].astype(MXU_DTYPE)
            db_ref[...] += ds_c
            ds_b = ds_c.astype(MXU_DTYPE)
            dvn_c = jnp.zeros((CHUNK, A_W), F32)
            for h in range(4):
                mk = masks[h][:CHUNK]
                dws_ref[h] += _nt(jnp.where(mk, ds_c, 0.0).astype(MXU_DTYPE), vn_c)
                dvn_c = dvn_c + jnp.where(mk, _nn(wst_ref[h], ds_b), 0.0)
            dvn_parts.append(dvn_c)
        dvn = jnp.concatenate(dvn_parts, axis=0)
        dv = rstd * (dvn - _head_mean(dvn, masks) - vn * _head_mean(dvn * vn, masks))
        gg = _gelu_grad(zv)
        dz_ref[:, :A_W] = (du * gg[:, :A_W]).astype(dz_ref.dtype)
        dz_ref[:, A_W:] = (dv * gg[:, A_W:]).astype(dz_ref.dtype)

    return pl.pallas_call(
        body, grid=(lay.nb,),
        in_specs=[pl.BlockSpec((TB, 2 * A_W), lambda j: (j, 0)), pl.BlockSpec((TB, A_W), lambda j: (j, MCAT_A)),
                  pl.BlockSpec((4, CHUNK, CHUNK), lambda j: (0, 0, 0)), pl.BlockSpec((4, CHUNK, CHUNK), lambda j: (0, 0, 0)),
                  pl.BlockSpec((CHUNK, A_W), lambda j: (0, 0))],
        out_specs=[pl.BlockSpec((TB, 2 * A_W), lambda j: (j, 0)), pl.BlockSpec((4, CHUNK, CHUNK), lambda j: (0, 0, 0)),
                   pl.BlockSpec((CHUNK, A_W), lambda j: (0, 0))],
        out_shape=[jax.ShapeDtypeStruct((lay.nt, 2 * A_W), MXU_DTYPE), jax.ShapeDtypeStruct((4, CHUNK, CHUNK), F32),
                   jax.ShapeDtypeStruct((CHUNK, A_W), F32)],
        compiler_params=_cp(("arbitrary",)), name=name)(z, da, ws, wst, bias)


def _band_constants():
    bands = np.zeros((2, 4, TB, TB), np.float32)
    inv = np.zeros((2, 4, TB, 1), np.float32)
    for kind, n in ((0, GRID_W), (1, TB)):
        for i, w in enumerate(POOL_WINDOWS):
            for t in range(TB):
                base, tt = (t // n) * n, t % n
                lo = min(max(tt - w // 2, 0), n)
                hi = min(max(tt - w // 2 + w, 0), n)
                bands[kind, i, t, base + lo:base + hi] = 1.0
                inv[kind, i, t, 0] = 1.0 / (hi - lo)
    return bands, inv


def _split3(x):
    a = x.astype(MXU_DTYPE)
    r1 = x - a.astype(F32)
    b = r1.astype(MXU_DTYPE)
    c = (r1 - b.astype(F32)).astype(MXU_DTYPE)
    return a, b, c


def _window_apply(band_ref, inv_ref, x, masks, transpose, mxu_exact=False):
    out = jnp.zeros_like(x)
    for i in range(4):
        xi = x * inv_ref[0, i] if transpose else x
        acc = None
        for part in ((xi.astype(MXU_DTYPE),) if mxu_exact else _split3(xi)):
            r = _tn(band_ref[0, i], part) if transpose else _nn(band_ref[0, i], part)
            acc = r if acc is None else acc + r
        if not transpose:
            acc = acc * inv_ref[0, i]
        out = jnp.where(masks[i], acc, out)
    return out


def _pool_specs(lay):
    kind = lambda j: jnp.where(j < lay.nctx, 1, 0)
    return [pl.BlockSpec((1, 4, TB, TB), lambda j: (kind(j), 0, 0, 0)), pl.BlockSpec((1, 4, TB, 1), lambda j: (kind(j), 0, 0, 0))]


def _pool_fwd(lay, z, bands, inv, pw, scale, name):
    def body(p_ref, band_ref, inv_ref, pw_ref, sc_ref, o_ref):
        masks = _head_masks((TB, C_W))
        p = p_ref[...].astype(F32)
        diff = _window_apply(band_ref, inv_ref, p, masks, False, mxu_exact=True) - p
        o_ref[...] = (_nn(diff.astype(MXU_DTYPE), pw_ref[...]) * sc_ref[...]).astype(o_ref.dtype)

    return pl.pallas_call(
        body, grid=(lay.nb,),
        in_specs=[pl.BlockSpec((TB, C_W), lambda j: (j, 4))] + _pool_specs(lay)
        + [pl.BlockSpec((C_W, C_W), lambda j: (0, 0)), pl.BlockSpec((1, C_W), lambda j: (0, 0))],
        out_specs=pl.BlockSpec((TB, C_W), lambda j: (j, 0)),
        out_shape=jax.ShapeDtypeStruct((lay.nt, C_W), MXU_DTYPE),
        compiler_params=_cp(("parallel",)), name=name)(z, bands, inv, pw, scale)


def _pool_bwd(lay, z, dc, bands, inv, pw, scale, name):
    def body(p_ref, dc_ref, band_ref, inv_ref, pw_ref, sc_ref, dp_ref, dpw_ref, dsc_ref):
        j = pl.program_id(0)

        @pl.when(j == 0)
        def _():
            dpw_ref[...] = jnp.zeros_like(dpw_ref)
            dsc_ref[...] = jnp.zeros_like(dsc_ref)

        masks = _head_masks((TB, C_W))
        p = p_ref[...].astype(F32)
        dcv = dc_ref[...].astype(F32)
        diff = _window_apply(band_ref, inv_ref, p, masks, False, mxu_exact=True) - p
        diff_b = diff.astype(MXU_DTYPE)
        pre = _nn(diff_b, pw_ref[...])
        dsc_ref[...] += jnp.sum(dcv * pre, axis=0, keepdims=True)
        dpre = dcv * sc_ref[...]
        dpre_b = dpre.astype(MXU_DTYPE)
        dpw_ref[...] += _tn(diff_b, dpre_b)
        ddiff = _nt(dpre_b, pw_ref[...])
        dp_ref[...] = (_window_apply(band_ref, inv_ref, ddiff, masks, True) - ddiff).astype(dp_ref.dtype)

    return pl.pallas_call(
        body, grid=(lay.nb,),
        in_specs=[pl.BlockSpec((TB, C_W), lambda j: (j, 4)), pl.BlockSpec((TB, C_W), lambda j: (j, MCAT_C))] + _pool_specs(lay)
        + [pl.BlockSpec((C_W, C_W), lambda j: (0, 0)), pl.BlockSpec((1, C_W), lambda j: (0, 0))],
        out_specs=[pl.BlockSpec((TB, C_W), lambda j: (j, 0)), pl.BlockSpec((C_W, C_W), lambda j: (0, 0)),
                   pl.BlockSpec((1, C_W), lambda j: (0, 0))],
        out_shape=[jax.ShapeDtypeStruct((lay.nt, C_W), MXU_DTYPE), jax.ShapeDtypeStruct((C_W, C_W), F32),
                   jax.ShapeDtypeStruct((1, C_W), F32)],
        compiler_params=_cp(("arbitrary",)), name=name)(z, dc, bands, inv, pw, scale)


def _disc_math(lr, li, ldt, br, bi):
    dt = jnp.exp(ldt)
    e = jnp.exp(lr * dt)
    ar = e * jnp.cos(li * dt)
    ai = e * jnp.sin(li * dt)
    nr, ni = ar - 1.0, ai
    den = lr * lr + li * li
    qr = (nr * lr + ni * li) / den
    qi = (ni * lr - nr * li) / den
    return ar, ai, qr * br - qi * bi, qr * bi + qi * br


def _disc_fwd(lrx, lix, ldtx, brt, bit, name):
    def body(lr_ref, li_ref, ldt_ref, br_ref, bi_ref, ar_ref, ai_ref, obr_ref, obi_ref):
        ar, ai, obr, obi = _disc_math(lr_ref[...], li_ref[...], ldt_ref[...], br_ref[...], bi_ref[...])
        ar_ref[...] = ar
        ai_ref[...] = ai
        obr_ref[...] = obr
        obi_ref[...] = obi

    sh = jax.ShapeDtypeStruct(lrx.shape, F32)
    return pl.pallas_call(body, out_shape=[sh, sh, sh, sh], name=name)(lrx, lix, ldtx, brt, bit)


def _disc_bwd(lrx, lix, ldtx, brt, bit, dar, dai, dbr, dbi, name):
    nrow = lrx.shape[0] // SSM_H

    def body(lr_ref, li_ref, ldt_ref, br_ref, bi_ref, dar_ref, dai_ref, dbr_ref, dbi_ref,
             glr_ref, gli_ref, gdt_ref, gbr_ref, gbi_ref):
        _, vjp = jax.vjp(_disc_math, lr_ref[...], li_ref[...], ldt_ref[...], br_ref[...], bi_ref[...])
        glr, gli, gdt, gbr, gbi = vjp((dar_ref[...], dai_ref[...], dbr_ref[...], dbi_ref[...]))
        glr_ref[...] = jnp.sum(glr.reshape(nrow, SSM_H, SSM_P), axis=1)
        gli_ref[...] = jnp.sum(gli.reshape(nrow, SSM_H, SSM_P), axis=1)
        gdt_ref[...] = jnp.sum(jnp.sum(gdt.reshape(nrow, SSM_H, SSM_P), axis=1), axis=-1, keepdims=True)
        gbr_ref[...] = gbr
        gbi_ref[...] = gbi

    small = jax.ShapeDtypeStruct((nrow, SSM_P), F32)
    big = jax.ShapeDtypeStruct(lrx.shape, F32)
    return pl.pallas_call(body, out_shape=[small, small, jax.ShapeDtypeStruct((nrow, 1), F32), big, big],
                          name=name)(lrx, lix, ldtx, brt, bit, dar, dai, dbr, dbi)


HS = 1024
GQ, QC, QS = 8, 128, 512
LC = QS
SCAN_UNROLL = ST


def _scan_steps(step, carry):
    if SCAN_UNROLL >= ST:
        for s in range(ST):
            carry = step(s, carry)
        return carry

    def body(i, c):
        for j in range(SCAN_UNROLL):
            c = step(i * SCAN_UNROLL + j, c)
        return c

    return lax.fori_loop(0, ST // SCAN_UNROLL, body, carry)


def _tile_row(s):
    return s * 8 if isinstance(s, int) else pl.multiple_of(s * 8, 8)


def _dir_cat(x, d0, qq):
    xq = x[:, QC * qq:QC * qq + QC]
    zero = jnp.zeros_like(xq)
    return jnp.concatenate([jnp.where(d0, xq, zero), jnp.where(d0, zero, xq)], axis=1)


def _dir_pick(x, d0):
    return jnp.where(d0, x[:, :QC], x[:, QC:])


def _d0_rows(n):
    row = lax.broadcasted_iota(jnp.int32, (n, 1), 0)
    return jnp.bitwise_and(row, 4) == 0


def _scan_perm(bl):
    n = 2 * bl * ST
    p = np.zeros((n, n), np.float32)
    for s in range(ST):
        for d in range(2):
            for b in range(bl):
                t = s if d == 0 else ST - 1 - s
                p[s * 2 * bl + d * bl + b, d * bl * ST + b * ST + t] = 1.0
    return p


def _scan_maps(lay):
    spc = TB // ST
    nlc = lay.nlb * spc

    def fwd(k):
        return k // spc, k % spc

    def rev(k):
        cpos = nlc - 1 - jnp.maximum(k - spc, 0)
        return jnp.where(k < spc, 0, 1 + cpos // spc), jnp.where(k < spc, spc - 1 - k, cpos % spc)

    return fwd, rev


def _pack_rows(f_ref, r_ref, p_ref, rc):
    st = jnp.concatenate([f_ref[0].reshape(rc // 2, 256), r_ref[0].reshape(rc // 2, 256)], axis=0).astype(MXU_DTYPE)
    return _nn(p_ref[...], st).astype(MXU_DTYPE)


def _side_split(refs, n_in, n_out, n_scr, side):
    ns = side.n if side is not None else 0
    ins, sin = refs[:n_in], refs[n_in:n_in + ns]
    o0 = n_in + ns
    outs, sout = refs[o0:o0 + n_out], refs[o0 + n_out:o0 + n_out + ns]
    s0 = o0 + n_out + ns
    return ins + outs + refs[s0:s0 + n_scr], (sin, sout, refs[s0 + n_scr:])


def _side_start(side, srefs, first):
    if side is not None:
        @pl.when(first)
        def _():
            side.start(*srefs)


def _side_wait(side, srefs, last):
    if side is not None:
        @pl.when(last)
        def _():
            side.wait(*srefs)


def _ssm_fwd(lay, z, perm, bh, ch, ar8, ai8, name, side=None):
    bl = lay.bl
    rc = ST * 2 * bl
    nch = lay.nr * (TB // ST)
    fwd, rev = _scan_maps(lay)
    z4 = z.reshape(lay.nr, bl, TB, z.shape[1])

    def body(*refs):
        own, srefs = _side_split(refs, 7, 3, 2, side)
        uf_ref, ur_ref, p_ref, bh_ref, ch_ref, ar_ref, ai_ref, yf_ref, yr_ref, hst_ref, hs, hc = own
        f, k = pl.program_id(0), pl.program_id(1)
        _side_start(side, srefs, jnp.logical_and(f == 0, k == 0))

        @pl.when(k == 0)
        def _():
            hc[...] = jnp.zeros_like(hc)

        hst_ref[0] = hc[...]
        d0 = _d0_rows(rc)
        uv = _pack_rows(uf_ref, ur_ref, p_ref, rc)
        for q in range(2):
            cr, ci = 2 * QS * q, 2 * QS * q + QS
            hs[:, cr:cr + 2 * QS] = _nn(_dir_cat(uv, d0, q), bh_ref[q])
            ar = ar_ref[:, QS * q:QS * q + QS]
            ai = ai_ref[:, QS * q:QS * q + QS]

            def step(s, carry, cr=cr, ci=ci, ar=ar, ai=ai):
                hr, hi = carry
                base = _tile_row(s)
                nr = ar * hr - ai * hi + hs[pl.ds(base, 8), cr:cr + LC]
                ni = ar * hi + ai * hr + hs[pl.ds(base, 8), ci:ci + LC]
                hs[pl.ds(base, 8), cr:cr + LC] = nr
                hs[pl.ds(base, 8), ci:ci + LC] = ni
                return nr, ni

            hr, hi = _scan_steps(step, (hc[:, cr:cr + LC], hc[:, ci:ci + LC]))
            hc[:, cr:cr + LC] = hr
            hc[:, ci:ci + LC] = hi
        yi = jnp.concatenate(
            [_dir_pick(_nn(hs[:, 2 * QS * q:2 * QS * (q + 1)].astype(MXU_DTYPE), ch_ref[q]), d0) for q in range(2)], axis=1)
        yd = _tn(p_ref[...], yi.astype(MXU_DTYPE))
        yf_ref[0] = yd[:rc // 2].reshape(bl, ST, 256).astype(yf_ref.dtype)
        yr_ref[0] = yd[rc // 2:].reshape(bl, ST, 256).astype(yr_ref.dtype)
        _side_wait(side, srefs, jnp.logical_and(f == 1, k == nch - 1))

    sd = side if side is not None else _Side([])
    blk = (1, bl, ST, 256)
    ysh = jax.ShapeDtypeStruct((lay.nr, bl, TB, B_W), MXU_DTYPE)
    outs = pl.pallas_call(
        body, grid=(2, nch),
        in_specs=[pl.BlockSpec(blk, lambda f, k: (fwd(k)[0], 0, fwd(k)[1], 2 + f)),
                  pl.BlockSpec(blk, lambda f, k: (rev(k)[0], 0, rev(k)[1], 2 + f)),
                  pl.BlockSpec((rc, rc), lambda f, k: (0, 0)),
                  pl.BlockSpec((2, 2 * QC, 2 * QS), lambda f, k: (f, 0, 0)),
                  pl.BlockSpec((2, 2 * QS, 2 * QC), lambda f, k: (f, 0, 0)),
                  pl.BlockSpec((8, HS), lambda f, k: (0, f)), pl.BlockSpec((8, HS), lambda f, k: (0, f))] + sd.in_specs,
        out_specs=[pl.BlockSpec(blk, lambda f, k: (fwd(k)[0], 0, fwd(k)[1], f)),
                   pl.BlockSpec(blk, lambda f, k: (rev(k)[0], 0, rev(k)[1], f)),
                   pl.BlockSpec((1, 8, 2 * HS), lambda f, k: (k, 0, f))] + sd.out_specs,
        out_shape=[ysh, ysh, jax.ShapeDtypeStruct((nch, 8, 4 * HS), F32)] + sd.out_shape,
        scratch_shapes=[pltpu.VMEM((rc, 2 * HS), F32), pltpu.VMEM((8, 2 * HS), F32)] + (sd.scratch if side is not None else []),
        compiler_params=_cp(("arbitrary", "arbitrary")), name=name)(z4, z4, perm, bh, ch, ar8, ai8, *sd.arrays)
    yf, yr, hst = outs[:3]
    return yf.reshape(lay.nt, B_W), yr.reshape(lay.nt, B_W), hst, list(outs[3:])


def _ssm_bwd(lay, z, dy, perm, hst, bh, ch, ar8, ai8, name, side=None):
    bl = lay.bl
    rc = ST * 2 * bl
    nch = lay.nr * (TB // ST)
    fwd, rev = _scan_maps(lay)
    z4 = z.reshape(lay.nr, bl, TB, z.shape[1])
    dy4 = dy.reshape(lay.nr, bl, TB, B_W)

    def body(*refs):
        own, srefs = _side_split(refs, 10, 6, 5, side)
        (uf_ref, ur_ref, dyf_ref, dyr_ref, p_ref, hst_ref, bh_ref, ch_ref, ar_ref, ai_ref,
         duf_ref, dur_ref, dbh_ref, dch_ref, dar_ref, dai_ref, hs, es, ec, accr, acci) = own
        f, k = pl.program_id(0), pl.program_id(1)
        _side_start(side, srefs, jnp.logical_and(f == 0, k == 0))

        @pl.when(k == 0)
        def _():
            ec[...] = jnp.zeros_like(ec)
            accr[...] = jnp.zeros_like(accr)
            acci[...] = jnp.zeros_like(acci)
            dbh_ref[...] = jnp.zeros_like(dbh_ref)
            dch_ref[...] = jnp.zeros_like(dch_ref)

        d0 = _d0_rows(rc)
        uv = _pack_rows(uf_ref, ur_ref, p_ref, rc)
        dyv = _pack_rows(dyf_ref, dyr_ref, p_ref, rc)

        hs[0:8, :] = hst_ref[0]
        ucat, dycat = [], []
        for q in range(2):
            cr, ci = 2 * QS * q, 2 * QS * q + QS
            ucat.append(_dir_cat(uv, d0, q))
            dycat.append(_dir_cat(dyv, d0, q))
            hs[8:, cr:cr + 2 * QS] = _nn(ucat[q], bh_ref[q])
            ar = ar_ref[:, QS * q:QS * q + QS]
            ai = ai_ref[:, QS * q:QS * q + QS]

            def step(s, carry, cr=cr, ci=ci, ar=ar, ai=ai):
                hr, hi = carry
                base = _tile_row(s + 1)
                nr = ar * hr - ai * hi + hs[pl.ds(base, 8), cr:cr + LC]
                ni = ar * hi + ai * hr + hs[pl.ds(base, 8), ci:ci + LC]
                hs[pl.ds(base, 8), cr:cr + LC] = nr
                hs[pl.ds(base, 8), ci:ci + LC] = ni
                return nr, ni

            _scan_steps(step, (hs[0:8, cr:cr + LC], hs[0:8, ci:ci + LC]))
            dch_ref[q] += _tn(hs[8:, cr:cr + 2 * QS].astype(MXU_DTYPE), dycat[q])
            es[:, cr:cr + 2 * QS] = _nt(dycat[q], ch_ref[q])

        dui = []
        for q in range(2):
            cr, ci = 2 * QS * q, 2 * QS * q + QS
            ar = ar_ref[:, QS * q:QS * q + QS]
            ai = ai_ref[:, QS * q:QS * q + QS]

            def bstep(i, carry, cr=cr, ci=ci, ar=ar, ai=ai):
                er, ei, sr, si = carry
                base = _tile_row(ST - 1 - i)
                ner = es[pl.ds(base, 8), cr:cr + LC] + ar * er + ai * ei
                nei = es[pl.ds(base, 8), ci:ci + LC] - ai * er + ar * ei
                es[pl.ds(base, 8), cr:cr + LC] = ner
                es[pl.ds(base, 8), ci:ci + LC] = nei
                hpr = hs[pl.ds(base, 8), cr:cr + LC]
                hpi = hs[pl.ds(base, 8), ci:ci + LC]
                return ner, nei, sr + ner * hpr + nei * hpi, si - ner * hpi + nei * hpr

            lo = QS * q
            er, ei, sr, si = _scan_steps(
                bstep, (ec[:, cr:cr + LC], ec[:, ci:ci + LC], accr[:, lo:lo + LC], acci[:, lo:lo + LC]))
            ec[:, cr:cr + LC] = er
            ec[:, ci:ci + LC] = ei
            accr[:, lo:lo + LC] = sr
            acci[:, lo:lo + LC] = si
            eb = es[:, cr:cr + 2 * QS].astype(MXU_DTYPE)
            dui.append(_dir_pick(_nt(eb, bh_ref[q]), d0))
            dbh_ref[q] += _tn(ucat[q], eb)

        dud = _tn(p_ref[...], jnp.concatenate(dui, axis=1).astype(MXU_DTYPE))
        duf_ref[0] = dud[:rc // 2].reshape(bl, ST, 256).astype(duf_ref.dtype)
        dur_ref[0] = dud[rc // 2:].reshape(bl, ST, 256).astype(dur_ref.dtype)

        @pl.when(k == nch - 1)
        def _():
            for d in range(2):
                dar_ref[d:d + 1, :] = jnp.sum(accr[4 * d:4 * d + 4, :], axis=0, keepdims=True)
                dai_ref[d:d + 1, :] = jnp.sum(acci[4 * d:4 * d + 4, :], axis=0, keepdims=True)

        _side_wait(side, srefs, jnp.logical_and(f == 1, k == nch - 1))

    sd = side if side is not None else _Side([])
    last = lambda k: nch - 1 - k
    blk = (1, bl, ST, 256)
    fspec = lambda c0: pl.BlockSpec(blk, lambda f, k: (fwd(last(k))[0], 0, fwd(last(k))[1], c0 + f))
    rspec = lambda c0: pl.BlockSpec(blk, lambda f, k: (rev(last(k))[0], 0, rev(last(k))[1], c0 + f))
    dush = jax.ShapeDtypeStruct((lay.nr, bl, TB, B_W), MXU_DTYPE)
    outs = pl.pallas_call(
        body, grid=(2, nch),
        in_specs=[fspec(2), rspec(2), fspec(0), rspec(0),
                  pl.BlockSpec((rc, rc), lambda f, k: (0, 0)),
                  pl.BlockSpec((1, 8, 2 * HS), lambda f, k: (last(k), 0, f)),
                  pl.BlockSpec((2, 2 * QC, 2 * QS), lambda f, k: (f, 0, 0)),
                  pl.BlockSpec((2, 2 * QS, 2 * QC), lambda f, k: (f, 0, 0)),
                  pl.BlockSpec((8, HS), lambda f, k: (0, f)), pl.BlockSpec((8, HS), lambda f, k: (0, f))] + sd.in_specs,
        out_specs=[fspec(0), rspec(0),
                   pl.BlockSpec((2, 2 * QC, 2 * QS), lambda f, k: (f, 0, 0)),
                   pl.BlockSpec((2, 2 * QS, 2 * QC), lambda f, k: (f, 0, 0)),
                   pl.BlockSpec((2, HS), lambda f, k: (0, f)), pl.BlockSpec((2, HS), lambda f, k: (0, f))] + sd.out_specs,
        out_shape=[dush, dush, jax.ShapeDtypeStruct((4, 2 * QC, 2 * QS), F32),
                   jax.ShapeDtypeStruct((4, 2 * QS, 2 * QC), F32), jax.ShapeDtypeStruct((2, 2 * HS), F32),
                   jax.ShapeDtypeStruct((2, 2 * HS), F32)] + sd.out_shape,
        scratch_shapes=[pltpu.VMEM((rc + 8, 2 * HS), F32), pltpu.VMEM((rc, 2 * HS), F32), pltpu.VMEM((8, 2 * HS), F32),
                        pltpu.VMEM((8, HS), F32), pltpu.VMEM((8, HS), F32)] + (sd.scratch if side is not None else []),
        compiler_params=_cp(("arbitrary", "arbitrary")), name=name)(z4, z4, dy4, dy4, perm, hst, bh, ch, ar8, ai8, *sd.arrays)
    duf, dur, dbh, dch, dar, dai = outs[:6]
    return duf.reshape(lay.nt, B_W), dur.reshape(lay.nt, B_W), dbh, dch, dar, dai, list(outs[6:])


def _glu_fwd(lay, z, yf, yr, dvec, wglu, bglu, name):
    def body(u_ref, yf_ref, yr_ref, d_ref, w_ref, b_ref, o_ref, y_ref):
        y = yf_ref[...].astype(F32) + yr_ref[...].astype(F32) + d_ref[...] * u_ref[...].astype(F32)
        y_ref[...] = y
        g = _gelu(y)
        pre = _nn(g.astype(MXU_DTYPE), w_ref[...]) + b_ref[...]
        o_ref[...] = (g * _sigmoid(pre)).astype(o_ref.dtype)

    tok = pl.BlockSpec((TB, B_W), lambda j: (j, 0))
    vec = pl.BlockSpec((1, B_W), lambda j: (0, 0))
    return pl.pallas_call(
        body, grid=(lay.nb,),
        in_specs=[pl.BlockSpec((TB, B_W), lambda j: (j, 1)), tok, tok, vec, pl.BlockSpec((B_W, B_W), lambda j: (0, 0)), vec],
        out_specs=[tok, tok],
        out_shape=[jax.ShapeDtypeStruct((lay.nt, B_W), MXU_DTYPE), jax.ShapeDtypeStruct((lay.nt, B_W), F32)],
        compiler_params=_cp(("parallel",)), name=name)(z, yf, yr, dvec, wglu, bglu)


def _glu_bwd(lay, z, y, ds, dvec, wglu, bglu, name):
    def body(u_ref, y_ref, ds_ref, d_ref, w_ref, b_ref, dy_ref, dud_ref, dw_ref, db_ref, dd_ref):
        j = pl.program_id(0)

        @pl.when(j == 0)
        def _():
            dw_ref[...] = jnp.zeros_like(dw_ref)
            db_ref[...] = jnp.zeros_like(db_ref)
            dd_ref[...] = jnp.zeros_like(dd_ref)

        yv = y_ref[...]
        g = _gelu(yv)
        gb = g.astype(MXU_DTYPE)
        sg = _sigmoid(_nn(gb, w_ref[...]) + b_ref[...])
        dsv = ds_ref[...].astype(F32)
        dpre = dsv * g * sg * (1.0 - sg)
        dpre_b = dpre.astype(MXU_DTYPE)
        dg = dsv * sg + _nt(dpre_b, w_ref[...])
        dw_ref[...] += _tn(gb, dpre_b)
        db_ref[...] += jnp.sum(dpre, axis=0, keepdims=True)
        dy = dg * _gelu_grad(yv)
        dy_ref[...] = dy.astype(dy_ref.dtype)
        dd_ref[...] += jnp.sum(dy * u_ref[...].astype(F32), axis=0, keepdims=True)
        dud_ref[...] = (dy * d_ref[...]).astype(dud_ref.dtype)

    tok = pl.BlockSpec((TB, B_W), lambda j: (j, 0))
    vec = pl.BlockSpec((1, B_W), lambda j: (0, 0))
    mat = pl.BlockSpec((B_W, B_W), lambda j: (0, 0))
    vsh = jax.ShapeDtypeStruct((1, B_W), F32)
    return pl.pallas_call(
        body, grid=(lay.nb,),
        in_specs=[pl.BlockSpec((TB, B_W), lambda j: (j, 1)), tok, tok, vec, mat, vec],
        out_specs=[tok, tok, mat, vec, vec],
        out_shape=[jax.ShapeDtypeStruct((lay.nt, B_W), MXU_DTYPE), jax.ShapeDtypeStruct((lay.nt, B_W), F32),
                   jax.ShapeDtypeStruct((B_W, B_W), F32), vsh, vsh],
        compiler_params=_cp(("arbitrary",)), name=name)(z, y, ds, dvec, wglu, bglu)


def _dz_assemble(lay, dz_a, duf, dur, dud, dz_p, name):
    def body(a_ref, f_ref, r_ref, d_ref, p_ref, o_ref):
        o_ref[:, :2 * A_W] = a_ref[...].astype(o_ref.dtype)
        o_ref[:, 2 * A_W:2 * A_W + B_W] = (f_ref[...].astype(F32) + r_ref[...].astype(F32) + d_ref[...]).astype(o_ref.dtype)
        o_ref[:, 2 * A_W + B_W:] = p_ref[...].astype(o_ref.dtype)

    spec = lambda w: pl.BlockSpec((TB, w), lambda j: (j, 0))
    return pl.pallas_call(
        body, grid=(lay.nb,), in_specs=[spec(2 * A_W), spec(B_W), spec(B_W), spec(B_W), spec(C_W)],
        out_specs=spec(D_IN), out_shape=jax.ShapeDtypeStruct((lay.nt, D_IN), MXU_DTYPE),
        compiler_params=_cp(("parallel",)), name=name)(dz_a, duf, dur, dud, dz_p)


def _expand_rows(a):
    return jnp.broadcast_to(a[:, :, None, :], (2, SSM_G, SSM_H, SSM_P)).reshape(-1, SSM_P)


def _ssm_params(lam_re, lam_im, log_dt, b_re, b_im, c_re, c_im, name):
    lrx, lix = _expand_rows(lam_re), _expand_rows(lam_im)
    ldtx = _expand_rows(jnp.broadcast_to(log_dt[:, :, None], (2, SSM_G, SSM_P)))
    brt = jnp.transpose(b_re, (0, 1, 3, 2)).reshape(-1, SSM_P)
    bit = jnp.transpose(b_im, (0, 1, 3, 2)).reshape(-1, SSM_P)
    arx, aix, bbr, bbi = _disc_fwd(lrx, lix, ldtx, brt, bit, name)
    ar = arx.reshape(2, SSM_G, SSM_H, SSM_P)[:, :, 0].reshape(2, SSM_G * SSM_P)
    ai = aix.reshape(2, SSM_G, SSM_H, SSM_P)[:, :, 0].reshape(2, SSM_G * SSM_P)
    eye = jnp.eye(GQ, dtype=F32)

    def bmat(bt):
        t = bt.reshape(2, 4, GQ, SSM_H, SSM_P)
        return jnp.einsum('dqghp,gk->qdghkp', t, eye).reshape(4, 2 * QC, QS)

    bh = jnp.concatenate([bmat(bbr), bmat(bbi)], axis=-1).astype(MXU_DTYPE)

    def cmat(c):
        t = c.reshape(2, 4, GQ, SSM_H, SSM_P)
        return jnp.einsum('dqghp,gk->qgpdkh', t, eye).reshape(4, QS, 2 * QC)

    ch = jnp.concatenate([cmat(c_re), -cmat(c_im)], axis=1).astype(MXU_DTYPE)

    def rows8(a):
        return jnp.repeat(a, 4, axis=0)

    return dict(lrx=lrx, lix=lix, ldtx=ldtx, brt=brt, bit=bit, bh=bh, ch=ch, ar8=rows8(ar), ai8=rows8(ai))


def _ssm_param_grads(sp, dbh, dch, dar, dai, name):
    def bdiag(m):
        t = m.reshape(4, 2, GQ, SSM_H, GQ, SSM_P)
        return jnp.einsum('qdghgp->dqghp', t).reshape(-1, SSM_P)

    dbr, dbi = bdiag(dbh[..., :QS]), bdiag(dbh[..., QS:])

    def cdiag(m):
        t = m.reshape(4, GQ, SSM_P, 2, GQ, SSM_H)
        return jnp.einsum('qgpdgh->dqghp', t).reshape(2, SSM_G, SSM_H, SSM_P)

    dc_re, dc_im = cdiag(dch[:, :QS]), -cdiag(dch[:, QS:])

    def hrow(a):
        t = a.reshape(2, SSM_G, 1, SSM_P)
        return jnp.concatenate([t, jnp.zeros((2, SSM_G, SSM_H - 1, SSM_P), F32)], axis=2).reshape(-1, SSM_P)

    glr, gli, gdt, gbr, gbi = _disc_bwd(sp["lrx"], sp["lix"], sp["ldtx"], sp["brt"], sp["bit"],
                                        hrow(dar), hrow(dai), dbr, dbi, name)
    to_b = lambda g: jnp.transpose(g.reshape(2, SSM_G, SSM_H, SSM_P), (0, 1, 3, 2))
    return dict(ssm_lam_re=glr.reshape(2, SSM_G, SSM_P), ssm_lam_im=gli.reshape(2, SSM_G, SSM_P),
                ssm_log_dt=gdt.reshape(2, SSM_G), ssm_b_re=to_b(gbr), ssm_b_im=to_b(gbi),
                ssm_c_re=dc_re, ssm_c_im=dc_im)


def _layer_consts(p):
    c = {}
    c["ws"] = p["sgu_w"].astype(MXU_DTYPE)
    c["wst"] = jnp.transpose(p["sgu_w"], (0, 2, 1)).astype(MXU_DTYPE)
    c["gbias"] = jnp.repeat(p["sgu_b"].T, 64, axis=1)
    pw = jnp.zeros((C_W, C_W), F32)
    for i in range(4):
        pw = pw.at[64 * i:64 * i + 64, 64 * i:64 * i + 64].set(p["pool_w"][i])
    c["pw"] = pw.astype(MXU_DTYPE)
    c["pscale"] = p["pool_scale"].reshape(1, C_W)
    c["dvec"] = p["ssm_d"].reshape(1, B_W)
    c["bglu"] = p["glu_b"].reshape(1, B_W)
    return c


def _layer_fwd(lay, i, x, modarr, p, w, cst, sp, bands, inv, perm, sides=None):
    n = f"l{i}_"
    sides = sides or {}
    win_side, win_fill = sides.get("win", (None, None))
    ssm_side, ssm_fill = sides.get("ssm", (None, None))
    ffn_side, ffn_fill = sides.get("ffn", (None, None))
    res = {"x0": x}
    h = _normmod_fwd(lay, x, p["norm_mix_pre"].reshape(1, D), modarr, 0, 1, n + "nm1")
    z = _mm([(h, w["win_t"])], True, MXU_DTYPE, n + "win", side=win_side)
    if win_side is not None:
        z, extra = z
        win_fill(extra)
    a = _gate_fwd(lay, z, cst["ws"], cst["gbias"], n + "gate")
    yf, yr, hst, extra = _ssm_fwd(lay, z, perm, sp["bh"], sp["ch"], sp["ar8"], sp["ai8"], n + "ssm", ssm_side)
    if ssm_side is not None:
        ssm_fill(extra)
    s, y = _glu_fwd(lay, z, yf, yr, cst["dvec"], w["wglu"], cst["bglu"], n + "glu")
    c = _pool_fwd(lay, z, bands, inv, cst["pw"], cst["pscale"], n + "pool")
    mcat = jnp.concatenate([s, a, c], axis=1)
    res["wout_p"] = _perm_wout(w["wout"])
    m = _mm([(mcat, res["wout_p"])], False, MXU_DTYPE, n + "wout")
    x1 = _resnorm_fwd(lay, x, m, p["norm_mix_post"].reshape(1, D), modarr, 2, n + "rn1")
    h2 = _normmod_fwd(lay, x1, p["norm_ffn_pre"].reshape(1, D), modarr, 3, 4, n + "nm2")
    g, u, act, extra = _ffn_up(h2, w["wg_t"], w["wu_t"], n + "ffn_up", ffn_side)
    if ffn_side is not None:
        ffn_fill(extra)
    f = _mm([(act, w["wd"])], False, MXU_DTYPE, n + "ffn_down")
    x2 = _resnorm_fwd(lay, x1, f, p["norm_ffn_post"].reshape(1, D), modarr, 5, n + "rn2")
    res.update(h=h, z=z, hst=hst, y=y, mcat=mcat, m=m, x1=x1, h2=h2, g=g, u=u, act=act, f=f)
    return x2, res


def _layer_bwd(lay, i, dx2, modarr, p, w, cst, sp, bands, inv, perm, res, side_fns=None):
    n = f"l{i}b_"
    big, small = {}, {}
    side_fns = side_fns or {}
    side_of = lambda key: side_fns[key](big) if key in side_fns else None
    df, dg2, gpost2 = _resnorm_bwd(lay, dx2, res["f"], p["norm_ffn_post"].reshape(1, D), modarr, 5, n + "rn2")
    big["wd"] = _mm_tn(res["act"], df, MXU_DTYPE, n + "dwd")
    dg, du, early = _ffn_down_bwd(df, w["wd"], res["g"], res["u"], n + "ffn_down", side_of("ffn_down"))
    dh2_side = side_of("dh2")
    dh2 = _mm([(dg, w["wg_t"]), (du, w["wu_t"])], False, MXU_DTYPE, n + "dh2", side=dh2_side)
    if dh2_side is not None:
        dh2, ex = dh2
        early = early + ex
    big["wg_t"] = _mm_tn(dg, res["h2"], MXU_DTYPE, n + "dwg")
    big["wu_t"] = _mm_tn(du, res["h2"], MXU_DTYPE, n + "dwu")
    dx1, dsh2, dsc2, gpre2 = _normmod_bwd(lay, res["x1"], dh2, dx2, p["norm_ffn_pre"].reshape(1, D), modarr, 4, n + "nm2")
    dm, dg1, gpost1 = _resnorm_bwd(lay, dx1, res["m"], p["norm_mix_post"].reshape(1, D), modarr, 2, n + "rn1")
    big["wout"] = _unperm_wout(_mm_tn(res["mcat"], dm, MXU_DTYPE, n + "dwout"))
    dmcat = _mm([(dm, res["wout_p"])], True, MXU_DTYPE, n + "dmcat")
    z = res["z"]
    dz_a, dws, dgb = _gate_bwd(lay, z, dmcat, cst["ws"], cst["wst"], cst["gbias"], n + "gate")
    dy, dud, dwglu, dbglu, ddvec = _glu_bwd(lay, z, res["y"], dmcat, cst["dvec"], w["wglu"], cst["bglu"], n + "glu")
    big["wglu"] = dwglu.astype(MXU_DTYPE)
    duf, dur, dbh, dch, dar, dai, ex = _ssm_bwd(lay, z, dy, perm, res["hst"], sp["bh"], sp["ch"], sp["ar8"],
                                                sp["ai8"], n + "ssm", side_of("ssm"))
    early = early + ex
    dz_p, dpw, dpsc = _pool_bwd(lay, z, dmcat, bands, inv, cst["pw"], cst["pscale"], n + "pool")
    dz = _dz_assemble(lay, dz_a, duf, dur, dud, dz_p, n + "dz")
    big["win_t"] = _mm_tn(dz, res["h"], MXU_DTYPE, n + "dwin")
    dh = _mm([(dz, w["win_t"])], False, MXU_DTYPE, n + "dh")
    dx, dsh1, dsc1, gpre1 = _normmod_bwd(lay, res["x0"], dh, dx1, p["norm_mix_pre"].reshape(1, D), modarr, 1, n + "nm1",
                                         latent_only=(i == 0))

    small.update(norm_mix_pre=gpre1[0], norm_mix_post=gpost1[0], norm_ffn_pre=gpre2[0], norm_ffn_post=gpost2[0])
    small["sgu_w"] = dws
    small["sgu_b"] = jnp.sum(dgb.reshape(CHUNK, 4, 64), axis=-1).T
    small.update(_ssm_param_grads(sp, dbh, dch, dar, dai, n + "disc"))
    small["ssm_d"] = ddvec.reshape(SSM_G, SSM_H)
    small["glu_b"] = dbglu[0]
    small["pool_w"] = jnp.stack([dpw[64 * k:64 * k + 64, 64 * k:64 * k + 64] for k in range(4)])
    small["pool_scale"] = dpsc[0]
    dmod = jnp.concatenate([dsh1, dsc1, dg1, dsh2, dsc2, dg2], axis=1)[:lay.bl + 1]
    dmod = jnp.concatenate([dmod, jnp.zeros((8 - lay.bl - 1, 6, D), F32)], axis=0)
    return dx, big, small, dmod, early


def _perm_wout(w):
    return w.reshape(4, D // 4, D)[np.array(WOUT_PERM)].reshape(D, D)


def _unperm_wout(g):
    return g.reshape(4, D // 4, D)[np.array(WOUT_INV)].reshape(D, D)


SMALL_NAMES = ["norm_mix_pre", "norm_mix_post", "norm_ffn_pre", "norm_ffn_post", "sgu_w", "sgu_b", "ssm_lam_re",
               "ssm_lam_im", "ssm_log_dt", "ssm_b_re", "ssm_b_im", "ssm_c_re", "ssm_c_im", "ssm_d", "glu_b", "pool_w",
               "pool_scale"]
BIG_NAMES = ["win_t", "wout", "wglu", "wg_t", "wu_t", "wd"]


def _sincos_2d(rows, cols, dim):
    quarter = dim // 4
    omega = 1.0 / (10000.0 ** (jnp.arange(quarter, dtype=F32) / quarter))
    r = jnp.arange(rows, dtype=F32)[:, None] * omega
    cc = jnp.arange(cols, dtype=F32)[:, None] * omega
    er = jnp.concatenate([jnp.sin(r), jnp.cos(r)], axis=-1)
    ec = jnp.concatenate([jnp.sin(cc), jnp.cos(cc)], axis=-1)
    pe = jnp.concatenate([jnp.broadcast_to(er[:, None, :], (rows, cols, dim // 2)),
                          jnp.broadcast_to(ec[None, :, :], (rows, cols, dim // 2))], axis=-1)
    return pe.reshape(rows * cols, dim)


def _core(x, ctx, target, mods_local, params, weights, w_sides=None, g_side_fns=None):
    bl, lat, _ = x.shape
    assert bl == 4 and lat % TB == 0, "the scan fills 8 sublanes with 2 directions x 4 sequences"
    lay = _Layout(bl, lat)
    pe = _sincos_2d(lat // GRID_W, GRID_W, D)
    xt = _embed(lay, x.reshape(bl * lat, D), ctx.reshape(bl * CTX, D), pe)
    bands_np, inv_np = _band_constants()
    bands, inv = jnp.asarray(bands_np, MXU_DTYPE), jnp.asarray(inv_np, F32)
    perm = jnp.asarray(_scan_perm(bl), MXU_DTYPE)
    rows = lay.modrows_static()
    modarrs, csts, sps, ress, wls = [], [], [], [], []
    for i in range(2):
        modarrs.append(mods_local[i][rows].reshape(lay.nb * 6, 1, D))
        csts.append(_layer_consts(params[i]))
        p = params[i]
        sps.append(_ssm_params(p["ssm_lam_re"], p["ssm_lam_im"], p["ssm_log_dt"], p["ssm_b_re"], p["ssm_b_im"],
                               p["ssm_c_re"], p["ssm_c_im"], f"l{i}_disc"))
        wls.append(dict(weights[i]))

    for i in range(2):
        sides = {}
        for key, (side, fill) in ((w_sides or [{}, {}])[i]).items():
            sides[key] = (side, functools.partial(fill, wls))
        xt, res = _layer_fwd(lay, i, xt, modarrs[i], params[i], wls[i], csts[i], sps[i], bands, inv, perm, sides)
        ress.append(res)
    dx, lossv = _loss_bwd(lay, xt, target.reshape(bl * lat, D))
    bigs, smalls, dmods, early = [None, None], [None, None], [None, None], []
    for i in (1, 0):
        fns = {}
        if i == 0 and g_side_fns is not None:
            fns = {key: functools.partial(fn, bigs[1]) for key, fn in g_side_fns.items()}
        dx, bigs[i], smalls[i], dmods[i], ex = _layer_bwd(lay, i, dx, modarrs[i], params[i], wls[i], csts[i], sps[i],
                                                           bands, inv, perm, ress[i], fns)
        early += ex
    return lossv[0, 0], dx.reshape(bl, lat, D), bigs, smalls, dmods, early


def _my_index():
    return 4 * lax.axis_index("x") + 2 * lax.axis_index("y") + lax.axis_index("c")


def _peer(k):
    x, y, c = lax.axis_index("x"), lax.axis_index("y"), lax.axis_index("c")
    kx, ky, kc = (k >> 2) & 1, (k >> 1) & 1, k & 1
    px = 1 - x if kx else x
    py = 1 - y if ky else y
    pc = 1 - c if kc else c
    return (px, py, pc), 4 * px + 2 * py + pc


class _Side:
    def __init__(self, items):
        self.items = items
        self.n = len(items)
        self.ncopies = sum(len(it[2]) for it in items)
        self.arrays = [it[0] for it in items]
        anyspec = pl.BlockSpec(memory_space=pl.ANY)
        self.in_specs = [anyspec] * self.n
        self.out_specs = [anyspec] * self.n
        self.out_shape = [jax.ShapeDtypeStruct((slots,) + tuple(a.shape) if mode == "gather" else tuple(a.shape), a.dtype)
                          for a, mode, ks, slots in items]
        self.scratch = [pltpu.SemaphoreType.DMA((self.ncopies,)), pltpu.SemaphoreType.DMA((self.ncopies,)),
                        pltpu.SemaphoreType.DMA((self.n,))]

    def _copies(self, ins, outs, sems):
        send_sems, recv_sems, local_sems = sems
        slot_of = lambda idx, slots: idx if slots == 8 else (idx // 2 if slots == 4 else idx % 2)
        me = _my_index()
        local, sends, recvs = [], [], []
        q = 0
        for t, (arr, mode, ks, slots) in enumerate(self.items):
            src_own = ins[t] if mode == "gather" else ins[t].at[me]
            local.append(pltpu.make_async_copy(src_own, outs[t].at[slot_of(me, slots)], local_sems.at[t]))
            for k in ks:
                peer, pidx = _peer(k)
                src = ins[t] if mode == "gather" else ins[t].at[pidx]
                sends.append(pltpu.make_async_remote_copy(
                    src_ref=src, dst_ref=outs[t].at[slot_of(me, slots)], send_sem=send_sems.at[q], recv_sem=recv_sems.at[q],
                    device_id=peer, device_id_type=pl.DeviceIdType.MESH))
                recvs.append(pltpu.make_async_remote_copy(
                    src_ref=src, dst_ref=outs[t].at[slot_of(pidx, slots)], send_sem=send_sems.at[q], recv_sem=recv_sems.at[q],
                    device_id=peer, device_id_type=pl.DeviceIdType.MESH))
                q += 1
        return local, sends, recvs

    def start(self, ins, outs, sems):
        local, sends, _ = self._copies(ins, outs, sems)
        for cp in sends + local:
            cp.start()

    def wait(self, ins, outs, sems):
        local, sends, recvs = self._copies(ins, outs, sems)
        for cp in recvs:
            cp.wait_recv()
        for cp in sends:
            cp.wait_send()
        for cp in local:
            cp.wait()


def _comm(items, name):
    side = _Side(items)
    n = side.n

    def body(*refs):
        ins, outs, sems = refs[:n], refs[n:2 * n], refs[2 * n:]
        side.start(ins, outs, sems)
        side.wait(ins, outs, sems)

    return pl.pallas_call(
        body, in_specs=side.in_specs, out_specs=side.out_specs, out_shape=side.out_shape, scratch_shapes=side.scratch,
        compiler_params=pltpu.CompilerParams(has_side_effects=True), name=name)(*side.arrays)


def _spread(items, name):
    n = len(items)
    ncopies = sum(len(it[1]) for it in items)

    def slot_of(idx, slots):
        return idx if slots == 8 else (idx // 2 if slots == 4 else idx % 2)

    def body(*refs):
        ins, outs, bufs = refs[:n], refs[n:2 * n], refs[2 * n:3 * n]
        load_sems, store_sems, send_sems, recv_sems = refs[3 * n:]
        me = _my_index()
        loads = [pltpu.make_async_copy(ins[t], bufs[t], load_sems.at[t]) for t in range(n)]
        for cp in loads:
            cp.start()
        stores, sends, recvs = [], [], []
        q = 0
        for t, (arr, ks, slots) in enumerate(items):
            loads[t].wait()
            own = outs[t].at[slot_of(me, slots)]
            stores.append(pltpu.make_async_copy(bufs[t], own, store_sems.at[t]))
            stores[-1].start()
            for k in ks:
                peer, pidx = _peer(k)
                sends.append(pltpu.make_async_remote_copy(
                    src_ref=bufs[t], dst_ref=own, send_sem=send_sems.at[q], recv_sem=recv_sems.at[q],
                    device_id=peer, device_id_type=pl.DeviceIdType.MESH))
                recvs.append(pltpu.make_async_remote_copy(
                    src_ref=bufs[t], dst_ref=outs[t].at[slot_of(pidx, slots)], send_sem=send_sems.at[q],
                    recv_sem=recv_sems.at[q], device_id=peer, device_id_type=pl.DeviceIdType.MESH))
                sends[-1].start()
                q += 1
        for cp in recvs:
            cp.wait_recv()
        for cp in sends:
            cp.wait_send()
        for cp in stores:
            cp.wait()

    anyspec = pl.BlockSpec(memory_space=pl.ANY)
    return pl.pallas_call(
        body, in_specs=[anyspec] * n, out_specs=[anyspec] * n,
        out_shape=[jax.ShapeDtypeStruct((slots,) + tuple(arr.shape), arr.dtype) for arr, ks, slots in items],
        scratch_shapes=[pltpu.VMEM(tuple(arr.shape), arr.dtype) for arr, ks, slots in items]
        + [pltpu.SemaphoreType.DMA((n,)), pltpu.SemaphoreType.DMA((n,)), pltpu.SemaphoreType.DMA((ncopies,)),
           pltpu.SemaphoreType.DMA((ncopies,))],
        compiler_params=pltpu.CompilerParams(has_side_effects=True, vmem_limit_bytes=VMEM_LIMIT),
        name=name)(*[it[0] for it in items])


ALL7 = (1, 2, 3, 4, 5, 6, 7)
CHIPS3 = (2, 4, 6)


def _sum8(parts, name):
    def one(a, nm):
        _, r, c = a.shape
        tr = r if r <= 512 else _pick_rows(r)

        def body(a_ref, o_ref):
            acc = a_ref[0].astype(F32)
            for q in range(1, a_ref.shape[0]):
                acc = acc + a_ref[q].astype(F32)
            o_ref[...] = acc

        return pl.pallas_call(
            body, grid=(r // tr,), in_specs=[pl.BlockSpec((a.shape[0], tr, c), lambda i: (0, i, 0))],
            out_specs=pl.BlockSpec((tr, c), lambda i: (i, 0)), out_shape=jax.ShapeDtypeStruct((r, c), F32),
            compiler_params=_cp(("parallel",)), name=nm)(a)

    return [one(a, f"{name}{i}") for i, a in enumerate(parts)]


def _pick_rows(r, cap=512):
    for t in (512, 352, 256, 176, 128, 64, 32, 16, 8):
        if r % t == 0 and t <= cap:
            return t
    return r


def _adam(w, g, m, v, name):
    shape = w.shape
    nel = int(np.prod(shape))
    c1 = 1.0 / (1.0 - ADAM_B1 ** ADAM_STEP)
    c2 = 1.0 / (1.0 - ADAM_B2 ** ADAM_STEP)

    def body(w_ref, g_ref, m_ref, v_ref, d_ref, nm_ref, nv_ref):
        gv = g_ref[...]
        nm = ADAM_B1 * m_ref[...] + (1.0 - ADAM_B1) * gv
        nv = ADAM_B2 * v_ref[...] + (1.0 - ADAM_B2) * (gv * gv)
        d_ref[...] = -ADAM_LR * ((nm * c1) / (jnp.sqrt(nv * c2) + ADAM_EPS) + ADAM_WD * w_ref[...])
        nm_ref[...] = nm
        nv_ref[...] = nv

    padded = int(np.prod(shape[:-2])) * (-(-shape[-2] // 8) * 8) * (-(-shape[-1] // 128) * 128) if len(shape) >= 2 else nel
    if len(shape) >= 2 and padded <= 1024 * 1024:
        sh = jax.ShapeDtypeStruct(shape, F32)
        return pl.pallas_call(body, out_shape=[sh] * 3, compiler_params=_cp(None), name=name)(w, g, m, v)

    if len(shape) >= 2 and shape[-1] >= 128:
        lanes = shape[-1]
    else:
        lanes = 512 if nel % 512 == 0 else 128
    r = nel // lanes
    tr = r if r * lanes <= 384 * 1024 else _pick_rows(r, 384 * 1024 // lanes)

    spec = pl.BlockSpec((tr, lanes), lambda i: (i, 0))
    sh = jax.ShapeDtypeStruct((r, lanes), F32)
    outs = pl.pallas_call(
        body, grid=(r // tr,), in_specs=[spec] * 4, out_specs=[spec] * 3, out_shape=[sh] * 3,
        compiler_params=_cp(("parallel",)), name=name)(*[a.reshape(r, lanes) for a in (w, g, m, v)])
    return [o.reshape(shape) for o in outs]


def _silu(x):
    return x * _sigmoid(x)


def _mod_fwd(c_rows, w_mod, b_cols, name):
    def body(c_ref, w_ref, b_ref, o_ref):
        s = _silu(c_ref[...])
        for l in range(2):
            o_ref[l] = jnp.dot(s, w_ref[l], preferred_element_type=F32, precision=lax.Precision.HIGHEST) + b_ref[l]

    nc = w_mod.shape[2]
    return pl.pallas_call(body, out_shape=jax.ShapeDtypeStruct((2, c_rows.shape[0], nc), F32),
                          compiler_params=_cp(None), name=name)(c_rows, w_mod, b_cols)


def _mod_bwd(c_rows, w_mod, dlat, dctx8, name):
    nrow = c_rows.shape[0]
    nb = nrow - 8

    def body(c_ref, w_ref, dl_ref, dc_ref, gw_ref, gc_ref):
        s = _silu(c_ref[...])
        ctx_row = lax.broadcasted_iota(jnp.int32, (nrow, 1), 0) == nb
        gc = jnp.zeros((1, D), F32)
        for l in range(2):
            dctx = dc_ref[0, l]
            for q in range(1, 8):
                dctx = dctx + dc_ref[q, l]
            dm = dl_ref[l] + jnp.where(ctx_row, dctx, 0.0)
            gw_ref[l] = lax.dot_general(s, dm, (((0,), (0,)), ((), ())), preferred_element_type=F32,
                                        precision=lax.Precision.HIGHEST)
            gc = gc + lax.dot_general(dctx, w_ref[l], (((1,), (1,)), ((), ())), preferred_element_type=F32,
                                      precision=lax.Precision.HIGHEST)
        gc_ref[...] = gc

    nc = w_mod.shape[2]
    return pl.pallas_call(body, out_shape=[jax.ShapeDtypeStruct((2, D, nc), F32), jax.ShapeDtypeStruct((1, D), F32)],
                          compiler_params=_cp(None), name=name)(c_rows, w_mod, dlat, dctx8)


def _bmod_cctx(dmod_all, gc4, c_ctx, name):
    def body(dm_ref, gc_ref, cc_ref, gb_ref, gcc_ref):
        for l in range(2):
            acc = jnp.sum(dm_ref[0, l], axis=0, keepdims=True)
            for q in range(1, 8):
                acc = acc + jnp.sum(dm_ref[q, l], axis=0, keepdims=True)
            gb_ref[l:l + 1, :] = acc
        g = gc_ref[0] + gc_ref[1] + gc_ref[2] + gc_ref[3]
        cv = cc_ref[...]
        sg = _sigmoid(cv)
        gcc_ref[...] = g * (sg * (1.0 + cv * (1.0 - sg)))

    return pl.pallas_call(body, out_shape=[jax.ShapeDtypeStruct((2, 6 * D), F32), jax.ShapeDtypeStruct((1, D), F32)],
                          compiler_params=_cp(None), name=name)(dmod_all, gc4, c_ctx)


def kernel(x, c, ctx, c_ctx, w_mod, b_mod, norm_mix_pre, norm_mix_post, norm_ffn_pre, norm_ffn_post, w_in, w_out, sgu_w, sgu_b, ssm_lam_re, ssm_lam_im, ssm_log_dt, ssm_b_re, ssm_b_im, ssm_c_re, ssm_c_im, ssm_d, glu_w, glu_b, pool_w, pool_scale, ffn_w_gate, ffn_w_up, ffn_w_down, loss_target, m_c_ctx, m_w_mod, m_b_mod, m_norm_mix_pre, m_norm_mix_post, m_norm_ffn_pre, m_norm_ffn_post, m_w_in, m_w_out, m_sgu_w, m_sgu_b, m_ssm_lam_re, m_ssm_lam_im, m_ssm_log_dt, m_ssm_b_re, m_ssm_b_im, m_ssm_c_re, m_ssm_c_im, m_ssm_d, m_glu_w, m_glu_b, m_pool_w, m_pool_scale, m_ffn_w_gate, m_ffn_w_up, m_ffn_w_down, v_c_ctx, v_w_mod, v_b_mod, v_norm_mix_pre, v_norm_mix_post, v_norm_ffn_pre, v_norm_ffn_post, v_w_in, v_w_out, v_sgu_w, v_sgu_b, v_ssm_lam_re, v_ssm_lam_im, v_ssm_log_dt, v_ssm_b_re, v_ssm_b_im, v_ssm_c_re, v_ssm_c_im, v_ssm_d, v_glu_w, v_glu_b, v_pool_w, v_pool_scale, v_ffn_w_gate, v_ffn_w_up, v_ffn_w_down):
    wts = dict(c_ctx=c_ctx, w_mod=w_mod, b_mod=b_mod, norm_mix_pre=norm_mix_pre, norm_mix_post=norm_mix_post,
               norm_ffn_pre=norm_ffn_pre, norm_ffn_post=norm_ffn_post, w_in=w_in, w_out=w_out, sgu_w=sgu_w, sgu_b=sgu_b,
               ssm_lam_re=ssm_lam_re, ssm_lam_im=ssm_lam_im, ssm_log_dt=ssm_log_dt, ssm_b_re=ssm_b_re, ssm_b_im=ssm_b_im,
               ssm_c_re=ssm_c_re, ssm_c_im=ssm_c_im, ssm_d=ssm_d, glu_w=glu_w, glu_b=glu_b, pool_w=pool_w,
               pool_scale=pool_scale, ffn_w_gate=ffn_w_gate, ffn_w_up=ffn_w_up, ffn_w_down=ffn_w_down)
    ms = dict(c_ctx=m_c_ctx, w_mod=m_w_mod, b_mod=m_b_mod, norm_mix_pre=m_norm_mix_pre, norm_mix_post=m_norm_mix_post,
              norm_ffn_pre=m_norm_ffn_pre, norm_ffn_post=m_norm_ffn_post, w_in=m_w_in, w_out=m_w_out, sgu_w=m_sgu_w,
              sgu_b=m_sgu_b, ssm_lam_re=m_ssm_lam_re, ssm_lam_im=m_ssm_lam_im, ssm_log_dt=m_ssm_log_dt,
              ssm_b_re=m_ssm_b_re, ssm_b_im=m_ssm_b_im, ssm_c_re=m_ssm_c_re, ssm_c_im=m_ssm_c_im, ssm_d=m_ssm_d,
              glu_w=m_glu_w, glu_b=m_glu_b, pool_w=m_pool_w, pool_scale=m_pool_scale, ffn_w_gate=m_ffn_w_gate,
              ffn_w_up=m_ffn_w_up, ffn_w_down=m_ffn_w_down)
    vs = dict(c_ctx=v_c_ctx, w_mod=v_w_mod, b_mod=v_b_mod, norm_mix_pre=v_norm_mix_pre, norm_mix_post=v_norm_mix_post,
              norm_ffn_pre=v_norm_ffn_pre, norm_ffn_post=v_norm_ffn_post, w_in=v_w_in, w_out=v_w_out, sgu_w=v_sgu_w,
              sgu_b=v_sgu_b, ssm_lam_re=v_ssm_lam_re, ssm_lam_im=v_ssm_lam_im, ssm_log_dt=v_ssm_log_dt,
              ssm_b_re=v_ssm_b_re, ssm_b_im=v_ssm_b_im, ssm_c_re=v_ssm_c_re, ssm_c_im=v_ssm_c_im, ssm_d=v_ssm_d,
              glu_w=v_glu_w, glu_b=v_glu_b, pool_w=v_pool_w, pool_scale=v_pool_scale, ffn_w_gate=v_ffn_w_gate,
              ffn_w_up=v_ffn_w_up, ffn_w_down=v_ffn_w_down)
    order = list(wts.keys())
    bl = x.shape[0]
    nseq = bl * N_DEV
    me = _my_index()
    chip = me // 2
    ncol = w_mod.shape[2]

    (c_all,) = _spread([(c, ALL7, 8)], "ag_c")
    nrow = nseq + 8
    c_rows = jnp.concatenate([c_all.reshape(nseq, D), c_ctx[None], jnp.zeros((7, D), F32)], axis=0)
    b_cols = lax.dynamic_slice_in_dim(b_mod, chip * ncol, ncol, axis=1)[:, None, :]
    mod_cols = _mod_fwd(c_rows, w_mod, b_cols, "mod_fwd")
    (mod4,) = _spread([(mod_cols, CHIPS3, 4)], "ag_mod")
    mods = jnp.transpose(mod4, (1, 2, 0, 3)).reshape(2, nrow, 6 * D)
    mods_local = jnp.concatenate([lax.dynamic_slice_in_dim(mods, me * bl, bl, axis=1), mods[:, nseq:nseq + 1],
                                  jnp.zeros((2, 8 - bl - 1, 6 * D), F32)], axis=1)

    shards = {}
    for i in range(2):
        for nme, s in zip(BIG_NAMES, [w_in[i].T, w_out[i], glu_w[i], ffn_w_gate[i].T, ffn_w_up[i].T, ffn_w_down[i]]):
            shards[(i, nme)] = s.astype(MXU_DTYPE)
    (win0,) = _comm([(shards[(0, "win_t")], "gather", CHIPS3, 4)], "ag_win0")
    weights = [{"win_t": win0.reshape(-1, D)}, {}]
    ffn_names = ("wg_t", "wu_t", "wd")
    w_plan = [{"win": [(0, "wout"), (0, "wglu")], "ssm": [(0, nme) for nme in ffn_names],
               "ffn": [(1, "win_t"), (1, "wout"), (1, "wglu")]},
              {"ssm": [(1, nme) for nme in ffn_names]}]

    def w_entry(keys):
        def fill(wls, gathered):
            for (i, nme), g in zip(keys, gathered):
                wls[i][nme] = g.reshape(-1, g.shape[-1])
        return _Side([(shards[k2], "gather", CHIPS3, 4) for k2 in keys]), fill

    w_sides = [{key: w_entry(keys) for key, keys in plan.items()} for plan in w_plan]

    eighths = lambda g: g.reshape(8, g.shape[0] // 8, g.shape[1])
    g_plan = {"ffn_down": [(1, "win_t"), (1, "wout"), (1, "wglu"), (1, "wg_t")], "dh2": [(1, "wu_t"), (1, "wd")],
              "ssm": [(0, k) for k in BIG_NAMES if k != "win_t"]}
    early_g = g_plan["ffn_down"] + g_plan["dh2"] + g_plan["ssm"]

    def g_entry(keys):
        return lambda big1, big0: _Side([(eighths((big1 if i == 1 else big0)[k]), "a2a", ALL7, 8) for i, k in keys])

    g_side_fns = {key: g_entry(keys) for key, keys in g_plan.items()}

    params = [{k: wts[k][i] for k in SMALL_NAMES} for i in range(2)]
    loss_part, grad_x, bigs, smalls, dmods, early = _core(x, ctx, loss_target, mods_local, params, weights,
                                                           w_sides, g_side_fns)
    loss = lax.psum(loss_part, ("x", "y", "c"))

    dmod_local = jnp.stack([dmods[i].reshape(8, 6 * D) for i in range(2)])
    (dmod_all,) = _spread([(dmod_local, ALL7, 8)], "ag_dmod")
    dcols = lax.dynamic_slice_in_dim(dmod_all, chip * ncol, ncol, axis=3)
    dlat = jnp.transpose(dcols[:, :, :bl], (1, 0, 2, 3)).reshape(2, nseq, ncol)
    dlat = jnp.concatenate([dlat, jnp.zeros((2, 8, ncol), F32)], axis=1)
    dctx8 = dcols[:, :, bl:bl + 1]
    g_w_mod, gc_part = _mod_bwd(c_rows, w_mod, dlat, dctx8, "mod_bwd")
    (gc4,) = _spread([(gc_part, CHIPS3, 4)], "ag_cctx")
    g_b_mod, g_c_ctx = _bmod_cctx(dmod_all, gc4, c_ctx[None], "bmod_cctx")

    small_flat = jnp.concatenate([jnp.stack([smalls[i][k] for i in range(2)]).reshape(-1) for k in SMALL_NAMES])
    npad = (-small_flat.shape[0]) % (8 * 1024)
    small_flat = jnp.concatenate([small_flat, jnp.zeros((npad,), F32)])
    late = _comm([(eighths(bigs[0]["win_t"]), "a2a", ALL7, 8), (small_flat.reshape(8, -1, 1024), "a2a", ALL7, 8)],
                 "a2a_grads")
    sums = _sum8(list(early) + list(late), "gsum")
    fin = _spread([(s, (1,), 2) for s in sums[:-1]] + [(sums[-1], ALL7, 8)], "ag_grads")
    big_g = [{}, {}]
    for (i, k), g in zip(early_g + [(0, "win_t")], fin[:-1]):
        big_g[i][k] = g.reshape(-1, g.shape[-1])
    small_red = fin[-1].reshape(-1)

    grads = {}
    off = 0
    for k in SMALL_NAMES:
        shp = wts[k].shape
        nel = int(np.prod(shp))
        grads[k] = small_red[off:off + nel].reshape(shp)
        off += nel
    grads["c_ctx"] = g_c_ctx[0]
    grads["w_mod"] = g_w_mod
    grads["b_mod"] = g_b_mod
    grads["w_in"] = jnp.stack([big_g[i]["win_t"].T for i in range(2)])
    grads["w_out"] = jnp.stack([big_g[i]["wout"] for i in range(2)])
    grads["glu_w"] = jnp.stack([big_g[i]["wglu"] for i in range(2)])
    grads["ffn_w_gate"] = jnp.stack([big_g[i]["wg_t"].T for i in range(2)])
    grads["ffn_w_up"] = jnp.stack([big_g[i]["wu_t"].T for i in range(2)])
    grads["ffn_w_down"] = jnp.stack([big_g[i]["wd"] for i in range(2)])

    deltas, new_m, new_v = {}, {}, {}
    for k in order:
        deltas[k], new_m[k], new_v[k] = _adam(wts[k], grads[k], ms[k], vs[k], "adam_" + k)
    return (loss, grad_x, *[grads[k] for k in order], *[deltas[k] for k in order],
            *[new_m[k] for k in order], *[new_v[k] for k in order])
```

```python
import functools
import math

import numpy as np
import jax
import jax.numpy as jnp
from jax import lax
from jax.experimental import pallas as pl
from jax.experimental.pallas import tpu as pltpu

F32 = jnp.float32
BF16 = jnp.bfloat16
MXU_DTYPE = jnp.bfloat16
MCAT_A, MCAT_C = 2, 3
WOUT_PERM, WOUT_INV = (1, 2, 0, 3), (2, 0, 1, 3)

D = 1024
EPS = 1e-6
TB = 256
CTX = 256
CHUNK = 128
GRID_W = 64
A_W, B_W, C_W = 256, 512, 256
D_IN = 1280
D_FF = 2816
SSM_G, SSM_P, SSM_H = 32, 64, 16
ST = 64
POOL_WINDOWS = (2, 4, 8, 16)
N_DEV = 8
VMEM_LIMIT = 52 * 1024 * 1024
GELU_C = math.sqrt(2.0 / math.pi)

ADAM_LR, ADAM_B1, ADAM_B2, ADAM_EPS, ADAM_WD, ADAM_STEP = 0.001, 0.9, 0.999, 1e-08, 0.01, 10


def _cp(sem=None, vmem=VMEM_LIMIT, **kw):
    return pltpu.CompilerParams(dimension_semantics=sem, vmem_limit_bytes=vmem, **kw)


def _pick(n, cap):
    if n <= cap:
        return n
    best = None
    for t in range(128, cap + 1, 128):
        if n % t == 0:
            best = t
    assert best is not None, (n, cap)
    return best


def _gelu(x):
    return 0.5 * x * (1.0 + jnp.tanh(GELU_C * (x + 0.044715 * x * x * x)))


def _gelu_grad(x):
    t = jnp.tanh(GELU_C * (x + 0.044715 * x * x * x))
    return 0.5 * (1.0 + t) + 0.5 * x * (1.0 - t * t) * GELU_C * (1.0 + 3.0 * 0.044715 * x * x)


def _sigmoid(x):
    return 1.0 / (1.0 + jnp.exp(-x))


def _dot(a, b, dims):
    return lax.dot_general(a, b, (dims, ((), ())), preferred_element_type=F32)


def _nn(a, b):
    return _dot(a, b, ((1,), (0,)))


def _nt(a, b):
    return _dot(a, b, ((1,), (1,)))


def _tn(a, b):
    return _dot(a, b, ((0,), (0,)))


def _mm(pairs, nt, out_dtype, name, tm=512, side=None):
    m = pairs[0][0].shape[0]
    n = pairs[0][1].shape[0] if nt else pairs[0][1].shape[1]
    tn = _pick(n, 1408)
    tm = min(tm, m)
    npairs = len(pairs)
    ni, nj = m // tm, n // tn

    def body(*refs):
        own, srefs = _side_split(refs, 2 * npairs, 1, 0, side)
        o_ref = own[-1]
        i, j = pl.program_id(0), pl.program_id(1)
        _side_start(side, srefs, jnp.logical_and(i == 0, j == 0))
        acc = None
        for t in range(npairs):
            a = own[2 * t][...].astype(MXU_DTYPE)
            b = own[2 * t + 1][...].astype(MXU_DTYPE)
            r = _nt(a, b) if nt else _nn(a, b)
            acc = r if acc is None else acc + r
        o_ref[...] = acc.astype(o_ref.dtype)
        _side_wait(side, srefs, jnp.logical_and(i == ni - 1, j == nj - 1))

    sd = side if side is not None else _Side([])
    in_specs, flat = [], []
    for a, b in pairs:
        k = a.shape[1]
        in_specs.append(pl.BlockSpec((tm, k), lambda i, j: (i, 0)))
        in_specs.append(pl.BlockSpec((tn, k), lambda i, j: (j, 0)) if nt else pl.BlockSpec((k, tn), lambda i, j: (0, j)))
        flat += [a, b]
    outs = pl.pallas_call(
        body, grid=(ni, nj), in_specs=in_specs + sd.in_specs,
        out_specs=[pl.BlockSpec((tm, tn), lambda i, j: (i, j))] + sd.out_specs,
        out_shape=[jax.ShapeDtypeStruct((m, n), out_dtype)] + sd.out_shape,
        scratch_shapes=sd.scratch if side is not None else [],
        compiler_params=_cp(("arbitrary", "arbitrary") if side is not None else ("parallel", "parallel")),
        name=name)(*flat, *sd.arrays)
    return outs[0] if side is None else (outs[0], list(outs[1:]))


def _mm_tn(a, b, out_dtype, name):
    m, k1 = a.shape
    n = b.shape[1]
    t1 = _pick(k1, 1408)
    tn = _pick(n, 1024)
    tm = max(t for t in (512, 1024, 1536) if m % t == 0)
    nsteps = m // tm

    def body(a_ref, b_ref, o_ref, acc_ref):
        t = pl.program_id(2)

        @pl.when(t == 0)
        def _():
            acc_ref[...] = jnp.zeros_like(acc_ref)

        acc_ref[...] += _tn(a_ref[...].astype(MXU_DTYPE), b_ref[...].astype(MXU_DTYPE))

        @pl.when(t == nsteps - 1)
        def _():
            o_ref[...] = acc_ref[...].astype(o_ref.dtype)

    return pl.pallas_call(
        body, grid=(k1 // t1, n // tn, nsteps),
        in_specs=[pl.BlockSpec((tm, t1), lambda i, j, t: (t, i)), pl.BlockSpec((tm, tn), lambda i, j, t: (t, j))],
        out_specs=pl.BlockSpec((t1, tn), lambda i, j, t: (i, j)),
        out_shape=jax.ShapeDtypeStruct((k1, n), out_dtype),
        scratch_shapes=[pltpu.VMEM((t1, tn), F32)],
        compiler_params=_cp(("parallel", "parallel", "arbitrary")), name=name)(a, b)


class _Layout:
    def __init__(self, bl, lat):
        self.bl, self.lat = bl, lat
        self.nlb = lat // TB
        self.nr = 1 + self.nlb
        self.nctx = bl
        self.nb = self.nr * bl
        self.nt = self.nb * TB
        self.ctx_row = bl

    def mod_tiles(self, mods):
        rows = np.array([[self.ctx_row if r == 0 else b for b in range(self.bl)] for r in range(self.nr)], np.int32)
        t = mods[rows].reshape(self.nr, self.bl, 6, D)
        return jnp.transpose(t, (0, 2, 1, 3)).reshape(self.nr * 6, self.bl, 1, D)


ST_FWD, ST_BWD = 4, 2


def _tok_spec(lay, st):
    nc = lay.bl // st
    return pl.BlockSpec((st * TB, D), lambda c, r: (r * nc + c, 0))


def _vec_spec():
    return pl.BlockSpec((1, D), lambda c, r: (0, 0))


def _mod_spec(st, k):
    return pl.BlockSpec((1, st, 1, D), lambda c, r: (r * 6 + k, c, 0, 0))


def _x_spec(lay, st):
    return pl.BlockSpec((st, 1, TB, D), lambda c, r: (c, jnp.maximum(r - 1, 0), 0, 0))


def _rows3(ref_or_val, st):
    return ref_or_val.reshape(st, TB, D)


def _acc_rows(acc_ref, val3, st, ctx_row):
    c, r = pl.program_id(0), pl.program_id(1)
    s = jnp.sum(val3, axis=1, keepdims=True)

    @pl.when(r == 0)
    def _():
        acc_ref[ctx_row:ctx_row + 1] += jnp.sum(s, axis=0, keepdims=True)

    @pl.when(r > 0)
    def _():
        acc_ref[pl.ds(c * st, st)] += s


def _first_step():
    return jnp.logical_and(pl.program_id(0) == 0, pl.program_id(1) == 0)


def _embed(lay, x, ctx, pe):
    st = ST_FWD
    bl, nlb = lay.bl, lay.nlb

    def body(x_ref, c_ref, pe_ref, o_ref):
        r = pl.program_id(1)

        @pl.when(r == 0)
        def _():
            o_ref[...] = c_ref[...].reshape(st * TB, D)

        @pl.when(r > 0)
        def _():
            o_ref[...] = (x_ref[...].reshape(st, TB, D) + pe_ref[...]).reshape(st * TB, D)

    return pl.pallas_call(
        body, grid=(bl // st, lay.nr),
        in_specs=[_x_spec(lay, st), pl.BlockSpec((st, CTX, D), lambda c, r: (c, 0, 0)),
                  pl.BlockSpec((1, TB, D), lambda c, r: (jnp.maximum(r - 1, 0), 0, 0))],
        out_specs=_tok_spec(lay, st), out_shape=jax.ShapeDtypeStruct((lay.nt, D), F32),
        compiler_params=_cp(("parallel", "parallel")), name="embed")(
            x.reshape(bl, nlb, TB, D), ctx, pe.reshape(nlb, TB, D))


def _normmod_fwd(lay, x, gain, modt, ksh, ksc, name):
    st = ST_FWD

    def body(x_ref, g_ref, sh_ref, sc_ref, o_ref):
        xv = _rows3(x_ref[...], st)
        r = lax.rsqrt(jnp.mean(xv * xv, axis=-1, keepdims=True) + EPS)
        o_ref[...] = ((xv * r * g_ref[...]) * (1.0 + sc_ref[0]) + sh_ref[0]).reshape(st * TB, D).astype(o_ref.dtype)

    return pl.pallas_call(
        body, grid=(lay.bl // st, lay.nr),
        in_specs=[_tok_spec(lay, st), _vec_spec(), _mod_spec(st, ksh), _mod_spec(st, ksc)],
        out_specs=_tok_spec(lay, st), out_shape=jax.ShapeDtypeStruct((lay.nt, D), MXU_DTYPE),
        compiler_params=_cp(("parallel", "parallel")), name=name)(x, gain, modt, modt)


def _acc_out():
    return pl.BlockSpec((8, 1, D), lambda c, r: (0, 0, 0)), jax.ShapeDtypeStruct((8, 1, D), F32)


def _normmod_bwd(lay, x, dh, dx_in, gain, modt, ksc, name, latent_only=False):
    st = ST_BWD
    acc_spec, acc_shape = _acc_out()
    if latent_only:
        dx_spec, dx_shape = _x_spec(lay, st), jax.ShapeDtypeStruct((lay.bl, lay.nlb, TB, D), F32)
    else:
        dx_spec, dx_shape = _tok_spec(lay, st), jax.ShapeDtypeStruct((lay.nt, D), F32)

    def body(x_ref, dh_ref, dxi_ref, g_ref, sc_ref, dx_ref, dsh_ref, dsc_ref, dg_ref):
        xv = _rows3(x_ref[...], st)
        dhv = _rows3(dh_ref[...].astype(F32), st)
        g = g_ref[...]
        sc1 = 1.0 + sc_ref[0]
        r = lax.rsqrt(jnp.mean(xv * xv, axis=-1, keepdims=True) + EPS)
        xh = xv * r
        dxh = dhv * (g * sc1)
        dx = _rows3(dxi_ref[...], st) + r * (dxh - xh * jnp.mean(dxh * xh, axis=-1, keepdims=True))
        dx_ref[...] = dx.reshape(dx_ref.shape)

        @pl.when(_first_step())
        def _():
            dsh_ref[...] = jnp.zeros_like(dsh_ref)
            dsc_ref[...] = jnp.zeros_like(dsc_ref)
            dg_ref[...] = jnp.zeros_like(dg_ref)

        _acc_rows(dsh_ref, dhv, st, lay.ctx_row)
        _acc_rows(dsc_ref, dhv * (xh * g), st, lay.ctx_row)
        dg_ref[...] += jnp.sum((dhv * sc1 * xh).reshape(st * TB, D), axis=0, keepdims=True)

    return pl.pallas_call(
        body, grid=(lay.bl // st, lay.nr),
        in_specs=[_tok_spec(lay, st), _tok_spec(lay, st), _tok_spec(lay, st), _vec_spec(), _mod_spec(st, ksc)],
        out_specs=[dx_spec, acc_spec, acc_spec, _vec_spec()],
        out_shape=[dx_shape, acc_shape, acc_shape, jax.ShapeDtypeStruct((1, D), F32)],
        compiler_params=_cp(("arbitrary", "arbitrary")), name=name)(x, dh, dx_in, gain, modt)


def _resnorm_fwd(lay, x, m, gain, modt, kgate, name):
    st = ST_FWD

    def body(x_ref, m_ref, g_ref, gate_ref, o_ref):
        mv = _rows3(m_ref[...].astype(F32), st)
        r = lax.rsqrt(jnp.mean(mv * mv, axis=-1, keepdims=True) + EPS)
        o_ref[...] = x_ref[...] + (gate_ref[0] * (mv * r * g_ref[...])).reshape(st * TB, D)

    return pl.pallas_call(
        body, grid=(lay.bl // st, lay.nr),
        in_specs=[_tok_spec(lay, st), _tok_spec(lay, st), _vec_spec(), _mod_spec(st, kgate)],
        out_specs=_tok_spec(lay, st), out_shape=jax.ShapeDtypeStruct((lay.nt, D), F32),
        compiler_params=_cp(("parallel", "parallel")), name=name)(x, m, gain, modt)


def _resnorm_bwd(lay, dxn, m, gain, modt, kgate, name):
    st = ST_BWD
    acc_spec, acc_shape = _acc_out()

    def body(d_ref, m_ref, g_ref, gate_ref, dm_ref, dgate_ref, dg_ref):
        dv = _rows3(d_ref[...], st)
        mv = _rows3(m_ref[...].astype(F32), st)
        g = g_ref[...]
        r = lax.rsqrt(jnp.mean(mv * mv, axis=-1, keepdims=True) + EPS)
        xh = mv * r
        dy = dv * gate_ref[0]
        dxh = dy * g
        dm = r * (dxh - xh * jnp.mean(dxh * xh, axis=-1, keepdims=True))
        dm_ref[...] = dm.reshape(st * TB, D).astype(dm_ref.dtype)

        @pl.when(_first_step())
        def _():
            dgate_ref[...] = jnp.zeros_like(dgate_ref)
            dg_ref[...] = jnp.zeros_like(dg_ref)

        _acc_rows(dgate_ref, dv * (xh * g), st, lay.ctx_row)
        dg_ref[...] += jnp.sum((dy * xh).reshape(st * TB, D), axis=0, keepdims=True)

    return pl.pallas_call(
        body, grid=(lay.bl // st, lay.nr),
        in_specs=[_tok_spec(lay, st), _tok_spec(lay, st), _vec_spec(), _mod_spec(st, kgate)],
        out_specs=[_tok_spec(lay, st), acc_spec, _vec_spec()],
        out_shape=[jax.ShapeDtypeStruct((lay.nt, D), MXU_DTYPE), acc_shape, jax.ShapeDtypeStruct((1, D), F32)],
        compiler_params=_cp(("arbitrary", "arbitrary")), name=name)(dxn, m, gain, modt)


def _loss_bwd(lay, xf, tgt):
    st = ST_FWD

    def body(x_ref, t_ref, dx_ref, l_ref):
        r = pl.program_id(1)

        @pl.when(_first_step())
        def _():
            l_ref[...] = jnp.zeros_like(l_ref)

        @pl.when(r == 0)
        def _():
            dx_ref[...] = jnp.zeros_like(dx_ref)

        @pl.when(r > 0)
        def _():
            e = x_ref[...] - t_ref[...].reshape(st * TB, D)
            dx_ref[...] = e * (1.0 / D)
            l_ref[...] += jnp.sum(e * e) * (0.5 / D)

    return pl.pallas_call(
        body, grid=(lay.bl // st, lay.nr),
        in_specs=[_tok_spec(lay, st), _x_spec(lay, st)],
        out_specs=[_tok_spec(lay, st), pl.BlockSpec((8, 128), lambda c, r: (0, 0))],
        out_shape=[jax.ShapeDtypeStruct((lay.nt, D), F32), jax.ShapeDtypeStruct((8, 128), F32)],
        compiler_params=_cp(("arbitrary", "arbitrary")), name="loss")(xf, tgt.reshape(lay.bl, lay.nlb, TB, D))


FF_TN = D_FF // 2
FF_CHUNKS = ((0, 512), (512, 512), (1024, 384))


def _ffn_up(h, wgt, wut, name, side=None):
    m = h.shape[0]
    tm, tn = min(512, m), FF_TN
    ni, nj = m // tm, D_FF // tn

    def body(*refs):
        (h_ref, wg_ref, wu_ref, g_ref, u_ref, a_ref), srefs = _side_split(refs, 3, 3, 0, side)
        j, i = pl.program_id(0), pl.program_id(1)
        _side_start(side, srefs, jnp.logical_and(i == 0, j == 0))
        hv = h_ref[...]
        for c0, cw in FF_CHUNKS:
            g = _nt(hv, wg_ref[c0:c0 + cw, :])
            u = _nt(hv, wu_ref[c0:c0 + cw, :])
            g_ref[:, c0:c0 + cw] = g.astype(g_ref.dtype)
            u_ref[:, c0:c0 + cw] = u.astype(u_ref.dtype)
            a_ref[:, c0:c0 + cw] = (g * _sigmoid(g) * u).astype(a_ref.dtype)
        _side_wait(side, srefs, jnp.logical_and(i == ni - 1, j == nj - 1))

    sd = side if side is not None else _Side([])
    osp = pl.BlockSpec((tm, tn), lambda j, i: (i, j))
    osh = jax.ShapeDtypeStruct((m, D_FF), MXU_DTYPE)
    outs = pl.pallas_call(
        body, grid=(nj, ni),
        in_specs=[pl.BlockSpec((tm, D), lambda j, i: (i, 0)), pl.BlockSpec((tn, D), lambda j, i: (j, 0)),
                  pl.BlockSpec((tn, D), lambda j, i: (j, 0))] + sd.in_specs,
        out_specs=[osp, osp, osp] + sd.out_specs, out_shape=[osh, osh, osh] + sd.out_shape,
        scratch_shapes=sd.scratch if side is not None else [],
        compiler_params=_cp(("arbitrary", "arbitrary") if side is not None else ("parallel", "parallel")),
        name=name)(h, wgt, wut, *sd.arrays)
    return outs[0], outs[1], outs[2], list(outs[3:])


def _ffn_down_bwd(df, wd, g, u, name, side=None):
    m = df.shape[0]
    tm, tn = min(512, m), FF_TN
    ni, nj = m // tm, D_FF // tn

    def body(*refs):
        (df_ref, wd_ref, g_ref, u_ref, dg_ref, du_ref), srefs = _side_split(refs, 4, 2, 0, side)
        j, i = pl.program_id(0), pl.program_id(1)
        _side_start(side, srefs, jnp.logical_and(i == 0, j == 0))
        dfv = df_ref[...]
        for c0, cw in FF_CHUNKS:
            da = _nt(dfv, wd_ref[c0:c0 + cw, :])
            gv = g_ref[:, c0:c0 + cw].astype(F32)
            uv = u_ref[:, c0:c0 + cw].astype(F32)
            s = _sigmoid(gv)
            dg_ref[:, c0:c0 + cw] = (da * uv * (s * (1.0 + gv * (1.0 - s)))).astype(dg_ref.dtype)
            du_ref[:, c0:c0 + cw] = (da * gv * s).astype(du_ref.dtype)
        _side_wait(side, srefs, jnp.logical_and(i == ni - 1, j == nj - 1))

    sd = side if side is not None else _Side([])
    osp = pl.BlockSpec((tm, tn), lambda j, i: (i, j))
    osh = jax.ShapeDtypeStruct((m, D_FF), MXU_DTYPE)
    outs = pl.pallas_call(
        body, grid=(nj, ni),
        in_specs=[pl.BlockSpec((tm, D), lambda j, i: (i, 0)), pl.BlockSpec((tn, D), lambda j, i: (j, 0)), osp, osp]
        + sd.in_specs,
        out_specs=[osp, osp] + sd.out_specs, out_shape=[osh, osh] + sd.out_shape,
        scratch_shapes=sd.scratch if side is not None else [],
        compiler_params=_cp(("arbitrary", "arbitrary") if side is not None else ("parallel", "parallel")),
        name=name)(df, wd, g, u, *sd.arrays)
    return outs[0], outs[1], list(outs[2:])


def _head_masks(shape):
    lane = lax.broadcasted_iota(jnp.int32, shape, 1)
    return [jnp.logical_and(lane >= 64 * h, lane < 64 * h + 64) for h in range(4)]


def _head_mean(x, masks):
    out = jnp.zeros_like(x)
    for mk in masks:
        s = jnp.sum(jnp.where(mk, x, 0.0), axis=-1, keepdims=True) * (1.0 / 64.0)
        out = jnp.where(mk, s, out)
    return out


def _gate_common(z, masks):
    zg = _gelu(z)
    u = zg[:, :A_W]
    v = zg[:, A_W:]
    mu = _head_mean(v, masks)
    vc = v - mu
    rstd = lax.rsqrt(_head_mean(vc * vc, masks) + EPS)
    return u, vc * rstd, rstd


def _gate_s(vn, ws_ref, bias, masks):
    parts = []
    for c in range(TB // CHUNK):
        vc = vn[c * CHUNK:(c + 1) * CHUNK]
        s = bias
        for h in range(4):
            s = s + _nn(ws_ref[h], jnp.where(masks[h][:CHUNK], vc, 0.0).astype(MXU_DTYPE))
        parts.append(s)
    return jnp.concatenate(parts, axis=0)


def _gate_fwd(lay, z, ws, bias, name):
    def body(z_ref, ws_ref, b_ref, o_ref):
        masks = _head_masks((TB, A_W))
        u, vn, _ = _gate_common(z_ref[...].astype(F32), masks)
        o_ref[...] = (u * _gate_s(vn, ws_ref, b_ref[...], masks)).astype(o_ref.dtype)

    return pl.pallas_call(
        body, grid=(lay.nb,),
        in_specs=[pl.BlockSpec((TB, 2 * A_W), lambda j: (j, 0)), pl.BlockSpec((4, CHUNK, CHUNK), lambda j: (0, 0, 0)),
                  pl.BlockSpec((CHUNK, A_W), lambda j: (0, 0))],
        out_specs=pl.BlockSpec((TB, A_W), lambda j: (j, 0)),
        out_shape=jax.ShapeDtypeStruct((lay.nt, A_W), MXU_DTYPE),
        compiler_params=_cp(("parallel",)), name=name)(z, ws, bias)


def _gate_bwd(lay, z, da, ws, wst, bias, name):
    def body(z_ref, da_ref, ws_ref, wst_ref, b_ref, dz_ref, dws_ref, db_ref):
        j = pl.program_id(0)

        @pl.when(j == 0)
        def _():
            dws_ref[...] = jnp.zeros_like(dws_ref)
            db_ref[...] = jnp.zeros_like(db_ref)

        masks = _head_masks((TB, A_W))
        zv = z_ref[...].astype(F32)
        u, vn, rstd = _gate_common(zv, masks)
        s = _gate_s(vn, ws_ref, b_ref[...], masks)
        dav = da_ref[...].astype(F32)
        du = dav * s
        ds = dav * u
        dvn_parts = []
        for c in range(TB // CHUNK):
            sl = slice(c * CHUNK, (c + 1) * CHUNK)
            ds_c = ds[sl]
            vn_c = vn[sl].astype(MXU_DTYPE)
            db_ref[...] += ds_c
            ds_b = ds_c.astype(MXU_DTYPE)
            dvn_c = jnp.zeros((CHUNK, A_W), F32)
            for h in range(4):
                mk = masks[h][:CHUNK]
                dws_ref[h] += _nt(jnp.where(mk, ds_c, 0.0).astype(MXU_DTYPE), vn_c)
                dvn_c = dvn_c + jnp.where(mk, _nn(wst_ref[h], ds_b), 0.0)
            dvn_parts.append(dvn_c)
        dvn = jnp.concatenate(dvn_parts, axis=0)
        dv = rstd * (dvn - _head_mean(dvn, masks) - vn * _head_mean(dvn * vn, masks))
        gg = _gelu_grad(zv)
        dz_ref[:, :A_W] = (du * gg[:, :A_W]).astype(dz_ref.dtype)
        dz_ref[:, A_W:] = (dv * gg[:, A_W:]).astype(dz_ref.dtype)

    return pl.pallas_call(
        body, grid=(lay.nb,),
        in_specs=[pl.BlockSpec((TB, 2 * A_W), lambda j: (j, 0)), pl.BlockSpec((TB, A_W), lambda j: (j, MCAT_A)),
                  pl.BlockSpec((4, CHUNK, CHUNK), lambda j: (0, 0, 0)), pl.BlockSpec((4, CHUNK, CHUNK), lambda j: (0, 0, 0)),
                  pl.BlockSpec((CHUNK, A_W), lambda j: (0, 0))],
        out_specs=[pl.BlockSpec((TB, 2 * A_W), lambda j: (j, 0)), pl.BlockSpec((4, CHUNK, CHUNK), lambda j: (0, 0, 0)),
                   pl.BlockSpec((CHUNK, A_W), lambda j: (0, 0))],
        out_shape=[jax.ShapeDtypeStruct((lay.nt, 2 * A_W), MXU_DTYPE), jax.ShapeDtypeStruct((4, CHUNK, CHUNK), F32),
                   jax.ShapeDtypeStruct((CHUNK, A_W), F32)],
        compiler_params=_cp(("arbitrary",)), name=name)(z, da, ws, wst, bias)


def _band_constants():
    bands = np.zeros((2, 4, TB, TB), np.float32)
    inv = np.zeros((2, 4, TB, 1), np.float32)
    for kind, n in ((0, GRID_W), (1, TB)):
        for i, w in enumerate(POOL_WINDOWS):
            for t in range(TB):
                base, tt = (t // n) * n, t % n
                lo = min(max(tt - w // 2, 0), n)
                hi = min(max(tt - w // 2 + w, 0), n)
                bands[kind, i, t, base + lo:base + hi] = 1.0
                inv[kind, i, t, 0] = 1.0 / (hi - lo)
    return bands, inv


def _split3(x):
    a = x.astype(MXU_DTYPE)
    r1 = x - a.astype(F32)
    b = r1.astype(MXU_DTYPE)
    c = (r1 - b.astype(F32)).astype(MXU_DTYPE)
    return a, b, c


def _window_apply(band_ref, inv_ref, x, masks, transpose, mxu_exact=False):
    out = jnp.zeros_like(x)
    for i in range(4):
        xi = x * inv_ref[0, i] if transpose else x
        acc = None
        for part in ((xi.astype(MXU_DTYPE),) if mxu_exact else _split3(xi)):
            r = _tn(band_ref[0, i], part) if transpose else _nn(band_ref[0, i], part)
            acc = r if acc is None else acc + r
        if not transpose:
            acc = acc * inv_ref[0, i]
        out = jnp.where(masks[i], acc, out)
    return out


def _pool_specs(lay):
    kind = lambda j: jnp.where(j < lay.nctx, 1, 0)
    return [pl.BlockSpec((1, 4, TB, TB), lambda j: (kind(j), 0, 0, 0)), pl.BlockSpec((1, 4, TB, 1), lambda j: (kind(j), 0, 0, 0))]


def _pool_fwd(lay, z, bands, inv, pw, scale, name):
    def body(p_ref, band_ref, inv_ref, pw_ref, sc_ref, o_ref):
        masks = _head_masks((TB, C_W))
        p = p_ref[...].astype(F32)
        diff = _window_apply(band_ref, inv_ref, p, masks, False, mxu_exact=True) - p
        o_ref[...] = (_nn(diff.astype(MXU_DTYPE), pw_ref[...]) * sc_ref[...]).astype(o_ref.dtype)

    return pl.pallas_call(
        body, grid=(lay.nb,),
        in_specs=[pl.BlockSpec((TB, C_W), lambda j: (j, 4))] + _pool_specs(lay)
        + [pl.BlockSpec((C_W, C_W), lambda j: (0, 0)), pl.BlockSpec((1, C_W), lambda j: (0, 0))],
        out_specs=pl.BlockSpec((TB, C_W), lambda j: (j, 0)),
        out_shape=jax.ShapeDtypeStruct((lay.nt, C_W), MXU_DTYPE),
        compiler_params=_cp(("parallel",)), name=name)(z, bands, inv, pw, scale)


def _pool_bwd(lay, z, dc, bands, inv, pw, scale, name):
    def body(p_ref, dc_ref, band_ref, inv_ref, pw_ref, sc_ref, dp_ref, dpw_ref, dsc_ref):
        j = pl.program_id(0)

        @pl.when(j == 0)
        def _():
            dpw_ref[...] = jnp.zeros_like(dpw_ref)
            dsc_ref[...] = jnp.zeros_like(dsc_ref)

        masks = _head_masks((TB, C_W))
        p = p_ref[...].astype(F32)
        dcv = dc_ref[...].astype(F32)
        diff = _window_apply(band_ref, inv_ref, p, masks, False, mxu_exact=True) - p
        diff_b = diff.astype(MXU_DTYPE)
        pre = _nn(diff_b, pw_ref[...])
        dsc_ref[...] += jnp.sum(dcv * pre, axis=0, keepdims=True)
        dpre = dcv * sc_ref[...]
        dpre_b = dpre.astype(MXU_DTYPE)
        dpw_ref[...] += _tn(diff_b, dpre_b)
        ddiff = _nt(dpre_b, pw_ref[...])
        dp_ref[...] = (_window_apply(band_ref, inv_ref, ddiff, masks, True) - ddiff).astype(dp_ref.dtype)

    return pl.pallas_call(
        body, grid=(lay.nb,),
        in_specs=[pl.BlockSpec((TB, C_W), lambda j: (j, 4)), pl.BlockSpec((TB, C_W), lambda j: (j, MCAT_C))] + _pool_specs(lay)
        + [pl.BlockSpec((C_W, C_W), lambda j: (0, 0)), pl.BlockSpec((1, C_W), lambda j: (0, 0))],
        out_specs=[pl.BlockSpec((TB, C_W), lambda j: (j, 0)), pl.BlockSpec((C_W, C_W), lambda j: (0, 0)),
                   pl.BlockSpec((1, C_W), lambda j: (0, 0))],
        out_shape=[jax.ShapeDtypeStruct((lay.nt, C_W), MXU_DTYPE), jax.ShapeDtypeStruct((C_W, C_W), F32),
                   jax.ShapeDtypeStruct((1, C_W), F32)],
        compiler_params=_cp(("arbitrary",)), name=name)(z, dc, bands, inv, pw, scale)


def _disc_math(lr, li, ldt, br, bi):
    dt = jnp.exp(ldt)
    e = jnp.exp(lr * dt)
    ar = e * jnp.cos(li * dt)
    ai = e * jnp.sin(li * dt)
    nr, ni = ar - 1.0, ai
    den = lr * lr + li * li
    qr = (nr * lr + ni * li) / den
    qi = (ni * lr - nr * li) / den
    return ar, ai, qr * br - qi * bi, qr * bi + qi * br


def _disc_fwd(lrx, lix, ldtx, brt, bit, name):
    def body(lr_ref, li_ref, ldt_ref, br_ref, bi_ref, ar_ref, ai_ref, obr_ref, obi_ref):
        ar, ai, obr, obi = _disc_math(lr_ref[...], li_ref[...], ldt_ref[...], br_ref[...], bi_ref[...])
        ar_ref[...] = ar
        ai_ref[...] = ai
        obr_ref[...] = obr
        obi_ref[...] = obi

    sh = jax.ShapeDtypeStruct(lrx.shape, F32)
    return pl.pallas_call(body, out_shape=[sh, sh, sh, sh], name=name)(lrx, lix, ldtx, brt, bit)


def _disc_bwd(lrx, lix, ldtx, brt, bit, dar, dai, dbr, dbi, name):
    nrow = lrx.shape[0] // SSM_H

    def body(lr_ref, li_ref, ldt_ref, br_ref, bi_ref, dar_ref, dai_ref, dbr_ref, dbi_ref,
             glr_ref, gli_ref, gdt_ref, gbr_ref, gbi_ref):
        _, vjp = jax.vjp(_disc_math, lr_ref[...], li_ref[...], ldt_ref[...], br_ref[...], bi_ref[...])
        glr, gli, gdt, gbr, gbi = vjp((dar_ref[...], dai_ref[...], dbr_ref[...], dbi_ref[...]))
        glr_ref[...] = jnp.sum(glr.reshape(nrow, SSM_H, SSM_P), axis=1)
        gli_ref[...] = jnp.sum(gli.reshape(nrow, SSM_H, SSM_P), axis=1)
        gdt_ref[...] = jnp.sum(jnp.sum(gdt.reshape(nrow, SSM_H, SSM_P), axis=1), axis=-1, keepdims=True)
        gbr_ref[...] = gbr
        gbi_ref[...] = gbi

    small = jax.ShapeDtypeStruct((nrow, SSM_P), F32)
    big = jax.ShapeDtypeStruct(lrx.shape, F32)
    return pl.pallas_call(body, out_shape=[small, small, jax.ShapeDtypeStruct((nrow, 1), F32), big, big],
                          name=name)(lrx, lix, ldtx, brt, bit, dar, dai, dbr, dbi)


HS = 1024
GQ, QC, QS = 8, 128, 512
LC = QS
SCAN_UNROLL = ST


def _scan_steps(step, carry):
    if SCAN_UNROLL >= ST:
        for s in range(ST):
            carry = step(s, carry)
        return carry

    def body(i, c):
        for j in range(SCAN_UNROLL):
            c = step(i * SCAN_UNROLL + j, c)
        return c

    return lax.fori_loop(0, ST // SCAN_UNROLL, body, carry)


def _tile_row(s):
    return s * 8 if isinstance(s, int) else pl.multiple_of(s * 8, 8)


def _dir_cat(x, d0, qq):
    xq = x[:, QC * qq:QC * qq + QC]
    zero = jnp.zeros_like(xq)
    return jnp.concatenate([jnp.where(d0, xq, zero), jnp.where(d0, zero, xq)], axis=1)


def _dir_pick(x, d0):
    return jnp.where(d0, x[:, :QC], x[:, QC:])


def _d0_rows(n):
    row = lax.broadcasted_iota(jnp.int32, (n, 1), 0)
    return jnp.bitwise_and(row, 4) == 0


def _scan_perm(bl):
    n = 2 * bl * ST
    p = np.zeros((n, n), np.float32)
    for s in range(ST):
        for d in range(2):
            for b in range(bl):
                t = s if d == 0 else ST - 1 - s
                p[s * 2 * bl + d * bl + b, d * bl * ST + b * ST + t] = 1.0
    return p


def _scan_maps(lay):
    spc = TB // ST
    nlc = lay.nlb * spc

    def fwd(k):
        return k // spc, k % spc

    def rev(k):
        cpos = nlc - 1 - jnp.maximum(k - spc, 0)
        return jnp.where(k < spc, 0, 1 + cpos // spc), jnp.where(k < spc, spc - 1 - k, cpos % spc)

    return fwd, rev


def _pack_rows(f_ref, r_ref, p_ref, rc):
    st = jnp.concatenate([f_ref[0].reshape(rc // 2, 256), r_ref[0].reshape(rc // 2, 256)], axis=0).astype(MXU_DTYPE)
    return _nn(p_ref[...], st).astype(MXU_DTYPE)


def _side_split(refs, n_in, n_out, n_scr, side):
    ns = side.n if side is not None else 0
    ins, sin = refs[:n_in], refs[n_in:n_in + ns]
    o0 = n_in + ns
    outs, sout = refs[o0:o0 + n_out], refs[o0 + n_out:o0 + n_out + ns]
    s0 = o0 + n_out + ns
    return ins + outs + refs[s0:s0 + n_scr], (sin, sout, refs[s0 + n_scr:])


def _side_start(side, srefs, first):
    if side is not None:
        @pl.when(first)
        def _():
            side.start(*srefs)


def _side_wait(side, srefs, last):
    if side is not None:
        @pl.when(last)
        def _():
            side.wait(*srefs)


def _ssm_fwd(lay, z, perm, bh, ch, ar8, ai8, name, side=None):
    bl = lay.bl
    rc = ST * 2 * bl
    nch = lay.nr * (TB // ST)
    fwd, rev = _scan_maps(lay)
    z4 = z.reshape(lay.nr, bl, TB, z.shape[1])

    def body(*refs):
        own, srefs = _side_split(refs, 7, 3, 2, side)
        uf_ref, ur_ref, p_ref, bh_ref, ch_ref, ar_ref, ai_ref, yf_ref, yr_ref, hst_ref, hs, hc = own
        f, k = pl.program_id(0), pl.program_id(1)
        _side_start(side, srefs, jnp.logical_and(f == 0, k == 0))

        @pl.when(k == 0)
        def _():
            hc[...] = jnp.zeros_like(hc)

        hst_ref[0] = hc[...]
        d0 = _d0_rows(rc)
        uv = _pack_rows(uf_ref, ur_ref, p_ref, rc)
        for q in range(2):
            cr, ci = 2 * QS * q, 2 * QS * q + QS
            hs[:, cr:cr + 2 * QS] = _nn(_dir_cat(uv, d0, q), bh_ref[q])
            ar = ar_ref[:, QS * q:QS * q + QS]
            ai = ai_ref[:, QS * q:QS * q + QS]

            def step(s, carry, cr=cr, ci=ci, ar=ar, ai=ai):
                hr, hi = carry
                base = _tile_row(s)
                nr = ar * hr - ai * hi + hs[pl.ds(base, 8), cr:cr + LC]
                ni = ar * hi + ai * hr + hs[pl.ds(base, 8), ci:ci + LC]
                hs[pl.ds(base, 8), cr:cr + LC] = nr
                hs[pl.ds(base, 8), ci:ci + LC] = ni
                return nr, ni

            hr, hi = _scan_steps(step, (hc[:, cr:cr + LC], hc[:, ci:ci + LC]))
            hc[:, cr:cr + LC] = hr
            hc[:, ci:ci + LC] = hi
        yi = jnp.concatenate(
            [_dir_pick(_nn(hs[:, 2 * QS * q:2 * QS * (q + 1)].astype(MXU_DTYPE), ch_ref[q]), d0) for q in range(2)], axis=1)
        yd = _tn(p_ref[...], yi.astype(MXU_DTYPE))
        yf_ref[0] = yd[:rc // 2].reshape(bl, ST, 256).astype(yf_ref.dtype)
        yr_ref[0] = yd[rc // 2:].reshape(bl, ST, 256).astype(yr_ref.dtype)
        _side_wait(side, srefs, jnp.logical_and(f == 1, k == nch - 1))

    sd = side if side is not None else _Side([])
    blk = (1, bl, ST, 256)
    ysh = jax.ShapeDtypeStruct((lay.nr, bl, TB, B_W), MXU_DTYPE)
    outs = pl.pallas_call(
        body, grid=(2, nch),
        in_specs=[pl.BlockSpec(blk, lambda f, k: (fwd(k)[0], 0, fwd(k)[1], 2 + f)),
                  pl.BlockSpec(blk, lambda f, k: (rev(k)[0], 0, rev(k)[1], 2 + f)),
                  pl.BlockSpec((rc, rc), lambda f, k: (0, 0)),
                  pl.BlockSpec((2, 2 * QC, 2 * QS), lambda f, k: (f, 0, 0)),
                  pl.BlockSpec((2, 2 * QS, 2 * QC), lambda f, k: (f, 0, 0)),
                  pl.BlockSpec((8, HS), lambda f, k: (0, f)), pl.BlockSpec((8, HS), lambda f, k: (0, f))] + sd.in_specs,
        out_specs=[pl.BlockSpec(blk, lambda f, k: (fwd(k)[0], 0, fwd(k)[1], f)),
                   pl.BlockSpec(blk, lambda f, k: (rev(k)[0], 0, rev(k)[1], f)),
                   pl.BlockSpec((1, 8, 2 * HS), lambda f, k: (k, 0, f))] + sd.out_specs,
        out_shape=[ysh, ysh, jax.ShapeDtypeStruct((nch, 8, 4 * HS), F32)] + sd.out_shape,
        scratch_shapes=[pltpu.VMEM((rc, 2 * HS), F32), pltpu.VMEM((8, 2 * HS), F32)] + (sd.scratch if side is not None else []),
        compiler_params=_cp(("arbitrary", "arbitrary")), name=name)(z4, z4, perm, bh, ch, ar8, ai8, *sd.arrays)
    yf, yr, hst = outs[:3]
    return yf.reshape(lay.nt, B_W), yr.reshape(lay.nt, B_W), hst, list(outs[3:])


def _ssm_bwd(lay, z, dy, perm, hst, bh, ch, ar8, ai8, name, side=None):
    bl = lay.bl
    rc = ST * 2 * bl
    nch = lay.nr * (TB // ST)
    fwd, rev = _scan_maps(lay)
    z4 = z.reshape(lay.nr, bl, TB, z.shape[1])
    dy4 = dy.reshape(lay.nr, bl, TB, B_W)

    def body(*refs):
        own, srefs = _side_split(refs, 10, 6, 5, side)
        (uf_ref, ur_ref, dyf_ref, dyr_ref, p_ref, hst_ref, bh_ref, ch_ref, ar_ref, ai_ref,
         duf_ref, dur_ref, dbh_ref, dch_ref, dar_ref, dai_ref, hs, es, ec, accr, acci) = own
        f, k = pl.program_id(0), pl.program_id(1)
        _side_start(side, srefs, jnp.logical_and(f == 0, k == 0))

        @pl.when(k == 0)
        def _():
            ec[...] = jnp.zeros_like(ec)
            accr[...] = jnp.zeros_like(accr)
            acci[...] = jnp.zeros_like(acci)
            dbh_ref[...] = jnp.zeros_like(dbh_ref)
            dch_ref[...] = jnp.zeros_like(dch_ref)

        d0 = _d0_rows(rc)
        uv = _pack_rows(uf_ref, ur_ref, p_ref, rc)
        dyv = _pack_rows(dyf_ref, dyr_ref, p_ref, rc)

        hs[0:8, :] = hst_ref[0]
        ucat, dycat = [], []
        for q in range(2):
            cr, ci = 2 * QS * q, 2 * QS * q + QS
            ucat.append(_dir_cat(uv, d0, q))
            dycat.append(_dir_cat(dyv, d0, q))
            hs[8:, cr:cr + 2 * QS] = _nn(ucat[q], bh_ref[q])
            ar = ar_ref[:, QS * q:QS * q + QS]
            ai = ai_ref[:, QS * q:QS * q + QS]

            def step(s, carry, cr=cr, ci=ci, ar=ar, ai=ai):
                hr, hi = carry
                base = _tile_row(s + 1)
                nr = ar * hr - ai * hi + hs[pl.ds(base, 8), cr:cr + LC]
                ni = ar * hi + ai * hr + hs[pl.ds(base, 8), ci:ci + LC]
                hs[pl.ds(base, 8), cr:cr + LC] = nr
                hs[pl.ds(base, 8), ci:ci + LC] = ni
                return nr, ni

            _scan_steps(step, (hs[0:8, cr:cr + LC], hs[0:8, ci:ci + LC]))
            dch_ref[q] += _tn(hs[8:, cr:cr + 2 * QS].astype(MXU_DTYPE), dycat[q])
            es[:, cr:cr + 2 * QS] = _nt(dycat[q], ch_ref[q])

        dui = []
        for q in range(2):
            cr, ci = 2 * QS * q, 2 * QS * q + QS
            ar = ar_ref[:, QS * q:QS * q + QS]
            ai = ai_ref[:, QS * q:QS * q + QS]

            def bstep(i, carry, cr=cr, ci=ci, ar=ar, ai=ai):
                er, ei, sr, si = carry
                base = _tile_row(ST - 1 - i)
                ner = es[pl.ds(base, 8), cr:cr + LC] + ar * er + ai * ei
                nei = es[pl.ds(base, 8), ci:ci + LC] - ai * er + ar * ei
                es[pl.ds(base, 8), cr:cr + LC] = ner
                es[pl.ds(base, 8), ci:ci + LC] = nei
                hpr = hs[pl.ds(base, 8), cr:cr + LC]
                hpi = hs[pl.ds(base, 8), ci:ci + LC]
                return ner, nei, sr + ner * hpr + nei * hpi, si - ner * hpi + nei * hpr

            lo = QS * q
            er, ei, sr, si = _scan_steps(
                bstep, (ec[:, cr:cr + LC], ec[:, ci:ci + LC], accr[:, lo:lo + LC], acci[:, lo:lo + LC]))
            ec[:, cr:cr + LC] = er
            ec[:, ci:ci + LC] = ei
            accr[:, lo:lo + LC] = sr
            acci[:, lo:lo + LC] = si
            eb = es[:, cr:cr + 2 * QS].astype(MXU_DTYPE)
            dui.append(_dir_pick(_nt(eb, bh_ref[q]), d0))
            dbh_ref[q] += _tn(ucat[q], eb)

        dud = _tn(p_ref[...], jnp.concatenate(dui, axis=1).astype(MXU_DTYPE))
        duf_ref[0] = dud[:rc // 2].reshape(bl, ST, 256).astype(duf_ref.dtype)
        dur_ref[0] = dud[rc // 2:].reshape(bl, ST, 256).astype(dur_ref.dtype)

        @pl.when(k == nch - 1)
        def _():
            for d in range(2):
                dar_ref[d:d + 1, :] = jnp.sum(accr[4 * d:4 * d + 4, :], axis=0, keepdims=True)
                dai_ref[d:d + 1, :] = jnp.sum(acci[4 * d:4 * d + 4, :], axis=0, keepdims=True)

        _side_wait(side, srefs, jnp.logical_and(f == 1, k == nch - 1))

    sd = side if side is not None else _Side([])
    last = lambda k: nch - 1 - k
    blk = (1, bl, ST, 256)
    fspec = lambda c0: pl.BlockSpec(blk, lambda f, k: (fwd(last(k))[0], 0, fwd(last(k))[1], c0 + f))
    rspec = lambda c0: pl.BlockSpec(blk, lambda f, k: (rev(last(k))[0], 0, rev(last(k))[1], c0 + f))
    dush = jax.ShapeDtypeStruct((lay.nr, bl, TB, B_W), MXU_DTYPE)
    outs = pl.pallas_call(
        body, grid=(2, nch),
        in_specs=[fspec(2), rspec(2), fspec(0), rspec(0),
                  pl.BlockSpec((rc, rc), lambda f, k: (0, 0)),
                  pl.BlockSpec((1, 8, 2 * HS), lambda f, k: (last(k), 0, f)),
                  pl.BlockSpec((2, 2 * QC, 2 * QS), lambda f, k: (f, 0, 0)),
                  pl.BlockSpec((2, 2 * QS, 2 * QC), lambda f, k: (f, 0, 0)),
                  pl.BlockSpec((8, HS), lambda f, k: (0, f)), pl.BlockSpec((8, HS), lambda f, k: (0, f))] + sd.in_specs,
        out_specs=[fspec(0), rspec(0),
                   pl.BlockSpec((2, 2 * QC, 2 * QS), lambda f, k: (f, 0, 0)),
                   pl.BlockSpec((2, 2 * QS, 2 * QC), lambda f, k: (f, 0, 0)),
                   pl.BlockSpec((2, HS), lambda f, k: (0, f)), pl.BlockSpec((2, HS), lambda f, k: (0, f))] + sd.out_specs,
        out_shape=[dush, dush, jax.ShapeDtypeStruct((4, 2 * QC, 2 * QS), F32),
                   jax.ShapeDtypeStruct((4, 2 * QS, 2 * QC), F32), jax.ShapeDtypeStruct((2, 2 * HS), F32),
                   jax.ShapeDtypeStruct((2, 2 * HS), F32)] + sd.out_shape,
        scratch_shapes=[pltpu.VMEM((rc + 8, 2 * HS), F32), pltpu.VMEM((rc, 2 * HS), F32), pltpu.VMEM((8, 2 * HS), F32),
                        pltpu.VMEM((8, HS), F32), pltpu.VMEM((8, HS), F32)] + (sd.scratch if side is not None else []),
        compiler_params=_cp(("arbitrary", "arbitrary")), name=name)(z4, z4, dy4, dy4, perm, hst, bh, ch, ar8, ai8, *sd.arrays)
    duf, dur, dbh, dch, dar, dai = outs[:6]
    return duf.reshape(lay.nt, B_W), dur.reshape(lay.nt, B_W), dbh, dch, dar, dai, list(outs[6:])


def _glu_fwd(lay, z, yf, yr, dvec, wglu, bglu, name):
    def body(u_ref, yf_ref, yr_ref, d_ref, w_ref, b_ref, o_ref, y_ref):
        y = yf_ref[...].astype(F32) + yr_ref[...].astype(F32) + d_ref[...] * u_ref[...].astype(F32)
        y_ref[...] = y
        g = _gelu(y)
        pre = _nn(g.astype(MXU_DTYPE), w_ref[...]) + b_ref[...]
        o_ref[...] = (g * _sigmoid(pre)).astype(o_ref.dtype)

    tok = pl.BlockSpec((TB, B_W), lambda j: (j, 0))
    vec = pl.BlockSpec((1, B_W), lambda j: (0, 0))
    return pl.pallas_call(
        body, grid=(lay.nb,),
        in_specs=[pl.BlockSpec((TB, B_W), lambda j: (j, 1)), tok, tok, vec, pl.BlockSpec((B_W, B_W), lambda j: (0, 0)), vec],
        out_specs=[tok, tok],
        out_shape=[jax.ShapeDtypeStruct((lay.nt, B_W), MXU_DTYPE), jax.ShapeDtypeStruct((lay.nt, B_W), F32)],
        compiler_params=_cp(("parallel",)), name=name)(z, yf, yr, dvec, wglu, bglu)


def _glu_bwd(lay, z, y, ds, dvec, wglu, bglu, name):
    def body(u_ref, y_ref, ds_ref, d_ref, w_ref, b_ref, dy_ref, dud_ref, dw_ref, db_ref, dd_ref):
        j = pl.program_id(0)

        @pl.when(j == 0)
        def _():
            dw_ref[...] = jnp.zeros_like(dw_ref)
            db_ref[...] = jnp.zeros_like(db_ref)
            dd_ref[...] = jnp.zeros_like(dd_ref)

        yv = y_ref[...]
        g = _gelu(yv)
        gb = g.astype(MXU_DTYPE)
        sg = _sigmoid(_nn(gb, w_ref[...]) + b_ref[...])
        dsv = ds_ref[...].astype(F32)
        dpre = dsv * g * sg * (1.0 - sg)
        dpre_b = dpre.astype(MXU_DTYPE)
        dg = dsv * sg + _nt(dpre_b, w_ref[...])
        dw_ref[...] += _tn(gb, dpre_b)
        db_ref[...] += jnp.sum(dpre, axis=0, keepdims=True)
        dy = dg * _gelu_grad(yv)
        dy_ref[...] = dy.astype(dy_ref.dtype)
        dd_ref[...] += jnp.sum(dy * u_ref[...].astype(F32), axis=0, keepdims=True)
        dud_ref[...] = (dy * d_ref[...]).astype(dud_ref.dtype)

    tok = pl.BlockSpec((TB, B_W), lambda j: (j, 0))
    vec = pl.BlockSpec((1, B_W), lambda j: (0, 0))
    mat = pl.BlockSpec((B_W, B_W), lambda j: (0, 0))
    vsh = jax.ShapeDtypeStruct((1, B_W), F32)
    return pl.pallas_call(
        body, grid=(lay.nb,),
        in_specs=[pl.BlockSpec((TB, B_W), lambda j: (j, 1)), tok, tok, vec, mat, vec],
        out_specs=[tok, tok, mat, vec, vec],
        out_shape=[jax.ShapeDtypeStruct((lay.nt, B_W), MXU_DTYPE), jax.ShapeDtypeStruct((lay.nt, B_W), F32),
                   jax.ShapeDtypeStruct((B_W, B_W), F32), vsh, vsh],
        compiler_params=_cp(("arbitrary",)), name=name)(z, y, ds, dvec, wglu, bglu)


def _dz_assemble(lay, dz_a, duf, dur, dud, dz_p, name):
    def body(a_ref, f_ref, r_ref, d_ref, p_ref, o_ref):
        o_ref[:, :2 * A_W] = a_ref[...].astype(o_ref.dtype)
        o_ref[:, 2 * A_W:2 * A_W + B_W] = (f_ref[...].astype(F32) + r_ref[...].astype(F32) + d_ref[...]).astype(o_ref.dtype)
        o_ref[:, 2 * A_W + B_W:] = p_ref[...].astype(o_ref.dtype)

    spec = lambda w: pl.BlockSpec((TB, w), lambda j: (j, 0))
    return pl.pallas_call(
        body, grid=(lay.nb,), in_specs=[spec(2 * A_W), spec(B_W), spec(B_W), spec(B_W), spec(C_W)],
        out_specs=spec(D_IN), out_shape=jax.ShapeDtypeStruct((lay.nt, D_IN), MXU_DTYPE),
        compiler_params=_cp(("parallel",)), name=name)(dz_a, duf, dur, dud, dz_p)


def _expand_rows(a):
    return jnp.broadcast_to(a[:, :, None, :], (2, SSM_G, SSM_H, SSM_P)).reshape(-1, SSM_P)


def _ssm_params(lam_re, lam_im, log_dt, b_re, b_im, c_re, c_im, name):
    lrx, lix = _expand_rows(lam_re), _expand_rows(lam_im)
    ldtx = _expand_rows(jnp.broadcast_to(log_dt[:, :, None], (2, SSM_G, SSM_P)))
    brt = jnp.transpose(b_re, (0, 1, 3, 2)).reshape(-1, SSM_P)
    bit = jnp.transpose(b_im, (0, 1, 3, 2)).reshape(-1, SSM_P)
    arx, aix, bbr, bbi = _disc_fwd(lrx, lix, ldtx, brt, bit, name)
    ar = arx.reshape(2, SSM_G, SSM_H, SSM_P)[:, :, 0].reshape(2, SSM_G * SSM_P)
    ai = aix.reshape(2, SSM_G, SSM_H, SSM_P)[:, :, 0].reshape(2, SSM_G * SSM_P)
    eye = jnp.eye(GQ, dtype=F32)

    def bmat(bt):
        t = bt.reshape(2, 4, GQ, SSM_H, SSM_P)
        return jnp.einsum('dqghp,gk->qdghkp', t, eye).reshape(4, 2 * QC, QS)

    bh = jnp.concatenate([bmat(bbr), bmat(bbi)], axis=-1).astype(MXU_DTYPE)

    def cmat(c):
        t = c.reshape(2, 4, GQ, SSM_H, SSM_P)
        return jnp.einsum('dqghp,gk->qgpdkh', t, eye).reshape(4, QS, 2 * QC)

    ch = jnp.concatenate([cmat(c_re), -cmat(c_im)], axis=1).astype(MXU_DTYPE)

    def rows8(a):
        return jnp.repeat(a, 4, axis=0)

    return dict(lrx=lrx, lix=lix, ldtx=ldtx, brt=brt, bit=bit, bh=bh, ch=ch, ar8=rows8(ar), ai8=rows8(ai))


def _ssm_param_grads(sp, dbh, dch, dar, dai, name):
    def bdiag(m):
        t = m.reshape(4, 2, GQ, SSM_H, GQ, SSM_P)
        return jnp.einsum('qdghgp->dqghp', t).reshape(-1, SSM_P)

    dbr, dbi = bdiag(dbh[..., :QS]), bdiag(dbh[..., QS:])

    def cdiag(m):
        t = m.reshape(4, GQ, SSM_P, 2, GQ, SSM_H)
        return jnp.einsum('qgpdgh->dqghp', t).reshape(2, SSM_G, SSM_H, SSM_P)

    dc_re, dc_im = cdiag(dch[:, :QS]), -cdiag(dch[:, QS:])

    def hrow(a):
        t = a.reshape(2, SSM_G, 1, SSM_P)
        return jnp.concatenate([t, jnp.zeros((2, SSM_G, SSM_H - 1, SSM_P), F32)], axis=2).reshape(-1, SSM_P)

    glr, gli, gdt, gbr, gbi = _disc_bwd(sp["lrx"], sp["lix"], sp["ldtx"], sp["brt"], sp["bit"],
                                        hrow(dar), hrow(dai), dbr, dbi, name)
    to_b = lambda g: jnp.transpose(g.reshape(2, SSM_G, SSM_H, SSM_P), (0, 1, 3, 2))
    return dict(ssm_lam_re=glr.reshape(2, SSM_G, SSM_P), ssm_lam_im=gli.reshape(2, SSM_G, SSM_P),
                ssm_log_dt=gdt.reshape(2, SSM_G), ssm_b_re=to_b(gbr), ssm_b_im=to_b(gbi),
                ssm_c_re=dc_re, ssm_c_im=dc_im)


def _layer_consts(p):
    c = {}
    c["ws"] = p["sgu_w"].astype(MXU_DTYPE)
    c["wst"] = jnp.transpose(p["sgu_w"], (0, 2, 1)).astype(MXU_DTYPE)
    c["gbias"] = jnp.repeat(p["sgu_b"].T, 64, axis=1)
    pw = jnp.zeros((C_W, C_W), F32)
    for i in range(4):
        pw = pw.at[64 * i:64 * i + 64, 64 * i:64 * i + 64].set(p["pool_w"][i])
    c["pw"] = pw.astype(MXU_DTYPE)
    c["pscale"] = p["pool_scale"].reshape(1, C_W)
    c["dvec"] = p["ssm_d"].reshape(1, B_W)
    c["bglu"] = p["glu_b"].reshape(1, B_W)
    return c


def _layer_fwd(lay, i, x, modarr, p, w, cst, sp, bands, inv, perm, sides=None):
    n = f"l{i}_"
    sides = sides or {}
    win_side, win_fill = sides.get("win", (None, None))
    ssm_side, ssm_fill = sides.get("ssm", (None, None))
    ffn_side, ffn_fill = sides.get("ffn", (None, None))
    res = {"x0": x}
    h = _normmod_fwd(lay, x, p["norm_mix_pre"].reshape(1, D), modarr, 0, 1, n + "nm1")
    z = _mm([(h, w["win_t"])], True, MXU_DTYPE, n + "win", side=win_side)
    if win_side is not None:
        z, extra = z
        win_fill(extra)
    a = _gate_fwd(lay, z, cst["ws"], cst["gbias"], n + "gate")
    yf, yr, hst, extra = _ssm_fwd(lay, z, perm, sp["bh"], sp["ch"], sp["ar8"], sp["ai8"], n + "ssm", ssm_side)
    if ssm_side is not None:
        ssm_fill(extra)
    s, y = _glu_fwd(lay, z, yf, yr, cst["dvec"], w["wglu"], cst["bglu"], n + "glu")
    c = _pool_fwd(lay, z, bands, inv, cst["pw"], cst["pscale"], n + "pool")
    mcat = jnp.concatenate([s, a, c], axis=1)
    res["wout_p"] = _perm_wout(w["wout"])
    m = _mm([(mcat, res["wout_p"])], False, MXU_DTYPE, n + "wout")
    x1 = _resnorm_fwd(lay, x, m, p["norm_mix_post"].reshape(1, D), modarr, 2, n + "rn1")
    h2 = _normmod_fwd(lay, x1, p["norm_ffn_pre"].reshape(1, D), modarr, 3, 4, n + "nm2")
    g, u, act, extra = _ffn_up(h2, w["wg_t"], w["wu_t"], n + "ffn_up", ffn_side)
    if ffn_side is not None:
        ffn_fill(extra)
    f = _mm([(act, w["wd"])], False, MXU_DTYPE, n + "ffn_down")
    x2 = _resnorm_fwd(lay, x1, f, p["norm_ffn_post"].reshape(1, D), modarr, 5, n + "rn2")
    res.update(h=h, z=z, hst=hst, y=y, mcat=mcat, m=m, x1=x1, h2=h2, g=g, u=u, act=act, f=f)
    return x2, res


def _layer_bwd(lay, i, dx2, modarr, p, w, cst, sp, bands, inv, perm, res, side_fns=None):
    n = f"l{i}b_"
    big, small = {}, {}
    side_fns = side_fns or {}
    side_of = lambda key: side_fns[key](big) if key in side_fns else None
    df, dg2, gpost2 = _resnorm_bwd(lay, dx2, res["f"], p["norm_ffn_post"].reshape(1, D), modarr, 5, n + "rn2")
    big["wd"] = _mm_tn(res["act"], df, MXU_DTYPE, n + "dwd")
    dg, du, early = _ffn_down_bwd(df, w["wd"], res["g"], res["u"], n + "ffn_down", side_of("ffn_down"))
    dh2_side = side_of("dh2")
    dh2 = _mm([(dg, w["wg_t"]), (du, w["wu_t"])], False, MXU_DTYPE, n + "dh2", side=dh2_side)
    if dh2_side is not None:
        dh2, ex = dh2
        early = early + ex
    big["wg_t"] = _mm_tn(dg, res["h2"], MXU_DTYPE, n + "dwg")
    big["wu_t"] = _mm_tn(du, res["h2"], MXU_DTYPE, n + "dwu")
    dx1, dsh2, dsc2, gpre2 = _normmod_bwd(lay, res["x1"], dh2, dx2, p["norm_ffn_pre"].reshape(1, D), modarr, 4, n + "nm2")
    dm, dg1, gpost1 = _resnorm_bwd(lay, dx1, res["m"], p["norm_mix_post"].reshape(1, D), modarr, 2, n + "rn1")
    big["wout"] = _unperm_wout(_mm_tn(res["mcat"], dm, MXU_DTYPE, n + "dwout"))
    dmcat = _mm([(dm, res["wout_p"])], True, MXU_DTYPE, n + "dmcat")
    z = res["z"]
    dz_a, dws, dgb = _gate_bwd(lay, z, dmcat, cst["ws"], cst["wst"], cst["gbias"], n + "gate")
    dy, dud, dwglu, dbglu, ddvec = _glu_bwd(lay, z, res["y"], dmcat, cst["dvec"], w["wglu"], cst["bglu"], n + "glu")
    big["wglu"] = dwglu.astype(MXU_DTYPE)
    duf, dur, dbh, dch, dar, dai, ex = _ssm_bwd(lay, z, dy, perm, res["hst"], sp["bh"], sp["ch"], sp["ar8"],
                                                sp["ai8"], n + "ssm", side_of("ssm"))
    early = early + ex
    dz_p, dpw, dpsc = _pool_bwd(lay, z, dmcat, bands, inv, cst["pw"], cst["pscale"], n + "pool")
    dz = _dz_assemble(lay, dz_a, duf, dur, dud, dz_p, n + "dz")
    big["win_t"] = _mm_tn(dz, res["h"], MXU_DTYPE, n + "dwin")
    dh = _mm([(dz, w["win_t"])], False, MXU_DTYPE, n + "dh")
    dx, dsh1, dsc1, gpre1 = _normmod_bwd(lay, res["x0"], dh, dx1, p["norm_mix_pre"].reshape(1, D), modarr, 1, n + "nm1",
                                         latent_only=(i == 0))

    small.update(norm_mix_pre=gpre1[0], norm_mix_post=gpost1[0], norm_ffn_pre=gpre2[0], norm_ffn_post=gpost2[0])
    small["sgu_w"] = dws
    small["sgu_b"] = jnp.sum(dgb.reshape(CHUNK, 4, 64), axis=-1).T
    small.update(_ssm_param_grads(sp, dbh, dch, dar, dai, n + "disc"))
    small["ssm_d"] = ddvec.reshape(SSM_G, SSM_H)
    small["glu_b"] = dbglu[0]
    small["pool_w"] = jnp.stack([dpw[64 * k:64 * k + 64, 64 * k:64 * k + 64] for k in range(4)])
    small["pool_scale"] = dpsc[0]
    dmod = jnp.concatenate([dsh1, dsc1, dg1, dsh2, dsc2, dg2], axis=1)[:lay.bl + 1]
    dmod = jnp.concatenate([dmod, jnp.zeros((8 - lay.bl - 1, 6, D), F32)], axis=0)
    return dx, big, small, dmod, early


def _perm_wout(w):
    return w.reshape(4, D // 4, D)[np.array(WOUT_PERM)].reshape(D, D)


def _unperm_wout(g):
    return g.reshape(4, D // 4, D)[np.array(WOUT_INV)].reshape(D, D)


SMALL_NAMES = ["norm_mix_pre", "norm_mix_post", "norm_ffn_pre", "norm_ffn_post", "sgu_w", "sgu_b", "ssm_lam_re",
               "ssm_lam_im", "ssm_log_dt", "ssm_b_re", "ssm_b_im", "ssm_c_re", "ssm_c_im", "ssm_d", "glu_b", "pool_w",
               "pool_scale"]
BIG_NAMES = ["win_t", "wout", "wglu", "wg_t", "wu_t", "wd"]


def _sincos_2d(rows, cols, dim):
    quarter = dim // 4
    omega = 1.0 / (10000.0 ** (jnp.arange(quarter, dtype=F32) / quarter))
    r = jnp.arange(rows, dtype=F32)[:, None] * omega
    cc = jnp.arange(cols, dtype=F32)[:, None] * omega
    er = jnp.concatenate([jnp.sin(r), jnp.cos(r)], axis=-1)
    ec = jnp.concatenate([jnp.sin(cc), jnp.cos(cc)], axis=-1)
    pe = jnp.concatenate([jnp.broadcast_to(er[:, None, :], (rows, cols, dim // 2)),
                          jnp.broadcast_to(ec[None, :, :], (rows, cols, dim // 2))], axis=-1)
    return pe.reshape(rows * cols, dim)


def _core(x, ctx, target, mods_local, params, weights, w_sides=None, g_side_fns=None):
    bl, lat, _ = x.shape
    assert bl == 4 and lat % TB == 0, "the scan fills 8 sublanes with 2 directions x 4 sequences"
    lay = _Layout(bl, lat)
    pe = _sincos_2d(lat // GRID_W, GRID_W, D)
    xt = _embed(lay, x, ctx, pe)
    bands_np, inv_np = _band_constants()
    bands, inv = jnp.asarray(bands_np, MXU_DTYPE), jnp.asarray(inv_np, F32)
    perm = jnp.asarray(_scan_perm(bl), MXU_DTYPE)
    modarrs, csts, sps, ress, wls = [], [], [], [], []
    for i in range(2):
        modarrs.append(lay.mod_tiles(mods_local[i]))
        csts.append(_layer_consts(params[i]))
        p = params[i]
        sps.append(_ssm_params(p["ssm_lam_re"], p["ssm_lam_im"], p["ssm_log_dt"], p["ssm_b_re"], p["ssm_b_im"],
                               p["ssm_c_re"], p["ssm_c_im"], f"l{i}_disc"))
        wls.append(dict(weights[i]))

    for i in range(2):
        sides = {}
        for key, (side, fill) in ((w_sides or [{}, {}])[i]).items():
            sides[key] = (side, functools.partial(fill, wls))
        xt, res = _layer_fwd(lay, i, xt, modarrs[i], params[i], wls[i], csts[i], sps[i], bands, inv, perm, sides)
        ress.append(res)
    dx, lossv = _loss_bwd(lay, xt, target)
    bigs, smalls, dmods, early = [None, None], [None, None], [None, None], []
    for i in (1, 0):
        fns = {}
        if i == 0 and g_side_fns is not None:
            fns = {key: functools.partial(fn, bigs[1]) for key, fn in g_side_fns.items()}
        dx, bigs[i], smalls[i], dmods[i], ex = _layer_bwd(lay, i, dx, modarrs[i], params[i], wls[i], csts[i], sps[i],
                                                           bands, inv, perm, ress[i], fns)
        early += ex
    return lossv[0, 0], dx.reshape(bl, lat, D), bigs, smalls, dmods, early


def _my_index():
    return 4 * lax.axis_index("x") + 2 * lax.axis_index("y") + lax.axis_index("c")


def _peer(k):
    x, y, c = lax.axis_index("x"), lax.axis_index("y"), lax.axis_index("c")
    kx, ky, kc = (k >> 2) & 1, (k >> 1) & 1, k & 1
    px = 1 - x if kx else x
    py = 1 - y if ky else y
    pc = 1 - c if kc else c
    return (px, py, pc), 4 * px + 2 * py + pc


class _Side:
    def __init__(self, items):
        self.items = items
        self.n = len(items)
        self.ncopies = sum(len(it[2]) for it in items)
        self.arrays = [it[0] for it in items]
        anyspec = pl.BlockSpec(memory_space=pl.ANY)
        self.in_specs = [anyspec] * self.n
        self.out_specs = [anyspec] * self.n
        self.out_shape = [jax.ShapeDtypeStruct((slots,) + tuple(a.shape) if mode == "gather" else tuple(a.shape), a.dtype)
                          for a, mode, ks, slots in items]
        self.scratch = [pltpu.SemaphoreType.DMA((self.ncopies,)), pltpu.SemaphoreType.DMA((self.ncopies,)),
                        pltpu.SemaphoreType.DMA((self.n,))]

    def _copies(self, ins, outs, sems):
        send_sems, recv_sems, local_sems = sems
        slot_of = lambda idx, slots: idx if slots == 8 else (idx // 2 if slots == 4 else idx % 2)
        me = _my_index()
        local, sends, recvs = [], [], []
        q = 0
        for t, (arr, mode, ks, slots) in enumerate(self.items):
            src_own = ins[t] if mode == "gather" else ins[t].at[me]
            local.append(pltpu.make_async_copy(src_own, outs[t].at[slot_of(me, slots)], local_sems.at[t]))
            for k in ks:
                peer, pidx = _peer(k)
                src = ins[t] if mode == "gather" else ins[t].at[pidx]
                sends.append(pltpu.make_async_remote_copy(
                    src_ref=src, dst_ref=outs[t].at[slot_of(me, slots)], send_sem=send_sems.at[q], recv_sem=recv_sems.at[q],
                    device_id=peer, device_id_type=pl.DeviceIdType.MESH))
                recvs.append(pltpu.make_async_remote_copy(
                    src_ref=src, dst_ref=outs[t].at[slot_of(pidx, slots)], send_sem=send_sems.at[q], recv_sem=recv_sems.at[q],
                    device_id=peer, device_id_type=pl.DeviceIdType.MESH))
                q += 1
        return local, sends, recvs

    def start(self, ins, outs, sems):
        local, sends, _ = self._copies(ins, outs, sems)
        for cp in sends + local:
            cp.start()

    def wait(self, ins, outs, sems):
        local, sends, recvs = self._copies(ins, outs, sems)
        for cp in recvs:
            cp.wait_recv()
        for cp in sends:
            cp.wait_send()
        for cp in local:
            cp.wait()


def _comm(items, name):
    side = _Side(items)
    n = side.n

    def body(*refs):
        ins, outs, sems = refs[:n], refs[n:2 * n], refs[2 * n:]
        side.start(ins, outs, sems)
        side.wait(ins, outs, sems)

    return pl.pallas_call(
        body, in_specs=side.in_specs, out_specs=side.out_specs, out_shape=side.out_shape, scratch_shapes=side.scratch,
        compiler_params=pltpu.CompilerParams(has_side_effects=True), name=name)(*side.arrays)


def _spread(items, name):
    n = len(items)
    ncopies = sum(len(it[1]) for it in items)

    def slot_of(idx, slots):
        return idx if slots == 8 else (idx // 2 if slots == 4 else idx % 2)

    def body(*refs):
        ins, outs, bufs = refs[:n], refs[n:2 * n], refs[2 * n:3 * n]
        load_sems, store_sems, send_sems, recv_sems = refs[3 * n:]
        me = _my_index()
        loads = [pltpu.make_async_copy(ins[t], bufs[t], load_sems.at[t]) for t in range(n)]
        for cp in loads:
            cp.start()
        stores, sends, recvs = [], [], []
        q = 0
        for t, (arr, ks, slots) in enumerate(items):
            loads[t].wait()
            own = outs[t].at[slot_of(me, slots)]
            stores.append(pltpu.make_async_copy(bufs[t], own, store_sems.at[t]))
            stores[-1].start()
            for k in ks:
                peer, pidx = _peer(k)
                sends.append(pltpu.make_async_remote_copy(
                    src_ref=bufs[t], dst_ref=own, send_sem=send_sems.at[q], recv_sem=recv_sems.at[q],
                    device_id=peer, device_id_type=pl.DeviceIdType.MESH))
                recvs.append(pltpu.make_async_remote_copy(
                    src_ref=bufs[t], dst_ref=outs[t].at[slot_of(pidx, slots)], send_sem=send_sems.at[q],
                    recv_sem=recv_sems.at[q], device_id=peer, device_id_type=pl.DeviceIdType.MESH))
                sends[-1].start()
                q += 1
        for cp in recvs:
            cp.wait_recv()
        for cp in sends:
            cp.wait_send()
        for cp in stores:
            cp.wait()

    anyspec = pl.BlockSpec(memory_space=pl.ANY)
    return pl.pallas_call(
        body, in_specs=[anyspec] * n, out_specs=[anyspec] * n,
        out_shape=[jax.ShapeDtypeStruct((slots,) + tuple(arr.shape), arr.dtype) for arr, ks, slots in items],
        scratch_shapes=[pltpu.VMEM(tuple(arr.shape), arr.dtype) for arr, ks, slots in items]
        + [pltpu.SemaphoreType.DMA((n,)), pltpu.SemaphoreType.DMA((n,)), pltpu.SemaphoreType.DMA((ncopies,)),
           pltpu.SemaphoreType.DMA((ncopies,))],
        compiler_params=pltpu.CompilerParams(has_side_effects=True, vmem_limit_bytes=VMEM_LIMIT),
        name=name)(*[it[0] for it in items])


ALL7 = (1, 2, 3, 4, 5, 6, 7)
CHIPS3 = (2, 4, 6)


def _sum8(parts, name):
    def one(a, nm):
        _, r, c = a.shape
        tr = r if r <= 512 else _pick_rows(r)

        def body(a_ref, o_ref):
            acc = a_ref[0].astype(F32)
            for q in range(1, a_ref.shape[0]):
                acc = acc + a_ref[q].astype(F32)
            o_ref[...] = acc

        return pl.pallas_call(
            body, grid=(r // tr,), in_specs=[pl.BlockSpec((a.shape[0], tr, c), lambda i: (0, i, 0))],
            out_specs=pl.BlockSpec((tr, c), lambda i: (i, 0)), out_shape=jax.ShapeDtypeStruct((r, c), F32),
            compiler_params=_cp(("parallel",)), name=nm)(a)

    return [one(a, f"{name}{i}") for i, a in enumerate(parts)]


def _pick_rows(r, cap=512):
    for t in (512, 352, 256, 176, 128, 64, 32, 16, 8):
        if r % t == 0 and t <= cap:
            return t
    return r


def _adam(w, g, m, v, name):
    shape = w.shape
    nel = int(np.prod(shape))
    c1 = 1.0 / (1.0 - ADAM_B1 ** ADAM_STEP)
    c2 = 1.0 / (1.0 - ADAM_B2 ** ADAM_STEP)

    def body(w_ref, g_ref, m_ref, v_ref, d_ref, nm_ref, nv_ref):
        gv = g_ref[...]
        nm = ADAM_B1 * m_ref[...] + (1.0 - ADAM_B1) * gv
        nv = ADAM_B2 * v_ref[...] + (1.0 - ADAM_B2) * (gv * gv)
        d_ref[...] = -ADAM_LR * ((nm * c1) / (jnp.sqrt(nv * c2) + ADAM_EPS) + ADAM_WD * w_ref[...])
        nm_ref[...] = nm
        nv_ref[...] = nv

    padded = int(np.prod(shape[:-2])) * (-(-shape[-2] // 8) * 8) * (-(-shape[-1] // 128) * 128) if len(shape) >= 2 else nel
    if len(shape) >= 2 and padded <= 1024 * 1024:
        sh = jax.ShapeDtypeStruct(shape, F32)
        return pl.pallas_call(body, out_shape=[sh] * 3, compiler_params=_cp(None), name=name)(w, g, m, v)

    if len(shape) >= 2 and shape[-1] >= 128:
        lanes = shape[-1]
    else:
        lanes = 512 if nel % 512 == 0 else 128
    r = nel // lanes
    tr = r if r * lanes <= 384 * 1024 else _pick_rows(r, 384 * 1024 // lanes)

    spec = pl.BlockSpec((tr, lanes), lambda i: (i, 0))
    sh = jax.ShapeDtypeStruct((r, lanes), F32)
    outs = pl.pallas_call(
        body, grid=(r // tr,), in_specs=[spec] * 4, out_specs=[spec] * 3, out_shape=[sh] * 3,
        compiler_params=_cp(("parallel",)), name=name)(*[a.reshape(r, lanes) for a in (w, g, m, v)])
    return [o.reshape(shape) for o in outs]


def _silu(x):
    return x * _sigmoid(x)


def _mod_fwd(c_rows, w_mod, b_cols, name):
    def body(c_ref, w_ref, b_ref, o_ref):
        s = _silu(c_ref[...])
        for l in range(2):
            o_ref[l] = jnp.dot(s, w_ref[l], preferred_element_type=F32, precision=lax.Precision.HIGHEST) + b_ref[l]

    nc = w_mod.shape[2]
    return pl.pallas_call(body, out_shape=jax.ShapeDtypeStruct((2, c_rows.shape[0], nc), F32),
                          compiler_params=_cp(None), name=name)(c_rows, w_mod, b_cols)


def _mod_bwd(c_rows, w_mod, dlat, dctx8, name):
    nrow = c_rows.shape[0]
    nb = nrow - 8

    def body(c_ref, w_ref, dl_ref, dc_ref, gw_ref, gc_ref):
        s = _silu(c_ref[...])
        ctx_row = lax.broadcasted_iota(jnp.int32, (nrow, 1), 0) == nb
        gc = jnp.zeros((1, D), F32)
        for l in range(2):
            dctx = dc_ref[0, l]
            for q in range(1, 8):
                dctx = dctx + dc_ref[q, l]
            dm = dl_ref[l] + jnp.where(ctx_row, dctx, 0.0)
            gw_ref[l] = lax.dot_general(s, dm, (((0,), (0,)), ((), ())), preferred_element_type=F32,
                                        precision=lax.Precision.HIGHEST)
            gc = gc + lax.dot_general(dctx, w_ref[l], (((1,), (1,)), ((), ())), preferred_element_type=F32,
                                      precision=lax.Precision.HIGHEST)
        gc_ref[...] = gc

    nc = w_mod.shape[2]
    return pl.pallas_call(body, out_shape=[jax.ShapeDtypeStruct((2, D, nc), F32), jax.ShapeDtypeStruct((1, D), F32)],
                          compiler_params=_cp(None), name=name)(c_rows, w_mod, dlat, dctx8)


def _bmod_cctx(dmod_all, gc4, c_ctx, name):
    def body(dm_ref, gc_ref, cc_ref, gb_ref, gcc_ref):
        for l in range(2):
            acc = jnp.sum(dm_ref[0, l], axis=0, keepdims=True)
            for q in range(1, 8):
                acc = acc + jnp.sum(dm_ref[q, l], axis=0, keepdims=True)
            gb_ref[l:l + 1, :] = acc
        g = gc_ref[0] + gc_ref[1] + gc_ref[2] + gc_ref[3]
        cv = cc_ref[...]
        sg = _sigmoid(cv)
        gcc_ref[...] = g * (sg * (1.0 + cv * (1.0 - sg)))

    return pl.pallas_call(body, out_shape=[jax.ShapeDtypeStruct((2, 6 * D), F32), jax.ShapeDtypeStruct((1, D), F32)],
                          compiler_params=_cp(None), name=name)(dmod_all, gc4, c_ctx)


def kernel(x, c, ctx, c_ctx, w_mod, b_mod, norm_mix_pre, norm_mix_post, norm_ffn_pre, norm_ffn_post, w_in, w_out, sgu_w, sgu_b, ssm_lam_re, ssm_lam_im, ssm_log_dt, ssm_b_re, ssm_b_im, ssm_c_re, ssm_c_im, ssm_d, glu_w, glu_b, pool_w, pool_scale, ffn_w_gate, ffn_w_up, ffn_w_down, loss_target, m_c_ctx, m_w_mod, m_b_mod, m_norm_mix_pre, m_norm_mix_post, m_norm_ffn_pre, m_norm_ffn_post, m_w_in, m_w_out, m_sgu_w, m_sgu_b, m_ssm_lam_re, m_ssm_lam_im, m_ssm_log_dt, m_ssm_b_re, m_ssm_b_im, m_ssm_c_re, m_ssm_c_im, m_ssm_d, m_glu_w, m_glu_b, m_pool_w, m_pool_scale, m_ffn_w_gate, m_ffn_w_up, m_ffn_w_down, v_c_ctx, v_w_mod, v_b_mod, v_norm_mix_pre, v_norm_mix_post, v_norm_ffn_pre, v_norm_ffn_post, v_w_in, v_w_out, v_sgu_w, v_sgu_b, v_ssm_lam_re, v_ssm_lam_im, v_ssm_log_dt, v_ssm_b_re, v_ssm_b_im, v_ssm_c_re, v_ssm_c_im, v_ssm_d, v_glu_w, v_glu_b, v_pool_w, v_pool_scale, v_ffn_w_gate, v_ffn_w_up, v_ffn_w_down):
    wts = dict(c_ctx=c_ctx, w_mod=w_mod, b_mod=b_mod, norm_mix_pre=norm_mix_pre, norm_mix_post=norm_mix_post,
               norm_ffn_pre=norm_ffn_pre, norm_ffn_post=norm_ffn_post, w_in=w_in, w_out=w_out, sgu_w=sgu_w, sgu_b=sgu_b,
               ssm_lam_re=ssm_lam_re, ssm_lam_im=ssm_lam_im, ssm_log_dt=ssm_log_dt, ssm_b_re=ssm_b_re, ssm_b_im=ssm_b_im,
               ssm_c_re=ssm_c_re, ssm_c_im=ssm_c_im, ssm_d=ssm_d, glu_w=glu_w, glu_b=glu_b, pool_w=pool_w,
               pool_scale=pool_scale, ffn_w_gate=ffn_w_gate, ffn_w_up=ffn_w_up, ffn_w_down=ffn_w_down)
    ms = dict(c_ctx=m_c_ctx, w_mod=m_w_mod, b_mod=m_b_mod, norm_mix_pre=m_norm_mix_pre, norm_mix_post=m_norm_mix_post,
              norm_ffn_pre=m_norm_ffn_pre, norm_ffn_post=m_norm_ffn_post, w_in=m_w_in, w_out=m_w_out, sgu_w=m_sgu_w,
              sgu_b=m_sgu_b, ssm_lam_re=m_ssm_lam_re, ssm_lam_im=m_ssm_lam_im, ssm_log_dt=m_ssm_log_dt,
              ssm_b_re=m_ssm_b_re, ssm_b_im=m_ssm_b_im, ssm_c_re=m_ssm_c_re, ssm_c_im=m_ssm_c_im, ssm_d=m_ssm_d,
              glu_w=m_glu_w, glu_b=m_glu_b, pool_w=m_pool_w, pool_scale=m_pool_scale, ffn_w_gate=m_ffn_w_gate,
              ffn_w_up=m_ffn_w_up, ffn_w_down=m_ffn_w_down)
    vs = dict(c_ctx=v_c_ctx, w_mod=v_w_mod, b_mod=v_b_mod, norm_mix_pre=v_norm_mix_pre, norm_mix_post=v_norm_mix_post,
              norm_ffn_pre=v_norm_ffn_pre, norm_ffn_post=v_norm_ffn_post, w_in=v_w_in, w_out=v_w_out, sgu_w=v_sgu_w,
              sgu_b=v_sgu_b, ssm_lam_re=v_ssm_lam_re, ssm_lam_im=v_ssm_lam_im, ssm_log_dt=v_ssm_log_dt,
              ssm_b_re=v_ssm_b_re, ssm_b_im=v_ssm_b_im, ssm_c_re=v_ssm_c_re, ssm_c_im=v_ssm_c_im, ssm_d=v_ssm_d,
              glu_w=v_glu_w, glu_b=v_glu_b, pool_w=v_pool_w, pool_scale=v_pool_scale, ffn_w_gate=v_ffn_w_gate,
              ffn_w_up=v_ffn_w_up, ffn_w_down=v_ffn_w_down)
    order = list(wts.keys())
    bl = x.shape[0]
    nseq = bl * N_DEV
    me = _my_index()
    chip = me // 2
    ncol = w_mod.shape[2]

    (c_all,) = _spread([(c, ALL7, 8)], "ag_c")
    nrow = nseq + 8
    c_rows = jnp.concatenate([c_all.reshape(nseq, D), c_ctx[None], jnp.zeros((7, D), F32)], axis=0)
    b_cols = lax.dynamic_slice_in_dim(b_mod, chip * ncol, ncol, axis=1)[:, None, :]
    mod_cols = _mod_fwd(c_rows, w_mod, b_cols, "mod_fwd")
    (mod4,) = _spread([(mod_cols, CHIPS3, 4)], "ag_mod")
    mods = jnp.transpose(mod4, (1, 2, 0, 3)).reshape(2, nrow, 6 * D)
    mods_local = jnp.concatenate([lax.dynamic_slice_in_dim(mods, me * bl, bl, axis=1), mods[:, nseq:nseq + 1],
                                  jnp.zeros((2, 8 - bl - 1, 6 * D), F32)], axis=1)

    shards = {}
    for i in range(2):
        for nme, s in zip(BIG_NAMES, [w_in[i].T, w_out[i], glu_w[i], ffn_w_gate[i].T, ffn_w_up[i].T, ffn_w_down[i]]):
            shards[(i, nme)] = s.astype(MXU_DTYPE)
    (win0,) = _comm([(shards[(0, "win_t")], "gather", CHIPS3, 4)], "ag_win0")
    weights = [{"win_t": win0.reshape(-1, D)}, {}]
    ffn_names = ("wg_t", "wu_t", "wd")
    w_plan = [{"win": [(0, "wout"), (0, "wglu")], "ssm": [(0, nme) for nme in ffn_names],
               "ffn": [(1, "win_t"), (1, "wout"), (1, "wglu")]},
              {"ssm": [(1, nme) for nme in ffn_names]}]

    def w_entry(keys):
        def fill(wls, gathered):
            for (i, nme), g in zip(keys, gathered):
                wls[i][nme] = g.reshape(-1, g.shape[-1])
        return _Side([(shards[k2], "gather", CHIPS3, 4) for k2 in keys]), fill

    w_sides = [{key: w_entry(keys) for key, keys in plan.items()} for plan in w_plan]

    eighths = lambda g: g.reshape(8, g.shape[0] // 8, g.shape[1])
    g_plan = {"ffn_down": [(1, "win_t"), (1, "wout"), (1, "wglu"), (1, "wg_t")], "dh2": [(1, "wu_t"), (1, "wd")],
              "ssm": [(0, k) for k in BIG_NAMES if k != "win_t"]}
    early_g = g_plan["ffn_down"] + g_plan["dh2"] + g_plan["ssm"]

    def g_entry(keys):
        return lambda big1, big0: _Side([(eighths((big1 if i == 1 else big0)[k]), "a2a", ALL7, 8) for i, k in keys])

    g_side_fns = {key: g_entry(keys) for key, keys in g_plan.items()}

    params = [{k: wts[k][i] for k in SMALL_NAMES} for i in range(2)]
    loss_part, grad_x, bigs, smalls, dmods, early = _core(x, ctx, loss_target, mods_local, params, weights,
                                                           w_sides, g_side_fns)
    loss = lax.psum(loss_part, ("x", "y", "c"))

    dmod_local = jnp.stack([dmods[i].reshape(8, 6 * D) for i in range(2)])
    (dmod_all,) = _spread([(dmod_local, ALL7, 8)], "ag_dmod")
    dcols = lax.dynamic_slice_in_dim(dmod_all, chip * ncol, ncol, axis=3)
    dlat = jnp.transpose(dcols[:, :, :bl], (1, 0, 2, 3)).reshape(2, nseq, ncol)
    dlat = jnp.concatenate([dlat, jnp.zeros((2, 8, ncol), F32)], axis=1)
    dctx8 = dcols[:, :, bl:bl + 1]
    g_w_mod, gc_part = _mod_bwd(c_rows, w_mod, dlat, dctx8, "mod_bwd")
    (gc4,) = _spread([(gc_part, CHIPS3, 4)], "ag_cctx")
    g_b_mod, g_c_ctx = _bmod_cctx(dmod_all, gc4, c_ctx[None], "bmod_cctx")

    small_flat = jnp.concatenate([jnp.stack([smalls[i][k] for i in range(2)]).reshape(-1) for k in SMALL_NAMES])
    npad = (-small_flat.shape[0]) % (8 * 1024)
    small_flat = jnp.concatenate([small_flat, jnp.zeros((npad,), F32)])
    late = _comm([(eighths(bigs[0]["win_t"]), "a2a", ALL7, 8), (small_flat.reshape(8, -1, 1024), "a2a", ALL7, 8)],
                 "a2a_grads")
    sums = _sum8(list(early) + list(late), "gsum")
    fin = _spread([(s, (1,), 2) for s in sums[:-1]] + [(sums[-1], ALL7, 8)], "ag_grads")
    big_g = [{}, {}]
    for (i, k), g in zip(early_g + [(0, "win_t")], fin[:-1]):
        big_g[i][k] = g.reshape(-1, g.shape[-1])
    small_red = fin[-1].reshape(-1)

    grads = {}
    off = 0
    for k in SMALL_NAMES:
        shp = wts[k].shape
        nel = int(np.prod(shp))
        grads[k] = small_red[off:off + nel].reshape(shp)
        off += nel
    grads["c_ctx"] = g_c_ctx[0]
    grads["w_mod"] = g_w_mod
    grads["b_mod"] = g_b_mod
    grads["w_in"] = jnp.stack([big_g[i]["win_t"].T for i in range(2)])
    grads["w_out"] = jnp.stack([big_g[i]["wout"] for i in range(2)])
    grads["glu_w"] = jnp.stack([big_g[i]["wglu"] for i in range(2)])
    grads["ffn_w_gate"] = jnp.stack([big_g[i]["wg_t"].T for i in range(2)])
    grads["ffn_w_up"] = jnp.stack([big_g[i]["wu_t"].T for i in range(2)])
    grads["ffn_w_down"] = jnp.stack([big_g[i]["wd"] for i in range(2)])

    deltas, new_m, new_v = {}, {}, {}
    for k in order:
        deltas[k], new_m[k], new_v[k] = _adam(wts[k], grads[k], ms[k], vs[k], "adam_" + k)
    return (loss, grad_x, *[grads[k] for k in order], *[deltas[k] for k in order],
            *[new_m[k] for k in order], *[new_v[k] for k in order])
```

```python
import functools
import math

import numpy as np
import jax
import jax.numpy as jnp
from jax import lax
from jax.experimental import pallas as pl
from jax.experimental.pallas import tpu as pltpu

F32 = jnp.float32
BF16 = jnp.bfloat16
MXU_DTYPE = jnp.bfloat16
MCAT_A, MCAT_C = 2, 3
WOUT_PERM, WOUT_INV = (1, 2, 0, 3), (2, 0, 1, 3)

D = 1024
EPS = 1e-6
TB = 256
CTX = 256
CHUNK = 128
GRID_W = 64
A_W, B_W, C_W = 256, 512, 256
D_IN = 1280
D_FF = 2816
SSM_G, SSM_P, SSM_H = 32, 64, 16
ST = 64
POOL_WINDOWS = (2, 4, 8, 16)
N_DEV = 8
VMEM_LIMIT = 52 * 1024 * 1024
GELU_C = math.sqrt(2.0 / math.pi)

ADAM_LR, ADAM_B1, ADAM_B2, ADAM_EPS, ADAM_WD, ADAM_STEP = 0.001, 0.9, 0.999, 1e-08, 0.01, 10


def _cp(sem=None, vmem=VMEM_LIMIT, **kw):
    return pltpu.CompilerParams(dimension_semantics=sem, vmem_limit_bytes=vmem, **kw)


def _pick(n, cap):
    if n <= cap:
        return n
    best = None
    for t in range(128, cap + 1, 128):
        if n % t == 0:
            best = t
    assert best is not None, (n, cap)
    return best


def _gelu(x):
    return 0.5 * x * (1.0 + jnp.tanh(GELU_C * (x + 0.044715 * x * x * x)))


def _gelu_grad(x):
    t = jnp.tanh(GELU_C * (x + 0.044715 * x * x * x))
    return 0.5 * (1.0 + t) + 0.5 * x * (1.0 - t * t) * GELU_C * (1.0 + 3.0 * 0.044715 * x * x)


def _sigmoid(x):
    return 1.0 / (1.0 + jnp.exp(-x))


def _dot(a, b, dims):
    return lax.dot_general(a, b, (dims, ((), ())), preferred_element_type=F32)


def _nn(a, b):
    return _dot(a, b, ((1,), (0,)))


def _nt(a, b):
    return _dot(a, b, ((1,), (1,)))


def _tn(a, b):
    return _dot(a, b, ((0,), (0,)))


def _mm(pairs, nt, out_dtype, name, tm=512, side=None):
    m = pairs[0][0].shape[0]
    n = pairs[0][1].shape[0] if nt else pairs[0][1].shape[1]
    tn = _pick(n, 1408)
    tm = min(tm, m)
    npairs = len(pairs)
    ni, nj = m // tm, n // tn

    def body(*refs):
        own, srefs = _side_split(refs, 2 * npairs, 1, 0, side)
        o_ref = own[-1]
        i, j = pl.program_id(0), pl.program_id(1)
        _side_start(side, srefs, jnp.logical_and(i == 0, j == 0))
        acc = None
        for t in range(npairs):
            a = own[2 * t][...].astype(MXU_DTYPE)
            b = own[2 * t + 1][...].astype(MXU_DTYPE)
            r = _nt(a, b) if nt else _nn(a, b)
            acc = r if acc is None else acc + r
        o_ref[...] = acc.astype(o_ref.dtype)
        _side_wait(side, srefs, jnp.logical_and(i == ni - 1, j == nj - 1))

    sd = side if side is not None else _Side([])
    in_specs, flat = [], []
    for a, b in pairs:
        k = a.shape[1]
        in_specs.append(pl.BlockSpec((tm, k), lambda i, j: (i, 0)))
        in_specs.append(pl.BlockSpec((tn, k), lambda i, j: (j, 0)) if nt else pl.BlockSpec((k, tn), lambda i, j: (0, j)))
        flat += [a, b]
    outs = pl.pallas_call(
        body, grid=(ni, nj), in_specs=in_specs + sd.in_specs,
        out_specs=[pl.BlockSpec((tm, tn), lambda i, j: (i, j))] + sd.out_specs,
        out_shape=[jax.ShapeDtypeStruct((m, n), out_dtype)] + sd.out_shape,
        scratch_shapes=sd.scratch if side is not None else [],
        compiler_params=_cp(("arbitrary", "arbitrary") if side is not None else ("parallel", "parallel")),
        name=name)(*flat, *sd.arrays)
    return outs[0] if side is None else (outs[0], list(outs[1:]))


def _mm_tn(a, b, out_dtype, name):
    m, k1 = a.shape
    n = b.shape[1]
    t1 = _pick(k1, 1408)
    tn = _pick(n, 1024)
    tm = max(t for t in (512, 1024, 1536) if m % t == 0)
    nsteps = m // tm

    def body(a_ref, b_ref, o_ref, acc_ref):
        t = pl.program_id(2)

        @pl.when(t == 0)
        def _():
            acc_ref[...] = jnp.zeros_like(acc_ref)

        acc_ref[...] += _tn(a_ref[...].astype(MXU_DTYPE), b_ref[...].astype(MXU_DTYPE))

        @pl.when(t == nsteps - 1)
        def _():
            o_ref[...] = acc_ref[...].astype(o_ref.dtype)

    return pl.pallas_call(
        body, grid=(k1 // t1, n // tn, nsteps),
        in_specs=[pl.BlockSpec((tm, t1), lambda i, j, t: (t, i)), pl.BlockSpec((tm, tn), lambda i, j, t: (t, j))],
        out_specs=pl.BlockSpec((t1, tn), lambda i, j, t: (i, j)),
        out_shape=jax.ShapeDtypeStruct((k1, n), out_dtype),
        scratch_shapes=[pltpu.VMEM((t1, tn), F32)],
        compiler_params=_cp(("parallel", "parallel", "arbitrary")), name=name)(a, b)


class _Layout:
    def __init__(self, bl, lat):
        self.bl, self.lat = bl, lat
        self.nlb = lat // TB
        self.nr = 1 + self.nlb
        self.nctx = bl
        self.nb = self.nr * bl
        self.nt = self.nb * TB
        self.ctx_row = bl

    def mod_tiles(self, mods):
        rows = np.array([[self.ctx_row if r == 0 else b for b in range(self.bl)] for r in range(self.nr)], np.int32)
        t = mods[rows].reshape(self.nr, self.bl, 6, D)
        return jnp.transpose(t, (0, 2, 1, 3)).reshape(self.nr * 6, self.bl, 1, D)


ST_FWD, ST_BWD = 4, 2


def _tok_spec(lay, st):
    nc = lay.bl // st
    return pl.BlockSpec((st * TB, D), lambda c, r: (r * nc + c, 0))


def _vec_spec():
    return pl.BlockSpec((1, D), lambda c, r: (0, 0))


def _mod_spec(st, k):
    return pl.BlockSpec((1, st, 1, D), lambda c, r: (r * 6 + k, c, 0, 0))


def _x_spec(lay, st):
    return pl.BlockSpec((st, 1, TB, D), lambda c, r: (c, jnp.maximum(r - 1, 0), 0, 0))


def _rows3(ref_or_val, st):
    return ref_or_val.reshape(st, TB, D)


def _acc_rows(acc_ref, val3, st, ctx_row):
    c, r = pl.program_id(0), pl.program_id(1)
    s = jnp.sum(val3, axis=1, keepdims=True)

    @pl.when(r == 0)
    def _():
        acc_ref[ctx_row:ctx_row + 1] += jnp.sum(s, axis=0, keepdims=True)

    @pl.when(r > 0)
    def _():
        acc_ref[pl.ds(c * st, st)] += s


def _first_step():
    return jnp.logical_and(pl.program_id(0) == 0, pl.program_id(1) == 0)


def _embed(lay, x, ctx, pe, side=None):
    st = ST_FWD
    bl, nlb = lay.bl, lay.nlb
    nc = bl // st

    def body(*refs):
        (x_ref, c_ref, pe_ref, o_ref), srefs = _side_split(refs, 3, 1, 0, side)
        c, r = pl.program_id(0), pl.program_id(1)
        _side_start(side, srefs, jnp.logical_and(c == 0, r == 0))

        @pl.when(r == 0)
        def _():
            o_ref[...] = c_ref[...].reshape(st * TB, D)

        @pl.when(r > 0)
        def _():
            o_ref[...] = (x_ref[...].reshape(st, TB, D) + pe_ref[...]).reshape(st * TB, D)

        _side_wait(side, srefs, jnp.logical_and(c == nc - 1, r == lay.nr - 1))

    sd = side if side is not None else _Side([])
    outs = pl.pallas_call(
        body, grid=(nc, lay.nr),
        in_specs=[_x_spec(lay, st), pl.BlockSpec((st, CTX, D), lambda c, r: (c, 0, 0)),
                  pl.BlockSpec((1, TB, D), lambda c, r: (jnp.maximum(r - 1, 0), 0, 0))] + sd.in_specs,
        out_specs=[_tok_spec(lay, st)] + sd.out_specs,
        out_shape=[jax.ShapeDtypeStruct((lay.nt, D), F32)] + sd.out_shape,
        scratch_shapes=sd.scratch if side is not None else [],
        compiler_params=_cp(("arbitrary", "arbitrary") if side is not None else ("parallel", "parallel")),
        name="embed")(x.reshape(bl, nlb, TB, D), ctx, pe.reshape(nlb, TB, D), *sd.arrays)
    return outs[0], list(outs[1:])


def _normmod_fwd(lay, x, gain, modt, ksh, ksc, name):
    st = ST_FWD

    def body(x_ref, g_ref, sh_ref, sc_ref, o_ref):
        xv = _rows3(x_ref[...], st)
        r = lax.rsqrt(jnp.mean(xv * xv, axis=-1, keepdims=True) + EPS)
        o_ref[...] = ((xv * r * g_ref[...]) * (1.0 + sc_ref[0]) + sh_ref[0]).reshape(st * TB, D).astype(o_ref.dtype)

    return pl.pallas_call(
        body, grid=(lay.bl // st, lay.nr),
        in_specs=[_tok_spec(lay, st), _vec_spec(), _mod_spec(st, ksh), _mod_spec(st, ksc)],
        out_specs=_tok_spec(lay, st), out_shape=jax.ShapeDtypeStruct((lay.nt, D), MXU_DTYPE),
        compiler_params=_cp(("parallel", "parallel")), name=name)(x, gain, modt, modt)


def _acc_out():
    return pl.BlockSpec((8, 1, D), lambda c, r: (0, 0, 0)), jax.ShapeDtypeStruct((8, 1, D), F32)


def _normmod_bwd(lay, x, dh, dx_in, gain, modt, ksc, name, latent_only=False):
    st = ST_BWD
    acc_spec, acc_shape = _acc_out()
    if latent_only:
        dx_spec, dx_shape = _x_spec(lay, st), jax.ShapeDtypeStruct((lay.bl, lay.nlb, TB, D), F32)
    else:
        dx_spec, dx_shape = _tok_spec(lay, st), jax.ShapeDtypeStruct((lay.nt, D), F32)

    def body(x_ref, dh_ref, dxi_ref, g_ref, sc_ref, dx_ref, dsh_ref, dsc_ref, dg_ref):
        xv = _rows3(x_ref[...], st)
        dhv = _rows3(dh_ref[...].astype(F32), st)
        g = g_ref[...]
        sc1 = 1.0 + sc_ref[0]
        r = lax.rsqrt(jnp.mean(xv * xv, axis=-1, keepdims=True) + EPS)
        xh = xv * r
        dxh = dhv * (g * sc1)
        dx = _rows3(dxi_ref[...], st) + r * (dxh - xh * jnp.mean(dxh * xh, axis=-1, keepdims=True))
        dx_ref[...] = dx.reshape(dx_ref.shape)

        @pl.when(_first_step())
        def _():
            dsh_ref[...] = jnp.zeros_like(dsh_ref)
            dsc_ref[...] = jnp.zeros_like(dsc_ref)
            dg_ref[...] = jnp.zeros_like(dg_ref)

        _acc_rows(dsh_ref, dhv, st, lay.ctx_row)
        _acc_rows(dsc_ref, dhv * (xh * g), st, lay.ctx_row)
        dg_ref[...] += jnp.sum((dhv * sc1 * xh).reshape(st * TB, D), axis=0, keepdims=True)

    return pl.pallas_call(
        body, grid=(lay.bl // st, lay.nr),
        in_specs=[_tok_spec(lay, st), _tok_spec(lay, st), _tok_spec(lay, st), _vec_spec(), _mod_spec(st, ksc)],
        out_specs=[dx_spec, acc_spec, acc_spec, _vec_spec()],
        out_shape=[dx_shape, acc_shape, acc_shape, jax.ShapeDtypeStruct((1, D), F32)],
        compiler_params=_cp(("arbitrary", "arbitrary")), name=name)(x, dh, dx_in, gain, modt)


def _resnorm_fwd(lay, x, m, gain, modt, kgate, name):
    st = ST_FWD

    def body(x_ref, m_ref, g_ref, gate_ref, o_ref):
        mv = _rows3(m_ref[...].astype(F32), st)
        r = lax.rsqrt(jnp.mean(mv * mv, axis=-1, keepdims=True) + EPS)
        o_ref[...] = x_ref[...] + (gate_ref[0] * (mv * r * g_ref[...])).reshape(st * TB, D)

    return pl.pallas_call(
        body, grid=(lay.bl // st, lay.nr),
        in_specs=[_tok_spec(lay, st), _tok_spec(lay, st), _vec_spec(), _mod_spec(st, kgate)],
        out_specs=_tok_spec(lay, st), out_shape=jax.ShapeDtypeStruct((lay.nt, D), F32),
        compiler_params=_cp(("parallel", "parallel")), name=name)(x, m, gain, modt)


def _resnorm_bwd(lay, dxn, m, gain, modt, kgate, name):
    st = ST_BWD
    acc_spec, acc_shape = _acc_out()

    def body(d_ref, m_ref, g_ref, gate_ref, dm_ref, dgate_ref, dg_ref):
        dv = _rows3(d_ref[...], st)
        mv = _rows3(m_ref[...].astype(F32), st)
        g = g_ref[...]
        r = lax.rsqrt(jnp.mean(mv * mv, axis=-1, keepdims=True) + EPS)
        xh = mv * r
        dy = dv * gate_ref[0]
        dxh = dy * g
        dm = r * (dxh - xh * jnp.mean(dxh * xh, axis=-1, keepdims=True))
        dm_ref[...] = dm.reshape(st * TB, D).astype(dm_ref.dtype)

        @pl.when(_first_step())
        def _():
            dgate_ref[...] = jnp.zeros_like(dgate_ref)
            dg_ref[...] = jnp.zeros_like(dg_ref)

        _acc_rows(dgate_ref, dv * (xh * g), st, lay.ctx_row)
        dg_ref[...] += jnp.sum((dy * xh).reshape(st * TB, D), axis=0, keepdims=True)

    return pl.pallas_call(
        body, grid=(lay.bl // st, lay.nr),
        in_specs=[_tok_spec(lay, st), _tok_spec(lay, st), _vec_spec(), _mod_spec(st, kgate)],
        out_specs=[_tok_spec(lay, st), acc_spec, _vec_spec()],
        out_shape=[jax.ShapeDtypeStruct((lay.nt, D), MXU_DTYPE), acc_shape, jax.ShapeDtypeStruct((1, D), F32)],
        compiler_params=_cp(("arbitrary", "arbitrary")), name=name)(dxn, m, gain, modt)


def _loss_bwd(lay, xf, tgt):
    st = ST_FWD

    def body(x_ref, t_ref, dx_ref, l_ref):
        r = pl.program_id(1)

        @pl.when(_first_step())
        def _():
            l_ref[...] = jnp.zeros_like(l_ref)

        @pl.when(r == 0)
        def _():
            dx_ref[...] = jnp.zeros_like(dx_ref)

        @pl.when(r > 0)
        def _():
            e = x_ref[...] - t_ref[...].reshape(st * TB, D)
            dx_ref[...] = e * (1.0 / D)
            l_ref[...] += jnp.sum(e * e) * (0.5 / D)

    return pl.pallas_call(
        body, grid=(lay.bl // st, lay.nr),
        in_specs=[_tok_spec(lay, st), _x_spec(lay, st)],
        out_specs=[_tok_spec(lay, st), pl.BlockSpec((8, 128), lambda c, r: (0, 0))],
        out_shape=[jax.ShapeDtypeStruct((lay.nt, D), F32), jax.ShapeDtypeStruct((8, 128), F32)],
        compiler_params=_cp(("arbitrary", "arbitrary")), name="loss")(xf, tgt.reshape(lay.bl, lay.nlb, TB, D))


FF_TN = D_FF // 2
FF_CHUNKS = ((0, 512), (512, 512), (1024, 384))


def _ffn_up(h, wgt, wut, name, side=None):
    m = h.shape[0]
    tm, tn = min(512, m), FF_TN
    ni, nj = m // tm, D_FF // tn

    def body(*refs):
        (h_ref, wg_ref, wu_ref, g_ref, u_ref, a_ref), srefs = _side_split(refs, 3, 3, 0, side)
        j, i = pl.program_id(0), pl.program_id(1)
        _side_start(side, srefs, jnp.logical_and(i == 0, j == 0))
        hv = h_ref[...]
        for c0, cw in FF_CHUNKS:
            g = _nt(hv, wg_ref[c0:c0 + cw, :])
            u = _nt(hv, wu_ref[c0:c0 + cw, :])
            g_ref[:, c0:c0 + cw] = g.astype(g_ref.dtype)
            u_ref[:, c0:c0 + cw] = u.astype(u_ref.dtype)
            a_ref[:, c0:c0 + cw] = (g * _sigmoid(g) * u).astype(a_ref.dtype)
        _side_wait(side, srefs, jnp.logical_and(i == ni - 1, j == nj - 1))

    sd = side if side is not None else _Side([])
    osp = pl.BlockSpec((tm, tn), lambda j, i: (i, j))
    osh = jax.ShapeDtypeStruct((m, D_FF), MXU_DTYPE)
    outs = pl.pallas_call(
        body, grid=(nj, ni),
        in_specs=[pl.BlockSpec((tm, D), lambda j, i: (i, 0)), pl.BlockSpec((tn, D), lambda j, i: (j, 0)),
                  pl.BlockSpec((tn, D), lambda j, i: (j, 0))] + sd.in_specs,
        out_specs=[osp, osp, osp] + sd.out_specs, out_shape=[osh, osh, osh] + sd.out_shape,
        scratch_shapes=sd.scratch if side is not None else [],
        compiler_params=_cp(("arbitrary", "arbitrary") if side is not None else ("parallel", "parallel")),
        name=name)(h, wgt, wut, *sd.arrays)
    return outs[0], outs[1], outs[2], list(outs[3:])


def _ffn_down_bwd(df, wd, g, u, name, side=None):
    m = df.shape[0]
    tm, tn = min(512, m), FF_TN
    ni, nj = m // tm, D_FF // tn

    def body(*refs):
        (df_ref, wd_ref, g_ref, u_ref, dg_ref, du_ref), srefs = _side_split(refs, 4, 2, 0, side)
        j, i = pl.program_id(0), pl.program_id(1)
        _side_start(side, srefs, jnp.logical_and(i == 0, j == 0))
        dfv = df_ref[...]
        for c0, cw in FF_CHUNKS:
            da = _nt(dfv, wd_ref[c0:c0 + cw, :])
            gv = g_ref[:, c0:c0 + cw].astype(F32)
            uv = u_ref[:, c0:c0 + cw].astype(F32)
            s = _sigmoid(gv)
            dg_ref[:, c0:c0 + cw] = (da * uv * (s * (1.0 + gv * (1.0 - s)))).astype(dg_ref.dtype)
            du_ref[:, c0:c0 + cw] = (da * gv * s).astype(du_ref.dtype)
        _side_wait(side, srefs, jnp.logical_and(i == ni - 1, j == nj - 1))

    sd = side if side is not None else _Side([])
    osp = pl.BlockSpec((tm, tn), lambda j, i: (i, j))
    osh = jax.ShapeDtypeStruct((m, D_FF), MXU_DTYPE)
    outs = pl.pallas_call(
        body, grid=(nj, ni),
        in_specs=[pl.BlockSpec((tm, D), lambda j, i: (i, 0)), pl.BlockSpec((tn, D), lambda j, i: (j, 0)), osp, osp]
        + sd.in_specs,
        out_specs=[osp, osp] + sd.out_specs, out_shape=[osh, osh] + sd.out_shape,
        scratch_shapes=sd.scratch if side is not None else [],
        compiler_params=_cp(("arbitrary", "arbitrary") if side is not None else ("parallel", "parallel")),
        name=name)(df, wd, g, u, *sd.arrays)
    return outs[0], outs[1], list(outs[2:])


def _head_masks(shape):
    lane = lax.broadcasted_iota(jnp.int32, shape, 1)
    return [jnp.logical_and(lane >= 64 * h, lane < 64 * h + 64) for h in range(4)]


def _head_mean(x, masks):
    out = jnp.zeros_like(x)
    for mk in masks:
        s = jnp.sum(jnp.where(mk, x, 0.0), axis=-1, keepdims=True) * (1.0 / 64.0)
        out = jnp.where(mk, s, out)
    return out


def _gate_common(z, masks):
    zg = _gelu(z)
    u = zg[:, :A_W]
    v = zg[:, A_W:]
    mu = _head_mean(v, masks)
    vc = v - mu
    rstd = lax.rsqrt(_head_mean(vc * vc, masks) + EPS)
    return u, vc * rstd, rstd


def _gate_s(vn, ws_ref, bias, masks):
    parts = []
    for c in range(TB // CHUNK):
        vc = vn[c * CHUNK:(c + 1) * CHUNK]
        s = bias
        for h in range(4):
            s = s + _nn(ws_ref[h], jnp.where(masks[h][:CHUNK], vc, 0.0).astype(MXU_DTYPE))
        parts.append(s)
    return jnp.concatenate(parts, axis=0)


MT = 4


def _blocks():
    return [pl.ds(s * TB, TB) for s in range(MT)]


def _gate_fwd(lay, z, ws, bias, name):
    def body(z_ref, ws_ref, b_ref, o_ref):
        masks = _head_masks((TB, A_W))
        for sl in _blocks():
            u, vn, _ = _gate_common(z_ref[sl, :].astype(F32), masks)
            o_ref[sl, :] = (u * _gate_s(vn, ws_ref, b_ref[...], masks)).astype(o_ref.dtype)

    return pl.pallas_call(
        body, grid=(lay.nb // MT,),
        in_specs=[pl.BlockSpec((MT * TB, 2 * A_W), lambda j: (j, 0)), pl.BlockSpec((4, CHUNK, CHUNK), lambda j: (0, 0, 0)),
                  pl.BlockSpec((CHUNK, A_W), lambda j: (0, 0))],
        out_specs=pl.BlockSpec((MT * TB, A_W), lambda j: (j, 0)),
        out_shape=jax.ShapeDtypeStruct((lay.nt, A_W), MXU_DTYPE),
        compiler_params=_cp(("parallel",)), name=name)(z, ws, bias)


def _gate_bwd(lay, z, da, ws, wst, bias, name):
    def body(z_ref, da_ref, ws_ref, wst_ref, b_ref, dz_ref, dws_ref, db_ref):
        j = pl.program_id(0)

        @pl.when(j == 0)
        def _():
            dws_ref[...] = jnp.zeros_like(dws_ref)
            db_ref[...] = jnp.zeros_like(db_ref)

        masks = _head_masks((TB, A_W))
        for blk in _blocks():
            zv = z_ref[blk, :].astype(F32)
            u, vn, rstd = _gate_common(zv, masks)
            s = _gate_s(vn, ws_ref, b_ref[...], masks)
            dav = da_ref[blk, :].astype(F32)
            du = dav * s
            ds = dav * u
            dvn_parts = []
            for c in range(TB // CHUNK):
                sl = slice(c * CHUNK, (c + 1) * CHUNK)
                ds_c = ds[sl]
                vn_c = vn[sl].astype(MXU_DTYPE)
                db_ref[...] += ds_c
                ds_b = ds_c.astype(MXU_DTYPE)
                dvn_c = jnp.zeros((CHUNK, A_W), F32)
                for h in range(4):
                    mk = masks[h][:CHUNK]
                    dws_ref[h] += _nt(jnp.where(mk, ds_c, 0.0).astype(MXU_DTYPE), vn_c)
                    dvn_c = dvn_c + jnp.where(mk, _nn(wst_ref[h], ds_b), 0.0)
                dvn_parts.append(dvn_c)
            dvn = jnp.concatenate(dvn_parts, axis=0)
            dv = rstd * (dvn - _head_mean(dvn, masks) - vn * _head_mean(dvn * vn, masks))
            gg = _gelu_grad(zv)
            dz_ref[blk, :A_W] = (du * gg[:, :A_W]).astype(dz_ref.dtype)
            dz_ref[blk, A_W:] = (dv * gg[:, A_W:]).astype(dz_ref.dtype)

    return pl.pallas_call(
        body, grid=(lay.nb // MT,),
        in_specs=[pl.BlockSpec((MT * TB, 2 * A_W), lambda j: (j, 0)), pl.BlockSpec((MT * TB, A_W), lambda j: (j, MCAT_A)),
                  pl.BlockSpec((4, CHUNK, CHUNK), lambda j: (0, 0, 0)), pl.BlockSpec((4, CHUNK, CHUNK), lambda j: (0, 0, 0)),
                  pl.BlockSpec((CHUNK, A_W), lambda j: (0, 0))],
        out_specs=[pl.BlockSpec((MT * TB, 2 * A_W), lambda j: (j, 0)), pl.BlockSpec((4, CHUNK, CHUNK), lambda j: (0, 0, 0)),
                   pl.BlockSpec((CHUNK, A_W), lambda j: (0, 0))],
        out_shape=[jax.ShapeDtypeStruct((lay.nt, 2 * A_W), MXU_DTYPE), jax.ShapeDtypeStruct((4, CHUNK, CHUNK), F32),
                   jax.ShapeDtypeStruct((CHUNK, A_W), F32)],
        compiler_params=_cp(("arbitrary",)), name=name)(z, da, ws, wst, bias)


def _band_constants():
    bands = np.zeros((2, 4, TB, TB), np.float32)
    inv = np.zeros((2, 4, TB, 1), np.float32)
    for kind, n in ((0, GRID_W), (1, TB)):
        for i, w in enumerate(POOL_WINDOWS):
            for t in range(TB):
                base, tt = (t // n) * n, t % n
                lo = min(max(tt - w // 2, 0), n)
                hi = min(max(tt - w // 2 + w, 0), n)
                bands[kind, i, t, base + lo:base + hi] = 1.0
                inv[kind, i, t, 0] = 1.0 / (hi - lo)
    return bands, inv


def _split3(x):
    a = x.astype(MXU_DTYPE)
    r1 = x - a.astype(F32)
    b = r1.astype(MXU_DTYPE)
    c = (r1 - b.astype(F32)).astype(MXU_DTYPE)
    return a, b, c


def _window_apply(band_ref, inv_ref, x, masks, transpose, mxu_exact=False):
    out = jnp.zeros_like(x)
    for i in range(4):
        xi = x * inv_ref[0, i] if transpose else x
        acc = None
        for part in ((xi.astype(MXU_DTYPE),) if mxu_exact else _split3(xi)):
            r = _tn(band_ref[0, i], part) if transpose else _nn(band_ref[0, i], part)
            acc = r if acc is None else acc + r
        if not transpose:
            acc = acc * inv_ref[0, i]
        out = jnp.where(masks[i], acc, out)
    return out


def _pool_specs(lay):
    kind = lambda j: jnp.where(j < lay.nctx // MT, 1, 0)
    return [pl.BlockSpec((1, 4, TB, TB), lambda j: (kind(j), 0, 0, 0)), pl.BlockSpec((1, 4, TB, 1), lambda j: (kind(j), 0, 0, 0))]


def _pool_fwd(lay, z, bands, inv, pw, scale, name):
    def body(p_ref, band_ref, inv_ref, pw_ref, sc_ref, o_ref):
        masks = _head_masks((TB, C_W))
        for blk in _blocks():
            p = p_ref[blk, :].astype(F32)
            diff = _window_apply(band_ref, inv_ref, p, masks, False, mxu_exact=True) - p
            o_ref[blk, :] = (_nn(diff.astype(MXU_DTYPE), pw_ref[...]) * sc_ref[...]).astype(o_ref.dtype)

    return pl.pallas_call(
        body, grid=(lay.nb // MT,),
        in_specs=[pl.BlockSpec((MT * TB, C_W), lambda j: (j, 4))] + _pool_specs(lay)
        + [pl.BlockSpec((C_W, C_W), lambda j: (0, 0)), pl.BlockSpec((1, C_W), lambda j: (0, 0))],
        out_specs=pl.BlockSpec((MT * TB, C_W), lambda j: (j, 0)),
        out_shape=jax.ShapeDtypeStruct((lay.nt, C_W), MXU_DTYPE),
        compiler_params=_cp(("parallel",)), name=name)(z, bands, inv, pw, scale)


def _pool_bwd(lay, z, dc, bands, inv, pw, scale, name):
    def body(p_ref, dc_ref, band_ref, inv_ref, pw_ref, sc_ref, dp_ref, dpw_ref, dsc_ref):
        j = pl.program_id(0)

        @pl.when(j == 0)
        def _():
            dpw_ref[...] = jnp.zeros_like(dpw_ref)
            dsc_ref[...] = jnp.zeros_like(dsc_ref)

        masks = _head_masks((TB, C_W))
        for blk in _blocks():
            p = p_ref[blk, :].astype(F32)
            dcv = dc_ref[blk, :].astype(F32)
            diff = _window_apply(band_ref, inv_ref, p, masks, False, mxu_exact=True) - p
            diff_b = diff.astype(MXU_DTYPE)
            pre = _nn(diff_b, pw_ref[...])
            dsc_ref[...] += jnp.sum(dcv * pre, axis=0, keepdims=True)
            dpre = dcv * sc_ref[...]
            dpre_b = dpre.astype(MXU_DTYPE)
            dpw_ref[...] += _tn(diff_b, dpre_b)
            ddiff = _nt(dpre_b, pw_ref[...])
            dp_ref[blk, :] = (_window_apply(band_ref, inv_ref, ddiff, masks, True) - ddiff).astype(dp_ref.dtype)

    return pl.pallas_call(
        body, grid=(lay.nb // MT,),
        in_specs=[pl.BlockSpec((MT * TB, C_W), lambda j: (j, 4)), pl.BlockSpec((MT * TB, C_W), lambda j: (j, MCAT_C))]
        + _pool_specs(lay)
        + [pl.BlockSpec((C_W, C_W), lambda j: (0, 0)), pl.BlockSpec((1, C_W), lambda j: (0, 0))],
        out_specs=[pl.BlockSpec((MT * TB, C_W), lambda j: (j, 0)), pl.BlockSpec((C_W, C_W), lambda j: (0, 0)),
                   pl.BlockSpec((1, C_W), lambda j: (0, 0))],
        out_shape=[jax.ShapeDtypeStruct((lay.nt, C_W), MXU_DTYPE), jax.ShapeDtypeStruct((C_W, C_W), F32),
                   jax.ShapeDtypeStruct((1, C_W), F32)],
        compiler_params=_cp(("arbitrary",)), name=name)(z, dc, bands, inv, pw, scale)


def _disc_math(lr, li, ldt, br, bi):
    dt = jnp.exp(ldt)
    e = jnp.exp(lr * dt)
    ar = e * jnp.cos(li * dt)
    ai = e * jnp.sin(li * dt)
    nr, ni = ar - 1.0, ai
    den = lr * lr + li * li
    qr = (nr * lr + ni * li) / den
    qi = (ni * lr - nr * li) / den
    return ar, ai, qr * br - qi * bi, qr * bi + qi * br


def _disc_fwd(lrx, lix, ldtx, brt, bit, name):
    def body(lr_ref, li_ref, ldt_ref, br_ref, bi_ref, ar_ref, ai_ref, obr_ref, obi_ref):
        ar, ai, obr, obi = _disc_math(lr_ref[...], li_ref[...], ldt_ref[...], br_ref[...], bi_ref[...])
        ar_ref[...] = ar
        ai_ref[...] = ai
        obr_ref[...] = obr
        obi_ref[...] = obi

    sh = jax.ShapeDtypeStruct(lrx.shape, F32)
    return pl.pallas_call(body, out_shape=[sh, sh, sh, sh], name=name)(lrx, lix, ldtx, brt, bit)


def _disc_bwd(lrx, lix, ldtx, brt, bit, dar, dai, dbr, dbi, name):
    nrow = lrx.shape[0] // SSM_H

    def body(lr_ref, li_ref, ldt_ref, br_ref, bi_ref, dar_ref, dai_ref, dbr_ref, dbi_ref,
             glr_ref, gli_ref, gdt_ref, gbr_ref, gbi_ref):
        _, vjp = jax.vjp(_disc_math, lr_ref[...], li_ref[...], ldt_ref[...], br_ref[...], bi_ref[...])
        glr, gli, gdt, gbr, gbi = vjp((dar_ref[...], dai_ref[...], dbr_ref[...], dbi_ref[...]))
        glr_ref[...] = jnp.sum(glr.reshape(nrow, SSM_H, SSM_P), axis=1)
        gli_ref[...] = jnp.sum(gli.reshape(nrow, SSM_H, SSM_P), axis=1)
        gdt_ref[...] = jnp.sum(jnp.sum(gdt.reshape(nrow, SSM_H, SSM_P), axis=1), axis=-1, keepdims=True)
        gbr_ref[...] = gbr
        gbi_ref[...] = gbi

    small = jax.ShapeDtypeStruct((nrow, SSM_P), F32)
    big = jax.ShapeDtypeStruct(lrx.shape, F32)
    return pl.pallas_call(body, out_shape=[small, small, jax.ShapeDtypeStruct((nrow, 1), F32), big, big],
                          name=name)(lrx, lix, ldtx, brt, bit, dar, dai, dbr, dbi)


HS = 1024
GQ, QC, QS = 8, 128, 512
LC = QS
SCAN_UNROLL = ST


def _scan_steps(step, carry):
    if SCAN_UNROLL >= ST:
        for s in range(ST):
            carry = step(s, carry)
        return carry

    def body(i, c):
        for j in range(SCAN_UNROLL):
            c = step(i * SCAN_UNROLL + j, c)
        return c

    return lax.fori_loop(0, ST // SCAN_UNROLL, body, carry)


def _tile_row(s):
    return s * 8 if isinstance(s, int) else pl.multiple_of(s * 8, 8)


def _dir_cat(x, d0, qq):
    xq = x[:, QC * qq:QC * qq + QC]
    zero = jnp.zeros_like(xq)
    return jnp.concatenate([jnp.where(d0, xq, zero), jnp.where(d0, zero, xq)], axis=1)


def _dir_pick(x, d0):
    return jnp.where(d0, x[:, :QC], x[:, QC:])


def _d0_rows(n):
    row = lax.broadcasted_iota(jnp.int32, (n, 1), 0)
    return jnp.bitwise_and(row, 4) == 0


def _scan_perm(bl):
    n = 2 * bl * ST
    p = np.zeros((n, n), np.float32)
    for s in range(ST):
        for d in range(2):
            for b in range(bl):
                t = s if d == 0 else ST - 1 - s
                p[s * 2 * bl + d * bl + b, d * bl * ST + b * ST + t] = 1.0
    return p


def _scan_maps(lay):
    spc = TB // ST
    nlc = lay.nlb * spc

    def fwd(k):
        return k // spc, k % spc

    def rev(k):
        cpos = nlc - 1 - jnp.maximum(k - spc, 0)
        return jnp.where(k < spc, 0, 1 + cpos // spc), jnp.where(k < spc, spc - 1 - k, cpos % spc)

    return fwd, rev


def _pack_rows(f_ref, r_ref, p_ref, rc):
    st = jnp.concatenate([f_ref[0].reshape(rc // 2, 256), r_ref[0].reshape(rc // 2, 256)], axis=0).astype(MXU_DTYPE)
    return _nn(p_ref[...], st).astype(MXU_DTYPE)


def _side_split(refs, n_in, n_out, n_scr, side):
    ns = side.n if side is not None else 0
    ins, sin = refs[:n_in], refs[n_in:n_in + ns]
    o0 = n_in + ns
    outs, sout = refs[o0:o0 + n_out], refs[o0 + n_out:o0 + n_out + ns]
    s0 = o0 + n_out + ns
    return ins + outs + refs[s0:s0 + n_scr], (sin, sout, refs[s0 + n_scr:])


def _side_start(side, srefs, first):
    if side is not None:
        @pl.when(first)
        def _():
            side.start(*srefs)


def _side_wait(side, srefs, last):
    if side is not None:
        @pl.when(last)
        def _():
            side.wait(*srefs)


def _ssm_fwd(lay, z, perm, bh, ch, ar8, ai8, name, side=None):
    bl = lay.bl
    rc = ST * 2 * bl
    nch = lay.nr * (TB // ST)
    fwd, rev = _scan_maps(lay)
    z4 = z.reshape(lay.nr, bl, TB, z.shape[1])

    def body(*refs):
        own, srefs = _side_split(refs, 7, 3, 2, side)
        uf_ref, ur_ref, p_ref, bh_ref, ch_ref, ar_ref, ai_ref, yf_ref, yr_ref, hst_ref, hs, hc = own
        f, k = pl.program_id(0), pl.program_id(1)
        _side_start(side, srefs, jnp.logical_and(f == 0, k == 0))

        @pl.when(k == 0)
        def _():
            hc[...] = jnp.zeros_like(hc)

        hst_ref[0] = hc[...]
        d0 = _d0_rows(rc)
        uv = _pack_rows(uf_ref, ur_ref, p_ref, rc)
        for q in range(2):
            cr, ci = 2 * QS * q, 2 * QS * q + QS
            hs[:, cr:cr + 2 * QS] = _nn(_dir_cat(uv, d0, q), bh_ref[q])
            ar = ar_ref[:, QS * q:QS * q + QS]
            ai = ai_ref[:, QS * q:QS * q + QS]

            def step(s, carry, cr=cr, ci=ci, ar=ar, ai=ai):
                hr, hi = carry
                base = _tile_row(s)
                nr = ar * hr - ai * hi + hs[pl.ds(base, 8), cr:cr + LC]
                ni = ar * hi + ai * hr + hs[pl.ds(base, 8), ci:ci + LC]
                hs[pl.ds(base, 8), cr:cr + LC] = nr
                hs[pl.ds(base, 8), ci:ci + LC] = ni
                return nr, ni

            hr, hi = _scan_steps(step, (hc[:, cr:cr + LC], hc[:, ci:ci + LC]))
            hc[:, cr:cr + LC] = hr
            hc[:, ci:ci + LC] = hi
        yi = jnp.concatenate(
            [_dir_pick(_nn(hs[:, 2 * QS * q:2 * QS * (q + 1)].astype(MXU_DTYPE), ch_ref[q]), d0) for q in range(2)], axis=1)
        yd = _tn(p_ref[...], yi.astype(MXU_DTYPE))
        yf_ref[0] = yd[:rc // 2].reshape(bl, ST, 256).astype(yf_ref.dtype)
        yr_ref[0] = yd[rc // 2:].reshape(bl, ST, 256).astype(yr_ref.dtype)
        _side_wait(side, srefs, jnp.logical_and(f == 1, k == nch - 1))

    sd = side if side is not None else _Side([])
    blk = (1, bl, ST, 256)
    ysh = jax.ShapeDtypeStruct((lay.nr, bl, TB, B_W), MXU_DTYPE)
    outs = pl.pallas_call(
        body, grid=(2, nch),
        in_specs=[pl.BlockSpec(blk, lambda f, k: (fwd(k)[0], 0, fwd(k)[1], 2 + f)),
                  pl.BlockSpec(blk, lambda f, k: (rev(k)[0], 0, rev(k)[1], 2 + f)),
                  pl.BlockSpec((rc, rc), lambda f, k: (0, 0)),
                  pl.BlockSpec((2, 2 * QC, 2 * QS), lambda f, k: (f, 0, 0)),
                  pl.BlockSpec((2, 2 * QS, 2 * QC), lambda f, k: (f, 0, 0)),
                  pl.BlockSpec((8, HS), lambda f, k: (0, f)), pl.BlockSpec((8, HS), lambda f, k: (0, f))] + sd.in_specs,
        out_specs=[pl.BlockSpec(blk, lambda f, k: (fwd(k)[0], 0, fwd(k)[1], f)),
                   pl.BlockSpec(blk, lambda f, k: (rev(k)[0], 0, rev(k)[1], f)),
                   pl.BlockSpec((1, 8, 2 * HS), lambda f, k: (k, 0, f))] + sd.out_specs,
        out_shape=[ysh, ysh, jax.ShapeDtypeStruct((nch, 8, 4 * HS), F32)] + sd.out_shape,
        scratch_shapes=[pltpu.VMEM((rc, 2 * HS), F32), pltpu.VMEM((8, 2 * HS), F32)] + (sd.scratch if side is not None else []),
        compiler_params=_cp(("arbitrary", "arbitrary")), name=name)(z4, z4, perm, bh, ch, ar8, ai8, *sd.arrays)
    yf, yr, hst = outs[:3]
    return yf.reshape(lay.nt, B_W), yr.reshape(lay.nt, B_W), hst, list(outs[3:])


def _ssm_bwd(lay, z, dy, perm, hst, bh, ch, ar8, ai8, name, side=None):
    bl = lay.bl
    rc = ST * 2 * bl
    nch = lay.nr * (TB // ST)
    fwd, rev = _scan_maps(lay)
    z4 = z.reshape(lay.nr, bl, TB, z.shape[1])
    dy4 = dy.reshape(lay.nr, bl, TB, B_W)

    def body(*refs):
        own, srefs = _side_split(refs, 10, 6, 5, side)
        (uf_ref, ur_ref, dyf_ref, dyr_ref, p_ref, hst_ref, bh_ref, ch_ref, ar_ref, ai_ref,
         duf_ref, dur_ref, dbh_ref, dch_ref, dar_ref, dai_ref, hs, es, ec, accr, acci) = own
        f, k = pl.program_id(0), pl.program_id(1)
        _side_start(side, srefs, jnp.logical_and(f == 0, k == 0))

        @pl.when(k == 0)
        def _():
            ec[...] = jnp.zeros_like(ec)
            accr[...] = jnp.zeros_like(accr)
            acci[...] = jnp.zeros_like(acci)
            dbh_ref[...] = jnp.zeros_like(dbh_ref)
            dch_ref[...] = jnp.zeros_like(dch_ref)

        d0 = _d0_rows(rc)
        uv = _pack_rows(uf_ref, ur_ref, p_ref, rc)
        dyv = _pack_rows(dyf_ref, dyr_ref, p_ref, rc)

        hs[0:8, :] = hst_ref[0]
        ucat, dycat = [], []
        for q in range(2):
            cr, ci = 2 * QS * q, 2 * QS * q + QS
            ucat.append(_dir_cat(uv, d0, q))
            dycat.append(_dir_cat(dyv, d0, q))
            hs[8:, cr:cr + 2 * QS] = _nn(ucat[q], bh_ref[q])
            ar = ar_ref[:, QS * q:QS * q + QS]
            ai = ai_ref[:, QS * q:QS * q + QS]

            def step(s, carry, cr=cr, ci=ci, ar=ar, ai=ai):
                hr, hi = carry
                base = _tile_row(s + 1)
                nr = ar * hr - ai * hi + hs[pl.ds(base, 8), cr:cr + LC]
                ni = ar * hi + ai * hr + hs[pl.ds(base, 8), ci:ci + LC]
                hs[pl.ds(base, 8), cr:cr + LC] = nr
                hs[pl.ds(base, 8), ci:ci + LC] = ni
                return nr, ni

            _scan_steps(step, (hs[0:8, cr:cr + LC], hs[0:8, ci:ci + LC]))
            dch_ref[q] += _tn(hs[8:, cr:cr + 2 * QS].astype(MXU_DTYPE), dycat[q])
            es[:, cr:cr + 2 * QS] = _nt(dycat[q], ch_ref[q])

        dui = []
        for q in range(2):
            cr, ci = 2 * QS * q, 2 * QS * q + QS
            ar = ar_ref[:, QS * q:QS * q + QS]
            ai = ai_ref[:, QS * q:QS * q + QS]

            def bstep(i, carry, cr=cr, ci=ci, ar=ar, ai=ai):
                er, ei, sr, si = carry
                base = _tile_row(ST - 1 - i)
                ner = es[pl.ds(base, 8), cr:cr + LC] + ar * er + ai * ei
                nei = es[pl.ds(base, 8), ci:ci + LC] - ai * er + ar * ei
                es[pl.ds(base, 8), cr:cr + LC] = ner
                es[pl.ds(base, 8), ci:ci + LC] = nei
                hpr = hs[pl.ds(base, 8), cr:cr + LC]
                hpi = hs[pl.ds(base, 8), ci:ci + LC]
                return ner, nei, sr + ner * hpr + nei * hpi, si - ner * hpi + nei * hpr

            lo = QS * q
            er, ei, sr, si = _scan_steps(
                bstep, (ec[:, cr:cr + LC], ec[:, ci:ci + LC], accr[:, lo:lo + LC], acci[:, lo:lo + LC]))
            ec[:, cr:cr + LC] = er
            ec[:, ci:ci + LC] = ei
            accr[:, lo:lo + LC] = sr
            acci[:, lo:lo + LC] = si
            eb = es[:, cr:cr + 2 * QS].astype(MXU_DTYPE)
            dui.append(_dir_pick(_nt(eb, bh_ref[q]), d0))
            dbh_ref[q] += _tn(ucat[q], eb)

        dud = _tn(p_ref[...], jnp.concatenate(dui, axis=1).astype(MXU_DTYPE))
        duf_ref[0] = dud[:rc // 2].reshape(bl, ST, 256).astype(duf_ref.dtype)
        dur_ref[0] = dud[rc // 2:].reshape(bl, ST, 256).astype(dur_ref.dtype)

        @pl.when(k == nch - 1)
        def _():
            for d in range(2):
                dar_ref[d:d + 1, :] = jnp.sum(accr[4 * d:4 * d + 4, :], axis=0, keepdims=True)
                dai_ref[d:d + 1, :] = jnp.sum(acci[4 * d:4 * d + 4, :], axis=0, keepdims=True)

        _side_wait(side, srefs, jnp.logical_and(f == 1, k == nch - 1))

    sd = side if side is not None else _Side([])
    last = lambda k: nch - 1 - k
    blk = (1, bl, ST, 256)
    fspec = lambda c0: pl.BlockSpec(blk, lambda f, k: (fwd(last(k))[0], 0, fwd(last(k))[1], c0 + f))
    rspec = lambda c0: pl.BlockSpec(blk, lambda f, k: (rev(last(k))[0], 0, rev(last(k))[1], c0 + f))
    dush = jax.ShapeDtypeStruct((lay.nr, bl, TB, B_W), MXU_DTYPE)
    outs = pl.pallas_call(
        body, grid=(2, nch),
        in_specs=[fspec(2), rspec(2), fspec(0), rspec(0),
                  pl.BlockSpec((rc, rc), lambda f, k: (0, 0)),
                  pl.BlockSpec((1, 8, 2 * HS), lambda f, k: (last(k), 0, f)),
                  pl.BlockSpec((2, 2 * QC, 2 * QS), lambda f, k: (f, 0, 0)),
                  pl.BlockSpec((2, 2 * QS, 2 * QC), lambda f, k: (f, 0, 0)),
                  pl.BlockSpec((8, HS), lambda f, k: (0, f)), pl.BlockSpec((8, HS), lambda f, k: (0, f))] + sd.in_specs,
        out_specs=[fspec(0), rspec(0),
                   pl.BlockSpec((2, 2 * QC, 2 * QS), lambda f, k: (f, 0, 0)),
                   pl.BlockSpec((2, 2 * QS, 2 * QC), lambda f, k: (f, 0, 0)),
                   pl.BlockSpec((2, HS), lambda f, k: (0, f)), pl.BlockSpec((2, HS), lambda f, k: (0, f))] + sd.out_specs,
        out_shape=[dush, dush, jax.ShapeDtypeStruct((4, 2 * QC, 2 * QS), F32),
                   jax.ShapeDtypeStruct((4, 2 * QS, 2 * QC), F32), jax.ShapeDtypeStruct((2, 2 * HS), F32),
                   jax.ShapeDtypeStruct((2, 2 * HS), F32)] + sd.out_shape,
        scratch_shapes=[pltpu.VMEM((rc + 8, 2 * HS), F32), pltpu.VMEM((rc, 2 * HS), F32), pltpu.VMEM((8, 2 * HS), F32),
                        pltpu.VMEM((8, HS), F32), pltpu.VMEM((8, HS), F32)] + (sd.scratch if side is not None else []),
        compiler_params=_cp(("arbitrary", "arbitrary")), name=name)(z4, z4, dy4, dy4, perm, hst, bh, ch, ar8, ai8, *sd.arrays)
    duf, dur, dbh, dch, dar, dai = outs[:6]
    return duf.reshape(lay.nt, B_W), dur.reshape(lay.nt, B_W), dbh, dch, dar, dai, list(outs[6:])


def _glu_fwd(lay, z, yf, yr, dvec, wglu, bglu, name):
    def body(u_ref, yf_ref, yr_ref, d_ref, w_ref, b_ref, o_ref, y_ref):
        y = yf_ref[...].astype(F32) + yr_ref[...].astype(F32) + d_ref[...] * u_ref[...].astype(F32)
        y_ref[...] = y
        g = _gelu(y)
        pre = _nn(g.astype(MXU_DTYPE), w_ref[...]) + b_ref[...]
        o_ref[...] = (g * _sigmoid(pre)).astype(o_ref.dtype)

    tok = pl.BlockSpec((MT * TB, B_W), lambda j: (j, 0))
    vec = pl.BlockSpec((1, B_W), lambda j: (0, 0))
    return pl.pallas_call(
        body, grid=(lay.nb // MT,),
        in_specs=[pl.BlockSpec((MT * TB, B_W), lambda j: (j, 1)), tok, tok, vec,
                  pl.BlockSpec((B_W, B_W), lambda j: (0, 0)), vec],
        out_specs=[tok, tok],
        out_shape=[jax.ShapeDtypeStruct((lay.nt, B_W), MXU_DTYPE), jax.ShapeDtypeStruct((lay.nt, B_W), F32)],
        compiler_params=_cp(("parallel",)), name=name)(z, yf, yr, dvec, wglu, bglu)


def _glu_bwd(lay, z, y, ds, dvec, wglu, bglu, name):
    def body(u_ref, y_ref, ds_ref, d_ref, w_ref, b_ref, dy_ref, dud_ref, dw_ref, db_ref, dd_ref):
        j = pl.program_id(0)

        @pl.when(j == 0)
        def _():
            dw_ref[...] = jnp.zeros_like(dw_ref)
            db_ref[...] = jnp.zeros_like(db_ref)
            dd_ref[...] = jnp.zeros_like(dd_ref)

        yv = y_ref[...]
        g = _gelu(yv)
        gb = g.astype(MXU_DTYPE)
        sg = _sigmoid(_nn(gb, w_ref[...]) + b_ref[...])
        dsv = ds_ref[...].astype(F32)
        dpre = dsv * g * sg * (1.0 - sg)
        dpre_b = dpre.astype(MXU_DTYPE)
        dg = dsv * sg + _nt(dpre_b, w_ref[...])
        dw_ref[...] += _tn(gb, dpre_b)
        db_ref[...] += jnp.sum(dpre, axis=0, keepdims=True)
        dy = dg * _gelu_grad(yv)
        dy_ref[...] = dy.astype(dy_ref.dtype)
        dd_ref[...] += jnp.sum(dy * u_ref[...].astype(F32), axis=0, keepdims=True)
        dud_ref[...] = (dy * d_ref[...]).astype(dud_ref.dtype)

    tok = pl.BlockSpec((MT * TB, B_W), lambda j: (j, 0))
    vec = pl.BlockSpec((1, B_W), lambda j: (0, 0))
    mat = pl.BlockSpec((B_W, B_W), lambda j: (0, 0))
    vsh = jax.ShapeDtypeStruct((1, B_W), F32)
    return pl.pallas_call(
        body, grid=(lay.nb // MT,),
        in_specs=[pl.BlockSpec((MT * TB, B_W), lambda j: (j, 1)), tok, tok, vec, mat, vec],
        out_specs=[tok, tok, mat, vec, vec],
        out_shape=[jax.ShapeDtypeStruct((lay.nt, B_W), MXU_DTYPE), jax.ShapeDtypeStruct((lay.nt, B_W), F32),
                   jax.ShapeDtypeStruct((B_W, B_W), F32), vsh, vsh],
        compiler_params=_cp(("arbitrary",)), name=name)(z, y, ds, dvec, wglu, bglu)


def _dz_assemble(lay, dz_a, duf, dur, dud, dz_p, name):
    def body(a_ref, f_ref, r_ref, d_ref, p_ref, o_ref):
        o_ref[:, :2 * A_W] = a_ref[...].astype(o_ref.dtype)
        o_ref[:, 2 * A_W:2 * A_W + B_W] = (f_ref[...].astype(F32) + r_ref[...].astype(F32) + d_ref[...]).astype(o_ref.dtype)
        o_ref[:, 2 * A_W + B_W:] = p_ref[...].astype(o_ref.dtype)

    spec = lambda w: pl.BlockSpec((MT * TB, w), lambda j: (j, 0))
    return pl.pallas_call(
        body, grid=(lay.nb // MT,), in_specs=[spec(2 * A_W), spec(B_W), spec(B_W), spec(B_W), spec(C_W)],
        out_specs=spec(D_IN), out_shape=jax.ShapeDtypeStruct((lay.nt, D_IN), MXU_DTYPE),
        compiler_params=_cp(("parallel",)), name=name)(dz_a, duf, dur, dud, dz_p)


def _expand_rows(a):
    return jnp.broadcast_to(a[:, :, None, :], (2, SSM_G, SSM_H, SSM_P)).reshape(-1, SSM_P)


def _ssm_params(lam_re, lam_im, log_dt, b_re, b_im, c_re, c_im, name):
    lrx, lix = _expand_rows(lam_re), _expand_rows(lam_im)
    ldtx = _expand_rows(jnp.broadcast_to(log_dt[:, :, None], (2, SSM_G, SSM_P)))
    brt = jnp.transpose(b_re, (0, 1, 3, 2)).reshape(-1, SSM_P)
    bit = jnp.transpose(b_im, (0, 1, 3, 2)).reshape(-1, SSM_P)
    arx, aix, bbr, bbi = _disc_fwd(lrx, lix, ldtx, brt, bit, name)
    ar = arx.reshape(2, SSM_G, SSM_H, SSM_P)[:, :, 0].reshape(2, SSM_G * SSM_P)
    ai = aix.reshape(2, SSM_G, SSM_H, SSM_P)[:, :, 0].reshape(2, SSM_G * SSM_P)
    eye = jnp.eye(GQ, dtype=F32)

    def bmat(bt):
        t = bt.reshape(2, 4, GQ, SSM_H, SSM_P)
        return jnp.einsum('dqghp,gk->qdghkp', t, eye).reshape(4, 2 * QC, QS)

    bh = jnp.concatenate([bmat(bbr), bmat(bbi)], axis=-1).astype(MXU_DTYPE)

    def cmat(c):
        t = c.reshape(2, 4, GQ, SSM_H, SSM_P)
        return jnp.einsum('dqghp,gk->qgpdkh', t, eye).reshape(4, QS, 2 * QC)

    ch = jnp.concatenate([cmat(c_re), -cmat(c_im)], axis=1).astype(MXU_DTYPE)

    def rows8(a):
        return jnp.repeat(a, 4, axis=0)

    return dict(lrx=lrx, lix=lix, ldtx=ldtx, brt=brt, bit=bit, bh=bh, ch=ch, ar8=rows8(ar), ai8=rows8(ai))


def _ssm_param_grads(sp, dbh, dch, dar, dai, name):
    def bdiag(m):
        t = m.reshape(4, 2, GQ, SSM_H, GQ, SSM_P)
        return jnp.einsum('qdghgp->dqghp', t).reshape(-1, SSM_P)

    dbr, dbi = bdiag(dbh[..., :QS]), bdiag(dbh[..., QS:])

    def cdiag(m):
        t = m.reshape(4, GQ, SSM_P, 2, GQ, SSM_H)
        return jnp.einsum('qgpdgh->dqghp', t).reshape(2, SSM_G, SSM_H, SSM_P)

    dc_re, dc_im = cdiag(dch[:, :QS]), -cdiag(dch[:, QS:])

    def hrow(a):
        t = a.reshape(2, SSM_G, 1, SSM_P)
        return jnp.concatenate([t, jnp.zeros((2, SSM_G, SSM_H - 1, SSM_P), F32)], axis=2).reshape(-1, SSM_P)

    glr, gli, gdt, gbr, gbi = _disc_bwd(sp["lrx"], sp["lix"], sp["ldtx"], sp["brt"], sp["bit"],
                                        hrow(dar), hrow(dai), dbr, dbi, name)
    to_b = lambda g: jnp.transpose(g.reshape(2, SSM_G, SSM_H, SSM_P), (0, 1, 3, 2))
    return dict(ssm_lam_re=glr.reshape(2, SSM_G, SSM_P), ssm_lam_im=gli.reshape(2, SSM_G, SSM_P),
                ssm_log_dt=gdt.reshape(2, SSM_G), ssm_b_re=to_b(gbr), ssm_b_im=to_b(gbi),
                ssm_c_re=dc_re, ssm_c_im=dc_im)


def _layer_consts(p):
    c = {}
    c["ws"] = p["sgu_w"].astype(MXU_DTYPE)
    c["wst"] = jnp.transpose(p["sgu_w"], (0, 2, 1)).astype(MXU_DTYPE)
    c["gbias"] = jnp.repeat(p["sgu_b"].T, 64, axis=1)
    pw = jnp.zeros((C_W, C_W), F32)
    for i in range(4):
        pw = pw.at[64 * i:64 * i + 64, 64 * i:64 * i + 64].set(p["pool_w"][i])
    c["pw"] = pw.astype(MXU_DTYPE)
    c["pscale"] = p["pool_scale"].reshape(1, C_W)
    c["dvec"] = p["ssm_d"].reshape(1, B_W)
    c["bglu"] = p["glu_b"].reshape(1, B_W)
    return c


def _layer_fwd(lay, i, x, modarr, p, w, cst, sp, bands, inv, perm, sides=None):
    n = f"l{i}_"
    sides = sides or {}
    win_side, win_fill = sides.get("win", (None, None))
    ssm_side, ssm_fill = sides.get("ssm", (None, None))
    ffn_side, ffn_fill = sides.get("ffn", (None, None))
    res = {"x0": x}
    h = _normmod_fwd(lay, x, p["norm_mix_pre"].reshape(1, D), modarr, 0, 1, n + "nm1")
    z = _mm([(h, w["win_t"])], True, MXU_DTYPE, n + "win", side=win_side)
    if win_side is not None:
        z, extra = z
        win_fill(extra)
    a = _gate_fwd(lay, z, cst["ws"], cst["gbias"], n + "gate")
    yf, yr, hst, extra = _ssm_fwd(lay, z, perm, sp["bh"], sp["ch"], sp["ar8"], sp["ai8"], n + "ssm", ssm_side)
    if ssm_side is not None:
        ssm_fill(extra)
    s, y = _glu_fwd(lay, z, yf, yr, cst["dvec"], w["wglu"], cst["bglu"], n + "glu")
    c = _pool_fwd(lay, z, bands, inv, cst["pw"], cst["pscale"], n + "pool")
    mcat = jnp.concatenate([s, a, c], axis=1)
    res["wout_p"] = _perm_wout(w["wout"])
    m = _mm([(mcat, res["wout_p"])], False, MXU_DTYPE, n + "wout")
    x1 = _resnorm_fwd(lay, x, m, p["norm_mix_post"].reshape(1, D), modarr, 2, n + "rn1")
    h2 = _normmod_fwd(lay, x1, p["norm_ffn_pre"].reshape(1, D), modarr, 3, 4, n + "nm2")
    g, u, act, extra = _ffn_up(h2, w["wg_t"], w["wu_t"], n + "ffn_up", ffn_side)
    if ffn_side is not None:
        ffn_fill(extra)
    f = _mm([(act, w["wd"])], False, MXU_DTYPE, n + "ffn_down")
    x2 = _resnorm_fwd(lay, x1, f, p["norm_ffn_post"].reshape(1, D), modarr, 5, n + "rn2")
    res.update(h=h, z=z, hst=hst, y=y, mcat=mcat, m=m, x1=x1, h2=h2, g=g, u=u, act=act, f=f)
    return x2, res


def _layer_bwd(lay, i, dx2, modarr, p, w, cst, sp, bands, inv, perm, res, side_fns=None):
    n = f"l{i}b_"
    big, small = {}, {}
    side_fns = side_fns or {}
    side_of = lambda key: side_fns[key](big) if key in side_fns else None
    df, dg2, gpost2 = _resnorm_bwd(lay, dx2, res["f"], p["norm_ffn_post"].reshape(1, D), modarr, 5, n + "rn2")
    big["wd"] = _mm_tn(res["act"], df, MXU_DTYPE, n + "dwd")
    dg, du, early = _ffn_down_bwd(df, w["wd"], res["g"], res["u"], n + "ffn_down", side_of("ffn_down"))
    dh2_side = side_of("dh2")
    dh2 = _mm([(dg, w["wg_t"]), (du, w["wu_t"])], False, MXU_DTYPE, n + "dh2", side=dh2_side)
    if dh2_side is not None:
        dh2, ex = dh2
        early = early + ex
    big["wg_t"] = _mm_tn(dg, res["h2"], MXU_DTYPE, n + "dwg")
    big["wu_t"] = _mm_tn(du, res["h2"], MXU_DTYPE, n + "dwu")
    dx1, dsh2, dsc2, gpre2 = _normmod_bwd(lay, res["x1"], dh2, dx2, p["norm_ffn_pre"].reshape(1, D), modarr, 4, n + "nm2")
    dm, dg1, gpost1 = _resnorm_bwd(lay, dx1, res["m"], p["norm_mix_post"].reshape(1, D), modarr, 2, n + "rn1")
    big["wout"] = _unperm_wout(_mm_tn(res["mcat"], dm, MXU_DTYPE, n + "dwout"))
    dmcat = _mm([(dm, res["wout_p"])], True, MXU_DTYPE, n + "dmcat")
    z = res["z"]
    dz_a, dws, dgb = _gate_bwd(lay, z, dmcat, cst["ws"], cst["wst"], cst["gbias"], n + "gate")
    dy, dud, dwglu, dbglu, ddvec = _glu_bwd(lay, z, res["y"], dmcat, cst["dvec"], w["wglu"], cst["bglu"], n + "glu")
    big["wglu"] = dwglu.astype(MXU_DTYPE)
    duf, dur, dbh, dch, dar, dai, ex = _ssm_bwd(lay, z, dy, perm, res["hst"], sp["bh"], sp["ch"], sp["ar8"],
                                                sp["ai8"], n + "ssm", side_of("ssm"))
    early = early + ex
    dz_p, dpw, dpsc = _pool_bwd(lay, z, dmcat, bands, inv, cst["pw"], cst["pscale"], n + "pool")
    dz = _dz_assemble(lay, dz_a, duf, dur, dud, dz_p, n + "dz")
    big["win_t"] = _mm_tn(dz, res["h"], MXU_DTYPE, n + "dwin")
    dh_side = side_of("dh")
    dh = _mm([(dz, w["win_t"])], False, MXU_DTYPE, n + "dh", side=dh_side)
    if dh_side is not None:
        dh, ex = dh
        early = early + ex
    dx, dsh1, dsc1, gpre1 = _normmod_bwd(lay, res["x0"], dh, dx1, p["norm_mix_pre"].reshape(1, D), modarr, 1, n + "nm1",
                                         latent_only=(i == 0))

    small.update(norm_mix_pre=gpre1[0], norm_mix_post=gpost1[0], norm_ffn_pre=gpre2[0], norm_ffn_post=gpost2[0])
    small["sgu_w"] = dws
    small["sgu_b"] = jnp.sum(dgb.reshape(CHUNK, 4, 64), axis=-1).T
    small.update(_ssm_param_grads(sp, dbh, dch, dar, dai, n + "disc"))
    small["ssm_d"] = ddvec.reshape(SSM_G, SSM_H)
    small["glu_b"] = dbglu[0]
    small["pool_w"] = jnp.stack([dpw[64 * k:64 * k + 64, 64 * k:64 * k + 64] for k in range(4)])
    small["pool_scale"] = dpsc[0]
    dmod = jnp.concatenate([dsh1, dsc1, dg1, dsh2, dsc2, dg2], axis=1)[:lay.bl + 1]
    dmod = jnp.concatenate([dmod, jnp.zeros((8 - lay.bl - 1, 6, D), F32)], axis=0)
    return dx, big, small, dmod, early


def _perm_wout(w):
    return w.reshape(4, D // 4, D)[np.array(WOUT_PERM)].reshape(D, D)


def _unperm_wout(g):
    return g.reshape(4, D // 4, D)[np.array(WOUT_INV)].reshape(D, D)


SMALL_NAMES = ["norm_mix_pre", "norm_mix_post", "norm_ffn_pre", "norm_ffn_post", "sgu_w", "sgu_b", "ssm_lam_re",
               "ssm_lam_im", "ssm_log_dt", "ssm_b_re", "ssm_b_im", "ssm_c_re", "ssm_c_im", "ssm_d", "glu_b", "pool_w",
               "pool_scale"]
BIG_NAMES = ["win_t", "wout", "wglu", "wg_t", "wu_t", "wd"]


def _sincos_2d(rows, cols, dim):
    quarter = dim // 4
    omega = 1.0 / (10000.0 ** (jnp.arange(quarter, dtype=F32) / quarter))
    r = jnp.arange(rows, dtype=F32)[:, None] * omega
    cc = jnp.arange(cols, dtype=F32)[:, None] * omega
    er = jnp.concatenate([jnp.sin(r), jnp.cos(r)], axis=-1)
    ec = jnp.concatenate([jnp.sin(cc), jnp.cos(cc)], axis=-1)
    pe = jnp.concatenate([jnp.broadcast_to(er[:, None, :], (rows, cols, dim // 2)),
                          jnp.broadcast_to(ec[None, :, :], (rows, cols, dim // 2))], axis=-1)
    return pe.reshape(rows * cols, dim)


def _core(x, ctx, target, mods_local, params, weights, w_sides=None, g_side_fns=None):
    bl, lat, _ = x.shape
    assert bl == 4 and lat % TB == 0, "the scan fills 8 sublanes with 2 directions x 4 sequences"
    lay = _Layout(bl, lat)
    pe = _sincos_2d(lat // GRID_W, GRID_W, D)
    bands_np, inv_np = _band_constants()
    bands, inv = jnp.asarray(bands_np, MXU_DTYPE), jnp.asarray(inv_np, F32)
    perm = jnp.asarray(_scan_perm(bl), MXU_DTYPE)
    modarrs, csts, sps, ress, wls = [], [], [], [], []
    for i in range(2):
        modarrs.append(lay.mod_tiles(mods_local[i]))
        csts.append(_layer_consts(params[i]))
        p = params[i]
        sps.append(_ssm_params(p["ssm_lam_re"], p["ssm_lam_im"], p["ssm_log_dt"], p["ssm_b_re"], p["ssm_b_im"],
                               p["ssm_c_re"], p["ssm_c_im"], f"l{i}_disc"))
        wls.append(dict(weights[i]))

    embed_side, embed_fill = (w_sides[0].get("embed") if w_sides else None) or (None, None)
    xt, extra = _embed(lay, x, ctx, pe, embed_side)
    if embed_side is not None:
        embed_fill(wls, extra)
    for i in range(2):
        sides = {}
        for key, (side, fill) in ((w_sides or [{}, {}])[i]).items():
            sides[key] = (side, functools.partial(fill, wls))
        xt, res = _layer_fwd(lay, i, xt, modarrs[i], params[i], wls[i], csts[i], sps[i], bands, inv, perm, sides)
        ress.append(res)
    dx, lossv = _loss_bwd(lay, xt, target)
    bigs, smalls, dmods, early = [None, None], [None, None], [None, None], []
    for i in (1, 0):
        fns = {}
        if i == 0 and g_side_fns is not None:
            fns = {key: functools.partial(fn, bigs[1]) for key, fn in g_side_fns.items()}
        dx, bigs[i], smalls[i], dmods[i], ex = _layer_bwd(lay, i, dx, modarrs[i], params[i], wls[i], csts[i], sps[i],
                                                           bands, inv, perm, ress[i], fns)
        early += ex
    return lossv[0, 0], dx.reshape(bl, lat, D), bigs, smalls, dmods, early


def _my_index():
    return 4 * lax.axis_index("x") + 2 * lax.axis_index("y") + lax.axis_index("c")


def _peer(k):
    x, y, c = lax.axis_index("x"), lax.axis_index("y"), lax.axis_index("c")
    kx, ky, kc = (k >> 2) & 1, (k >> 1) & 1, k & 1
    px = 1 - x if kx else x
    py = 1 - y if ky else y
    pc = 1 - c if kc else c
    return (px, py, pc), 4 * px + 2 * py + pc


class _Side:
    def __init__(self, items):
        self.items = items
        self.n = len(items)
        self.ncopies = sum(len(it[2]) for it in items)
        self.arrays = [it[0] for it in items]
        anyspec = pl.BlockSpec(memory_space=pl.ANY)
        self.in_specs = [anyspec] * self.n
        self.out_specs = [anyspec] * self.n
        self.out_shape = [jax.ShapeDtypeStruct((slots,) + tuple(a.shape) if mode == "gather" else tuple(a.shape), a.dtype)
                          for a, mode, ks, slots in items]
        self.scratch = [pltpu.SemaphoreType.DMA((self.ncopies,)), pltpu.SemaphoreType.DMA((self.ncopies,)),
                        pltpu.SemaphoreType.DMA((self.n,))]

    def _copies(self, ins, outs, sems):
        send_sems, recv_sems, local_sems = sems
        slot_of = lambda idx, slots: idx if slots == 8 else (idx // 2 if slots == 4 else idx % 2)
        me = _my_index()
        local, sends, recvs = [], [], []
        q = 0
        for t, (arr, mode, ks, slots) in enumerate(self.items):
            src_own = ins[t] if mode == "gather" else ins[t].at[me]
            local.append(pltpu.make_async_copy(src_own, outs[t].at[slot_of(me, slots)], local_sems.at[t]))
            for k in ks:
                peer, pidx = _peer(k)
                src = ins[t] if mode == "gather" else ins[t].at[pidx]
                sends.append(pltpu.make_async_remote_copy(
                    src_ref=src, dst_ref=outs[t].at[slot_of(me, slots)], send_sem=send_sems.at[q], recv_sem=recv_sems.at[q],
                    device_id=peer, device_id_type=pl.DeviceIdType.MESH))
                recvs.append(pltpu.make_async_remote_copy(
                    src_ref=src, dst_ref=outs[t].at[slot_of(pidx, slots)], send_sem=send_sems.at[q], recv_sem=recv_sems.at[q],
                    device_id=peer, device_id_type=pl.DeviceIdType.MESH))
                q += 1
        return local, sends, recvs

    def start(self, ins, outs, sems):
        local, sends, _ = self._copies(ins, outs, sems)
        for cp in sends + local:
            cp.start()

    def wait(self, ins, outs, sems):
        local, sends, recvs = self._copies(ins, outs, sems)
        for cp in recvs:
            cp.wait_recv()
        for cp in sends:
            cp.wait_send()
        for cp in local:
            cp.wait()


def _comm(items, name):
    side = _Side(items)
    n = side.n

    def body(*refs):
        ins, outs, sems = refs[:n], refs[n:2 * n], refs[2 * n:]
        side.start(ins, outs, sems)
        side.wait(ins, outs, sems)

    return pl.pallas_call(
        body, in_specs=side.in_specs, out_specs=side.out_specs, out_shape=side.out_shape, scratch_shapes=side.scratch,
        compiler_params=pltpu.CompilerParams(has_side_effects=True), name=name)(*side.arrays)


def _spread(items, name):
    n = len(items)
    ncopies = sum(len(it[1]) for it in items)

    def slot_of(idx, slots):
        return idx if slots == 8 else (idx // 2 if slots == 4 else idx % 2)

    def body(*refs):
        ins, outs, bufs = refs[:n], refs[n:2 * n], refs[2 * n:3 * n]
        load_sems, store_sems, send_sems, recv_sems = refs[3 * n:]
        me = _my_index()
        loads = [pltpu.make_async_copy(ins[t], bufs[t], load_sems.at[t]) for t in range(n)]
        for cp in loads:
            cp.start()
        stores, sends, recvs = [], [], []
        q = 0
        for t, (arr, ks, slots) in enumerate(items):
            loads[t].wait()
            own = outs[t].at[slot_of(me, slots)]
            stores.append(pltpu.make_async_copy(bufs[t], own, store_sems.at[t]))
            stores[-1].start()
            for k in ks:
                peer, pidx = _peer(k)
                sends.append(pltpu.make_async_remote_copy(
                    src_ref=bufs[t], dst_ref=own, send_sem=send_sems.at[q], recv_sem=recv_sems.at[q],
                    device_id=peer, device_id_type=pl.DeviceIdType.MESH))
                recvs.append(pltpu.make_async_remote_copy(
                    src_ref=bufs[t], dst_ref=outs[t].at[slot_of(pidx, slots)], send_sem=send_sems.at[q],
                    recv_sem=recv_sems.at[q], device_id=peer, device_id_type=pl.DeviceIdType.MESH))
                sends[-1].start()
                q += 1
        for cp in recvs:
            cp.wait_recv()
        for cp in sends:
            cp.wait_send()
        for cp in stores:
            cp.wait()

    anyspec = pl.BlockSpec(memory_space=pl.ANY)
    return pl.pallas_call(
        body, in_specs=[anyspec] * n, out_specs=[anyspec] * n,
        out_shape=[jax.ShapeDtypeStruct((slots,) + tuple(arr.shape), arr.dtype) for arr, ks, slots in items],
        scratch_shapes=[pltpu.VMEM(tuple(arr.shape), arr.dtype) for arr, ks, slots in items]
        + [pltpu.SemaphoreType.DMA((n,)), pltpu.SemaphoreType.DMA((n,)), pltpu.SemaphoreType.DMA((ncopies,)),
           pltpu.SemaphoreType.DMA((ncopies,))],
        compiler_params=pltpu.CompilerParams(has_side_effects=True, vmem_limit_bytes=VMEM_LIMIT),
        name=name)(*[it[0] for it in items])


ALL7 = (1, 2, 3, 4, 5, 6, 7)
CHIPS3 = (2, 4, 6)


def _sum8(parts, name):
    def one(a, nm):
        _, r, c = a.shape
        tr = r if r <= 512 else _pick_rows(r)

        def body(a_ref, o_ref):
            acc = a_ref[0].astype(F32)
            for q in range(1, a_ref.shape[0]):
                acc = acc + a_ref[q].astype(F32)
            o_ref[...] = acc

        return pl.pallas_call(
            body, grid=(r // tr,), in_specs=[pl.BlockSpec((a.shape[0], tr, c), lambda i: (0, i, 0))],
            out_specs=pl.BlockSpec((tr, c), lambda i: (i, 0)), out_shape=jax.ShapeDtypeStruct((r, c), F32),
            compiler_params=_cp(("parallel",)), name=nm)(a)

    return [one(a, f"{name}{i}") for i, a in enumerate(parts)]


def _pick_rows(r, cap=512):
    for t in (512, 352, 256, 176, 128, 64, 32, 16, 8):
        if r % t == 0 and t <= cap:
            return t
    return r


def _adam(w, g, m, v, name):
    shape = w.shape
    nel = int(np.prod(shape))
    c1 = 1.0 / (1.0 - ADAM_B1 ** ADAM_STEP)
    c2 = 1.0 / (1.0 - ADAM_B2 ** ADAM_STEP)

    def body(w_ref, g_ref, m_ref, v_ref, d_ref, nm_ref, nv_ref):
        gv = g_ref[...]
        nm = ADAM_B1 * m_ref[...] + (1.0 - ADAM_B1) * gv
        nv = ADAM_B2 * v_ref[...] + (1.0 - ADAM_B2) * (gv * gv)
        d_ref[...] = -ADAM_LR * ((nm * c1) / (jnp.sqrt(nv * c2) + ADAM_EPS) + ADAM_WD * w_ref[...])
        nm_ref[...] = nm
        nv_ref[...] = nv

    padded = int(np.prod(shape[:-2])) * (-(-shape[-2] // 8) * 8) * (-(-shape[-1] // 128) * 128) if len(shape) >= 2 else nel
    if len(shape) >= 2 and padded <= 1024 * 1024:
        sh = jax.ShapeDtypeStruct(shape, F32)
        return pl.pallas_call(body, out_shape=[sh] * 3, compiler_params=_cp(None), name=name)(w, g, m, v)

    if len(shape) >= 2 and shape[-1] >= 128:
        lanes = shape[-1]
    else:
        lanes = 512 if nel % 512 == 0 else 128
    r = nel // lanes
    tr = r if r * lanes <= 384 * 1024 else _pick_rows(r, 384 * 1024 // lanes)

    spec = pl.BlockSpec((tr, lanes), lambda i: (i, 0))
    sh = jax.ShapeDtypeStruct((r, lanes), F32)
    outs = pl.pallas_call(
        body, grid=(r // tr,), in_specs=[spec] * 4, out_specs=[spec] * 3, out_shape=[sh] * 3,
        compiler_params=_cp(("parallel",)), name=name)(*[a.reshape(r, lanes) for a in (w, g, m, v)])
    return [o.reshape(shape) for o in outs]


def _silu(x):
    return x * _sigmoid(x)


def _mod_fwd(c_rows, w_mod, b_cols, name):
    def body(c_ref, w_ref, b_ref, o_ref):
        s = _silu(c_ref[...])
        for l in range(2):
            o_ref[l] = jnp.dot(s, w_ref[l], preferred_element_type=F32, precision=lax.Precision.HIGHEST) + b_ref[l]

    nc = w_mod.shape[2]
    return pl.pallas_call(body, out_shape=jax.ShapeDtypeStruct((2, c_rows.shape[0], nc), F32),
                          compiler_params=_cp(None), name=name)(c_rows, w_mod, b_cols)


def _mod_bwd(c_rows, w_mod, dlat, dctx8, name):
    nrow = c_rows.shape[0]
    nb = nrow - 8

    def body(c_ref, w_ref, dl_ref, dc_ref, gw_ref, gc_ref):
        s = _silu(c_ref[...])
        ctx_row = lax.broadcasted_iota(jnp.int32, (nrow, 1), 0) == nb
        gc = jnp.zeros((1, D), F32)
        for l in range(2):
            dctx = dc_ref[0, l]
            for q in range(1, 8):
                dctx = dctx + dc_ref[q, l]
            dm = dl_ref[l] + jnp.where(ctx_row, dctx, 0.0)
            gw_ref[l] = lax.dot_general(s, dm, (((0,), (0,)), ((), ())), preferred_element_type=F32,
                                        precision=lax.Precision.HIGHEST)
            gc = gc + lax.dot_general(dctx, w_ref[l], (((1,), (1,)), ((), ())), preferred_element_type=F32,
                                      precision=lax.Precision.HIGHEST)
        gc_ref[...] = gc

    nc = w_mod.shape[2]
    return pl.pallas_call(body, out_shape=[jax.ShapeDtypeStruct((2, D, nc), F32), jax.ShapeDtypeStruct((1, D), F32)],
                          compiler_params=_cp(None), name=name)(c_rows, w_mod, dlat, dctx8)


def _bmod_cctx(dmod_all, gc4, c_ctx, name):
    def body(dm_ref, gc_ref, cc_ref, gb_ref, gcc_ref):
        for l in range(2):
            acc = jnp.sum(dm_ref[0, l], axis=0, keepdims=True)
            for q in range(1, 8):
                acc = acc + jnp.sum(dm_ref[q, l], axis=0, keepdims=True)
            gb_ref[l:l + 1, :] = acc
        g = gc_ref[0] + gc_ref[1] + gc_ref[2] + gc_ref[3]
        cv = cc_ref[...]
        sg = _sigmoid(cv)
        gcc_ref[...] = g * (sg * (1.0 + cv * (1.0 - sg)))

    return pl.pallas_call(body, out_shape=[jax.ShapeDtypeStruct((2, 6 * D), F32), jax.ShapeDtypeStruct((1, D), F32)],
                          compiler_params=_cp(None), name=name)(dmod_all, gc4, c_ctx)


def kernel(x, c, ctx, c_ctx, w_mod, b_mod, norm_mix_pre, norm_mix_post, norm_ffn_pre, norm_ffn_post, w_in, w_out, sgu_w, sgu_b, ssm_lam_re, ssm_lam_im, ssm_log_dt, ssm_b_re, ssm_b_im, ssm_c_re, ssm_c_im, ssm_d, glu_w, glu_b, pool_w, pool_scale, ffn_w_gate, ffn_w_up, ffn_w_down, loss_target, m_c_ctx, m_w_mod, m_b_mod, m_norm_mix_pre, m_norm_mix_post, m_norm_ffn_pre, m_norm_ffn_post, m_w_in, m_w_out, m_sgu_w, m_sgu_b, m_ssm_lam_re, m_ssm_lam_im, m_ssm_log_dt, m_ssm_b_re, m_ssm_b_im, m_ssm_c_re, m_ssm_c_im, m_ssm_d, m_glu_w, m_glu_b, m_pool_w, m_pool_scale, m_ffn_w_gate, m_ffn_w_up, m_ffn_w_down, v_c_ctx, v_w_mod, v_b_mod, v_norm_mix_pre, v_norm_mix_post, v_norm_ffn_pre, v_norm_ffn_post, v_w_in, v_w_out, v_sgu_w, v_sgu_b, v_ssm_lam_re, v_ssm_lam_im, v_ssm_log_dt, v_ssm_b_re, v_ssm_b_im, v_ssm_c_re, v_ssm_c_im, v_ssm_d, v_glu_w, v_glu_b, v_pool_w, v_pool_scale, v_ffn_w_gate, v_ffn_w_up, v_ffn_w_down):
    wts = dict(c_ctx=c_ctx, w_mod=w_mod, b_mod=b_mod, norm_mix_pre=norm_mix_pre, norm_mix_post=norm_mix_post,
               norm_ffn_pre=norm_ffn_pre, norm_ffn_post=norm_ffn_post, w_in=w_in, w_out=w_out, sgu_w=sgu_w, sgu_b=sgu_b,
               ssm_lam_re=ssm_lam_re, ssm_lam_im=ssm_lam_im, ssm_log_dt=ssm_log_dt, ssm_b_re=ssm_b_re, ssm_b_im=ssm_b_im,
               ssm_c_re=ssm_c_re, ssm_c_im=ssm_c_im, ssm_d=ssm_d, glu_w=glu_w, glu_b=glu_b, pool_w=pool_w,
               pool_scale=pool_scale, ffn_w_gate=ffn_w_gate, ffn_w_up=ffn_w_up, ffn_w_down=ffn_w_down)
    ms = dict(c_ctx=m_c_ctx, w_mod=m_w_mod, b_mod=m_b_mod, norm_mix_pre=m_norm_mix_pre, norm_mix_post=m_norm_mix_post,
              norm_ffn_pre=m_norm_ffn_pre, norm_ffn_post=m_norm_ffn_post, w_in=m_w_in, w_out=m_w_out, sgu_w=m_sgu_w,
              sgu_b=m_sgu_b, ssm_lam_re=m_ssm_lam_re, ssm_lam_im=m_ssm_lam_im, ssm_log_dt=m_ssm_log_dt,
              ssm_b_re=m_ssm_b_re, ssm_b_im=m_ssm_b_im, ssm_c_re=m_ssm_c_re, ssm_c_im=m_ssm_c_im, ssm_d=m_ssm_d,
              glu_w=m_glu_w, glu_b=m_glu_b, pool_w=m_pool_w, pool_scale=m_pool_scale, ffn_w_gate=m_ffn_w_gate,
              ffn_w_up=m_ffn_w_up, ffn_w_down=m_ffn_w_down)
    vs = dict(c_ctx=v_c_ctx, w_mod=v_w_mod, b_mod=v_b_mod, norm_mix_pre=v_norm_mix_pre, norm_mix_post=v_norm_mix_post,
              norm_ffn_pre=v_norm_ffn_pre, norm_ffn_post=v_norm_ffn_post, w_in=v_w_in, w_out=v_w_out, sgu_w=v_sgu_w,
              sgu_b=v_sgu_b, ssm_lam_re=v_ssm_lam_re, ssm_lam_im=v_ssm_lam_im, ssm_log_dt=v_ssm_log_dt,
              ssm_b_re=v_ssm_b_re, ssm_b_im=v_ssm_b_im, ssm_c_re=v_ssm_c_re, ssm_c_im=v_ssm_c_im, ssm_d=v_ssm_d,
              glu_w=v_glu_w, glu_b=v_glu_b, pool_w=v_pool_w, pool_scale=v_pool_scale, ffn_w_gate=v_ffn_w_gate,
              ffn_w_up=v_ffn_w_up, ffn_w_down=v_ffn_w_down)
    order = list(wts.keys())
    bl = x.shape[0]
    nseq = bl * N_DEV
    me = _my_index()
    chip = me // 2
    ncol = w_mod.shape[2]

    (c_all,) = _spread([(c, ALL7, 8)], "ag_c")
    nrow = nseq + 8
    c_rows = jnp.concatenate([c_all.reshape(nseq, D), c_ctx[None], jnp.zeros((7, D), F32)], axis=0)
    b_cols = lax.dynamic_slice_in_dim(b_mod, chip * ncol, ncol, axis=1)[:, None, :]
    mod_cols = _mod_fwd(c_rows, w_mod, b_cols, "mod_fwd")
    (mod4,) = _spread([(mod_cols, CHIPS3, 4)], "ag_mod")
    mods = jnp.transpose(mod4, (1, 2, 0, 3)).reshape(2, nrow, 6 * D)
    mods_local = jnp.concatenate([lax.dynamic_slice_in_dim(mods, me * bl, bl, axis=1), mods[:, nseq:nseq + 1],
                                  jnp.zeros((2, 8 - bl - 1, 6 * D), F32)], axis=1)

    shards = {}
    for i in range(2):
        for nme, s in zip(BIG_NAMES, [w_in[i].T, w_out[i], glu_w[i], ffn_w_gate[i].T, ffn_w_up[i].T, ffn_w_down[i]]):
            shards[(i, nme)] = s.astype(MXU_DTYPE)
    weights = [{}, {}]
    ffn_names = ("wg_t", "wu_t", "wd")
    w_plan = [{"embed": [(0, "win_t")], "win": [(0, "wout"), (0, "wglu")], "ssm": [(0, nme) for nme in ffn_names],
               "ffn": [(1, "win_t"), (1, "wout"), (1, "wglu")]},
              {"ssm": [(1, nme) for nme in ffn_names]}]

    def w_entry(keys):
        def fill(wls, gathered):
            for (i, nme), g in zip(keys, gathered):
                wls[i][nme] = g.reshape(-1, g.shape[-1])
        return _Side([(shards[k2], "gather", CHIPS3, 4) for k2 in keys]), fill

    w_sides = [{key: w_entry(keys) for key, keys in plan.items()} for plan in w_plan]

    eighths = lambda g: g.reshape(8, g.shape[0] // 8, g.shape[1])
    g_plan = {"ffn_down": [(1, "win_t"), (1, "wout"), (1, "wglu"), (1, "wg_t")], "dh2": [(1, "wu_t"), (1, "wd")],
              "ssm": [(0, k) for k in BIG_NAMES if k != "win_t"], "dh": [(0, "win_t")]}
    early_g = g_plan["ffn_down"] + g_plan["dh2"] + g_plan["ssm"] + g_plan["dh"]

    def g_entry(keys):
        return lambda big1, big0: _Side([(eighths((big1 if i == 1 else big0)[k]), "a2a", ALL7, 8) for i, k in keys])

    g_side_fns = {key: g_entry(keys) for key, keys in g_plan.items()}

    params = [{k: wts[k][i] for k in SMALL_NAMES} for i in range(2)]
    loss_part, grad_x, bigs, smalls, dmods, early = _core(x, ctx, loss_target, mods_local, params, weights,
                                                           w_sides, g_side_fns)
    loss = lax.psum(loss_part, ("x", "y", "c"))

    dmod_local = jnp.stack([dmods[i].reshape(8, 6 * D) for i in range(2)])
    (dmod_all,) = _spread([(dmod_local, ALL7, 8)], "ag_dmod")
    dcols = lax.dynamic_slice_in_dim(dmod_all, chip * ncol, ncol, axis=3)
    dlat = jnp.transpose(dcols[:, :, :bl], (1, 0, 2, 3)).reshape(2, nseq, ncol)
    dlat = jnp.concatenate([dlat, jnp.zeros((2, 8, ncol), F32)], axis=1)
    dctx8 = dcols[:, :, bl:bl + 1]
    g_w_mod, gc_part = _mod_bwd(c_rows, w_mod, dlat, dctx8, "mod_bwd")
    (gc4,) = _spread([(gc_part, CHIPS3, 4)], "ag_cctx")
    g_b_mod, g_c_ctx = _bmod_cctx(dmod_all, gc4, c_ctx[None], "bmod_cctx")

    small_flat = jnp.concatenate([jnp.stack([smalls[i][k] for i in range(2)]).reshape(-1) for k in SMALL_NAMES])
    npad = (-small_flat.shape[0]) % (8 * 1024)
    small_flat = jnp.concatenate([small_flat, jnp.zeros((npad,), F32)])
    late = _comm([(small_flat.reshape(8, -1, 1024), "a2a", ALL7, 8)], "a2a_grads")
    sums = _sum8(list(early) + list(late), "gsum")
    fin = _spread([(s, (1,), 2) for s in sums[:-1]] + [(sums[-1], ALL7, 8)], "ag_grads")
    big_g = [{}, {}]
    for (i, k), g in zip(early_g, fin[:-1]):
        big_g[i][k] = g.reshape(-1, g.shape[-1])
    small_red = fin[-1].reshape(-1)

    grads = {}
    off = 0
    for k in SMALL_NAMES:
        shp = wts[k].shape
        nel = int(np.prod(shp))
        grads[k] = small_red[off:off + nel].reshape(shp)
        off += nel
    grads["c_ctx"] = g_c_ctx[0]
    grads["w_mod"] = g_w_mod
    grads["b_mod"] = g_b_mod
    grads["w_in"] = jnp.stack([big_g[i]["win_t"].T for i in range(2)])
    grads["w_out"] = jnp.stack([big_g[i]["wout"] for i in range(2)])
    grads["glu_w"] = jnp.stack([big_g[i]["wglu"] for i in range(2)])
    grads["ffn_w_gate"] = jnp.stack([big_g[i]["wg_t"].T for i in range(2)])
    grads["ffn_w_up"] = jnp.stack([big_g[i]["wu_t"].T for i in range(2)])
    grads["ffn_w_down"] = jnp.stack([big_g[i]["wd"] for i in range(2)])

    deltas, new_m, new_v = {}, {}, {}
    for k in order:
        deltas[k], new_m[k], new_v[k] = _adam(wts[k], grads[k], ms[k], vs[k], "adam_" + k)
    return (loss, grad_x, *[grads[k] for k in order], *[deltas[k] for k in order],
            *[new_m[k] for k in order], *[new_v[k] for k in order])
```

```python
import functools
import math

import numpy as np
import jax
import jax.numpy as jnp
from jax import lax
from jax.experimental import pallas as pl
from jax.experimental.pallas import tpu as pltpu

F32 = jnp.float32
BF16 = jnp.bfloat16
MXU_DTYPE = jnp.bfloat16
MCAT_A, MCAT_C = 2, 3
WOUT_PERM, WOUT_INV = (1, 2, 0, 3), (2, 0, 1, 3)

D = 1024
EPS = 1e-6
TB = 256
CTX = 256
CHUNK = 128
GRID_W = 64
A_W, B_W, C_W = 256, 512, 256
D_IN = 1280
D_FF = 2816
SSM_G, SSM_P, SSM_H = 32, 64, 16
ST = 64
POOL_WINDOWS = (2, 4, 8, 16)
N_DEV = 8
VMEM_LIMIT = 52 * 1024 * 1024
GELU_C = math.sqrt(2.0 / math.pi)

ADAM_LR, ADAM_B1, ADAM_B2, ADAM_EPS, ADAM_WD, ADAM_STEP = 0.001, 0.9, 0.999, 1e-08, 0.01, 10


def _cp(sem=None, vmem=VMEM_LIMIT, **kw):
    return pltpu.CompilerParams(dimension_semantics=sem, vmem_limit_bytes=vmem, **kw)


def _pick(n, cap):
    if n <= cap:
        return n
    best = None
    for t in range(128, cap + 1, 128):
        if n % t == 0:
            best = t
    assert best is not None, (n, cap)
    return best


def _gelu(x):
    return 0.5 * x * (1.0 + jnp.tanh(GELU_C * (x + 0.044715 * x * x * x)))


def _gelu_grad(x):
    t = jnp.tanh(GELU_C * (x + 0.044715 * x * x * x))
    return 0.5 * (1.0 + t) + 0.5 * x * (1.0 - t * t) * GELU_C * (1.0 + 3.0 * 0.044715 * x * x)


def _sigmoid(x):
    return 1.0 / (1.0 + jnp.exp(-x))


def _dot(a, b, dims):
    return lax.dot_general(a, b, (dims, ((), ())), preferred_element_type=F32)


def _nn(a, b):
    return _dot(a, b, ((1,), (0,)))


def _nt(a, b):
    return _dot(a, b, ((1,), (1,)))


def _tn(a, b):
    return _dot(a, b, ((0,), (0,)))


def _mm(pairs, nt, out_dtype, name, tm=512, side=None):
    m = pairs[0][0].shape[0]
    n = pairs[0][1].shape[0] if nt else pairs[0][1].shape[1]
    tn = _pick(n, 1408)
    tm = min(tm, m)
    npairs = len(pairs)
    ni, nj = m // tm, n // tn

    def body(*refs):
        own, srefs = _side_split(refs, 2 * npairs, 1, 0, side)
        o_ref = own[-1]
        i, j = pl.program_id(0), pl.program_id(1)
        _side_start(side, srefs, jnp.logical_and(i == 0, j == 0))
        acc = None
        for t in range(npairs):
            a = own[2 * t][...].astype(MXU_DTYPE)
            b = own[2 * t + 1][...].astype(MXU_DTYPE)
            r = _nt(a, b) if nt else _nn(a, b)
            acc = r if acc is None else acc + r
        o_ref[...] = acc.astype(o_ref.dtype)
        _side_wait(side, srefs, jnp.logical_and(i == ni - 1, j == nj - 1))

    sd = side if side is not None else _Side([])
    in_specs, flat = [], []
    for a, b in pairs:
        k = a.shape[1]
        in_specs.append(pl.BlockSpec((tm, k), lambda i, j: (i, 0)))
        in_specs.append(pl.BlockSpec((tn, k), lambda i, j: (j, 0)) if nt else pl.BlockSpec((k, tn), lambda i, j: (0, j)))
        flat += [a, b]
    outs = pl.pallas_call(
        body, grid=(ni, nj), in_specs=in_specs + sd.in_specs,
        out_specs=[pl.BlockSpec((tm, tn), lambda i, j: (i, j))] + sd.out_specs,
        out_shape=[jax.ShapeDtypeStruct((m, n), out_dtype)] + sd.out_shape,
        scratch_shapes=sd.scratch if side is not None else [],
        compiler_params=_cp(("arbitrary", "arbitrary") if side is not None else ("parallel", "parallel")),
        name=name)(*flat, *sd.arrays)
    return outs[0] if side is None else (outs[0], list(outs[1:]))


def _mm_tn(a, b, out_dtype, name):
    m, k1 = a.shape
    n = b.shape[1]
    t1 = _pick(k1, 1408)
    tn = _pick(n, 1024)
    tm = max(t for t in (512, 1024, 1536) if m % t == 0)
    nsteps = m // tm

    def body(a_ref, b_ref, o_ref, acc_ref):
        t = pl.program_id(2)

        @pl.when(t == 0)
        def _():
            acc_ref[...] = jnp.zeros_like(acc_ref)

        acc_ref[...] += _tn(a_ref[...].astype(MXU_DTYPE), b_ref[...].astype(MXU_DTYPE))

        @pl.when(t == nsteps - 1)
        def _():
            o_ref[...] = acc_ref[...].astype(o_ref.dtype)

    return pl.pallas_call(
        body, grid=(k1 // t1, n // tn, nsteps),
        in_specs=[pl.BlockSpec((tm, t1), lambda i, j, t: (t, i)), pl.BlockSpec((tm, tn), lambda i, j, t: (t, j))],
        out_specs=pl.BlockSpec((t1, tn), lambda i, j, t: (i, j)),
        out_shape=jax.ShapeDtypeStruct((k1, n), out_dtype),
        scratch_shapes=[pltpu.VMEM((t1, tn), F32)],
        compiler_params=_cp(("parallel", "parallel", "arbitrary")), name=name)(a, b)


class _Layout:
    def __init__(self, bl, lat):
        self.bl, self.lat = bl, lat
        self.nlb = lat // TB
        self.nr = 1 + self.nlb
        self.nctx = bl
        self.nb = self.nr * bl
        self.nt = self.nb * TB
        self.ctx_row = bl

    def mod_tiles(self, mods):
        rows = np.array([[self.ctx_row if r == 0 else b for b in range(self.bl)] for r in range(self.nr)], np.int32)
        t = mods[rows].reshape(self.nr, self.bl, 6, D)
        return jnp.transpose(t, (0, 2, 1, 3)).reshape(self.nr * 6, self.bl, 1, D)


ST_FWD, ST_BWD = 4, 2


def _tok_spec(lay, st):
    nc = lay.bl // st
    return pl.BlockSpec((st * TB, D), lambda c, r: (r * nc + c, 0))


def _vec_spec():
    return pl.BlockSpec((1, D), lambda c, r: (0, 0))


def _mod_spec(st, k):
    return pl.BlockSpec((1, st, 1, D), lambda c, r: (r * 6 + k, c, 0, 0))


def _x_spec(lay, st):
    return pl.BlockSpec((st, 1, TB, D), lambda c, r: (c, jnp.maximum(r - 1, 0), 0, 0))


def _rows3(ref_or_val, st):
    return ref_or_val.reshape(st, TB, D)


def _acc_rows(acc_ref, val3, st, ctx_row):
    c, r = pl.program_id(0), pl.program_id(1)
    s = jnp.sum(val3, axis=1, keepdims=True)

    @pl.when(r == 0)
    def _():
        acc_ref[ctx_row:ctx_row + 1] += jnp.sum(s, axis=0, keepdims=True)

    @pl.when(r > 0)
    def _():
        acc_ref[pl.ds(c * st, st)] += s


def _first_step():
    return jnp.logical_and(pl.program_id(0) == 0, pl.program_id(1) == 0)


def _embed(lay, x, ctx, pe, side=None):
    st = ST_FWD
    bl, nlb = lay.bl, lay.nlb
    nc = bl // st

    def body(*refs):
        (x_ref, c_ref, pe_ref, o_ref), srefs = _side_split(refs, 3, 1, 0, side)
        c, r = pl.program_id(0), pl.program_id(1)
        _side_start(side, srefs, jnp.logical_and(c == 0, r == 0))

        @pl.when(r == 0)
        def _():
            o_ref[...] = c_ref[...].reshape(st * TB, D)

        @pl.when(r > 0)
        def _():
            o_ref[...] = (x_ref[...].reshape(st, TB, D) + pe_ref[...]).reshape(st * TB, D)

        _side_wait(side, srefs, jnp.logical_and(c == nc - 1, r == lay.nr - 1))

    sd = side if side is not None else _Side([])
    outs = pl.pallas_call(
        body, grid=(nc, lay.nr),
        in_specs=[_x_spec(lay, st), pl.BlockSpec((st, CTX, D), lambda c, r: (c, 0, 0)),
                  pl.BlockSpec((1, TB, D), lambda c, r: (jnp.maximum(r - 1, 0), 0, 0))] + sd.in_specs,
        out_specs=[_tok_spec(lay, st)] + sd.out_specs,
        out_shape=[jax.ShapeDtypeStruct((lay.nt, D), F32)] + sd.out_shape,
        scratch_shapes=sd.scratch if side is not None else [],
        compiler_params=_cp(("arbitrary", "arbitrary") if side is not None else ("parallel", "parallel")),
        name="embed")(x.reshape(bl, nlb, TB, D), ctx, pe.reshape(nlb, TB, D), *sd.arrays)
    return outs[0], list(outs[1:])


def _normmod_fwd(lay, x, gain, modt, ksh, ksc, name):
    st = ST_FWD

    def body(x_ref, g_ref, sh_ref, sc_ref, o_ref):
        xv = _rows3(x_ref[...], st)
        r = lax.rsqrt(jnp.mean(xv * xv, axis=-1, keepdims=True) + EPS)
        o_ref[...] = ((xv * r * g_ref[...]) * (1.0 + sc_ref[0]) + sh_ref[0]).reshape(st * TB, D).astype(o_ref.dtype)

    return pl.pallas_call(
        body, grid=(lay.bl // st, lay.nr),
        in_specs=[_tok_spec(lay, st), _vec_spec(), _mod_spec(st, ksh), _mod_spec(st, ksc)],
        out_specs=_tok_spec(lay, st), out_shape=jax.ShapeDtypeStruct((lay.nt, D), MXU_DTYPE),
        compiler_params=_cp(("parallel", "parallel")), name=name)(x, gain, modt, modt)


def _acc_out():
    return pl.BlockSpec((8, 1, D), lambda c, r: (0, 0, 0)), jax.ShapeDtypeStruct((8, 1, D), F32)


def _normmod_bwd(lay, x, dh, dx_in, gain, modt, ksc, name, latent_only=False):
    st = ST_BWD
    acc_spec, acc_shape = _acc_out()
    if latent_only:
        dx_spec, dx_shape = _x_spec(lay, st), jax.ShapeDtypeStruct((lay.bl, lay.nlb, TB, D), F32)
    else:
        dx_spec, dx_shape = _tok_spec(lay, st), jax.ShapeDtypeStruct((lay.nt, D), F32)

    def body(x_ref, dh_ref, dxi_ref, g_ref, sc_ref, dx_ref, dsh_ref, dsc_ref, dg_ref):
        xv = _rows3(x_ref[...], st)
        dhv = _rows3(dh_ref[...].astype(F32), st)
        g = g_ref[...]
        sc1 = 1.0 + sc_ref[0]
        r = lax.rsqrt(jnp.mean(xv * xv, axis=-1, keepdims=True) + EPS)
        xh = xv * r
        dxh = dhv * (g * sc1)
        dx = _rows3(dxi_ref[...], st) + r * (dxh - xh * jnp.mean(dxh * xh, axis=-1, keepdims=True))
        dx_ref[...] = dx.reshape(dx_ref.shape)

        @pl.when(_first_step())
        def _():
            dsh_ref[...] = jnp.zeros_like(dsh_ref)
            dsc_ref[...] = jnp.zeros_like(dsc_ref)
            dg_ref[...] = jnp.zeros_like(dg_ref)

        _acc_rows(dsh_ref, dhv, st, lay.ctx_row)
        _acc_rows(dsc_ref, dhv * (xh * g), st, lay.ctx_row)
        dg_ref[...] += jnp.sum((dhv * sc1 * xh).reshape(st * TB, D), axis=0, keepdims=True)

    return pl.pallas_call(
        body, grid=(lay.bl // st, lay.nr),
        in_specs=[_tok_spec(lay, st), _tok_spec(lay, st), _tok_spec(lay, st), _vec_spec(), _mod_spec(st, ksc)],
        out_specs=[dx_spec, acc_spec, acc_spec, _vec_spec()],
        out_shape=[dx_shape, acc_shape, acc_shape, jax.ShapeDtypeStruct((1, D), F32)],
        compiler_params=_cp(("arbitrary", "arbitrary")), name=name)(x, dh, dx_in, gain, modt)


def _resnorm_fwd(lay, x, m, gain, modt, kgate, name):
    st = ST_FWD

    def body(x_ref, m_ref, g_ref, gate_ref, o_ref):
        mv = _rows3(m_ref[...].astype(F32), st)
        r = lax.rsqrt(jnp.mean(mv * mv, axis=-1, keepdims=True) + EPS)
        o_ref[...] = x_ref[...] + (gate_ref[0] * (mv * r * g_ref[...])).reshape(st * TB, D)

    return pl.pallas_call(
        body, grid=(lay.bl // st, lay.nr),
        in_specs=[_tok_spec(lay, st), _tok_spec(lay, st), _vec_spec(), _mod_spec(st, kgate)],
        out_specs=_tok_spec(lay, st), out_shape=jax.ShapeDtypeStruct((lay.nt, D), F32),
        compiler_params=_cp(("parallel", "parallel")), name=name)(x, m, gain, modt)


def _resnorm_bwd(lay, dxn, m, gain, modt, kgate, name):
    st = ST_BWD
    acc_spec, acc_shape = _acc_out()

    def body(d_ref, m_ref, g_ref, gate_ref, dm_ref, dgate_ref, dg_ref):
        dv = _rows3(d_ref[...], st)
        mv = _rows3(m_ref[...].astype(F32), st)
        g = g_ref[...]
        r = lax.rsqrt(jnp.mean(mv * mv, axis=-1, keepdims=True) + EPS)
        xh = mv * r
        dy = dv * gate_ref[0]
        dxh = dy * g
        dm = r * (dxh - xh * jnp.mean(dxh * xh, axis=-1, keepdims=True))
        dm_ref[...] = dm.reshape(st * TB, D).astype(dm_ref.dtype)

        @pl.when(_first_step())
        def _():
            dgate_ref[...] = jnp.zeros_like(dgate_ref)
            dg_ref[...] = jnp.zeros_like(dg_ref)

        _acc_rows(dgate_ref, dv * (xh * g), st, lay.ctx_row)
        dg_ref[...] += jnp.sum((dy * xh).reshape(st * TB, D), axis=0, keepdims=True)

    return pl.pallas_call(
        body, grid=(lay.bl // st, lay.nr),
        in_specs=[_tok_spec(lay, st), _tok_spec(lay, st), _vec_spec(), _mod_spec(st, kgate)],
        out_specs=[_tok_spec(lay, st), acc_spec, _vec_spec()],
        out_shape=[jax.ShapeDtypeStruct((lay.nt, D), MXU_DTYPE), acc_shape, jax.ShapeDtypeStruct((1, D), F32)],
        compiler_params=_cp(("arbitrary", "arbitrary")), name=name)(dxn, m, gain, modt)


def _loss_bwd(lay, xf, tgt):
    st = ST_FWD

    def body(x_ref, t_ref, dx_ref, l_ref):
        r = pl.program_id(1)

        @pl.when(_first_step())
        def _():
            l_ref[...] = jnp.zeros_like(l_ref)

        @pl.when(r == 0)
        def _():
            dx_ref[...] = jnp.zeros_like(dx_ref)

        @pl.when(r > 0)
        def _():
            e = x_ref[...] - t_ref[...].reshape(st * TB, D)
            dx_ref[...] = e * (1.0 / D)
            l_ref[...] += jnp.sum(e * e) * (0.5 / D)

    return pl.pallas_call(
        body, grid=(lay.bl // st, lay.nr),
        in_specs=[_tok_spec(lay, st), _x_spec(lay, st)],
        out_specs=[_tok_spec(lay, st), pl.BlockSpec((8, 128), lambda c, r: (0, 0))],
        out_shape=[jax.ShapeDtypeStruct((lay.nt, D), F32), jax.ShapeDtypeStruct((8, 128), F32)],
        compiler_params=_cp(("arbitrary", "arbitrary")), name="loss")(xf, tgt.reshape(lay.bl, lay.nlb, TB, D))


FF_TN = D_FF // 2
FF_CHUNKS = ((0, 512), (512, 512), (1024, 384))


def _ffn_up(h, wgt, wut, name, side=None):
    m = h.shape[0]
    tm, tn = min(512, m), FF_TN
    ni, nj = m // tm, D_FF // tn

    def body(*refs):
        (h_ref, wg_ref, wu_ref, g_ref, u_ref, a_ref), srefs = _side_split(refs, 3, 3, 0, side)
        j, i = pl.program_id(0), pl.program_id(1)
        _side_start(side, srefs, jnp.logical_and(i == 0, j == 0))
        hv = h_ref[...]
        for c0, cw in FF_CHUNKS:
            g = _nt(hv, wg_ref[c0:c0 + cw, :])
            u = _nt(hv, wu_ref[c0:c0 + cw, :])
            g_ref[:, c0:c0 + cw] = g.astype(g_ref.dtype)
            u_ref[:, c0:c0 + cw] = u.astype(u_ref.dtype)
            a_ref[:, c0:c0 + cw] = (g * _sigmoid(g) * u).astype(a_ref.dtype)
        _side_wait(side, srefs, jnp.logical_and(i == ni - 1, j == nj - 1))

    sd = side if side is not None else _Side([])
    osp = pl.BlockSpec((tm, tn), lambda j, i: (i, j))
    osh = jax.ShapeDtypeStruct((m, D_FF), MXU_DTYPE)
    outs = pl.pallas_call(
        body, grid=(nj, ni),
        in_specs=[pl.BlockSpec((tm, D), lambda j, i: (i, 0)), pl.BlockSpec((tn, D), lambda j, i: (j, 0)),
                  pl.BlockSpec((tn, D), lambda j, i: (j, 0))] + sd.in_specs,
        out_specs=[osp, osp, osp] + sd.out_specs, out_shape=[osh, osh, osh] + sd.out_shape,
        scratch_shapes=sd.scratch if side is not None else [],
        compiler_params=_cp(("arbitrary", "arbitrary") if side is not None else ("parallel", "parallel")),
        name=name)(h, wgt, wut, *sd.arrays)
    return outs[0], outs[1], outs[2], list(outs[3:])


def _ffn_down_bwd(df, wd, g, u, name, side=None):
    m = df.shape[0]
    tm, tn = min(512, m), FF_TN
    ni, nj = m // tm, D_FF // tn

    def body(*refs):
        (df_ref, wd_ref, g_ref, u_ref, dg_ref, du_ref), srefs = _side_split(refs, 4, 2, 0, side)
        j, i = pl.program_id(0), pl.program_id(1)
        _side_start(side, srefs, jnp.logical_and(i == 0, j == 0))
        dfv = df_ref[...]
        for c0, cw in FF_CHUNKS:
            da = _nt(dfv, wd_ref[c0:c0 + cw, :])
            gv = g_ref[:, c0:c0 + cw].astype(F32)
            uv = u_ref[:, c0:c0 + cw].astype(F32)
            s = _sigmoid(gv)
            dg_ref[:, c0:c0 + cw] = (da * uv * (s * (1.0 + gv * (1.0 - s)))).astype(dg_ref.dtype)
            du_ref[:, c0:c0 + cw] = (da * gv * s).astype(du_ref.dtype)
        _side_wait(side, srefs, jnp.logical_and(i == ni - 1, j == nj - 1))

    sd = side if side is not None else _Side([])
    osp = pl.BlockSpec((tm, tn), lambda j, i: (i, j))
    osh = jax.ShapeDtypeStruct((m, D_FF), MXU_DTYPE)
    outs = pl.pallas_call(
        body, grid=(nj, ni),
        in_specs=[pl.BlockSpec((tm, D), lambda j, i: (i, 0)), pl.BlockSpec((tn, D), lambda j, i: (j, 0)), osp, osp]
        + sd.in_specs,
        out_specs=[osp, osp] + sd.out_specs, out_shape=[osh, osh] + sd.out_shape,
        scratch_shapes=sd.scratch if side is not None else [],
        compiler_params=_cp(("arbitrary", "arbitrary") if side is not None else ("parallel", "parallel")),
        name=name)(df, wd, g, u, *sd.arrays)
    return outs[0], outs[1], list(outs[2:])


def _head_masks(shape):
    lane = lax.broadcasted_iota(jnp.int32, shape, 1)
    return [jnp.logical_and(lane >= 64 * h, lane < 64 * h + 64) for h in range(4)]


def _head_mean(x, masks):
    out = jnp.zeros_like(x)
    for mk in masks:
        s = jnp.sum(jnp.where(mk, x, 0.0), axis=-1, keepdims=True) * (1.0 / 64.0)
        out = jnp.where(mk, s, out)
    return out


def _gate_common(z, masks):
    zg = _gelu(z)
    u = zg[:, :A_W]
    v = zg[:, A_W:]
    mu = _head_mean(v, masks)
    vc = v - mu
    rstd = lax.rsqrt(_head_mean(vc * vc, masks) + EPS)
    return u, vc * rstd, rstd


def _gate_s(vn, ws_ref, bias, masks):
    parts = []
    for c in range(TB // CHUNK):
        vc = vn[c * CHUNK:(c + 1) * CHUNK]
        s = bias
        for h in range(4):
            s = s + _nn(ws_ref[h], jnp.where(masks[h][:CHUNK], vc, 0.0).astype(MXU_DTYPE))
        parts.append(s)
    return jnp.concatenate(parts, axis=0)


MT = 4


def _blocks():
    return [pl.ds(s * TB, TB) for s in range(MT)]


def _gate_fwd(lay, z, ws, bias, name):
    def body(z_ref, ws_ref, b_ref, o_ref):
        masks = _head_masks((TB, A_W))
        for sl in _blocks():
            u, vn, _ = _gate_common(z_ref[sl, :].astype(F32), masks)
            o_ref[sl, :] = (u * _gate_s(vn, ws_ref, b_ref[...], masks)).astype(o_ref.dtype)

    return pl.pallas_call(
        body, grid=(lay.nb // MT,),
        in_specs=[pl.BlockSpec((MT * TB, 2 * A_W), lambda j: (j, 0)), pl.BlockSpec((4, CHUNK, CHUNK), lambda j: (0, 0, 0)),
                  pl.BlockSpec((CHUNK, A_W), lambda j: (0, 0))],
        out_specs=pl.BlockSpec((MT * TB, A_W), lambda j: (j, 0)),
        out_shape=jax.ShapeDtypeStruct((lay.nt, A_W), MXU_DTYPE),
        compiler_params=_cp(("parallel",)), name=name)(z, ws, bias)


def _gate_bwd(lay, z, da, ws, wst, bias, name):
    def body(z_ref, da_ref, ws_ref, wst_ref, b_ref, dz_ref, dws_ref, db_ref):
        j = pl.program_id(0)

        @pl.when(j == 0)
        def _():
            dws_ref[...] = jnp.zeros_like(dws_ref)
            db_ref[...] = jnp.zeros_like(db_ref)

        masks = _head_masks((TB, A_W))
        for blk in _blocks():
            zv = z_ref[blk, :].astype(F32)
            u, vn, rstd = _gate_common(zv, masks)
            s = _gate_s(vn, ws_ref, b_ref[...], masks)
            dav = da_ref[blk, :].astype(F32)
            du = dav * s
            ds = dav * u
            dvn_parts = []
            for c in range(TB // CHUNK):
                sl = slice(c * CHUNK, (c + 1) * CHUNK)
                ds_c = ds[sl]
                vn_c = vn[sl].astype(MXU_DTYPE)
                db_ref[...] += ds_c
                ds_b = ds_c.astype(MXU_DTYPE)
                dvn_c = jnp.zeros((CHUNK, A_W), F32)
                for h in range(4):
                    mk = masks[h][:CHUNK]
                    dws_ref[h] += _nt(jnp.where(mk, ds_c, 0.0).astype(MXU_DTYPE), vn_c)
                    dvn_c = dvn_c + jnp.where(mk, _nn(wst_ref[h], ds_b), 0.0)
                dvn_parts.append(dvn_c)
            dvn = jnp.concatenate(dvn_parts, axis=0)
            dv = rstd * (dvn - _head_mean(dvn, masks) - vn * _head_mean(dvn * vn, masks))
            gg = _gelu_grad(zv)
            dz_ref[blk, :A_W] = (du * gg[:, :A_W]).astype(dz_ref.dtype)
            dz_ref[blk, A_W:] = (dv * gg[:, A_W:]).astype(dz_ref.dtype)

    return pl.pallas_call(
        body, grid=(lay.nb // MT,),
        in_specs=[pl.BlockSpec((MT * TB, 2 * A_W), lambda j: (j, 0)), pl.BlockSpec((MT * TB, A_W), lambda j: (j, MCAT_A)),
                  pl.BlockSpec((4, CHUNK, CHUNK), lambda j: (0, 0, 0)), pl.BlockSpec((4, CHUNK, CHUNK), lambda j: (0, 0, 0)),
                  pl.BlockSpec((CHUNK, A_W), lambda j: (0, 0))],
        out_specs=[pl.BlockSpec((MT * TB, 2 * A_W), lambda j: (j, 0)), pl.BlockSpec((4, CHUNK, CHUNK), lambda j: (0, 0, 0)),
                   pl.BlockSpec((CHUNK, A_W), lambda j: (0, 0))],
        out_shape=[jax.ShapeDtypeStruct((lay.nt, 2 * A_W), MXU_DTYPE), jax.ShapeDtypeStruct((4, CHUNK, CHUNK), F32),
                   jax.ShapeDtypeStruct((CHUNK, A_W), F32)],
        compiler_params=_cp(("arbitrary",)), name=name)(z, da, ws, wst, bias)


def _band_constants():
    bands = np.zeros((2, 4, TB, TB), np.float32)
    inv = np.zeros((2, 4, TB, 1), np.float32)
    for kind, n in ((0, GRID_W), (1, TB)):
        for i, w in enumerate(POOL_WINDOWS):
            for t in range(TB):
                base, tt = (t // n) * n, t % n
                lo = min(max(tt - w // 2, 0), n)
                hi = min(max(tt - w // 2 + w, 0), n)
                bands[kind, i, t, base + lo:base + hi] = 1.0
                inv[kind, i, t, 0] = 1.0 / (hi - lo)
    return bands, inv


def _split3(x):
    a = x.astype(MXU_DTYPE)
    r1 = x - a.astype(F32)
    b = r1.astype(MXU_DTYPE)
    c = (r1 - b.astype(F32)).astype(MXU_DTYPE)
    return a, b, c


def _window_apply(band_ref, inv_ref, x, masks, transpose, mxu_exact=False):
    out = jnp.zeros_like(x)
    for i in range(4):
        xi = x * inv_ref[0, i] if transpose else x
        acc = None
        for part in ((xi.astype(MXU_DTYPE),) if mxu_exact else _split3(xi)):
            r = _tn(band_ref[0, i], part) if transpose else _nn(band_ref[0, i], part)
            acc = r if acc is None else acc + r
        if not transpose:
            acc = acc * inv_ref[0, i]
        out = jnp.where(masks[i], acc, out)
    return out


def _pool_specs(lay):
    kind = lambda j: jnp.where(j < lay.nctx // MT, 1, 0)
    return [pl.BlockSpec((1, 4, TB, TB), lambda j: (kind(j), 0, 0, 0)), pl.BlockSpec((1, 4, TB, 1), lambda j: (kind(j), 0, 0, 0))]


def _pool_fwd(lay, z, bands, inv, pw, scale, name):
    def body(p_ref, band_ref, inv_ref, pw_ref, sc_ref, o_ref):
        masks = _head_masks((TB, C_W))
        for blk in _blocks():
            p = p_ref[blk, :].astype(F32)
            diff = _window_apply(band_ref, inv_ref, p, masks, False, mxu_exact=True) - p
            o_ref[blk, :] = (_nn(diff.astype(MXU_DTYPE), pw_ref[...]) * sc_ref[...]).astype(o_ref.dtype)

    return pl.pallas_call(
        body, grid=(lay.nb // MT,),
        in_specs=[pl.BlockSpec((MT * TB, C_W), lambda j: (j, 4))] + _pool_specs(lay)
        + [pl.BlockSpec((C_W, C_W), lambda j: (0, 0)), pl.BlockSpec((1, C_W), lambda j: (0, 0))],
        out_specs=pl.BlockSpec((MT * TB, C_W), lambda j: (j, 0)),
        out_shape=jax.ShapeDtypeStruct((lay.nt, C_W), MXU_DTYPE),
        compiler_params=_cp(("parallel",)), name=name)(z, bands, inv, pw, scale)


def _pool_bwd(lay, z, dc, bands, inv, pw, scale, name):
    def body(p_ref, dc_ref, band_ref, inv_ref, pw_ref, sc_ref, dp_ref, dpw_ref, dsc_ref):
        j = pl.program_id(0)

        @pl.when(j == 0)
        def _():
            dpw_ref[...] = jnp.zeros_like(dpw_ref)
            dsc_ref[...] = jnp.zeros_like(dsc_ref)

        masks = _head_masks((TB, C_W))
        for blk in _blocks():
            p = p_ref[blk, :].astype(F32)
            dcv = dc_ref[blk, :].astype(F32)
            diff = _window_apply(band_ref, inv_ref, p, masks, False, mxu_exact=True) - p
            diff_b = diff.astype(MXU_DTYPE)
            pre = _nn(diff_b, pw_ref[...])
            dsc_ref[...] += jnp.sum(dcv * pre, axis=0, keepdims=True)
            dpre = dcv * sc_ref[...]
            dpre_b = dpre.astype(MXU_DTYPE)
            dpw_ref[...] += _tn(diff_b, dpre_b)
            ddiff = _nt(dpre_b, pw_ref[...])
            dp_ref[blk, :] = (_window_apply(band_ref, inv_ref, ddiff, masks, True) - ddiff).astype(dp_ref.dtype)

    return pl.pallas_call(
        body, grid=(lay.nb // MT,),
        in_specs=[pl.BlockSpec((MT * TB, C_W), lambda j: (j, 4)), pl.BlockSpec((MT * TB, C_W), lambda j: (j, MCAT_C))]
        + _pool_specs(lay)
        + [pl.BlockSpec((C_W, C_W), lambda j: (0, 0)), pl.BlockSpec((1, C_W), lambda j: (0, 0))],
        out_specs=[pl.BlockSpec((MT * TB, C_W), lambda j: (j, 0)), pl.BlockSpec((C_W, C_W), lambda j: (0, 0)),
                   pl.BlockSpec((1, C_W), lambda j: (0, 0))],
        out_shape=[jax.ShapeDtypeStruct((lay.nt, C_W), MXU_DTYPE), jax.ShapeDtypeStruct((C_W, C_W), F32),
                   jax.ShapeDtypeStruct((1, C_W), F32)],
        compiler_params=_cp(("arbitrary",)), name=name)(z, dc, bands, inv, pw, scale)


def _disc_math(lr, li, ldt, br, bi):
    dt = jnp.exp(ldt)
    e = jnp.exp(lr * dt)
    ar = e * jnp.cos(li * dt)
    ai = e * jnp.sin(li * dt)
    nr, ni = ar - 1.0, ai
    den = lr * lr + li * li
    qr = (nr * lr + ni * li) / den
    qi = (ni * lr - nr * li) / den
    return ar, ai, qr * br - qi * bi, qr * bi + qi * br


def _disc_fwd(lrx, lix, ldtx, brt, bit, name):
    def body(lr_ref, li_ref, ldt_ref, br_ref, bi_ref, ar_ref, ai_ref, obr_ref, obi_ref):
        ar, ai, obr, obi = _disc_math(lr_ref[...], li_ref[...], ldt_ref[...], br_ref[...], bi_ref[...])
        ar_ref[...] = ar
        ai_ref[...] = ai
        obr_ref[...] = obr
        obi_ref[...] = obi

    sh = jax.ShapeDtypeStruct(lrx.shape, F32)
    return pl.pallas_call(body, out_shape=[sh, sh, sh, sh], name=name)(lrx, lix, ldtx, brt, bit)


def _disc_bwd(lrx, lix, ldtx, brt, bit, dar, dai, dbr, dbi, name):
    nrow = lrx.shape[0] // SSM_H

    def body(lr_ref, li_ref, ldt_ref, br_ref, bi_ref, dar_ref, dai_ref, dbr_ref, dbi_ref,
             glr_ref, gli_ref, gdt_ref, gbr_ref, gbi_ref):
        _, vjp = jax.vjp(_disc_math, lr_ref[...], li_ref[...], ldt_ref[...], br_ref[...], bi_ref[...])
        glr, gli, gdt, gbr, gbi = vjp((dar_ref[...], dai_ref[...], dbr_ref[...], dbi_ref[...]))
        glr_ref[...] = jnp.sum(glr.reshape(nrow, SSM_H, SSM_P), axis=1)
        gli_ref[...] = jnp.sum(gli.reshape(nrow, SSM_H, SSM_P), axis=1)
        gdt_ref[...] = jnp.sum(jnp.sum(gdt.reshape(nrow, SSM_H, SSM_P), axis=1), axis=-1, keepdims=True)
        gbr_ref[...] = gbr
        gbi_ref[...] = gbi

    small = jax.ShapeDtypeStruct((nrow, SSM_P), F32)
    big = jax.ShapeDtypeStruct(lrx.shape, F32)
    return pl.pallas_call(body, out_shape=[small, small, jax.ShapeDtypeStruct((nrow, 1), F32), big, big],
                          name=name)(lrx, lix, ldtx, brt, bit, dar, dai, dbr, dbi)


HS = 1024
GQ, QC, QS = 8, 128, 512
LC = QS
SCAN_UNROLL = ST


def _scan_steps(step, carry):
    if SCAN_UNROLL >= ST:
        for s in range(ST):
            carry = step(s, carry)
        return carry

    def body(i, c):
        for j in range(SCAN_UNROLL):
            c = step(i * SCAN_UNROLL + j, c)
        return c

    return lax.fori_loop(0, ST // SCAN_UNROLL, body, carry)


def _tile_row(s):
    return s * 8 if isinstance(s, int) else pl.multiple_of(s * 8, 8)


def _dir_cat(x, d0, qq):
    xq = x[:, QC * qq:QC * qq + QC]
    zero = jnp.zeros_like(xq)
    return jnp.concatenate([jnp.where(d0, xq, zero), jnp.where(d0, zero, xq)], axis=1)


def _dir_pick(x, d0):
    return jnp.where(d0, x[:, :QC], x[:, QC:])


def _d0_rows(n):
    row = lax.broadcasted_iota(jnp.int32, (n, 1), 0)
    return jnp.bitwise_and(row, 4) == 0


def _scan_perm(bl):
    n = 2 * bl * ST
    p = np.zeros((n, n), np.float32)
    for s in range(ST):
        for d in range(2):
            for b in range(bl):
                t = s if d == 0 else ST - 1 - s
                p[s * 2 * bl + d * bl + b, d * bl * ST + b * ST + t] = 1.0
    return p


def _scan_maps(lay):
    spc = TB // ST
    nlc = lay.nlb * spc

    def fwd(k):
        return k // spc, k % spc

    def rev(k):
        cpos = nlc - 1 - jnp.maximum(k - spc, 0)
        return jnp.where(k < spc, 0, 1 + cpos // spc), jnp.where(k < spc, spc - 1 - k, cpos % spc)

    return fwd, rev


def _pack_rows(f_ref, r_ref, p_ref, rc):
    st = jnp.concatenate([f_ref[0].reshape(rc // 2, 256), r_ref[0].reshape(rc // 2, 256)], axis=0).astype(MXU_DTYPE)
    return _nn(p_ref[...], st).astype(MXU_DTYPE)


def _side_split(refs, n_in, n_out, n_scr, side):
    ns = side.n if side is not None else 0
    ins, sin = refs[:n_in], refs[n_in:n_in + ns]
    o0 = n_in + ns
    outs, sout = refs[o0:o0 + n_out], refs[o0 + n_out:o0 + n_out + ns]
    s0 = o0 + n_out + ns
    return ins + outs + refs[s0:s0 + n_scr], (sin, sout, refs[s0 + n_scr:])


def _side_start(side, srefs, first):
    if side is not None:
        @pl.when(first)
        def _():
            side.start(*srefs)


def _side_wait(side, srefs, last):
    if side is not None:
        @pl.when(last)
        def _():
            side.wait(*srefs)


def _ssm_fwd(lay, z, perm, bh, ch, ar8, ai8, name, side=None):
    bl = lay.bl
    rc = ST * 2 * bl
    nch = lay.nr * (TB // ST)
    fwd, rev = _scan_maps(lay)
    z4 = z.reshape(lay.nr, bl, TB, z.shape[1])

    def body(*refs):
        own, srefs = _side_split(refs, 7, 3, 2, side)
        uf_ref, ur_ref, p_ref, bh_ref, ch_ref, ar_ref, ai_ref, yf_ref, yr_ref, hst_ref, hs, hc = own
        f, k = pl.program_id(0), pl.program_id(1)
        _side_start(side, srefs, jnp.logical_and(f == 0, k == 0))

        @pl.when(k == 0)
        def _():
            hc[...] = jnp.zeros_like(hc)

        hst_ref[0] = hc[...]
        d0 = _d0_rows(rc)
        uv = _pack_rows(uf_ref, ur_ref, p_ref, rc)
        for q in range(2):
            cr, ci = 2 * QS * q, 2 * QS * q + QS
            hs[:, cr:cr + 2 * QS] = _nn(_dir_cat(uv, d0, q), bh_ref[q])
            ar = ar_ref[:, QS * q:QS * q + QS]
            ai = ai_ref[:, QS * q:QS * q + QS]

            def step(s, carry, cr=cr, ci=ci, ar=ar, ai=ai):
                hr, hi = carry
                base = _tile_row(s)
                nr = ar * hr - ai * hi + hs[pl.ds(base, 8), cr:cr + LC]
                ni = ar * hi + ai * hr + hs[pl.ds(base, 8), ci:ci + LC]
                hs[pl.ds(base, 8), cr:cr + LC] = nr
                hs[pl.ds(base, 8), ci:ci + LC] = ni
                return nr, ni

            hr, hi = _scan_steps(step, (hc[:, cr:cr + LC], hc[:, ci:ci + LC]))
            hc[:, cr:cr + LC] = hr
            hc[:, ci:ci + LC] = hi
        yi = jnp.concatenate(
            [_dir_pick(_nn(hs[:, 2 * QS * q:2 * QS * (q + 1)].astype(MXU_DTYPE), ch_ref[q]), d0) for q in range(2)], axis=1)
        yd = _tn(p_ref[...], yi.astype(MXU_DTYPE))
        yf_ref[0] = yd[:rc // 2].reshape(bl, ST, 256).astype(yf_ref.dtype)
        yr_ref[0] = yd[rc // 2:].reshape(bl, ST, 256).astype(yr_ref.dtype)
        _side_wait(side, srefs, jnp.logical_and(f == 1, k == nch - 1))

    sd = side if side is not None else _Side([])
    blk = (1, bl, ST, 256)
    ysh = jax.ShapeDtypeStruct((lay.nr, bl, TB, B_W), MXU_DTYPE)
    outs = pl.pallas_call(
        body, grid=(2, nch),
        in_specs=[pl.BlockSpec(blk, lambda f, k: (fwd(k)[0], 0, fwd(k)[1], 2 + f)),
                  pl.BlockSpec(blk, lambda f, k: (rev(k)[0], 0, rev(k)[1], 2 + f)),
                  pl.BlockSpec((rc, rc), lambda f, k: (0, 0)),
                  pl.BlockSpec((2, 2 * QC, 2 * QS), lambda f, k: (f, 0, 0)),
                  pl.BlockSpec((2, 2 * QS, 2 * QC), lambda f, k: (f, 0, 0)),
                  pl.BlockSpec((8, HS), lambda f, k: (0, f)), pl.BlockSpec((8, HS), lambda f, k: (0, f))] + sd.in_specs,
        out_specs=[pl.BlockSpec(blk, lambda f, k: (fwd(k)[0], 0, fwd(k)[1], f)),
                   pl.BlockSpec(blk, lambda f, k: (rev(k)[0], 0, rev(k)[1], f)),
                   pl.BlockSpec((1, 8, 2 * HS), lambda f, k: (k, 0, f))] + sd.out_specs,
        out_shape=[ysh, ysh, jax.ShapeDtypeStruct((nch, 8, 4 * HS), F32)] + sd.out_shape,
        scratch_shapes=[pltpu.VMEM((rc, 2 * HS), F32), pltpu.VMEM((8, 2 * HS), F32)] + (sd.scratch if side is not None else []),
        compiler_params=_cp(("arbitrary", "arbitrary")), name=name)(z4, z4, perm, bh, ch, ar8, ai8, *sd.arrays)
    yf, yr, hst = outs[:3]
    return yf.reshape(lay.nt, B_W), yr.reshape(lay.nt, B_W), hst, list(outs[3:])


def _ssm_bwd(lay, z, dy, perm, hst, bh, ch, ar8, ai8, name, side=None):
    bl = lay.bl
    rc = ST * 2 * bl
    nch = lay.nr * (TB // ST)
    fwd, rev = _scan_maps(lay)
    z4 = z.reshape(lay.nr, bl, TB, z.shape[1])
    dy4 = dy.reshape(lay.nr, bl, TB, B_W)

    def body(*refs):
        own, srefs = _side_split(refs, 10, 6, 5, side)
        (uf_ref, ur_ref, dyf_ref, dyr_ref, p_ref, hst_ref, bh_ref, ch_ref, ar_ref, ai_ref,
         duf_ref, dur_ref, dbh_ref, dch_ref, dar_ref, dai_ref, hs, es, ec, accr, acci) = own
        f, k = pl.program_id(0), pl.program_id(1)
        _side_start(side, srefs, jnp.logical_and(f == 0, k == 0))

        @pl.when(k == 0)
        def _():
            ec[...] = jnp.zeros_like(ec)
            accr[...] = jnp.zeros_like(accr)
            acci[...] = jnp.zeros_like(acci)
            dbh_ref[...] = jnp.zeros_like(dbh_ref)
            dch_ref[...] = jnp.zeros_like(dch_ref)

        d0 = _d0_rows(rc)
        uv = _pack_rows(uf_ref, ur_ref, p_ref, rc)
        dyv = _pack_rows(dyf_ref, dyr_ref, p_ref, rc)

        hs[0:8, :] = hst_ref[0]
        ucat, dycat = [], []
        for q in range(2):
            cr, ci = 2 * QS * q, 2 * QS * q + QS
            ucat.append(_dir_cat(uv, d0, q))
            dycat.append(_dir_cat(dyv, d0, q))
            hs[8:, cr:cr + 2 * QS] = _nn(ucat[q], bh_ref[q])
            ar = ar_ref[:, QS * q:QS * q + QS]
            ai = ai_ref[:, QS * q:QS * q + QS]

            def step(s, carry, cr=cr, ci=ci, ar=ar, ai=ai):
                hr, hi = carry
                base = _tile_row(s + 1)
                nr = ar * hr - ai * hi + hs[pl.ds(base, 8), cr:cr + LC]
                ni = ar * hi + ai * hr + hs[pl.ds(base, 8), ci:ci + LC]
                hs[pl.ds(base, 8), cr:cr + LC] = nr
                hs[pl.ds(base, 8), ci:ci + LC] = ni
                return nr, ni

            _scan_steps(step, (hs[0:8, cr:cr + LC], hs[0:8, ci:ci + LC]))
            dch_ref[q] += _tn(hs[8:, cr:cr + 2 * QS].astype(MXU_DTYPE), dycat[q])
            es[:, cr:cr + 2 * QS] = _nt(dycat[q], ch_ref[q])

        dui = []
        for q in range(2):
            cr, ci = 2 * QS * q, 2 * QS * q + QS
            ar = ar_ref[:, QS * q:QS * q + QS]
            ai = ai_ref[:, QS * q:QS * q + QS]

            def bstep(i, carry, cr=cr, ci=ci, ar=ar, ai=ai):
                er, ei, sr, si = carry
                base = _tile_row(ST - 1 - i)
                ner = es[pl.ds(base, 8), cr:cr + LC] + ar * er + ai * ei
                nei = es[pl.ds(base, 8), ci:ci + LC] - ai * er + ar * ei
                es[pl.ds(base, 8), cr:cr + LC] = ner
                es[pl.ds(base, 8), ci:ci + LC] = nei
                hpr = hs[pl.ds(base, 8), cr:cr + LC]
                hpi = hs[pl.ds(base, 8), ci:ci + LC]
                return ner, nei, sr + ner * hpr + nei * hpi, si - ner * hpi + nei * hpr

            lo = QS * q
            er, ei, sr, si = _scan_steps(
                bstep, (ec[:, cr:cr + LC], ec[:, ci:ci + LC], accr[:, lo:lo + LC], acci[:, lo:lo + LC]))
            ec[:, cr:cr + LC] = er
            ec[:, ci:ci + LC] = ei
            accr[:, lo:lo + LC] = sr
            acci[:, lo:lo + LC] = si
            eb = es[:, cr:cr + 2 * QS].astype(MXU_DTYPE)
            dui.append(_dir_pick(_nt(eb, bh_ref[q]), d0))
            dbh_ref[q] += _tn(ucat[q], eb)

        dud = _tn(p_ref[...], jnp.concatenate(dui, axis=1).astype(MXU_DTYPE))
        duf_ref[0] = dud[:rc // 2].reshape(bl, ST, 256).astype(duf_ref.dtype)
        dur_ref[0] = dud[rc // 2:].reshape(bl, ST, 256).astype(dur_ref.dtype)

        @pl.when(k == nch - 1)
        def _():
            for d in range(2):
                dar_ref[d:d + 1, :] = jnp.sum(accr[4 * d:4 * d + 4, :], axis=0, keepdims=True)
                dai_ref[d:d + 1, :] = jnp.sum(acci[4 * d:4 * d + 4, :], axis=0, keepdims=True)

        _side_wait(side, srefs, jnp.logical_and(f == 1, k == nch - 1))

    sd = side if side is not None else _Side([])
    last = lambda k: nch - 1 - k
    blk = (1, bl, ST, 256)
    fspec = lambda c0: pl.BlockSpec(blk, lambda f, k: (fwd(last(k))[0], 0, fwd(last(k))[1], c0 + f))
    rspec = lambda c0: pl.BlockSpec(blk, lambda f, k: (rev(last(k))[0], 0, rev(last(k))[1], c0 + f))
    dush = jax.ShapeDtypeStruct((lay.nr, bl, TB, B_W), MXU_DTYPE)
    outs = pl.pallas_call(
        body, grid=(2, nch),
        in_specs=[fspec(2), rspec(2), fspec(0), rspec(0),
                  pl.BlockSpec((rc, rc), lambda f, k: (0, 0)),
                  pl.BlockSpec((1, 8, 2 * HS), lambda f, k: (last(k), 0, f)),
                  pl.BlockSpec((2, 2 * QC, 2 * QS), lambda f, k: (f, 0, 0)),
                  pl.BlockSpec((2, 2 * QS, 2 * QC), lambda f, k: (f, 0, 0)),
                  pl.BlockSpec((8, HS), lambda f, k: (0, f)), pl.BlockSpec((8, HS), lambda f, k: (0, f))] + sd.in_specs,
        out_specs=[fspec(0), rspec(0),
                   pl.BlockSpec((2, 2 * QC, 2 * QS), lambda f, k: (f, 0, 0)),
                   pl.BlockSpec((2, 2 * QS, 2 * QC), lambda f, k: (f, 0, 0)),
                   pl.BlockSpec((2, HS), lambda f, k: (0, f)), pl.BlockSpec((2, HS), lambda f, k: (0, f))] + sd.out_specs,
        out_shape=[dush, dush, jax.ShapeDtypeStruct((4, 2 * QC, 2 * QS), F32),
                   jax.ShapeDtypeStruct((4, 2 * QS, 2 * QC), F32), jax.ShapeDtypeStruct((2, 2 * HS), F32),
                   jax.ShapeDtypeStruct((2, 2 * HS), F32)] + sd.out_shape,
        scratch_shapes=[pltpu.VMEM((rc + 8, 2 * HS), F32), pltpu.VMEM((rc, 2 * HS), F32), pltpu.VMEM((8, 2 * HS), F32),
                        pltpu.VMEM((8, HS), F32), pltpu.VMEM((8, HS), F32)] + (sd.scratch if side is not None else []),
        compiler_params=_cp(("arbitrary", "arbitrary")), name=name)(z4, z4, dy4, dy4, perm, hst, bh, ch, ar8, ai8, *sd.arrays)
    duf, dur, dbh, dch, dar, dai = outs[:6]
    return duf.reshape(lay.nt, B_W), dur.reshape(lay.nt, B_W), dbh, dch, dar, dai, list(outs[6:])


def _glu_fwd(lay, z, yf, yr, dvec, wglu, bglu, name):
    def body(u_ref, yf_ref, yr_ref, d_ref, w_ref, b_ref, o_ref, y_ref):
        y = yf_ref[...].astype(F32) + yr_ref[...].astype(F32) + d_ref[...] * u_ref[...].astype(F32)
        y_ref[...] = y
        g = _gelu(y)
        pre = _nn(g.astype(MXU_DTYPE), w_ref[...]) + b_ref[...]
        o_ref[...] = (g * _sigmoid(pre)).astype(o_ref.dtype)

    tok = pl.BlockSpec((MT * TB, B_W), lambda j: (j, 0))
    vec = pl.BlockSpec((1, B_W), lambda j: (0, 0))
    return pl.pallas_call(
        body, grid=(lay.nb // MT,),
        in_specs=[pl.BlockSpec((MT * TB, B_W), lambda j: (j, 1)), tok, tok, vec,
                  pl.BlockSpec((B_W, B_W), lambda j: (0, 0)), vec],
        out_specs=[tok, tok],
        out_shape=[jax.ShapeDtypeStruct((lay.nt, B_W), MXU_DTYPE), jax.ShapeDtypeStruct((lay.nt, B_W), F32)],
        compiler_params=_cp(("parallel",)), name=name)(z, yf, yr, dvec, wglu, bglu)


def _glu_bwd(lay, z, y, ds, dvec, wglu, bglu, name):
    def body(u_ref, y_ref, ds_ref, d_ref, w_ref, b_ref, dy_ref, dud_ref, dw_ref, db_ref, dd_ref):
        j = pl.program_id(0)

        @pl.when(j == 0)
        def _():
            dw_ref[...] = jnp.zeros_like(dw_ref)
            db_ref[...] = jnp.zeros_like(db_ref)
            dd_ref[...] = jnp.zeros_like(dd_ref)

        yv = y_ref[...]
        g = _gelu(yv)
        gb = g.astype(MXU_DTYPE)
        sg = _sigmoid(_nn(gb, w_ref[...]) + b_ref[...])
        dsv = ds_ref[...].astype(F32)
        dpre = dsv * g * sg * (1.0 - sg)
        dpre_b = dpre.astype(MXU_DTYPE)
        dg = dsv * sg + _nt(dpre_b, w_ref[...])
        dw_ref[...] += _tn(gb, dpre_b)
        db_ref[...] += jnp.sum(dpre, axis=0, keepdims=True)
        dy = dg * _gelu_grad(yv)
        dy_ref[...] = dy.astype(dy_ref.dtype)
        dd_ref[...] += jnp.sum(dy * u_ref[...].astype(F32), axis=0, keepdims=True)
        dud_ref[...] = (dy * d_ref[...]).astype(dud_ref.dtype)

    tok = pl.BlockSpec((MT * TB, B_W), lambda j: (j, 0))
    vec = pl.BlockSpec((1, B_W), lambda j: (0, 0))
    mat = pl.BlockSpec((B_W, B_W), lambda j: (0, 0))
    vsh = jax.ShapeDtypeStruct((1, B_W), F32)
    return pl.pallas_call(
        body, grid=(lay.nb // MT,),
        in_specs=[pl.BlockSpec((MT * TB, B_W), lambda j: (j, 1)), tok, tok, vec, mat, vec],
        out_specs=[tok, tok, mat, vec, vec],
        out_shape=[jax.ShapeDtypeStruct((lay.nt, B_W), MXU_DTYPE), jax.ShapeDtypeStruct((lay.nt, B_W), F32),
                   jax.ShapeDtypeStruct((B_W, B_W), F32), vsh, vsh],
        compiler_params=_cp(("arbitrary",)), name=name)(z, y, ds, dvec, wglu, bglu)


def _dz_assemble(lay, dz_a, duf, dur, dud, dz_p, name):
    def body(a_ref, f_ref, r_ref, d_ref, p_ref, o_ref):
        o_ref[:, :2 * A_W] = a_ref[...].astype(o_ref.dtype)
        o_ref[:, 2 * A_W:2 * A_W + B_W] = (f_ref[...].astype(F32) + r_ref[...].astype(F32) + d_ref[...]).astype(o_ref.dtype)
        o_ref[:, 2 * A_W + B_W:] = p_ref[...].astype(o_ref.dtype)

    spec = lambda w: pl.BlockSpec((MT * TB, w), lambda j: (j, 0))
    return pl.pallas_call(
        body, grid=(lay.nb // MT,), in_specs=[spec(2 * A_W), spec(B_W), spec(B_W), spec(B_W), spec(C_W)],
        out_specs=spec(D_IN), out_shape=jax.ShapeDtypeStruct((lay.nt, D_IN), MXU_DTYPE),
        compiler_params=_cp(("parallel",)), name=name)(dz_a, duf, dur, dud, dz_p)


def _expand_rows(a):
    return jnp.broadcast_to(a[:, :, None, :], (2, SSM_G, SSM_H, SSM_P)).reshape(-1, SSM_P)


def _ssm_params(lam_re, lam_im, log_dt, b_re, b_im, c_re, c_im, name):
    lrx, lix = _expand_rows(lam_re), _expand_rows(lam_im)
    ldtx = _expand_rows(jnp.broadcast_to(log_dt[:, :, None], (2, SSM_G, SSM_P)))
    brt = jnp.transpose(b_re, (0, 1, 3, 2)).reshape(-1, SSM_P)
    bit = jnp.transpose(b_im, (0, 1, 3, 2)).reshape(-1, SSM_P)
    arx, aix, bbr, bbi = _disc_fwd(lrx, lix, ldtx, brt, bit, name)
    ar = arx.reshape(2, SSM_G, SSM_H, SSM_P)[:, :, 0].reshape(2, SSM_G * SSM_P)
    ai = aix.reshape(2, SSM_G, SSM_H, SSM_P)[:, :, 0].reshape(2, SSM_G * SSM_P)
    eye = jnp.eye(GQ, dtype=F32)

    def bmat(bt):
        t = bt.reshape(2, 4, GQ, SSM_H, SSM_P)
        return jnp.einsum('dqghp,gk->qdghkp', t, eye).reshape(4, 2 * QC, QS)

    bh = jnp.concatenate([bmat(bbr), bmat(bbi)], axis=-1).astype(MXU_DTYPE)

    def cmat(c):
        t = c.reshape(2, 4, GQ, SSM_H, SSM_P)
        return jnp.einsum('dqghp,gk->qgpdkh', t, eye).reshape(4, QS, 2 * QC)

    ch = jnp.concatenate([cmat(c_re), -cmat(c_im)], axis=1).astype(MXU_DTYPE)

    def rows8(a):
        return jnp.repeat(a, 4, axis=0)

    return dict(lrx=lrx, lix=lix, ldtx=ldtx, brt=brt, bit=bit, bh=bh, ch=ch, ar8=rows8(ar), ai8=rows8(ai))


def _ssm_param_grads(sp, dbh, dch, dar, dai, name):
    def bdiag(m):
        t = m.reshape(4, 2, GQ, SSM_H, GQ, SSM_P)
        return jnp.einsum('qdghgp->dqghp', t).reshape(-1, SSM_P)

    dbr, dbi = bdiag(dbh[..., :QS]), bdiag(dbh[..., QS:])

    def cdiag(m):
        t = m.reshape(4, GQ, SSM_P, 2, GQ, SSM_H)
        return jnp.einsum('qgpdgh->dqghp', t).reshape(2, SSM_G, SSM_H, SSM_P)

    dc_re, dc_im = cdiag(dch[:, :QS]), -cdiag(dch[:, QS:])

    def hrow(a):
        t = a.reshape(2, SSM_G, 1, SSM_P)
        return jnp.concatenate([t, jnp.zeros((2, SSM_G, SSM_H - 1, SSM_P), F32)], axis=2).reshape(-1, SSM_P)

    glr, gli, gdt, gbr, gbi = _disc_bwd(sp["lrx"], sp["lix"], sp["ldtx"], sp["brt"], sp["bit"],
                                        hrow(dar), hrow(dai), dbr, dbi, name)
    to_b = lambda g: jnp.transpose(g.reshape(2, SSM_G, SSM_H, SSM_P), (0, 1, 3, 2))
    return dict(ssm_lam_re=glr.reshape(2, SSM_G, SSM_P), ssm_lam_im=gli.reshape(2, SSM_G, SSM_P),
                ssm_log_dt=gdt.reshape(2, SSM_G), ssm_b_re=to_b(gbr), ssm_b_im=to_b(gbi),
                ssm_c_re=dc_re, ssm_c_im=dc_im)


def _layer_consts(p):
    c = {}
    c["ws"] = p["sgu_w"].astype(MXU_DTYPE)
    c["wst"] = jnp.transpose(p["sgu_w"], (0, 2, 1)).astype(MXU_DTYPE)
    c["gbias"] = jnp.repeat(p["sgu_b"].T, 64, axis=1)
    pw = jnp.zeros((C_W, C_W), F32)
    for i in range(4):
        pw = pw.at[64 * i:64 * i + 64, 64 * i:64 * i + 64].set(p["pool_w"][i])
    c["pw"] = pw.astype(MXU_DTYPE)
    c["pscale"] = p["pool_scale"].reshape(1, C_W)
    c["dvec"] = p["ssm_d"].reshape(1, B_W)
    c["bglu"] = p["glu_b"].reshape(1, B_W)
    return c


def _layer_fwd(lay, i, x, modarr, p, w, cst, sp, bands, inv, perm, sides=None):
    n = f"l{i}_"
    sides = sides or {}
    win_side, win_fill = sides.get("win", (None, None))
    ssm_side, ssm_fill = sides.get("ssm", (None, None))
    ffn_side, ffn_fill = sides.get("ffn", (None, None))
    res = {"x0": x}
    h = _normmod_fwd(lay, x, p["norm_mix_pre"].reshape(1, D), modarr, 0, 1, n + "nm1")
    z = _mm([(h, w["win_t"])], True, MXU_DTYPE, n + "win", side=win_side)
    if win_side is not None:
        z, extra = z
        win_fill(extra)
    a = _gate_fwd(lay, z, cst["ws"], cst["gbias"], n + "gate")
    yf, yr, hst, extra = _ssm_fwd(lay, z, perm, sp["bh"], sp["ch"], sp["ar8"], sp["ai8"], n + "ssm", ssm_side)
    if ssm_side is not None:
        ssm_fill(extra)
    s, y = _glu_fwd(lay, z, yf, yr, cst["dvec"], w["wglu"], cst["bglu"], n + "glu")
    c = _pool_fwd(lay, z, bands, inv, cst["pw"], cst["pscale"], n + "pool")
    mcat = jnp.concatenate([s, a, c], axis=1)
    res["wout_p"] = _perm_wout(w["wout"])
    m = _mm([(mcat, res["wout_p"])], False, MXU_DTYPE, n + "wout")
    x1 = _resnorm_fwd(lay, x, m, p["norm_mix_post"].reshape(1, D), modarr, 2, n + "rn1")
    h2 = _normmod_fwd(lay, x1, p["norm_ffn_pre"].reshape(1, D), modarr, 3, 4, n + "nm2")
    g, u, act, extra = _ffn_up(h2, w["wg_t"], w["wu_t"], n + "ffn_up", ffn_side)
    if ffn_side is not None:
        ffn_fill(extra)
    f = _mm([(act, w["wd"])], False, MXU_DTYPE, n + "ffn_down")
    x2 = _resnorm_fwd(lay, x1, f, p["norm_ffn_post"].reshape(1, D), modarr, 5, n + "rn2")
    res.update(h=h, z=z, hst=hst, y=y, mcat=mcat, m=m, x1=x1, h2=h2, g=g, u=u, act=act, f=f)
    return x2, res


def _layer_bwd(lay, i, dx2, modarr, p, w, cst, sp, bands, inv, perm, res, side_fns=None):
    n = f"l{i}b_"
    big, small = {}, {}
    side_fns = side_fns or {}
    side_of = lambda key: side_fns[key](big) if key in side_fns else None
    df, dg2, gpost2 = _resnorm_bwd(lay, dx2, res["f"], p["norm_ffn_post"].reshape(1, D), modarr, 5, n + "rn2")
    big["wd"] = _mm_tn(res["act"], df, MXU_DTYPE, n + "dwd")
    dg, du, early = _ffn_down_bwd(df, w["wd"], res["g"], res["u"], n + "ffn_down", side_of("ffn_down"))
    dh2_side = side_of("dh2")
    dh2 = _mm([(dg, w["wg_t"]), (du, w["wu_t"])], False, MXU_DTYPE, n + "dh2", side=dh2_side)
    if dh2_side is not None:
        dh2, ex = dh2
        early = early + ex
    big["wg_t"] = _mm_tn(dg, res["h2"], MXU_DTYPE, n + "dwg")
    big["wu_t"] = _mm_tn(du, res["h2"], MXU_DTYPE, n + "dwu")
    dx1, dsh2, dsc2, gpre2 = _normmod_bwd(lay, res["x1"], dh2, dx2, p["norm_ffn_pre"].reshape(1, D), modarr, 4, n + "nm2")
    dm, dg1, gpost1 = _resnorm_bwd(lay, dx1, res["m"], p["norm_mix_post"].reshape(1, D), modarr, 2, n + "rn1")
    big["wout"] = _unperm_wout(_mm_tn(res["mcat"], dm, MXU_DTYPE, n + "dwout"))
    dmcat = _mm([(dm, res["wout_p"])], True, MXU_DTYPE, n + "dmcat")
    z = res["z"]
    dz_a, dws, dgb = _gate_bwd(lay, z, dmcat, cst["ws"], cst["wst"], cst["gbias"], n + "gate")
    dy, dud, dwglu, dbglu, ddvec = _glu_bwd(lay, z, res["y"], dmcat, cst["dvec"], w["wglu"], cst["bglu"], n + "glu")
    big["wglu"] = dwglu.astype(MXU_DTYPE)
    duf, dur, dbh, dch, dar, dai, ex = _ssm_bwd(lay, z, dy, perm, res["hst"], sp["bh"], sp["ch"], sp["ar8"],
                                                sp["ai8"], n + "ssm", side_of("ssm"))
    early = early + ex
    dz_p, dpw, dpsc = _pool_bwd(lay, z, dmcat, bands, inv, cst["pw"], cst["pscale"], n + "pool")
    dz = _dz_assemble(lay, dz_a, duf, dur, dud, dz_p, n + "dz")
    big["win_t"] = _mm_tn(dz, res["h"], MXU_DTYPE, n + "dwin")
    dh_side = side_of("dh")
    dh = _mm([(dz, w["win_t"])], False, MXU_DTYPE, n + "dh", side=dh_side)
    if dh_side is not None:
        dh, ex = dh
        early = early + ex
    dx, dsh1, dsc1, gpre1 = _normmod_bwd(lay, res["x0"], dh, dx1, p["norm_mix_pre"].reshape(1, D), modarr, 1, n + "nm1",
                                         latent_only=(i == 0))

    small.update(norm_mix_pre=gpre1[0], norm_mix_post=gpost1[0], norm_ffn_pre=gpre2[0], norm_ffn_post=gpost2[0])
    small["sgu_w"] = dws
    small["sgu_b"] = jnp.sum(dgb.reshape(CHUNK, 4, 64), axis=-1).T
    small.update(_ssm_param_grads(sp, dbh, dch, dar, dai, n + "disc"))
    small["ssm_d"] = ddvec.reshape(SSM_G, SSM_H)
    small["glu_b"] = dbglu[0]
    small["pool_w"] = jnp.stack([dpw[64 * k:64 * k + 64, 64 * k:64 * k + 64] for k in range(4)])
    small["pool_scale"] = dpsc[0]
    dmod = jnp.concatenate([dsh1, dsc1, dg1, dsh2, dsc2, dg2], axis=1)[:lay.bl + 1]
    dmod = jnp.concatenate([dmod, jnp.zeros((8 - lay.bl - 1, 6, D), F32)], axis=0)
    return dx, big, small, dmod, early


def _perm_wout(w):
    return w.reshape(4, D // 4, D)[np.array(WOUT_PERM)].reshape(D, D)


def _unperm_wout(g):
    return g.reshape(4, D // 4, D)[np.array(WOUT_INV)].reshape(D, D)


SMALL_NAMES = ["norm_mix_pre", "norm_mix_post", "norm_ffn_pre", "norm_ffn_post", "sgu_w", "sgu_b", "ssm_lam_re",
               "ssm_lam_im", "ssm_log_dt", "ssm_b_re", "ssm_b_im", "ssm_c_re", "ssm_c_im", "ssm_d", "glu_b", "pool_w",
               "pool_scale"]
BIG_NAMES = ["win_t", "wout", "wglu", "wg_t", "wu_t", "wd"]


def _sincos_2d(rows, cols, dim):
    quarter = dim // 4
    omega = 1.0 / (10000.0 ** (jnp.arange(quarter, dtype=F32) / quarter))
    r = jnp.arange(rows, dtype=F32)[:, None] * omega
    cc = jnp.arange(cols, dtype=F32)[:, None] * omega
    er = jnp.concatenate([jnp.sin(r), jnp.cos(r)], axis=-1)
    ec = jnp.concatenate([jnp.sin(cc), jnp.cos(cc)], axis=-1)
    pe = jnp.concatenate([jnp.broadcast_to(er[:, None, :], (rows, cols, dim // 2)),
                          jnp.broadcast_to(ec[None, :, :], (rows, cols, dim // 2))], axis=-1)
    return pe.reshape(rows * cols, dim)


def _core(x, ctx, target, mods_local, params, weights, w_sides=None, g_side_fns=None):
    bl, lat, _ = x.shape
    assert bl == 4 and lat % TB == 0, "the scan fills 8 sublanes with 2 directions x 4 sequences"
    lay = _Layout(bl, lat)
    pe = _sincos_2d(lat // GRID_W, GRID_W, D)
    bands_np, inv_np = _band_constants()
    bands, inv = jnp.asarray(bands_np, MXU_DTYPE), jnp.asarray(inv_np, F32)
    perm = jnp.asarray(_scan_perm(bl), MXU_DTYPE)
    csts, sps, ress, wls = [], [], [], []
    for i in range(2):
        csts.append(_layer_consts(params[i]))
        p = params[i]
        sps.append(_ssm_params(p["ssm_lam_re"], p["ssm_lam_im"], p["ssm_log_dt"], p["ssm_b_re"], p["ssm_b_im"],
                               p["ssm_c_re"], p["ssm_c_im"], f"l{i}_disc"))
        wls.append(dict(weights[i]))

    embed_side, embed_fill = (w_sides[0].get("embed") if w_sides else None) or (None, None)
    xt, extra = _embed(lay, x, ctx, pe, embed_side)
    if embed_side is not None:
        embed_fill(wls, extra)
    if callable(mods_local):
        mods_local = mods_local()
    modarrs = [lay.mod_tiles(mods_local[i]) for i in range(2)]
    for i in range(2):
        sides = {}
        for key, (side, fill) in ((w_sides or [{}, {}])[i]).items():
            sides[key] = (side, functools.partial(fill, wls))
        xt, res = _layer_fwd(lay, i, xt, modarrs[i], params[i], wls[i], csts[i], sps[i], bands, inv, perm, sides)
        ress.append(res)
    dx, lossv = _loss_bwd(lay, xt, target)
    bigs, smalls, dmods, early = [None, None], [None, None], [None, None], []
    for i in (1, 0):
        fns = {}
        if i == 0 and g_side_fns is not None:
            fns = {key: functools.partial(fn, bigs[1]) for key, fn in g_side_fns.items()}
        dx, bigs[i], smalls[i], dmods[i], ex = _layer_bwd(lay, i, dx, modarrs[i], params[i], wls[i], csts[i], sps[i],
                                                           bands, inv, perm, ress[i], fns)
        early += ex
    return lossv[0, 0], dx.reshape(bl, lat, D), bigs, smalls, dmods, early


def _my_index():
    return 4 * lax.axis_index("x") + 2 * lax.axis_index("y") + lax.axis_index("c")


def _peer(k):
    x, y, c = lax.axis_index("x"), lax.axis_index("y"), lax.axis_index("c")
    kx, ky, kc = (k >> 2) & 1, (k >> 1) & 1, k & 1
    px = 1 - x if kx else x
    py = 1 - y if ky else y
    pc = 1 - c if kc else c
    return (px, py, pc), 4 * px + 2 * py + pc


class _Side:
    def __init__(self, items):
        self.items = items
        self.n = len(items)
        self.ncopies = sum(len(it[2]) for it in items)
        self.arrays = [it[0] for it in items]
        anyspec = pl.BlockSpec(memory_space=pl.ANY)
        self.in_specs = [anyspec] * self.n
        self.out_specs = [anyspec] * self.n
        self.out_shape = [jax.ShapeDtypeStruct((slots,) + tuple(a.shape) if mode == "gather" else tuple(a.shape), a.dtype)
                          for a, mode, ks, slots in items]
        self.scratch = [pltpu.SemaphoreType.DMA((self.ncopies,)), pltpu.SemaphoreType.DMA((self.ncopies,)),
                        pltpu.SemaphoreType.DMA((self.n,))]

    def _copies(self, ins, outs, sems):
        send_sems, recv_sems, local_sems = sems
        slot_of = lambda idx, slots: idx if slots == 8 else (idx // 2 if slots == 4 else idx % 2)
        me = _my_index()
        local, sends, recvs = [], [], []
        q = 0
        for t, (arr, mode, ks, slots) in enumerate(self.items):
            src_own = ins[t] if mode == "gather" else ins[t].at[me]
            local.append(pltpu.make_async_copy(src_own, outs[t].at[slot_of(me, slots)], local_sems.at[t]))
            for k in ks:
                peer, pidx = _peer(k)
                src = ins[t] if mode == "gather" else ins[t].at[pidx]
                sends.append(pltpu.make_async_remote_copy(
                    src_ref=src, dst_ref=outs[t].at[slot_of(me, slots)], send_sem=send_sems.at[q], recv_sem=recv_sems.at[q],
                    device_id=peer, device_id_type=pl.DeviceIdType.MESH))
                recvs.append(pltpu.make_async_remote_copy(
                    src_ref=src, dst_ref=outs[t].at[slot_of(pidx, slots)], send_sem=send_sems.at[q], recv_sem=recv_sems.at[q],
                    device_id=peer, device_id_type=pl.DeviceIdType.MESH))
                q += 1
        return local, sends, recvs

    def start(self, ins, outs, sems):
        local, sends, _ = self._copies(ins, outs, sems)
        for cp in sends + local:
            cp.start()

    def wait(self, ins, outs, sems):
        local, sends, recvs = self._copies(ins, outs, sems)
        for cp in recvs:
            cp.wait_recv()
        for cp in sends:
            cp.wait_send()
        for cp in local:
            cp.wait()


def _comm(items, name):
    side = _Side(items)
    n = side.n

    def body(*refs):
        ins, outs, sems = refs[:n], refs[n:2 * n], refs[2 * n:]
        side.start(ins, outs, sems)
        side.wait(ins, outs, sems)

    return pl.pallas_call(
        body, in_specs=side.in_specs, out_specs=side.out_specs, out_shape=side.out_shape, scratch_shapes=side.scratch,
        compiler_params=pltpu.CompilerParams(has_side_effects=True), name=name)(*side.arrays)


def _spread(items, name):
    n = len(items)
    ncopies = sum(len(it[1]) for it in items)

    def slot_of(idx, slots):
        return idx if slots == 8 else (idx // 2 if slots == 4 else idx % 2)

    def body(*refs):
        ins, outs, bufs = refs[:n], refs[n:2 * n], refs[2 * n:3 * n]
        load_sems, store_sems, send_sems, recv_sems = refs[3 * n:]
        me = _my_index()
        loads = [pltpu.make_async_copy(ins[t], bufs[t], load_sems.at[t]) for t in range(n)]
        for cp in loads:
            cp.start()
        stores, sends, recvs = [], [], []
        q = 0
        for t, (arr, ks, slots) in enumerate(items):
            loads[t].wait()
            own = outs[t].at[slot_of(me, slots)]
            stores.append(pltpu.make_async_copy(bufs[t], own, store_sems.at[t]))
            stores[-1].start()
            for k in ks:
                peer, pidx = _peer(k)
                sends.append(pltpu.make_async_remote_copy(
                    src_ref=bufs[t], dst_ref=own, send_sem=send_sems.at[q], recv_sem=recv_sems.at[q],
                    device_id=peer, device_id_type=pl.DeviceIdType.MESH))
                recvs.append(pltpu.make_async_remote_copy(
                    src_ref=bufs[t], dst_ref=outs[t].at[slot_of(pidx, slots)], send_sem=send_sems.at[q],
                    recv_sem=recv_sems.at[q], device_id=peer, device_id_type=pl.DeviceIdType.MESH))
                sends[-1].start()
                q += 1
        for cp in recvs:
            cp.wait_recv()
        for cp in sends:
            cp.wait_send()
        for cp in stores:
            cp.wait()

    anyspec = pl.BlockSpec(memory_space=pl.ANY)
    return pl.pallas_call(
        body, in_specs=[anyspec] * n, out_specs=[anyspec] * n,
        out_shape=[jax.ShapeDtypeStruct((slots,) + tuple(arr.shape), arr.dtype) for arr, ks, slots in items],
        scratch_shapes=[pltpu.VMEM(tuple(arr.shape), arr.dtype) for arr, ks, slots in items]
        + [pltpu.SemaphoreType.DMA((n,)), pltpu.SemaphoreType.DMA((n,)), pltpu.SemaphoreType.DMA((ncopies,)),
           pltpu.SemaphoreType.DMA((ncopies,))],
        compiler_params=pltpu.CompilerParams(has_side_effects=True, vmem_limit_bytes=VMEM_LIMIT),
        name=name)(*[it[0] for it in items])


ALL7 = (1, 2, 3, 4, 5, 6, 7)
CHIPS3 = (2, 4, 6)


def _sum8(parts, name):
    def one(a, nm):
        _, r, c = a.shape
        tr = r if r <= 512 else _pick_rows(r)

        def body(a_ref, o_ref):
            acc = a_ref[0].astype(F32)
            for q in range(1, a_ref.shape[0]):
                acc = acc + a_ref[q].astype(F32)
            o_ref[...] = acc

        return pl.pallas_call(
            body, grid=(r // tr,), in_specs=[pl.BlockSpec((a.shape[0], tr, c), lambda i: (0, i, 0))],
            out_specs=pl.BlockSpec((tr, c), lambda i: (i, 0)), out_shape=jax.ShapeDtypeStruct((r, c), F32),
            compiler_params=_cp(("parallel",)), name=nm)(a)

    return [one(a, f"{name}{i}") for i, a in enumerate(parts)]


def _pick_rows(r, cap=512):
    for t in (512, 352, 256, 176, 128, 64, 32, 16, 8):
        if r % t == 0 and t <= cap:
            return t
    return r


def _adam(w, g, m, v, name):
    shape = w.shape
    nel = int(np.prod(shape))
    c1 = 1.0 / (1.0 - ADAM_B1 ** ADAM_STEP)
    c2 = 1.0 / (1.0 - ADAM_B2 ** ADAM_STEP)

    def body(w_ref, g_ref, m_ref, v_ref, d_ref, nm_ref, nv_ref):
        gv = g_ref[...]
        nm = ADAM_B1 * m_ref[...] + (1.0 - ADAM_B1) * gv
        nv = ADAM_B2 * v_ref[...] + (1.0 - ADAM_B2) * (gv * gv)
        d_ref[...] = -ADAM_LR * ((nm * c1) / (jnp.sqrt(nv * c2) + ADAM_EPS) + ADAM_WD * w_ref[...])
        nm_ref[...] = nm
        nv_ref[...] = nv

    padded = int(np.prod(shape[:-2])) * (-(-shape[-2] // 8) * 8) * (-(-shape[-1] // 128) * 128) if len(shape) >= 2 else nel
    if len(shape) >= 2 and padded <= 1024 * 1024:
        sh = jax.ShapeDtypeStruct(shape, F32)
        return pl.pallas_call(body, out_shape=[sh] * 3, compiler_params=_cp(None), name=name)(w, g, m, v)

    if len(shape) >= 2 and shape[-1] >= 128:
        lanes = shape[-1]
    else:
        lanes = 512 if nel % 512 == 0 else 128
    r = nel // lanes
    tr = r if r * lanes <= 384 * 1024 else _pick_rows(r, 384 * 1024 // lanes)

    spec = pl.BlockSpec((tr, lanes), lambda i: (i, 0))
    sh = jax.ShapeDtypeStruct((r, lanes), F32)
    outs = pl.pallas_call(
        body, grid=(r // tr,), in_specs=[spec] * 4, out_specs=[spec] * 3, out_shape=[sh] * 3,
        compiler_params=_cp(("parallel",)), name=name)(*[a.reshape(r, lanes) for a in (w, g, m, v)])
    return [o.reshape(shape) for o in outs]


def _silu(x):
    return x * _sigmoid(x)


def _mod_fwd(c_rows, w_mod, b_cols, name):
    def body(c_ref, w_ref, b_ref, o_ref):
        s = _silu(c_ref[...])
        for l in range(2):
            o_ref[l] = jnp.dot(s, w_ref[l], preferred_element_type=F32, precision=lax.Precision.HIGHEST) + b_ref[l]

    nc = w_mod.shape[2]
    return pl.pallas_call(body, out_shape=jax.ShapeDtypeStruct((2, c_rows.shape[0], nc), F32),
                          compiler_params=_cp(None), name=name)(c_rows, w_mod, b_cols)


def _mod_bwd(c_rows, w_mod, dlat, dctx8, name):
    nrow = c_rows.shape[0]
    nb = nrow - 8

    def body(c_ref, w_ref, dl_ref, dc_ref, gw_ref, gc_ref):
        s = _silu(c_ref[...])
        ctx_row = lax.broadcasted_iota(jnp.int32, (nrow, 1), 0) == nb
        gc = jnp.zeros((1, D), F32)
        for l in range(2):
            dctx = dc_ref[0, l]
            for q in range(1, 8):
                dctx = dctx + dc_ref[q, l]
            dm = dl_ref[l] + jnp.where(ctx_row, dctx, 0.0)
            gw_ref[l] = lax.dot_general(s, dm, (((0,), (0,)), ((), ())), preferred_element_type=F32,
                                        precision=lax.Precision.HIGHEST)
            gc = gc + lax.dot_general(dctx, w_ref[l], (((1,), (1,)), ((), ())), preferred_element_type=F32,
                                      precision=lax.Precision.HIGHEST)
        gc_ref[...] = gc

    nc = w_mod.shape[2]
    return pl.pallas_call(body, out_shape=[jax.ShapeDtypeStruct((2, D, nc), F32), jax.ShapeDtypeStruct((1, D), F32)],
                          compiler_params=_cp(None), name=name)(c_rows, w_mod, dlat, dctx8)


def _bmod_cctx(dmod_all, gc4, c_ctx, name):
    def body(dm_ref, gc_ref, cc_ref, gb_ref, gcc_ref):
        for l in range(2):
            acc = jnp.sum(dm_ref[0, l], axis=0, keepdims=True)
            for q in range(1, 8):
                acc = acc + jnp.sum(dm_ref[q, l], axis=0, keepdims=True)
            gb_ref[l:l + 1, :] = acc
        g = gc_ref[0] + gc_ref[1] + gc_ref[2] + gc_ref[3]
        cv = cc_ref[...]
        sg = _sigmoid(cv)
        gcc_ref[...] = g * (sg * (1.0 + cv * (1.0 - sg)))

    return pl.pallas_call(body, out_shape=[jax.ShapeDtypeStruct((2, 6 * D), F32), jax.ShapeDtypeStruct((1, D), F32)],
                          compiler_params=_cp(None), name=name)(dmod_all, gc4, c_ctx)


def kernel(x, c, ctx, c_ctx, w_mod, b_mod, norm_mix_pre, norm_mix_post, norm_ffn_pre, norm_ffn_post, w_in, w_out, sgu_w, sgu_b, ssm_lam_re, ssm_lam_im, ssm_log_dt, ssm_b_re, ssm_b_im, ssm_c_re, ssm_c_im, ssm_d, glu_w, glu_b, pool_w, pool_scale, ffn_w_gate, ffn_w_up, ffn_w_down, loss_target, m_c_ctx, m_w_mod, m_b_mod, m_norm_mix_pre, m_norm_mix_post, m_norm_ffn_pre, m_norm_ffn_post, m_w_in, m_w_out, m_sgu_w, m_sgu_b, m_ssm_lam_re, m_ssm_lam_im, m_ssm_log_dt, m_ssm_b_re, m_ssm_b_im, m_ssm_c_re, m_ssm_c_im, m_ssm_d, m_glu_w, m_glu_b, m_pool_w, m_pool_scale, m_ffn_w_gate, m_ffn_w_up, m_ffn_w_down, v_c_ctx, v_w_mod, v_b_mod, v_norm_mix_pre, v_norm_mix_post, v_norm_ffn_pre, v_norm_ffn_post, v_w_in, v_w_out, v_sgu_w, v_sgu_b, v_ssm_lam_re, v_ssm_lam_im, v_ssm_log_dt, v_ssm_b_re, v_ssm_b_im, v_ssm_c_re, v_ssm_c_im, v_ssm_d, v_glu_w, v_glu_b, v_pool_w, v_pool_scale, v_ffn_w_gate, v_ffn_w_up, v_ffn_w_down):
    wts = dict(c_ctx=c_ctx, w_mod=w_mod, b_mod=b_mod, norm_mix_pre=norm_mix_pre, norm_mix_post=norm_mix_post,
               norm_ffn_pre=norm_ffn_pre, norm_ffn_post=norm_ffn_post, w_in=w_in, w_out=w_out, sgu_w=sgu_w, sgu_b=sgu_b,
               ssm_lam_re=ssm_lam_re, ssm_lam_im=ssm_lam_im, ssm_log_dt=ssm_log_dt, ssm_b_re=ssm_b_re, ssm_b_im=ssm_b_im,
               ssm_c_re=ssm_c_re, ssm_c_im=ssm_c_im, ssm_d=ssm_d, glu_w=glu_w, glu_b=glu_b, pool_w=pool_w,
               pool_scale=pool_scale, ffn_w_gate=ffn_w_gate, ffn_w_up=ffn_w_up, ffn_w_down=ffn_w_down)
    ms = dict(c_ctx=m_c_ctx, w_mod=m_w_mod, b_mod=m_b_mod, norm_mix_pre=m_norm_mix_pre, norm_mix_post=m_norm_mix_post,
              norm_ffn_pre=m_norm_ffn_pre, norm_ffn_post=m_norm_ffn_post, w_in=m_w_in, w_out=m_w_out, sgu_w=m_sgu_w,
              sgu_b=m_sgu_b, ssm_lam_re=m_ssm_lam_re, ssm_lam_im=m_ssm_lam_im, ssm_log_dt=m_ssm_log_dt,
              ssm_b_re=m_ssm_b_re, ssm_b_im=m_ssm_b_im, ssm_c_re=m_ssm_c_re, ssm_c_im=m_ssm_c_im, ssm_d=m_ssm_d,
              glu_w=m_glu_w, glu_b=m_glu_b, pool_w=m_pool_w, pool_scale=m_pool_scale, ffn_w_gate=m_ffn_w_gate,
              ffn_w_up=m_ffn_w_up, ffn_w_down=m_ffn_w_down)
    vs = dict(c_ctx=v_c_ctx, w_mod=v_w_mod, b_mod=v_b_mod, norm_mix_pre=v_norm_mix_pre, norm_mix_post=v_norm_mix_post,
              norm_ffn_pre=v_norm_ffn_pre, norm_ffn_post=v_norm_ffn_post, w_in=v_w_in, w_out=v_w_out, sgu_w=v_sgu_w,
              sgu_b=v_sgu_b, ssm_lam_re=v_ssm_lam_re, ssm_lam_im=v_ssm_lam_im, ssm_log_dt=v_ssm_log_dt,
              ssm_b_re=v_ssm_b_re, ssm_b_im=v_ssm_b_im, ssm_c_re=v_ssm_c_re, ssm_c_im=v_ssm_c_im, ssm_d=v_ssm_d,
              glu_w=v_glu_w, glu_b=v_glu_b, pool_w=v_pool_w, pool_scale=v_pool_scale, ffn_w_gate=v_ffn_w_gate,
              ffn_w_up=v_ffn_w_up, ffn_w_down=v_ffn_w_down)
    order = list(wts.keys())
    bl = x.shape[0]
    nseq = bl * N_DEV
    me = _my_index()
    chip = me // 2
    ncol = w_mod.shape[2]

    (c_all,) = _spread([(c, ALL7, 8)], "ag_c")
    nrow = nseq + 8
    c_rows = jnp.concatenate([c_all.reshape(nseq, D), c_ctx[None], jnp.zeros((7, D), F32)], axis=0)
    b_cols = lax.dynamic_slice_in_dim(b_mod, chip * ncol, ncol, axis=1)[:, None, :]
    mod_cols = _mod_fwd(c_rows, w_mod, b_cols, "mod_fwd")
    stash = {}

    def mods_local():
        mods = jnp.transpose(stash["mod4"], (1, 2, 0, 3)).reshape(2, nrow, 6 * D)
        return jnp.concatenate([lax.dynamic_slice_in_dim(mods, me * bl, bl, axis=1), mods[:, nseq:nseq + 1],
                                jnp.zeros((2, 8 - bl - 1, 6 * D), F32)], axis=1)

    shards = {}
    for i in range(2):
        for nme, s in zip(BIG_NAMES, [w_in[i].T, w_out[i], glu_w[i], ffn_w_gate[i].T, ffn_w_up[i].T, ffn_w_down[i]]):
            shards[(i, nme)] = s.astype(MXU_DTYPE)
    weights = [{}, {}]
    ffn_names = ("wg_t", "wu_t", "wd")
    w_plan = [{"embed": [(0, "win_t")], "win": [(0, "wout"), (0, "wglu")], "ssm": [(0, "wg_t"), (0, "wu_t")],
               "ffn": [(0, "wd"), (1, "win_t"), (1, "wout"), (1, "wglu")]},
              {"ssm": [(1, "wg_t"), (1, "wu_t")], "ffn": [(1, "wd")]}]

    def w_entry(keys, more=()):
        def fill(wls, gathered):
            for (i, nme), g in zip(keys, gathered):
                wls[i][nme] = g.reshape(-1, g.shape[-1])
            for (nme, _), g in zip(more, gathered[len(keys):]):
                stash[nme] = g
        return _Side([(shards[k2], "gather", CHIPS3, 4) for k2 in keys] + [(a, "gather", CHIPS3, 4) for _, a in more]), fill

    w_sides = [{key: w_entry(keys) for key, keys in plan.items()} for plan in w_plan]
    w_sides[0]["embed"] = w_entry(w_plan[0]["embed"], more=[("mod4", mod_cols)])

    eighths = lambda g: g.reshape(8, g.shape[0] // 8, g.shape[1])
    g_plan = {"ffn_down": [(1, "win_t"), (1, "wg_t")], "dh2": [(1, "wu_t"), (1, "wout"), (1, "wglu")],
              "ssm": [(0, k) for k in BIG_NAMES if k != "win_t"] + [(1, "wd")], "dh": [(0, "win_t")]}
    early_g = g_plan["ffn_down"] + g_plan["dh2"] + g_plan["ssm"] + g_plan["dh"]

    def g_entry(keys):
        return lambda big1, big0: _Side([(eighths((big1 if i == 1 else big0)[k]), "a2a", ALL7, 8) for i, k in keys])

    g_side_fns = {key: g_entry(keys) for key, keys in g_plan.items()}

    params = [{k: wts[k][i] for k in SMALL_NAMES} for i in range(2)]
    loss_part, grad_x, bigs, smalls, dmods, early = _core(x, ctx, loss_target, mods_local, params, weights,
                                                           w_sides, g_side_fns)
    loss = lax.psum(loss_part, ("x", "y", "c"))

    dmod_local = jnp.stack([dmods[i].reshape(8, 6 * D) for i in range(2)])
    (dmod_all,) = _spread([(dmod_local, ALL7, 8)], "ag_dmod")
    dcols = lax.dynamic_slice_in_dim(dmod_all, chip * ncol, ncol, axis=3)
    dlat = jnp.transpose(dcols[:, :, :bl], (1, 0, 2, 3)).reshape(2, nseq, ncol)
    dlat = jnp.concatenate([dlat, jnp.zeros((2, 8, ncol), F32)], axis=1)
    dctx8 = dcols[:, :, bl:bl + 1]
    g_w_mod, gc_part = _mod_bwd(c_rows, w_mod, dlat, dctx8, "mod_bwd")
    (gc4,) = _spread([(gc_part, CHIPS3, 4)], "ag_cctx")
    g_b_mod, g_c_ctx = _bmod_cctx(dmod_all, gc4, c_ctx[None], "bmod_cctx")

    seg, rows = [], 0
    for k in SMALL_NAMES:
        flat = jnp.stack([smalls[i][k] for i in range(2)]).reshape(-1)
        nrows = -(-flat.shape[0] // 1024)
        seg.append(jnp.concatenate([flat, jnp.zeros((nrows * 1024 - flat.shape[0],), F32)]).reshape(nrows, 1024))
        rows += nrows
    seg.append(jnp.zeros(((-rows) % 8, 1024), F32))
    small_rows = jnp.concatenate(seg, axis=0)
    late = _comm([(small_rows.reshape(8, -1, 1024), "a2a", ALL7, 8)], "a2a_grads")
    sums = _sum8(list(early) + list(late), "gsum")
    fin = _spread([(s, (1,), 2) for s in sums[:-1]] + [(sums[-1], ALL7, 8)], "ag_grads")
    big_g = [{}, {}]
    for (i, k), g in zip(early_g, fin[:-1]):
        big_g[i][k] = g.reshape(-1, g.shape[-1])
    small_red = fin[-1].reshape(-1, 1024)

    grads = {}
    off = 0
    for k in SMALL_NAMES:
        shp = wts[k].shape
        nel = int(np.prod(shp))
        nrows = -(-nel // 1024)
        grads[k] = small_red[off:off + nrows].reshape(-1)[:nel].reshape(shp)
        off += nrows
    grads["c_ctx"] = g_c_ctx[0]
    grads["w_mod"] = g_w_mod
    grads["b_mod"] = g_b_mod
    grads["w_in"] = jnp.stack([big_g[i]["win_t"].T for i in range(2)])
    grads["w_out"] = jnp.stack([big_g[i]["wout"] for i in range(2)])
    grads["glu_w"] = jnp.stack([big_g[i]["wglu"] for i in range(2)])
    grads["ffn_w_gate"] = jnp.stack([big_g[i]["wg_t"].T for i in range(2)])
    grads["ffn_w_up"] = jnp.stack([big_g[i]["wu_t"].T for i in range(2)])
    grads["ffn_w_down"] = jnp.stack([big_g[i]["wd"] for i in range(2)])

    deltas, new_m, new_v = {}, {}, {}
    for k in order:
        deltas[k], new_m[k], new_v[k] = _adam(wts[k], grads[k], ms[k], vs[k], "adam_" + k)
    return (loss, grad_x, *[grads[k] for k in order], *[deltas[k] for k in order],
            *[new_m[k] for k in order], *[new_v[k] for k in order])
```

```python
import functools
import math

import numpy as np
import jax
import jax.numpy as jnp
from jax import lax
from jax.experimental import pallas as pl
from jax.experimental.pallas import tpu as pltpu

F32 = jnp.float32
BF16 = jnp.bfloat16
MXU_DTYPE = jnp.bfloat16
MCAT_A, MCAT_C = 2, 3
WOUT_PERM, WOUT_INV = (1, 2, 0, 3), (2, 0, 1, 3)

D = 1024
EPS = 1e-6
TB = 256
CTX = 256
CHUNK = 128
GRID_W = 64
A_W, B_W, C_W = 256, 512, 256
D_IN = 1280
D_FF = 2816
SSM_G, SSM_P, SSM_H = 32, 64, 16
ST = 64
POOL_WINDOWS = (2, 4, 8, 16)
N_DEV = 8
VMEM_LIMIT = 52 * 1024 * 1024
GELU_C = math.sqrt(2.0 / math.pi)

ADAM_LR, ADAM_B1, ADAM_B2, ADAM_EPS, ADAM_WD, ADAM_STEP = 0.001, 0.9, 0.999, 1e-08, 0.01, 10


def _cp(sem=None, vmem=VMEM_LIMIT, **kw):
    return pltpu.CompilerParams(dimension_semantics=sem, vmem_limit_bytes=vmem, **kw)


def _pick(n, cap):
    if n <= cap:
        return n
    best = None
    for t in range(128, cap + 1, 128):
        if n % t == 0:
            best = t
    assert best is not None, (n, cap)
    return best


def _gelu(x):
    return 0.5 * x * (1.0 + jnp.tanh(GELU_C * (x + 0.044715 * x * x * x)))


def _gelu_grad(x):
    t = jnp.tanh(GELU_C * (x + 0.044715 * x * x * x))
    return 0.5 * (1.0 + t) + 0.5 * x * (1.0 - t * t) * GELU_C * (1.0 + 3.0 * 0.044715 * x * x)


def _sigmoid(x):
    return 1.0 / (1.0 + jnp.exp(-x))


def _dot(a, b, dims):
    return lax.dot_general(a, b, (dims, ((), ())), preferred_element_type=F32)


def _nn(a, b):
    return _dot(a, b, ((1,), (0,)))


def _nt(a, b):
    return _dot(a, b, ((1,), (1,)))


def _tn(a, b):
    return _dot(a, b, ((0,), (0,)))


def _mm(pairs, nt, out_dtype, name, tm=512, side=None):
    m = pairs[0][0].shape[0]
    n = pairs[0][1].shape[0] if nt else pairs[0][1].shape[1]
    tn = _pick(n, 1408)
    tm = min(tm, m)
    npairs = len(pairs)
    ni, nj = m // tm, n // tn

    def body(*refs):
        own, srefs = _side_split(refs, 2 * npairs, 1, 0, side)
        o_ref = own[-1]
        i, j = pl.program_id(0), pl.program_id(1)
        _side_start(side, srefs, jnp.logical_and(i == 0, j == 0))
        acc = None
        for t in range(npairs):
            a = own[2 * t][...].astype(MXU_DTYPE)
            b = own[2 * t + 1][...].astype(MXU_DTYPE)
            r = _nt(a, b) if nt else _nn(a, b)
            acc = r if acc is None else acc + r
        o_ref[...] = acc.astype(o_ref.dtype)
        _side_wait(side, srefs, jnp.logical_and(i == ni - 1, j == nj - 1))

    sd = side if side is not None else _Side([])
    in_specs, flat = [], []
    for a, b in pairs:
        k = a.shape[1]
        in_specs.append(pl.BlockSpec((tm, k), lambda i, j: (i, 0)))
        in_specs.append(pl.BlockSpec((tn, k), lambda i, j: (j, 0)) if nt else pl.BlockSpec((k, tn), lambda i, j: (0, j)))
        flat += [a, b]
    outs = pl.pallas_call(
        body, grid=(ni, nj), in_specs=in_specs + sd.in_specs,
        out_specs=[pl.BlockSpec((tm, tn), lambda i, j: (i, j))] + sd.out_specs,
        out_shape=[jax.ShapeDtypeStruct((m, n), out_dtype)] + sd.out_shape,
        scratch_shapes=sd.scratch if side is not None else [],
        compiler_params=_cp(("arbitrary", "arbitrary") if side is not None else ("parallel", "parallel")),
        name=name)(*flat, *sd.arrays)
    return outs[0] if side is None else (outs[0], list(outs[1:]))


def _mm_tn(a, b, out_dtype, name):
    m, k1 = a.shape
    n = b.shape[1]
    t1 = _pick(k1, 1408)
    tn = _pick(n, 1024)
    tm = max(t for t in (512, 1024, 1536) if m % t == 0)
    nsteps = m // tm

    def body(a_ref, b_ref, o_ref, acc_ref):
        t = pl.program_id(2)

        @pl.when(t == 0)
        def _():
            acc_ref[...] = jnp.zeros_like(acc_ref)

        acc_ref[...] += _tn(a_ref[...].astype(MXU_DTYPE), b_ref[...].astype(MXU_DTYPE))

        @pl.when(t == nsteps - 1)
        def _():
            o_ref[...] = acc_ref[...].astype(o_ref.dtype)

    return pl.pallas_call(
        body, grid=(k1 // t1, n // tn, nsteps),
        in_specs=[pl.BlockSpec((tm, t1), lambda i, j, t: (t, i)), pl.BlockSpec((tm, tn), lambda i, j, t: (t, j))],
        out_specs=pl.BlockSpec((t1, tn), lambda i, j, t: (i, j)),
        out_shape=jax.ShapeDtypeStruct((k1, n), out_dtype),
        scratch_shapes=[pltpu.VMEM((t1, tn), F32)],
        compiler_params=_cp(("parallel", "parallel", "arbitrary")), name=name)(a, b)


class _Layout:
    def __init__(self, bl, lat):
        self.bl, self.lat = bl, lat
        self.nlb = lat // TB
        self.nr = 1 + self.nlb
        self.nctx = bl
        self.nb = self.nr * bl
        self.nt = self.nb * TB
        self.ctx_row = bl

    def mod_tiles(self, mods):
        rows = np.array([[self.ctx_row if r == 0 else b for b in range(self.bl)] for r in range(self.nr)], np.int32)
        t = mods[rows].reshape(self.nr, self.bl, 6, D)
        return jnp.transpose(t, (0, 2, 1, 3)).reshape(self.nr * 6, self.bl, 1, D)


ST_FWD, ST_BWD = 4, 2


def _tok_spec(lay, st):
    nc = lay.bl // st
    return pl.BlockSpec((st * TB, D), lambda c, r: (r * nc + c, 0))


def _vec_spec():
    return pl.BlockSpec((1, D), lambda c, r: (0, 0))


def _mod_spec(st, k):
    return pl.BlockSpec((1, st, 1, D), lambda c, r: (r * 6 + k, c, 0, 0))


def _x_spec(lay, st):
    return pl.BlockSpec((st, 1, TB, D), lambda c, r: (c, jnp.maximum(r - 1, 0), 0, 0))


def _rows3(ref_or_val, st):
    return ref_or_val.reshape(st, TB, D)


def _acc_rows(acc_ref, val3, st, ctx_row):
    c, r = pl.program_id(0), pl.program_id(1)
    s = jnp.sum(val3, axis=1, keepdims=True)

    @pl.when(r == 0)
    def _():
        acc_ref[ctx_row:ctx_row + 1] += jnp.sum(s, axis=0, keepdims=True)

    @pl.when(r > 0)
    def _():
        acc_ref[pl.ds(c * st, st)] += s


def _first_step():
    return jnp.logical_and(pl.program_id(0) == 0, pl.program_id(1) == 0)


def _embed(lay, x, ctx, pe, side=None):
    st = ST_FWD
    bl, nlb = lay.bl, lay.nlb
    nc = bl // st

    def body(*refs):
        (x_ref, c_ref, pe_ref, o_ref), srefs = _side_split(refs, 3, 1, 0, side)
        c, r = pl.program_id(0), pl.program_id(1)
        _side_start(side, srefs, jnp.logical_and(c == 0, r == 0))

        @pl.when(r == 0)
        def _():
            o_ref[...] = c_ref[...].reshape(st * TB, D)

        @pl.when(r > 0)
        def _():
            o_ref[...] = (x_ref[...].reshape(st, TB, D) + pe_ref[...]).reshape(st * TB, D)

        _side_wait(side, srefs, jnp.logical_and(c == nc - 1, r == lay.nr - 1))

    sd = side if side is not None else _Side([])
    outs = pl.pallas_call(
        body, grid=(nc, lay.nr),
        in_specs=[_x_spec(lay, st), pl.BlockSpec((st, CTX, D), lambda c, r: (c, 0, 0)),
                  pl.BlockSpec((1, TB, D), lambda c, r: (jnp.maximum(r - 1, 0), 0, 0))] + sd.in_specs,
        out_specs=[_tok_spec(lay, st)] + sd.out_specs,
        out_shape=[jax.ShapeDtypeStruct((lay.nt, D), F32)] + sd.out_shape,
        scratch_shapes=sd.scratch if side is not None else [],
        compiler_params=_cp(("arbitrary", "arbitrary") if side is not None else ("parallel", "parallel")),
        name="embed")(x.reshape(bl, nlb, TB, D), ctx, pe.reshape(nlb, TB, D), *sd.arrays)
    return outs[0], list(outs[1:])


def _normmod_fwd(lay, x, gain, modt, ksh, ksc, name):
    st = ST_FWD

    def body(x_ref, g_ref, sh_ref, sc_ref, o_ref):
        xv = _rows3(x_ref[...], st)
        r = lax.rsqrt(jnp.mean(xv * xv, axis=-1, keepdims=True) + EPS)
        o_ref[...] = ((xv * r * g_ref[...]) * (1.0 + sc_ref[0]) + sh_ref[0]).reshape(st * TB, D).astype(o_ref.dtype)

    return pl.pallas_call(
        body, grid=(lay.bl // st, lay.nr),
        in_specs=[_tok_spec(lay, st), _vec_spec(), _mod_spec(st, ksh), _mod_spec(st, ksc)],
        out_specs=_tok_spec(lay, st), out_shape=jax.ShapeDtypeStruct((lay.nt, D), MXU_DTYPE),
        compiler_params=_cp(("parallel", "parallel")), name=name)(x, gain, modt, modt)


def _acc_out():
    return pl.BlockSpec((8, 1, D), lambda c, r: (0, 0, 0)), jax.ShapeDtypeStruct((8, 1, D), F32)


def _normmod_bwd(lay, x, dh, dx_in, gain, modt, ksc, name, latent_only=False):
    st = ST_BWD
    acc_spec, acc_shape = _acc_out()
    if latent_only:
        dx_spec, dx_shape = _x_spec(lay, st), jax.ShapeDtypeStruct((lay.bl, lay.nlb, TB, D), F32)
    else:
        dx_spec, dx_shape = _tok_spec(lay, st), jax.ShapeDtypeStruct((lay.nt, D), F32)

    def body(x_ref, dh_ref, dxi_ref, g_ref, sc_ref, dx_ref, dsh_ref, dsc_ref, dg_ref):
        xv = _rows3(x_ref[...], st)
        dhv = _rows3(dh_ref[...].astype(F32), st)
        g = g_ref[...]
        sc1 = 1.0 + sc_ref[0]
        r = lax.rsqrt(jnp.mean(xv * xv, axis=-1, keepdims=True) + EPS)
        xh = xv * r
        dxh = dhv * (g * sc1)
        dx = _rows3(dxi_ref[...], st) + r * (dxh - xh * jnp.mean(dxh * xh, axis=-1, keepdims=True))
        dx_ref[...] = dx.reshape(dx_ref.shape)

        @pl.when(_first_step())
        def _():
            dsh_ref[...] = jnp.zeros_like(dsh_ref)
            dsc_ref[...] = jnp.zeros_like(dsc_ref)
            dg_ref[...] = jnp.zeros_like(dg_ref)

        _acc_rows(dsh_ref, dhv, st, lay.ctx_row)
        _acc_rows(dsc_ref, dhv * (xh * g), st, lay.ctx_row)
        dg_ref[...] += jnp.sum((dhv * sc1 * xh).reshape(st * TB, D), axis=0, keepdims=True)

    return pl.pallas_call(
        body, grid=(lay.bl // st, lay.nr),
        in_specs=[_tok_spec(lay, st), _tok_spec(lay, st), _tok_spec(lay, st), _vec_spec(), _mod_spec(st, ksc)],
        out_specs=[dx_spec, acc_spec, acc_spec, _vec_spec()],
        out_shape=[dx_shape, acc_shape, acc_shape, jax.ShapeDtypeStruct((1, D), F32)],
        compiler_params=_cp(("arbitrary", "arbitrary")), name=name)(x, dh, dx_in, gain, modt)


def _resnorm_fwd(lay, x, m, gain, modt, kgate, name):
    st = ST_FWD

    def body(x_ref, m_ref, g_ref, gate_ref, o_ref):
        mv = _rows3(m_ref[...].astype(F32), st)
        r = lax.rsqrt(jnp.mean(mv * mv, axis=-1, keepdims=True) + EPS)
        o_ref[...] = x_ref[...] + (gate_ref[0] * (mv * r * g_ref[...])).reshape(st * TB, D)

    return pl.pallas_call(
        body, grid=(lay.bl // st, lay.nr),
        in_specs=[_tok_spec(lay, st), _tok_spec(lay, st), _vec_spec(), _mod_spec(st, kgate)],
        out_specs=_tok_spec(lay, st), out_shape=jax.ShapeDtypeStruct((lay.nt, D), F32),
        compiler_params=_cp(("parallel", "parallel")), name=name)(x, m, gain, modt)


def _resnorm_bwd(lay, dxn, m, gain, modt, kgate, name):
    st = ST_BWD
    acc_spec, acc_shape = _acc_out()

    def body(d_ref, m_ref, g_ref, gate_ref, dm_ref, dgate_ref, dg_ref):
        dv = _rows3(d_ref[...], st)
        mv = _rows3(m_ref[...].astype(F32), st)
        g = g_ref[...]
        r = lax.rsqrt(jnp.mean(mv * mv, axis=-1, keepdims=True) + EPS)
        xh = mv * r
        dy = dv * gate_ref[0]
        dxh = dy * g
        dm = r * (dxh - xh * jnp.mean(dxh * xh, axis=-1, keepdims=True))
        dm_ref[...] = dm.reshape(st * TB, D).astype(dm_ref.dtype)

        @pl.when(_first_step())
        def _():
            dgate_ref[...] = jnp.zeros_like(dgate_ref)
            dg_ref[...] = jnp.zeros_like(dg_ref)

        _acc_rows(dgate_ref, dv * (xh * g), st, lay.ctx_row)
        dg_ref[...] += jnp.sum((dy * xh).reshape(st * TB, D), axis=0, keepdims=True)

    return pl.pallas_call(
        body, grid=(lay.bl // st, lay.nr),
        in_specs=[_tok_spec(lay, st), _tok_spec(lay, st), _vec_spec(), _mod_spec(st, kgate)],
        out_specs=[_tok_spec(lay, st), acc_spec, _vec_spec()],
        out_shape=[jax.ShapeDtypeStruct((lay.nt, D), MXU_DTYPE), acc_shape, jax.ShapeDtypeStruct((1, D), F32)],
        compiler_params=_cp(("arbitrary", "arbitrary")), name=name)(dxn, m, gain, modt)


def _rms(v):
    return lax.rsqrt(jnp.mean(v * v, axis=-1, keepdims=True) + EPS)


def _resnorm_normmod_fwd(lay, x, m, gpost, gpre, modt, kgate, ksh, ksc, name):
    st = ST_BWD

    def body(x_ref, m_ref, gp_ref, gq_ref, gate_ref, sh_ref, sc_ref, x1_ref, h_ref):
        mv = _rows3(m_ref[...].astype(F32), st)
        x1 = _rows3(x_ref[...], st) + gate_ref[0] * (mv * _rms(mv) * gp_ref[...])
        x1_ref[...] = x1.reshape(st * TB, D)
        h = (x1 * _rms(x1) * gq_ref[...]) * (1.0 + sc_ref[0]) + sh_ref[0]
        h_ref[...] = h.reshape(st * TB, D).astype(h_ref.dtype)

    tok = _tok_spec(lay, st)
    return pl.pallas_call(
        body, grid=(lay.bl // st, lay.nr),
        in_specs=[tok, tok, _vec_spec(), _vec_spec(), _mod_spec(st, kgate), _mod_spec(st, ksh), _mod_spec(st, ksc)],
        out_specs=[tok, tok],
        out_shape=[jax.ShapeDtypeStruct((lay.nt, D), F32), jax.ShapeDtypeStruct((lay.nt, D), MXU_DTYPE)],
        compiler_params=_cp(("parallel", "parallel")), name=name)(x, m, gpost, gpre, modt, modt, modt)


def _normmod_resnorm_bwd(lay, x1, dh, dx_in, gpre, m, gpost, modt, ksc, kgate, name):
    st = ST_BWD
    acc_spec, acc_shape = _acc_out()

    def body(x_ref, dh_ref, dxi_ref, gq_ref, sc_ref, m_ref, gp_ref, gate_ref,
             dx_ref, dm_ref, dsh_ref, dsc_ref, dgq_ref, dgate_ref, dgp_ref):
        xv = _rows3(x_ref[...], st)
        dhv = _rows3(dh_ref[...].astype(F32), st)
        gq = gq_ref[...]
        sc1 = 1.0 + sc_ref[0]
        r = _rms(xv)
        xh = xv * r
        dxh = dhv * (gq * sc1)
        dx1 = _rows3(dxi_ref[...], st) + r * (dxh - xh * jnp.mean(dxh * xh, axis=-1, keepdims=True))
        dx_ref[...] = dx1.reshape(st * TB, D)
        mv = _rows3(m_ref[...].astype(F32), st)
        gp = gp_ref[...]
        rm = _rms(mv)
        mh = mv * rm
        dy = dx1 * gate_ref[0]
        dmh = dy * gp
        dm = rm * (dmh - mh * jnp.mean(dmh * mh, axis=-1, keepdims=True))
        dm_ref[...] = dm.reshape(st * TB, D).astype(dm_ref.dtype)

        @pl.when(_first_step())
        def _():
            for ref in (dsh_ref, dsc_ref, dgq_ref, dgate_ref, dgp_ref):
                ref[...] = jnp.zeros_like(ref)

        _acc_rows(dsh_ref, dhv, st, lay.ctx_row)
        _acc_rows(dsc_ref, dhv * (xh * gq), st, lay.ctx_row)
        dgq_ref[...] += jnp.sum((dhv * sc1 * xh).reshape(st * TB, D), axis=0, keepdims=True)
        _acc_rows(dgate_ref, dx1 * (mh * gp), st, lay.ctx_row)
        dgp_ref[...] += jnp.sum((dy * mh).reshape(st * TB, D), axis=0, keepdims=True)

    tok = _tok_spec(lay, st)
    vsh = jax.ShapeDtypeStruct((1, D), F32)
    return pl.pallas_call(
        body, grid=(lay.bl // st, lay.nr),
        in_specs=[tok, tok, tok, _vec_spec(), _mod_spec(st, ksc), tok, _vec_spec(), _mod_spec(st, kgate)],
        out_specs=[tok, tok, acc_spec, acc_spec, _vec_spec(), acc_spec, _vec_spec()],
        out_shape=[jax.ShapeDtypeStruct((lay.nt, D), F32), jax.ShapeDtypeStruct((lay.nt, D), MXU_DTYPE),
                   acc_shape, acc_shape, vsh, acc_shape, vsh],
        compiler_params=_cp(("arbitrary", "arbitrary")), name=name)(x1, dh, dx_in, gpre, modt, m, gpost, modt)


def _resnorm_loss(lay, x, f, gain, modt, kgate, tgt):
    st = ST_BWD

    def body(x_ref, f_ref, g_ref, gate_ref, t_ref, dx_ref, l_ref):
        r = pl.program_id(1)

        @pl.when(_first_step())
        def _():
            l_ref[...] = jnp.zeros_like(l_ref)

        @pl.when(r == 0)
        def _():
            dx_ref[...] = jnp.zeros_like(dx_ref)

        @pl.when(r > 0)
        def _():
            fv = _rows3(f_ref[...].astype(F32), st)
            y = _rows3(x_ref[...], st) + gate_ref[0] * (fv * _rms(fv) * g_ref[...])
            e = y - t_ref[...].reshape(st, TB, D)
            dx_ref[...] = (e * (1.0 / D)).reshape(st * TB, D)
            l_ref[...] += jnp.sum(e * e) * (0.5 / D)

    tok = _tok_spec(lay, st)
    return pl.pallas_call(
        body, grid=(lay.bl // st, lay.nr),
        in_specs=[tok, tok, _vec_spec(), _mod_spec(st, kgate), _x_spec(lay, st)],
        out_specs=[tok, pl.BlockSpec((8, 128), lambda c, r: (0, 0))],
        out_shape=[jax.ShapeDtypeStruct((lay.nt, D), F32), jax.ShapeDtypeStruct((8, 128), F32)],
        compiler_params=_cp(("arbitrary", "arbitrary")), name="loss")(
            x, f, gain, modt, tgt.reshape(lay.bl, lay.nlb, TB, D))


FF_TN = D_FF // 2
FF_CHUNKS = ((0, 512), (512, 512), (1024, 384))


def _ffn_up(h, wgt, wut, name, side=None):
    m = h.shape[0]
    tm, tn = min(512, m), FF_TN
    ni, nj = m // tm, D_FF // tn

    def body(*refs):
        (h_ref, wg_ref, wu_ref, g_ref, u_ref, a_ref), srefs = _side_split(refs, 3, 3, 0, side)
        j, i = pl.program_id(0), pl.program_id(1)
        _side_start(side, srefs, jnp.logical_and(i == 0, j == 0))
        hv = h_ref[...]
        for c0, cw in FF_CHUNKS:
            g = _nt(hv, wg_ref[c0:c0 + cw, :])
            u = _nt(hv, wu_ref[c0:c0 + cw, :])
            g_ref[:, c0:c0 + cw] = g.astype(g_ref.dtype)
            u_ref[:, c0:c0 + cw] = u.astype(u_ref.dtype)
            a_ref[:, c0:c0 + cw] = (g * _sigmoid(g) * u).astype(a_ref.dtype)
        _side_wait(side, srefs, jnp.logical_and(i == ni - 1, j == nj - 1))

    sd = side if side is not None else _Side([])
    osp = pl.BlockSpec((tm, tn), lambda j, i: (i, j))
    osh = jax.ShapeDtypeStruct((m, D_FF), MXU_DTYPE)
    outs = pl.pallas_call(
        body, grid=(nj, ni),
        in_specs=[pl.BlockSpec((tm, D), lambda j, i: (i, 0)), pl.BlockSpec((tn, D), lambda j, i: (j, 0)),
                  pl.BlockSpec((tn, D), lambda j, i: (j, 0))] + sd.in_specs,
        out_specs=[osp, osp, osp] + sd.out_specs, out_shape=[osh, osh, osh] + sd.out_shape,
        scratch_shapes=sd.scratch if side is not None else [],
        compiler_params=_cp(("arbitrary", "arbitrary") if side is not None else ("parallel", "parallel")),
        name=name)(h, wgt, wut, *sd.arrays)
    return outs[0], outs[1], outs[2], list(outs[3:])


def _ffn_down_bwd(df, wd, g, u, name, side=None):
    m = df.shape[0]
    tm, tn = min(512, m), FF_TN
    ni, nj = m // tm, D_FF // tn

    def body(*refs):
        (df_ref, wd_ref, g_ref, u_ref, dg_ref, du_ref), srefs = _side_split(refs, 4, 2, 0, side)
        j, i = pl.program_id(0), pl.program_id(1)
        _side_start(side, srefs, jnp.logical_and(i == 0, j == 0))
        dfv = df_ref[...]
        for c0, cw in FF_CHUNKS:
            da = _nt(dfv, wd_ref[c0:c0 + cw, :])
            gv = g_ref[:, c0:c0 + cw].astype(F32)
            uv = u_ref[:, c0:c0 + cw].astype(F32)
            s = _sigmoid(gv)
            dg_ref[:, c0:c0 + cw] = (da * uv * (s * (1.0 + gv * (1.0 - s)))).astype(dg_ref.dtype)
            du_ref[:, c0:c0 + cw] = (da * gv * s).astype(du_ref.dtype)
        _side_wait(side, srefs, jnp.logical_and(i == ni - 1, j == nj - 1))

    sd = side if side is not None else _Side([])
    osp = pl.BlockSpec((tm, tn), lambda j, i: (i, j))
    osh = jax.ShapeDtypeStruct((m, D_FF), MXU_DTYPE)
    outs = pl.pallas_call(
        body, grid=(nj, ni),
        in_specs=[pl.BlockSpec((tm, D), lambda j, i: (i, 0)), pl.BlockSpec((tn, D), lambda j, i: (j, 0)), osp, osp]
        + sd.in_specs,
        out_specs=[osp, osp] + sd.out_specs, out_shape=[osh, osh] + sd.out_shape,
        scratch_shapes=sd.scratch if side is not None else [],
        compiler_params=_cp(("arbitrary", "arbitrary") if side is not None else ("parallel", "parallel")),
        name=name)(df, wd, g, u, *sd.arrays)
    return outs[0], outs[1], list(outs[2:])


def _head_masks(shape):
    lane = lax.broadcasted_iota(jnp.int32, shape, 1)
    return [jnp.logical_and(lane >= 64 * h, lane < 64 * h + 64) for h in range(4)]


def _head_mean(x, masks):
    out = jnp.zeros_like(x)
    for mk in masks:
        s = jnp.sum(jnp.where(mk, x, 0.0), axis=-1, keepdims=True) * (1.0 / 64.0)
        out = jnp.where(mk, s, out)
    return out


def _gate_common(z, masks):
    zg = _gelu(z)
    u = zg[:, :A_W]
    v = zg[:, A_W:]
    mu = _head_mean(v, masks)
    vc = v - mu
    rstd = lax.rsqrt(_head_mean(vc * vc, masks) + EPS)
    return u, vc * rstd, rstd


def _gate_s(vn, ws_ref, bias, masks):
    parts = []
    for c in range(TB // CHUNK):
        vc = vn[c * CHUNK:(c + 1) * CHUNK]
        s = bias
        for h in range(4):
            s = s + _nn(ws_ref[h], jnp.where(masks[h][:CHUNK], vc, 0.0).astype(MXU_DTYPE))
        parts.append(s)
    return jnp.concatenate(parts, axis=0)


MT = 4


def _blocks():
    return [pl.ds(s * TB, TB) for s in range(MT)]


def _gate_fwd(lay, z, ws, bias, name):
    def body(z_ref, ws_ref, b_ref, o_ref):
        masks = _head_masks((TB, A_W))
        for sl in _blocks():
            u, vn, _ = _gate_common(z_ref[sl, :].astype(F32), masks)
            o_ref[sl, :] = (u * _gate_s(vn, ws_ref, b_ref[...], masks)).astype(o_ref.dtype)

    return pl.pallas_call(
        body, grid=(lay.nb // MT,),
        in_specs=[pl.BlockSpec((MT * TB, 2 * A_W), lambda j: (j, 0)), pl.BlockSpec((4, CHUNK, CHUNK), lambda j: (0, 0, 0)),
                  pl.BlockSpec((CHUNK, A_W), lambda j: (0, 0))],
        out_specs=pl.BlockSpec((MT * TB, A_W), lambda j: (j, 0)),
        out_shape=jax.ShapeDtypeStruct((lay.nt, A_W), MXU_DTYPE),
        compiler_params=_cp(("parallel",)), name=name)(z, ws, bias)


def _gate_bwd(lay, z, da, ws, wst, bias, name):
    def body(z_ref, da_ref, ws_ref, wst_ref, b_ref, dz_ref, dws_ref, db_ref):
        j = pl.program_id(0)

        @pl.when(j == 0)
        def _():
            dws_ref[...] = jnp.zeros_like(dws_ref)
            db_ref[...] = jnp.zeros_like(db_ref)

        masks = _head_masks((TB, A_W))
        for blk in _blocks():
            zv = z_ref[blk, :].astype(F32)
            u, vn, rstd = _gate_common(zv, masks)
            s = _gate_s(vn, ws_ref, b_ref[...], masks)
            dav = da_ref[blk, :].astype(F32)
            du = dav * s
            ds = dav * u
            dvn_parts = []
            for c in range(TB // CHUNK):
                sl = slice(c * CHUNK, (c + 1) * CHUNK)
                ds_c = ds[sl]
                vn_c = vn[sl].astype(MXU_DTYPE)
                db_ref[...] += ds_c
                ds_b = ds_c.astype(MXU_DTYPE)
                dvn_c = jnp.zeros((CHUNK, A_W), F32)
                for h in range(4):
                    mk = masks[h][:CHUNK]
                    dws_ref[h] += _nt(jnp.where(mk, ds_c, 0.0).astype(MXU_DTYPE), vn_c)
                    dvn_c = dvn_c + jnp.where(mk, _nn(wst_ref[h], ds_b), 0.0)
                dvn_parts.append(dvn_c)
            dvn = jnp.concatenate(dvn_parts, axis=0)
            dv = rstd * (dvn - _head_mean(dvn, masks) - vn * _head_mean(dvn * vn, masks))
            gg = _gelu_grad(zv)
            dz_ref[blk, :A_W] = (du * gg[:, :A_W]).astype(dz_ref.dtype)
            dz_ref[blk, A_W:] = (dv * gg[:, A_W:]).astype(dz_ref.dtype)

    return pl.pallas_call(
        body, grid=(lay.nb // MT,),
        in_specs=[pl.BlockSpec((MT * TB, 2 * A_W), lambda j: (j, 0)), pl.BlockSpec((MT * TB, A_W), lambda j: (j, MCAT_A)),
                  pl.BlockSpec((4, CHUNK, CHUNK), lambda j: (0, 0, 0)), pl.BlockSpec((4, CHUNK, CHUNK), lambda j: (0, 0, 0)),
                  pl.BlockSpec((CHUNK, A_W), lambda j: (0, 0))],
        out_specs=[pl.BlockSpec((MT * TB, 2 * A_W), lambda j: (j, 0)), pl.BlockSpec((4, CHUNK, CHUNK), lambda j: (0, 0, 0)),
                   pl.BlockSpec((CHUNK, A_W), lambda j: (0, 0))],
        out_shape=[jax.ShapeDtypeStruct((lay.nt, 2 * A_W), MXU_DTYPE), jax.ShapeDtypeStruct((4, CHUNK, CHUNK), F32),
                   jax.ShapeDtypeStruct((CHUNK, A_W), F32)],
        compiler_params=_cp(("arbitrary",)), name=name)(z, da, ws, wst, bias)


def _band_constants():
    bands = np.zeros((2, 4, TB, TB), np.float32)
    inv = np.zeros((2, 4, TB, 1), np.float32)
    for kind, n in ((0, GRID_W), (1, TB)):
        for i, w in enumerate(POOL_WINDOWS):
            for t in range(TB):
                base, tt = (t // n) * n, t % n
                lo = min(max(tt - w // 2, 0), n)
                hi = min(max(tt - w // 2 + w, 0), n)
                bands[kind, i, t, base + lo:base + hi] = 1.0
                inv[kind, i, t, 0] = 1.0 / (hi - lo)
    return bands, inv


def _split3(x):
    a = x.astype(MXU_DTYPE)
    r1 = x - a.astype(F32)
    b = r1.astype(MXU_DTYPE)
    c = (r1 - b.astype(F32)).astype(MXU_DTYPE)
    return a, b, c


def _window_apply(band_ref, inv_ref, x, masks, transpose, mxu_exact=False):
    out = jnp.zeros_like(x)
    for i in range(4):
        xi = x * inv_ref[0, i] if transpose else x
        acc = None
        for part in ((xi.astype(MXU_DTYPE),) if mxu_exact else _split3(xi)):
            r = _tn(band_ref[0, i], part) if transpose else _nn(band_ref[0, i], part)
            acc = r if acc is None else acc + r
        if not transpose:
            acc = acc * inv_ref[0, i]
        out = jnp.where(masks[i], acc, out)
    return out


def _pool_specs(lay):
    kind = lambda j: jnp.where(j < lay.nctx // MT, 1, 0)
    return [pl.BlockSpec((1, 4, TB, TB), lambda j: (kind(j), 0, 0, 0)), pl.BlockSpec((1, 4, TB, 1), lambda j: (kind(j), 0, 0, 0))]


def _pool_fwd(lay, z, bands, inv, pw, scale, name):
    def body(p_ref, band_ref, inv_ref, pw_ref, sc_ref, o_ref):
        masks = _head_masks((TB, C_W))
        for blk in _blocks():
            p = p_ref[blk, :].astype(F32)
            diff = _window_apply(band_ref, inv_ref, p, masks, False, mxu_exact=True) - p
            o_ref[blk, :] = (_nn(diff.astype(MXU_DTYPE), pw_ref[...]) * sc_ref[...]).astype(o_ref.dtype)

    return pl.pallas_call(
        body, grid=(lay.nb // MT,),
        in_specs=[pl.BlockSpec((MT * TB, C_W), lambda j: (j, 4))] + _pool_specs(lay)
        + [pl.BlockSpec((C_W, C_W), lambda j: (0, 0)), pl.BlockSpec((1, C_W), lambda j: (0, 0))],
        out_specs=pl.BlockSpec((MT * TB, C_W), lambda j: (j, 0)),
        out_shape=jax.ShapeDtypeStruct((lay.nt, C_W), MXU_DTYPE),
        compiler_params=_cp(("parallel",)), name=name)(z, bands, inv, pw, scale)


def _pool_bwd(lay, z, dc, bands, inv, pw, scale, name):
    def body(p_ref, dc_ref, band_ref, inv_ref, pw_ref, sc_ref, dp_ref, dpw_ref, dsc_ref):
        j = pl.program_id(0)

        @pl.when(j == 0)
        def _():
            dpw_ref[...] = jnp.zeros_like(dpw_ref)
            dsc_ref[...] = jnp.zeros_like(dsc_ref)

        masks = _head_masks((TB, C_W))
        for blk in _blocks():
            p = p_ref[blk, :].astype(F32)
            dcv = dc_ref[blk, :].astype(F32)
            diff = _window_apply(band_ref, inv_ref, p, masks, False, mxu_exact=True) - p
            diff_b = diff.astype(MXU_DTYPE)
            pre = _nn(diff_b, pw_ref[...])
            dsc_ref[...] += jnp.sum(dcv * pre, axis=0, keepdims=True)
            dpre = dcv * sc_ref[...]
            dpre_b = dpre.astype(MXU_DTYPE)
            dpw_ref[...] += _tn(diff_b, dpre_b)
            ddiff = _nt(dpre_b, pw_ref[...])
            dp_ref[blk, :] = (_window_apply(band_ref, inv_ref, ddiff, masks, True) - ddiff).astype(dp_ref.dtype)

    return pl.pallas_call(
        body, grid=(lay.nb // MT,),
        in_specs=[pl.BlockSpec((MT * TB, C_W), lambda j: (j, 4)), pl.BlockSpec((MT * TB, C_W), lambda j: (j, MCAT_C))]
        + _pool_specs(lay)
        + [pl.BlockSpec((C_W, C_W), lambda j: (0, 0)), pl.BlockSpec((1, C_W), lambda j: (0, 0))],
        out_specs=[pl.BlockSpec((MT * TB, C_W), lambda j: (j, 0)), pl.BlockSpec((C_W, C_W), lambda j: (0, 0)),
                   pl.BlockSpec((1, C_W), lambda j: (0, 0))],
        out_shape=[jax.ShapeDtypeStruct((lay.nt, C_W), MXU_DTYPE), jax.ShapeDtypeStruct((C_W, C_W), F32),
                   jax.ShapeDtypeStruct((1, C_W), F32)],
        compiler_params=_cp(("arbitrary",)), name=name)(z, dc, bands, inv, pw, scale)


def _disc_math(lr, li, ldt, br, bi):
    dt = jnp.exp(ldt)
    e = jnp.exp(lr * dt)
    ar = e * jnp.cos(li * dt)
    ai = e * jnp.sin(li * dt)
    nr, ni = ar - 1.0, ai
    den = lr * lr + li * li
    qr = (nr * lr + ni * li) / den
    qi = (ni * lr - nr * li) / den
    return ar, ai, qr * br - qi * bi, qr * bi + qi * br


def _disc_fwd(lrx, lix, ldtx, brt, bit, name):
    def body(lr_ref, li_ref, ldt_ref, br_ref, bi_ref, ar_ref, ai_ref, obr_ref, obi_ref):
        ar, ai, obr, obi = _disc_math(lr_ref[...], li_ref[...], ldt_ref[...], br_ref[...], bi_ref[...])
        ar_ref[...] = ar
        ai_ref[...] = ai
        obr_ref[...] = obr
        obi_ref[...] = obi

    sh = jax.ShapeDtypeStruct(lrx.shape, F32)
    return pl.pallas_call(body, out_shape=[sh, sh, sh, sh], name=name)(lrx, lix, ldtx, brt, bit)


def _disc_bwd(lrx, lix, ldtx, brt, bit, dar, dai, dbr, dbi, name):
    nrow = lrx.shape[0] // SSM_H

    def body(lr_ref, li_ref, ldt_ref, br_ref, bi_ref, dar_ref, dai_ref, dbr_ref, dbi_ref,
             glr_ref, gli_ref, gdt_ref, gbr_ref, gbi_ref):
        _, vjp = jax.vjp(_disc_math, lr_ref[...], li_ref[...], ldt_ref[...], br_ref[...], bi_ref[...])
        glr, gli, gdt, gbr, gbi = vjp((dar_ref[...], dai_ref[...], dbr_ref[...], dbi_ref[...]))
        glr_ref[...] = jnp.sum(glr.reshape(nrow, SSM_H, SSM_P), axis=1)
        gli_ref[...] = jnp.sum(gli.reshape(nrow, SSM_H, SSM_P), axis=1)
        gdt_ref[...] = jnp.sum(jnp.sum(gdt.reshape(nrow, SSM_H, SSM_P), axis=1), axis=-1, keepdims=True)
        gbr_ref[...] = gbr
        gbi_ref[...] = gbi

    small = jax.ShapeDtypeStruct((nrow, SSM_P), F32)
    big = jax.ShapeDtypeStruct(lrx.shape, F32)
    return pl.pallas_call(body, out_shape=[small, small, jax.ShapeDtypeStruct((nrow, 1), F32), big, big],
                          name=name)(lrx, lix, ldtx, brt, bit, dar, dai, dbr, dbi)


HS = 1024
GQ, QC, QS = 8, 128, 512
LC = QS
SCAN_UNROLL = ST


def _scan_steps(step, carry):
    if SCAN_UNROLL >= ST:
        for s in range(ST):
            carry = step(s, carry)
        return carry

    def body(i, c):
        for j in range(SCAN_UNROLL):
            c = step(i * SCAN_UNROLL + j, c)
        return c

    return lax.fori_loop(0, ST // SCAN_UNROLL, body, carry)


def _tile_row(s):
    return s * 8 if isinstance(s, int) else pl.multiple_of(s * 8, 8)


def _dir_cat(x, d0, qq):
    xq = x[:, QC * qq:QC * qq + QC]
    zero = jnp.zeros_like(xq)
    return jnp.concatenate([jnp.where(d0, xq, zero), jnp.where(d0, zero, xq)], axis=1)


def _dir_pick(x, d0):
    return jnp.where(d0, x[:, :QC], x[:, QC:])


def _d0_rows(n):
    row = lax.broadcasted_iota(jnp.int32, (n, 1), 0)
    return jnp.bitwise_and(row, 4) == 0


def _scan_perm(bl):
    n = 2 * bl * ST
    p = np.zeros((n, n), np.float32)
    for s in range(ST):
        for d in range(2):
            for b in range(bl):
                t = s if d == 0 else ST - 1 - s
                p[s * 2 * bl + d * bl + b, d * bl * ST + b * ST + t] = 1.0
    return p


def _scan_maps(lay):
    spc = TB // ST
    nlc = lay.nlb * spc

    def fwd(k):
        return k // spc, k % spc

    def rev(k):
        cpos = nlc - 1 - jnp.maximum(k - spc, 0)
        return jnp.where(k < spc, 0, 1 + cpos // spc), jnp.where(k < spc, spc - 1 - k, cpos % spc)

    return fwd, rev


def _pack_rows(f_ref, r_ref, p_ref, rc):
    st = jnp.concatenate([f_ref[0].reshape(rc // 2, 256), r_ref[0].reshape(rc // 2, 256)], axis=0).astype(MXU_DTYPE)
    return _nn(p_ref[...], st).astype(MXU_DTYPE)


def _side_split(refs, n_in, n_out, n_scr, side):
    ns = side.n if side is not None else 0
    ins, sin = refs[:n_in], refs[n_in:n_in + ns]
    o0 = n_in + ns
    outs, sout = refs[o0:o0 + n_out], refs[o0 + n_out:o0 + n_out + ns]
    s0 = o0 + n_out + ns
    return ins + outs + refs[s0:s0 + n_scr], (sin, sout, refs[s0 + n_scr:])


def _side_start(side, srefs, first):
    if side is not None:
        @pl.when(first)
        def _():
            side.start(*srefs)


def _side_wait(side, srefs, last):
    if side is not None:
        @pl.when(last)
        def _():
            side.wait(*srefs)


def _ssm_fwd(lay, z, perm, bh, ch, ar8, ai8, name, side=None):
    bl = lay.bl
    rc = ST * 2 * bl
    nch = lay.nr * (TB // ST)
    fwd, rev = _scan_maps(lay)
    z4 = z.reshape(lay.nr, bl, TB, z.shape[1])

    def body(*refs):
        own, srefs = _side_split(refs, 7, 3, 2, side)
        uf_ref, ur_ref, p_ref, bh_ref, ch_ref, ar_ref, ai_ref, yf_ref, yr_ref, hst_ref, hs, hc = own
        f, k = pl.program_id(0), pl.program_id(1)
        _side_start(side, srefs, jnp.logical_and(f == 0, k == 0))

        @pl.when(k == 0)
        def _():
            hc[...] = jnp.zeros_like(hc)

        hst_ref[0] = hc[...]
        d0 = _d0_rows(rc)
        uv = _pack_rows(uf_ref, ur_ref, p_ref, rc)
        for q in range(2):
            cr, ci = 2 * QS * q, 2 * QS * q + QS
            hs[:, cr:cr + 2 * QS] = _nn(_dir_cat(uv, d0, q), bh_ref[q])
            ar = ar_ref[:, QS * q:QS * q + QS]
            ai = ai_ref[:, QS * q:QS * q + QS]

            def step(s, carry, cr=cr, ci=ci, ar=ar, ai=ai):
                hr, hi = carry
                base = _tile_row(s)
                nr = ar * hr - ai * hi + hs[pl.ds(base, 8), cr:cr + LC]
                ni = ar * hi + ai * hr + hs[pl.ds(base, 8), ci:ci + LC]
                hs[pl.ds(base, 8), cr:cr + LC] = nr
                hs[pl.ds(base, 8), ci:ci + LC] = ni
                return nr, ni

            hr, hi = _scan_steps(step, (hc[:, cr:cr + LC], hc[:, ci:ci + LC]))
            hc[:, cr:cr + LC] = hr
            hc[:, ci:ci + LC] = hi
        yi = jnp.concatenate(
            [_dir_pick(_nn(hs[:, 2 * QS * q:2 * QS * (q + 1)].astype(MXU_DTYPE), ch_ref[q]), d0) for q in range(2)], axis=1)
        yd = _tn(p_ref[...], yi.astype(MXU_DTYPE))
        yf_ref[0] = yd[:rc // 2].reshape(bl, ST, 256).astype(yf_ref.dtype)
        yr_ref[0] = yd[rc // 2:].reshape(bl, ST, 256).astype(yr_ref.dtype)
        _side_wait(side, srefs, jnp.logical_and(f == 1, k == nch - 1))

    sd = side if side is not None else _Side([])
    blk = (1, bl, ST, 256)
    ysh = jax.ShapeDtypeStruct((lay.nr, bl, TB, B_W), MXU_DTYPE)
    outs = pl.pallas_call(
        body, grid=(2, nch),
        in_specs=[pl.BlockSpec(blk, lambda f, k: (fwd(k)[0], 0, fwd(k)[1], 2 + f)),
                  pl.BlockSpec(blk, lambda f, k: (rev(k)[0], 0, rev(k)[1], 2 + f)),
                  pl.BlockSpec((rc, rc), lambda f, k: (0, 0)),
                  pl.BlockSpec((2, 2 * QC, 2 * QS), lambda f, k: (f, 0, 0)),
                  pl.BlockSpec((2, 2 * QS, 2 * QC), lambda f, k: (f, 0, 0)),
                  pl.BlockSpec((8, HS), lambda f, k: (0, f)), pl.BlockSpec((8, HS), lambda f, k: (0, f))] + sd.in_specs,
        out_specs=[pl.BlockSpec(blk, lambda f, k: (fwd(k)[0], 0, fwd(k)[1], f)),
                   pl.BlockSpec(blk, lambda f, k: (rev(k)[0], 0, rev(k)[1], f)),
                   pl.BlockSpec((1, 8, 2 * HS), lambda f, k: (k, 0, f))] + sd.out_specs,
        out_shape=[ysh, ysh, jax.ShapeDtypeStruct((nch, 8, 4 * HS), F32)] + sd.out_shape,
        scratch_shapes=[pltpu.VMEM((rc, 2 * HS), F32), pltpu.VMEM((8, 2 * HS), F32)] + (sd.scratch if side is not None else []),
        compiler_params=_cp(("arbitrary", "arbitrary")), name=name)(z4, z4, perm, bh, ch, ar8, ai8, *sd.arrays)
    yf, yr, hst = outs[:3]
    return yf.reshape(lay.nt, B_W), yr.reshape(lay.nt, B_W), hst, list(outs[3:])


def _ssm_bwd(lay, z, dy, perm, hst, bh, ch, ar8, ai8, name, side=None):
    bl = lay.bl
    rc = ST * 2 * bl
    nch = lay.nr * (TB // ST)
    fwd, rev = _scan_maps(lay)
    z4 = z.reshape(lay.nr, bl, TB, z.shape[1])
    dy4 = dy.reshape(lay.nr, bl, TB, B_W)

    def body(*refs):
        own, srefs = _side_split(refs, 10, 6, 5, side)
        (uf_ref, ur_ref, dyf_ref, dyr_ref, p_ref, hst_ref, bh_ref, ch_ref, ar_ref, ai_ref,
         duf_ref, dur_ref, dbh_ref, dch_ref, dar_ref, dai_ref, hs, es, ec, accr, acci) = own
        f, k = pl.program_id(0), pl.program_id(1)
        _side_start(side, srefs, jnp.logical_and(f == 0, k == 0))

        @pl.when(k == 0)
        def _():
            ec[...] = jnp.zeros_like(ec)
            accr[...] = jnp.zeros_like(accr)
            acci[...] = jnp.zeros_like(acci)
            dbh_ref[...] = jnp.zeros_like(dbh_ref)
            dch_ref[...] = jnp.zeros_like(dch_ref)

        d0 = _d0_rows(rc)
        uv = _pack_rows(uf_ref, ur_ref, p_ref, rc)
        dyv = _pack_rows(dyf_ref, dyr_ref, p_ref, rc)

        hs[0:8, :] = hst_ref[0]
        ucat, dycat = [], []
        for q in range(2):
            cr, ci = 2 * QS * q, 2 * QS * q + QS
            ucat.append(_dir_cat(uv, d0, q))
            dycat.append(_dir_cat(dyv, d0, q))
            hs[8:, cr:cr + 2 * QS] = _nn(ucat[q], bh_ref[q])
            ar = ar_ref[:, QS * q:QS * q + QS]
            ai = ai_ref[:, QS * q:QS * q + QS]

            def step(s, carry, cr=cr, ci=ci, ar=ar, ai=ai):
                hr, hi = carry
                base = _tile_row(s + 1)
                nr = ar * hr - ai * hi + hs[pl.ds(base, 8), cr:cr + LC]
                ni = ar * hi + ai * hr + hs[pl.ds(base, 8), ci:ci + LC]
                hs[pl.ds(base, 8), cr:cr + LC] = nr
                hs[pl.ds(base, 8), ci:ci + LC] = ni
                return nr, ni

            _scan_steps(step, (hs[0:8, cr:cr + LC], hs[0:8, ci:ci + LC]))
            dch_ref[q] += _tn(hs[8:, cr:cr + 2 * QS].astype(MXU_DTYPE), dycat[q])
            es[:, cr:cr + 2 * QS] = _nt(dycat[q], ch_ref[q])

        dui = []
        for q in range(2):
            cr, ci = 2 * QS * q, 2 * QS * q + QS
            ar = ar_ref[:, QS * q:QS * q + QS]
            ai = ai_ref[:, QS * q:QS * q + QS]

            def bstep(i, carry, cr=cr, ci=ci, ar=ar, ai=ai):
                er, ei, sr, si = carry
                base = _tile_row(ST - 1 - i)
                ner = es[pl.ds(base, 8), cr:cr + LC] + ar * er + ai * ei
                nei = es[pl.ds(base, 8), ci:ci + LC] - ai * er + ar * ei
                es[pl.ds(base, 8), cr:cr + LC] = ner
                es[pl.ds(base, 8), ci:ci + LC] = nei
                hpr = hs[pl.ds(base, 8), cr:cr + LC]
                hpi = hs[pl.ds(base, 8), ci:ci + LC]
                return ner, nei, sr + ner * hpr + nei * hpi, si - ner * hpi + nei * hpr

            lo = QS * q
            er, ei, sr, si = _scan_steps(
                bstep, (ec[:, cr:cr + LC], ec[:, ci:ci + LC], accr[:, lo:lo + LC], acci[:, lo:lo + LC]))
            ec[:, cr:cr + LC] = er
            ec[:, ci:ci + LC] = ei
            accr[:, lo:lo + LC] = sr
            acci[:, lo:lo + LC] = si
            eb = es[:, cr:cr + 2 * QS].astype(MXU_DTYPE)
            dui.append(_dir_pick(_nt(eb, bh_ref[q]), d0))
            dbh_ref[q] += _tn(ucat[q], eb)

        dud = _tn(p_ref[...], jnp.concatenate(dui, axis=1).astype(MXU_DTYPE))
        duf_ref[0] = dud[:rc // 2].reshape(bl, ST, 256).astype(duf_ref.dtype)
        dur_ref[0] = dud[rc // 2:].reshape(bl, ST, 256).astype(dur_ref.dtype)

        @pl.when(k == nch - 1)
        def _():
            for d in range(2):
                dar_ref[d:d + 1, :] = jnp.sum(accr[4 * d:4 * d + 4, :], axis=0, keepdims=True)
                dai_ref[d:d + 1, :] = jnp.sum(acci[4 * d:4 * d + 4, :], axis=0, keepdims=True)

        _side_wait(side, srefs, jnp.logical_and(f == 1, k == nch - 1))

    sd = side if side is not None else _Side([])
    last = lambda k: nch - 1 - k
    blk = (1, bl, ST, 256)
    fspec = lambda c0: pl.BlockSpec(blk, lambda f, k: (fwd(last(k))[0], 0, fwd(last(k))[1], c0 + f))
    rspec = lambda c0: pl.BlockSpec(blk, lambda f, k: (rev(last(k))[0], 0, rev(last(k))[1], c0 + f))
    dush = jax.ShapeDtypeStruct((lay.nr, bl, TB, B_W), MXU_DTYPE)
    outs = pl.pallas_call(
        body, grid=(2, nch),
        in_specs=[fspec(2), rspec(2), fspec(0), rspec(0),
                  pl.BlockSpec((rc, rc), lambda f, k: (0, 0)),
                  pl.BlockSpec((1, 8, 2 * HS), lambda f, k: (last(k), 0, f)),
                  pl.BlockSpec((2, 2 * QC, 2 * QS), lambda f, k: (f, 0, 0)),
                  pl.BlockSpec((2, 2 * QS, 2 * QC), lambda f, k: (f, 0, 0)),
                  pl.BlockSpec((8, HS), lambda f, k: (0, f)), pl.BlockSpec((8, HS), lambda f, k: (0, f))] + sd.in_specs,
        out_specs=[fspec(0), rspec(0),
                   pl.BlockSpec((2, 2 * QC, 2 * QS), lambda f, k: (f, 0, 0)),
                   pl.BlockSpec((2, 2 * QS, 2 * QC), lambda f, k: (f, 0, 0)),
                   pl.BlockSpec((2, HS), lambda f, k: (0, f)), pl.BlockSpec((2, HS), lambda f, k: (0, f))] + sd.out_specs,
        out_shape=[dush, dush, jax.ShapeDtypeStruct((4, 2 * QC, 2 * QS), F32),
                   jax.ShapeDtypeStruct((4, 2 * QS, 2 * QC), F32), jax.ShapeDtypeStruct((2, 2 * HS), F32),
                   jax.ShapeDtypeStruct((2, 2 * HS), F32)] + sd.out_shape,
        scratch_shapes=[pltpu.VMEM((rc + 8, 2 * HS), F32), pltpu.VMEM((rc, 2 * HS), F32), pltpu.VMEM((8, 2 * HS), F32),
                        pltpu.VMEM((8, HS), F32), pltpu.VMEM((8, HS), F32)] + (sd.scratch if side is not None else []),
        compiler_params=_cp(("arbitrary", "arbitrary")), name=name)(z4, z4, dy4, dy4, perm, hst, bh, ch, ar8, ai8, *sd.arrays)
    duf, dur, dbh, dch, dar, dai = outs[:6]
    return duf.reshape(lay.nt, B_W), dur.reshape(lay.nt, B_W), dbh, dch, dar, dai, list(outs[6:])


def _glu_fwd(lay, z, yf, yr, dvec, wglu, bglu, name):
    def body(u_ref, yf_ref, yr_ref, d_ref, w_ref, b_ref, o_ref, y_ref):
        y = yf_ref[...].astype(F32) + yr_ref[...].astype(F32) + d_ref[...] * u_ref[...].astype(F32)
        y_ref[...] = y
        g = _gelu(y)
        pre = _nn(g.astype(MXU_DTYPE), w_ref[...]) + b_ref[...]
        o_ref[...] = (g * _sigmoid(pre)).astype(o_ref.dtype)

    tok = pl.BlockSpec((MT * TB, B_W), lambda j: (j, 0))
    vec = pl.BlockSpec((1, B_W), lambda j: (0, 0))
    return pl.pallas_call(
        body, grid=(lay.nb // MT,),
        in_specs=[pl.BlockSpec((MT * TB, B_W), lambda j: (j, 1)), tok, tok, vec,
                  pl.BlockSpec((B_W, B_W), lambda j: (0, 0)), vec],
        out_specs=[tok, tok],
        out_shape=[jax.ShapeDtypeStruct((lay.nt, B_W), MXU_DTYPE), jax.ShapeDtypeStruct((lay.nt, B_W), F32)],
        compiler_params=_cp(("parallel",)), name=name)(z, yf, yr, dvec, wglu, bglu)


def _glu_bwd(lay, z, y, ds, dvec, wglu, bglu, name):
    def body(u_ref, y_ref, ds_ref, d_ref, w_ref, b_ref, dy_ref, dud_ref, dw_ref, db_ref, dd_ref):
        j = pl.program_id(0)

        @pl.when(j == 0)
        def _():
            dw_ref[...] = jnp.zeros_like(dw_ref)
            db_ref[...] = jnp.zeros_like(db_ref)
            dd_ref[...] = jnp.zeros_like(dd_ref)

        yv = y_ref[...]
        g = _gelu(yv)
        gb = g.astype(MXU_DTYPE)
        sg = _sigmoid(_nn(gb, w_ref[...]) + b_ref[...])
        dsv = ds_ref[...].astype(F32)
        dpre = dsv * g * sg * (1.0 - sg)
        dpre_b = dpre.astype(MXU_DTYPE)
        dg = dsv * sg + _nt(dpre_b, w_ref[...])
        dw_ref[...] += _tn(gb, dpre_b)
        db_ref[...] += jnp.sum(dpre, axis=0, keepdims=True)
        dy = dg * _gelu_grad(yv)
        dy_ref[...] = dy.astype(dy_ref.dtype)
        dd_ref[...] += jnp.sum(dy * u_ref[...].astype(F32), axis=0, keepdims=True)
        dud_ref[...] = (dy * d_ref[...]).astype(dud_ref.dtype)

    tok = pl.BlockSpec((MT * TB, B_W), lambda j: (j, 0))
    vec = pl.BlockSpec((1, B_W), lambda j: (0, 0))
    mat = pl.BlockSpec((B_W, B_W), lambda j: (0, 0))
    vsh = jax.ShapeDtypeStruct((1, B_W), F32)
    return pl.pallas_call(
        body, grid=(lay.nb // MT,),
        in_specs=[pl.BlockSpec((MT * TB, B_W), lambda j: (j, 1)), tok, tok, vec, mat, vec],
        out_specs=[tok, tok, mat, vec, vec],
        out_shape=[jax.ShapeDtypeStruct((lay.nt, B_W), MXU_DTYPE), jax.ShapeDtypeStruct((lay.nt, B_W), F32),
                   jax.ShapeDtypeStruct((B_W, B_W), F32), vsh, vsh],
        compiler_params=_cp(("arbitrary",)), name=name)(z, y, ds, dvec, wglu, bglu)


def _dz_assemble(lay, dz_a, duf, dur, dud, dz_p, name):
    def body(a_ref, f_ref, r_ref, d_ref, p_ref, o_ref):
        o_ref[:, :2 * A_W] = a_ref[...].astype(o_ref.dtype)
        o_ref[:, 2 * A_W:2 * A_W + B_W] = (f_ref[...].astype(F32) + r_ref[...].astype(F32) + d_ref[...]).astype(o_ref.dtype)
        o_ref[:, 2 * A_W + B_W:] = p_ref[...].astype(o_ref.dtype)

    spec = lambda w: pl.BlockSpec((MT * TB, w), lambda j: (j, 0))
    return pl.pallas_call(
        body, grid=(lay.nb // MT,), in_specs=[spec(2 * A_W), spec(B_W), spec(B_W), spec(B_W), spec(C_W)],
        out_specs=spec(D_IN), out_shape=jax.ShapeDtypeStruct((lay.nt, D_IN), MXU_DTYPE),
        compiler_params=_cp(("parallel",)), name=name)(dz_a, duf, dur, dud, dz_p)


def _expand_rows(a):
    return jnp.broadcast_to(a[:, :, None, :], (2, SSM_G, SSM_H, SSM_P)).reshape(-1, SSM_P)


def _ssm_params(lam_re, lam_im, log_dt, b_re, b_im, c_re, c_im, name):
    lrx, lix = _expand_rows(lam_re), _expand_rows(lam_im)
    ldtx = _expand_rows(jnp.broadcast_to(log_dt[:, :, None], (2, SSM_G, SSM_P)))
    brt = jnp.transpose(b_re, (0, 1, 3, 2)).reshape(-1, SSM_P)
    bit = jnp.transpose(b_im, (0, 1, 3, 2)).reshape(-1, SSM_P)
    arx, aix, bbr, bbi = _disc_fwd(lrx, lix, ldtx, brt, bit, name)
    ar = arx.reshape(2, SSM_G, SSM_H, SSM_P)[:, :, 0].reshape(2, SSM_G * SSM_P)
    ai = aix.reshape(2, SSM_G, SSM_H, SSM_P)[:, :, 0].reshape(2, SSM_G * SSM_P)
    eye = jnp.eye(GQ, dtype=F32)

    def bmat(bt):
        t = bt.reshape(2, 4, GQ, SSM_H, SSM_P)
        return jnp.einsum('dqghp,gk->qdghkp', t, eye).reshape(4, 2 * QC, QS)

    bh = jnp.concatenate([bmat(bbr), bmat(bbi)], axis=-1).astype(MXU_DTYPE)

    def cmat(c):
        t = c.reshape(2, 4, GQ, SSM_H, SSM_P)
        return jnp.einsum('dqghp,gk->qgpdkh', t, eye).reshape(4, QS, 2 * QC)

    ch = jnp.concatenate([cmat(c_re), -cmat(c_im)], axis=1).astype(MXU_DTYPE)

    def rows8(a):
        return jnp.repeat(a, 4, axis=0)

    return dict(lrx=lrx, lix=lix, ldtx=ldtx, brt=brt, bit=bit, bh=bh, ch=ch, ar8=rows8(ar), ai8=rows8(ai))


def _ssm_param_grads(sp, dbh, dch, dar, dai, name):
    def bdiag(m):
        t = m.reshape(4, 2, GQ, SSM_H, GQ, SSM_P)
        return jnp.einsum('qdghgp->dqghp', t).reshape(-1, SSM_P)

    dbr, dbi = bdiag(dbh[..., :QS]), bdiag(dbh[..., QS:])

    def cdiag(m):
        t = m.reshape(4, GQ, SSM_P, 2, GQ, SSM_H)
        return jnp.einsum('qgpdgh->dqghp', t).reshape(2, SSM_G, SSM_H, SSM_P)

    dc_re, dc_im = cdiag(dch[:, :QS]), -cdiag(dch[:, QS:])

    def hrow(a):
        t = a.reshape(2, SSM_G, 1, SSM_P)
        return jnp.concatenate([t, jnp.zeros((2, SSM_G, SSM_H - 1, SSM_P), F32)], axis=2).reshape(-1, SSM_P)

    glr, gli, gdt, gbr, gbi = _disc_bwd(sp["lrx"], sp["lix"], sp["ldtx"], sp["brt"], sp["bit"],
                                        hrow(dar), hrow(dai), dbr, dbi, name)
    to_b = lambda g: jnp.transpose(g.reshape(2, SSM_G, SSM_H, SSM_P), (0, 1, 3, 2))
    return dict(ssm_lam_re=glr.reshape(2, SSM_G, SSM_P), ssm_lam_im=gli.reshape(2, SSM_G, SSM_P),
                ssm_log_dt=gdt.reshape(2, SSM_G), ssm_b_re=to_b(gbr), ssm_b_im=to_b(gbi),
                ssm_c_re=dc_re, ssm_c_im=dc_im)


def _layer_consts(p):
    c = {}
    c["ws"] = p["sgu_w"].astype(MXU_DTYPE)
    c["wst"] = jnp.transpose(p["sgu_w"], (0, 2, 1)).astype(MXU_DTYPE)
    c["gbias"] = jnp.repeat(p["sgu_b"].T, 64, axis=1)
    pw = jnp.zeros((C_W, C_W), F32)
    for i in range(4):
        pw = pw.at[64 * i:64 * i + 64, 64 * i:64 * i + 64].set(p["pool_w"][i])
    c["pw"] = pw.astype(MXU_DTYPE)
    c["pscale"] = p["pool_scale"].reshape(1, C_W)
    c["dvec"] = p["ssm_d"].reshape(1, B_W)
    c["bglu"] = p["glu_b"].reshape(1, B_W)
    return c


def _layer_fwd(lay, i, x, modarr, p, w, cst, sp, bands, inv, perm, sides=None, last=False):
    n = f"l{i}_"
    sides = sides or {}
    win_side, win_fill = sides.get("win", (None, None))
    ssm_side, ssm_fill = sides.get("ssm", (None, None))
    ffn_side, ffn_fill = sides.get("ffn", (None, None))
    res = {"x0": x}
    h = _normmod_fwd(lay, x, p["norm_mix_pre"].reshape(1, D), modarr, 0, 1, n + "nm1")
    z = _mm([(h, w["win_t"])], True, MXU_DTYPE, n + "win", side=win_side)
    if win_side is not None:
        z, extra = z
        win_fill(extra)
    a = _gate_fwd(lay, z, cst["ws"], cst["gbias"], n + "gate")
    yf, yr, hst, extra = _ssm_fwd(lay, z, perm, sp["bh"], sp["ch"], sp["ar8"], sp["ai8"], n + "ssm", ssm_side)
    if ssm_side is not None:
        ssm_fill(extra)
    s, y = _glu_fwd(lay, z, yf, yr, cst["dvec"], w["wglu"], cst["bglu"], n + "glu")
    c = _pool_fwd(lay, z, bands, inv, cst["pw"], cst["pscale"], n + "pool")
    mcat = jnp.concatenate([s, a, c], axis=1)
    res["wout_p"] = _perm_wout(w["wout"])
    m = _mm([(mcat, res["wout_p"])], False, MXU_DTYPE, n + "wout")
    x1, h2 = _resnorm_normmod_fwd(lay, x, m, p["norm_mix_post"].reshape(1, D), p["norm_ffn_pre"].reshape(1, D),
                                  modarr, 2, 3, 4, n + "rn1nm2")
    g, u, act, extra = _ffn_up(h2, w["wg_t"], w["wu_t"], n + "ffn_up", ffn_side)
    if ffn_side is not None:
        ffn_fill(extra)
    f = _mm([(act, w["wd"])], False, MXU_DTYPE, n + "ffn_down")
    res.update(h=h, z=z, hst=hst, y=y, mcat=mcat, m=m, x1=x1, h2=h2, g=g, u=u, act=act, f=f)
    if last:
        return None, res
    x2 = _resnorm_fwd(lay, x1, f, p["norm_ffn_post"].reshape(1, D), modarr, 5, n + "rn2")
    return x2, res


def _layer_bwd(lay, i, dx2, modarr, p, w, cst, sp, bands, inv, perm, res, side_fns=None):
    n = f"l{i}b_"
    big, small = {}, {}
    side_fns = side_fns or {}
    side_of = lambda key: side_fns[key](big) if key in side_fns else None
    df, dg2, gpost2 = _resnorm_bwd(lay, dx2, res["f"], p["norm_ffn_post"].reshape(1, D), modarr, 5, n + "rn2")
    big["wd"] = _mm_tn(res["act"], df, MXU_DTYPE, n + "dwd")
    dg, du, early = _ffn_down_bwd(df, w["wd"], res["g"], res["u"], n + "ffn_down", side_of("ffn_down"))
    dh2_side = side_of("dh2")
    dh2 = _mm([(dg, w["wg_t"]), (du, w["wu_t"])], False, MXU_DTYPE, n + "dh2", side=dh2_side)
    if dh2_side is not None:
        dh2, ex = dh2
        early = early + ex
    big["wg_t"] = _mm_tn(dg, res["h2"], MXU_DTYPE, n + "dwg")
    big["wu_t"] = _mm_tn(du, res["h2"], MXU_DTYPE, n + "dwu")
    dx1, dm, dsh2, dsc2, gpre2, dg1, gpost1 = _normmod_resnorm_bwd(
        lay, res["x1"], dh2, dx2, p["norm_ffn_pre"].reshape(1, D), res["m"], p["norm_mix_post"].reshape(1, D), modarr,
        4, 2, n + "nm2rn1")
    big["wout"] = _unperm_wout(_mm_tn(res["mcat"], dm, MXU_DTYPE, n + "dwout"))
    dmcat = _mm([(dm, res["wout_p"])], True, MXU_DTYPE, n + "dmcat")
    z = res["z"]
    dz_a, dws, dgb = _gate_bwd(lay, z, dmcat, cst["ws"], cst["wst"], cst["gbias"], n + "gate")
    dy, dud, dwglu, dbglu, ddvec = _glu_bwd(lay, z, res["y"], dmcat, cst["dvec"], w["wglu"], cst["bglu"], n + "glu")
    big["wglu"] = dwglu.astype(MXU_DTYPE)
    duf, dur, dbh, dch, dar, dai, ex = _ssm_bwd(lay, z, dy, perm, res["hst"], sp["bh"], sp["ch"], sp["ar8"],
                                                sp["ai8"], n + "ssm", side_of("ssm"))
    early = early + ex
    dz_p, dpw, dpsc = _pool_bwd(lay, z, dmcat, bands, inv, cst["pw"], cst["pscale"], n + "pool")
    dz = _dz_assemble(lay, dz_a, duf, dur, dud, dz_p, n + "dz")
    big["win_t"] = _mm_tn(dz, res["h"], MXU_DTYPE, n + "dwin")
    dh_side = side_of("dh")
    dh = _mm([(dz, w["win_t"])], False, MXU_DTYPE, n + "dh", side=dh_side)
    if dh_side is not None:
        dh, ex = dh
        early = early + ex
    dx, dsh1, dsc1, gpre1 = _normmod_bwd(lay, res["x0"], dh, dx1, p["norm_mix_pre"].reshape(1, D), modarr, 1, n + "nm1",
                                         latent_only=(i == 0))

    small.update(norm_mix_pre=gpre1[0], norm_mix_post=gpost1[0], norm_ffn_pre=gpre2[0], norm_ffn_post=gpost2[0])
    small["sgu_w"] = dws
    small["sgu_b"] = jnp.sum(dgb.reshape(CHUNK, 4, 64), axis=-1).T
    small.update(_ssm_param_grads(sp, dbh, dch, dar, dai, n + "disc"))
    small["ssm_d"] = ddvec.reshape(SSM_G, SSM_H)
    small["glu_b"] = dbglu[0]
    small["pool_w"] = jnp.stack([dpw[64 * k:64 * k + 64, 64 * k:64 * k + 64] for k in range(4)])
    small["pool_scale"] = dpsc[0]
    dmod = jnp.concatenate([dsh1, dsc1, dg1, dsh2, dsc2, dg2], axis=1)[:lay.bl + 1]
    dmod = jnp.concatenate([dmod, jnp.zeros((8 - lay.bl - 1, 6, D), F32)], axis=0)
    return dx, big, small, dmod, early


def _perm_wout(w):
    return w.reshape(4, D // 4, D)[np.array(WOUT_PERM)].reshape(D, D)


def _unperm_wout(g):
    return g.reshape(4, D // 4, D)[np.array(WOUT_INV)].reshape(D, D)


SMALL_NAMES = ["norm_mix_pre", "norm_mix_post", "norm_ffn_pre", "norm_ffn_post", "sgu_w", "sgu_b", "ssm_lam_re",
               "ssm_lam_im", "ssm_log_dt", "ssm_b_re", "ssm_b_im", "ssm_c_re", "ssm_c_im", "ssm_d", "glu_b", "pool_w",
               "pool_scale"]
BIG_NAMES = ["win_t", "wout", "wglu", "wg_t", "wu_t", "wd"]


def _sincos_2d(rows, cols, dim):
    quarter = dim // 4
    omega = 1.0 / (10000.0 ** (jnp.arange(quarter, dtype=F32) / quarter))
    r = jnp.arange(rows, dtype=F32)[:, None] * omega
    cc = jnp.arange(cols, dtype=F32)[:, None] * omega
    er = jnp.concatenate([jnp.sin(r), jnp.cos(r)], axis=-1)
    ec = jnp.concatenate([jnp.sin(cc), jnp.cos(cc)], axis=-1)
    pe = jnp.concatenate([jnp.broadcast_to(er[:, None, :], (rows, cols, dim // 2)),
                          jnp.broadcast_to(ec[None, :, :], (rows, cols, dim // 2))], axis=-1)
    return pe.reshape(rows * cols, dim)


def _core(x, ctx, target, mods_local, params, weights, w_sides=None, g_side_fns=None):
    bl, lat, _ = x.shape
    assert bl == 4 and lat % TB == 0, "the scan fills 8 sublanes with 2 directions x 4 sequences"
    lay = _Layout(bl, lat)
    pe = _sincos_2d(lat // GRID_W, GRID_W, D)
    bands_np, inv_np = _band_constants()
    bands, inv = jnp.asarray(bands_np, MXU_DTYPE), jnp.asarray(inv_np, F32)
    perm = jnp.asarray(_scan_perm(bl), MXU_DTYPE)
    csts, sps, ress, wls = [], [], [], []
    for i in range(2):
        csts.append(_layer_consts(params[i]))
        p = params[i]
        sps.append(_ssm_params(p["ssm_lam_re"], p["ssm_lam_im"], p["ssm_log_dt"], p["ssm_b_re"], p["ssm_b_im"],
                               p["ssm_c_re"], p["ssm_c_im"], f"l{i}_disc"))
        wls.append(dict(weights[i]))

    embed_side, embed_fill = (w_sides[0].get("embed") if w_sides else None) or (None, None)
    xt, extra = _embed(lay, x, ctx, pe, embed_side)
    if embed_side is not None:
        embed_fill(wls, extra)
    if callable(mods_local):
        mods_local = mods_local()
    modarrs = [lay.mod_tiles(mods_local[i]) for i in range(2)]
    for i in range(2):
        sides = {}
        for key, (side, fill) in ((w_sides or [{}, {}])[i]).items():
            sides[key] = (side, functools.partial(fill, wls))
        xt, res = _layer_fwd(lay, i, xt, modarrs[i], params[i], wls[i], csts[i], sps[i], bands, inv, perm, sides,
                             last=(i == 1))
        ress.append(res)
    dx, lossv = _resnorm_loss(lay, ress[1]["x1"], ress[1]["f"], params[1]["norm_ffn_post"].reshape(1, D), modarrs[1], 5,
                              target)
    bigs, smalls, dmods, early = [None, None], [None, None], [None, None], []
    for i in (1, 0):
        fns = {}
        if i == 0 and g_side_fns is not None:
            fns = {key: functools.partial(fn, bigs[1]) for key, fn in g_side_fns.items()}
        dx, bigs[i], smalls[i], dmods[i], ex = _layer_bwd(lay, i, dx, modarrs[i], params[i], wls[i], csts[i], sps[i],
                                                           bands, inv, perm, ress[i], fns)
        early += ex
    return lossv[0, 0], dx.reshape(bl, lat, D), bigs, smalls, dmods, early


def _my_index():
    return 4 * lax.axis_index("x") + 2 * lax.axis_index("y") + lax.axis_index("c")


def _peer(k):
    x, y, c = lax.axis_index("x"), lax.axis_index("y"), lax.axis_index("c")
    kx, ky, kc = (k >> 2) & 1, (k >> 1) & 1, k & 1
    px = 1 - x if kx else x
    py = 1 - y if ky else y
    pc = 1 - c if kc else c
    return (px, py, pc), 4 * px + 2 * py + pc


class _Side:
    def __init__(self, items):
        self.items = items
        self.n = len(items)
        self.ncopies = sum(len(it[2]) for it in items)
        self.arrays = [it[0] for it in items]
        anyspec = pl.BlockSpec(memory_space=pl.ANY)
        self.in_specs = [anyspec] * self.n
        self.out_specs = [anyspec] * self.n
        self.out_shape = [jax.ShapeDtypeStruct((slots,) + tuple(a.shape) if mode == "gather" else tuple(a.shape), a.dtype)
                          for a, mode, ks, slots in items]
        self.scratch = [pltpu.SemaphoreType.DMA((self.ncopies,)), pltpu.SemaphoreType.DMA((self.ncopies,)),
                        pltpu.SemaphoreType.DMA((self.n,))]

    def _copies(self, ins, outs, sems):
        send_sems, recv_sems, local_sems = sems
        slot_of = lambda idx, slots: idx if slots == 8 else (idx // 2 if slots == 4 else idx % 2)
        me = _my_index()
        local, sends, recvs = [], [], []
        q = 0
        for t, (arr, mode, ks, slots) in enumerate(self.items):
            src_own = ins[t] if mode == "gather" else ins[t].at[me]
            local.append(pltpu.make_async_copy(src_own, outs[t].at[slot_of(me, slots)], local_sems.at[t]))
            for k in ks:
                peer, pidx = _peer(k)
                src = ins[t] if mode == "gather" else ins[t].at[pidx]
                sends.append(pltpu.make_async_remote_copy(
                    src_ref=src, dst_ref=outs[t].at[slot_of(me, slots)], send_sem=send_sems.at[q], recv_sem=recv_sems.at[q],
                    device_id=peer, device_id_type=pl.DeviceIdType.MESH))
                recvs.append(pltpu.make_async_remote_copy(
                    src_ref=src, dst_ref=outs[t].at[slot_of(pidx, slots)], send_sem=send_sems.at[q], recv_sem=recv_sems.at[q],
                    device_id=peer, device_id_type=pl.DeviceIdType.MESH))
                q += 1
        return local, sends, recvs

    def start(self, ins, outs, sems):
        local, sends, _ = self._copies(ins, outs, sems)
        for cp in sends + local:
            cp.start()

    def wait(self, ins, outs, sems):
        local, sends, recvs = self._copies(ins, outs, sems)
        for cp in recvs:
            cp.wait_recv()
        for cp in sends:
            cp.wait_send()
        for cp in local:
            cp.wait()


def _comm(items, name):
    side = _Side(items)
    n = side.n

    def body(*refs):
        ins, outs, sems = refs[:n], refs[n:2 * n], refs[2 * n:]
        side.start(ins, outs, sems)
        side.wait(ins, outs, sems)

    return pl.pallas_call(
        body, in_specs=side.in_specs, out_specs=side.out_specs, out_shape=side.out_shape, scratch_shapes=side.scratch,
        compiler_params=pltpu.CompilerParams(has_side_effects=True), name=name)(*side.arrays)


def _spread(items, name):
    n = len(items)
    ncopies = sum(len(it[1]) for it in items)

    def slot_of(idx, slots):
        return idx if slots == 8 else (idx // 2 if slots == 4 else idx % 2)

    def body(*refs):
        ins, outs, bufs = refs[:n], refs[n:2 * n], refs[2 * n:3 * n]
        load_sems, store_sems, send_sems, recv_sems = refs[3 * n:]
        me = _my_index()
        loads = [pltpu.make_async_copy(ins[t], bufs[t], load_sems.at[t]) for t in range(n)]
        for cp in loads:
            cp.start()
        stores, sends, recvs = [], [], []
        q = 0
        for t, (arr, ks, slots) in enumerate(items):
            loads[t].wait()
            own = outs[t].at[slot_of(me, slots)]
            stores.append(pltpu.make_async_copy(bufs[t], own, store_sems.at[t]))
            stores[-1].start()
            for k in ks:
                peer, pidx = _peer(k)
                sends.append(pltpu.make_async_remote_copy(
                    src_ref=bufs[t], dst_ref=own, send_sem=send_sems.at[q], recv_sem=recv_sems.at[q],
                    device_id=peer, device_id_type=pl.DeviceIdType.MESH))
                recvs.append(pltpu.make_async_remote_copy(
                    src_ref=bufs[t], dst_ref=outs[t].at[slot_of(pidx, slots)], send_sem=send_sems.at[q],
                    recv_sem=recv_sems.at[q], device_id=peer, device_id_type=pl.DeviceIdType.MESH))
                sends[-1].start()
                q += 1
        for cp in recvs:
            cp.wait_recv()
        for cp in sends:
            cp.wait_send()
        for cp in stores:
            cp.wait()

    anyspec = pl.BlockSpec(memory_space=pl.ANY)
    return pl.pallas_call(
        body, in_specs=[anyspec] * n, out_specs=[anyspec] * n,
        out_shape=[jax.ShapeDtypeStruct((slots,) + tuple(arr.shape), arr.dtype) for arr, ks, slots in items],
        scratch_shapes=[pltpu.VMEM(tuple(arr.shape), arr.dtype) for arr, ks, slots in items]
        + [pltpu.SemaphoreType.DMA((n,)), pltpu.SemaphoreType.DMA((n,)), pltpu.SemaphoreType.DMA((ncopies,)),
           pltpu.SemaphoreType.DMA((ncopies,))],
        compiler_params=pltpu.CompilerParams(has_side_effects=True, vmem_limit_bytes=VMEM_LIMIT),
        name=name)(*[it[0] for it in items])


ALL7 = (1, 2, 3, 4, 5, 6, 7)
CHIPS3 = (2, 4, 6)


def _sum8(parts, name):
    def one(a, nm):
        _, r, c = a.shape
        tr = r if r <= 512 else _pick_rows(r)

        def body(a_ref, o_ref):
            acc = a_ref[0].astype(F32)
            for q in range(1, a_ref.shape[0]):
                acc = acc + a_ref[q].astype(F32)
            o_ref[...] = acc

        return pl.pallas_call(
            body, grid=(r // tr,), in_specs=[pl.BlockSpec((a.shape[0], tr, c), lambda i: (0, i, 0))],
            out_specs=pl.BlockSpec((tr, c), lambda i: (i, 0)), out_shape=jax.ShapeDtypeStruct((r, c), F32),
            compiler_params=_cp(("parallel",)), name=nm)(a)

    return [one(a, f"{name}{i}") for i, a in enumerate(parts)]


def _pick_rows(r, cap=512):
    for t in (512, 352, 256, 176, 128, 64, 32, 16, 8):
        if r % t == 0 and t <= cap:
            return t
    return r


def _adam(w, g, m, v, name):
    shape = w.shape
    nel = int(np.prod(shape))
    c1 = 1.0 / (1.0 - ADAM_B1 ** ADAM_STEP)
    c2 = 1.0 / (1.0 - ADAM_B2 ** ADAM_STEP)

    def body(w_ref, g_ref, m_ref, v_ref, d_ref, nm_ref, nv_ref):
        gv = g_ref[...]
        nm = ADAM_B1 * m_ref[...] + (1.0 - ADAM_B1) * gv
        nv = ADAM_B2 * v_ref[...] + (1.0 - ADAM_B2) * (gv * gv)
        d_ref[...] = -ADAM_LR * ((nm * c1) / (jnp.sqrt(nv * c2) + ADAM_EPS) + ADAM_WD * w_ref[...])
        nm_ref[...] = nm
        nv_ref[...] = nv

    padded = int(np.prod(shape[:-2])) * (-(-shape[-2] // 8) * 8) * (-(-shape[-1] // 128) * 128) if len(shape) >= 2 else nel
    if len(shape) >= 2 and padded <= 1024 * 1024:
        sh = jax.ShapeDtypeStruct(shape, F32)
        return pl.pallas_call(body, out_shape=[sh] * 3, compiler_params=_cp(None), name=name)(w, g, m, v)

    if len(shape) >= 2 and shape[-1] >= 128:
        lanes = shape[-1]
    else:
        lanes = 512 if nel % 512 == 0 else 128
    r = nel // lanes
    tr = r if r * lanes <= 384 * 1024 else _pick_rows(r, 384 * 1024 // lanes)

    spec = pl.BlockSpec((tr, lanes), lambda i: (i, 0))
    sh = jax.ShapeDtypeStruct((r, lanes), F32)
    outs = pl.pallas_call(
        body, grid=(r // tr,), in_specs=[spec] * 4, out_specs=[spec] * 3, out_shape=[sh] * 3,
        compiler_params=_cp(("parallel",)), name=name)(*[a.reshape(r, lanes) for a in (w, g, m, v)])
    return [o.reshape(shape) for o in outs]


def _silu(x):
    return x * _sigmoid(x)


def _mod_fwd(c_rows, w_mod, b_cols, name):
    def body(c_ref, w_ref, b_ref, o_ref):
        s = _silu(c_ref[...])
        for l in range(2):
            o_ref[l] = jnp.dot(s, w_ref[l], preferred_element_type=F32, precision=lax.Precision.HIGHEST) + b_ref[l]

    nc = w_mod.shape[2]
    return pl.pallas_call(body, out_shape=jax.ShapeDtypeStruct((2, c_rows.shape[0], nc), F32),
                          compiler_params=_cp(None), name=name)(c_rows, w_mod, b_cols)


def _mod_bwd(c_rows, w_mod, dlat, dctx8, name):
    nrow = c_rows.shape[0]
    nb = nrow - 8

    def body(c_ref, w_ref, dl_ref, dc_ref, gw_ref, gc_ref):
        s = _silu(c_ref[...])
        ctx_row = lax.broadcasted_iota(jnp.int32, (nrow, 1), 0) == nb
        gc = jnp.zeros((1, D), F32)
        for l in range(2):
            dctx = dc_ref[0, l]
            for q in range(1, 8):
                dctx = dctx + dc_ref[q, l]
            dm = dl_ref[l] + jnp.where(ctx_row, dctx, 0.0)
            gw_ref[l] = lax.dot_general(s, dm, (((0,), (0,)), ((), ())), preferred_element_type=F32,
                                        precision=lax.Precision.HIGHEST)
            gc = gc + lax.dot_general(dctx, w_ref[l], (((1,), (1,)), ((), ())), preferred_element_type=F32,
                                      precision=lax.Precision.HIGHEST)
        gc_ref[...] = gc

    nc = w_mod.shape[2]
    return pl.pallas_call(body, out_shape=[jax.ShapeDtypeStruct((2, D, nc), F32), jax.ShapeDtypeStruct((1, D), F32)],
                          compiler_params=_cp(None), name=name)(c_rows, w_mod, dlat, dctx8)


def _bmod_cctx(dmod_all, gc4, c_ctx, name):
    def body(dm_ref, gc_ref, cc_ref, gb_ref, gcc_ref):
        for l in range(2):
            acc = jnp.sum(dm_ref[0, l], axis=0, keepdims=True)
            for q in range(1, 8):
                acc = acc + jnp.sum(dm_ref[q, l], axis=0, keepdims=True)
            gb_ref[l:l + 1, :] = acc
        g = gc_ref[0] + gc_ref[1] + gc_ref[2] + gc_ref[3]
        cv = cc_ref[...]
        sg = _sigmoid(cv)
        gcc_ref[...] = g * (sg * (1.0 + cv * (1.0 - sg)))

    return pl.pallas_call(body, out_shape=[jax.ShapeDtypeStruct((2, 6 * D), F32), jax.ShapeDtypeStruct((1, D), F32)],
                          compiler_params=_cp(None), name=name)(dmod_all, gc4, c_ctx)


def kernel(x, c, ctx, c_ctx, w_mod, b_mod, norm_mix_pre, norm_mix_post, norm_ffn_pre, norm_ffn_post, w_in, w_out, sgu_w, sgu_b, ssm_lam_re, ssm_lam_im, ssm_log_dt, ssm_b_re, ssm_b_im, ssm_c_re, ssm_c_im, ssm_d, glu_w, glu_b, pool_w, pool_scale, ffn_w_gate, ffn_w_up, ffn_w_down, loss_target, m_c_ctx, m_w_mod, m_b_mod, m_norm_mix_pre, m_norm_mix_post, m_norm_ffn_pre, m_norm_ffn_post, m_w_in, m_w_out, m_sgu_w, m_sgu_b, m_ssm_lam_re, m_ssm_lam_im, m_ssm_log_dt, m_ssm_b_re, m_ssm_b_im, m_ssm_c_re, m_ssm_c_im, m_ssm_d, m_glu_w, m_glu_b, m_pool_w, m_pool_scale, m_ffn_w_gate, m_ffn_w_up, m_ffn_w_down, v_c_ctx, v_w_mod, v_b_mod, v_norm_mix_pre, v_norm_mix_post, v_norm_ffn_pre, v_norm_ffn_post, v_w_in, v_w_out, v_sgu_w, v_sgu_b, v_ssm_lam_re, v_ssm_lam_im, v_ssm_log_dt, v_ssm_b_re, v_ssm_b_im, v_ssm_c_re, v_ssm_c_im, v_ssm_d, v_glu_w, v_glu_b, v_pool_w, v_pool_scale, v_ffn_w_gate, v_ffn_w_up, v_ffn_w_down):
    wts = dict(c_ctx=c_ctx, w_mod=w_mod, b_mod=b_mod, norm_mix_pre=norm_mix_pre, norm_mix_post=norm_mix_post,
               norm_ffn_pre=norm_ffn_pre, norm_ffn_post=norm_ffn_post, w_in=w_in, w_out=w_out, sgu_w=sgu_w, sgu_b=sgu_b,
               ssm_lam_re=ssm_lam_re, ssm_lam_im=ssm_lam_im, ssm_log_dt=ssm_log_dt, ssm_b_re=ssm_b_re, ssm_b_im=ssm_b_im,
               ssm_c_re=ssm_c_re, ssm_c_im=ssm_c_im, ssm_d=ssm_d, glu_w=glu_w, glu_b=glu_b, pool_w=pool_w,
               pool_scale=pool_scale, ffn_w_gate=ffn_w_gate, ffn_w_up=ffn_w_up, ffn_w_down=ffn_w_down)
    ms = dict(c_ctx=m_c_ctx, w_mod=m_w_mod, b_mod=m_b_mod, norm_mix_pre=m_norm_mix_pre, norm_mix_post=m_norm_mix_post,
              norm_ffn_pre=m_norm_ffn_pre, norm_ffn_post=m_norm_ffn_post, w_in=m_w_in, w_out=m_w_out, sgu_w=m_sgu_w,
              sgu_b=m_sgu_b, ssm_lam_re=m_ssm_lam_re, ssm_lam_im=m_ssm_lam_im, ssm_log_dt=m_ssm_log_dt,
              ssm_b_re=m_ssm_b_re, ssm_b_im=m_ssm_b_im, ssm_c_re=m_ssm_c_re, ssm_c_im=m_ssm_c_im, ssm_d=m_ssm_d,
              glu_w=m_glu_w, glu_b=m_glu_b, pool_w=m_pool_w, pool_scale=m_pool_scale, ffn_w_gate=m_ffn_w_gate,
              ffn_w_up=m_ffn_w_up, ffn_w_down=m_ffn_w_down)
    vs = dict(c_ctx=v_c_ctx, w_mod=v_w_mod, b_mod=v_b_mod, norm_mix_pre=v_norm_mix_pre, norm_mix_post=v_norm_mix_post,
              norm_ffn_pre=v_norm_ffn_pre, norm_ffn_post=v_norm_ffn_post, w_in=v_w_in, w_out=v_w_out, sgu_w=v_sgu_w,
              sgu_b=v_sgu_b, ssm_lam_re=v_ssm_lam_re, ssm_lam_im=v_ssm_lam_im, ssm_log_dt=v_ssm_log_dt,
              ssm_b_re=v_ssm_b_re, ssm_b_im=v_ssm_b_im, ssm_c_re=v_ssm_c_re, ssm_c_im=v_ssm_c_im, ssm_d=v_ssm_d,
              glu_w=v_glu_w, glu_b=v_glu_b, pool_w=v_pool_w, pool_scale=v_pool_scale, ffn_w_gate=v_ffn_w_gate,
              ffn_w_up=v_ffn_w_up, ffn_w_down=v_ffn_w_down)
    order = list(wts.keys())
    bl = x.shape[0]
    nseq = bl * N_DEV
    me = _my_index()
    chip = me // 2
    ncol = w_mod.shape[2]

    (c_all,) = _spread([(c, ALL7, 8)], "ag_c")
    nrow = nseq + 8
    c_rows = jnp.concatenate([c_all.reshape(nseq, D), c_ctx[None], jnp.zeros((7, D), F32)], axis=0)
    b_cols = lax.dynamic_slice_in_dim(b_mod, chip * ncol, ncol, axis=1)[:, None, :]
    mod_cols = _mod_fwd(c_rows, w_mod, b_cols, "mod_fwd")
    stash = {}

    def mods_local():
        mods = jnp.transpose(stash["mod4"], (1, 2, 0, 3)).reshape(2, nrow, 6 * D)
        return jnp.concatenate([lax.dynamic_slice_in_dim(mods, me * bl, bl, axis=1), mods[:, nseq:nseq + 1],
                                jnp.zeros((2, 8 - bl - 1, 6 * D), F32)], axis=1)

    shards = {}
    for i in range(2):
        for nme, s in zip(BIG_NAMES, [w_in[i].T, w_out[i], glu_w[i], ffn_w_gate[i].T, ffn_w_up[i].T, ffn_w_down[i]]):
            shards[(i, nme)] = s.astype(MXU_DTYPE)
    weights = [{}, {}]
    ffn_names = ("wg_t", "wu_t", "wd")
    w_plan = [{"embed": [(0, "win_t")], "win": [(0, "wout"), (0, "wglu")], "ssm": [(0, "wg_t"), (0, "wu_t")],
               "ffn": [(0, "wd"), (1, "win_t"), (1, "wout"), (1, "wglu")]},
              {"ssm": [(1, "wg_t"), (1, "wu_t")], "ffn": [(1, "wd")]}]

    def w_entry(keys, more=()):
        def fill(wls, gathered):
            for (i, nme), g in zip(keys, gathered):
                wls[i][nme] = g.reshape(-1, g.shape[-1])
            for (nme, _), g in zip(more, gathered[len(keys):]):
                stash[nme] = g
        return _Side([(shards[k2], "gather", CHIPS3, 4) for k2 in keys] + [(a, "gather", CHIPS3, 4) for _, a in more]), fill

    w_sides = [{key: w_entry(keys) for key, keys in plan.items()} for plan in w_plan]
    w_sides[0]["embed"] = w_entry(w_plan[0]["embed"], more=[("mod4", mod_cols)])

    eighths = lambda g: g.reshape(8, g.shape[0] // 8, g.shape[1])
    g_plan = {"ffn_down": [(1, "win_t"), (1, "wg_t")], "dh2": [(1, "wu_t"), (1, "wout"), (1, "wglu")],
              "ssm": [(0, k) for k in BIG_NAMES if k != "win_t"] + [(1, "wd")], "dh": [(0, "win_t")]}
    early_g = g_plan["ffn_down"] + g_plan["dh2"] + g_plan["ssm"] + g_plan["dh"]

    def g_entry(keys):
        return lambda big1, big0: _Side([(eighths((big1 if i == 1 else big0)[k]), "a2a", ALL7, 8) for i, k in keys])

    g_side_fns = {key: g_entry(keys) for key, keys in g_plan.items()}

    params = [{k: wts[k][i] for k in SMALL_NAMES} for i in range(2)]
    loss_part, grad_x, bigs, smalls, dmods, early = _core(x, ctx, loss_target, mods_local, params, weights,
                                                           w_sides, g_side_fns)
    loss = lax.psum(loss_part, ("x", "y", "c"))

    dmod_local = jnp.stack([dmods[i].reshape(8, 6 * D) for i in range(2)])
    (dmod_all,) = _spread([(dmod_local, ALL7, 8)], "ag_dmod")
    dcols = lax.dynamic_slice_in_dim(dmod_all, chip * ncol, ncol, axis=3)
    dlat = jnp.transpose(dcols[:, :, :bl], (1, 0, 2, 3)).reshape(2, nseq, ncol)
    dlat = jnp.concatenate([dlat, jnp.zeros((2, 8, ncol), F32)], axis=1)
    dctx8 = dcols[:, :, bl:bl + 1]
    g_w_mod, gc_part = _mod_bwd(c_rows, w_mod, dlat, dctx8, "mod_bwd")
    (gc4,) = _spread([(gc_part, CHIPS3, 4)], "ag_cctx")
    g_b_mod, g_c_ctx = _bmod_cctx(dmod_all, gc4, c_ctx[None], "bmod_cctx")

    small_flat = jnp.concatenate([jnp.stack([smalls[i][k] for i in range(2)]).reshape(-1) for k in SMALL_NAMES])
    npad = (-small_flat.shape[0]) % (8 * 1024)
    small_flat = jnp.concatenate([small_flat, jnp.zeros((npad,), F32)])
    late = _comm([(small_flat.reshape(8, -1, 1024), "a2a", ALL7, 8)], "a2a_grads")
    sums = _sum8(list(early) + list(late), "gsum")
    fin = _spread([(s, (1,), 2) for s in sums[:-1]] + [(sums[-1], ALL7, 8)], "ag_grads")
    big_g = [{}, {}]
    for (i, k), g in zip(early_g, fin[:-1]):
        big_g[i][k] = g.reshape(-1, g.shape[-1])
    small_red = fin[-1].reshape(-1)

    grads = {}
    off = 0
    for k in SMALL_NAMES:
        shp = wts[k].shape
        nel = int(np.prod(shp))
        grads[k] = small_red[off:off + nel].reshape(shp)
        off += nel
    grads["c_ctx"] = g_c_ctx[0]
    grads["w_mod"] = g_w_mod
    grads["b_mod"] = g_b_mod
    grads["w_in"] = jnp.stack([big_g[i]["win_t"].T for i in range(2)])
    grads["w_out"] = jnp.stack([big_g[i]["wout"] for i in range(2)])
    grads["glu_w"] = jnp.stack([big_g[i]["wglu"] for i in range(2)])
    grads["ffn_w_gate"] = jnp.stack([big_g[i]["wg_t"].T for i in range(2)])
    grads["ffn_w_up"] = jnp.stack([big_g[i]["wu_t"].T for i in range(2)])
    grads["ffn_w_down"] = jnp.stack([big_g[i]["wd"] for i in range(2)])

    deltas, new_m, new_v = {}, {}, {}
    for k in order:
        deltas[k], new_m[k], new_v[k] = _adam(wts[k], grads[k], ms[k], vs[k], "adam_" + k)
    return (loss, grad_x, *[grads[k] for k in order], *[deltas[k] for k in order],
            *[new_m[k] for k in order], *[new_v[k] for k in order])
```

```python
import functools
import math

import numpy as np
import jax
import jax.numpy as jnp
from jax import lax
from jax.experimental import pallas as pl
from jax.experimental.pallas import tpu as pltpu

F32 = jnp.float32
BF16 = jnp.bfloat16
MXU_DTYPE = jnp.bfloat16
MCAT_A, MCAT_C = 2, 3
WOUT_PERM, WOUT_INV = (1, 2, 0, 3), (2, 0, 1, 3)

D = 1024
EPS = 1e-6
TB = 256
CTX = 256
CHUNK = 128
GRID_W = 64
A_W, B_W, C_W = 256, 512, 256
D_IN = 1280
D_FF = 2816
SSM_G, SSM_P, SSM_H = 32, 64, 16
ST = 64
POOL_WINDOWS = (2, 4, 8, 16)
N_DEV = 8
VMEM_LIMIT = 52 * 1024 * 1024
GELU_C = math.sqrt(2.0 / math.pi)

ADAM_LR, ADAM_B1, ADAM_B2, ADAM_EPS, ADAM_WD, ADAM_STEP = 0.001, 0.9, 0.999, 1e-08, 0.01, 10


def _cp(sem=None, vmem=VMEM_LIMIT, **kw):
    return pltpu.CompilerParams(dimension_semantics=sem, vmem_limit_bytes=vmem, **kw)


def _pick(n, cap):
    if n <= cap:
        return n
    best = None
    for t in range(128, cap + 1, 128):
        if n % t == 0:
            best = t
    assert best is not None, (n, cap)
    return best


def _gelu(x):
    return 0.5 * x * (1.0 + jnp.tanh(GELU_C * (x + 0.044715 * x * x * x)))


def _gelu_grad(x):
    t = jnp.tanh(GELU_C * (x + 0.044715 * x * x * x))
    return 0.5 * (1.0 + t) + 0.5 * x * (1.0 - t * t) * GELU_C * (1.0 + 3.0 * 0.044715 * x * x)


def _sigmoid(x):
    return 1.0 / (1.0 + jnp.exp(-x))


def _dot(a, b, dims):
    return lax.dot_general(a, b, (dims, ((), ())), preferred_element_type=F32)


def _nn(a, b):
    return _dot(a, b, ((1,), (0,)))


def _nt(a, b):
    return _dot(a, b, ((1,), (1,)))


def _tn(a, b):
    return _dot(a, b, ((0,), (0,)))


def _mm(pairs, nt, out_dtype, name, tm=512, side=None):
    m = pairs[0][0].shape[0]
    n = pairs[0][1].shape[0] if nt else pairs[0][1].shape[1]
    tn = _pick(n, 1408)
    tm = min(tm, m)
    npairs = len(pairs)
    ni, nj = m // tm, n // tn

    def body(*refs):
        own, srefs = _side_split(refs, 2 * npairs, 1, 0, side)
        o_ref = own[-1]
        i, j = pl.program_id(0), pl.program_id(1)
        _side_start(side, srefs, jnp.logical_and(i == 0, j == 0))
        acc = None
        for t in range(npairs):
            a = own[2 * t][...].astype(MXU_DTYPE)
            b = own[2 * t + 1][...].astype(MXU_DTYPE)
            r = _nt(a, b) if nt else _nn(a, b)
            acc = r if acc is None else acc + r
        o_ref[...] = acc.astype(o_ref.dtype)
        _side_wait(side, srefs, jnp.logical_and(i == ni - 1, j == nj - 1))

    sd = side if side is not None else _Side([])
    in_specs, flat = [], []
    for a, b in pairs:
        k = a.shape[1]
        in_specs.append(pl.BlockSpec((tm, k), lambda i, j: (i, 0)))
        in_specs.append(pl.BlockSpec((tn, k), lambda i, j: (j, 0)) if nt else pl.BlockSpec((k, tn), lambda i, j: (0, j)))
        flat += [a, b]
    outs = pl.pallas_call(
        body, grid=(ni, nj), in_specs=in_specs + sd.in_specs,
        out_specs=[pl.BlockSpec((tm, tn), lambda i, j: (i, j))] + sd.out_specs,
        out_shape=[jax.ShapeDtypeStruct((m, n), out_dtype)] + sd.out_shape,
        scratch_shapes=sd.scratch if side is not None else [],
        compiler_params=_cp(("arbitrary", "arbitrary") if side is not None else ("parallel", "parallel")),
        name=name)(*flat, *sd.arrays)
    return outs[0] if side is None else (outs[0], list(outs[1:]))


def _mm_tn(a, b, out_dtype, name):
    m, k1 = a.shape
    n = b.shape[1]
    t1 = _pick(k1, 1408)
    tn = _pick(n, 1024)
    tm = max(t for t in (512, 1024, 1536) if m % t == 0)
    nsteps = m // tm

    def body(a_ref, b_ref, o_ref, acc_ref):
        t = pl.program_id(2)

        @pl.when(t == 0)
        def _():
            acc_ref[...] = jnp.zeros_like(acc_ref)

        acc_ref[...] += _tn(a_ref[...].astype(MXU_DTYPE), b_ref[...].astype(MXU_DTYPE))

        @pl.when(t == nsteps - 1)
        def _():
            o_ref[...] = acc_ref[...].astype(o_ref.dtype)

    return pl.pallas_call(
        body, grid=(k1 // t1, n // tn, nsteps),
        in_specs=[pl.BlockSpec((tm, t1), lambda i, j, t: (t, i)), pl.BlockSpec((tm, tn), lambda i, j, t: (t, j))],
        out_specs=pl.BlockSpec((t1, tn), lambda i, j, t: (i, j)),
        out_shape=jax.ShapeDtypeStruct((k1, n), out_dtype),
        scratch_shapes=[pltpu.VMEM((t1, tn), F32)],
        compiler_params=_cp(("parallel", "parallel", "arbitrary")), name=name)(a, b)


class _Layout:
    def __init__(self, bl, lat):
        self.bl, self.lat = bl, lat
        self.nlb = lat // TB
        self.nr = 1 + self.nlb
        self.nctx = bl
        self.nb = self.nr * bl
        self.nt = self.nb * TB
        self.ctx_row = bl

    def mod_tiles(self, mods):
        rows = np.array([[self.ctx_row if r == 0 else b for b in range(self.bl)] for r in range(self.nr)], np.int32)
        t = mods[rows].reshape(self.nr, self.bl, 6, D)
        return jnp.transpose(t, (0, 2, 1, 3)).reshape(self.nr * 6, self.bl, 1, D)


ST_FWD, ST_BWD = 4, 2


def _tok_spec(lay, st):
    nc = lay.bl // st
    return pl.BlockSpec((st * TB, D), lambda c, r: (r * nc + c, 0))


def _vec_spec():
    return pl.BlockSpec((1, D), lambda c, r: (0, 0))


def _mod_spec(st, k):
    return pl.BlockSpec((1, st, 1, D), lambda c, r: (r * 6 + k, c, 0, 0))


def _x_spec(lay, st):
    return pl.BlockSpec((st, 1, TB, D), lambda c, r: (c, jnp.maximum(r - 1, 0), 0, 0))


def _rows3(ref_or_val, st):
    return ref_or_val.reshape(st, TB, D)


def _acc_rows(acc_ref, val3, st, ctx_row):
    c, r = pl.program_id(0), pl.program_id(1)
    s = jnp.sum(val3, axis=1, keepdims=True)

    @pl.when(r == 0)
    def _():
        acc_ref[ctx_row:ctx_row + 1] += jnp.sum(s, axis=0, keepdims=True)

    @pl.when(r > 0)
    def _():
        acc_ref[pl.ds(c * st, st)] += s


def _first_step():
    return jnp.logical_and(pl.program_id(0) == 0, pl.program_id(1) == 0)


def _embed(lay, x, ctx, pe, side=None):
    st = ST_FWD
    bl, nlb = lay.bl, lay.nlb
    nc = bl // st

    def body(*refs):
        (x_ref, c_ref, pe_ref, o_ref), srefs = _side_split(refs, 3, 1, 0, side)
        c, r = pl.program_id(0), pl.program_id(1)
        _side_start(side, srefs, jnp.logical_and(c == 0, r == 0))

        @pl.when(r == 0)
        def _():
            o_ref[...] = c_ref[...].reshape(st * TB, D)

        @pl.when(r > 0)
        def _():
            o_ref[...] = (x_ref[...].reshape(st, TB, D) + pe_ref[...]).reshape(st * TB, D)

        _side_wait(side, srefs, jnp.logical_and(c == nc - 1, r == lay.nr - 1))

    sd = side if side is not None else _Side([])
    outs = pl.pallas_call(
        body, grid=(nc, lay.nr),
        in_specs=[_x_spec(lay, st), pl.BlockSpec((st, CTX, D), lambda c, r: (c, 0, 0)),
                  pl.BlockSpec((1, TB, D), lambda c, r: (jnp.maximum(r - 1, 0), 0, 0))] + sd.in_specs,
        out_specs=[_tok_spec(lay, st)] + sd.out_specs,
        out_shape=[jax.ShapeDtypeStruct((lay.nt, D), F32)] + sd.out_shape,
        scratch_shapes=sd.scratch if side is not None else [],
        compiler_params=_cp(("arbitrary", "arbitrary") if side is not None else ("parallel", "parallel")),
        name="embed")(x.reshape(bl, nlb, TB, D), ctx, pe.reshape(nlb, TB, D), *sd.arrays)
    return outs[0], list(outs[1:])


def _normmod_fwd(lay, x, gain, modt, ksh, ksc, name):
    st = ST_FWD

    def body(x_ref, g_ref, sh_ref, sc_ref, o_ref):
        xv = _rows3(x_ref[...], st)
        r = lax.rsqrt(jnp.mean(xv * xv, axis=-1, keepdims=True) + EPS)
        o_ref[...] = ((xv * r * g_ref[...]) * (1.0 + sc_ref[0]) + sh_ref[0]).reshape(st * TB, D).astype(o_ref.dtype)

    return pl.pallas_call(
        body, grid=(lay.bl // st, lay.nr),
        in_specs=[_tok_spec(lay, st), _vec_spec(), _mod_spec(st, ksh), _mod_spec(st, ksc)],
        out_specs=_tok_spec(lay, st), out_shape=jax.ShapeDtypeStruct((lay.nt, D), MXU_DTYPE),
        compiler_params=_cp(("parallel", "parallel")), name=name)(x, gain, modt, modt)


def _acc_out():
    return pl.BlockSpec((8, 1, D), lambda c, r: (0, 0, 0)), jax.ShapeDtypeStruct((8, 1, D), F32)


def _normmod_bwd(lay, x, dh, dx_in, gain, modt, ksc, name, latent_only=False):
    st = ST_BWD
    acc_spec, acc_shape = _acc_out()
    if latent_only:
        dx_spec, dx_shape = _x_spec(lay, st), jax.ShapeDtypeStruct((lay.bl, lay.nlb, TB, D), F32)
    else:
        dx_spec, dx_shape = _tok_spec(lay, st), jax.ShapeDtypeStruct((lay.nt, D), F32)

    def body(x_ref, dh_ref, dxi_ref, g_ref, sc_ref, dx_ref, dsh_ref, dsc_ref, dg_ref):
        xv = _rows3(x_ref[...], st)
        dhv = _rows3(dh_ref[...].astype(F32), st)
        g = g_ref[...]
        sc1 = 1.0 + sc_ref[0]
        r = lax.rsqrt(jnp.mean(xv * xv, axis=-1, keepdims=True) + EPS)
        xh = xv * r
        dxh = dhv * (g * sc1)
        dx = _rows3(dxi_ref[...], st) + r * (dxh - xh * jnp.mean(dxh * xh, axis=-1, keepdims=True))
        dx_ref[...] = dx.reshape(dx_ref.shape)

        @pl.when(_first_step())
        def _():
            dsh_ref[...] = jnp.zeros_like(dsh_ref)
            dsc_ref[...] = jnp.zeros_like(dsc_ref)
            dg_ref[...] = jnp.zeros_like(dg_ref)

        _acc_rows(dsh_ref, dhv, st, lay.ctx_row)
        _acc_rows(dsc_ref, dhv * (xh * g), st, lay.ctx_row)
        dg_ref[...] += jnp.sum((dhv * sc1 * xh).reshape(st * TB, D), axis=0, keepdims=True)

    return pl.pallas_call(
        body, grid=(lay.bl // st, lay.nr),
        in_specs=[_tok_spec(lay, st), _tok_spec(lay, st), _tok_spec(lay, st), _vec_spec(), _mod_spec(st, ksc)],
        out_specs=[dx_spec, acc_spec, acc_spec, _vec_spec()],
        out_shape=[dx_shape, acc_shape, acc_shape, jax.ShapeDtypeStruct((1, D), F32)],
        compiler_params=_cp(("arbitrary", "arbitrary")), name=name)(x, dh, dx_in, gain, modt)


def _resnorm_fwd(lay, x, m, gain, modt, kgate, name):
    st = ST_FWD

    def body(x_ref, m_ref, g_ref, gate_ref, o_ref):
        mv = _rows3(m_ref[...].astype(F32), st)
        r = lax.rsqrt(jnp.mean(mv * mv, axis=-1, keepdims=True) + EPS)
        o_ref[...] = x_ref[...] + (gate_ref[0] * (mv * r * g_ref[...])).reshape(st * TB, D)

    return pl.pallas_call(
        body, grid=(lay.bl // st, lay.nr),
        in_specs=[_tok_spec(lay, st), _tok_spec(lay, st), _vec_spec(), _mod_spec(st, kgate)],
        out_specs=_tok_spec(lay, st), out_shape=jax.ShapeDtypeStruct((lay.nt, D), F32),
        compiler_params=_cp(("parallel", "parallel")), name=name)(x, m, gain, modt)


def _resnorm_bwd(lay, dxn, m, gain, modt, kgate, name):
    st = ST_BWD
    acc_spec, acc_shape = _acc_out()

    def body(d_ref, m_ref, g_ref, gate_ref, dm_ref, dgate_ref, dg_ref):
        dv = _rows3(d_ref[...], st)
        mv = _rows3(m_ref[...].astype(F32), st)
        g = g_ref[...]
        r = lax.rsqrt(jnp.mean(mv * mv, axis=-1, keepdims=True) + EPS)
        xh = mv * r
        dy = dv * gate_ref[0]
        dxh = dy * g
        dm = r * (dxh - xh * jnp.mean(dxh * xh, axis=-1, keepdims=True))
        dm_ref[...] = dm.reshape(st * TB, D).astype(dm_ref.dtype)

        @pl.when(_first_step())
        def _():
            dgate_ref[...] = jnp.zeros_like(dgate_ref)
            dg_ref[...] = jnp.zeros_like(dg_ref)

        _acc_rows(dgate_ref, dv * (xh * g), st, lay.ctx_row)
        dg_ref[...] += jnp.sum((dy * xh).reshape(st * TB, D), axis=0, keepdims=True)

    return pl.pallas_call(
        body, grid=(lay.bl // st, lay.nr),
        in_specs=[_tok_spec(lay, st), _tok_spec(lay, st), _vec_spec(), _mod_spec(st, kgate)],
        out_specs=[_tok_spec(lay, st), acc_spec, _vec_spec()],
        out_shape=[jax.ShapeDtypeStruct((lay.nt, D), MXU_DTYPE), acc_shape, jax.ShapeDtypeStruct((1, D), F32)],
        compiler_params=_cp(("arbitrary", "arbitrary")), name=name)(dxn, m, gain, modt)


def _rms(v):
    return lax.rsqrt(jnp.mean(v * v, axis=-1, keepdims=True) + EPS)


def _resnorm_normmod_fwd(lay, x, m, gpost, gpre, modt, kgate, ksh, ksc, name):
    st = ST_BWD

    def body(x_ref, m_ref, gp_ref, gq_ref, gate_ref, sh_ref, sc_ref, x1_ref, h_ref):
        mv = _rows3(m_ref[...].astype(F32), st)
        x1 = _rows3(x_ref[...], st) + gate_ref[0] * (mv * _rms(mv) * gp_ref[...])
        x1_ref[...] = x1.reshape(st * TB, D)
        h = (x1 * _rms(x1) * gq_ref[...]) * (1.0 + sc_ref[0]) + sh_ref[0]
        h_ref[...] = h.reshape(st * TB, D).astype(h_ref.dtype)

    tok = _tok_spec(lay, st)
    return pl.pallas_call(
        body, grid=(lay.bl // st, lay.nr),
        in_specs=[tok, tok, _vec_spec(), _vec_spec(), _mod_spec(st, kgate), _mod_spec(st, ksh), _mod_spec(st, ksc)],
        out_specs=[tok, tok],
        out_shape=[jax.ShapeDtypeStruct((lay.nt, D), F32), jax.ShapeDtypeStruct((lay.nt, D), MXU_DTYPE)],
        compiler_params=_cp(("parallel", "parallel")), name=name)(x, m, gpost, gpre, modt, modt, modt)


def _normmod_resnorm_bwd(lay, x1, dh, dx_in, gpre, m, gpost, modt, ksc, kgate, name):
    st = ST_BWD
    acc_spec, acc_shape = _acc_out()

    def body(x_ref, dh_ref, dxi_ref, gq_ref, sc_ref, m_ref, gp_ref, gate_ref,
             dx_ref, dm_ref, dsh_ref, dsc_ref, dgq_ref, dgate_ref, dgp_ref):
        xv = _rows3(x_ref[...], st)
        dhv = _rows3(dh_ref[...].astype(F32), st)
        gq = gq_ref[...]
        sc1 = 1.0 + sc_ref[0]
        r = _rms(xv)
        xh = xv * r
        dxh = dhv * (gq * sc1)
        dx1 = _rows3(dxi_ref[...], st) + r * (dxh - xh * jnp.mean(dxh * xh, axis=-1, keepdims=True))
        dx_ref[...] = dx1.reshape(st * TB, D)
        mv = _rows3(m_ref[...].astype(F32), st)
        gp = gp_ref[...]
        rm = _rms(mv)
        mh = mv * rm
        dy = dx1 * gate_ref[0]
        dmh = dy * gp
        dm = rm * (dmh - mh * jnp.mean(dmh * mh, axis=-1, keepdims=True))
        dm_ref[...] = dm.reshape(st * TB, D).astype(dm_ref.dtype)

        @pl.when(_first_step())
        def _():
            for ref in (dsh_ref, dsc_ref, dgq_ref, dgate_ref, dgp_ref):
                ref[...] = jnp.zeros_like(ref)

        _acc_rows(dsh_ref, dhv, st, lay.ctx_row)
        _acc_rows(dsc_ref, dhv * (xh * gq), st, lay.ctx_row)
        dgq_ref[...] += jnp.sum((dhv * sc1 * xh).reshape(st * TB, D), axis=0, keepdims=True)
        _acc_rows(dgate_ref, dx1 * (mh * gp), st, lay.ctx_row)
        dgp_ref[...] += jnp.sum((dy * mh).reshape(st * TB, D), axis=0, keepdims=True)

    tok = _tok_spec(lay, st)
    vsh = jax.ShapeDtypeStruct((1, D), F32)
    return pl.pallas_call(
        body, grid=(lay.bl // st, lay.nr),
        in_specs=[tok, tok, tok, _vec_spec(), _mod_spec(st, ksc), tok, _vec_spec(), _mod_spec(st, kgate)],
        out_specs=[tok, tok, acc_spec, acc_spec, _vec_spec(), acc_spec, _vec_spec()],
        out_shape=[jax.ShapeDtypeStruct((lay.nt, D), F32), jax.ShapeDtypeStruct((lay.nt, D), MXU_DTYPE),
                   acc_shape, acc_shape, vsh, acc_shape, vsh],
        compiler_params=_cp(("arbitrary", "arbitrary")), name=name)(x1, dh, dx_in, gpre, modt, m, gpost, modt)


def _resnorm_loss(lay, x, f, gain, modt, kgate, tgt):
    st = ST_BWD

    def body(x_ref, f_ref, g_ref, gate_ref, t_ref, dx_ref, l_ref):
        r = pl.program_id(1)

        @pl.when(_first_step())
        def _():
            l_ref[...] = jnp.zeros_like(l_ref)

        @pl.when(r == 0)
        def _():
            dx_ref[...] = jnp.zeros_like(dx_ref)

        @pl.when(r > 0)
        def _():
            fv = _rows3(f_ref[...].astype(F32), st)
            y = _rows3(x_ref[...], st) + gate_ref[0] * (fv * _rms(fv) * g_ref[...])
            e = y - t_ref[...].reshape(st, TB, D)
            dx_ref[...] = (e * (1.0 / D)).reshape(st * TB, D)
            l_ref[...] += jnp.sum(e * e) * (0.5 / D)

    tok = _tok_spec(lay, st)
    return pl.pallas_call(
        body, grid=(lay.bl // st, lay.nr),
        in_specs=[tok, tok, _vec_spec(), _mod_spec(st, kgate), _x_spec(lay, st)],
        out_specs=[tok, pl.BlockSpec((8, 128), lambda c, r: (0, 0))],
        out_shape=[jax.ShapeDtypeStruct((lay.nt, D), F32), jax.ShapeDtypeStruct((8, 128), F32)],
        compiler_params=_cp(("arbitrary", "arbitrary")), name="loss")(
            x, f, gain, modt, tgt.reshape(lay.bl, lay.nlb, TB, D))


FF_TN = D_FF // 2
FF_CHUNKS = ((0, 512), (512, 512), (1024, 384))


def _ffn_up(h, wgt, wut, name, side=None):
    m = h.shape[0]
    tm, tn = min(512, m), FF_TN
    ni, nj = m // tm, D_FF // tn

    def body(*refs):
        (h_ref, wg_ref, wu_ref, g_ref, u_ref, a_ref), srefs = _side_split(refs, 3, 3, 0, side)
        j, i = pl.program_id(0), pl.program_id(1)
        _side_start(side, srefs, jnp.logical_and(i == 0, j == 0))
        hv = h_ref[...]
        for c0, cw in FF_CHUNKS:
            g = _nt(hv, wg_ref[c0:c0 + cw, :])
            u = _nt(hv, wu_ref[c0:c0 + cw, :])
            g_ref[:, c0:c0 + cw] = g.astype(g_ref.dtype)
            u_ref[:, c0:c0 + cw] = u.astype(u_ref.dtype)
            a_ref[:, c0:c0 + cw] = (g * _sigmoid(g) * u).astype(a_ref.dtype)
        _side_wait(side, srefs, jnp.logical_and(i == ni - 1, j == nj - 1))

    sd = side if side is not None else _Side([])
    osp = pl.BlockSpec((tm, tn), lambda j, i: (i, j))
    osh = jax.ShapeDtypeStruct((m, D_FF), MXU_DTYPE)
    outs = pl.pallas_call(
        body, grid=(nj, ni),
        in_specs=[pl.BlockSpec((tm, D), lambda j, i: (i, 0)), pl.BlockSpec((tn, D), lambda j, i: (j, 0)),
                  pl.BlockSpec((tn, D), lambda j, i: (j, 0))] + sd.in_specs,
        out_specs=[osp, osp, osp] + sd.out_specs, out_shape=[osh, osh, osh] + sd.out_shape,
        scratch_shapes=sd.scratch if side is not None else [],
        compiler_params=_cp(("arbitrary", "arbitrary") if side is not None else ("parallel", "parallel")),
        name=name)(h, wgt, wut, *sd.arrays)
    return outs[0], outs[1], outs[2], list(outs[3:])


def _ffn_down_bwd(df, wd, g, u, name, side=None):
    m = df.shape[0]
    tm, tn = min(512, m), FF_TN
    ni, nj = m // tm, D_FF // tn

    def body(*refs):
        (df_ref, wd_ref, g_ref, u_ref, dg_ref, du_ref), srefs = _side_split(refs, 4, 2, 0, side)
        j, i = pl.program_id(0), pl.program_id(1)
        _side_start(side, srefs, jnp.logical_and(i == 0, j == 0))
        dfv = df_ref[...]
        for c0, cw in FF_CHUNKS:
            da = _nt(dfv, wd_ref[c0:c0 + cw, :])
            gv = g_ref[:, c0:c0 + cw].astype(F32)
            uv = u_ref[:, c0:c0 + cw].astype(F32)
            s = _sigmoid(gv)
            dg_ref[:, c0:c0 + cw] = (da * uv * (s * (1.0 + gv * (1.0 - s)))).astype(dg_ref.dtype)
            du_ref[:, c0:c0 + cw] = (da * gv * s).astype(du_ref.dtype)
        _side_wait(side, srefs, jnp.logical_and(i == ni - 1, j == nj - 1))

    sd = side if side is not None else _Side([])
    osp = pl.BlockSpec((tm, tn), lambda j, i: (i, j))
    osh = jax.ShapeDtypeStruct((m, D_FF), MXU_DTYPE)
    outs = pl.pallas_call(
        body, grid=(nj, ni),
        in_specs=[pl.BlockSpec((tm, D), lambda j, i: (i, 0)), pl.BlockSpec((tn, D), lambda j, i: (j, 0)), osp, osp]
        + sd.in_specs,
        out_specs=[osp, osp] + sd.out_specs, out_shape=[osh, osh] + sd.out_shape,
        scratch_shapes=sd.scratch if side is not None else [],
        compiler_params=_cp(("arbitrary", "arbitrary") if side is not None else ("parallel", "parallel")),
        name=name)(df, wd, g, u, *sd.arrays)
    return outs[0], outs[1], list(outs[2:])


def _head_masks(shape):
    lane = lax.broadcasted_iota(jnp.int32, shape, 1)
    return [jnp.logical_and(lane >= 64 * h, lane < 64 * h + 64) for h in range(4)]


def _head_mean(x, masks):
    out = jnp.zeros_like(x)
    for mk in masks:
        s = jnp.sum(jnp.where(mk, x, 0.0), axis=-1, keepdims=True) * (1.0 / 64.0)
        out = jnp.where(mk, s, out)
    return out


def _gate_common(z, masks):
    zg = _gelu(z)
    u = zg[:, :A_W]
    v = zg[:, A_W:]
    mu = _head_mean(v, masks)
    vc = v - mu
    rstd = lax.rsqrt(_head_mean(vc * vc, masks) + EPS)
    return u, vc * rstd, rstd


def _gate_s(vn, ws_ref, bias, masks):
    parts = []
    for c in range(TB // CHUNK):
        vc = vn[c * CHUNK:(c + 1) * CHUNK]
        s = bias
        for h in range(4):
            s = s + _nn(ws_ref[h], jnp.where(masks[h][:CHUNK], vc, 0.0).astype(MXU_DTYPE))
        parts.append(s)
    return jnp.concatenate(parts, axis=0)


MT = 4


def _blocks():
    return [pl.ds(s * TB, TB) for s in range(MT)]


def _gate_fwd(lay, z, ws, bias, name):
    def body(z_ref, ws_ref, b_ref, o_ref):
        masks = _head_masks((TB, A_W))
        for sl in _blocks():
            u, vn, _ = _gate_common(z_ref[sl, :].astype(F32), masks)
            o_ref[sl, :] = (u * _gate_s(vn, ws_ref, b_ref[...], masks)).astype(o_ref.dtype)

    return pl.pallas_call(
        body, grid=(lay.nb // MT,),
        in_specs=[pl.BlockSpec((MT * TB, 2 * A_W), lambda j: (j, 0)), pl.BlockSpec((4, CHUNK, CHUNK), lambda j: (0, 0, 0)),
                  pl.BlockSpec((CHUNK, A_W), lambda j: (0, 0))],
        out_specs=pl.BlockSpec((MT * TB, A_W), lambda j: (j, 0)),
        out_shape=jax.ShapeDtypeStruct((lay.nt, A_W), MXU_DTYPE),
        compiler_params=_cp(("parallel",)), name=name)(z, ws, bias)


def _gate_bwd(lay, z, da, ws, wst, bias, name):
    def body(z_ref, da_ref, ws_ref, wst_ref, b_ref, dz_ref, dws_ref, db_ref):
        j = pl.program_id(0)

        @pl.when(j == 0)
        def _():
            dws_ref[...] = jnp.zeros_like(dws_ref)
            db_ref[...] = jnp.zeros_like(db_ref)

        masks = _head_masks((TB, A_W))
        for blk in _blocks():
            zv = z_ref[blk, :].astype(F32)
            u, vn, rstd = _gate_common(zv, masks)
            s = _gate_s(vn, ws_ref, b_ref[...], masks)
            dav = da_ref[blk, :].astype(F32)
            du = dav * s
            ds = dav * u
            dvn_parts = []
            for c in range(TB // CHUNK):
                sl = slice(c * CHUNK, (c + 1) * CHUNK)
                ds_c = ds[sl]
                vn_c = vn[sl].astype(MXU_DTYPE)
                db_ref[...] += ds_c
                ds_b = ds_c.astype(MXU_DTYPE)
                dvn_c = jnp.zeros((CHUNK, A_W), F32)
                for h in range(4):
                    mk = masks[h][:CHUNK]
                    dws_ref[h] += _nt(jnp.where(mk, ds_c, 0.0).astype(MXU_DTYPE), vn_c)
                    dvn_c = dvn_c + jnp.where(mk, _nn(wst_ref[h], ds_b), 0.0)
                dvn_parts.append(dvn_c)
            dvn = jnp.concatenate(dvn_parts, axis=0)
            dv = rstd * (dvn - _head_mean(dvn, masks) - vn * _head_mean(dvn * vn, masks))
            gg = _gelu_grad(zv)
            dz_ref[blk, :A_W] = (du * gg[:, :A_W]).astype(dz_ref.dtype)
            dz_ref[blk, A_W:] = (dv * gg[:, A_W:]).astype(dz_ref.dtype)

    return pl.pallas_call(
        body, grid=(lay.nb // MT,),
        in_specs=[pl.BlockSpec((MT * TB, 2 * A_W), lambda j: (j, 0)), pl.BlockSpec((MT * TB, A_W), lambda j: (j, MCAT_A)),
                  pl.BlockSpec((4, CHUNK, CHUNK), lambda j: (0, 0, 0)), pl.BlockSpec((4, CHUNK, CHUNK), lambda j: (0, 0, 0)),
                  pl.BlockSpec((CHUNK, A_W), lambda j: (0, 0))],
        out_specs=[pl.BlockSpec((MT * TB, 2 * A_W), lambda j: (j, 0)), pl.BlockSpec((4, CHUNK, CHUNK), lambda j: (0, 0, 0)),
                   pl.BlockSpec((CHUNK, A_W), lambda j: (0, 0))],
        out_shape=[jax.ShapeDtypeStruct((lay.nt, 2 * A_W), MXU_DTYPE), jax.ShapeDtypeStruct((4, CHUNK, CHUNK), F32),
                   jax.ShapeDtypeStruct((CHUNK, A_W), F32)],
        compiler_params=_cp(("arbitrary",)), name=name)(z, da, ws, wst, bias)


def _band_constants():
    bands = np.zeros((2, 4, TB, TB), np.float32)
    inv = np.zeros((2, 4, TB, 1), np.float32)
    for kind, n in ((0, GRID_W), (1, TB)):
        for i, w in enumerate(POOL_WINDOWS):
            for t in range(TB):
                base, tt = (t // n) * n, t % n
                lo = min(max(tt - w // 2, 0), n)
                hi = min(max(tt - w // 2 + w, 0), n)
                bands[kind, i, t, base + lo:base + hi] = 1.0
                inv[kind, i, t, 0] = 1.0 / (hi - lo)
    return bands, inv


def _split3(x):
    a = x.astype(MXU_DTYPE)
    r1 = x - a.astype(F32)
    b = r1.astype(MXU_DTYPE)
    c = (r1 - b.astype(F32)).astype(MXU_DTYPE)
    return a, b, c


def _window_apply(band_ref, inv_ref, x, masks, transpose, mxu_exact=False):
    out = jnp.zeros_like(x)
    for i in range(4):
        xi = x * inv_ref[0, i] if transpose else x
        acc = None
        for part in ((xi.astype(MXU_DTYPE),) if mxu_exact else _split3(xi)):
            r = _tn(band_ref[0, i], part) if transpose else _nn(band_ref[0, i], part)
            acc = r if acc is None else acc + r
        if not transpose:
            acc = acc * inv_ref[0, i]
        out = jnp.where(masks[i], acc, out)
    return out


def _pool_specs(lay):
    kind = lambda j: jnp.where(j < lay.nctx // MT, 1, 0)
    return [pl.BlockSpec((1, 4, TB, TB), lambda j: (kind(j), 0, 0, 0)), pl.BlockSpec((1, 4, TB, 1), lambda j: (kind(j), 0, 0, 0))]


def _pool_fwd(lay, z, bands, inv, pw, scale, name):
    def body(p_ref, band_ref, inv_ref, pw_ref, sc_ref, o_ref):
        masks = _head_masks((TB, C_W))
        for blk in _blocks():
            p = p_ref[blk, :].astype(F32)
            diff = _window_apply(band_ref, inv_ref, p, masks, False, mxu_exact=True) - p
            o_ref[blk, :] = (_nn(diff.astype(MXU_DTYPE), pw_ref[...]) * sc_ref[...]).astype(o_ref.dtype)

    return pl.pallas_call(
        body, grid=(lay.nb // MT,),
        in_specs=[pl.BlockSpec((MT * TB, C_W), lambda j: (j, 4))] + _pool_specs(lay)
        + [pl.BlockSpec((C_W, C_W), lambda j: (0, 0)), pl.BlockSpec((1, C_W), lambda j: (0, 0))],
        out_specs=pl.BlockSpec((MT * TB, C_W), lambda j: (j, 0)),
        out_shape=jax.ShapeDtypeStruct((lay.nt, C_W), MXU_DTYPE),
        compiler_params=_cp(("parallel",)), name=name)(z, bands, inv, pw, scale)


def _pool_bwd(lay, z, dc, bands, inv, pw, scale, name):
    def body(p_ref, dc_ref, band_ref, inv_ref, pw_ref, sc_ref, dp_ref, dpw_ref, dsc_ref):
        j = pl.program_id(0)

        @pl.when(j == 0)
        def _():
            dpw_ref[...] = jnp.zeros_like(dpw_ref)
            dsc_ref[...] = jnp.zeros_like(dsc_ref)

        masks = _head_masks((TB, C_W))
        for blk in _blocks():
            p = p_ref[blk, :].astype(F32)
            dcv = dc_ref[blk, :].astype(F32)
            diff = _window_apply(band_ref, inv_ref, p, masks, False, mxu_exact=True) - p
            diff_b = diff.astype(MXU_DTYPE)
            pre = _nn(diff_b, pw_ref[...])
            dsc_ref[...] += jnp.sum(dcv * pre, axis=0, keepdims=True)
            dpre = dcv * sc_ref[...]
            dpre_b = dpre.astype(MXU_DTYPE)
            dpw_ref[...] += _tn(diff_b, dpre_b)
            ddiff = _nt(dpre_b, pw_ref[...])
            dp_ref[blk, :] = (_window_apply(band_ref, inv_ref, ddiff, masks, True) - ddiff).astype(dp_ref.dtype)

    return pl.pallas_call(
        body, grid=(lay.nb // MT,),
        in_specs=[pl.BlockSpec((MT * TB, C_W), lambda j: (j, 4)), pl.BlockSpec((MT * TB, C_W), lambda j: (j, MCAT_C))]
        + _pool_specs(lay)
        + [pl.BlockSpec((C_W, C_W), lambda j: (0, 0)), pl.BlockSpec((1, C_W), lambda j: (0, 0))],
        out_specs=[pl.BlockSpec((MT * TB, C_W), lambda j: (j, 0)), pl.BlockSpec((C_W, C_W), lambda j: (0, 0)),
                   pl.BlockSpec((1, C_W), lambda j: (0, 0))],
        out_shape=[jax.ShapeDtypeStruct((lay.nt, C_W), MXU_DTYPE), jax.ShapeDtypeStruct((C_W, C_W), F32),
                   jax.ShapeDtypeStruct((1, C_W), F32)],
        compiler_params=_cp(("arbitrary",)), name=name)(z, dc, bands, inv, pw, scale)


def _disc_math(lr, li, ldt, br, bi):
    dt = jnp.exp(ldt)
    e = jnp.exp(lr * dt)
    ar = e * jnp.cos(li * dt)
    ai = e * jnp.sin(li * dt)
    nr, ni = ar - 1.0, ai
    den = lr * lr + li * li
    qr = (nr * lr + ni * li) / den
    qi = (ni * lr - nr * li) / den
    return ar, ai, qr * br - qi * bi, qr * bi + qi * br


def _disc_fwd(lrx, lix, ldtx, brt, bit, name):
    def body(lr_ref, li_ref, ldt_ref, br_ref, bi_ref, ar_ref, ai_ref, obr_ref, obi_ref):
        ar, ai, obr, obi = _disc_math(lr_ref[...], li_ref[...], ldt_ref[...], br_ref[...], bi_ref[...])
        ar_ref[...] = ar
        ai_ref[...] = ai
        obr_ref[...] = obr
        obi_ref[...] = obi

    sh = jax.ShapeDtypeStruct(lrx.shape, F32)
    return pl.pallas_call(body, out_shape=[sh, sh, sh, sh], name=name)(lrx, lix, ldtx, brt, bit)


def _disc_bwd(lrx, lix, ldtx, brt, bit, dar, dai, dbr, dbi, name):
    nrow = lrx.shape[0] // SSM_H

    def body(lr_ref, li_ref, ldt_ref, br_ref, bi_ref, dar_ref, dai_ref, dbr_ref, dbi_ref,
             glr_ref, gli_ref, gdt_ref, gbr_ref, gbi_ref):
        _, vjp = jax.vjp(_disc_math, lr_ref[...], li_ref[...], ldt_ref[...], br_ref[...], bi_ref[...])
        glr, gli, gdt, gbr, gbi = vjp((dar_ref[...], dai_ref[...], dbr_ref[...], dbi_ref[...]))
        glr_ref[...] = jnp.sum(glr.reshape(nrow, SSM_H, SSM_P), axis=1)
        gli_ref[...] = jnp.sum(gli.reshape(nrow, SSM_H, SSM_P), axis=1)
        gdt_ref[...] = jnp.sum(jnp.sum(gdt.reshape(nrow, SSM_H, SSM_P), axis=1), axis=-1, keepdims=True)
        gbr_ref[...] = gbr
        gbi_ref[...] = gbi

    small = jax.ShapeDtypeStruct((nrow, SSM_P), F32)
    big = jax.ShapeDtypeStruct(lrx.shape, F32)
    return pl.pallas_call(body, out_shape=[small, small, jax.ShapeDtypeStruct((nrow, 1), F32), big, big],
                          name=name)(lrx, lix, ldtx, brt, bit, dar, dai, dbr, dbi)


HS = 1024
GQ, QC, QS = 8, 128, 512
LC = QS
SCAN_UNROLL = ST


def _scan_steps(step, carry):
    if SCAN_UNROLL >= ST:
        for s in range(ST):
            carry = step(s, carry)
        return carry

    def body(i, c):
        for j in range(SCAN_UNROLL):
            c = step(i * SCAN_UNROLL + j, c)
        return c

    return lax.fori_loop(0, ST // SCAN_UNROLL, body, carry)


def _tile_row(s):
    return s * 8 if isinstance(s, int) else pl.multiple_of(s * 8, 8)


def _dir_cat(x, d0, qq):
    xq = x[:, QC * qq:QC * qq + QC]
    zero = jnp.zeros_like(xq)
    return jnp.concatenate([jnp.where(d0, xq, zero), jnp.where(d0, zero, xq)], axis=1)


def _dir_pick(x, d0):
    return jnp.where(d0, x[:, :QC], x[:, QC:])


def _d0_rows(n):
    row = lax.broadcasted_iota(jnp.int32, (n, 1), 0)
    return jnp.bitwise_and(row, 4) == 0


def _scan_perm(bl):
    n = 2 * bl * ST
    p = np.zeros((n, n), np.float32)
    for s in range(ST):
        for d in range(2):
            for b in range(bl):
                t = s if d == 0 else ST - 1 - s
                p[s * 2 * bl + d * bl + b, d * bl * ST + b * ST + t] = 1.0
    return p


def _scan_maps(lay):
    spc = TB // ST
    nlc = lay.nlb * spc

    def fwd(k):
        return k // spc, k % spc

    def rev(k):
        cpos = nlc - 1 - jnp.maximum(k - spc, 0)
        return jnp.where(k < spc, 0, 1 + cpos // spc), jnp.where(k < spc, spc - 1 - k, cpos % spc)

    return fwd, rev


def _pack_rows(f_ref, r_ref, p_ref, rc):
    st = jnp.concatenate([f_ref[0].reshape(rc // 2, 256), r_ref[0].reshape(rc // 2, 256)], axis=0).astype(MXU_DTYPE)
    return _nn(p_ref[...], st).astype(MXU_DTYPE)


def _side_split(refs, n_in, n_out, n_scr, side):
    ns = side.n if side is not None else 0
    ins, sin = refs[:n_in], refs[n_in:n_in + ns]
    o0 = n_in + ns
    outs, sout = refs[o0:o0 + n_out], refs[o0 + n_out:o0 + n_out + ns]
    s0 = o0 + n_out + ns
    return ins + outs + refs[s0:s0 + n_scr], (sin, sout, refs[s0 + n_scr:])


def _side_start(side, srefs, first):
    if side is not None:
        @pl.when(first)
        def _():
            side.start(*srefs)


def _side_wait(side, srefs, last):
    if side is not None:
        @pl.when(last)
        def _():
            side.wait(*srefs)


def _ssm_fwd(lay, z, perm, bh, ch, ar8, ai8, name, side=None):
    bl = lay.bl
    rc = ST * 2 * bl
    nch = lay.nr * (TB // ST)
    fwd, rev = _scan_maps(lay)
    z4 = z.reshape(lay.nr, bl, TB, z.shape[1])

    def body(*refs):
        own, srefs = _side_split(refs, 7, 4, 2, side)
        uf_ref, ur_ref, p_ref, bh_ref, ch_ref, ar_ref, ai_ref, yf_ref, yr_ref, hst_ref, hsv_ref, hs, hc = own
        f, k = pl.program_id(0), pl.program_id(1)
        _side_start(side, srefs, jnp.logical_and(f == 0, k == 0))

        @pl.when(k == 0)
        def _():
            hc[...] = jnp.zeros_like(hc)

        hst_ref[0] = hc[...]
        d0 = _d0_rows(rc)
        uv = _pack_rows(uf_ref, ur_ref, p_ref, rc)
        for q in range(2):
            cr, ci = 2 * QS * q, 2 * QS * q + QS
            hs[:, cr:cr + 2 * QS] = _nn(_dir_cat(uv, d0, q), bh_ref[q])
            ar = ar_ref[:, QS * q:QS * q + QS]
            ai = ai_ref[:, QS * q:QS * q + QS]

            def step(s, carry, cr=cr, ci=ci, ar=ar, ai=ai):
                hr, hi = carry
                base = _tile_row(s)
                nr = ar * hr - ai * hi + hs[pl.ds(base, 8), cr:cr + LC]
                ni = ar * hi + ai * hr + hs[pl.ds(base, 8), ci:ci + LC]
                hs[pl.ds(base, 8), cr:cr + LC] = nr
                hs[pl.ds(base, 8), ci:ci + LC] = ni
                return nr, ni

            hr, hi = _scan_steps(step, (hc[:, cr:cr + LC], hc[:, ci:ci + LC]))
            hc[:, cr:cr + LC] = hr
            hc[:, ci:ci + LC] = hi
        hsv_ref[0] = hs[...].astype(hsv_ref.dtype)
        yi = jnp.concatenate(
            [_dir_pick(_nn(hsv_ref[0, :, 2 * QS * q:2 * QS * (q + 1)], ch_ref[q]), d0) for q in range(2)], axis=1)
        yd = _tn(p_ref[...], yi.astype(MXU_DTYPE))
        yf_ref[0] = yd[:rc // 2].reshape(bl, ST, 256).astype(yf_ref.dtype)
        yr_ref[0] = yd[rc // 2:].reshape(bl, ST, 256).astype(yr_ref.dtype)
        _side_wait(side, srefs, jnp.logical_and(f == 1, k == nch - 1))

    sd = side if side is not None else _Side([])
    blk = (1, bl, ST, 256)
    ysh = jax.ShapeDtypeStruct((lay.nr, bl, TB, B_W), MXU_DTYPE)
    outs = pl.pallas_call(
        body, grid=(2, nch),
        in_specs=[pl.BlockSpec(blk, lambda f, k: (fwd(k)[0], 0, fwd(k)[1], 2 + f)),
                  pl.BlockSpec(blk, lambda f, k: (rev(k)[0], 0, rev(k)[1], 2 + f)),
                  pl.BlockSpec((rc, rc), lambda f, k: (0, 0)),
                  pl.BlockSpec((2, 2 * QC, 2 * QS), lambda f, k: (f, 0, 0)),
                  pl.BlockSpec((2, 2 * QS, 2 * QC), lambda f, k: (f, 0, 0)),
                  pl.BlockSpec((8, HS), lambda f, k: (0, f)), pl.BlockSpec((8, HS), lambda f, k: (0, f))] + sd.in_specs,
        out_specs=[pl.BlockSpec(blk, lambda f, k: (fwd(k)[0], 0, fwd(k)[1], f)),
                   pl.BlockSpec(blk, lambda f, k: (rev(k)[0], 0, rev(k)[1], f)),
                   pl.BlockSpec((1, 8, 2 * HS), lambda f, k: (k, 0, f)),
                   pl.BlockSpec((1, rc, 2 * HS), lambda f, k: (k, 0, f))] + sd.out_specs,
        out_shape=[ysh, ysh, jax.ShapeDtypeStruct((nch, 8, 4 * HS), F32),
                   jax.ShapeDtypeStruct((nch, rc, 4 * HS), MXU_DTYPE)] + sd.out_shape,
        scratch_shapes=[pltpu.VMEM((rc, 2 * HS), F32), pltpu.VMEM((8, 2 * HS), F32)] + (sd.scratch if side is not None else []),
        compiler_params=_cp(("arbitrary", "arbitrary")), name=name)(z4, z4, perm, bh, ch, ar8, ai8, *sd.arrays)
    yf, yr, hst, hsv = outs[:4]
    return yf.reshape(lay.nt, B_W), yr.reshape(lay.nt, B_W), (hst, hsv), list(outs[4:])


def _ssm_bwd(lay, z, dy, perm, hst, bh, ch, ar8, ai8, name, side=None):
    bl = lay.bl
    rc = ST * 2 * bl
    nch = lay.nr * (TB // ST)
    fwd, rev = _scan_maps(lay)
    z4 = z.reshape(lay.nr, bl, TB, z.shape[1])
    dy4 = dy.reshape(lay.nr, bl, TB, B_W)

    hst, hsv = hst

    def body(*refs):
        own, srefs = _side_split(refs, 11, 6, 5, side)
        (uf_ref, ur_ref, dyf_ref, dyr_ref, p_ref, hst_ref, hsv_ref, bh_ref, ch_ref, ar_ref, ai_ref,
         duf_ref, dur_ref, dbh_ref, dch_ref, dar_ref, dai_ref, hs, es, ec, accr, acci) = own
        f, k = pl.program_id(0), pl.program_id(1)
        _side_start(side, srefs, jnp.logical_and(f == 0, k == 0))

        @pl.when(k == 0)
        def _():
            ec[...] = jnp.zeros_like(ec)
            accr[...] = jnp.zeros_like(accr)
            acci[...] = jnp.zeros_like(acci)
            dbh_ref[...] = jnp.zeros_like(dbh_ref)
            dch_ref[...] = jnp.zeros_like(dch_ref)

        d0 = _d0_rows(rc)
        uv = _pack_rows(uf_ref, ur_ref, p_ref, rc)
        dyv = _pack_rows(dyf_ref, dyr_ref, p_ref, rc)

        hs[0:8, :] = hst_ref[0]
        hs[8:, :] = hsv_ref[0].astype(F32)
        ucat, dycat = [], []
        for q in range(2):
            cr = 2 * QS * q
            ucat.append(_dir_cat(uv, d0, q))
            dycat.append(_dir_cat(dyv, d0, q))
            dch_ref[q] += _tn(hsv_ref[0, :, cr:cr + 2 * QS], dycat[q])
            es[:, cr:cr + 2 * QS] = _nt(dycat[q], ch_ref[q])

        dui = []
        for q in range(2):
            cr, ci = 2 * QS * q, 2 * QS * q + QS
            ar = ar_ref[:, QS * q:QS * q + QS]
            ai = ai_ref[:, QS * q:QS * q + QS]

            def bstep(i, carry, cr=cr, ci=ci, ar=ar, ai=ai):
                er, ei, sr, si = carry
                base = _tile_row(ST - 1 - i)
                ner = es[pl.ds(base, 8), cr:cr + LC] + ar * er + ai * ei
                nei = es[pl.ds(base, 8), ci:ci + LC] - ai * er + ar * ei
                es[pl.ds(base, 8), cr:cr + LC] = ner
                es[pl.ds(base, 8), ci:ci + LC] = nei
                hpr = hs[pl.ds(base, 8), cr:cr + LC]
                hpi = hs[pl.ds(base, 8), ci:ci + LC]
                return ner, nei, sr + ner * hpr + nei * hpi, si - ner * hpi + nei * hpr

            lo = QS * q
            er, ei, sr, si = _scan_steps(
                bstep, (ec[:, cr:cr + LC], ec[:, ci:ci + LC], accr[:, lo:lo + LC], acci[:, lo:lo + LC]))
            ec[:, cr:cr + LC] = er
            ec[:, ci:ci + LC] = ei
            accr[:, lo:lo + LC] = sr
            acci[:, lo:lo + LC] = si
            eb = es[:, cr:cr + 2 * QS].astype(MXU_DTYPE)
            dui.append(_dir_pick(_nt(eb, bh_ref[q]), d0))
            dbh_ref[q] += _tn(ucat[q], eb)

        dud = _tn(p_ref[...], jnp.concatenate(dui, axis=1).astype(MXU_DTYPE))
        duf_ref[0] = dud[:rc // 2].reshape(bl, ST, 256).astype(duf_ref.dtype)
        dur_ref[0] = dud[rc // 2:].reshape(bl, ST, 256).astype(dur_ref.dtype)

        @pl.when(k == nch - 1)
        def _():
            for d in range(2):
                dar_ref[d:d + 1, :] = jnp.sum(accr[4 * d:4 * d + 4, :], axis=0, keepdims=True)
                dai_ref[d:d + 1, :] = jnp.sum(acci[4 * d:4 * d + 4, :], axis=0, keepdims=True)

        _side_wait(side, srefs, jnp.logical_and(f == 1, k == nch - 1))

    sd = side if side is not None else _Side([])
    last = lambda k: nch - 1 - k
    blk = (1, bl, ST, 256)
    fspec = lambda c0: pl.BlockSpec(blk, lambda f, k: (fwd(last(k))[0], 0, fwd(last(k))[1], c0 + f))
    rspec = lambda c0: pl.BlockSpec(blk, lambda f, k: (rev(last(k))[0], 0, rev(last(k))[1], c0 + f))
    dush = jax.ShapeDtypeStruct((lay.nr, bl, TB, B_W), MXU_DTYPE)
    outs = pl.pallas_call(
        body, grid=(2, nch),
        in_specs=[fspec(2), rspec(2), fspec(0), rspec(0),
                  pl.BlockSpec((rc, rc), lambda f, k: (0, 0)),
                  pl.BlockSpec((1, 8, 2 * HS), lambda f, k: (last(k), 0, f)),
                  pl.BlockSpec((1, rc, 2 * HS), lambda f, k: (last(k), 0, f)),
                  pl.BlockSpec((2, 2 * QC, 2 * QS), lambda f, k: (f, 0, 0)),
                  pl.BlockSpec((2, 2 * QS, 2 * QC), lambda f, k: (f, 0, 0)),
                  pl.BlockSpec((8, HS), lambda f, k: (0, f)), pl.BlockSpec((8, HS), lambda f, k: (0, f))] + sd.in_specs,
        out_specs=[fspec(0), rspec(0),
                   pl.BlockSpec((2, 2 * QC, 2 * QS), lambda f, k: (f, 0, 0)),
                   pl.BlockSpec((2, 2 * QS, 2 * QC), lambda f, k: (f, 0, 0)),
                   pl.BlockSpec((2, HS), lambda f, k: (0, f)), pl.BlockSpec((2, HS), lambda f, k: (0, f))] + sd.out_specs,
        out_shape=[dush, dush, jax.ShapeDtypeStruct((4, 2 * QC, 2 * QS), F32),
                   jax.ShapeDtypeStruct((4, 2 * QS, 2 * QC), F32), jax.ShapeDtypeStruct((2, 2 * HS), F32),
                   jax.ShapeDtypeStruct((2, 2 * HS), F32)] + sd.out_shape,
        scratch_shapes=[pltpu.VMEM((rc + 8, 2 * HS), F32), pltpu.VMEM((rc, 2 * HS), F32), pltpu.VMEM((8, 2 * HS), F32),
                        pltpu.VMEM((8, HS), F32), pltpu.VMEM((8, HS), F32)] + (sd.scratch if side is not None else []),
        compiler_params=_cp(("arbitrary", "arbitrary")), name=name)(
            z4, z4, dy4, dy4, perm, hst, hsv, bh, ch, ar8, ai8, *sd.arrays)
    duf, dur, dbh, dch, dar, dai = outs[:6]
    return duf.reshape(lay.nt, B_W), dur.reshape(lay.nt, B_W), dbh, dch, dar, dai, list(outs[6:])


def _glu_fwd(lay, z, yf, yr, dvec, wglu, bglu, name):
    def body(u_ref, yf_ref, yr_ref, d_ref, w_ref, b_ref, o_ref, y_ref):
        y = yf_ref[...].astype(F32) + yr_ref[...].astype(F32) + d_ref[...] * u_ref[...].astype(F32)
        y_ref[...] = y
        g = _gelu(y)
        pre = _nn(g.astype(MXU_DTYPE), w_ref[...]) + b_ref[...]
        o_ref[...] = (g * _sigmoid(pre)).astype(o_ref.dtype)

    tok = pl.BlockSpec((MT * TB, B_W), lambda j: (j, 0))
    vec = pl.BlockSpec((1, B_W), lambda j: (0, 0))
    return pl.pallas_call(
        body, grid=(lay.nb // MT,),
        in_specs=[pl.BlockSpec((MT * TB, B_W), lambda j: (j, 1)), tok, tok, vec,
                  pl.BlockSpec((B_W, B_W), lambda j: (0, 0)), vec],
        out_specs=[tok, tok],
        out_shape=[jax.ShapeDtypeStruct((lay.nt, B_W), MXU_DTYPE), jax.ShapeDtypeStruct((lay.nt, B_W), F32)],
        compiler_params=_cp(("parallel",)), name=name)(z, yf, yr, dvec, wglu, bglu)


def _glu_bwd(lay, z, y, ds, dvec, wglu, bglu, name):
    def body(u_ref, y_ref, ds_ref, d_ref, w_ref, b_ref, dy_ref, dud_ref, dw_ref, db_ref, dd_ref):
        j = pl.program_id(0)

        @pl.when(j == 0)
        def _():
            dw_ref[...] = jnp.zeros_like(dw_ref)
            db_ref[...] = jnp.zeros_like(db_ref)
            dd_ref[...] = jnp.zeros_like(dd_ref)

        yv = y_ref[...]
        g = _gelu(yv)
        gb = g.astype(MXU_DTYPE)
        sg = _sigmoid(_nn(gb, w_ref[...]) + b_ref[...])
        dsv = ds_ref[...].astype(F32)
        dpre = dsv * g * sg * (1.0 - sg)
        dpre_b = dpre.astype(MXU_DTYPE)
        dg = dsv * sg + _nt(dpre_b, w_ref[...])
        dw_ref[...] += _tn(gb, dpre_b)
        db_ref[...] += jnp.sum(dpre, axis=0, keepdims=True)
        dy = dg * _gelu_grad(yv)
        dy_ref[...] = dy.astype(dy_ref.dtype)
        dd_ref[...] += jnp.sum(dy * u_ref[...].astype(F32), axis=0, keepdims=True)
        dud_ref[...] = (dy * d_ref[...]).astype(dud_ref.dtype)

    tok = pl.BlockSpec((MT * TB, B_W), lambda j: (j, 0))
    vec = pl.BlockSpec((1, B_W), lambda j: (0, 0))
    mat = pl.BlockSpec((B_W, B_W), lambda j: (0, 0))
    vsh = jax.ShapeDtypeStruct((1, B_W), F32)
    return pl.pallas_call(
        body, grid=(lay.nb // MT,),
        in_specs=[pl.BlockSpec((MT * TB, B_W), lambda j: (j, 1)), tok, tok, vec, mat, vec],
        out_specs=[tok, tok, mat, vec, vec],
        out_shape=[jax.ShapeDtypeStruct((lay.nt, B_W), MXU_DTYPE), jax.ShapeDtypeStruct((lay.nt, B_W), F32),
                   jax.ShapeDtypeStruct((B_W, B_W), F32), vsh, vsh],
        compiler_params=_cp(("arbitrary",)), name=name)(z, y, ds, dvec, wglu, bglu)


def _dz_assemble(lay, dz_a, duf, dur, dud, dz_p, name):
    def body(a_ref, f_ref, r_ref, d_ref, p_ref, o_ref):
        o_ref[:, :2 * A_W] = a_ref[...].astype(o_ref.dtype)
        o_ref[:, 2 * A_W:2 * A_W + B_W] = (f_ref[...].astype(F32) + r_ref[...].astype(F32) + d_ref[...]).astype(o_ref.dtype)
        o_ref[:, 2 * A_W + B_W:] = p_ref[...].astype(o_ref.dtype)

    spec = lambda w: pl.BlockSpec((MT * TB, w), lambda j: (j, 0))
    return pl.pallas_call(
        body, grid=(lay.nb // MT,), in_specs=[spec(2 * A_W), spec(B_W), spec(B_W), spec(B_W), spec(C_W)],
        out_specs=spec(D_IN), out_shape=jax.ShapeDtypeStruct((lay.nt, D_IN), MXU_DTYPE),
        compiler_params=_cp(("parallel",)), name=name)(dz_a, duf, dur, dud, dz_p)


def _expand_rows(a):
    return jnp.broadcast_to(a[:, :, None, :], (2, SSM_G, SSM_H, SSM_P)).reshape(-1, SSM_P)


def _ssm_params(lam_re, lam_im, log_dt, b_re, b_im, c_re, c_im, name):
    lrx, lix = _expand_rows(lam_re), _expand_rows(lam_im)
    ldtx = _expand_rows(jnp.broadcast_to(log_dt[:, :, None], (2, SSM_G, SSM_P)))
    brt = jnp.transpose(b_re, (0, 1, 3, 2)).reshape(-1, SSM_P)
    bit = jnp.transpose(b_im, (0, 1, 3, 2)).reshape(-1, SSM_P)
    arx, aix, bbr, bbi = _disc_fwd(lrx, lix, ldtx, brt, bit, name)
    ar = arx.reshape(2, SSM_G, SSM_H, SSM_P)[:, :, 0].reshape(2, SSM_G * SSM_P)
    ai = aix.reshape(2, SSM_G, SSM_H, SSM_P)[:, :, 0].reshape(2, SSM_G * SSM_P)
    eye = jnp.eye(GQ, dtype=F32)

    def bmat(bt):
        t = bt.reshape(2, 4, GQ, SSM_H, SSM_P)
        return jnp.einsum('dqghp,gk->qdghkp', t, eye).reshape(4, 2 * QC, QS)

    bh = jnp.concatenate([bmat(bbr), bmat(bbi)], axis=-1).astype(MXU_DTYPE)

    def cmat(c):
        t = c.reshape(2, 4, GQ, SSM_H, SSM_P)
        return jnp.einsum('dqghp,gk->qgpdkh', t, eye).reshape(4, QS, 2 * QC)

    ch = jnp.concatenate([cmat(c_re), -cmat(c_im)], axis=1).astype(MXU_DTYPE)

    def rows8(a):
        return jnp.repeat(a, 4, axis=0)

    return dict(lrx=lrx, lix=lix, ldtx=ldtx, brt=brt, bit=bit, bh=bh, ch=ch, ar8=rows8(ar), ai8=rows8(ai))


def _ssm_param_grads(sp, dbh, dch, dar, dai, name):
    def bdiag(m):
        t = m.reshape(4, 2, GQ, SSM_H, GQ, SSM_P)
        return jnp.einsum('qdghgp->dqghp', t).reshape(-1, SSM_P)

    dbr, dbi = bdiag(dbh[..., :QS]), bdiag(dbh[..., QS:])

    def cdiag(m):
        t = m.reshape(4, GQ, SSM_P, 2, GQ, SSM_H)
        return jnp.einsum('qgpdgh->dqghp', t).reshape(2, SSM_G, SSM_H, SSM_P)

    dc_re, dc_im = cdiag(dch[:, :QS]), -cdiag(dch[:, QS:])

    def hrow(a):
        t = a.reshape(2, SSM_G, 1, SSM_P)
        return jnp.concatenate([t, jnp.zeros((2, SSM_G, SSM_H - 1, SSM_P), F32)], axis=2).reshape(-1, SSM_P)

    glr, gli, gdt, gbr, gbi = _disc_bwd(sp["lrx"], sp["lix"], sp["ldtx"], sp["brt"], sp["bit"],
                                        hrow(dar), hrow(dai), dbr, dbi, name)
    to_b = lambda g: jnp.transpose(g.reshape(2, SSM_G, SSM_H, SSM_P), (0, 1, 3, 2))
    return dict(ssm_lam_re=glr.reshape(2, SSM_G, SSM_P), ssm_lam_im=gli.reshape(2, SSM_G, SSM_P),
                ssm_log_dt=gdt.reshape(2, SSM_G), ssm_b_re=to_b(gbr), ssm_b_im=to_b(gbi),
                ssm_c_re=dc_re, ssm_c_im=dc_im)


def _layer_consts(p):
    c = {}
    c["ws"] = p["sgu_w"].astype(MXU_DTYPE)
    c["wst"] = jnp.transpose(p["sgu_w"], (0, 2, 1)).astype(MXU_DTYPE)
    c["gbias"] = jnp.repeat(p["sgu_b"].T, 64, axis=1)
    pw = jnp.zeros((C_W, C_W), F32)
    for i in range(4):
        pw = pw.at[64 * i:64 * i + 64, 64 * i:64 * i + 64].set(p["pool_w"][i])
    c["pw"] = pw.astype(MXU_DTYPE)
    c["pscale"] = p["pool_scale"].reshape(1, C_W)
    c["dvec"] = p["ssm_d"].reshape(1, B_W)
    c["bglu"] = p["glu_b"].reshape(1, B_W)
    return c


def _layer_fwd(lay, i, x, modarr, p, w, cst, sp, bands, inv, perm, sides=None, last=False):
    n = f"l{i}_"
    sides = sides or {}
    win_side, win_fill = sides.get("win", (None, None))
    ssm_side, ssm_fill = sides.get("ssm", (None, None))
    ffn_side, ffn_fill = sides.get("ffn", (None, None))
    res = {"x0": x}
    h = _normmod_fwd(lay, x, p["norm_mix_pre"].reshape(1, D), modarr, 0, 1, n + "nm1")
    z = _mm([(h, w["win_t"])], True, MXU_DTYPE, n + "win", side=win_side)
    if win_side is not None:
        z, extra = z
        win_fill(extra)
    a = _gate_fwd(lay, z, cst["ws"], cst["gbias"], n + "gate")
    yf, yr, hst, extra = _ssm_fwd(lay, z, perm, sp["bh"], sp["ch"], sp["ar8"], sp["ai8"], n + "ssm", ssm_side)
    if ssm_side is not None:
        ssm_fill(extra)
    s, y = _glu_fwd(lay, z, yf, yr, cst["dvec"], w["wglu"], cst["bglu"], n + "glu")
    c = _pool_fwd(lay, z, bands, inv, cst["pw"], cst["pscale"], n + "pool")
    mcat = jnp.concatenate([s, a, c], axis=1)
    res["wout_p"] = _perm_wout(w["wout"])
    m = _mm([(mcat, res["wout_p"])], False, MXU_DTYPE, n + "wout")
    x1, h2 = _resnorm_normmod_fwd(lay, x, m, p["norm_mix_post"].reshape(1, D), p["norm_ffn_pre"].reshape(1, D),
                                  modarr, 2, 3, 4, n + "rn1nm2")
    g, u, act, extra = _ffn_up(h2, w["wg_t"], w["wu_t"], n + "ffn_up", ffn_side)
    if ffn_side is not None:
        ffn_fill(extra)
    f = _mm([(act, w["wd"])], False, MXU_DTYPE, n + "ffn_down")
    res.update(h=h, z=z, hst=hst, y=y, mcat=mcat, m=m, x1=x1, h2=h2, g=g, u=u, act=act, f=f)
    if last:
        return None, res
    x2 = _resnorm_fwd(lay, x1, f, p["norm_ffn_post"].reshape(1, D), modarr, 5, n + "rn2")
    return x2, res


def _layer_bwd(lay, i, dx2, modarr, p, w, cst, sp, bands, inv, perm, res, side_fns=None):
    n = f"l{i}b_"
    big, small = {}, {}
    side_fns = side_fns or {}
    side_of = lambda key: side_fns[key](big) if key in side_fns else None
    df, dg2, gpost2 = _resnorm_bwd(lay, dx2, res["f"], p["norm_ffn_post"].reshape(1, D), modarr, 5, n + "rn2")
    big["wd"] = _mm_tn(res["act"], df, MXU_DTYPE, n + "dwd")
    dg, du, early = _ffn_down_bwd(df, w["wd"], res["g"], res["u"], n + "ffn_down", side_of("ffn_down"))
    dh2_side = side_of("dh2")
    dh2 = _mm([(dg, w["wg_t"]), (du, w["wu_t"])], False, MXU_DTYPE, n + "dh2", side=dh2_side)
    if dh2_side is not None:
        dh2, ex = dh2
        early = early + ex
    big["wg_t"] = _mm_tn(dg, res["h2"], MXU_DTYPE, n + "dwg")
    big["wu_t"] = _mm_tn(du, res["h2"], MXU_DTYPE, n + "dwu")
    dx1, dm, dsh2, dsc2, gpre2, dg1, gpost1 = _normmod_resnorm_bwd(
        lay, res["x1"], dh2, dx2, p["norm_ffn_pre"].reshape(1, D), res["m"], p["norm_mix_post"].reshape(1, D), modarr,
        4, 2, n + "nm2rn1")
    big["wout"] = _unperm_wout(_mm_tn(res["mcat"], dm, MXU_DTYPE, n + "dwout"))
    dmcat = _mm([(dm, res["wout_p"])], True, MXU_DTYPE, n + "dmcat")
    z = res["z"]
    dz_a, dws, dgb = _gate_bwd(lay, z, dmcat, cst["ws"], cst["wst"], cst["gbias"], n + "gate")
    dy, dud, dwglu, dbglu, ddvec = _glu_bwd(lay, z, res["y"], dmcat, cst["dvec"], w["wglu"], cst["bglu"], n + "glu")
    big["wglu"] = dwglu.astype(MXU_DTYPE)
    duf, dur, dbh, dch, dar, dai, ex = _ssm_bwd(lay, z, dy, perm, res["hst"], sp["bh"], sp["ch"], sp["ar8"],
                                                sp["ai8"], n + "ssm", side_of("ssm"))
    early = early + ex
    dz_p, dpw, dpsc = _pool_bwd(lay, z, dmcat, bands, inv, cst["pw"], cst["pscale"], n + "pool")
    dz = _dz_assemble(lay, dz_a, duf, dur, dud, dz_p, n + "dz")
    big["win_t"] = _mm_tn(dz, res["h"], MXU_DTYPE, n + "dwin")
    dh_side = side_of("dh")
    dh = _mm([(dz, w["win_t"])], False, MXU_DTYPE, n + "dh", side=dh_side)
    if dh_side is not None:
        dh, ex = dh
        early = early + ex
    dx, dsh1, dsc1, gpre1 = _normmod_bwd(lay, res["x0"], dh, dx1, p["norm_mix_pre"].reshape(1, D), modarr, 1, n + "nm1",
                                         latent_only=(i == 0))

    small.update(norm_mix_pre=gpre1[0], norm_mix_post=gpost1[0], norm_ffn_pre=gpre2[0], norm_ffn_post=gpost2[0])
    small["sgu_w"] = dws
    small["sgu_b"] = jnp.sum(dgb.reshape(CHUNK, 4, 64), axis=-1).T
    small.update(_ssm_param_grads(sp, dbh, dch, dar, dai, n + "disc"))
    small["ssm_d"] = ddvec.reshape(SSM_G, SSM_H)
    small["glu_b"] = dbglu[0]
    small["pool_w"] = jnp.stack([dpw[64 * k:64 * k + 64, 64 * k:64 * k + 64] for k in range(4)])
    small["pool_scale"] = dpsc[0]
    dmod = jnp.concatenate([dsh1, dsc1, dg1, dsh2, dsc2, dg2], axis=1)[:lay.bl + 1]
    dmod = jnp.concatenate([dmod, jnp.zeros((8 - lay.bl - 1, 6, D), F32)], axis=0)
    return dx, big, small, dmod, early


def _perm_wout(w):
    return w.reshape(4, D // 4, D)[np.array(WOUT_PERM)].reshape(D, D)


def _unperm_wout(g):
    return g.reshape(4, D // 4, D)[np.array(WOUT_INV)].reshape(D, D)


SMALL_NAMES = ["norm_mix_pre", "norm_mix_post", "norm_ffn_pre", "norm_ffn_post", "sgu_w", "sgu_b", "ssm_lam_re",
               "ssm_lam_im", "ssm_log_dt", "ssm_b_re", "ssm_b_im", "ssm_c_re", "ssm_c_im", "ssm_d", "glu_b", "pool_w",
               "pool_scale"]
BIG_NAMES = ["win_t", "wout", "wglu", "wg_t", "wu_t", "wd"]


def _sincos_2d(rows, cols, dim):
    quarter = dim // 4
    omega = 1.0 / (10000.0 ** (jnp.arange(quarter, dtype=F32) / quarter))
    r = jnp.arange(rows, dtype=F32)[:, None] * omega
    cc = jnp.arange(cols, dtype=F32)[:, None] * omega
    er = jnp.concatenate([jnp.sin(r), jnp.cos(r)], axis=-1)
    ec = jnp.concatenate([jnp.sin(cc), jnp.cos(cc)], axis=-1)
    pe = jnp.concatenate([jnp.broadcast_to(er[:, None, :], (rows, cols, dim // 2)),
                          jnp.broadcast_to(ec[None, :, :], (rows, cols, dim // 2))], axis=-1)
    return pe.reshape(rows * cols, dim)


def _core(x, ctx, target, mods_local, params, weights, w_sides=None, g_side_fns=None):
    bl, lat, _ = x.shape
    assert bl == 4 and lat % TB == 0, "the scan fills 8 sublanes with 2 directions x 4 sequences"
    lay = _Layout(bl, lat)
    pe = _sincos_2d(lat // GRID_W, GRID_W, D)
    bands_np, inv_np = _band_constants()
    bands, inv = jnp.asarray(bands_np, MXU_DTYPE), jnp.asarray(inv_np, F32)
    perm = jnp.asarray(_scan_perm(bl), MXU_DTYPE)
    csts, sps, ress, wls = [], [], [], []
    for i in range(2):
        csts.append(_layer_consts(params[i]))
        p = params[i]
        sps.append(_ssm_params(p["ssm_lam_re"], p["ssm_lam_im"], p["ssm_log_dt"], p["ssm_b_re"], p["ssm_b_im"],
                               p["ssm_c_re"], p["ssm_c_im"], f"l{i}_disc"))
        wls.append(dict(weights[i]))

    embed_side, embed_fill = (w_sides[0].get("embed") if w_sides else None) or (None, None)
    xt, extra = _embed(lay, x, ctx, pe, embed_side)
    if embed_side is not None:
        embed_fill(wls, extra)
    if callable(mods_local):
        mods_local = mods_local()
    modarrs = [lay.mod_tiles(mods_local[i]) for i in range(2)]
    for i in range(2):
        sides = {}
        for key, (side, fill) in ((w_sides or [{}, {}])[i]).items():
            sides[key] = (side, functools.partial(fill, wls))
        xt, res = _layer_fwd(lay, i, xt, modarrs[i], params[i], wls[i], csts[i], sps[i], bands, inv, perm, sides,
                             last=(i == 1))
        ress.append(res)
    dx, lossv = _resnorm_loss(lay, ress[1]["x1"], ress[1]["f"], params[1]["norm_ffn_post"].reshape(1, D), modarrs[1], 5,
                              target)
    bigs, smalls, dmods, early = [None, None], [None, None], [None, None], []
    for i in (1, 0):
        fns = {}
        if i == 0 and g_side_fns is not None:
            fns = {key: functools.partial(fn, bigs[1]) for key, fn in g_side_fns.items()}
        dx, bigs[i], smalls[i], dmods[i], ex = _layer_bwd(lay, i, dx, modarrs[i], params[i], wls[i], csts[i], sps[i],
                                                           bands, inv, perm, ress[i], fns)
        early += ex
    return lossv[0, 0], dx.reshape(bl, lat, D), bigs, smalls, dmods, early


def _my_index():
    return 4 * lax.axis_index("x") + 2 * lax.axis_index("y") + lax.axis_index("c")


def _peer(k):
    x, y, c = lax.axis_index("x"), lax.axis_index("y"), lax.axis_index("c")
    kx, ky, kc = (k >> 2) & 1, (k >> 1) & 1, k & 1
    px = 1 - x if kx else x
    py = 1 - y if ky else y
    pc = 1 - c if kc else c
    return (px, py, pc), 4 * px + 2 * py + pc


class _Side:
    def __init__(self, items):
        self.items = items
        self.n = len(items)
        self.ncopies = sum(len(it[2]) for it in items)
        self.arrays = [it[0] for it in items]
        anyspec = pl.BlockSpec(memory_space=pl.ANY)
        self.in_specs = [anyspec] * self.n
        self.out_specs = [anyspec] * self.n
        self.out_shape = [jax.ShapeDtypeStruct((slots,) + tuple(a.shape) if mode == "gather" else tuple(a.shape), a.dtype)
                          for a, mode, ks, slots in items]
        self.scratch = [pltpu.SemaphoreType.DMA((self.ncopies,)), pltpu.SemaphoreType.DMA((self.ncopies,)),
                        pltpu.SemaphoreType.DMA((self.n,))]

    def _copies(self, ins, outs, sems):
        send_sems, recv_sems, local_sems = sems
        slot_of = lambda idx, slots: idx if slots == 8 else (idx // 2 if slots == 4 else idx % 2)
        me = _my_index()
        local, sends, recvs = [], [], []
        q = 0
        for t, (arr, mode, ks, slots) in enumerate(self.items):
            src_own = ins[t] if mode == "gather" else ins[t].at[me]
            local.append(pltpu.make_async_copy(src_own, outs[t].at[slot_of(me, slots)], local_sems.at[t]))
            for k in ks:
                peer, pidx = _peer(k)
                src = ins[t] if mode == "gather" else ins[t].at[pidx]
                sends.append(pltpu.make_async_remote_copy(
                    src_ref=src, dst_ref=outs[t].at[slot_of(me, slots)], send_sem=send_sems.at[q], recv_sem=recv_sems.at[q],
                    device_id=peer, device_id_type=pl.DeviceIdType.MESH))
                recvs.append(pltpu.make_async_remote_copy(
                    src_ref=src, dst_ref=outs[t].at[slot_of(pidx, slots)], send_sem=send_sems.at[q], recv_sem=recv_sems.at[q],
                    device_id=peer, device_id_type=pl.DeviceIdType.MESH))
                q += 1
        return local, sends, recvs

    def start(self, ins, outs, sems):
        local, sends, _ = self._copies(ins, outs, sems)
        for cp in sends + local:
            cp.start()

    def wait(self, ins, outs, sems):
        local, sends, recvs = self._copies(ins, outs, sems)
        for cp in recvs:
            cp.wait_recv()
        for cp in sends:
            cp.wait_send()
        for cp in local:
            cp.wait()


def _comm(items, name):
    side = _Side(items)
    n = side.n

    def body(*refs):
        ins, outs, sems = refs[:n], refs[n:2 * n], refs[2 * n:]
        side.start(ins, outs, sems)
        side.wait(ins, outs, sems)

    return pl.pallas_call(
        body, in_specs=side.in_specs, out_specs=side.out_specs, out_shape=side.out_shape, scratch_shapes=side.scratch,
        compiler_params=pltpu.CompilerParams(has_side_effects=True), name=name)(*side.arrays)


def _spread(items, name):
    n = len(items)
    ncopies = sum(len(it[1]) for it in items)

    def slot_of(idx, slots):
        return idx if slots == 8 else (idx // 2 if slots == 4 else idx % 2)

    def body(*refs):
        ins, outs, bufs = refs[:n], refs[n:2 * n], refs[2 * n:3 * n]
        load_sems, store_sems, send_sems, recv_sems = refs[3 * n:]
        me = _my_index()
        loads = [pltpu.make_async_copy(ins[t], bufs[t], load_sems.at[t]) for t in range(n)]
        for cp in loads:
            cp.start()
        stores, sends, recvs = [], [], []
        q = 0
        for t, (arr, ks, slots) in enumerate(items):
            loads[t].wait()
            own = outs[t].at[slot_of(me, slots)]
            stores.append(pltpu.make_async_copy(bufs[t], own, store_sems.at[t]))
            stores[-1].start()
            for k in ks:
                peer, pidx = _peer(k)
                sends.append(pltpu.make_async_remote_copy(
                    src_ref=bufs[t], dst_ref=own, send_sem=send_sems.at[q], recv_sem=recv_sems.at[q],
                    device_id=peer, device_id_type=pl.DeviceIdType.MESH))
                recvs.append(pltpu.make_async_remote_copy(
                    src_ref=bufs[t], dst_ref=outs[t].at[slot_of(pidx, slots)], send_sem=send_sems.at[q],
                    recv_sem=recv_sems.at[q], device_id=peer, device_id_type=pl.DeviceIdType.MESH))
                sends[-1].start()
                q += 1
        for cp in recvs:
            cp.wait_recv()
        for cp in sends:
            cp.wait_send()
        for cp in stores:
            cp.wait()

    anyspec = pl.BlockSpec(memory_space=pl.ANY)
    return pl.pallas_call(
        body, in_specs=[anyspec] * n, out_specs=[anyspec] * n,
        out_shape=[jax.ShapeDtypeStruct((slots,) + tuple(arr.shape), arr.dtype) for arr, ks, slots in items],
        scratch_shapes=[pltpu.VMEM(tuple(arr.shape), arr.dtype) for arr, ks, slots in items]
        + [pltpu.SemaphoreType.DMA((n,)), pltpu.SemaphoreType.DMA((n,)), pltpu.SemaphoreType.DMA((ncopies,)),
           pltpu.SemaphoreType.DMA((ncopies,))],
        compiler_params=pltpu.CompilerParams(has_side_effects=True, vmem_limit_bytes=VMEM_LIMIT),
        name=name)(*[it[0] for it in items])


ALL7 = (1, 2, 3, 4, 5, 6, 7)
CHIPS3 = (2, 4, 6)


def _sum8(parts, name):
    def one(a, nm):
        _, r, c = a.shape
        tr = r if r <= 512 else _pick_rows(r)

        def body(a_ref, o_ref):
            acc = a_ref[0].astype(F32)
            for q in range(1, a_ref.shape[0]):
                acc = acc + a_ref[q].astype(F32)
            o_ref[...] = acc

        return pl.pallas_call(
            body, grid=(r // tr,), in_specs=[pl.BlockSpec((a.shape[0], tr, c), lambda i: (0, i, 0))],
            out_specs=pl.BlockSpec((tr, c), lambda i: (i, 0)), out_shape=jax.ShapeDtypeStruct((r, c), F32),
            compiler_params=_cp(("parallel",)), name=nm)(a)

    return [one(a, f"{name}{i}") for i, a in enumerate(parts)]


def _pick_rows(r, cap=512):
    for t in (512, 352, 256, 176, 128, 64, 32, 16, 8):
        if r % t == 0 and t <= cap:
            return t
    return r


def _adam(w, g, m, v, name):
    shape = w.shape
    nel = int(np.prod(shape))
    c1 = 1.0 / (1.0 - ADAM_B1 ** ADAM_STEP)
    c2 = 1.0 / (1.0 - ADAM_B2 ** ADAM_STEP)

    def body(w_ref, g_ref, m_ref, v_ref, d_ref, nm_ref, nv_ref):
        gv = g_ref[...]
        nm = ADAM_B1 * m_ref[...] + (1.0 - ADAM_B1) * gv
        nv = ADAM_B2 * v_ref[...] + (1.0 - ADAM_B2) * (gv * gv)
        d_ref[...] = -ADAM_LR * ((nm * c1) / (jnp.sqrt(nv * c2) + ADAM_EPS) + ADAM_WD * w_ref[...])
        nm_ref[...] = nm
        nv_ref[...] = nv

    padded = int(np.prod(shape[:-2])) * (-(-shape[-2] // 8) * 8) * (-(-shape[-1] // 128) * 128) if len(shape) >= 2 else nel
    if len(shape) >= 2 and padded <= 1024 * 1024:
        sh = jax.ShapeDtypeStruct(shape, F32)
        return pl.pallas_call(body, out_shape=[sh] * 3, compiler_params=_cp(None), name=name)(w, g, m, v)

    if len(shape) >= 2 and shape[-1] >= 128:
        lanes = shape[-1]
    else:
        lanes = 512 if nel % 512 == 0 else 128
    r = nel // lanes
    tr = r if r * lanes <= 384 * 1024 else _pick_rows(r, 384 * 1024 // lanes)

    spec = pl.BlockSpec((tr, lanes), lambda i: (i, 0))
    sh = jax.ShapeDtypeStruct((r, lanes), F32)
    outs = pl.pallas_call(
        body, grid=(r // tr,), in_specs=[spec] * 4, out_specs=[spec] * 3, out_shape=[sh] * 3,
        compiler_params=_cp(("parallel",)), name=name)(*[a.reshape(r, lanes) for a in (w, g, m, v)])
    return [o.reshape(shape) for o in outs]


def _silu(x):
    return x * _sigmoid(x)


def _mod_fwd(c_rows, w_mod, b_cols, name):
    def body(c_ref, w_ref, b_ref, o_ref):
        s = _silu(c_ref[...])
        for l in range(2):
            o_ref[l] = jnp.dot(s, w_ref[l], preferred_element_type=F32, precision=lax.Precision.HIGHEST) + b_ref[l]

    nc = w_mod.shape[2]
    return pl.pallas_call(body, out_shape=jax.ShapeDtypeStruct((2, c_rows.shape[0], nc), F32),
                          compiler_params=_cp(None), name=name)(c_rows, w_mod, b_cols)


def _mod_bwd(c_rows, w_mod, dlat, dctx8, name):
    nrow = c_rows.shape[0]
    nb = nrow - 8

    def body(c_ref, w_ref, dl_ref, dc_ref, gw_ref, gc_ref):
        s = _silu(c_ref[...])
        ctx_row = lax.broadcasted_iota(jnp.int32, (nrow, 1), 0) == nb
        gc = jnp.zeros((1, D), F32)
        for l in range(2):
            dctx = dc_ref[0, l]
            for q in range(1, 8):
                dctx = dctx + dc_ref[q, l]
            dm = dl_ref[l] + jnp.where(ctx_row, dctx, 0.0)
            gw_ref[l] = lax.dot_general(s, dm, (((0,), (0,)), ((), ())), preferred_element_type=F32,
                                        precision=lax.Precision.HIGHEST)
            gc = gc + lax.dot_general(dctx, w_ref[l], (((1,), (1,)), ((), ())), preferred_element_type=F32,
                                      precision=lax.Precision.HIGHEST)
        gc_ref[...] = gc

    nc = w_mod.shape[2]
    return pl.pallas_call(body, out_shape=[jax.ShapeDtypeStruct((2, D, nc), F32), jax.ShapeDtypeStruct((1, D), F32)],
                          compiler_params=_cp(None), name=name)(c_rows, w_mod, dlat, dctx8)


def _bmod_cctx(dmod_all, gc4, c_ctx, name):
    def body(dm_ref, gc_ref, cc_ref, gb_ref, gcc_ref):
        for l in range(2):
            acc = jnp.sum(dm_ref[0, l], axis=0, keepdims=True)
            for q in range(1, 8):
                acc = acc + jnp.sum(dm_ref[q, l], axis=0, keepdims=True)
            gb_ref[l:l + 1, :] = acc
        g = gc_ref[0] + gc_ref[1] + gc_ref[2] + gc_ref[3]
        cv = cc_ref[...]
        sg = _sigmoid(cv)
        gcc_ref[...] = g * (sg * (1.0 + cv * (1.0 - sg)))

    return pl.pallas_call(body, out_shape=[jax.ShapeDtypeStruct((2, 6 * D), F32), jax.ShapeDtypeStruct((1, D), F32)],
                          compiler_params=_cp(None), name=name)(dmod_all, gc4, c_ctx)


def kernel(x, c, ctx, c_ctx, w_mod, b_mod, norm_mix_pre, norm_mix_post, norm_ffn_pre, norm_ffn_post, w_in, w_out, sgu_w, sgu_b, ssm_lam_re, ssm_lam_im, ssm_log_dt, ssm_b_re, ssm_b_im, ssm_c_re, ssm_c_im, ssm_d, glu_w, glu_b, pool_w, pool_scale, ffn_w_gate, ffn_w_up, ffn_w_down, loss_target, m_c_ctx, m_w_mod, m_b_mod, m_norm_mix_pre, m_norm_mix_post, m_norm_ffn_pre, m_norm_ffn_post, m_w_in, m_w_out, m_sgu_w, m_sgu_b, m_ssm_lam_re, m_ssm_lam_im, m_ssm_log_dt, m_ssm_b_re, m_ssm_b_im, m_ssm_c_re, m_ssm_c_im, m_ssm_d, m_glu_w, m_glu_b, m_pool_w, m_pool_scale, m_ffn_w_gate, m_ffn_w_up, m_ffn_w_down, v_c_ctx, v_w_mod, v_b_mod, v_norm_mix_pre, v_norm_mix_post, v_norm_ffn_pre, v_norm_ffn_post, v_w_in, v_w_out, v_sgu_w, v_sgu_b, v_ssm_lam_re, v_ssm_lam_im, v_ssm_log_dt, v_ssm_b_re, v_ssm_b_im, v_ssm_c_re, v_ssm_c_im, v_ssm_d, v_glu_w, v_glu_b, v_pool_w, v_pool_scale, v_ffn_w_gate, v_ffn_w_up, v_ffn_w_down):
    wts = dict(c_ctx=c_ctx, w_mod=w_mod, b_mod=b_mod, norm_mix_pre=norm_mix_pre, norm_mix_post=norm_mix_post,
               norm_ffn_pre=norm_ffn_pre, norm_ffn_post=norm_ffn_post, w_in=w_in, w_out=w_out, sgu_w=sgu_w, sgu_b=sgu_b,
               ssm_lam_re=ssm_lam_re, ssm_lam_im=ssm_lam_im, ssm_log_dt=ssm_log_dt, ssm_b_re=ssm_b_re, ssm_b_im=ssm_b_im,
               ssm_c_re=ssm_c_re, ssm_c_im=ssm_c_im, ssm_d=ssm_d, glu_w=glu_w, glu_b=glu_b, pool_w=pool_w,
               pool_scale=pool_scale, ffn_w_gate=ffn_w_gate, ffn_w_up=ffn_w_up, ffn_w_down=ffn_w_down)
    ms = dict(c_ctx=m_c_ctx, w_mod=m_w_mod, b_mod=m_b_mod, norm_mix_pre=m_norm_mix_pre, norm_mix_post=m_norm_mix_post,
              norm_ffn_pre=m_norm_ffn_pre, norm_ffn_post=m_norm_ffn_post, w_in=m_w_in, w_out=m_w_out, sgu_w=m_sgu_w,
              sgu_b=m_sgu_b, ssm_lam_re=m_ssm_lam_re, ssm_lam_im=m_ssm_lam_im, ssm_log_dt=m_ssm_log_dt,
              ssm_b_re=m_ssm_b_re, ssm_b_im=m_ssm_b_im, ssm_c_re=m_ssm_c_re, ssm_c_im=m_ssm_c_im, ssm_d=m_ssm_d,
              glu_w=m_glu_w, glu_b=m_glu_b, pool_w=m_pool_w, pool_scale=m_pool_scale, ffn_w_gate=m_ffn_w_gate,
              ffn_w_up=m_ffn_w_up, ffn_w_down=m_ffn_w_down)
    vs = dict(c_ctx=v_c_ctx, w_mod=v_w_mod, b_mod=v_b_mod, norm_mix_pre=v_norm_mix_pre, norm_mix_post=v_norm_mix_post,
              norm_ffn_pre=v_norm_ffn_pre, norm_ffn_post=v_norm_ffn_post, w_in=v_w_in, w_out=v_w_out, sgu_w=v_sgu_w,
              sgu_b=v_sgu_b, ssm_lam_re=v_ssm_lam_re, ssm_lam_im=v_ssm_lam_im, ssm_log_dt=v_ssm_log_dt,
              ssm_b_re=v_ssm_b_re, ssm_b_im=v_ssm_b_im, ssm_c_re=v_ssm_c_re, ssm_c_im=v_ssm_c_im, ssm_d=v_ssm_d,
              glu_w=v_glu_w, glu_b=v_glu_b, pool_w=v_pool_w, pool_scale=v_pool_scale, ffn_w_gate=v_ffn_w_gate,
              ffn_w_up=v_ffn_w_up, ffn_w_down=v_ffn_w_down)
    order = list(wts.keys())
    bl = x.shape[0]
    nseq = bl * N_DEV
    me = _my_index()
    chip = me // 2
    ncol = w_mod.shape[2]

    (c_all,) = _spread([(c, ALL7, 8)], "ag_c")
    nrow = nseq + 8
    c_rows = jnp.concatenate([c_all.reshape(nseq, D), c_ctx[None], jnp.zeros((7, D), F32)], axis=0)
    b_cols = lax.dynamic_slice_in_dim(b_mod, chip * ncol, ncol, axis=1)[:, None, :]
    mod_cols = _mod_fwd(c_rows, w_mod, b_cols, "mod_fwd")
    stash = {}

    def mods_local():
        mods = jnp.transpose(stash["mod4"], (1, 2, 0, 3)).reshape(2, nrow, 6 * D)
        return jnp.concatenate([lax.dynamic_slice_in_dim(mods, me * bl, bl, axis=1), mods[:, nseq:nseq + 1],
                                jnp.zeros((2, 8 - bl - 1, 6 * D), F32)], axis=1)

    shards = {}
    for i in range(2):
        for nme, s in zip(BIG_NAMES, [w_in[i].T, w_out[i], glu_w[i], ffn_w_gate[i].T, ffn_w_up[i].T, ffn_w_down[i]]):
            shards[(i, nme)] = s.astype(MXU_DTYPE)
    weights = [{}, {}]
    ffn_names = ("wg_t", "wu_t", "wd")
    w_plan = [{"embed": [(0, "win_t")], "win": [(0, "wout"), (0, "wglu")], "ssm": [(0, "wg_t"), (0, "wu_t")],
               "ffn": [(0, "wd"), (1, "win_t"), (1, "wout"), (1, "wglu")]},
              {"ssm": [(1, "wg_t"), (1, "wu_t")], "ffn": [(1, "wd")]}]

    def w_entry(keys, more=()):
        def fill(wls, gathered):
            for (i, nme), g in zip(keys, gathered):
                wls[i][nme] = g.reshape(-1, g.shape[-1])
            for (nme, _), g in zip(more, gathered[len(keys):]):
                stash[nme] = g
        return _Side([(shards[k2], "gather", CHIPS3, 4) for k2 in keys] + [(a, "gather", CHIPS3, 4) for _, a in more]), fill

    w_sides = [{key: w_entry(keys) for key, keys in plan.items()} for plan in w_plan]
    w_sides[0]["embed"] = w_entry(w_plan[0]["embed"], more=[("mod4", mod_cols)])

    eighths = lambda g: g.reshape(8, g.shape[0] // 8, g.shape[1])
    g_plan = {"ffn_down": [(1, "win_t"), (1, "wg_t")], "dh2": [(1, "wu_t"), (1, "wout"), (1, "wglu")],
              "ssm": [(0, k) for k in BIG_NAMES if k != "win_t"] + [(1, "wd")], "dh": [(0, "win_t")]}
    early_g = g_plan["ffn_down"] + g_plan["dh2"] + g_plan["ssm"] + g_plan["dh"]

    def g_entry(keys):
        return lambda big1, big0: _Side([(eighths((big1 if i == 1 else big0)[k]), "a2a", ALL7, 8) for i, k in keys])

    g_side_fns = {key: g_entry(keys) for key, keys in g_plan.items()}

    params = [{k: wts[k][i] for k in SMALL_NAMES} for i in range(2)]
    loss_part, grad_x, bigs, smalls, dmods, early = _core(x, ctx, loss_target, mods_local, params, weights,
                                                           w_sides, g_side_fns)
    loss = lax.psum(loss_part, ("x", "y", "c"))

    dmod_local = jnp.stack([dmods[i].reshape(8, 6 * D) for i in range(2)])
    (dmod_all,) = _spread([(dmod_local, ALL7, 8)], "ag_dmod")
    dcols = lax.dynamic_slice_in_dim(dmod_all, chip * ncol, ncol, axis=3)
    dlat = jnp.transpose(dcols[:, :, :bl], (1, 0, 2, 3)).reshape(2, nseq, ncol)
    dlat = jnp.concatenate([dlat, jnp.zeros((2, 8, ncol), F32)], axis=1)
    dctx8 = dcols[:, :, bl:bl + 1]
    g_w_mod, gc_part = _mod_bwd(c_rows, w_mod, dlat, dctx8, "mod_bwd")
    (gc4,) = _spread([(gc_part, CHIPS3, 4)], "ag_cctx")
    g_b_mod, g_c_ctx = _bmod_cctx(dmod_all, gc4, c_ctx[None], "bmod_cctx")

    small_flat = jnp.concatenate([jnp.stack([smalls[i][k] for i in range(2)]).reshape(-1) for k in SMALL_NAMES])
    npad = (-small_flat.shape[0]) % (8 * 1024)
    small_flat = jnp.concatenate([small_flat, jnp.zeros((npad,), F32)])
    late = _comm([(small_flat.reshape(8, -1, 1024), "a2a", ALL7, 8)], "a2a_grads")
    sums = _sum8(list(early) + list(late), "gsum")
    fin = _spread([(s, (1,), 2) for s in sums[:-1]] + [(sums[-1], ALL7, 8)], "ag_grads")
    big_g = [{}, {}]
    for (i, k), g in zip(early_g, fin[:-1]):
        big_g[i][k] = g.reshape(-1, g.shape[-1])
    small_red = fin[-1].reshape(-1)

    grads = {}
    off = 0
    for k in SMALL_NAMES:
        shp = wts[k].shape
        nel = int(np.prod(shp))
        grads[k] = small_red[off:off + nel].reshape(shp)
        off += nel
    grads["c_ctx"] = g_c_ctx[0]
    grads["w_mod"] = g_w_mod
    grads["b_mod"] = g_b_mod
    grads["w_in"] = jnp.stack([big_g[i]["win_t"].T for i in range(2)])
    grads["w_out"] = jnp.stack([big_g[i]["wout"] for i in range(2)])
    grads["glu_w"] = jnp.stack([big_g[i]["wglu"] for i in range(2)])
    grads["ffn_w_gate"] = jnp.stack([big_g[i]["wg_t"].T for i in range(2)])
    grads["ffn_w_up"] = jnp.stack([big_g[i]["wu_t"].T for i in range(2)])
    grads["ffn_w_down"] = jnp.stack([big_g[i]["wd"] for i in range(2)])

    deltas, new_m, new_v = {}, {}, {}
    for k in order:
        deltas[k], new_m[k], new_v[k] = _adam(wts[k], grads[k], ms[k], vs[k], "adam_" + k)
    return (loss, grad_x, *[grads[k] for k in order], *[deltas[k] for k in order],
            *[new_m[k] for k in order], *[new_v[k] for k in order])
```

```python
import functools
import math

import numpy as np
import jax
import jax.numpy as jnp
from jax import lax
from jax.experimental import pallas as pl
from jax.experimental.pallas import tpu as pltpu

F32 = jnp.float32
BF16 = jnp.bfloat16
MXU_DTYPE = jnp.bfloat16
MCAT_A, MCAT_C = 2, 3
WOUT_PERM, WOUT_INV = (1, 2, 0, 3), (2, 0, 1, 3)

D = 1024
EPS = 1e-6
TB = 256
CTX = 256
CHUNK = 128
GRID_W = 64
A_W, B_W, C_W = 256, 512, 256
D_IN = 1280
D_FF = 2816
SSM_G, SSM_P, SSM_H = 32, 64, 16
ST = 64
POOL_WINDOWS = (2, 4, 8, 16)
N_DEV = 8
VMEM_LIMIT = 52 * 1024 * 1024
GELU_C = math.sqrt(2.0 / math.pi)

ADAM_LR, ADAM_B1, ADAM_B2, ADAM_EPS, ADAM_WD, ADAM_STEP = 0.001, 0.9, 0.999, 1e-08, 0.01, 10


def _cp(sem=None, vmem=VMEM_LIMIT, **kw):
    return pltpu.CompilerParams(dimension_semantics=sem, vmem_limit_bytes=vmem, **kw)


def _pick(n, cap):
    if n <= cap:
        return n
    best = None
    for t in range(128, cap + 1, 128):
        if n % t == 0:
            best = t
    assert best is not None, (n, cap)
    return best


def _gelu(x):
    return 0.5 * x * (1.0 + jnp.tanh(GELU_C * (x + 0.044715 * x * x * x)))


def _gelu_grad(x):
    t = jnp.tanh(GELU_C * (x + 0.044715 * x * x * x))
    return 0.5 * (1.0 + t) + 0.5 * x * (1.0 - t * t) * GELU_C * (1.0 + 3.0 * 0.044715 * x * x)


def _sigmoid(x):
    return 1.0 / (1.0 + jnp.exp(-x))


def _dot(a, b, dims):
    return lax.dot_general(a, b, (dims, ((), ())), preferred_element_type=F32)


def _nn(a, b):
    return _dot(a, b, ((1,), (0,)))


def _nt(a, b):
    return _dot(a, b, ((1,), (1,)))


def _tn(a, b):
    return _dot(a, b, ((0,), (0,)))


def _mm(pairs, nt, out_dtype, name, tm=512, side=None):
    m = pairs[0][0].shape[0]
    n = pairs[0][1].shape[0] if nt else pairs[0][1].shape[1]
    tn = _pick(n, 1408)
    tm = min(tm, m)
    npairs = len(pairs)
    ni, nj = m // tm, n // tn

    def body(*refs):
        own, srefs = _side_split(refs, 2 * npairs, 1, 0, side)
        o_ref = own[-1]
        i, j = pl.program_id(0), pl.program_id(1)
        _side_start(side, srefs, jnp.logical_and(i == 0, j == 0))
        acc = None
        for t in range(npairs):
            a = own[2 * t][...].astype(MXU_DTYPE)
            b = own[2 * t + 1][...].astype(MXU_DTYPE)
            r = _nt(a, b) if nt else _nn(a, b)
            acc = r if acc is None else acc + r
        o_ref[...] = acc.astype(o_ref.dtype)
        _side_wait(side, srefs, jnp.logical_and(i == ni - 1, j == nj - 1))

    sd = side if side is not None else _Side([])
    in_specs, flat = [], []
    for a, b in pairs:
        k = a.shape[1]
        in_specs.append(pl.BlockSpec((tm, k), lambda i, j: (i, 0)))
        in_specs.append(pl.BlockSpec((tn, k), lambda i, j: (j, 0)) if nt else pl.BlockSpec((k, tn), lambda i, j: (0, j)))
        flat += [a, b]
    outs = pl.pallas_call(
        body, grid=(ni, nj), in_specs=in_specs + sd.in_specs,
        out_specs=[pl.BlockSpec((tm, tn), lambda i, j: (i, j))] + sd.out_specs,
        out_shape=[jax.ShapeDtypeStruct((m, n), out_dtype)] + sd.out_shape,
        scratch_shapes=sd.scratch if side is not None else [],
        compiler_params=_cp(("arbitrary", "arbitrary") if side is not None else ("parallel", "parallel")),
        name=name)(*flat, *sd.arrays)
    return outs[0] if side is None else (outs[0], list(outs[1:]))


def _mm_tn(a, b, out_dtype, name):
    m, k1 = a.shape
    n = b.shape[1]
    t1 = _pick(k1, 1408)
    tn = _pick(n, 1024)
    tm = max(t for t in (512, 1024, 1536) if m % t == 0)
    nsteps = m // tm

    def body(a_ref, b_ref, o_ref, acc_ref):
        t = pl.program_id(2)

        @pl.when(t == 0)
        def _():
            acc_ref[...] = jnp.zeros_like(acc_ref)

        acc_ref[...] += _tn(a_ref[...].astype(MXU_DTYPE), b_ref[...].astype(MXU_DTYPE))

        @pl.when(t == nsteps - 1)
        def _():
            o_ref[...] = acc_ref[...].astype(o_ref.dtype)

    return pl.pallas_call(
        body, grid=(k1 // t1, n // tn, nsteps),
        in_specs=[pl.BlockSpec((tm, t1), lambda i, j, t: (t, i)), pl.BlockSpec((tm, tn), lambda i, j, t: (t, j))],
        out_specs=pl.BlockSpec((t1, tn), lambda i, j, t: (i, j)),
        out_shape=jax.ShapeDtypeStruct((k1, n), out_dtype),
        scratch_shapes=[pltpu.VMEM((t1, tn), F32)],
        compiler_params=_cp(("parallel", "parallel", "arbitrary")), name=name)(a, b)


class _Layout:
    def __init__(self, bl, lat):
        self.bl, self.lat = bl, lat
        self.nlb = lat // TB
        self.nr = 1 + self.nlb
        self.nctx = bl
        self.nb = self.nr * bl
        self.nt = self.nb * TB
        self.ctx_row = bl

    def mod_tiles(self, mods):
        rows = np.array([[self.ctx_row if r == 0 else b for b in range(self.bl)] for r in range(self.nr)], np.int32)
        t = mods[rows].reshape(self.nr, self.bl, 6, D)
        return jnp.transpose(t, (0, 2, 1, 3)).reshape(self.nr * 6, self.bl, 1, D)


ST_FWD, ST_BWD = 4, 2


def _tok_spec(lay, st):
    nc = lay.bl // st
    return pl.BlockSpec((st * TB, D), lambda c, r: (r * nc + c, 0))


def _vec_spec():
    return pl.BlockSpec((1, D), lambda c, r: (0, 0))


def _mod_spec(st, k):
    return pl.BlockSpec((1, st, 1, D), lambda c, r: (r * 6 + k, c, 0, 0))


def _x_spec(lay, st):
    return pl.BlockSpec((st, 1, TB, D), lambda c, r: (c, jnp.maximum(r - 1, 0), 0, 0))


def _rows3(ref_or_val, st):
    return ref_or_val.reshape(st, TB, D)


def _acc_rows(acc_ref, val3, st, ctx_row):
    c, r = pl.program_id(0), pl.program_id(1)
    s = jnp.sum(val3, axis=1, keepdims=True)

    @pl.when(r == 0)
    def _():
        acc_ref[ctx_row:ctx_row + 1] += jnp.sum(s, axis=0, keepdims=True)

    @pl.when(r > 0)
    def _():
        acc_ref[pl.ds(c * st, st)] += s


def _first_step():
    return jnp.logical_and(pl.program_id(0) == 0, pl.program_id(1) == 0)


def _embed(lay, x, ctx, pe, side=None):
    st = ST_FWD
    bl, nlb = lay.bl, lay.nlb
    nc = bl // st

    def body(*refs):
        (x_ref, c_ref, pe_ref, o_ref), srefs = _side_split(refs, 3, 1, 0, side)
        c, r = pl.program_id(0), pl.program_id(1)
        _side_start(side, srefs, jnp.logical_and(c == 0, r == 0))

        @pl.when(r == 0)
        def _():
            o_ref[...] = c_ref[...].reshape(st * TB, D)

        @pl.when(r > 0)
        def _():
            o_ref[...] = (x_ref[...].reshape(st, TB, D) + pe_ref[...]).reshape(st * TB, D)

        _side_wait(side, srefs, jnp.logical_and(c == nc - 1, r == lay.nr - 1))

    sd = side if side is not None else _Side([])
    outs = pl.pallas_call(
        body, grid=(nc, lay.nr),
        in_specs=[_x_spec(lay, st), pl.BlockSpec((st, CTX, D), lambda c, r: (c, 0, 0)),
                  pl.BlockSpec((1, TB, D), lambda c, r: (jnp.maximum(r - 1, 0), 0, 0))] + sd.in_specs,
        out_specs=[_tok_spec(lay, st)] + sd.out_specs,
        out_shape=[jax.ShapeDtypeStruct((lay.nt, D), F32)] + sd.out_shape,
        scratch_shapes=sd.scratch if side is not None else [],
        compiler_params=_cp(("arbitrary", "arbitrary") if side is not None else ("parallel", "parallel")),
        name="embed")(x.reshape(bl, nlb, TB, D), ctx, pe.reshape(nlb, TB, D), *sd.arrays)
    return outs[0], list(outs[1:])


def _normmod_fwd(lay, x, gain, modt, ksh, ksc, name):
    st = ST_FWD

    def body(x_ref, g_ref, sh_ref, sc_ref, o_ref):
        xv = _rows3(x_ref[...], st)
        r = lax.rsqrt(jnp.mean(xv * xv, axis=-1, keepdims=True) + EPS)
        o_ref[...] = ((xv * r * g_ref[...]) * (1.0 + sc_ref[0]) + sh_ref[0]).reshape(st * TB, D).astype(o_ref.dtype)

    return pl.pallas_call(
        body, grid=(lay.bl // st, lay.nr),
        in_specs=[_tok_spec(lay, st), _vec_spec(), _mod_spec(st, ksh), _mod_spec(st, ksc)],
        out_specs=_tok_spec(lay, st), out_shape=jax.ShapeDtypeStruct((lay.nt, D), MXU_DTYPE),
        compiler_params=_cp(("parallel", "parallel")), name=name)(x, gain, modt, modt)


def _acc_out():
    return pl.BlockSpec((8, 1, D), lambda c, r: (0, 0, 0)), jax.ShapeDtypeStruct((8, 1, D), F32)


def _normmod_bwd(lay, x, dh, dx_in, gain, modt, ksc, name, latent_only=False):
    st = ST_BWD
    acc_spec, acc_shape = _acc_out()
    if latent_only:
        dx_spec, dx_shape = _x_spec(lay, st), jax.ShapeDtypeStruct((lay.bl, lay.nlb, TB, D), F32)
    else:
        dx_spec, dx_shape = _tok_spec(lay, st), jax.ShapeDtypeStruct((lay.nt, D), F32)

    def body(x_ref, dh_ref, dxi_ref, g_ref, sc_ref, dx_ref, dsh_ref, dsc_ref, dg_ref):
        xv = _rows3(x_ref[...], st)
        dhv = _rows3(dh_ref[...].astype(F32), st)
        g = g_ref[...]
        sc1 = 1.0 + sc_ref[0]
        r = lax.rsqrt(jnp.mean(xv * xv, axis=-1, keepdims=True) + EPS)
        xh = xv * r
        dxh = dhv * (g * sc1)
        dx = _rows3(dxi_ref[...], st) + r * (dxh - xh * jnp.mean(dxh * xh, axis=-1, keepdims=True))
        dx_ref[...] = dx.reshape(dx_ref.shape)

        @pl.when(_first_step())
        def _():
            dsh_ref[...] = jnp.zeros_like(dsh_ref)
            dsc_ref[...] = jnp.zeros_like(dsc_ref)
            dg_ref[...] = jnp.zeros_like(dg_ref)

        _acc_rows(dsh_ref, dhv, st, lay.ctx_row)
        _acc_rows(dsc_ref, dhv * (xh * g), st, lay.ctx_row)
        dg_ref[...] += jnp.sum((dhv * sc1 * xh).reshape(st * TB, D), axis=0, keepdims=True)

    return pl.pallas_call(
        body, grid=(lay.bl // st, lay.nr),
        in_specs=[_tok_spec(lay, st), _tok_spec(lay, st), _tok_spec(lay, st), _vec_spec(), _mod_spec(st, ksc)],
        out_specs=[dx_spec, acc_spec, acc_spec, _vec_spec()],
        out_shape=[dx_shape, acc_shape, acc_shape, jax.ShapeDtypeStruct((1, D), F32)],
        compiler_params=_cp(("arbitrary", "arbitrary")), name=name)(x, dh, dx_in, gain, modt)


def _resnorm_fwd(lay, x, m, gain, modt, kgate, name):
    st = ST_FWD

    def body(x_ref, m_ref, g_ref, gate_ref, o_ref):
        mv = _rows3(m_ref[...].astype(F32), st)
        r = lax.rsqrt(jnp.mean(mv * mv, axis=-1, keepdims=True) + EPS)
        o_ref[...] = x_ref[...] + (gate_ref[0] * (mv * r * g_ref[...])).reshape(st * TB, D)

    return pl.pallas_call(
        body, grid=(lay.bl // st, lay.nr),
        in_specs=[_tok_spec(lay, st), _tok_spec(lay, st), _vec_spec(), _mod_spec(st, kgate)],
        out_specs=_tok_spec(lay, st), out_shape=jax.ShapeDtypeStruct((lay.nt, D), F32),
        compiler_params=_cp(("parallel", "parallel")), name=name)(x, m, gain, modt)


def _resnorm_bwd(lay, dxn, m, gain, modt, kgate, name):
    st = ST_FWD
    acc_spec, acc_shape = _acc_out()

    def body(d_ref, m_ref, g_ref, gate_ref, dm_ref, dgate_ref, dg_ref):
        dv = _rows3(d_ref[...], st)
        mv = _rows3(m_ref[...].astype(F32), st)
        g = g_ref[...]
        r = lax.rsqrt(jnp.mean(mv * mv, axis=-1, keepdims=True) + EPS)
        xh = mv * r
        dy = dv * gate_ref[0]
        dxh = dy * g
        dm = r * (dxh - xh * jnp.mean(dxh * xh, axis=-1, keepdims=True))
        dm_ref[...] = dm.reshape(st * TB, D).astype(dm_ref.dtype)

        @pl.when(_first_step())
        def _():
            dgate_ref[...] = jnp.zeros_like(dgate_ref)
            dg_ref[...] = jnp.zeros_like(dg_ref)

        _acc_rows(dgate_ref, dv * (xh * g), st, lay.ctx_row)
        dg_ref[...] += jnp.sum((dy * xh).reshape(st * TB, D), axis=0, keepdims=True)

    return pl.pallas_call(
        body, grid=(lay.bl // st, lay.nr),
        in_specs=[_tok_spec(lay, st), _tok_spec(lay, st), _vec_spec(), _mod_spec(st, kgate)],
        out_specs=[_tok_spec(lay, st), acc_spec, _vec_spec()],
        out_shape=[jax.ShapeDtypeStruct((lay.nt, D), MXU_DTYPE), acc_shape, jax.ShapeDtypeStruct((1, D), F32)],
        compiler_params=_cp(("arbitrary", "arbitrary")), name=name)(dxn, m, gain, modt)


def _rms(v):
    return lax.rsqrt(jnp.mean(v * v, axis=-1, keepdims=True) + EPS)


def _resnorm_normmod_fwd(lay, x, m, gpost, gpre, modt, kgate, ksh, ksc, name):
    st = ST_BWD

    def body(x_ref, m_ref, gp_ref, gq_ref, gate_ref, sh_ref, sc_ref, x1_ref, h_ref):
        mv = _rows3(m_ref[...].astype(F32), st)
        x1 = _rows3(x_ref[...], st) + gate_ref[0] * (mv * _rms(mv) * gp_ref[...])
        x1_ref[...] = x1.reshape(st * TB, D)
        h = (x1 * _rms(x1) * gq_ref[...]) * (1.0 + sc_ref[0]) + sh_ref[0]
        h_ref[...] = h.reshape(st * TB, D).astype(h_ref.dtype)

    tok = _tok_spec(lay, st)
    return pl.pallas_call(
        body, grid=(lay.bl // st, lay.nr),
        in_specs=[tok, tok, _vec_spec(), _vec_spec(), _mod_spec(st, kgate), _mod_spec(st, ksh), _mod_spec(st, ksc)],
        out_specs=[tok, tok],
        out_shape=[jax.ShapeDtypeStruct((lay.nt, D), F32), jax.ShapeDtypeStruct((lay.nt, D), MXU_DTYPE)],
        compiler_params=_cp(("parallel", "parallel")), name=name)(x, m, gpost, gpre, modt, modt, modt)


def _normmod_resnorm_bwd(lay, x1, dh, dx_in, gpre, m, gpost, modt, ksc, kgate, name):
    st = ST_BWD
    acc_spec, acc_shape = _acc_out()

    def body(x_ref, dh_ref, dxi_ref, gq_ref, sc_ref, m_ref, gp_ref, gate_ref,
             dx_ref, dm_ref, dsh_ref, dsc_ref, dgq_ref, dgate_ref, dgp_ref):
        xv = _rows3(x_ref[...], st)
        dhv = _rows3(dh_ref[...].astype(F32), st)
        gq = gq_ref[...]
        sc1 = 1.0 + sc_ref[0]
        r = _rms(xv)
        xh = xv * r
        dxh = dhv * (gq * sc1)
        dx1 = _rows3(dxi_ref[...], st) + r * (dxh - xh * jnp.mean(dxh * xh, axis=-1, keepdims=True))
        dx_ref[...] = dx1.reshape(st * TB, D)
        mv = _rows3(m_ref[...].astype(F32), st)
        gp = gp_ref[...]
        rm = _rms(mv)
        mh = mv * rm
        dy = dx1 * gate_ref[0]
        dmh = dy * gp
        dm = rm * (dmh - mh * jnp.mean(dmh * mh, axis=-1, keepdims=True))
        dm_ref[...] = dm.reshape(st * TB, D).astype(dm_ref.dtype)

        @pl.when(_first_step())
        def _():
            for ref in (dsh_ref, dsc_ref, dgq_ref, dgate_ref, dgp_ref):
                ref[...] = jnp.zeros_like(ref)

        _acc_rows(dsh_ref, dhv, st, lay.ctx_row)
        _acc_rows(dsc_ref, dhv * (xh * gq), st, lay.ctx_row)
        dgq_ref[...] += jnp.sum((dhv * sc1 * xh).reshape(st * TB, D), axis=0, keepdims=True)
        _acc_rows(dgate_ref, dx1 * (mh * gp), st, lay.ctx_row)
        dgp_ref[...] += jnp.sum((dy * mh).reshape(st * TB, D), axis=0, keepdims=True)

    tok = _tok_spec(lay, st)
    vsh = jax.ShapeDtypeStruct((1, D), F32)
    return pl.pallas_call(
        body, grid=(lay.bl // st, lay.nr),
        in_specs=[tok, tok, tok, _vec_spec(), _mod_spec(st, ksc), tok, _vec_spec(), _mod_spec(st, kgate)],
        out_specs=[tok, tok, acc_spec, acc_spec, _vec_spec(), acc_spec, _vec_spec()],
        out_shape=[jax.ShapeDtypeStruct((lay.nt, D), F32), jax.ShapeDtypeStruct((lay.nt, D), MXU_DTYPE),
                   acc_shape, acc_shape, vsh, acc_shape, vsh],
        compiler_params=_cp(("arbitrary", "arbitrary")), name=name)(x1, dh, dx_in, gpre, modt, m, gpost, modt)


def _resnorm_loss(lay, x, f, gain, modt, kgate, tgt):
    st = ST_BWD

    def body(x_ref, f_ref, g_ref, gate_ref, t_ref, dx_ref, l_ref):
        r = pl.program_id(1)

        @pl.when(_first_step())
        def _():
            l_ref[...] = jnp.zeros_like(l_ref)

        @pl.when(r == 0)
        def _():
            dx_ref[...] = jnp.zeros_like(dx_ref)

        @pl.when(r > 0)
        def _():
            fv = _rows3(f_ref[...].astype(F32), st)
            y = _rows3(x_ref[...], st) + gate_ref[0] * (fv * _rms(fv) * g_ref[...])
            e = y - t_ref[...].reshape(st, TB, D)
            dx_ref[...] = (e * (1.0 / D)).reshape(st * TB, D)
            l_ref[...] += jnp.sum(e * e) * (0.5 / D)

    tok = _tok_spec(lay, st)
    return pl.pallas_call(
        body, grid=(lay.bl // st, lay.nr),
        in_specs=[tok, tok, _vec_spec(), _mod_spec(st, kgate), _x_spec(lay, st)],
        out_specs=[tok, pl.BlockSpec((8, 128), lambda c, r: (0, 0))],
        out_shape=[jax.ShapeDtypeStruct((lay.nt, D), F32), jax.ShapeDtypeStruct((8, 128), F32)],
        compiler_params=_cp(("arbitrary", "arbitrary")), name="loss")(
            x, f, gain, modt, tgt.reshape(lay.bl, lay.nlb, TB, D))


FF_TN = D_FF // 2
FF_CHUNKS = ((0, 512), (512, 512), (1024, 384))


def _ffn_up(h, wgt, wut, name, side=None):
    m = h.shape[0]
    tm, tn = min(512, m), FF_TN
    ni, nj = m // tm, D_FF // tn

    def body(*refs):
        (h_ref, wg_ref, wu_ref, g_ref, u_ref, a_ref), srefs = _side_split(refs, 3, 3, 0, side)
        j, i = pl.program_id(0), pl.program_id(1)
        _side_start(side, srefs, jnp.logical_and(i == 0, j == 0))
        hv = h_ref[...]
        for c0, cw in FF_CHUNKS:
            g = _nt(hv, wg_ref[c0:c0 + cw, :])
            u = _nt(hv, wu_ref[c0:c0 + cw, :])
            g_ref[:, c0:c0 + cw] = g.astype(g_ref.dtype)
            u_ref[:, c0:c0 + cw] = u.astype(u_ref.dtype)
            a_ref[:, c0:c0 + cw] = (g * _sigmoid(g) * u).astype(a_ref.dtype)
        _side_wait(side, srefs, jnp.logical_and(i == ni - 1, j == nj - 1))

    sd = side if side is not None else _Side([])
    osp = pl.BlockSpec((tm, tn), lambda j, i: (i, j))
    osh = jax.ShapeDtypeStruct((m, D_FF), MXU_DTYPE)
    outs = pl.pallas_call(
        body, grid=(nj, ni),
        in_specs=[pl.BlockSpec((tm, D), lambda j, i: (i, 0)), pl.BlockSpec((tn, D), lambda j, i: (j, 0)),
                  pl.BlockSpec((tn, D), lambda j, i: (j, 0))] + sd.in_specs,
        out_specs=[osp, osp, osp] + sd.out_specs, out_shape=[osh, osh, osh] + sd.out_shape,
        scratch_shapes=sd.scratch if side is not None else [],
        compiler_params=_cp(("arbitrary", "arbitrary") if side is not None else ("parallel", "parallel")),
        name=name)(h, wgt, wut, *sd.arrays)
    return outs[0], outs[1], outs[2], list(outs[3:])


def _ffn_down_bwd(df, wd, g, u, name, side=None):
    m = df.shape[0]
    tm, tn = min(512, m), FF_TN
    ni, nj = m // tm, D_FF // tn

    def body(*refs):
        (df_ref, wd_ref, g_ref, u_ref, dg_ref, du_ref), srefs = _side_split(refs, 4, 2, 0, side)
        j, i = pl.program_id(0), pl.program_id(1)
        _side_start(side, srefs, jnp.logical_and(i == 0, j == 0))
        dfv = df_ref[...]
        for c0, cw in FF_CHUNKS:
            da = _nt(dfv, wd_ref[c0:c0 + cw, :])
            gv = g_ref[:, c0:c0 + cw].astype(F32)
            uv = u_ref[:, c0:c0 + cw].astype(F32)
            s = _sigmoid(gv)
            dg_ref[:, c0:c0 + cw] = (da * uv * (s * (1.0 + gv * (1.0 - s)))).astype(dg_ref.dtype)
            du_ref[:, c0:c0 + cw] = (da * gv * s).astype(du_ref.dtype)
        _side_wait(side, srefs, jnp.logical_and(i == ni - 1, j == nj - 1))

    sd = side if side is not None else _Side([])
    osp = pl.BlockSpec((tm, tn), lambda j, i: (i, j))
    osh = jax.ShapeDtypeStruct((m, D_FF), MXU_DTYPE)
    outs = pl.pallas_call(
        body, grid=(nj, ni),
        in_specs=[pl.BlockSpec((tm, D), lambda j, i: (i, 0)), pl.BlockSpec((tn, D), lambda j, i: (j, 0)), osp, osp]
        + sd.in_specs,
        out_specs=[osp, osp] + sd.out_specs, out_shape=[osh, osh] + sd.out_shape,
        scratch_shapes=sd.scratch if side is not None else [],
        compiler_params=_cp(("arbitrary", "arbitrary") if side is not None else ("parallel", "parallel")),
        name=name)(df, wd, g, u, *sd.arrays)
    return outs[0], outs[1], list(outs[2:])


def _head_masks(shape):
    lane = lax.broadcasted_iota(jnp.int32, shape, 1)
    return [jnp.logical_and(lane >= 64 * h, lane < 64 * h + 64) for h in range(4)]


def _head_mean(x, masks):
    out = jnp.zeros_like(x)
    for mk in masks:
        s = jnp.sum(jnp.where(mk, x, 0.0), axis=-1, keepdims=True) * (1.0 / 64.0)
        out = jnp.where(mk, s, out)
    return out


def _gate_common(z, masks):
    zg = _gelu(z)
    u = zg[:, :A_W]
    v = zg[:, A_W:]
    mu = _head_mean(v, masks)
    vc = v - mu
    rstd = lax.rsqrt(_head_mean(vc * vc, masks) + EPS)
    return u, vc * rstd, rstd


def _gate_s(vn, ws_ref, bias, masks):
    parts = []
    for c in range(TB // CHUNK):
        vc = vn[c * CHUNK:(c + 1) * CHUNK]
        s = bias
        for h in range(4):
            s = s + _nn(ws_ref[h], jnp.where(masks[h][:CHUNK], vc, 0.0).astype(MXU_DTYPE))
        parts.append(s)
    return jnp.concatenate(parts, axis=0)


MT = 4


def _blocks():
    return [pl.ds(s * TB, TB) for s in range(MT)]


def _gate_fwd(lay, z, ws, bias, name):
    def body(z_ref, ws_ref, b_ref, o_ref):
        masks = _head_masks((TB, A_W))
        for sl in _blocks():
            u, vn, _ = _gate_common(z_ref[sl, :].astype(F32), masks)
            o_ref[sl, :] = (u * _gate_s(vn, ws_ref, b_ref[...], masks)).astype(o_ref.dtype)

    return pl.pallas_call(
        body, grid=(lay.nb // MT,),
        in_specs=[pl.BlockSpec((MT * TB, 2 * A_W), lambda j: (j, 0)), pl.BlockSpec((4, CHUNK, CHUNK), lambda j: (0, 0, 0)),
                  pl.BlockSpec((CHUNK, A_W), lambda j: (0, 0))],
        out_specs=pl.BlockSpec((MT * TB, A_W), lambda j: (j, 0)),
        out_shape=jax.ShapeDtypeStruct((lay.nt, A_W), MXU_DTYPE),
        compiler_params=_cp(("parallel",)), name=name)(z, ws, bias)


def _gate_bwd(lay, z, da, ws, wst, bias, name):
    def body(z_ref, da_ref, ws_ref, wst_ref, b_ref, dz_ref, dws_ref, db_ref):
        j = pl.program_id(0)

        @pl.when(j == 0)
        def _():
            dws_ref[...] = jnp.zeros_like(dws_ref)
            db_ref[...] = jnp.zeros_like(db_ref)

        masks = _head_masks((TB, A_W))
        for blk in _blocks():
            zv = z_ref[blk, :].astype(F32)
            u, vn, rstd = _gate_common(zv, masks)
            s = _gate_s(vn, ws_ref, b_ref[...], masks)
            dav = da_ref[blk, :].astype(F32)
            du = dav * s
            ds = dav * u
            dvn_parts = []
            for c in range(TB // CHUNK):
                sl = slice(c * CHUNK, (c + 1) * CHUNK)
                ds_c = ds[sl]
                vn_c = vn[sl].astype(MXU_DTYPE)
                db_ref[...] += ds_c
                ds_b = ds_c.astype(MXU_DTYPE)
                dvn_c = jnp.zeros((CHUNK, A_W), F32)
                for h in range(4):
                    mk = masks[h][:CHUNK]
                    dws_ref[h] += _nt(jnp.where(mk, ds_c, 0.0).astype(MXU_DTYPE), vn_c)
                    dvn_c = dvn_c + jnp.where(mk, _nn(wst_ref[h], ds_b), 0.0)
                dvn_parts.append(dvn_c)
            dvn = jnp.concatenate(dvn_parts, axis=0)
            dv = rstd * (dvn - _head_mean(dvn, masks) - vn * _head_mean(dvn * vn, masks))
            gg = _gelu_grad(zv)
            dz_ref[blk, :A_W] = (du * gg[:, :A_W]).astype(dz_ref.dtype)
            dz_ref[blk, A_W:] = (dv * gg[:, A_W:]).astype(dz_ref.dtype)

    return pl.pallas_call(
        body, grid=(lay.nb // MT,),
        in_specs=[pl.BlockSpec((MT * TB, 2 * A_W), lambda j: (j, 0)), pl.BlockSpec((MT * TB, A_W), lambda j: (j, MCAT_A)),
                  pl.BlockSpec((4, CHUNK, CHUNK), lambda j: (0, 0, 0)), pl.BlockSpec((4, CHUNK, CHUNK), lambda j: (0, 0, 0)),
                  pl.BlockSpec((CHUNK, A_W), lambda j: (0, 0))],
        out_specs=[pl.BlockSpec((MT * TB, 2 * A_W), lambda j: (j, 0)), pl.BlockSpec((4, CHUNK, CHUNK), lambda j: (0, 0, 0)),
                   pl.BlockSpec((CHUNK, A_W), lambda j: (0, 0))],
        out_shape=[jax.ShapeDtypeStruct((lay.nt, 2 * A_W), MXU_DTYPE), jax.ShapeDtypeStruct((4, CHUNK, CHUNK), F32),
                   jax.ShapeDtypeStruct((CHUNK, A_W), F32)],
        compiler_params=_cp(("arbitrary",)), name=name)(z, da, ws, wst, bias)


def _band_constants():
    bands = np.zeros((2, 4, TB, TB), np.float32)
    inv = np.zeros((2, 4, TB, 1), np.float32)
    for kind, n in ((0, GRID_W), (1, TB)):
        for i, w in enumerate(POOL_WINDOWS):
            for t in range(TB):
                base, tt = (t // n) * n, t % n
                lo = min(max(tt - w // 2, 0), n)
                hi = min(max(tt - w // 2 + w, 0), n)
                bands[kind, i, t, base + lo:base + hi] = 1.0
                inv[kind, i, t, 0] = 1.0 / (hi - lo)
    return bands, inv


def _split3(x):
    a = x.astype(MXU_DTYPE)
    r1 = x - a.astype(F32)
    b = r1.astype(MXU_DTYPE)
    c = (r1 - b.astype(F32)).astype(MXU_DTYPE)
    return a, b, c


def _window_apply(band_ref, inv_ref, x, masks, transpose, mxu_exact=False):
    out = jnp.zeros_like(x)
    for i in range(4):
        xi = x * inv_ref[0, i] if transpose else x
        acc = None
        for part in ((xi.astype(MXU_DTYPE),) if mxu_exact else _split3(xi)):
            r = _tn(band_ref[0, i], part) if transpose else _nn(band_ref[0, i], part)
            acc = r if acc is None else acc + r
        if not transpose:
            acc = acc * inv_ref[0, i]
        out = jnp.where(masks[i], acc, out)
    return out


def _pool_specs(lay):
    kind = lambda j: jnp.where(j < lay.nctx // MT, 1, 0)
    return [pl.BlockSpec((1, 4, TB, TB), lambda j: (kind(j), 0, 0, 0)), pl.BlockSpec((1, 4, TB, 1), lambda j: (kind(j), 0, 0, 0))]


def _pool_fwd(lay, z, bands, inv, pw, scale, name):
    def body(p_ref, band_ref, inv_ref, pw_ref, sc_ref, o_ref):
        masks = _head_masks((TB, C_W))
        for blk in _blocks():
            p = p_ref[blk, :].astype(F32)
            diff = _window_apply(band_ref, inv_ref, p, masks, False, mxu_exact=True) - p
            o_ref[blk, :] = (_nn(diff.astype(MXU_DTYPE), pw_ref[...]) * sc_ref[...]).astype(o_ref.dtype)

    return pl.pallas_call(
        body, grid=(lay.nb // MT,),
        in_specs=[pl.BlockSpec((MT * TB, C_W), lambda j: (j, 4))] + _pool_specs(lay)
        + [pl.BlockSpec((C_W, C_W), lambda j: (0, 0)), pl.BlockSpec((1, C_W), lambda j: (0, 0))],
        out_specs=pl.BlockSpec((MT * TB, C_W), lambda j: (j, 0)),
        out_shape=jax.ShapeDtypeStruct((lay.nt, C_W), MXU_DTYPE),
        compiler_params=_cp(("parallel",)), name=name)(z, bands, inv, pw, scale)


def _pool_bwd(lay, z, dc, bands, inv, pw, scale, name):
    def body(p_ref, dc_ref, band_ref, inv_ref, pw_ref, sc_ref, dp_ref, dpw_ref, dsc_ref):
        j = pl.program_id(0)

        @pl.when(j == 0)
        def _():
            dpw_ref[...] = jnp.zeros_like(dpw_ref)
            dsc_ref[...] = jnp.zeros_like(dsc_ref)

        masks = _head_masks((TB, C_W))
        for blk in _blocks():
            p = p_ref[blk, :].astype(F32)
            dcv = dc_ref[blk, :].astype(F32)
            diff = _window_apply(band_ref, inv_ref, p, masks, False, mxu_exact=True) - p
            diff_b = diff.astype(MXU_DTYPE)
            pre = _nn(diff_b, pw_ref[...])
            dsc_ref[...] += jnp.sum(dcv * pre, axis=0, keepdims=True)
            dpre = dcv * sc_ref[...]
            dpre_b = dpre.astype(MXU_DTYPE)
            dpw_ref[...] += _tn(diff_b, dpre_b)
            ddiff = _nt(dpre_b, pw_ref[...])
            dp_ref[blk, :] = (_window_apply(band_ref, inv_ref, ddiff, masks, True) - ddiff).astype(dp_ref.dtype)

    return pl.pallas_call(
        body, grid=(lay.nb // MT,),
        in_specs=[pl.BlockSpec((MT * TB, C_W), lambda j: (j, 4)), pl.BlockSpec((MT * TB, C_W), lambda j: (j, MCAT_C))]
        + _pool_specs(lay)
        + [pl.BlockSpec((C_W, C_W), lambda j: (0, 0)), pl.BlockSpec((1, C_W), lambda j: (0, 0))],
        out_specs=[pl.BlockSpec((MT * TB, C_W), lambda j: (j, 0)), pl.BlockSpec((C_W, C_W), lambda j: (0, 0)),
                   pl.BlockSpec((1, C_W), lambda j: (0, 0))],
        out_shape=[jax.ShapeDtypeStruct((lay.nt, C_W), MXU_DTYPE), jax.ShapeDtypeStruct((C_W, C_W), F32),
                   jax.ShapeDtypeStruct((1, C_W), F32)],
        compiler_params=_cp(("arbitrary",)), name=name)(z, dc, bands, inv, pw, scale)


def _disc_math(lr, li, ldt, br, bi):
    dt = jnp.exp(ldt)
    e = jnp.exp(lr * dt)
    ar = e * jnp.cos(li * dt)
    ai = e * jnp.sin(li * dt)
    nr, ni = ar - 1.0, ai
    den = lr * lr + li * li
    qr = (nr * lr + ni * li) / den
    qi = (ni * lr - nr * li) / den
    return ar, ai, qr * br - qi * bi, qr * bi + qi * br


def _disc_fwd(lrx, lix, ldtx, brt, bit, name):
    def body(lr_ref, li_ref, ldt_ref, br_ref, bi_ref, ar_ref, ai_ref, obr_ref, obi_ref):
        ar, ai, obr, obi = _disc_math(lr_ref[...], li_ref[...], ldt_ref[...], br_ref[...], bi_ref[...])
        ar_ref[...] = ar
        ai_ref[...] = ai
        obr_ref[...] = obr
        obi_ref[...] = obi

    sh = jax.ShapeDtypeStruct(lrx.shape, F32)
    return pl.pallas_call(body, out_shape=[sh, sh, sh, sh], name=name)(lrx, lix, ldtx, brt, bit)


def _disc_bwd(lrx, lix, ldtx, brt, bit, dar, dai, dbr, dbi, name):
    nrow = lrx.shape[0] // SSM_H

    def body(lr_ref, li_ref, ldt_ref, br_ref, bi_ref, dar_ref, dai_ref, dbr_ref, dbi_ref,
             glr_ref, gli_ref, gdt_ref, gbr_ref, gbi_ref):
        _, vjp = jax.vjp(_disc_math, lr_ref[...], li_ref[...], ldt_ref[...], br_ref[...], bi_ref[...])
        glr, gli, gdt, gbr, gbi = vjp((dar_ref[...], dai_ref[...], dbr_ref[...], dbi_ref[...]))
        glr_ref[...] = jnp.sum(glr.reshape(nrow, SSM_H, SSM_P), axis=1)
        gli_ref[...] = jnp.sum(gli.reshape(nrow, SSM_H, SSM_P), axis=1)
        gdt_ref[...] = jnp.sum(jnp.sum(gdt.reshape(nrow, SSM_H, SSM_P), axis=1), axis=-1, keepdims=True)
        gbr_ref[...] = gbr
        gbi_ref[...] = gbi

    small = jax.ShapeDtypeStruct((nrow, SSM_P), F32)
    big = jax.ShapeDtypeStruct(lrx.shape, F32)
    return pl.pallas_call(body, out_shape=[small, small, jax.ShapeDtypeStruct((nrow, 1), F32), big, big],
                          name=name)(lrx, lix, ldtx, brt, bit, dar, dai, dbr, dbi)


HS = 1024
GQ, QC, QS = 8, 128, 512
LC = QS
SCAN_UNROLL = ST


def _scan_steps(step, carry):
    if SCAN_UNROLL >= ST:
        for s in range(ST):
            carry = step(s, carry)
        return carry

    def body(i, c):
        for j in range(SCAN_UNROLL):
            c = step(i * SCAN_UNROLL + j, c)
        return c

    return lax.fori_loop(0, ST // SCAN_UNROLL, body, carry)


def _tile_row(s):
    return s * 8 if isinstance(s, int) else pl.multiple_of(s * 8, 8)


def _dir_cat(x, d0, qq):
    xq = x[:, QC * qq:QC * qq + QC]
    zero = jnp.zeros_like(xq)
    return jnp.concatenate([jnp.where(d0, xq, zero), jnp.where(d0, zero, xq)], axis=1)


def _dir_pick(x, d0):
    return jnp.where(d0, x[:, :QC], x[:, QC:])


def _d0_rows(n):
    row = lax.broadcasted_iota(jnp.int32, (n, 1), 0)
    return jnp.bitwise_and(row, 4) == 0


def _scan_perm(bl):
    n = 2 * bl * ST
    p = np.zeros((n, n), np.float32)
    for s in range(ST):
        for d in range(2):
            for b in range(bl):
                t = s if d == 0 else ST - 1 - s
                p[s * 2 * bl + d * bl + b, d * bl * ST + b * ST + t] = 1.0
    return p


def _scan_maps(lay):
    spc = TB // ST
    nlc = lay.nlb * spc

    def fwd(k):
        return k // spc, k % spc

    def rev(k):
        cpos = nlc - 1 - jnp.maximum(k - spc, 0)
        return jnp.where(k < spc, 0, 1 + cpos // spc), jnp.where(k < spc, spc - 1 - k, cpos % spc)

    return fwd, rev


def _pack_rows(f_ref, r_ref, p_ref, rc):
    st = jnp.concatenate([f_ref[0].reshape(rc // 2, 256), r_ref[0].reshape(rc // 2, 256)], axis=0).astype(MXU_DTYPE)
    return _nn(p_ref[...], st).astype(MXU_DTYPE)


def _side_split(refs, n_in, n_out, n_scr, side):
    ns = side.n if side is not None else 0
    ins, sin = refs[:n_in], refs[n_in:n_in + ns]
    o0 = n_in + ns
    outs, sout = refs[o0:o0 + n_out], refs[o0 + n_out:o0 + n_out + ns]
    s0 = o0 + n_out + ns
    return ins + outs + refs[s0:s0 + n_scr], (sin, sout, refs[s0 + n_scr:])


def _side_start(side, srefs, first):
    if side is not None:
        @pl.when(first)
        def _():
            side.start(*srefs)


def _side_wait(side, srefs, last):
    if side is not None:
        @pl.when(last)
        def _():
            side.wait(*srefs)


def _ssm_fwd(lay, z, perm, bh, ch, ar8, ai8, name, side=None):
    bl = lay.bl
    rc = ST * 2 * bl
    nch = lay.nr * (TB // ST)
    fwd, rev = _scan_maps(lay)
    z4 = z.reshape(lay.nr, bl, TB, z.shape[1])

    def body(*refs):
        own, srefs = _side_split(refs, 7, 4, 2, side)
        uf_ref, ur_ref, p_ref, bh_ref, ch_ref, ar_ref, ai_ref, yf_ref, yr_ref, hst_ref, hsv_ref, hs, hc = own
        f, k = pl.program_id(0), pl.program_id(1)
        _side_start(side, srefs, jnp.logical_and(f == 0, k == 0))

        @pl.when(k == 0)
        def _():
            hc[...] = jnp.zeros_like(hc)

        hst_ref[0] = hc[...]
        d0 = _d0_rows(rc)
        uv = _pack_rows(uf_ref, ur_ref, p_ref, rc)
        for q in range(2):
            cr, ci = 2 * QS * q, 2 * QS * q + QS
            hs[:, cr:cr + 2 * QS] = _nn(_dir_cat(uv, d0, q), bh_ref[q])
            ar = ar_ref[:, QS * q:QS * q + QS]
            ai = ai_ref[:, QS * q:QS * q + QS]

            def step(s, carry, cr=cr, ci=ci, ar=ar, ai=ai):
                hr, hi = carry
                base = _tile_row(s)
                nr = ar * hr - ai * hi + hs[pl.ds(base, 8), cr:cr + LC]
                ni = ar * hi + ai * hr + hs[pl.ds(base, 8), ci:ci + LC]
                hs[pl.ds(base, 8), cr:cr + LC] = nr
                hs[pl.ds(base, 8), ci:ci + LC] = ni
                return nr, ni

            hr, hi = _scan_steps(step, (hc[:, cr:cr + LC], hc[:, ci:ci + LC]))
            hc[:, cr:cr + LC] = hr
            hc[:, ci:ci + LC] = hi
        hsv_ref[0] = hs[...].astype(hsv_ref.dtype)
        yi = jnp.concatenate(
            [_dir_pick(_nn(hsv_ref[0, :, 2 * QS * q:2 * QS * (q + 1)], ch_ref[q]), d0) for q in range(2)], axis=1)
        yd = _tn(p_ref[...], yi.astype(MXU_DTYPE))
        yf_ref[0] = yd[:rc // 2].reshape(bl, ST, 256).astype(yf_ref.dtype)
        yr_ref[0] = yd[rc // 2:].reshape(bl, ST, 256).astype(yr_ref.dtype)
        _side_wait(side, srefs, jnp.logical_and(f == 1, k == nch - 1))

    sd = side if side is not None else _Side([])
    blk = (1, bl, ST, 256)
    ysh = jax.ShapeDtypeStruct((lay.nr, bl, TB, B_W), MXU_DTYPE)
    outs = pl.pallas_call(
        body, grid=(2, nch),
        in_specs=[pl.BlockSpec(blk, lambda f, k: (fwd(k)[0], 0, fwd(k)[1], 2 + f)),
                  pl.BlockSpec(blk, lambda f, k: (rev(k)[0], 0, rev(k)[1], 2 + f)),
                  pl.BlockSpec((rc, rc), lambda f, k: (0, 0)),
                  pl.BlockSpec((2, 2 * QC, 2 * QS), lambda f, k: (f, 0, 0)),
                  pl.BlockSpec((2, 2 * QS, 2 * QC), lambda f, k: (f, 0, 0)),
                  pl.BlockSpec((8, HS), lambda f, k: (0, f)), pl.BlockSpec((8, HS), lambda f, k: (0, f))] + sd.in_specs,
        out_specs=[pl.BlockSpec(blk, lambda f, k: (fwd(k)[0], 0, fwd(k)[1], f)),
                   pl.BlockSpec(blk, lambda f, k: (rev(k)[0], 0, rev(k)[1], f)),
                   pl.BlockSpec((1, 8, 2 * HS), lambda f, k: (k, 0, f)),
                   pl.BlockSpec((1, rc, 2 * HS), lambda f, k: (k, 0, f))] + sd.out_specs,
        out_shape=[ysh, ysh, jax.ShapeDtypeStruct((nch, 8, 4 * HS), F32),
                   jax.ShapeDtypeStruct((nch, rc, 4 * HS), MXU_DTYPE)] + sd.out_shape,
        scratch_shapes=[pltpu.VMEM((rc, 2 * HS), F32), pltpu.VMEM((8, 2 * HS), F32)] + (sd.scratch if side is not None else []),
        compiler_params=_cp(("arbitrary", "arbitrary")), name=name)(z4, z4, perm, bh, ch, ar8, ai8, *sd.arrays)
    yf, yr, hst, hsv = outs[:4]
    return yf.reshape(lay.nt, B_W), yr.reshape(lay.nt, B_W), (hst, hsv), list(outs[4:])


def _ssm_bwd(lay, z, dy, perm, hst, bh, ch, ar8, ai8, name, side=None):
    bl = lay.bl
    rc = ST * 2 * bl
    nch = lay.nr * (TB // ST)
    fwd, rev = _scan_maps(lay)
    z4 = z.reshape(lay.nr, bl, TB, z.shape[1])
    dy4 = dy.reshape(lay.nr, bl, TB, B_W)

    hst, hsv = hst

    def body(*refs):
        own, srefs = _side_split(refs, 11, 6, 5, side)
        (uf_ref, ur_ref, dyf_ref, dyr_ref, p_ref, hst_ref, hsv_ref, bh_ref, ch_ref, ar_ref, ai_ref,
         duf_ref, dur_ref, dbh_ref, dch_ref, dar_ref, dai_ref, hs, es, ec, accr, acci) = own
        f, k = pl.program_id(0), pl.program_id(1)
        _side_start(side, srefs, jnp.logical_and(f == 0, k == 0))

        @pl.when(k == 0)
        def _():
            ec[...] = jnp.zeros_like(ec)
            accr[...] = jnp.zeros_like(accr)
            acci[...] = jnp.zeros_like(acci)
            dbh_ref[...] = jnp.zeros_like(dbh_ref)
            dch_ref[...] = jnp.zeros_like(dch_ref)

        d0 = _d0_rows(rc)
        uv = _pack_rows(uf_ref, ur_ref, p_ref, rc)
        dyv = _pack_rows(dyf_ref, dyr_ref, p_ref, rc)

        hs[0:8, :] = hst_ref[0]
        hs[8:, :] = hsv_ref[0].astype(F32)
        ucat, dycat = [], []
        for q in range(2):
            cr = 2 * QS * q
            ucat.append(_dir_cat(uv, d0, q))
            dycat.append(_dir_cat(dyv, d0, q))
            dch_ref[q] += _tn(hsv_ref[0, :, cr:cr + 2 * QS], dycat[q])
            es[:, cr:cr + 2 * QS] = _nt(dycat[q], ch_ref[q])

        dui = []
        for q in range(2):
            cr, ci = 2 * QS * q, 2 * QS * q + QS
            ar = ar_ref[:, QS * q:QS * q + QS]
            ai = ai_ref[:, QS * q:QS * q + QS]

            def bstep(i, carry, cr=cr, ci=ci, ar=ar, ai=ai):
                er, ei, sr, si = carry
                base = _tile_row(ST - 1 - i)
                ner = es[pl.ds(base, 8), cr:cr + LC] + ar * er + ai * ei
                nei = es[pl.ds(base, 8), ci:ci + LC] - ai * er + ar * ei
                es[pl.ds(base, 8), cr:cr + LC] = ner
                es[pl.ds(base, 8), ci:ci + LC] = nei
                hpr = hs[pl.ds(base, 8), cr:cr + LC]
                hpi = hs[pl.ds(base, 8), ci:ci + LC]
                return ner, nei, sr + ner * hpr + nei * hpi, si - ner * hpi + nei * hpr

            lo = QS * q
            er, ei, sr, si = _scan_steps(
                bstep, (ec[:, cr:cr + LC], ec[:, ci:ci + LC], accr[:, lo:lo + LC], acci[:, lo:lo + LC]))
            ec[:, cr:cr + LC] = er
            ec[:, ci:ci + LC] = ei
            accr[:, lo:lo + LC] = sr
            acci[:, lo:lo + LC] = si
            eb = es[:, cr:cr + 2 * QS].astype(MXU_DTYPE)
            dui.append(_dir_pick(_nt(eb, bh_ref[q]), d0))
            dbh_ref[q] += _tn(ucat[q], eb)

        dud = _tn(p_ref[...], jnp.concatenate(dui, axis=1).astype(MXU_DTYPE))
        duf_ref[0] = dud[:rc // 2].reshape(bl, ST, 256).astype(duf_ref.dtype)
        dur_ref[0] = dud[rc // 2:].reshape(bl, ST, 256).astype(dur_ref.dtype)

        @pl.when(k == nch - 1)
        def _():
            for d in range(2):
                dar_ref[d:d + 1, :] = jnp.sum(accr[4 * d:4 * d + 4, :], axis=0, keepdims=True)
                dai_ref[d:d + 1, :] = jnp.sum(acci[4 * d:4 * d + 4, :], axis=0, keepdims=True)

        _side_wait(side, srefs, jnp.logical_and(f == 1, k == nch - 1))

    sd = side if side is not None else _Side([])
    last = lambda k: nch - 1 - k
    blk = (1, bl, ST, 256)
    fspec = lambda c0: pl.BlockSpec(blk, lambda f, k: (fwd(last(k))[0], 0, fwd(last(k))[1], c0 + f))
    rspec = lambda c0: pl.BlockSpec(blk, lambda f, k: (rev(last(k))[0], 0, rev(last(k))[1], c0 + f))
    dush = jax.ShapeDtypeStruct((lay.nr, bl, TB, B_W), MXU_DTYPE)
    outs = pl.pallas_call(
        body, grid=(2, nch),
        in_specs=[fspec(2), rspec(2), fspec(0), rspec(0),
                  pl.BlockSpec((rc, rc), lambda f, k: (0, 0)),
                  pl.BlockSpec((1, 8, 2 * HS), lambda f, k: (last(k), 0, f)),
                  pl.BlockSpec((1, rc, 2 * HS), lambda f, k: (last(k), 0, f)),
                  pl.BlockSpec((2, 2 * QC, 2 * QS), lambda f, k: (f, 0, 0)),
                  pl.BlockSpec((2, 2 * QS, 2 * QC), lambda f, k: (f, 0, 0)),
                  pl.BlockSpec((8, HS), lambda f, k: (0, f)), pl.BlockSpec((8, HS), lambda f, k: (0, f))] + sd.in_specs,
        out_specs=[fspec(0), rspec(0),
                   pl.BlockSpec((2, 2 * QC, 2 * QS), lambda f, k: (f, 0, 0)),
                   pl.BlockSpec((2, 2 * QS, 2 * QC), lambda f, k: (f, 0, 0)),
                   pl.BlockSpec((2, HS), lambda f, k: (0, f)), pl.BlockSpec((2, HS), lambda f, k: (0, f))] + sd.out_specs,
        out_shape=[dush, dush, jax.ShapeDtypeStruct((4, 2 * QC, 2 * QS), F32),
                   jax.ShapeDtypeStruct((4, 2 * QS, 2 * QC), F32), jax.ShapeDtypeStruct((2, 2 * HS), F32),
                   jax.ShapeDtypeStruct((2, 2 * HS), F32)] + sd.out_shape,
        scratch_shapes=[pltpu.VMEM((rc + 8, 2 * HS), F32), pltpu.VMEM((rc, 2 * HS), F32), pltpu.VMEM((8, 2 * HS), F32),
                        pltpu.VMEM((8, HS), F32), pltpu.VMEM((8, HS), F32)] + (sd.scratch if side is not None else []),
        compiler_params=_cp(("arbitrary", "arbitrary")), name=name)(
            z4, z4, dy4, dy4, perm, hst, hsv, bh, ch, ar8, ai8, *sd.arrays)
    duf, dur, dbh, dch, dar, dai = outs[:6]
    return duf.reshape(lay.nt, B_W), dur.reshape(lay.nt, B_W), dbh, dch, dar, dai, list(outs[6:])


def _glu_fwd(lay, z, yf, yr, dvec, wglu, bglu, name):
    def body(u_ref, yf_ref, yr_ref, d_ref, w_ref, b_ref, o_ref, y_ref):
        y = yf_ref[...].astype(F32) + yr_ref[...].astype(F32) + d_ref[...] * u_ref[...].astype(F32)
        y_ref[...] = y
        g = _gelu(y)
        pre = _nn(g.astype(MXU_DTYPE), w_ref[...]) + b_ref[...]
        o_ref[...] = (g * _sigmoid(pre)).astype(o_ref.dtype)

    tok = pl.BlockSpec((MT * TB, B_W), lambda j: (j, 0))
    vec = pl.BlockSpec((1, B_W), lambda j: (0, 0))
    return pl.pallas_call(
        body, grid=(lay.nb // MT,),
        in_specs=[pl.BlockSpec((MT * TB, B_W), lambda j: (j, 1)), tok, tok, vec,
                  pl.BlockSpec((B_W, B_W), lambda j: (0, 0)), vec],
        out_specs=[tok, tok],
        out_shape=[jax.ShapeDtypeStruct((lay.nt, B_W), MXU_DTYPE), jax.ShapeDtypeStruct((lay.nt, B_W), F32)],
        compiler_params=_cp(("parallel",)), name=name)(z, yf, yr, dvec, wglu, bglu)


def _glu_bwd(lay, z, y, ds, dvec, wglu, bglu, name):
    def body(u_ref, y_ref, ds_ref, d_ref, w_ref, b_ref, dy_ref, dud_ref, dw_ref, db_ref, dd_ref):
        j = pl.program_id(0)

        @pl.when(j == 0)
        def _():
            dw_ref[...] = jnp.zeros_like(dw_ref)
            db_ref[...] = jnp.zeros_like(db_ref)
            dd_ref[...] = jnp.zeros_like(dd_ref)

        yv = y_ref[...]
        g = _gelu(yv)
        gb = g.astype(MXU_DTYPE)
        sg = _sigmoid(_nn(gb, w_ref[...]) + b_ref[...])
        dsv = ds_ref[...].astype(F32)
        dpre = dsv * g * sg * (1.0 - sg)
        dpre_b = dpre.astype(MXU_DTYPE)
        dg = dsv * sg + _nt(dpre_b, w_ref[...])
        dw_ref[...] += _tn(gb, dpre_b)
        db_ref[...] += jnp.sum(dpre, axis=0, keepdims=True)
        dy = dg * _gelu_grad(yv)
        dy_ref[...] = dy.astype(dy_ref.dtype)
        dd_ref[...] += jnp.sum(dy * u_ref[...].astype(F32), axis=0, keepdims=True)
        dud_ref[...] = (dy * d_ref[...]).astype(dud_ref.dtype)

    tok = pl.BlockSpec((MT * TB, B_W), lambda j: (j, 0))
    vec = pl.BlockSpec((1, B_W), lambda j: (0, 0))
    mat = pl.BlockSpec((B_W, B_W), lambda j: (0, 0))
    vsh = jax.ShapeDtypeStruct((1, B_W), F32)
    return pl.pallas_call(
        body, grid=(lay.nb // MT,),
        in_specs=[pl.BlockSpec((MT * TB, B_W), lambda j: (j, 1)), tok, tok, vec, mat, vec],
        out_specs=[tok, tok, mat, vec, vec],
        out_shape=[jax.ShapeDtypeStruct((lay.nt, B_W), MXU_DTYPE), jax.ShapeDtypeStruct((lay.nt, B_W), F32),
                   jax.ShapeDtypeStruct((B_W, B_W), F32), vsh, vsh],
        compiler_params=_cp(("arbitrary",)), name=name)(z, y, ds, dvec, wglu, bglu)


def _dz_assemble(lay, dz_a, duf, dur, dud, dz_p, name):
    def body(a_ref, f_ref, r_ref, d_ref, p_ref, o_ref):
        o_ref[:, :2 * A_W] = a_ref[...].astype(o_ref.dtype)
        o_ref[:, 2 * A_W:2 * A_W + B_W] = (f_ref[...].astype(F32) + r_ref[...].astype(F32) + d_ref[...]).astype(o_ref.dtype)
        o_ref[:, 2 * A_W + B_W:] = p_ref[...].astype(o_ref.dtype)

    spec = lambda w: pl.BlockSpec((MT * TB, w), lambda j: (j, 0))
    return pl.pallas_call(
        body, grid=(lay.nb // MT,), in_specs=[spec(2 * A_W), spec(B_W), spec(B_W), spec(B_W), spec(C_W)],
        out_specs=spec(D_IN), out_shape=jax.ShapeDtypeStruct((lay.nt, D_IN), MXU_DTYPE),
        compiler_params=_cp(("parallel",)), name=name)(dz_a, duf, dur, dud, dz_p)


def _expand_rows(a):
    return jnp.broadcast_to(a[:, :, None, :], (2, SSM_G, SSM_H, SSM_P)).reshape(-1, SSM_P)


def _ssm_params(lam_re, lam_im, log_dt, b_re, b_im, c_re, c_im, name):
    lrx, lix = _expand_rows(lam_re), _expand_rows(lam_im)
    ldtx = _expand_rows(jnp.broadcast_to(log_dt[:, :, None], (2, SSM_G, SSM_P)))
    brt = jnp.transpose(b_re, (0, 1, 3, 2)).reshape(-1, SSM_P)
    bit = jnp.transpose(b_im, (0, 1, 3, 2)).reshape(-1, SSM_P)
    arx, aix, bbr, bbi = _disc_fwd(lrx, lix, ldtx, brt, bit, name)
    ar = arx.reshape(2, SSM_G, SSM_H, SSM_P)[:, :, 0].reshape(2, SSM_G * SSM_P)
    ai = aix.reshape(2, SSM_G, SSM_H, SSM_P)[:, :, 0].reshape(2, SSM_G * SSM_P)
    eye = jnp.eye(GQ, dtype=F32)

    def bmat(bt):
        t = bt.reshape(2, 4, GQ, SSM_H, SSM_P)
        return jnp.einsum('dqghp,gk->qdghkp', t, eye).reshape(4, 2 * QC, QS)

    bh = jnp.concatenate([bmat(bbr), bmat(bbi)], axis=-1).astype(MXU_DTYPE)

    def cmat(c):
        t = c.reshape(2, 4, GQ, SSM_H, SSM_P)
        return jnp.einsum('dqghp,gk->qgpdkh', t, eye).reshape(4, QS, 2 * QC)

    ch = jnp.concatenate([cmat(c_re), -cmat(c_im)], axis=1).astype(MXU_DTYPE)

    def rows8(a):
        return jnp.repeat(a, 4, axis=0)

    return dict(lrx=lrx, lix=lix, ldtx=ldtx, brt=brt, bit=bit, bh=bh, ch=ch, ar8=rows8(ar), ai8=rows8(ai))


def _ssm_param_grads(sp, dbh, dch, dar, dai, name):
    def bdiag(m):
        t = m.reshape(4, 2, GQ, SSM_H, GQ, SSM_P)
        return jnp.einsum('qdghgp->dqghp', t).reshape(-1, SSM_P)

    dbr, dbi = bdiag(dbh[..., :QS]), bdiag(dbh[..., QS:])

    def cdiag(m):
        t = m.reshape(4, GQ, SSM_P, 2, GQ, SSM_H)
        return jnp.einsum('qgpdgh->dqghp', t).reshape(2, SSM_G, SSM_H, SSM_P)

    dc_re, dc_im = cdiag(dch[:, :QS]), -cdiag(dch[:, QS:])

    def hrow(a):
        t = a.reshape(2, SSM_G, 1, SSM_P)
        return jnp.concatenate([t, jnp.zeros((2, SSM_G, SSM_H - 1, SSM_P), F32)], axis=2).reshape(-1, SSM_P)

    glr, gli, gdt, gbr, gbi = _disc_bwd(sp["lrx"], sp["lix"], sp["ldtx"], sp["brt"], sp["bit"],
                                        hrow(dar), hrow(dai), dbr, dbi, name)
    to_b = lambda g: jnp.transpose(g.reshape(2, SSM_G, SSM_H, SSM_P), (0, 1, 3, 2))
    return dict(ssm_lam_re=glr.reshape(2, SSM_G, SSM_P), ssm_lam_im=gli.reshape(2, SSM_G, SSM_P),
                ssm_log_dt=gdt.reshape(2, SSM_G), ssm_b_re=to_b(gbr), ssm_b_im=to_b(gbi),
                ssm_c_re=dc_re, ssm_c_im=dc_im)


def _layer_consts(p):
    c = {}
    c["ws"] = p["sgu_w"].astype(MXU_DTYPE)
    c["wst"] = jnp.transpose(p["sgu_w"], (0, 2, 1)).astype(MXU_DTYPE)
    c["gbias"] = jnp.repeat(p["sgu_b"].T, 64, axis=1)
    pw = jnp.zeros((C_W, C_W), F32)
    for i in range(4):
        pw = pw.at[64 * i:64 * i + 64, 64 * i:64 * i + 64].set(p["pool_w"][i])
    c["pw"] = pw.astype(MXU_DTYPE)
    c["pscale"] = p["pool_scale"].reshape(1, C_W)
    c["dvec"] = p["ssm_d"].reshape(1, B_W)
    c["bglu"] = p["glu_b"].reshape(1, B_W)
    return c


def _layer_fwd(lay, i, x, modarr, p, w, cst, sp, bands, inv, perm, sides=None, last=False):
    n = f"l{i}_"
    sides = sides or {}
    win_side, win_fill = sides.get("win", (None, None))
    ssm_side, ssm_fill = sides.get("ssm", (None, None))
    ffn_side, ffn_fill = sides.get("ffn", (None, None))
    res = {"x0": x}
    h = _normmod_fwd(lay, x, p["norm_mix_pre"].reshape(1, D), modarr, 0, 1, n + "nm1")
    z = _mm([(h, w["win_t"])], True, MXU_DTYPE, n + "win", side=win_side)
    if win_side is not None:
        z, extra = z
        win_fill(extra)
    a = _gate_fwd(lay, z, cst["ws"], cst["gbias"], n + "gate")
    yf, yr, hst, extra = _ssm_fwd(lay, z, perm, sp["bh"], sp["ch"], sp["ar8"], sp["ai8"], n + "ssm", ssm_side)
    if ssm_side is not None:
        ssm_fill(extra)
    s, y = _glu_fwd(lay, z, yf, yr, cst["dvec"], w["wglu"], cst["bglu"], n + "glu")
    c = _pool_fwd(lay, z, bands, inv, cst["pw"], cst["pscale"], n + "pool")
    mcat = jnp.concatenate([s, a, c], axis=1)
    res["wout_p"] = _perm_wout(w["wout"])
    m = _mm([(mcat, res["wout_p"])], False, MXU_DTYPE, n + "wout")
    x1, h2 = _resnorm_normmod_fwd(lay, x, m, p["norm_mix_post"].reshape(1, D), p["norm_ffn_pre"].reshape(1, D),
                                  modarr, 2, 3, 4, n + "rn1nm2")
    g, u, act, extra = _ffn_up(h2, w["wg_t"], w["wu_t"], n + "ffn_up", ffn_side)
    if ffn_side is not None:
        ffn_fill(extra)
    f = _mm([(act, w["wd"])], False, MXU_DTYPE, n + "ffn_down")
    res.update(h=h, z=z, hst=hst, y=y, mcat=mcat, m=m, x1=x1, h2=h2, g=g, u=u, act=act, f=f)
    if last:
        return None, res
    x2 = _resnorm_fwd(lay, x1, f, p["norm_ffn_post"].reshape(1, D), modarr, 5, n + "rn2")
    return x2, res


def _layer_bwd(lay, i, dx2, modarr, p, w, cst, sp, bands, inv, perm, res, side_fns=None):
    n = f"l{i}b_"
    big, small = {}, {}
    side_fns = side_fns or {}
    side_of = lambda key: side_fns[key](big) if key in side_fns else None
    df, dg2, gpost2 = _resnorm_bwd(lay, dx2, res["f"], p["norm_ffn_post"].reshape(1, D), modarr, 5, n + "rn2")
    big["wd"] = _mm_tn(res["act"], df, MXU_DTYPE, n + "dwd")
    dg, du, early = _ffn_down_bwd(df, w["wd"], res["g"], res["u"], n + "ffn_down", side_of("ffn_down"))
    dh2_side = side_of("dh2")
    dh2 = _mm([(dg, w["wg_t"]), (du, w["wu_t"])], False, MXU_DTYPE, n + "dh2", side=dh2_side)
    if dh2_side is not None:
        dh2, ex = dh2
        early = early + ex
    big["wg_t"] = _mm_tn(dg, res["h2"], MXU_DTYPE, n + "dwg")
    big["wu_t"] = _mm_tn(du, res["h2"], MXU_DTYPE, n + "dwu")
    dx1, dm, dsh2, dsc2, gpre2, dg1, gpost1 = _normmod_resnorm_bwd(
        lay, res["x1"], dh2, dx2, p["norm_ffn_pre"].reshape(1, D), res["m"], p["norm_mix_post"].reshape(1, D), modarr,
        4, 2, n + "nm2rn1")
    big["wout"] = _unperm_wout(_mm_tn(res["mcat"], dm, MXU_DTYPE, n + "dwout"))
    dmcat = _mm([(dm, res["wout_p"])], True, MXU_DTYPE, n + "dmcat")
    z = res["z"]
    dz_a, dws, dgb = _gate_bwd(lay, z, dmcat, cst["ws"], cst["wst"], cst["gbias"], n + "gate")
    dy, dud, dwglu, dbglu, ddvec = _glu_bwd(lay, z, res["y"], dmcat, cst["dvec"], w["wglu"], cst["bglu"], n + "glu")
    big["wglu"] = dwglu.astype(MXU_DTYPE)
    duf, dur, dbh, dch, dar, dai, ex = _ssm_bwd(lay, z, dy, perm, res["hst"], sp["bh"], sp["ch"], sp["ar8"],
                                                sp["ai8"], n + "ssm", side_of("ssm"))
    early = early + ex
    dz_p, dpw, dpsc = _pool_bwd(lay, z, dmcat, bands, inv, cst["pw"], cst["pscale"], n + "pool")
    dz = _dz_assemble(lay, dz_a, duf, dur, dud, dz_p, n + "dz")
    big["win_t"] = _mm_tn(dz, res["h"], MXU_DTYPE, n + "dwin")
    dh_side = side_of("dh")
    dh = _mm([(dz, w["win_t"])], False, MXU_DTYPE, n + "dh", side=dh_side)
    if dh_side is not None:
        dh, ex = dh
        early = early + ex
    dx, dsh1, dsc1, gpre1 = _normmod_bwd(lay, res["x0"], dh, dx1, p["norm_mix_pre"].reshape(1, D), modarr, 1, n + "nm1",
                                         latent_only=(i == 0))

    small.update(norm_mix_pre=gpre1[0], norm_mix_post=gpost1[0], norm_ffn_pre=gpre2[0], norm_ffn_post=gpost2[0])
    small["sgu_w"] = dws
    small["sgu_b"] = jnp.sum(dgb.reshape(CHUNK, 4, 64), axis=-1).T
    small.update(_ssm_param_grads(sp, dbh, dch, dar, dai, n + "disc"))
    small["ssm_d"] = ddvec.reshape(SSM_G, SSM_H)
    small["glu_b"] = dbglu[0]
    small["pool_w"] = jnp.stack([dpw[64 * k:64 * k + 64, 64 * k:64 * k + 64] for k in range(4)])
    small["pool_scale"] = dpsc[0]
    dmod = jnp.concatenate([dsh1, dsc1, dg1, dsh2, dsc2, dg2], axis=1)[:lay.bl + 1]
    dmod = jnp.concatenate([dmod, jnp.zeros((8 - lay.bl - 1, 6, D), F32)], axis=0)
    return dx, big, small, dmod, early


def _perm_wout(w):
    return w.reshape(4, D // 4, D)[np.array(WOUT_PERM)].reshape(D, D)


def _unperm_wout(g):
    return g.reshape(4, D // 4, D)[np.array(WOUT_INV)].reshape(D, D)


SMALL_NAMES = ["norm_mix_pre", "norm_mix_post", "norm_ffn_pre", "norm_ffn_post", "sgu_w", "sgu_b", "ssm_lam_re",
               "ssm_lam_im", "ssm_log_dt", "ssm_b_re", "ssm_b_im", "ssm_c_re", "ssm_c_im", "ssm_d", "glu_b", "pool_w",
               "pool_scale"]
BIG_NAMES = ["win_t", "wout", "wglu", "wg_t", "wu_t", "wd"]


def _sincos_2d(rows, cols, dim):
    quarter = dim // 4
    omega = 1.0 / (10000.0 ** (jnp.arange(quarter, dtype=F32) / quarter))
    r = jnp.arange(rows, dtype=F32)[:, None] * omega
    cc = jnp.arange(cols, dtype=F32)[:, None] * omega
    er = jnp.concatenate([jnp.sin(r), jnp.cos(r)], axis=-1)
    ec = jnp.concatenate([jnp.sin(cc), jnp.cos(cc)], axis=-1)
    pe = jnp.concatenate([jnp.broadcast_to(er[:, None, :], (rows, cols, dim // 2)),
                          jnp.broadcast_to(ec[None, :, :], (rows, cols, dim // 2))], axis=-1)
    return pe.reshape(rows * cols, dim)


def _core(x, ctx, target, mods_local, params, weights, w_sides=None, g_side_fns=None):
    bl, lat, _ = x.shape
    assert bl == 4 and lat % TB == 0, "the scan fills 8 sublanes with 2 directions x 4 sequences"
    lay = _Layout(bl, lat)
    pe = _sincos_2d(lat // GRID_W, GRID_W, D)
    bands_np, inv_np = _band_constants()
    bands, inv = jnp.asarray(bands_np, MXU_DTYPE), jnp.asarray(inv_np, F32)
    perm = jnp.asarray(_scan_perm(bl), MXU_DTYPE)
    csts, sps, ress, wls = [], [], [], []
    for i in range(2):
        csts.append(_layer_consts(params[i]))
        p = params[i]
        sps.append(_ssm_params(p["ssm_lam_re"], p["ssm_lam_im"], p["ssm_log_dt"], p["ssm_b_re"], p["ssm_b_im"],
                               p["ssm_c_re"], p["ssm_c_im"], f"l{i}_disc"))
        wls.append(dict(weights[i]))

    embed_side, embed_fill = (w_sides[0].get("embed") if w_sides else None) or (None, None)
    xt, extra = _embed(lay, x, ctx, pe, embed_side)
    if embed_side is not None:
        embed_fill(wls, extra)
    if callable(mods_local):
        mods_local = mods_local()
    modarrs = [lay.mod_tiles(mods_local[i]) for i in range(2)]
    for i in range(2):
        sides = {}
        for key, (side, fill) in ((w_sides or [{}, {}])[i]).items():
            sides[key] = (side, functools.partial(fill, wls))
        xt, res = _layer_fwd(lay, i, xt, modarrs[i], params[i], wls[i], csts[i], sps[i], bands, inv, perm, sides,
                             last=(i == 1))
        ress.append(res)
    dx, lossv = _resnorm_loss(lay, ress[1]["x1"], ress[1]["f"], params[1]["norm_ffn_post"].reshape(1, D), modarrs[1], 5,
                              target)
    bigs, smalls, dmods, early = [None, None], [None, None], [None, None], []
    for i in (1, 0):
        fns = {}
        if i == 0 and g_side_fns is not None:
            fns = {key: functools.partial(fn, bigs[1]) for key, fn in g_side_fns.items()}
        dx, bigs[i], smalls[i], dmods[i], ex = _layer_bwd(lay, i, dx, modarrs[i], params[i], wls[i], csts[i], sps[i],
                                                           bands, inv, perm, ress[i], fns)
        early += ex
    return lossv[0, 0], dx.reshape(bl, lat, D), bigs, smalls, dmods, early


def _my_index():
    return 4 * lax.axis_index("x") + 2 * lax.axis_index("y") + lax.axis_index("c")


def _peer(k):
    x, y, c = lax.axis_index("x"), lax.axis_index("y"), lax.axis_index("c")
    kx, ky, kc = (k >> 2) & 1, (k >> 1) & 1, k & 1
    px = 1 - x if kx else x
    py = 1 - y if ky else y
    pc = 1 - c if kc else c
    return (px, py, pc), 4 * px + 2 * py + pc


class _Side:
    def __init__(self, items):
        self.items = items
        self.n = len(items)
        self.ncopies = sum(len(it[2]) for it in items)
        self.arrays = [it[0] for it in items]
        anyspec = pl.BlockSpec(memory_space=pl.ANY)
        self.in_specs = [anyspec] * self.n
        self.out_specs = [anyspec] * self.n
        self.out_shape = [jax.ShapeDtypeStruct((slots,) + tuple(a.shape) if mode == "gather" else tuple(a.shape), a.dtype)
                          for a, mode, ks, slots in items]
        self.scratch = [pltpu.SemaphoreType.DMA((self.ncopies,)), pltpu.SemaphoreType.DMA((self.ncopies,)),
                        pltpu.SemaphoreType.DMA((self.n,))]

    def _copies(self, ins, outs, sems):
        send_sems, recv_sems, local_sems = sems
        slot_of = lambda idx, slots: idx if slots == 8 else (idx // 2 if slots == 4 else idx % 2)
        me = _my_index()
        local, sends, recvs = [], [], []
        q = 0
        for t, (arr, mode, ks, slots) in enumerate(self.items):
            src_own = ins[t] if mode == "gather" else ins[t].at[me]
            local.append(pltpu.make_async_copy(src_own, outs[t].at[slot_of(me, slots)], local_sems.at[t]))
            for k in ks:
                peer, pidx = _peer(k)
                src = ins[t] if mode == "gather" else ins[t].at[pidx]
                sends.append(pltpu.make_async_remote_copy(
                    src_ref=src, dst_ref=outs[t].at[slot_of(me, slots)], send_sem=send_sems.at[q], recv_sem=recv_sems.at[q],
                    device_id=peer, device_id_type=pl.DeviceIdType.MESH))
                recvs.append(pltpu.make_async_remote_copy(
                    src_ref=src, dst_ref=outs[t].at[slot_of(pidx, slots)], send_sem=send_sems.at[q], recv_sem=recv_sems.at[q],
                    device_id=peer, device_id_type=pl.DeviceIdType.MESH))
                q += 1
        return local, sends, recvs

    def start(self, ins, outs, sems):
        local, sends, _ = self._copies(ins, outs, sems)
        for cp in sends + local:
            cp.start()

    def wait(self, ins, outs, sems):
        local, sends, recvs = self._copies(ins, outs, sems)
        for cp in recvs:
            cp.wait_recv()
        for cp in sends:
            cp.wait_send()
        for cp in local:
            cp.wait()


def _comm(items, name):
    side = _Side(items)
    n = side.n

    def body(*refs):
        ins, outs, sems = refs[:n], refs[n:2 * n], refs[2 * n:]
        side.start(ins, outs, sems)
        side.wait(ins, outs, sems)

    return pl.pallas_call(
        body, in_specs=side.in_specs, out_specs=side.out_specs, out_shape=side.out_shape, scratch_shapes=side.scratch,
        compiler_params=pltpu.CompilerParams(has_side_effects=True), name=name)(*side.arrays)


def _spread(items, name):
    n = len(items)
    ncopies = sum(len(it[1]) for it in items)

    def slot_of(idx, slots):
        return idx if slots == 8 else (idx // 2 if slots == 4 else idx % 2)

    def body(*refs):
        ins, outs, bufs = refs[:n], refs[n:2 * n], refs[2 * n:3 * n]
        load_sems, store_sems, send_sems, recv_sems = refs[3 * n:]
        me = _my_index()
        loads = [pltpu.make_async_copy(ins[t], bufs[t], load_sems.at[t]) for t in range(n)]
        for cp in loads:
            cp.start()
        stores, sends, recvs = [], [], []
        q = 0
        for t, (arr, ks, slots) in enumerate(items):
            loads[t].wait()
            own = outs[t].at[slot_of(me, slots)]
            stores.append(pltpu.make_async_copy(bufs[t], own, store_sems.at[t]))
            stores[-1].start()
            for k in ks:
                peer, pidx = _peer(k)
                sends.append(pltpu.make_async_remote_copy(
                    src_ref=bufs[t], dst_ref=own, send_sem=send_sems.at[q], recv_sem=recv_sems.at[q],
                    device_id=peer, device_id_type=pl.DeviceIdType.MESH))
                recvs.append(pltpu.make_async_remote_copy(
                    src_ref=bufs[t], dst_ref=outs[t].at[slot_of(pidx, slots)], send_sem=send_sems.at[q],
                    recv_sem=recv_sems.at[q], device_id=peer, device_id_type=pl.DeviceIdType.MESH))
                sends[-1].start()
                q += 1
        for cp in recvs:
            cp.wait_recv()
        for cp in sends:
            cp.wait_send()
        for cp in stores:
            cp.wait()

    anyspec = pl.BlockSpec(memory_space=pl.ANY)
    return pl.pallas_call(
        body, in_specs=[anyspec] * n, out_specs=[anyspec] * n,
        out_shape=[jax.ShapeDtypeStruct((slots,) + tuple(arr.shape), arr.dtype) for arr, ks, slots in items],
        scratch_shapes=[pltpu.VMEM(tuple(arr.shape), arr.dtype) for arr, ks, slots in items]
        + [pltpu.SemaphoreType.DMA((n,)), pltpu.SemaphoreType.DMA((n,)), pltpu.SemaphoreType.DMA((ncopies,)),
           pltpu.SemaphoreType.DMA((ncopies,))],
        compiler_params=pltpu.CompilerParams(has_side_effects=True, vmem_limit_bytes=VMEM_LIMIT),
        name=name)(*[it[0] for it in items])


ALL7 = (1, 2, 3, 4, 5, 6, 7)
CHIPS3 = (2, 4, 6)


def _sum8(parts, name):
    def one(a, nm):
        _, r, c = a.shape
        tr = r if r <= 512 else _pick_rows(r)

        def body(a_ref, o_ref):
            acc = a_ref[0].astype(F32)
            for q in range(1, a_ref.shape[0]):
                acc = acc + a_ref[q].astype(F32)
            o_ref[...] = acc

        return pl.pallas_call(
            body, grid=(r // tr,), in_specs=[pl.BlockSpec((a.shape[0], tr, c), lambda i: (0, i, 0))],
            out_specs=pl.BlockSpec((tr, c), lambda i: (i, 0)), out_shape=jax.ShapeDtypeStruct((r, c), F32),
            compiler_params=_cp(("parallel",)), name=nm)(a)

    return [one(a, f"{name}{i}") for i, a in enumerate(parts)]


def _pick_rows(r, cap=512):
    for t in (512, 352, 256, 176, 128, 64, 32, 16, 8):
        if r % t == 0 and t <= cap:
            return t
    return r


def _adam(w, g, m, v, name):
    shape = w.shape
    nel = int(np.prod(shape))
    c1 = 1.0 / (1.0 - ADAM_B1 ** ADAM_STEP)
    c2 = 1.0 / (1.0 - ADAM_B2 ** ADAM_STEP)

    def body(w_ref, g_ref, m_ref, v_ref, d_ref, nm_ref, nv_ref):
        gv = g_ref[...]
        nm = ADAM_B1 * m_ref[...] + (1.0 - ADAM_B1) * gv
        nv = ADAM_B2 * v_ref[...] + (1.0 - ADAM_B2) * (gv * gv)
        d_ref[...] = -ADAM_LR * ((nm * c1) / (jnp.sqrt(nv * c2) + ADAM_EPS) + ADAM_WD * w_ref[...])
        nm_ref[...] = nm
        nv_ref[...] = nv

    padded = int(np.prod(shape[:-2])) * (-(-shape[-2] // 8) * 8) * (-(-shape[-1] // 128) * 128) if len(shape) >= 2 else nel
    if len(shape) >= 2 and padded <= 1024 * 1024:
        sh = jax.ShapeDtypeStruct(shape, F32)
        return pl.pallas_call(body, out_shape=[sh] * 3, compiler_params=_cp(None), name=name)(w, g, m, v)

    if len(shape) >= 2 and shape[-1] >= 128:
        lanes = shape[-1]
    else:
        lanes = 512 if nel % 512 == 0 else 128
    r = nel // lanes
    tr = r if r * lanes <= 384 * 1024 else _pick_rows(r, 384 * 1024 // lanes)

    spec = pl.BlockSpec((tr, lanes), lambda i: (i, 0))
    sh = jax.ShapeDtypeStruct((r, lanes), F32)
    outs = pl.pallas_call(
        body, grid=(r // tr,), in_specs=[spec] * 4, out_specs=[spec] * 3, out_shape=[sh] * 3,
        compiler_params=_cp(("parallel",)), name=name)(*[a.reshape(r, lanes) for a in (w, g, m, v)])
    return [o.reshape(shape) for o in outs]


def _silu(x):
    return x * _sigmoid(x)


def _mod_fwd(c_rows, w_mod, b_cols, name):
    def body(c_ref, w_ref, b_ref, o_ref):
        s = _silu(c_ref[...])
        for l in range(2):
            o_ref[l] = jnp.dot(s, w_ref[l], preferred_element_type=F32, precision=lax.Precision.HIGHEST) + b_ref[l]

    nc = w_mod.shape[2]
    return pl.pallas_call(body, out_shape=jax.ShapeDtypeStruct((2, c_rows.shape[0], nc), F32),
                          compiler_params=_cp(None), name=name)(c_rows, w_mod, b_cols)


def _mod_bwd(c_rows, w_mod, dlat, dctx8, name):
    nrow = c_rows.shape[0]
    nb = nrow - 8

    def body(c_ref, w_ref, dl_ref, dc_ref, gw_ref, gc_ref):
        s = _silu(c_ref[...])
        ctx_row = lax.broadcasted_iota(jnp.int32, (nrow, 1), 0) == nb
        gc = jnp.zeros((1, D), F32)
        for l in range(2):
            dctx = dc_ref[0, l]
            for q in range(1, 8):
                dctx = dctx + dc_ref[q, l]
            dm = dl_ref[l] + jnp.where(ctx_row, dctx, 0.0)
            gw_ref[l] = lax.dot_general(s, dm, (((0,), (0,)), ((), ())), preferred_element_type=F32,
                                        precision=lax.Precision.HIGHEST)
            gc = gc + lax.dot_general(dctx, w_ref[l], (((1,), (1,)), ((), ())), preferred_element_type=F32,
                                      precision=lax.Precision.HIGHEST)
        gc_ref[...] = gc

    nc = w_mod.shape[2]
    return pl.pallas_call(body, out_shape=[jax.ShapeDtypeStruct((2, D, nc), F32), jax.ShapeDtypeStruct((1, D), F32)],
                          compiler_params=_cp(None), name=name)(c_rows, w_mod, dlat, dctx8)


def _bmod_cctx(dmod_all, gc4, c_ctx, name):
    def body(dm_ref, gc_ref, cc_ref, gb_ref, gcc_ref):
        for l in range(2):
            acc = jnp.sum(dm_ref[0, l], axis=0, keepdims=True)
            for q in range(1, 8):
                acc = acc + jnp.sum(dm_ref[q, l], axis=0, keepdims=True)
            gb_ref[l:l + 1, :] = acc
        g = gc_ref[0] + gc_ref[1] + gc_ref[2] + gc_ref[3]
        cv = cc_ref[...]
        sg = _sigmoid(cv)
        gcc_ref[...] = g * (sg * (1.0 + cv * (1.0 - sg)))

    return pl.pallas_call(body, out_shape=[jax.ShapeDtypeStruct((2, 6 * D), F32), jax.ShapeDtypeStruct((1, D), F32)],
                          compiler_params=_cp(None), name=name)(dmod_all, gc4, c_ctx)


def kernel(x, c, ctx, c_ctx, w_mod, b_mod, norm_mix_pre, norm_mix_post, norm_ffn_pre, norm_ffn_post, w_in, w_out, sgu_w, sgu_b, ssm_lam_re, ssm_lam_im, ssm_log_dt, ssm_b_re, ssm_b_im, ssm_c_re, ssm_c_im, ssm_d, glu_w, glu_b, pool_w, pool_scale, ffn_w_gate, ffn_w_up, ffn_w_down, loss_target, m_c_ctx, m_w_mod, m_b_mod, m_norm_mix_pre, m_norm_mix_post, m_norm_ffn_pre, m_norm_ffn_post, m_w_in, m_w_out, m_sgu_w, m_sgu_b, m_ssm_lam_re, m_ssm_lam_im, m_ssm_log_dt, m_ssm_b_re, m_ssm_b_im, m_ssm_c_re, m_ssm_c_im, m_ssm_d, m_glu_w, m_glu_b, m_pool_w, m_pool_scale, m_ffn_w_gate, m_ffn_w_up, m_ffn_w_down, v_c_ctx, v_w_mod, v_b_mod, v_norm_mix_pre, v_norm_mix_post, v_norm_ffn_pre, v_norm_ffn_post, v_w_in, v_w_out, v_sgu_w, v_sgu_b, v_ssm_lam_re, v_ssm_lam_im, v_ssm_log_dt, v_ssm_b_re, v_ssm_b_im, v_ssm_c_re, v_ssm_c_im, v_ssm_d, v_glu_w, v_glu_b, v_pool_w, v_pool_scale, v_ffn_w_gate, v_ffn_w_up, v_ffn_w_down):
    wts = dict(c_ctx=c_ctx, w_mod=w_mod, b_mod=b_mod, norm_mix_pre=norm_mix_pre, norm_mix_post=norm_mix_post,
               norm_ffn_pre=norm_ffn_pre, norm_ffn_post=norm_ffn_post, w_in=w_in, w_out=w_out, sgu_w=sgu_w, sgu_b=sgu_b,
               ssm_lam_re=ssm_lam_re, ssm_lam_im=ssm_lam_im, ssm_log_dt=ssm_log_dt, ssm_b_re=ssm_b_re, ssm_b_im=ssm_b_im,
               ssm_c_re=ssm_c_re, ssm_c_im=ssm_c_im, ssm_d=ssm_d, glu_w=glu_w, glu_b=glu_b, pool_w=pool_w,
               pool_scale=pool_scale, ffn_w_gate=ffn_w_gate, ffn_w_up=ffn_w_up, ffn_w_down=ffn_w_down)
    ms = dict(c_ctx=m_c_ctx, w_mod=m_w_mod, b_mod=m_b_mod, norm_mix_pre=m_norm_mix_pre, norm_mix_post=m_norm_mix_post,
              norm_ffn_pre=m_norm_ffn_pre, norm_ffn_post=m_norm_ffn_post, w_in=m_w_in, w_out=m_w_out, sgu_w=m_sgu_w,
              sgu_b=m_sgu_b, ssm_lam_re=m_ssm_lam_re, ssm_lam_im=m_ssm_lam_im, ssm_log_dt=m_ssm_log_dt,
              ssm_b_re=m_ssm_b_re, ssm_b_im=m_ssm_b_im, ssm_c_re=m_ssm_c_re, ssm_c_im=m_ssm_c_im, ssm_d=m_ssm_d,
              glu_w=m_glu_w, glu_b=m_glu_b, pool_w=m_pool_w, pool_scale=m_pool_scale, ffn_w_gate=m_ffn_w_gate,
              ffn_w_up=m_ffn_w_up, ffn_w_down=m_ffn_w_down)
    vs = dict(c_ctx=v_c_ctx, w_mod=v_w_mod, b_mod=v_b_mod, norm_mix_pre=v_norm_mix_pre, norm_mix_post=v_norm_mix_post,
              norm_ffn_pre=v_norm_ffn_pre, norm_ffn_post=v_norm_ffn_post, w_in=v_w_in, w_out=v_w_out, sgu_w=v_sgu_w,
              sgu_b=v_sgu_b, ssm_lam_re=v_ssm_lam_re, ssm_lam_im=v_ssm_lam_im, ssm_log_dt=v_ssm_log_dt,
              ssm_b_re=v_ssm_b_re, ssm_b_im=v_ssm_b_im, ssm_c_re=v_ssm_c_re, ssm_c_im=v_ssm_c_im, ssm_d=v_ssm_d,
              glu_w=v_glu_w, glu_b=v_glu_b, pool_w=v_pool_w, pool_scale=v_pool_scale, ffn_w_gate=v_ffn_w_gate,
              ffn_w_up=v_ffn_w_up, ffn_w_down=v_ffn_w_down)
    order = list(wts.keys())
    bl = x.shape[0]
    nseq = bl * N_DEV
    me = _my_index()
    chip = me // 2
    ncol = w_mod.shape[2]

    (c_all,) = _spread([(c, ALL7, 8)], "ag_c")
    nrow = nseq + 8
    c_rows = jnp.concatenate([c_all.reshape(nseq, D), c_ctx[None], jnp.zeros((7, D), F32)], axis=0)
    b_cols = lax.dynamic_slice_in_dim(b_mod, chip * ncol, ncol, axis=1)[:, None, :]
    mod_cols = _mod_fwd(c_rows, w_mod, b_cols, "mod_fwd")
    stash = {}

    def mods_local():
        mods = jnp.transpose(stash["mod4"], (1, 2, 0, 3)).reshape(2, nrow, 6 * D)
        return jnp.concatenate([lax.dynamic_slice_in_dim(mods, me * bl, bl, axis=1), mods[:, nseq:nseq + 1],
                                jnp.zeros((2, 8 - bl - 1, 6 * D), F32)], axis=1)

    shards = {}
    for i in range(2):
        for nme, s in zip(BIG_NAMES, [w_in[i].T, w_out[i], glu_w[i], ffn_w_gate[i].T, ffn_w_up[i].T, ffn_w_down[i]]):
            shards[(i, nme)] = s.astype(MXU_DTYPE)
    weights = [{}, {}]
    ffn_names = ("wg_t", "wu_t", "wd")
    w_plan = [{"embed": [(0, "win_t")], "win": [(0, "wout"), (0, "wglu")], "ssm": [(0, "wg_t"), (0, "wu_t")],
               "ffn": [(0, "wd"), (1, "win_t"), (1, "wout"), (1, "wglu")]},
              {"ssm": [(1, "wg_t"), (1, "wu_t")], "ffn": [(1, "wd")]}]

    def w_entry(keys, more=()):
        def fill(wls, gathered):
            for (i, nme), g in zip(keys, gathered):
                wls[i][nme] = g.reshape(-1, g.shape[-1])
            for (nme, _), g in zip(more, gathered[len(keys):]):
                stash[nme] = g
        return _Side([(shards[k2], "gather", CHIPS3, 4) for k2 in keys] + [(a, "gather", CHIPS3, 4) for _, a in more]), fill

    w_sides = [{key: w_entry(keys) for key, keys in plan.items()} for plan in w_plan]
    w_sides[0]["embed"] = w_entry(w_plan[0]["embed"], more=[("mod4", mod_cols)])

    eighths = lambda g: g.reshape(8, g.shape[0] // 8, g.shape[1])
    g_plan = {"ffn_down": [(1, "win_t"), (1, "wg_t")], "dh2": [(1, "wu_t"), (1, "wout"), (1, "wglu")],
              "ssm": [(0, k) for k in BIG_NAMES if k != "win_t"] + [(1, "wd")], "dh": [(0, "win_t")]}
    early_g = g_plan["ffn_down"] + g_plan["dh2"] + g_plan["ssm"] + g_plan["dh"]

    def g_entry(keys):
        return lambda big1, big0: _Side([(eighths((big1 if i == 1 else big0)[k]), "a2a", ALL7, 8) for i, k in keys])

    g_side_fns = {key: g_entry(keys) for key, keys in g_plan.items()}

    params = [{k: wts[k][i] for k in SMALL_NAMES} for i in range(2)]
    loss_part, grad_x, bigs, smalls, dmods, early = _core(x, ctx, loss_target, mods_local, params, weights,
                                                           w_sides, g_side_fns)
    loss = lax.psum(loss_part, ("x", "y", "c"))

    dmod_local = jnp.stack([dmods[i].reshape(8, 6 * D) for i in range(2)])
    (dmod_all,) = _spread([(dmod_local, ALL7, 8)], "ag_dmod")
    dcols = lax.dynamic_slice_in_dim(dmod_all, chip * ncol, ncol, axis=3)
    dlat = jnp.transpose(dcols[:, :, :bl], (1, 0, 2, 3)).reshape(2, nseq, ncol)
    dlat = jnp.concatenate([dlat, jnp.zeros((2, 8, ncol), F32)], axis=1)
    dctx8 = dcols[:, :, bl:bl + 1]
    g_w_mod, gc_part = _mod_bwd(c_rows, w_mod, dlat, dctx8, "mod_bwd")
    (gc4,) = _spread([(gc_part, CHIPS3, 4)], "ag_cctx")
    g_b_mod, g_c_ctx = _bmod_cctx(dmod_all, gc4, c_ctx[None], "bmod_cctx")

    small_flat = jnp.concatenate([jnp.stack([smalls[i][k] for i in range(2)]).reshape(-1) for k in SMALL_NAMES])
    npad = (-small_flat.shape[0]) % (8 * 1024)
    small_flat = jnp.concatenate([small_flat, jnp.zeros((npad,), F32)])
    late = _comm([(small_flat.reshape(8, -1, 1024), "a2a", ALL7, 8)], "a2a_grads")
    sums = _sum8(list(early) + list(late), "gsum")
    fin = _spread([(s, (1,), 2) for s in sums[:-1]] + [(sums[-1], ALL7, 8)], "ag_grads")
    big_g = [{}, {}]
    for (i, k), g in zip(early_g, fin[:-1]):
        big_g[i][k] = g.reshape(-1, g.shape[-1])
    small_red = fin[-1].reshape(-1)

    grads = {}
    off = 0
    for k in SMALL_NAMES:
        shp = wts[k].shape
        nel = int(np.prod(shp))
        grads[k] = small_red[off:off + nel].reshape(shp)
        off += nel
    grads["c_ctx"] = g_c_ctx[0]
    grads["w_mod"] = g_w_mod
    grads["b_mod"] = g_b_mod
    grads["w_in"] = jnp.stack([big_g[i]["win_t"].T for i in range(2)])
    grads["w_out"] = jnp.stack([big_g[i]["wout"] for i in range(2)])
    grads["glu_w"] = jnp.stack([big_g[i]["wglu"] for i in range(2)])
    grads["ffn_w_gate"] = jnp.stack([big_g[i]["wg_t"].T for i in range(2)])
    grads["ffn_w_up"] = jnp.stack([big_g[i]["wu_t"].T for i in range(2)])
    grads["ffn_w_down"] = jnp.stack([big_g[i]["wd"] for i in range(2)])

    deltas, new_m, new_v = {}, {}, {}
    for k in order:
        deltas[k], new_m[k], new_v[k] = _adam(wts[k], grads[k], ms[k], vs[k], "adam_" + k)
    return (loss, grad_x, *[grads[k] for k in order], *[deltas[k] for k in order],
            *[new_m[k] for k in order], *[new_v[k] for k in order])
```

```python
import functools
import math

import numpy as np
import jax
import jax.numpy as jnp
from jax import lax
from jax.experimental import pallas as pl
from jax.experimental.pallas import tpu as pltpu

F32 = jnp.float32
BF16 = jnp.bfloat16
MXU_DTYPE = jnp.bfloat16
MCAT_A, MCAT_C = 2, 3
WOUT_PERM, WOUT_INV = (1, 2, 0, 3), (2, 0, 1, 3)

D = 1024
EPS = 1e-6
TB = 256
CTX = 256
CHUNK = 128
GRID_W = 64
A_W, B_W, C_W = 256, 512, 256
D_IN = 1280
D_FF = 2816
SSM_G, SSM_P, SSM_H = 32, 64, 16
ST = 64
POOL_WINDOWS = (2, 4, 8, 16)
N_DEV = 8
VMEM_LIMIT = 52 * 1024 * 1024
GELU_C = math.sqrt(2.0 / math.pi)

ADAM_LR, ADAM_B1, ADAM_B2, ADAM_EPS, ADAM_WD, ADAM_STEP = 0.001, 0.9, 0.999, 1e-08, 0.01, 10


def _cp(sem=None, vmem=VMEM_LIMIT, **kw):
    return pltpu.CompilerParams(dimension_semantics=sem, vmem_limit_bytes=vmem, **kw)


def _pick(n, cap):
    if n <= cap:
        return n
    best = None
    for t in range(128, cap + 1, 128):
        if n % t == 0:
            best = t
    assert best is not None, (n, cap)
    return best


def _gelu(x):
    return 0.5 * x * (1.0 + jnp.tanh(GELU_C * (x + 0.044715 * x * x * x)))


def _gelu_grad(x):
    t = jnp.tanh(GELU_C * (x + 0.044715 * x * x * x))
    return 0.5 * (1.0 + t) + 0.5 * x * (1.0 - t * t) * GELU_C * (1.0 + 3.0 * 0.044715 * x * x)


def _sigmoid(x):
    return 1.0 / (1.0 + jnp.exp(-x))


def _dot(a, b, dims):
    return lax.dot_general(a, b, (dims, ((), ())), preferred_element_type=F32)


def _nn(a, b):
    return _dot(a, b, ((1,), (0,)))


def _nt(a, b):
    return _dot(a, b, ((1,), (1,)))


def _tn(a, b):
    return _dot(a, b, ((0,), (0,)))


def _mm(pairs, nt, out_dtype, name, tm=512, side=None):
    m = pairs[0][0].shape[0]
    n = pairs[0][1].shape[0] if nt else pairs[0][1].shape[1]
    tn = _pick(n, 1408)
    tm = min(tm, m)
    npairs = len(pairs)
    ni, nj = m // tm, n // tn

    def body(*refs):
        own, srefs = _side_split(refs, 2 * npairs, 1, 0, side)
        o_ref = own[-1]
        i, j = pl.program_id(0), pl.program_id(1)
        _side_start(side, srefs, jnp.logical_and(i == 0, j == 0))
        acc = None
        for t in range(npairs):
            a = own[2 * t][...].astype(MXU_DTYPE)
            b = own[2 * t + 1][...].astype(MXU_DTYPE)
            r = _nt(a, b) if nt else _nn(a, b)
            acc = r if acc is None else acc + r
        o_ref[...] = acc.astype(o_ref.dtype)
        _side_wait(side, srefs, jnp.logical_and(i == ni - 1, j == nj - 1))

    sd = side if side is not None else _Side([])
    in_specs, flat = [], []
    for a, b in pairs:
        k = a.shape[1]
        in_specs.append(pl.BlockSpec((tm, k), lambda i, j: (i, 0)))
        in_specs.append(pl.BlockSpec((tn, k), lambda i, j: (j, 0)) if nt else pl.BlockSpec((k, tn), lambda i, j: (0, j)))
        flat += [a, b]
    outs = pl.pallas_call(
        body, grid=(ni, nj), in_specs=in_specs + sd.in_specs,
        out_specs=[pl.BlockSpec((tm, tn), lambda i, j: (i, j))] + sd.out_specs,
        out_shape=[jax.ShapeDtypeStruct((m, n), out_dtype)] + sd.out_shape,
        scratch_shapes=sd.scratch if side is not None else [],
        compiler_params=_cp(("arbitrary", "arbitrary") if side is not None else ("parallel", "parallel")),
        name=name)(*flat, *sd.arrays)
    return outs[0] if side is None else (outs[0], list(outs[1:]))


def _mm_tn(a, b, out_dtype, name):
    m, k1 = a.shape
    n = b.shape[1]
    t1 = _pick(k1, 1408)
    tn = _pick(n, 1024)
    tm = max(t for t in (512, 1024, 1536) if m % t == 0)
    nsteps = m // tm

    def body(a_ref, b_ref, o_ref, acc_ref):
        t = pl.program_id(2)

        @pl.when(t == 0)
        def _():
            acc_ref[...] = jnp.zeros_like(acc_ref)

        acc_ref[...] += _tn(a_ref[...].astype(MXU_DTYPE), b_ref[...].astype(MXU_DTYPE))

        @pl.when(t == nsteps - 1)
        def _():
            o_ref[...] = acc_ref[...].astype(o_ref.dtype)

    return pl.pallas_call(
        body, grid=(k1 // t1, n // tn, nsteps),
        in_specs=[pl.BlockSpec((tm, t1), lambda i, j, t: (t, i)), pl.BlockSpec((tm, tn), lambda i, j, t: (t, j))],
        out_specs=pl.BlockSpec((t1, tn), lambda i, j, t: (i, j)),
        out_shape=jax.ShapeDtypeStruct((k1, n), out_dtype),
        scratch_shapes=[pltpu.VMEM((t1, tn), F32)],
        compiler_params=_cp(("parallel", "parallel", "arbitrary")), name=name)(a, b)


class _Layout:
    def __init__(self, bl, lat):
        self.bl, self.lat = bl, lat
        self.nlb = lat // TB
        self.nr = 1 + self.nlb
        self.nctx = bl
        self.nb = self.nr * bl
        self.nt = self.nb * TB
        self.ctx_row = bl

    def mod_tiles(self, mods):
        rows = np.array([[self.ctx_row if r == 0 else b for b in range(self.bl)] for r in range(self.nr)], np.int32)
        t = mods[rows].reshape(self.nr, self.bl, 6, D)
        return jnp.transpose(t, (0, 2, 1, 3)).reshape(self.nr * 6, self.bl, 1, D)


ST_FWD, ST_BWD = 4, 2


def _tok_spec(lay, st):
    nc = lay.bl // st
    return pl.BlockSpec((st * TB, D), lambda c, r: (r * nc + c, 0))


def _vec_spec():
    return pl.BlockSpec((1, D), lambda c, r: (0, 0))


def _mod_spec(st, k):
    return pl.BlockSpec((1, st, 1, D), lambda c, r: (r * 6 + k, c, 0, 0))


def _x_spec(lay, st):
    return pl.BlockSpec((st, 1, TB, D), lambda c, r: (c, jnp.maximum(r - 1, 0), 0, 0))


def _rows3(ref_or_val, st):
    return ref_or_val.reshape(st, TB, D)


def _acc_rows(acc_ref, val3, st, ctx_row):
    c, r = pl.program_id(0), pl.program_id(1)
    s = jnp.sum(val3, axis=1, keepdims=True)

    @pl.when(r == 0)
    def _():
        acc_ref[ctx_row:ctx_row + 1] += jnp.sum(s, axis=0, keepdims=True)

    @pl.when(r > 0)
    def _():
        acc_ref[pl.ds(c * st, st)] += s


def _first_step():
    return jnp.logical_and(pl.program_id(0) == 0, pl.program_id(1) == 0)


def _embed(lay, x, ctx, pe, side=None):
    st = ST_FWD
    bl, nlb = lay.bl, lay.nlb
    nc = bl // st

    def body(*refs):
        (x_ref, c_ref, pe_ref, o_ref), srefs = _side_split(refs, 3, 1, 0, side)
        c, r = pl.program_id(0), pl.program_id(1)
        _side_start(side, srefs, jnp.logical_and(c == 0, r == 0))

        @pl.when(r == 0)
        def _():
            o_ref[...] = c_ref[...].reshape(st * TB, D)

        @pl.when(r > 0)
        def _():
            o_ref[...] = (x_ref[...].reshape(st, TB, D) + pe_ref[...]).reshape(st * TB, D)

        _side_wait(side, srefs, jnp.logical_and(c == nc - 1, r == lay.nr - 1))

    sd = side if side is not None else _Side([])
    outs = pl.pallas_call(
        body, grid=(nc, lay.nr),
        in_specs=[_x_spec(lay, st), pl.BlockSpec((st, CTX, D), lambda c, r: (c, 0, 0)),
                  pl.BlockSpec((1, TB, D), lambda c, r: (jnp.maximum(r - 1, 0), 0, 0))] + sd.in_specs,
        out_specs=[_tok_spec(lay, st)] + sd.out_specs,
        out_shape=[jax.ShapeDtypeStruct((lay.nt, D), F32)] + sd.out_shape,
        scratch_shapes=sd.scratch if side is not None else [],
        compiler_params=_cp(("arbitrary", "arbitrary") if side is not None else ("parallel", "parallel")),
        name="embed")(x.reshape(bl, nlb, TB, D), ctx, pe.reshape(nlb, TB, D), *sd.arrays)
    return outs[0], list(outs[1:])


def _normmod_fwd(lay, x, gain, modt, ksh, ksc, name):
    st = ST_FWD

    def body(x_ref, g_ref, sh_ref, sc_ref, o_ref):
        xv = _rows3(x_ref[...], st)
        r = lax.rsqrt(jnp.mean(xv * xv, axis=-1, keepdims=True) + EPS)
        o_ref[...] = ((xv * r * g_ref[...]) * (1.0 + sc_ref[0]) + sh_ref[0]).reshape(st * TB, D).astype(o_ref.dtype)

    return pl.pallas_call(
        body, grid=(lay.bl // st, lay.nr),
        in_specs=[_tok_spec(lay, st), _vec_spec(), _mod_spec(st, ksh), _mod_spec(st, ksc)],
        out_specs=_tok_spec(lay, st), out_shape=jax.ShapeDtypeStruct((lay.nt, D), MXU_DTYPE),
        compiler_params=_cp(("parallel", "parallel")), name=name)(x, gain, modt, modt)


def _acc_out():
    return pl.BlockSpec((8, 1, D), lambda c, r: (0, 0, 0)), jax.ShapeDtypeStruct((8, 1, D), F32)


def _normmod_bwd(lay, x, dh, dx_in, gain, modt, ksc, name, latent_only=False):
    st = ST_BWD
    acc_spec, acc_shape = _acc_out()
    if latent_only:
        dx_spec, dx_shape = _x_spec(lay, st), jax.ShapeDtypeStruct((lay.bl, lay.nlb, TB, D), F32)
    else:
        dx_spec, dx_shape = _tok_spec(lay, st), jax.ShapeDtypeStruct((lay.nt, D), F32)

    def body(x_ref, dh_ref, dxi_ref, g_ref, sc_ref, dx_ref, dsh_ref, dsc_ref, dg_ref):
        xv = _rows3(x_ref[...], st)
        dhv = _rows3(dh_ref[...].astype(F32), st)
        g = g_ref[...]
        sc1 = 1.0 + sc_ref[0]
        r = lax.rsqrt(jnp.mean(xv * xv, axis=-1, keepdims=True) + EPS)
        xh = xv * r
        dxh = dhv * (g * sc1)
        dx = _rows3(dxi_ref[...], st) + r * (dxh - xh * jnp.mean(dxh * xh, axis=-1, keepdims=True))
        dx_ref[...] = dx.reshape(dx_ref.shape)

        @pl.when(_first_step())
        def _():
            dsh_ref[...] = jnp.zeros_like(dsh_ref)
            dsc_ref[...] = jnp.zeros_like(dsc_ref)
            dg_ref[...] = jnp.zeros_like(dg_ref)

        _acc_rows(dsh_ref, dhv, st, lay.ctx_row)
        _acc_rows(dsc_ref, dhv * (xh * g), st, lay.ctx_row)
        dg_ref[...] += jnp.sum((dhv * sc1 * xh).reshape(st * TB, D), axis=0, keepdims=True)

    return pl.pallas_call(
        body, grid=(lay.bl // st, lay.nr),
        in_specs=[_tok_spec(lay, st), _tok_spec(lay, st), _tok_spec(lay, st), _vec_spec(), _mod_spec(st, ksc)],
        out_specs=[dx_spec, acc_spec, acc_spec, _vec_spec()],
        out_shape=[dx_shape, acc_shape, acc_shape, jax.ShapeDtypeStruct((1, D), F32)],
        compiler_params=_cp(("arbitrary", "arbitrary")), name=name)(x, dh, dx_in, gain, modt)


def _resnorm_fwd(lay, x, m, gain, modt, kgate, name):
    st = ST_FWD

    def body(x_ref, m_ref, g_ref, gate_ref, o_ref):
        mv = _rows3(m_ref[...].astype(F32), st)
        r = lax.rsqrt(jnp.mean(mv * mv, axis=-1, keepdims=True) + EPS)
        o_ref[...] = x_ref[...] + (gate_ref[0] * (mv * r * g_ref[...])).reshape(st * TB, D)

    return pl.pallas_call(
        body, grid=(lay.bl // st, lay.nr),
        in_specs=[_tok_spec(lay, st), _tok_spec(lay, st), _vec_spec(), _mod_spec(st, kgate)],
        out_specs=_tok_spec(lay, st), out_shape=jax.ShapeDtypeStruct((lay.nt, D), F32),
        compiler_params=_cp(("parallel", "parallel")), name=name)(x, m, gain, modt)


def _resnorm_bwd(lay, dxn, m, gain, modt, kgate, name):
    st = ST_FWD
    acc_spec, acc_shape = _acc_out()

    def body(d_ref, m_ref, g_ref, gate_ref, dm_ref, dgate_ref, dg_ref):
        dv = _rows3(d_ref[...], st)
        mv = _rows3(m_ref[...].astype(F32), st)
        g = g_ref[...]
        r = lax.rsqrt(jnp.mean(mv * mv, axis=-1, keepdims=True) + EPS)
        xh = mv * r
        dy = dv * gate_ref[0]
        dxh = dy * g
        dm = r * (dxh - xh * jnp.mean(dxh * xh, axis=-1, keepdims=True))
        dm_ref[...] = dm.reshape(st * TB, D).astype(dm_ref.dtype)

        @pl.when(_first_step())
        def _():
            dgate_ref[...] = jnp.zeros_like(dgate_ref)
            dg_ref[...] = jnp.zeros_like(dg_ref)

        _acc_rows(dgate_ref, dv * (xh * g), st, lay.ctx_row)
        dg_ref[...] += jnp.sum((dy * xh).reshape(st * TB, D), axis=0, keepdims=True)

    return pl.pallas_call(
        body, grid=(lay.bl // st, lay.nr),
        in_specs=[_tok_spec(lay, st), _tok_spec(lay, st), _vec_spec(), _mod_spec(st, kgate)],
        out_specs=[_tok_spec(lay, st), acc_spec, _vec_spec()],
        out_shape=[jax.ShapeDtypeStruct((lay.nt, D), MXU_DTYPE), acc_shape, jax.ShapeDtypeStruct((1, D), F32)],
        compiler_params=_cp(("arbitrary", "arbitrary")), name=name)(dxn, m, gain, modt)


def _rms(v):
    return lax.rsqrt(jnp.mean(v * v, axis=-1, keepdims=True) + EPS)


def _resnorm_normmod_fwd(lay, x, m, gpost, gpre, modt, kgate, ksh, ksc, name):
    st = ST_FWD

    def body(x_ref, m_ref, gp_ref, gq_ref, gate_ref, sh_ref, sc_ref, x1_ref, h_ref):
        mv = _rows3(m_ref[...].astype(F32), st)
        x1 = _rows3(x_ref[...], st) + gate_ref[0] * (mv * _rms(mv) * gp_ref[...])
        x1_ref[...] = x1.reshape(st * TB, D)
        h = (x1 * _rms(x1) * gq_ref[...]) * (1.0 + sc_ref[0]) + sh_ref[0]
        h_ref[...] = h.reshape(st * TB, D).astype(h_ref.dtype)

    tok = _tok_spec(lay, st)
    return pl.pallas_call(
        body, grid=(lay.bl // st, lay.nr),
        in_specs=[tok, tok, _vec_spec(), _vec_spec(), _mod_spec(st, kgate), _mod_spec(st, ksh), _mod_spec(st, ksc)],
        out_specs=[tok, tok],
        out_shape=[jax.ShapeDtypeStruct((lay.nt, D), F32), jax.ShapeDtypeStruct((lay.nt, D), MXU_DTYPE)],
        compiler_params=_cp(("parallel", "parallel")), name=name)(x, m, gpost, gpre, modt, modt, modt)


def _normmod_resnorm_bwd(lay, x1, dh, dx_in, gpre, m, gpost, modt, ksc, kgate, name):
    st = ST_BWD
    acc_spec, acc_shape = _acc_out()

    def body(x_ref, dh_ref, dxi_ref, gq_ref, sc_ref, m_ref, gp_ref, gate_ref,
             dx_ref, dm_ref, dsh_ref, dsc_ref, dgq_ref, dgate_ref, dgp_ref):
        xv = _rows3(x_ref[...], st)
        dhv = _rows3(dh_ref[...].astype(F32), st)
        gq = gq_ref[...]
        sc1 = 1.0 + sc_ref[0]
        r = _rms(xv)
        xh = xv * r
        dxh = dhv * (gq * sc1)
        dx1 = _rows3(dxi_ref[...], st) + r * (dxh - xh * jnp.mean(dxh * xh, axis=-1, keepdims=True))
        dx_ref[...] = dx1.reshape(st * TB, D)
        mv = _rows3(m_ref[...].astype(F32), st)
        gp = gp_ref[...]
        rm = _rms(mv)
        mh = mv * rm
        dy = dx1 * gate_ref[0]
        dmh = dy * gp
        dm = rm * (dmh - mh * jnp.mean(dmh * mh, axis=-1, keepdims=True))
        dm_ref[...] = dm.reshape(st * TB, D).astype(dm_ref.dtype)

        @pl.when(_first_step())
        def _():
            for ref in (dsh_ref, dsc_ref, dgq_ref, dgate_ref, dgp_ref):
                ref[...] = jnp.zeros_like(ref)

        _acc_rows(dsh_ref, dhv, st, lay.ctx_row)
        _acc_rows(dsc_ref, dhv * (xh * gq), st, lay.ctx_row)
        dgq_ref[...] += jnp.sum((dhv * sc1 * xh).reshape(st * TB, D), axis=0, keepdims=True)
        _acc_rows(dgate_ref, dx1 * (mh * gp), st, lay.ctx_row)
        dgp_ref[...] += jnp.sum((dy * mh).reshape(st * TB, D), axis=0, keepdims=True)

    tok = _tok_spec(lay, st)
    vsh = jax.ShapeDtypeStruct((1, D), F32)
    return pl.pallas_call(
        body, grid=(lay.bl // st, lay.nr),
        in_specs=[tok, tok, tok, _vec_spec(), _mod_spec(st, ksc), tok, _vec_spec(), _mod_spec(st, kgate)],
        out_specs=[tok, tok, acc_spec, acc_spec, _vec_spec(), acc_spec, _vec_spec()],
        out_shape=[jax.ShapeDtypeStruct((lay.nt, D), F32), jax.ShapeDtypeStruct((lay.nt, D), MXU_DTYPE),
                   acc_shape, acc_shape, vsh, acc_shape, vsh],
        compiler_params=_cp(("arbitrary", "arbitrary")), name=name)(x1, dh, dx_in, gpre, modt, m, gpost, modt)


def _resnorm_loss(lay, x, f, gain, modt, kgate, tgt):
    st = ST_BWD

    def body(x_ref, f_ref, g_ref, gate_ref, t_ref, dx_ref, l_ref):
        r = pl.program_id(1)

        @pl.when(_first_step())
        def _():
            l_ref[...] = jnp.zeros_like(l_ref)

        @pl.when(r == 0)
        def _():
            dx_ref[...] = jnp.zeros_like(dx_ref)

        @pl.when(r > 0)
        def _():
            fv = _rows3(f_ref[...].astype(F32), st)
            y = _rows3(x_ref[...], st) + gate_ref[0] * (fv * _rms(fv) * g_ref[...])
            e = y - t_ref[...].reshape(st, TB, D)
            dx_ref[...] = (e * (1.0 / D)).reshape(st * TB, D)
            l_ref[...] += jnp.sum(e * e) * (0.5 / D)

    tok = _tok_spec(lay, st)
    return pl.pallas_call(
        body, grid=(lay.bl // st, lay.nr),
        in_specs=[tok, tok, _vec_spec(), _mod_spec(st, kgate), _x_spec(lay, st)],
        out_specs=[tok, pl.BlockSpec((8, 128), lambda c, r: (0, 0))],
        out_shape=[jax.ShapeDtypeStruct((lay.nt, D), F32), jax.ShapeDtypeStruct((8, 128), F32)],
        compiler_params=_cp(("arbitrary", "arbitrary")), name="loss")(
            x, f, gain, modt, tgt.reshape(lay.bl, lay.nlb, TB, D))


FF_TN = D_FF // 2
FF_CHUNKS = ((0, 512), (512, 512), (1024, 384))


def _ffn_up(h, wgt, wut, name, side=None):
    m = h.shape[0]
    tm, tn = min(512, m), FF_TN
    ni, nj = m // tm, D_FF // tn

    def body(*refs):
        (h_ref, wg_ref, wu_ref, g_ref, u_ref, a_ref), srefs = _side_split(refs, 3, 3, 0, side)
        j, i = pl.program_id(0), pl.program_id(1)
        _side_start(side, srefs, jnp.logical_and(i == 0, j == 0))
        hv = h_ref[...]
        for c0, cw in FF_CHUNKS:
            g = _nt(hv, wg_ref[c0:c0 + cw, :])
            u = _nt(hv, wu_ref[c0:c0 + cw, :])
            g_ref[:, c0:c0 + cw] = g.astype(g_ref.dtype)
            u_ref[:, c0:c0 + cw] = u.astype(u_ref.dtype)
            a_ref[:, c0:c0 + cw] = (g * _sigmoid(g) * u).astype(a_ref.dtype)
        _side_wait(side, srefs, jnp.logical_and(i == ni - 1, j == nj - 1))

    sd = side if side is not None else _Side([])
    osp = pl.BlockSpec((tm, tn), lambda j, i: (i, j))
    osh = jax.ShapeDtypeStruct((m, D_FF), MXU_DTYPE)
    outs = pl.pallas_call(
        body, grid=(nj, ni),
        in_specs=[pl.BlockSpec((tm, D), lambda j, i: (i, 0)), pl.BlockSpec((tn, D), lambda j, i: (j, 0)),
                  pl.BlockSpec((tn, D), lambda j, i: (j, 0))] + sd.in_specs,
        out_specs=[osp, osp, osp] + sd.out_specs, out_shape=[osh, osh, osh] + sd.out_shape,
        scratch_shapes=sd.scratch if side is not None else [],
        compiler_params=_cp(("arbitrary", "arbitrary") if side is not None else ("parallel", "parallel")),
        name=name)(h, wgt, wut, *sd.arrays)
    return outs[0], outs[1], outs[2], list(outs[3:])


def _ffn_down_bwd(df, wd, g, u, name, side=None):
    m = df.shape[0]
    tm, tn = min(512, m), FF_TN
    ni, nj = m // tm, D_FF // tn

    def body(*refs):
        (df_ref, wd_ref, g_ref, u_ref, dg_ref, du_ref), srefs = _side_split(refs, 4, 2, 0, side)
        j, i = pl.program_id(0), pl.program_id(1)
        _side_start(side, srefs, jnp.logical_and(i == 0, j == 0))
        dfv = df_ref[...]
        for c0, cw in FF_CHUNKS:
            da = _nt(dfv, wd_ref[c0:c0 + cw, :])
            gv = g_ref[:, c0:c0 + cw].astype(F32)
            uv = u_ref[:, c0:c0 + cw].astype(F32)
            s = _sigmoid(gv)
            dg_ref[:, c0:c0 + cw] = (da * uv * (s * (1.0 + gv * (1.0 - s)))).astype(dg_ref.dtype)
            du_ref[:, c0:c0 + cw] = (da * gv * s).astype(du_ref.dtype)
        _side_wait(side, srefs, jnp.logical_and(i == ni - 1, j == nj - 1))

    sd = side if side is not None else _Side([])
    osp = pl.BlockSpec((tm, tn), lambda j, i: (i, j))
    osh = jax.ShapeDtypeStruct((m, D_FF), MXU_DTYPE)
    outs = pl.pallas_call(
        body, grid=(nj, ni),
        in_specs=[pl.BlockSpec((tm, D), lambda j, i: (i, 0)), pl.BlockSpec((tn, D), lambda j, i: (j, 0)), osp, osp]
        + sd.in_specs,
        out_specs=[osp, osp] + sd.out_specs, out_shape=[osh, osh] + sd.out_shape,
        scratch_shapes=sd.scratch if side is not None else [],
        compiler_params=_cp(("arbitrary", "arbitrary") if side is not None else ("parallel", "parallel")),
        name=name)(df, wd, g, u, *sd.arrays)
    return outs[0], outs[1], list(outs[2:])


def _head_masks(shape):
    lane = lax.broadcasted_iota(jnp.int32, shape, 1)
    return [jnp.logical_and(lane >= 64 * h, lane < 64 * h + 64) for h in range(4)]


def _head_mean(x, masks):
    out = jnp.zeros_like(x)
    for mk in masks:
        s = jnp.sum(jnp.where(mk, x, 0.0), axis=-1, keepdims=True) * (1.0 / 64.0)
        out = jnp.where(mk, s, out)
    return out


def _gate_common(z, masks):
    zg = _gelu(z)
    u = zg[:, :A_W]
    v = zg[:, A_W:]
    mu = _head_mean(v, masks)
    vc = v - mu
    rstd = lax.rsqrt(_head_mean(vc * vc, masks) + EPS)
    return u, vc * rstd, rstd


def _gate_s(vn, ws_ref, bias, masks):
    parts = []
    for c in range(TB // CHUNK):
        vc = vn[c * CHUNK:(c + 1) * CHUNK]
        s = bias
        for h in range(4):
            s = s + _nn(ws_ref[h], jnp.where(masks[h][:CHUNK], vc, 0.0).astype(MXU_DTYPE))
        parts.append(s)
    return jnp.concatenate(parts, axis=0)


MT = 4


def _blocks():
    return [pl.ds(s * TB, TB) for s in range(MT)]


def _gate_fwd(lay, z, ws, bias, name):
    def body(z_ref, ws_ref, b_ref, o_ref):
        masks = _head_masks((TB, A_W))
        for sl in _blocks():
            u, vn, _ = _gate_common(z_ref[sl, :].astype(F32), masks)
            o_ref[sl, :] = (u * _gate_s(vn, ws_ref, b_ref[...], masks)).astype(o_ref.dtype)

    return pl.pallas_call(
        body, grid=(lay.nb // MT,),
        in_specs=[pl.BlockSpec((MT * TB, 2 * A_W), lambda j: (j, 0)), pl.BlockSpec((4, CHUNK, CHUNK), lambda j: (0, 0, 0)),
                  pl.BlockSpec((CHUNK, A_W), lambda j: (0, 0))],
        out_specs=pl.BlockSpec((MT * TB, A_W), lambda j: (j, 0)),
        out_shape=jax.ShapeDtypeStruct((lay.nt, A_W), MXU_DTYPE),
        compiler_params=_cp(("parallel",)), name=name)(z, ws, bias)


def _gate_bwd(lay, z, da, ws, wst, bias, name):
    def body(z_ref, da_ref, ws_ref, wst_ref, b_ref, dz_ref, dws_ref, db_ref):
        j = pl.program_id(0)

        @pl.when(j == 0)
        def _():
            dws_ref[...] = jnp.zeros_like(dws_ref)
            db_ref[...] = jnp.zeros_like(db_ref)

        masks = _head_masks((TB, A_W))
        for blk in _blocks():
            zv = z_ref[blk, :].astype(F32)
            u, vn, rstd = _gate_common(zv, masks)
            s = _gate_s(vn, ws_ref, b_ref[...], masks)
            dav = da_ref[blk, :].astype(F32)
            du = dav * s
            ds = dav * u
            dvn_parts = []
            for c in range(TB // CHUNK):
                sl = slice(c * CHUNK, (c + 1) * CHUNK)
                ds_c = ds[sl]
                vn_c = vn[sl].astype(MXU_DTYPE)
                db_ref[...] += ds_c
                ds_b = ds_c.astype(MXU_DTYPE)
                dvn_c = jnp.zeros((CHUNK, A_W), F32)
                for h in range(4):
                    mk = masks[h][:CHUNK]
                    dws_ref[h] += _nt(jnp.where(mk, ds_c, 0.0).astype(MXU_DTYPE), vn_c)
                    dvn_c = dvn_c + jnp.where(mk, _nn(wst_ref[h], ds_b), 0.0)
                dvn_parts.append(dvn_c)
            dvn = jnp.concatenate(dvn_parts, axis=0)
            dv = rstd * (dvn - _head_mean(dvn, masks) - vn * _head_mean(dvn * vn, masks))
            gg = _gelu_grad(zv)
            dz_ref[blk, :A_W] = (du * gg[:, :A_W]).astype(dz_ref.dtype)
            dz_ref[blk, A_W:] = (dv * gg[:, A_W:]).astype(dz_ref.dtype)

    return pl.pallas_call(
        body, grid=(lay.nb // MT,),
        in_specs=[pl.BlockSpec((MT * TB, 2 * A_W), lambda j: (j, 0)), pl.BlockSpec((MT * TB, A_W), lambda j: (j, MCAT_A)),
                  pl.BlockSpec((4, CHUNK, CHUNK), lambda j: (0, 0, 0)), pl.BlockSpec((4, CHUNK, CHUNK), lambda j: (0, 0, 0)),
                  pl.BlockSpec((CHUNK, A_W), lambda j: (0, 0))],
        out_specs=[pl.BlockSpec((MT * TB, 2 * A_W), lambda j: (j, 0)), pl.BlockSpec((4, CHUNK, CHUNK), lambda j: (0, 0, 0)),
                   pl.BlockSpec((CHUNK, A_W), lambda j: (0, 0))],
        out_shape=[jax.ShapeDtypeStruct((lay.nt, 2 * A_W), MXU_DTYPE), jax.ShapeDtypeStruct((4, CHUNK, CHUNK), F32),
                   jax.ShapeDtypeStruct((CHUNK, A_W), F32)],
        compiler_params=_cp(("arbitrary",)), name=name)(z, da, ws, wst, bias)


def _band_constants():
    bands = np.zeros((2, 4, TB, TB), np.float32)
    inv = np.zeros((2, 4, TB, 1), np.float32)
    for kind, n in ((0, GRID_W), (1, TB)):
        for i, w in enumerate(POOL_WINDOWS):
            for t in range(TB):
                base, tt = (t // n) * n, t % n
                lo = min(max(tt - w // 2, 0), n)
                hi = min(max(tt - w // 2 + w, 0), n)
                bands[kind, i, t, base + lo:base + hi] = 1.0
                inv[kind, i, t, 0] = 1.0 / (hi - lo)
    return bands, inv


def _split3(x):
    a = x.astype(MXU_DTYPE)
    r1 = x - a.astype(F32)
    b = r1.astype(MXU_DTYPE)
    c = (r1 - b.astype(F32)).astype(MXU_DTYPE)
    return a, b, c


def _window_apply(band_ref, inv_ref, x, masks, transpose, mxu_exact=False):
    out = jnp.zeros_like(x)
    for i in range(4):
        xi = x * inv_ref[0, i] if transpose else x
        acc = None
        for part in ((xi.astype(MXU_DTYPE),) if mxu_exact else _split3(xi)):
            r = _tn(band_ref[0, i], part) if transpose else _nn(band_ref[0, i], part)
            acc = r if acc is None else acc + r
        if not transpose:
            acc = acc * inv_ref[0, i]
        out = jnp.where(masks[i], acc, out)
    return out


def _pool_specs(lay):
    kind = lambda j: jnp.where(j < lay.nctx // MT, 1, 0)
    return [pl.BlockSpec((1, 4, TB, TB), lambda j: (kind(j), 0, 0, 0)), pl.BlockSpec((1, 4, TB, 1), lambda j: (kind(j), 0, 0, 0))]


def _pool_fwd(lay, z, bands, inv, pw, scale, name):
    def body(p_ref, band_ref, inv_ref, pw_ref, sc_ref, o_ref):
        masks = _head_masks((TB, C_W))
        for blk in _blocks():
            p = p_ref[blk, :].astype(F32)
            diff = _window_apply(band_ref, inv_ref, p, masks, False, mxu_exact=True) - p
            o_ref[blk, :] = (_nn(diff.astype(MXU_DTYPE), pw_ref[...]) * sc_ref[...]).astype(o_ref.dtype)

    return pl.pallas_call(
        body, grid=(lay.nb // MT,),
        in_specs=[pl.BlockSpec((MT * TB, C_W), lambda j: (j, 4))] + _pool_specs(lay)
        + [pl.BlockSpec((C_W, C_W), lambda j: (0, 0)), pl.BlockSpec((1, C_W), lambda j: (0, 0))],
        out_specs=pl.BlockSpec((MT * TB, C_W), lambda j: (j, 0)),
        out_shape=jax.ShapeDtypeStruct((lay.nt, C_W), MXU_DTYPE),
        compiler_params=_cp(("parallel",)), name=name)(z, bands, inv, pw, scale)


def _pool_bwd(lay, z, dc, bands, inv, pw, scale, name):
    def body(p_ref, dc_ref, band_ref, inv_ref, pw_ref, sc_ref, dp_ref, dpw_ref, dsc_ref):
        j = pl.program_id(0)

        @pl.when(j == 0)
        def _():
            dpw_ref[...] = jnp.zeros_like(dpw_ref)
            dsc_ref[...] = jnp.zeros_like(dsc_ref)

        masks = _head_masks((TB, C_W))
        for blk in _blocks():
            p = p_ref[blk, :].astype(F32)
            dcv = dc_ref[blk, :].astype(F32)
            diff = _window_apply(band_ref, inv_ref, p, masks, False, mxu_exact=True) - p
            diff_b = diff.astype(MXU_DTYPE)
            pre = _nn(diff_b, pw_ref[...])
            dsc_ref[...] += jnp.sum(dcv * pre, axis=0, keepdims=True)
            dpre = dcv * sc_ref[...]
            dpre_b = dpre.astype(MXU_DTYPE)
            dpw_ref[...] += _tn(diff_b, dpre_b)
            ddiff = _nt(dpre_b, pw_ref[...])
            dp_ref[blk, :] = (_window_apply(band_ref, inv_ref, ddiff, masks, True) - ddiff).astype(dp_ref.dtype)

    return pl.pallas_call(
        body, grid=(lay.nb // MT,),
        in_specs=[pl.BlockSpec((MT * TB, C_W), lambda j: (j, 4)), pl.BlockSpec((MT * TB, C_W), lambda j: (j, MCAT_C))]
        + _pool_specs(lay)
        + [pl.BlockSpec((C_W, C_W), lambda j: (0, 0)), pl.BlockSpec((1, C_W), lambda j: (0, 0))],
        out_specs=[pl.BlockSpec((MT * TB, C_W), lambda j: (j, 0)), pl.BlockSpec((C_W, C_W), lambda j: (0, 0)),
                   pl.BlockSpec((1, C_W), lambda j: (0, 0))],
        out_shape=[jax.ShapeDtypeStruct((lay.nt, C_W), MXU_DTYPE), jax.ShapeDtypeStruct((C_W, C_W), F32),
                   jax.ShapeDtypeStruct((1, C_W), F32)],
        compiler_params=_cp(("arbitrary",)), name=name)(z, dc, bands, inv, pw, scale)


def _disc_math(lr, li, ldt, br, bi):
    dt = jnp.exp(ldt)
    e = jnp.exp(lr * dt)
    ar = e * jnp.cos(li * dt)
    ai = e * jnp.sin(li * dt)
    nr, ni = ar - 1.0, ai
    den = lr * lr + li * li
    qr = (nr * lr + ni * li) / den
    qi = (ni * lr - nr * li) / den
    return ar, ai, qr * br - qi * bi, qr * bi + qi * br


def _disc_fwd(lrx, lix, ldtx, brt, bit, name):
    def body(lr_ref, li_ref, ldt_ref, br_ref, bi_ref, ar_ref, ai_ref, obr_ref, obi_ref):
        ar, ai, obr, obi = _disc_math(lr_ref[...], li_ref[...], ldt_ref[...], br_ref[...], bi_ref[...])
        ar_ref[...] = ar
        ai_ref[...] = ai
        obr_ref[...] = obr
        obi_ref[...] = obi

    sh = jax.ShapeDtypeStruct(lrx.shape, F32)
    return pl.pallas_call(body, out_shape=[sh, sh, sh, sh], name=name)(lrx, lix, ldtx, brt, bit)


def _disc_bwd(lrx, lix, ldtx, brt, bit, dar, dai, dbr, dbi, name):
    nrow = lrx.shape[0] // SSM_H

    def body(lr_ref, li_ref, ldt_ref, br_ref, bi_ref, dar_ref, dai_ref, dbr_ref, dbi_ref,
             glr_ref, gli_ref, gdt_ref, gbr_ref, gbi_ref):
        _, vjp = jax.vjp(_disc_math, lr_ref[...], li_ref[...], ldt_ref[...], br_ref[...], bi_ref[...])
        glr, gli, gdt, gbr, gbi = vjp((dar_ref[...], dai_ref[...], dbr_ref[...], dbi_ref[...]))
        glr_ref[...] = jnp.sum(glr.reshape(nrow, SSM_H, SSM_P), axis=1)
        gli_ref[...] = jnp.sum(gli.reshape(nrow, SSM_H, SSM_P), axis=1)
        gdt_ref[...] = jnp.sum(jnp.sum(gdt.reshape(nrow, SSM_H, SSM_P), axis=1), axis=-1, keepdims=True)
        gbr_ref[...] = gbr
        gbi_ref[...] = gbi

    small = jax.ShapeDtypeStruct((nrow, SSM_P), F32)
    big = jax.ShapeDtypeStruct(lrx.shape, F32)
    return pl.pallas_call(body, out_shape=[small, small, jax.ShapeDtypeStruct((nrow, 1), F32), big, big],
                          name=name)(lrx, lix, ldtx, brt, bit, dar, dai, dbr, dbi)


HS = 1024
GQ, QC, QS = 8, 128, 512
LC = QS
SCAN_UNROLL = ST


def _scan_steps(step, carry):
    if SCAN_UNROLL >= ST:
        for s in range(ST):
            carry = step(s, carry)
        return carry

    def body(i, c):
        for j in range(SCAN_UNROLL):
            c = step(i * SCAN_UNROLL + j, c)
        return c

    return lax.fori_loop(0, ST // SCAN_UNROLL, body, carry)


def _tile_row(s):
    return s * 8 if isinstance(s, int) else pl.multiple_of(s * 8, 8)


def _dir_cat(x, d0, qq):
    xq = x[:, QC * qq:QC * qq + QC]
    zero = jnp.zeros_like(xq)
    return jnp.concatenate([jnp.where(d0, xq, zero), jnp.where(d0, zero, xq)], axis=1)


def _dir_pick(x, d0):
    return jnp.where(d0, x[:, :QC], x[:, QC:])


def _d0_rows(n):
    row = lax.broadcasted_iota(jnp.int32, (n, 1), 0)
    return jnp.bitwise_and(row, 4) == 0


def _scan_perm(bl):
    n = 2 * bl * ST
    p = np.zeros((n, n), np.float32)
    for s in range(ST):
        for d in range(2):
            for b in range(bl):
                t = s if d == 0 else ST - 1 - s
                p[s * 2 * bl + d * bl + b, d * bl * ST + b * ST + t] = 1.0
    return p


def _scan_maps(lay):
    spc = TB // ST
    nlc = lay.nlb * spc

    def fwd(k):
        return k // spc, k % spc

    def rev(k):
        cpos = nlc - 1 - jnp.maximum(k - spc, 0)
        return jnp.where(k < spc, 0, 1 + cpos // spc), jnp.where(k < spc, spc - 1 - k, cpos % spc)

    return fwd, rev


def _pack_rows(f_ref, r_ref, p_ref, rc):
    st = jnp.concatenate([f_ref[0].reshape(rc // 2, 256), r_ref[0].reshape(rc // 2, 256)], axis=0).astype(MXU_DTYPE)
    return _nn(p_ref[...], st).astype(MXU_DTYPE)


def _side_split(refs, n_in, n_out, n_scr, side):
    ns = side.n if side is not None else 0
    ins, sin = refs[:n_in], refs[n_in:n_in + ns]
    o0 = n_in + ns
    outs, sout = refs[o0:o0 + n_out], refs[o0 + n_out:o0 + n_out + ns]
    s0 = o0 + n_out + ns
    return ins + outs + refs[s0:s0 + n_scr], (sin, sout, refs[s0 + n_scr:])


def _side_start(side, srefs, first):
    if side is not None:
        @pl.when(first)
        def _():
            side.start(*srefs)


def _side_wait(side, srefs, last):
    if side is not None:
        @pl.when(last)
        def _():
            side.wait(*srefs)


def _ssm_fwd(lay, z, perm, bh, ch, ar8, ai8, name, side=None):
    bl = lay.bl
    rc = ST * 2 * bl
    nch = lay.nr * (TB // ST)
    fwd, rev = _scan_maps(lay)
    z4 = z.reshape(lay.nr, bl, TB, z.shape[1])

    def body(*refs):
        own, srefs = _side_split(refs, 7, 4, 2, side)
        uf_ref, ur_ref, p_ref, bh_ref, ch_ref, ar_ref, ai_ref, yf_ref, yr_ref, hst_ref, hsv_ref, hs, hc = own
        f, k = pl.program_id(0), pl.program_id(1)
        _side_start(side, srefs, jnp.logical_and(f == 0, k == 0))

        @pl.when(k == 0)
        def _():
            hc[...] = jnp.zeros_like(hc)

        hst_ref[0] = hc[...]
        d0 = _d0_rows(rc)
        uv = _pack_rows(uf_ref, ur_ref, p_ref, rc)
        for q in range(2):
            cr, ci = 2 * QS * q, 2 * QS * q + QS
            hs[:, cr:cr + 2 * QS] = _nn(_dir_cat(uv, d0, q), bh_ref[q])
            ar = ar_ref[:, QS * q:QS * q + QS]
            ai = ai_ref[:, QS * q:QS * q + QS]

            def step(s, carry, cr=cr, ci=ci, ar=ar, ai=ai):
                hr, hi = carry
                base = _tile_row(s)
                nr = ar * hr - ai * hi + hs[pl.ds(base, 8), cr:cr + LC]
                ni = ar * hi + ai * hr + hs[pl.ds(base, 8), ci:ci + LC]
                hs[pl.ds(base, 8), cr:cr + LC] = nr
                hs[pl.ds(base, 8), ci:ci + LC] = ni
                return nr, ni

            hr, hi = _scan_steps(step, (hc[:, cr:cr + LC], hc[:, ci:ci + LC]))
            hc[:, cr:cr + LC] = hr
            hc[:, ci:ci + LC] = hi
        hsv_ref[0] = hs[...].astype(hsv_ref.dtype)
        yi = jnp.concatenate(
            [_dir_pick(_nn(hsv_ref[0, :, 2 * QS * q:2 * QS * (q + 1)], ch_ref[q]), d0) for q in range(2)], axis=1)
        yd = _tn(p_ref[...], yi.astype(MXU_DTYPE))
        yf_ref[0] = yd[:rc // 2].reshape(bl, ST, 256).astype(yf_ref.dtype)
        yr_ref[0] = yd[rc // 2:].reshape(bl, ST, 256).astype(yr_ref.dtype)
        _side_wait(side, srefs, jnp.logical_and(f == 1, k == nch - 1))

    sd = side if side is not None else _Side([])
    blk = (1, bl, ST, 256)
    ysh = jax.ShapeDtypeStruct((lay.nr, bl, TB, B_W), MXU_DTYPE)
    outs = pl.pallas_call(
        body, grid=(2, nch),
        in_specs=[pl.BlockSpec(blk, lambda f, k: (fwd(k)[0], 0, fwd(k)[1], 2 + f)),
                  pl.BlockSpec(blk, lambda f, k: (rev(k)[0], 0, rev(k)[1], 2 + f)),
                  pl.BlockSpec((rc, rc), lambda f, k: (0, 0)),
                  pl.BlockSpec((2, 2 * QC, 2 * QS), lambda f, k: (f, 0, 0)),
                  pl.BlockSpec((2, 2 * QS, 2 * QC), lambda f, k: (f, 0, 0)),
                  pl.BlockSpec((8, HS), lambda f, k: (0, f)), pl.BlockSpec((8, HS), lambda f, k: (0, f))] + sd.in_specs,
        out_specs=[pl.BlockSpec(blk, lambda f, k: (fwd(k)[0], 0, fwd(k)[1], f)),
                   pl.BlockSpec(blk, lambda f, k: (rev(k)[0], 0, rev(k)[1], f)),
                   pl.BlockSpec((1, 8, 2 * HS), lambda f, k: (k, 0, f)),
                   pl.BlockSpec((1, rc, 2 * HS), lambda f, k: (k, 0, f))] + sd.out_specs,
        out_shape=[ysh, ysh, jax.ShapeDtypeStruct((nch, 8, 4 * HS), F32),
                   jax.ShapeDtypeStruct((nch, rc, 4 * HS), MXU_DTYPE)] + sd.out_shape,
        scratch_shapes=[pltpu.VMEM((rc, 2 * HS), F32), pltpu.VMEM((8, 2 * HS), F32)] + (sd.scratch if side is not None else []),
        compiler_params=_cp(("arbitrary", "arbitrary")), name=name)(z4, z4, perm, bh, ch, ar8, ai8, *sd.arrays)
    yf, yr, hst, hsv = outs[:4]
    return yf.reshape(lay.nt, B_W), yr.reshape(lay.nt, B_W), (hst, hsv), list(outs[4:])


def _ssm_bwd(lay, z, dy, perm, hst, bh, ch, ar8, ai8, name, side=None):
    bl = lay.bl
    rc = ST * 2 * bl
    nch = lay.nr * (TB // ST)
    fwd, rev = _scan_maps(lay)
    z4 = z.reshape(lay.nr, bl, TB, z.shape[1])
    dy4 = dy.reshape(lay.nr, bl, TB, B_W)

    hst, hsv = hst

    def body(*refs):
        own, srefs = _side_split(refs, 11, 6, 5, side)
        (uf_ref, ur_ref, dyf_ref, dyr_ref, p_ref, hst_ref, hsv_ref, bh_ref, ch_ref, ar_ref, ai_ref,
         duf_ref, dur_ref, dbh_ref, dch_ref, dar_ref, dai_ref, hs, es, ec, accr, acci) = own
        f, k = pl.program_id(0), pl.program_id(1)
        _side_start(side, srefs, jnp.logical_and(f == 0, k == 0))

        @pl.when(k == 0)
        def _():
            ec[...] = jnp.zeros_like(ec)
            accr[...] = jnp.zeros_like(accr)
            acci[...] = jnp.zeros_like(acci)
            dbh_ref[...] = jnp.zeros_like(dbh_ref)
            dch_ref[...] = jnp.zeros_like(dch_ref)

        d0 = _d0_rows(rc)
        uv = _pack_rows(uf_ref, ur_ref, p_ref, rc)
        dyv = _pack_rows(dyf_ref, dyr_ref, p_ref, rc)

        hs[0:8, :] = hst_ref[0]
        hs[8:, :] = hsv_ref[0].astype(F32)
        ucat, dycat = [], []
        for q in range(2):
            cr = 2 * QS * q
            ucat.append(_dir_cat(uv, d0, q))
            dycat.append(_dir_cat(dyv, d0, q))
            dch_ref[q] += _tn(hsv_ref[0, :, cr:cr + 2 * QS], dycat[q])
            es[:, cr:cr + 2 * QS] = _nt(dycat[q], ch_ref[q])

        dui = []
        for q in range(2):
            cr, ci = 2 * QS * q, 2 * QS * q + QS
            ar = ar_ref[:, QS * q:QS * q + QS]
            ai = ai_ref[:, QS * q:QS * q + QS]

            def bstep(i, carry, cr=cr, ci=ci, ar=ar, ai=ai):
                er, ei, sr, si = carry
                base = _tile_row(ST - 1 - i)
                ner = es[pl.ds(base, 8), cr:cr + LC] + ar * er + ai * ei
                nei = es[pl.ds(base, 8), ci:ci + LC] - ai * er + ar * ei
                es[pl.ds(base, 8), cr:cr + LC] = ner
                es[pl.ds(base, 8), ci:ci + LC] = nei
                hpr = hs[pl.ds(base, 8), cr:cr + LC]
                hpi = hs[pl.ds(base, 8), ci:ci + LC]
                return ner, nei, sr + ner * hpr + nei * hpi, si - ner * hpi + nei * hpr

            lo = QS * q
            er, ei, sr, si = _scan_steps(
                bstep, (ec[:, cr:cr + LC], ec[:, ci:ci + LC], accr[:, lo:lo + LC], acci[:, lo:lo + LC]))
            ec[:, cr:cr + LC] = er
            ec[:, ci:ci + LC] = ei
            accr[:, lo:lo + LC] = sr
            acci[:, lo:lo + LC] = si
            eb = es[:, cr:cr + 2 * QS].astype(MXU_DTYPE)
            dui.append(_dir_pick(_nt(eb, bh_ref[q]), d0))
            dbh_ref[q] += _tn(ucat[q], eb)

        dud = _tn(p_ref[...], jnp.concatenate(dui, axis=1).astype(MXU_DTYPE))
        duf_ref[0] = dud[:rc // 2].reshape(bl, ST, 256).astype(duf_ref.dtype)
        dur_ref[0] = dud[rc // 2:].reshape(bl, ST, 256).astype(dur_ref.dtype)

        @pl.when(k == nch - 1)
        def _():
            for d in range(2):
                dar_ref[d:d + 1, :] = jnp.sum(accr[4 * d:4 * d + 4, :], axis=0, keepdims=True)
                dai_ref[d:d + 1, :] = jnp.sum(acci[4 * d:4 * d + 4, :], axis=0, keepdims=True)

        _side_wait(side, srefs, jnp.logical_and(f == 1, k == nch - 1))

    sd = side if side is not None else _Side([])
    last = lambda k: nch - 1 - k
    blk = (1, bl, ST, 256)
    fspec = lambda c0: pl.BlockSpec(blk, lambda f, k: (fwd(last(k))[0], 0, fwd(last(k))[1], c0 + f))
    rspec = lambda c0: pl.BlockSpec(blk, lambda f, k: (rev(last(k))[0], 0, rev(last(k))[1], c0 + f))
    dush = jax.ShapeDtypeStruct((lay.nr, bl, TB, B_W), MXU_DTYPE)
    outs = pl.pallas_call(
        body, grid=(2, nch),
        in_specs=[fspec(2), rspec(2), fspec(0), rspec(0),
                  pl.BlockSpec((rc, rc), lambda f, k: (0, 0)),
                  pl.BlockSpec((1, 8, 2 * HS), lambda f, k: (last(k), 0, f)),
                  pl.BlockSpec((1, rc, 2 * HS), lambda f, k: (last(k), 0, f)),
                  pl.BlockSpec((2, 2 * QC, 2 * QS), lambda f, k: (f, 0, 0)),
                  pl.BlockSpec((2, 2 * QS, 2 * QC), lambda f, k: (f, 0, 0)),
                  pl.BlockSpec((8, HS), lambda f, k: (0, f)), pl.BlockSpec((8, HS), lambda f, k: (0, f))] + sd.in_specs,
        out_specs=[fspec(0), rspec(0),
                   pl.BlockSpec((2, 2 * QC, 2 * QS), lambda f, k: (f, 0, 0)),
                   pl.BlockSpec((2, 2 * QS, 2 * QC), lambda f, k: (f, 0, 0)),
                   pl.BlockSpec((2, HS), lambda f, k: (0, f)), pl.BlockSpec((2, HS), lambda f, k: (0, f))] + sd.out_specs,
        out_shape=[dush, dush, jax.ShapeDtypeStruct((4, 2 * QC, 2 * QS), F32),
                   jax.ShapeDtypeStruct((4, 2 * QS, 2 * QC), F32), jax.ShapeDtypeStruct((2, 2 * HS), F32),
                   jax.ShapeDtypeStruct((2, 2 * HS), F32)] + sd.out_shape,
        scratch_shapes=[pltpu.VMEM((rc + 8, 2 * HS), F32), pltpu.VMEM((rc, 2 * HS), F32), pltpu.VMEM((8, 2 * HS), F32),
                        pltpu.VMEM((8, HS), F32), pltpu.VMEM((8, HS), F32)] + (sd.scratch if side is not None else []),
        compiler_params=_cp(("arbitrary", "arbitrary")), name=name)(
            z4, z4, dy4, dy4, perm, hst, hsv, bh, ch, ar8, ai8, *sd.arrays)
    duf, dur, dbh, dch, dar, dai = outs[:6]
    return duf.reshape(lay.nt, B_W), dur.reshape(lay.nt, B_W), dbh, dch, dar, dai, list(outs[6:])


def _glu_fwd(lay, z, yf, yr, dvec, wglu, bglu, name):
    def body(u_ref, yf_ref, yr_ref, d_ref, w_ref, b_ref, o_ref, y_ref):
        y = yf_ref[...].astype(F32) + yr_ref[...].astype(F32) + d_ref[...] * u_ref[...].astype(F32)
        y_ref[...] = y
        g = _gelu(y)
        pre = _nn(g.astype(MXU_DTYPE), w_ref[...]) + b_ref[...]
        o_ref[...] = (g * _sigmoid(pre)).astype(o_ref.dtype)

    tok = pl.BlockSpec((MT * TB, B_W), lambda j: (j, 0))
    vec = pl.BlockSpec((1, B_W), lambda j: (0, 0))
    return pl.pallas_call(
        body, grid=(lay.nb // MT,),
        in_specs=[pl.BlockSpec((MT * TB, B_W), lambda j: (j, 1)), tok, tok, vec,
                  pl.BlockSpec((B_W, B_W), lambda j: (0, 0)), vec],
        out_specs=[tok, tok],
        out_shape=[jax.ShapeDtypeStruct((lay.nt, B_W), MXU_DTYPE), jax.ShapeDtypeStruct((lay.nt, B_W), F32)],
        compiler_params=_cp(("parallel",)), name=name)(z, yf, yr, dvec, wglu, bglu)


def _glu_bwd(lay, z, y, ds, dvec, wglu, bglu, name):
    def body(u_ref, y_ref, ds_ref, d_ref, w_ref, b_ref, dy_ref, dud_ref, dw_ref, db_ref, dd_ref):
        j = pl.program_id(0)

        @pl.when(j == 0)
        def _():
            dw_ref[...] = jnp.zeros_like(dw_ref)
            db_ref[...] = jnp.zeros_like(db_ref)
            dd_ref[...] = jnp.zeros_like(dd_ref)

        yv = y_ref[...]
        g = _gelu(yv)
        gb = g.astype(MXU_DTYPE)
        sg = _sigmoid(_nn(gb, w_ref[...]) + b_ref[...])
        dsv = ds_ref[...].astype(F32)
        dpre = dsv * g * sg * (1.0 - sg)
        dpre_b = dpre.astype(MXU_DTYPE)
        dg = dsv * sg + _nt(dpre_b, w_ref[...])
        dw_ref[...] += _tn(gb, dpre_b)
        db_ref[...] += jnp.sum(dpre, axis=0, keepdims=True)
        dy = dg * _gelu_grad(yv)
        dy_ref[...] = dy.astype(dy_ref.dtype)
        dd_ref[...] += jnp.sum(dy * u_ref[...].astype(F32), axis=0, keepdims=True)
        dud_ref[...] = (dy * d_ref[...]).astype(dud_ref.dtype)

    tok = pl.BlockSpec((MT * TB, B_W), lambda j: (j, 0))
    vec = pl.BlockSpec((1, B_W), lambda j: (0, 0))
    mat = pl.BlockSpec((B_W, B_W), lambda j: (0, 0))
    vsh = jax.ShapeDtypeStruct((1, B_W), F32)
    return pl.pallas_call(
        body, grid=(lay.nb // MT,),
        in_specs=[pl.BlockSpec((MT * TB, B_W), lambda j: (j, 1)), tok, tok, vec, mat, vec],
        out_specs=[tok, tok, mat, vec, vec],
        out_shape=[jax.ShapeDtypeStruct((lay.nt, B_W), MXU_DTYPE), jax.ShapeDtypeStruct((lay.nt, B_W), F32),
                   jax.ShapeDtypeStruct((B_W, B_W), F32), vsh, vsh],
        compiler_params=_cp(("arbitrary",)), name=name)(z, y, ds, dvec, wglu, bglu)


def _dz_assemble(lay, dz_a, duf, dur, dud, dz_p, name):
    def body(a_ref, f_ref, r_ref, d_ref, p_ref, o_ref):
        o_ref[:, :2 * A_W] = a_ref[...].astype(o_ref.dtype)
        o_ref[:, 2 * A_W:2 * A_W + B_W] = (f_ref[...].astype(F32) + r_ref[...].astype(F32) + d_ref[...]).astype(o_ref.dtype)
        o_ref[:, 2 * A_W + B_W:] = p_ref[...].astype(o_ref.dtype)

    spec = lambda w: pl.BlockSpec((MT * TB, w), lambda j: (j, 0))
    return pl.pallas_call(
        body, grid=(lay.nb // MT,), in_specs=[spec(2 * A_W), spec(B_W), spec(B_W), spec(B_W), spec(C_W)],
        out_specs=spec(D_IN), out_shape=jax.ShapeDtypeStruct((lay.nt, D_IN), MXU_DTYPE),
        compiler_params=_cp(("parallel",)), name=name)(dz_a, duf, dur, dud, dz_p)


def _expand_rows(a):
    return jnp.broadcast_to(a[:, :, None, :], (2, SSM_G, SSM_H, SSM_P)).reshape(-1, SSM_P)


def _ssm_params(lam_re, lam_im, log_dt, b_re, b_im, c_re, c_im, name):
    lrx, lix = _expand_rows(lam_re), _expand_rows(lam_im)
    ldtx = _expand_rows(jnp.broadcast_to(log_dt[:, :, None], (2, SSM_G, SSM_P)))
    brt = jnp.transpose(b_re, (0, 1, 3, 2)).reshape(-1, SSM_P)
    bit = jnp.transpose(b_im, (0, 1, 3, 2)).reshape(-1, SSM_P)
    arx, aix, bbr, bbi = _disc_fwd(lrx, lix, ldtx, brt, bit, name)
    ar = arx.reshape(2, SSM_G, SSM_H, SSM_P)[:, :, 0].reshape(2, SSM_G * SSM_P)
    ai = aix.reshape(2, SSM_G, SSM_H, SSM_P)[:, :, 0].reshape(2, SSM_G * SSM_P)
    eye = jnp.eye(GQ, dtype=F32)

    def bmat(bt):
        t = bt.reshape(2, 4, GQ, SSM_H, SSM_P)
        return jnp.einsum('dqghp,gk->qdghkp', t, eye).reshape(4, 2 * QC, QS)

    bh = jnp.concatenate([bmat(bbr), bmat(bbi)], axis=-1).astype(MXU_DTYPE)

    def cmat(c):
        t = c.reshape(2, 4, GQ, SSM_H, SSM_P)
        return jnp.einsum('dqghp,gk->qgpdkh', t, eye).reshape(4, QS, 2 * QC)

    ch = jnp.concatenate([cmat(c_re), -cmat(c_im)], axis=1).astype(MXU_DTYPE)

    def rows8(a):
        return jnp.repeat(a, 4, axis=0)

    return dict(lrx=lrx, lix=lix, ldtx=ldtx, brt=brt, bit=bit, bh=bh, ch=ch, ar8=rows8(ar), ai8=rows8(ai))


def _ssm_param_grads(sp, dbh, dch, dar, dai, name):
    def bdiag(m):
        t = m.reshape(4, 2, GQ, SSM_H, GQ, SSM_P)
        return jnp.einsum('qdghgp->dqghp', t).reshape(-1, SSM_P)

    dbr, dbi = bdiag(dbh[..., :QS]), bdiag(dbh[..., QS:])

    def cdiag(m):
        t = m.reshape(4, GQ, SSM_P, 2, GQ, SSM_H)
        return jnp.einsum('qgpdgh->dqghp', t).reshape(2, SSM_G, SSM_H, SSM_P)

    dc_re, dc_im = cdiag(dch[:, :QS]), -cdiag(dch[:, QS:])

    def hrow(a):
        t = a.reshape(2, SSM_G, 1, SSM_P)
        return jnp.concatenate([t, jnp.zeros((2, SSM_G, SSM_H - 1, SSM_P), F32)], axis=2).reshape(-1, SSM_P)

    glr, gli, gdt, gbr, gbi = _disc_bwd(sp["lrx"], sp["lix"], sp["ldtx"], sp["brt"], sp["bit"],
                                        hrow(dar), hrow(dai), dbr, dbi, name)
    to_b = lambda g: jnp.transpose(g.reshape(2, SSM_G, SSM_H, SSM_P), (0, 1, 3, 2))
    return dict(ssm_lam_re=glr.reshape(2, SSM_G, SSM_P), ssm_lam_im=gli.reshape(2, SSM_G, SSM_P),
                ssm_log_dt=gdt.reshape(2, SSM_G), ssm_b_re=to_b(gbr), ssm_b_im=to_b(gbi),
                ssm_c_re=dc_re, ssm_c_im=dc_im)


def _layer_consts(p):
    c = {}
    c["ws"] = p["sgu_w"].astype(MXU_DTYPE)
    c["wst"] = jnp.transpose(p["sgu_w"], (0, 2, 1)).astype(MXU_DTYPE)
    c["gbias"] = jnp.repeat(p["sgu_b"].T, 64, axis=1)
    pw = jnp.zeros((C_W, C_W), F32)
    for i in range(4):
        pw = pw.at[64 * i:64 * i + 64, 64 * i:64 * i + 64].set(p["pool_w"][i])
    c["pw"] = pw.astype(MXU_DTYPE)
    c["pscale"] = p["pool_scale"].reshape(1, C_W)
    c["dvec"] = p["ssm_d"].reshape(1, B_W)
    c["bglu"] = p["glu_b"].reshape(1, B_W)
    return c


def _layer_fwd(lay, i, x, modarr, p, w, cst, sp, bands, inv, perm, sides=None, last=False):
    n = f"l{i}_"
    sides = sides or {}
    win_side, win_fill = sides.get("win", (None, None))
    ssm_side, ssm_fill = sides.get("ssm", (None, None))
    ffn_side, ffn_fill = sides.get("ffn", (None, None))
    res = {"x0": x}
    h = _normmod_fwd(lay, x, p["norm_mix_pre"].reshape(1, D), modarr, 0, 1, n + "nm1")
    z = _mm([(h, w["win_t"])], True, MXU_DTYPE, n + "win", side=win_side)
    if win_side is not None:
        z, extra = z
        win_fill(extra)
    a = _gate_fwd(lay, z, cst["ws"], cst["gbias"], n + "gate")
    yf, yr, hst, extra = _ssm_fwd(lay, z, perm, sp["bh"], sp["ch"], sp["ar8"], sp["ai8"], n + "ssm", ssm_side)
    if ssm_side is not None:
        ssm_fill(extra)
    s, y = _glu_fwd(lay, z, yf, yr, cst["dvec"], w["wglu"], cst["bglu"], n + "glu")
    c = _pool_fwd(lay, z, bands, inv, cst["pw"], cst["pscale"], n + "pool")
    mcat = jnp.concatenate([s, a, c], axis=1)
    res["wout_p"] = _perm_wout(w["wout"])
    m = _mm([(mcat, res["wout_p"])], False, MXU_DTYPE, n + "wout")
    x1, h2 = _resnorm_normmod_fwd(lay, x, m, p["norm_mix_post"].reshape(1, D), p["norm_ffn_pre"].reshape(1, D),
                                  modarr, 2, 3, 4, n + "rn1nm2")
    g, u, act, extra = _ffn_up(h2, w["wg_t"], w["wu_t"], n + "ffn_up", ffn_side)
    if ffn_side is not None:
        ffn_fill(extra)
    f = _mm([(act, w["wd"])], False, MXU_DTYPE, n + "ffn_down")
    res.update(h=h, z=z, hst=hst, y=y, mcat=mcat, m=m, x1=x1, h2=h2, g=g, u=u, act=act, f=f)
    if last:
        return None, res
    x2 = _resnorm_fwd(lay, x1, f, p["norm_ffn_post"].reshape(1, D), modarr, 5, n + "rn2")
    return x2, res


def _layer_bwd(lay, i, dx2, modarr, p, w, cst, sp, bands, inv, perm, res, side_fns=None):
    n = f"l{i}b_"
    big, small = {}, {}
    side_fns = side_fns or {}
    side_of = lambda key: side_fns[key](big) if key in side_fns else None
    df, dg2, gpost2 = _resnorm_bwd(lay, dx2, res["f"], p["norm_ffn_post"].reshape(1, D), modarr, 5, n + "rn2")
    big["wd"] = _mm_tn(res["act"], df, MXU_DTYPE, n + "dwd")
    dg, du, early = _ffn_down_bwd(df, w["wd"], res["g"], res["u"], n + "ffn_down", side_of("ffn_down"))
    dh2_side = side_of("dh2")
    dh2 = _mm([(dg, w["wg_t"]), (du, w["wu_t"])], False, MXU_DTYPE, n + "dh2", side=dh2_side)
    if dh2_side is not None:
        dh2, ex = dh2
        early = early + ex
    big["wg_t"] = _mm_tn(dg, res["h2"], MXU_DTYPE, n + "dwg")
    big["wu_t"] = _mm_tn(du, res["h2"], MXU_DTYPE, n + "dwu")
    dx1, dm, dsh2, dsc2, gpre2, dg1, gpost1 = _normmod_resnorm_bwd(
        lay, res["x1"], dh2, dx2, p["norm_ffn_pre"].reshape(1, D), res["m"], p["norm_mix_post"].reshape(1, D), modarr,
        4, 2, n + "nm2rn1")
    big["wout"] = _unperm_wout(_mm_tn(res["mcat"], dm, MXU_DTYPE, n + "dwout"))
    dmcat = _mm([(dm, res["wout_p"])], True, MXU_DTYPE, n + "dmcat")
    z = res["z"]
    dz_a, dws, dgb = _gate_bwd(lay, z, dmcat, cst["ws"], cst["wst"], cst["gbias"], n + "gate")
    dy, dud, dwglu, dbglu, ddvec = _glu_bwd(lay, z, res["y"], dmcat, cst["dvec"], w["wglu"], cst["bglu"], n + "glu")
    big["wglu"] = dwglu.astype(MXU_DTYPE)
    duf, dur, dbh, dch, dar, dai, ex = _ssm_bwd(lay, z, dy, perm, res["hst"], sp["bh"], sp["ch"], sp["ar8"],
                                                sp["ai8"], n + "ssm", side_of("ssm"))
    early = early + ex
    dz_p, dpw, dpsc = _pool_bwd(lay, z, dmcat, bands, inv, cst["pw"], cst["pscale"], n + "pool")
    dz = _dz_assemble(lay, dz_a, duf, dur, dud, dz_p, n + "dz")
    big["win_t"] = _mm_tn(dz, res["h"], MXU_DTYPE, n + "dwin")
    dh_side = side_of("dh")
    dh = _mm([(dz, w["win_t"])], False, MXU_DTYPE, n + "dh", side=dh_side)
    if dh_side is not None:
        dh, ex = dh
        early = early + ex
    dx, dsh1, dsc1, gpre1 = _normmod_bwd(lay, res["x0"], dh, dx1, p["norm_mix_pre"].reshape(1, D), modarr, 1, n + "nm1",
                                         latent_only=(i == 0))

    small.update(norm_mix_pre=gpre1[0], norm_mix_post=gpost1[0], norm_ffn_pre=gpre2[0], norm_ffn_post=gpost2[0])
    small["sgu_w"] = dws
    small["sgu_b"] = jnp.sum(dgb.reshape(CHUNK, 4, 64), axis=-1).T
    small.update(_ssm_param_grads(sp, dbh, dch, dar, dai, n + "disc"))
    small["ssm_d"] = ddvec.reshape(SSM_G, SSM_H)
    small["glu_b"] = dbglu[0]
    small["pool_w"] = jnp.stack([dpw[64 * k:64 * k + 64, 64 * k:64 * k + 64] for k in range(4)])
    small["pool_scale"] = dpsc[0]
    dmod = jnp.concatenate([dsh1, dsc1, dg1, dsh2, dsc2, dg2], axis=1)[:lay.bl + 1]
    dmod = jnp.concatenate([dmod, jnp.zeros((8 - lay.bl - 1, 6, D), F32)], axis=0)
    return dx, big, small, dmod, early


def _perm_wout(w):
    return w.reshape(4, D // 4, D)[np.array(WOUT_PERM)].reshape(D, D)


def _unperm_wout(g):
    return g.reshape(4, D // 4, D)[np.array(WOUT_INV)].reshape(D, D)


SMALL_NAMES = ["norm_mix_pre", "norm_mix_post", "norm_ffn_pre", "norm_ffn_post", "sgu_w", "sgu_b", "ssm_lam_re",
               "ssm_lam_im", "ssm_log_dt", "ssm_b_re", "ssm_b_im", "ssm_c_re", "ssm_c_im", "ssm_d", "glu_b", "pool_w",
               "pool_scale"]
BIG_NAMES = ["win_t", "wout", "wglu", "wg_t", "wu_t", "wd"]


def _sincos_2d(rows, cols, dim):
    quarter = dim // 4
    omega = 1.0 / (10000.0 ** (jnp.arange(quarter, dtype=F32) / quarter))
    r = jnp.arange(rows, dtype=F32)[:, None] * omega
    cc = jnp.arange(cols, dtype=F32)[:, None] * omega
    er = jnp.concatenate([jnp.sin(r), jnp.cos(r)], axis=-1)
    ec = jnp.concatenate([jnp.sin(cc), jnp.cos(cc)], axis=-1)
    pe = jnp.concatenate([jnp.broadcast_to(er[:, None, :], (rows, cols, dim // 2)),
                          jnp.broadcast_to(ec[None, :, :], (rows, cols, dim // 2))], axis=-1)
    return pe.reshape(rows * cols, dim)


def _core(x, ctx, target, mods_local, params, weights, w_sides=None, g_side_fns=None):
    bl, lat, _ = x.shape
    assert bl == 4 and lat % TB == 0, "the scan fills 8 sublanes with 2 directions x 4 sequences"
    lay = _Layout(bl, lat)
    pe = _sincos_2d(lat // GRID_W, GRID_W, D)
    bands_np, inv_np = _band_constants()
    bands, inv = jnp.asarray(bands_np, MXU_DTYPE), jnp.asarray(inv_np, F32)
    perm = jnp.asarray(_scan_perm(bl), MXU_DTYPE)
    csts, sps, ress, wls = [], [], [], []
    for i in range(2):
        csts.append(_layer_consts(params[i]))
        p = params[i]
        sps.append(_ssm_params(p["ssm_lam_re"], p["ssm_lam_im"], p["ssm_log_dt"], p["ssm_b_re"], p["ssm_b_im"],
                               p["ssm_c_re"], p["ssm_c_im"], f"l{i}_disc"))
        wls.append(dict(weights[i]))

    embed_side, embed_fill = (w_sides[0].get("embed") if w_sides else None) or (None, None)
    xt, extra = _embed(lay, x, ctx, pe, embed_side)
    if embed_side is not None:
        embed_fill(wls, extra)
    if callable(mods_local):
        mods_local = mods_local()
    modarrs = [lay.mod_tiles(mods_local[i]) for i in range(2)]
    for i in range(2):
        sides = {}
        for key, (side, fill) in ((w_sides or [{}, {}])[i]).items():
            sides[key] = (side, functools.partial(fill, wls))
        xt, res = _layer_fwd(lay, i, xt, modarrs[i], params[i], wls[i], csts[i], sps[i], bands, inv, perm, sides,
                             last=(i == 1))
        ress.append(res)
    dx, lossv = _resnorm_loss(lay, ress[1]["x1"], ress[1]["f"], params[1]["norm_ffn_post"].reshape(1, D), modarrs[1], 5,
                              target)
    bigs, smalls, dmods, early = [None, None], [None, None], [None, None], []
    for i in (1, 0):
        fns = {}
        if i == 0 and g_side_fns is not None:
            fns = {key: functools.partial(fn, bigs[1]) for key, fn in g_side_fns.items()}
        dx, bigs[i], smalls[i], dmods[i], ex = _layer_bwd(lay, i, dx, modarrs[i], params[i], wls[i], csts[i], sps[i],
                                                           bands, inv, perm, ress[i], fns)
        early += ex
    return lossv[0, 0], dx.reshape(bl, lat, D), bigs, smalls, dmods, early


def _my_index():
    return 4 * lax.axis_index("x") + 2 * lax.axis_index("y") + lax.axis_index("c")


def _peer(k):
    x, y, c = lax.axis_index("x"), lax.axis_index("y"), lax.axis_index("c")
    kx, ky, kc = (k >> 2) & 1, (k >> 1) & 1, k & 1
    px = 1 - x if kx else x
    py = 1 - y if ky else y
    pc = 1 - c if kc else c
    return (px, py, pc), 4 * px + 2 * py + pc


class _Side:
    def __init__(self, items):
        self.items = items
        self.n = len(items)
        self.ncopies = sum(len(it[2]) for it in items)
        self.arrays = [it[0] for it in items]
        anyspec = pl.BlockSpec(memory_space=pl.ANY)
        self.in_specs = [anyspec] * self.n
        self.out_specs = [anyspec] * self.n
        self.out_shape = [jax.ShapeDtypeStruct((slots,) + tuple(a.shape) if mode == "gather" else tuple(a.shape), a.dtype)
                          for a, mode, ks, slots in items]
        self.scratch = [pltpu.SemaphoreType.DMA((self.ncopies,)), pltpu.SemaphoreType.DMA((self.ncopies,)),
                        pltpu.SemaphoreType.DMA((self.n,))]

    def _copies(self, ins, outs, sems):
        send_sems, recv_sems, local_sems = sems
        slot_of = lambda idx, slots: idx if slots == 8 else (idx // 2 if slots == 4 else idx % 2)
        me = _my_index()
        local, sends, recvs = [], [], []
        q = 0
        for t, (arr, mode, ks, slots) in enumerate(self.items):
            src_own = ins[t] if mode == "gather" else ins[t].at[me]
            local.append(pltpu.make_async_copy(src_own, outs[t].at[slot_of(me, slots)], local_sems.at[t]))
            for k in ks:
                peer, pidx = _peer(k)
                src = ins[t] if mode == "gather" else ins[t].at[pidx]
                sends.append(pltpu.make_async_remote_copy(
                    src_ref=src, dst_ref=outs[t].at[slot_of(me, slots)], send_sem=send_sems.at[q], recv_sem=recv_sems.at[q],
                    device_id=peer, device_id_type=pl.DeviceIdType.MESH))
                recvs.append(pltpu.make_async_remote_copy(
                    src_ref=src, dst_ref=outs[t].at[slot_of(pidx, slots)], send_sem=send_sems.at[q], recv_sem=recv_sems.at[q],
                    device_id=peer, device_id_type=pl.DeviceIdType.MESH))
                q += 1
        return local, sends, recvs

    def start(self, ins, outs, sems):
        local, sends, _ = self._copies(ins, outs, sems)
        for cp in sends + local:
            cp.start()

    def wait(self, ins, outs, sems):
        local, sends, recvs = self._copies(ins, outs, sems)
        for cp in recvs:
            cp.wait_recv()
        for cp in sends:
            cp.wait_send()
        for cp in local:
            cp.wait()


def _comm(items, name):
    side = _Side(items)
    n = side.n

    def body(*refs):
        ins, outs, sems = refs[:n], refs[n:2 * n], refs[2 * n:]
        side.start(ins, outs, sems)
        side.wait(ins, outs, sems)

    return pl.pallas_call(
        body, in_specs=side.in_specs, out_specs=side.out_specs, out_shape=side.out_shape, scratch_shapes=side.scratch,
        compiler_params=pltpu.CompilerParams(has_side_effects=True), name=name)(*side.arrays)


def _spread(items, name):
    n = len(items)
    ncopies = sum(len(it[1]) for it in items)

    def slot_of(idx, slots):
        return idx if slots == 8 else (idx // 2 if slots == 4 else idx % 2)

    def body(*refs):
        ins, outs, bufs = refs[:n], refs[n:2 * n], refs[2 * n:3 * n]
        load_sems, store_sems, send_sems, recv_sems = refs[3 * n:]
        me = _my_index()
        loads = [pltpu.make_async_copy(ins[t], bufs[t], load_sems.at[t]) for t in range(n)]
        for cp in loads:
            cp.start()
        stores, sends, recvs = [], [], []
        q = 0
        for t, (arr, ks, slots) in enumerate(items):
            loads[t].wait()
            own = outs[t].at[slot_of(me, slots)]
            stores.append(pltpu.make_async_copy(bufs[t], own, store_sems.at[t]))
            stores[-1].start()
            for k in ks:
                peer, pidx = _peer(k)
                sends.append(pltpu.make_async_remote_copy(
                    src_ref=bufs[t], dst_ref=own, send_sem=send_sems.at[q], recv_sem=recv_sems.at[q],
                    device_id=peer, device_id_type=pl.DeviceIdType.MESH))
                recvs.append(pltpu.make_async_remote_copy(
                    src_ref=bufs[t], dst_ref=outs[t].at[slot_of(pidx, slots)], send_sem=send_sems.at[q],
                    recv_sem=recv_sems.at[q], device_id=peer, device_id_type=pl.DeviceIdType.MESH))
                sends[-1].start()
                q += 1
        for cp in recvs:
            cp.wait_recv()
        for cp in sends:
            cp.wait_send()
        for cp in stores:
            cp.wait()

    anyspec = pl.BlockSpec(memory_space=pl.ANY)
    return pl.pallas_call(
        body, in_specs=[anyspec] * n, out_specs=[anyspec] * n,
        out_shape=[jax.ShapeDtypeStruct((slots,) + tuple(arr.shape), arr.dtype) for arr, ks, slots in items],
        scratch_shapes=[pltpu.VMEM(tuple(arr.shape), arr.dtype) for arr, ks, slots in items]
        + [pltpu.SemaphoreType.DMA((n,)), pltpu.SemaphoreType.DMA((n,)), pltpu.SemaphoreType.DMA((ncopies,)),
           pltpu.SemaphoreType.DMA((ncopies,))],
        compiler_params=pltpu.CompilerParams(has_side_effects=True, vmem_limit_bytes=VMEM_LIMIT),
        name=name)(*[it[0] for it in items])


ALL7 = (1, 2, 3, 4, 5, 6, 7)
CHIPS3 = (2, 4, 6)


def _sum8(parts, name):
    def one(a, nm):
        _, r, c = a.shape
        tr = r if r <= 512 else _pick_rows(r)

        def body(a_ref, o_ref):
            acc = a_ref[0].astype(F32)
            for q in range(1, a_ref.shape[0]):
                acc = acc + a_ref[q].astype(F32)
            o_ref[...] = acc

        return pl.pallas_call(
            body, grid=(r // tr,), in_specs=[pl.BlockSpec((a.shape[0], tr, c), lambda i: (0, i, 0))],
            out_specs=pl.BlockSpec((tr, c), lambda i: (i, 0)), out_shape=jax.ShapeDtypeStruct((r, c), F32),
            compiler_params=_cp(("parallel",)), name=nm)(a)

    return [one(a, f"{name}{i}") for i, a in enumerate(parts)]


def _pick_rows(r, cap=512):
    for t in (512, 352, 256, 176, 128, 64, 32, 16, 8):
        if r % t == 0 and t <= cap:
            return t
    return r


def _adam(w, g, m, v, name):
    shape = w.shape
    nel = int(np.prod(shape))
    c1 = 1.0 / (1.0 - ADAM_B1 ** ADAM_STEP)
    c2 = 1.0 / (1.0 - ADAM_B2 ** ADAM_STEP)

    def body(w_ref, g_ref, m_ref, v_ref, d_ref, nm_ref, nv_ref):
        gv = g_ref[...]
        nm = ADAM_B1 * m_ref[...] + (1.0 - ADAM_B1) * gv
        nv = ADAM_B2 * v_ref[...] + (1.0 - ADAM_B2) * (gv * gv)
        d_ref[...] = -ADAM_LR * ((nm * c1) / (jnp.sqrt(nv * c2) + ADAM_EPS) + ADAM_WD * w_ref[...])
        nm_ref[...] = nm
        nv_ref[...] = nv

    padded = int(np.prod(shape[:-2])) * (-(-shape[-2] // 8) * 8) * (-(-shape[-1] // 128) * 128) if len(shape) >= 2 else nel
    if len(shape) >= 2 and padded <= 1024 * 1024:
        sh = jax.ShapeDtypeStruct(shape, F32)
        return pl.pallas_call(body, out_shape=[sh] * 3, compiler_params=_cp(None), name=name)(w, g, m, v)

    if len(shape) >= 2 and shape[-1] >= 128:
        lanes = shape[-1]
    else:
        lanes = 512 if nel % 512 == 0 else 128
    r = nel // lanes
    tr = r if r * lanes <= 384 * 1024 else _pick_rows(r, 384 * 1024 // lanes)

    spec = pl.BlockSpec((tr, lanes), lambda i: (i, 0))
    sh = jax.ShapeDtypeStruct((r, lanes), F32)
    outs = pl.pallas_call(
        body, grid=(r // tr,), in_specs=[spec] * 4, out_specs=[spec] * 3, out_shape=[sh] * 3,
        compiler_params=_cp(("parallel",)), name=name)(*[a.reshape(r, lanes) for a in (w, g, m, v)])
    return [o.reshape(shape) for o in outs]


def _silu(x):
    return x * _sigmoid(x)


def _mod_fwd(c_rows, w_mod, b_cols, name):
    def body(c_ref, w_ref, b_ref, o_ref):
        s = _silu(c_ref[...])
        for l in range(2):
            o_ref[l] = jnp.dot(s, w_ref[l], preferred_element_type=F32, precision=lax.Precision.HIGHEST) + b_ref[l]

    nc = w_mod.shape[2]
    return pl.pallas_call(body, out_shape=jax.ShapeDtypeStruct((2, c_rows.shape[0], nc), F32),
                          compiler_params=_cp(None), name=name)(c_rows, w_mod, b_cols)


def _mod_bwd(c_rows, w_mod, dlat, dctx8, name):
    nrow = c_rows.shape[0]
    nb = nrow - 8

    def body(c_ref, w_ref, dl_ref, dc_ref, gw_ref, gc_ref):
        s = _silu(c_ref[...])
        ctx_row = lax.broadcasted_iota(jnp.int32, (nrow, 1), 0) == nb
        gc = jnp.zeros((1, D), F32)
        for l in range(2):
            dctx = dc_ref[0, l]
            for q in range(1, 8):
                dctx = dctx + dc_ref[q, l]
            dm = dl_ref[l] + jnp.where(ctx_row, dctx, 0.0)
            gw_ref[l] = lax.dot_general(s, dm, (((0,), (0,)), ((), ())), preferred_element_type=F32,
                                        precision=lax.Precision.HIGHEST)
            gc = gc + lax.dot_general(dctx, w_ref[l], (((1,), (1,)), ((), ())), preferred_element_type=F32,
                                      precision=lax.Precision.HIGHEST)
        gc_ref[...] = gc

    nc = w_mod.shape[2]
    return pl.pallas_call(body, out_shape=[jax.ShapeDtypeStruct((2, D, nc), F32), jax.ShapeDtypeStruct((1, D), F32)],
                          compiler_params=_cp(None), name=name)(c_rows, w_mod, dlat, dctx8)


def _bmod_cctx(dmod_all, gc4, c_ctx, name):
    def body(dm_ref, gc_ref, cc_ref, gb_ref, gcc_ref):
        for l in range(2):
            acc = jnp.sum(dm_ref[0, l], axis=0, keepdims=True)
            for q in range(1, 8):
                acc = acc + jnp.sum(dm_ref[q, l], axis=0, keepdims=True)
            gb_ref[l:l + 1, :] = acc
        g = gc_ref[0] + gc_ref[1] + gc_ref[2] + gc_ref[3]
        cv = cc_ref[...]
        sg = _sigmoid(cv)
        gcc_ref[...] = g * (sg * (1.0 + cv * (1.0 - sg)))

    return pl.pallas_call(body, out_shape=[jax.ShapeDtypeStruct((2, 6 * D), F32), jax.ShapeDtypeStruct((1, D), F32)],
                          compiler_params=_cp(None), name=name)(dmod_all, gc4, c_ctx)


def kernel(x, c, ctx, c_ctx, w_mod, b_mod, norm_mix_pre, norm_mix_post, norm_ffn_pre, norm_ffn_post, w_in, w_out, sgu_w, sgu_b, ssm_lam_re, ssm_lam_im, ssm_log_dt, ssm_b_re, ssm_b_im, ssm_c_re, ssm_c_im, ssm_d, glu_w, glu_b, pool_w, pool_scale, ffn_w_gate, ffn_w_up, ffn_w_down, loss_target, m_c_ctx, m_w_mod, m_b_mod, m_norm_mix_pre, m_norm_mix_post, m_norm_ffn_pre, m_norm_ffn_post, m_w_in, m_w_out, m_sgu_w, m_sgu_b, m_ssm_lam_re, m_ssm_lam_im, m_ssm_log_dt, m_ssm_b_re, m_ssm_b_im, m_ssm_c_re, m_ssm_c_im, m_ssm_d, m_glu_w, m_glu_b, m_pool_w, m_pool_scale, m_ffn_w_gate, m_ffn_w_up, m_ffn_w_down, v_c_ctx, v_w_mod, v_b_mod, v_norm_mix_pre, v_norm_mix_post, v_norm_ffn_pre, v_norm_ffn_post, v_w_in, v_w_out, v_sgu_w, v_sgu_b, v_ssm_lam_re, v_ssm_lam_im, v_ssm_log_dt, v_ssm_b_re, v_ssm_b_im, v_ssm_c_re, v_ssm_c_im, v_ssm_d, v_glu_w, v_glu_b, v_pool_w, v_pool_scale, v_ffn_w_gate, v_ffn_w_up, v_ffn_w_down):
    wts = dict(c_ctx=c_ctx, w_mod=w_mod, b_mod=b_mod, norm_mix_pre=norm_mix_pre, norm_mix_post=norm_mix_post,
               norm_ffn_pre=norm_ffn_pre, norm_ffn_post=norm_ffn_post, w_in=w_in, w_out=w_out, sgu_w=sgu_w, sgu_b=sgu_b,
               ssm_lam_re=ssm_lam_re, ssm_lam_im=ssm_lam_im, ssm_log_dt=ssm_log_dt, ssm_b_re=ssm_b_re, ssm_b_im=ssm_b_im,
               ssm_c_re=ssm_c_re, ssm_c_im=ssm_c_im, ssm_d=ssm_d, glu_w=glu_w, glu_b=glu_b, pool_w=pool_w,
               pool_scale=pool_scale, ffn_w_gate=ffn_w_gate, ffn_w_up=ffn_w_up, ffn_w_down=ffn_w_down)
    ms = dict(c_ctx=m_c_ctx, w_mod=m_w_mod, b_mod=m_b_mod, norm_mix_pre=m_norm_mix_pre, norm_mix_post=m_norm_mix_post,
              norm_ffn_pre=m_norm_ffn_pre, norm_ffn_post=m_norm_ffn_post, w_in=m_w_in, w_out=m_w_out, sgu_w=m_sgu_w,
              sgu_b=m_sgu_b, ssm_lam_re=m_ssm_lam_re, ssm_lam_im=m_ssm_lam_im, ssm_log_dt=m_ssm_log_dt,
              ssm_b_re=m_ssm_b_re, ssm_b_im=m_ssm_b_im, ssm_c_re=m_ssm_c_re, ssm_c_im=m_ssm_c_im, ssm_d=m_ssm_d,
              glu_w=m_glu_w, glu_b=m_glu_b, pool_w=m_pool_w, pool_scale=m_pool_scale, ffn_w_gate=m_ffn_w_gate,
              ffn_w_up=m_ffn_w_up, ffn_w_down=m_ffn_w_down)
    vs = dict(c_ctx=v_c_ctx, w_mod=v_w_mod, b_mod=v_b_mod, norm_mix_pre=v_norm_mix_pre, norm_mix_post=v_norm_mix_post,
              norm_ffn_pre=v_norm_ffn_pre, norm_ffn_post=v_norm_ffn_post, w_in=v_w_in, w_out=v_w_out, sgu_w=v_sgu_w,
              sgu_b=v_sgu_b, ssm_lam_re=v_ssm_lam_re, ssm_lam_im=v_ssm_lam_im, ssm_log_dt=v_ssm_log_dt,
              ssm_b_re=v_ssm_b_re, ssm_b_im=v_ssm_b_im, ssm_c_re=v_ssm_c_re, ssm_c_im=v_ssm_c_im, ssm_d=v_ssm_d,
              glu_w=v_glu_w, glu_b=v_glu_b, pool_w=v_pool_w, pool_scale=v_pool_scale, ffn_w_gate=v_ffn_w_gate,
              ffn_w_up=v_ffn_w_up, ffn_w_down=v_ffn_w_down)
    order = list(wts.keys())
    bl = x.shape[0]
    nseq = bl * N_DEV
    me = _my_index()
    chip = me // 2
    ncol = w_mod.shape[2]

    (c_all,) = _spread([(c, ALL7, 8)], "ag_c")
    nrow = nseq + 8
    c_rows = jnp.concatenate([c_all.reshape(nseq, D), c_ctx[None], jnp.zeros((7, D), F32)], axis=0)
    b_cols = lax.dynamic_slice_in_dim(b_mod, chip * ncol, ncol, axis=1)[:, None, :]
    mod_cols = _mod_fwd(c_rows, w_mod, b_cols, "mod_fwd")
    stash = {}

    def mods_local():
        mods = jnp.transpose(stash["mod4"], (1, 2, 0, 3)).reshape(2, nrow, 6 * D)
        return jnp.concatenate([lax.dynamic_slice_in_dim(mods, me * bl, bl, axis=1), mods[:, nseq:nseq + 1],
                                jnp.zeros((2, 8 - bl - 1, 6 * D), F32)], axis=1)

    shards = {}
    for i in range(2):
        for nme, s in zip(BIG_NAMES, [w_in[i].T, w_out[i], glu_w[i], ffn_w_gate[i].T, ffn_w_up[i].T, ffn_w_down[i]]):
            shards[(i, nme)] = s.astype(MXU_DTYPE)
    weights = [{}, {}]
    ffn_names = ("wg_t", "wu_t", "wd")
    w_plan = [{"embed": [(0, "win_t")], "win": [(0, "wout"), (0, "wglu")], "ssm": [(0, "wg_t"), (0, "wu_t")],
               "ffn": [(0, "wd"), (1, "win_t"), (1, "wout"), (1, "wglu")]},
              {"ssm": [(1, "wg_t"), (1, "wu_t")], "ffn": [(1, "wd")]}]

    def w_entry(keys, more=()):
        def fill(wls, gathered):
            for (i, nme), g in zip(keys, gathered):
                wls[i][nme] = g.reshape(-1, g.shape[-1])
            for (nme, _), g in zip(more, gathered[len(keys):]):
                stash[nme] = g
        return _Side([(shards[k2], "gather", CHIPS3, 4) for k2 in keys] + [(a, "gather", CHIPS3, 4) for _, a in more]), fill

    w_sides = [{key: w_entry(keys) for key, keys in plan.items()} for plan in w_plan]
    w_sides[0]["embed"] = w_entry(w_plan[0]["embed"], more=[("mod4", mod_cols)])

    eighths = lambda g: g.reshape(8, g.shape[0] // 8, g.shape[1])
    g_plan = {"ffn_down": [(1, "win_t"), (1, "wg_t")], "dh2": [(1, "wu_t"), (1, "wout"), (1, "wglu")],
              "ssm": [(0, k) for k in BIG_NAMES if k != "win_t"] + [(1, "wd")], "dh": [(0, "win_t")]}
    early_g = g_plan["ffn_down"] + g_plan["dh2"] + g_plan["ssm"] + g_plan["dh"]

    def g_entry(keys):
        return lambda big1, big0: _Side([(eighths((big1 if i == 1 else big0)[k]), "a2a", ALL7, 8) for i, k in keys])

    g_side_fns = {key: g_entry(keys) for key, keys in g_plan.items()}

    params = [{k: wts[k][i] for k in SMALL_NAMES} for i in range(2)]
    loss_part, grad_x, bigs, smalls, dmods, early = _core(x, ctx, loss_target, mods_local, params, weights,
                                                           w_sides, g_side_fns)
    loss = lax.psum(loss_part, ("x", "y", "c"))

    dmod_local = jnp.stack([dmods[i].reshape(8, 6 * D) for i in range(2)])
    (dmod_all,) = _spread([(dmod_local, ALL7, 8)], "ag_dmod")
    dcols = lax.dynamic_slice_in_dim(dmod_all, chip * ncol, ncol, axis=3)
    dlat = jnp.transpose(dcols[:, :, :bl], (1, 0, 2, 3)).reshape(2, nseq, ncol)
    dlat = jnp.concatenate([dlat, jnp.zeros((2, 8, ncol), F32)], axis=1)
    dctx8 = dcols[:, :, bl:bl + 1]
    g_w_mod, gc_part = _mod_bwd(c_rows, w_mod, dlat, dctx8, "mod_bwd")
    (gc4,) = _spread([(gc_part, CHIPS3, 4)], "ag_cctx")
    g_b_mod, g_c_ctx = _bmod_cctx(dmod_all, gc4, c_ctx[None], "bmod_cctx")

    small_flat = jnp.concatenate([jnp.stack([smalls[i][k] for i in range(2)]).reshape(-1) for k in SMALL_NAMES])
    npad = (-small_flat.shape[0]) % (8 * 1024)
    small_flat = jnp.concatenate([small_flat, jnp.zeros((npad,), F32)])
    late = _comm([(small_flat.reshape(8, -1, 1024), "a2a", ALL7, 8)], "a2a_grads")
    sums = _sum8(list(early) + list(late), "gsum")
    fin = _spread([(s, (1,), 2) for s in sums[:-1]] + [(sums[-1], ALL7, 8)], "ag_grads")
    big_g = [{}, {}]
    for (i, k), g in zip(early_g, fin[:-1]):
        big_g[i][k] = g.reshape(-1, g.shape[-1])
    small_red = fin[-1].reshape(-1)

    grads = {}
    off = 0
    for k in SMALL_NAMES:
        shp = wts[k].shape
        nel = int(np.prod(shp))
        grads[k] = small_red[off:off + nel].reshape(shp)
        off += nel
    grads["c_ctx"] = g_c_ctx[0]
    grads["w_mod"] = g_w_mod
    grads["b_mod"] = g_b_mod
    grads["w_in"] = jnp.stack([big_g[i]["win_t"].T for i in range(2)])
    grads["w_out"] = jnp.stack([big_g[i]["wout"] for i in range(2)])
    grads["glu_w"] = jnp.stack([big_g[i]["wglu"] for i in range(2)])
    grads["ffn_w_gate"] = jnp.stack([big_g[i]["wg_t"].T for i in range(2)])
    grads["ffn_w_up"] = jnp.stack([big_g[i]["wu_t"].T for i in range(2)])
    grads["ffn_w_down"] = jnp.stack([big_g[i]["wd"] for i in range(2)])

    deltas, new_m, new_v = {}, {}, {}
    for k in order:
        deltas[k], new_m[k], new_v[k] = _adam(wts[k], grads[k], ms[k], vs[k], "adam_" + k)
    return (loss, grad_x, *[grads[k] for k in order], *[deltas[k] for k in order],
            *[new_m[k] for k in order], *[new_v[k] for k in order])
```

```python
import functools
import math

import numpy as np
import jax
import jax.numpy as jnp
from jax import lax
from jax.experimental import pallas as pl
from jax.experimental.pallas import tpu as pltpu

F32 = jnp.float32
BF16 = jnp.bfloat16
MXU_DTYPE = jnp.bfloat16
MCAT_A, MCAT_C = 2, 3
WOUT_PERM, WOUT_INV = (1, 2, 0, 3), (2, 0, 1, 3)

D = 1024
EPS = 1e-6
TB = 256
CTX = 256
CHUNK = 128
GRID_W = 64
A_W, B_W, C_W = 256, 512, 256
D_IN = 1280
D_FF = 2816
SSM_G, SSM_P, SSM_H = 32, 64, 16
ST = 64
POOL_WINDOWS = (2, 4, 8, 16)
N_DEV = 8
VMEM_LIMIT = 52 * 1024 * 1024
GELU_C = math.sqrt(2.0 / math.pi)

ADAM_LR, ADAM_B1, ADAM_B2, ADAM_EPS, ADAM_WD, ADAM_STEP = 0.001, 0.9, 0.999, 1e-08, 0.01, 10


def _cp(sem=None, vmem=VMEM_LIMIT, **kw):
    return pltpu.CompilerParams(dimension_semantics=sem, vmem_limit_bytes=vmem, **kw)


def _pick(n, cap):
    if n <= cap:
        return n
    best = None
    for t in range(128, cap + 1, 128):
        if n % t == 0:
            best = t
    assert best is not None, (n, cap)
    return best


def _gelu(x):
    return 0.5 * x * (1.0 + jnp.tanh(GELU_C * (x + 0.044715 * x * x * x)))


def _gelu_grad(x):
    t = jnp.tanh(GELU_C * (x + 0.044715 * x * x * x))
    return 0.5 * (1.0 + t) + 0.5 * x * (1.0 - t * t) * GELU_C * (1.0 + 3.0 * 0.044715 * x * x)


def _sigmoid(x):
    return 1.0 / (1.0 + jnp.exp(-x))


def _dot(a, b, dims):
    return lax.dot_general(a, b, (dims, ((), ())), preferred_element_type=F32)


def _nn(a, b):
    return _dot(a, b, ((1,), (0,)))


def _nt(a, b):
    return _dot(a, b, ((1,), (1,)))


def _tn(a, b):
    return _dot(a, b, ((0,), (0,)))


def _mm(pairs, nt, out_dtype, name, tm=512, side=None):
    m = pairs[0][0].shape[0]
    n = pairs[0][1].shape[0] if nt else pairs[0][1].shape[1]
    tn = _pick(n, 1408)
    tm = min(tm, m)
    npairs = len(pairs)
    ni, nj = m // tm, n // tn

    def body(*refs):
        own, srefs = _side_split(refs, 2 * npairs, 1, 0, side)
        o_ref = own[-1]
        i, j = pl.program_id(0), pl.program_id(1)
        _side_start(side, srefs, jnp.logical_and(i == 0, j == 0))
        acc = None
        for t in range(npairs):
            a = own[2 * t][...].astype(MXU_DTYPE)
            b = own[2 * t + 1][...].astype(MXU_DTYPE)
            r = _nt(a, b) if nt else _nn(a, b)
            acc = r if acc is None else acc + r
        o_ref[...] = acc.astype(o_ref.dtype)
        _side_wait(side, srefs, jnp.logical_and(i == ni - 1, j == nj - 1))

    sd = side if side is not None else _Side([])
    in_specs, flat = [], []
    for a, b in pairs:
        k = a.shape[1]
        in_specs.append(pl.BlockSpec((tm, k), lambda i, j: (i, 0)))
        in_specs.append(pl.BlockSpec((tn, k), lambda i, j: (j, 0)) if nt else pl.BlockSpec((k, tn), lambda i, j: (0, j)))
        flat += [a, b]
    outs = pl.pallas_call(
        body, grid=(ni, nj), in_specs=in_specs + sd.in_specs,
        out_specs=[pl.BlockSpec((tm, tn), lambda i, j: (i, j))] + sd.out_specs,
        out_shape=[jax.ShapeDtypeStruct((m, n), out_dtype)] + sd.out_shape,
        scratch_shapes=sd.scratch if side is not None else [],
        compiler_params=_cp(("arbitrary", "arbitrary") if side is not None else ("parallel", "parallel")),
        name=name)(*flat, *sd.arrays)
    return outs[0] if side is None else (outs[0], list(outs[1:]))


def _mm_tn(a, b, out_dtype, name):
    m, k1 = a.shape
    n = b.shape[1]
    t1 = _pick(k1, 1408)
    tn = _pick(n, 1024)
    tm = max(t for t in (512, 1024, 1536) if m % t == 0)
    nsteps = m // tm

    def body(a_ref, b_ref, o_ref, acc_ref):
        t = pl.program_id(2)

        @pl.when(t == 0)
        def _():
            acc_ref[...] = jnp.zeros_like(acc_ref)

        acc_ref[...] += _tn(a_ref[...].astype(MXU_DTYPE), b_ref[...].astype(MXU_DTYPE))

        @pl.when(t == nsteps - 1)
        def _():
            o_ref[...] = acc_ref[...].astype(o_ref.dtype)

    return pl.pallas_call(
        body, grid=(k1 // t1, n // tn, nsteps),
        in_specs=[pl.BlockSpec((tm, t1), lambda i, j, t: (t, i)), pl.BlockSpec((tm, tn), lambda i, j, t: (t, j))],
        out_specs=pl.BlockSpec((t1, tn), lambda i, j, t: (i, j)),
        out_shape=jax.ShapeDtypeStruct((k1, n), out_dtype),
        scratch_shapes=[pltpu.VMEM((t1, tn), F32)],
        compiler_params=_cp(("parallel", "parallel", "arbitrary")), name=name)(a, b)


class _Layout:
    def __init__(self, bl, lat):
        self.bl, self.lat = bl, lat
        self.nlb = lat // TB
        self.nr = 1 + self.nlb
        self.nctx = bl
        self.nb = self.nr * bl
        self.nt = self.nb * TB
        self.ctx_row = bl

    def mod_tiles(self, mods):
        rows = np.array([[self.ctx_row if r == 0 else b for b in range(self.bl)] for r in range(self.nr)], np.int32)
        t = mods[rows].reshape(self.nr, self.bl, 6, D)
        return jnp.transpose(t, (0, 2, 1, 3)).reshape(self.nr * 6, self.bl, 1, D)


ST_FWD, ST_BWD = 4, 2


def _tok_spec(lay, st):
    nc = lay.bl // st
    return pl.BlockSpec((st * TB, D), lambda c, r: (r * nc + c, 0))


def _vec_spec():
    return pl.BlockSpec((1, D), lambda c, r: (0, 0))


def _mod_spec(st, k):
    return pl.BlockSpec((1, st, 1, D), lambda c, r: (r * 6 + k, c, 0, 0))


def _x_spec(lay, st):
    return pl.BlockSpec((st, 1, TB, D), lambda c, r: (c, jnp.maximum(r - 1, 0), 0, 0))


def _rows3(ref_or_val, st):
    return ref_or_val.reshape(st, TB, D)


def _acc_rows(acc_ref, val3, st, ctx_row):
    c, r = pl.program_id(0), pl.program_id(1)
    s = jnp.sum(val3, axis=1, keepdims=True)

    @pl.when(r == 0)
    def _():
        acc_ref[ctx_row:ctx_row + 1] += jnp.sum(s, axis=0, keepdims=True)

    @pl.when(r > 0)
    def _():
        acc_ref[pl.ds(c * st, st)] += s


def _first_step():
    return jnp.logical_and(pl.program_id(0) == 0, pl.program_id(1) == 0)


def _embed(lay, x, ctx, pe, side=None):
    st = ST_FWD
    bl, nlb = lay.bl, lay.nlb
    nc = bl // st

    def body(*refs):
        (x_ref, c_ref, pe_ref, o_ref), srefs = _side_split(refs, 3, 1, 0, side)
        c, r = pl.program_id(0), pl.program_id(1)
        _side_start(side, srefs, jnp.logical_and(c == 0, r == 0))

        @pl.when(r == 0)
        def _():
            o_ref[...] = c_ref[...].reshape(st * TB, D)

        @pl.when(r > 0)
        def _():
            o_ref[...] = (x_ref[...].reshape(st, TB, D) + pe_ref[...]).reshape(st * TB, D)

        _side_wait(side, srefs, jnp.logical_and(c == nc - 1, r == lay.nr - 1))

    sd = side if side is not None else _Side([])
    outs = pl.pallas_call(
        body, grid=(nc, lay.nr),
        in_specs=[_x_spec(lay, st), pl.BlockSpec((st, CTX, D), lambda c, r: (c, 0, 0)),
                  pl.BlockSpec((1, TB, D), lambda c, r: (jnp.maximum(r - 1, 0), 0, 0))] + sd.in_specs,
        out_specs=[_tok_spec(lay, st)] + sd.out_specs,
        out_shape=[jax.ShapeDtypeStruct((lay.nt, D), F32)] + sd.out_shape,
        scratch_shapes=sd.scratch if side is not None else [],
        compiler_params=_cp(("arbitrary", "arbitrary") if side is not None else ("parallel", "parallel")),
        name="embed")(x.reshape(bl, nlb, TB, D), ctx, pe.reshape(nlb, TB, D), *sd.arrays)
    return outs[0], list(outs[1:])


def _normmod_fwd(lay, x, gain, modt, ksh, ksc, name):
    st = ST_FWD

    def body(x_ref, g_ref, sh_ref, sc_ref, o_ref):
        xv = _rows3(x_ref[...], st)
        r = lax.rsqrt(jnp.mean(xv * xv, axis=-1, keepdims=True) + EPS)
        o_ref[...] = ((xv * r * g_ref[...]) * (1.0 + sc_ref[0]) + sh_ref[0]).reshape(st * TB, D).astype(o_ref.dtype)

    return pl.pallas_call(
        body, grid=(lay.bl // st, lay.nr),
        in_specs=[_tok_spec(lay, st), _vec_spec(), _mod_spec(st, ksh), _mod_spec(st, ksc)],
        out_specs=_tok_spec(lay, st), out_shape=jax.ShapeDtypeStruct((lay.nt, D), MXU_DTYPE),
        compiler_params=_cp(("parallel", "parallel")), name=name)(x, gain, modt, modt)


def _acc_out():
    return pl.BlockSpec((8, 1, D), lambda c, r: (0, 0, 0)), jax.ShapeDtypeStruct((8, 1, D), F32)


def _normmod_bwd(lay, x, dh, dx_in, gain, modt, ksc, name, latent_only=False):
    st = ST_BWD
    acc_spec, acc_shape = _acc_out()
    if latent_only:
        dx_spec, dx_shape = _x_spec(lay, st), jax.ShapeDtypeStruct((lay.bl, lay.nlb, TB, D), F32)
    else:
        dx_spec, dx_shape = _tok_spec(lay, st), jax.ShapeDtypeStruct((lay.nt, D), F32)

    def body(x_ref, dh_ref, dxi_ref, g_ref, sc_ref, dx_ref, dsh_ref, dsc_ref, dg_ref):
        xv = _rows3(x_ref[...], st)
        dhv = _rows3(dh_ref[...].astype(F32), st)
        g = g_ref[...]
        sc1 = 1.0 + sc_ref[0]
        r = lax.rsqrt(jnp.mean(xv * xv, axis=-1, keepdims=True) + EPS)
        xh = xv * r
        dxh = dhv * (g * sc1)
        dx = _rows3(dxi_ref[...], st) + r * (dxh - xh * jnp.mean(dxh * xh, axis=-1, keepdims=True))
        dx_ref[...] = dx.reshape(dx_ref.shape)

        @pl.when(_first_step())
        def _():
            dsh_ref[...] = jnp.zeros_like(dsh_ref)
            dsc_ref[...] = jnp.zeros_like(dsc_ref)
            dg_ref[...] = jnp.zeros_like(dg_ref)

        _acc_rows(dsh_ref, dhv, st, lay.ctx_row)
        _acc_rows(dsc_ref, dhv * (xh * g), st, lay.ctx_row)
        dg_ref[...] += jnp.sum((dhv * sc1 * xh).reshape(st * TB, D), axis=0, keepdims=True)

    return pl.pallas_call(
        body, grid=(lay.bl // st, lay.nr),
        in_specs=[_tok_spec(lay, st), _tok_spec(lay, st), _tok_spec(lay, st), _vec_spec(), _mod_spec(st, ksc)],
        out_specs=[dx_spec, acc_spec, acc_spec, _vec_spec()],
        out_shape=[dx_shape, acc_shape, acc_shape, jax.ShapeDtypeStruct((1, D), F32)],
        compiler_params=_cp(("arbitrary", "arbitrary")), name=name)(x, dh, dx_in, gain, modt)


def _resnorm_fwd(lay, x, m, gain, modt, kgate, name):
    st = ST_FWD

    def body(x_ref, m_ref, g_ref, gate_ref, o_ref):
        mv = _rows3(m_ref[...].astype(F32), st)
        r = lax.rsqrt(jnp.mean(mv * mv, axis=-1, keepdims=True) + EPS)
        o_ref[...] = x_ref[...] + (gate_ref[0] * (mv * r * g_ref[...])).reshape(st * TB, D)

    return pl.pallas_call(
        body, grid=(lay.bl // st, lay.nr),
        in_specs=[_tok_spec(lay, st), _tok_spec(lay, st), _vec_spec(), _mod_spec(st, kgate)],
        out_specs=_tok_spec(lay, st), out_shape=jax.ShapeDtypeStruct((lay.nt, D), F32),
        compiler_params=_cp(("parallel", "parallel")), name=name)(x, m, gain, modt)


def _resnorm_bwd(lay, dxn, m, gain, modt, kgate, name):
    st = ST_FWD
    acc_spec, acc_shape = _acc_out()

    def body(d_ref, m_ref, g_ref, gate_ref, dm_ref, dgate_ref, dg_ref):
        dv = _rows3(d_ref[...], st)
        mv = _rows3(m_ref[...].astype(F32), st)
        g = g_ref[...]
        r = lax.rsqrt(jnp.mean(mv * mv, axis=-1, keepdims=True) + EPS)
        xh = mv * r
        dy = dv * gate_ref[0]
        dxh = dy * g
        dm = r * (dxh - xh * jnp.mean(dxh * xh, axis=-1, keepdims=True))
        dm_ref[...] = dm.reshape(st * TB, D).astype(dm_ref.dtype)

        @pl.when(_first_step())
        def _():
            dgate_ref[...] = jnp.zeros_like(dgate_ref)
            dg_ref[...] = jnp.zeros_like(dg_ref)

        _acc_rows(dgate_ref, dv * (xh * g), st, lay.ctx_row)
        dg_ref[...] += jnp.sum((dy * xh).reshape(st * TB, D), axis=0, keepdims=True)

    return pl.pallas_call(
        body, grid=(lay.bl // st, lay.nr),
        in_specs=[_tok_spec(lay, st), _tok_spec(lay, st), _vec_spec(), _mod_spec(st, kgate)],
        out_specs=[_tok_spec(lay, st), acc_spec, _vec_spec()],
        out_shape=[jax.ShapeDtypeStruct((lay.nt, D), MXU_DTYPE), acc_shape, jax.ShapeDtypeStruct((1, D), F32)],
        compiler_params=_cp(("arbitrary", "arbitrary")), name=name)(dxn, m, gain, modt)


def _rms(v):
    return lax.rsqrt(jnp.mean(v * v, axis=-1, keepdims=True) + EPS)


def _resnorm_normmod_fwd(lay, x, m, gpost, gpre, modt, kgate, ksh, ksc, name):
    st = ST_FWD

    def body(x_ref, m_ref, gp_ref, gq_ref, gate_ref, sh_ref, sc_ref, x1_ref, h_ref):
        mv = _rows3(m_ref[...].astype(F32), st)
        x1 = _rows3(x_ref[...], st) + gate_ref[0] * (mv * _rms(mv) * gp_ref[...])
        x1_ref[...] = x1.reshape(st * TB, D)
        h = (x1 * _rms(x1) * gq_ref[...]) * (1.0 + sc_ref[0]) + sh_ref[0]
        h_ref[...] = h.reshape(st * TB, D).astype(h_ref.dtype)

    tok = _tok_spec(lay, st)
    return pl.pallas_call(
        body, grid=(lay.bl // st, lay.nr),
        in_specs=[tok, tok, _vec_spec(), _vec_spec(), _mod_spec(st, kgate), _mod_spec(st, ksh), _mod_spec(st, ksc)],
        out_specs=[tok, tok],
        out_shape=[jax.ShapeDtypeStruct((lay.nt, D), F32), jax.ShapeDtypeStruct((lay.nt, D), MXU_DTYPE)],
        compiler_params=_cp(("parallel", "parallel")), name=name)(x, m, gpost, gpre, modt, modt, modt)


def _normmod_resnorm_bwd(lay, x1, dh, dx_in, gpre, m, gpost, modt, ksc, kgate, name):
    st = ST_BWD
    acc_spec, acc_shape = _acc_out()

    def body(x_ref, dh_ref, dxi_ref, gq_ref, sc_ref, m_ref, gp_ref, gate_ref,
             dx_ref, dm_ref, dsh_ref, dsc_ref, dgq_ref, dgate_ref, dgp_ref):
        xv = _rows3(x_ref[...], st)
        dhv = _rows3(dh_ref[...].astype(F32), st)
        gq = gq_ref[...]
        sc1 = 1.0 + sc_ref[0]
        r = _rms(xv)
        xh = xv * r
        dxh = dhv * (gq * sc1)
        dx1 = _rows3(dxi_ref[...], st) + r * (dxh - xh * jnp.mean(dxh * xh, axis=-1, keepdims=True))
        dx_ref[...] = dx1.reshape(st * TB, D)
        mv = _rows3(m_ref[...].astype(F32), st)
        gp = gp_ref[...]
        rm = _rms(mv)
        mh = mv * rm
        dy = dx1 * gate_ref[0]
        dmh = dy * gp
        dm = rm * (dmh - mh * jnp.mean(dmh * mh, axis=-1, keepdims=True))
        dm_ref[...] = dm.reshape(st * TB, D).astype(dm_ref.dtype)

        @pl.when(_first_step())
        def _():
            for ref in (dsh_ref, dsc_ref, dgq_ref, dgate_ref, dgp_ref):
                ref[...] = jnp.zeros_like(ref)

        _acc_rows(dsh_ref, dhv, st, lay.ctx_row)
        _acc_rows(dsc_ref, dhv * (xh * gq), st, lay.ctx_row)
        dgq_ref[...] += jnp.sum((dhv * sc1 * xh).reshape(st * TB, D), axis=0, keepdims=True)
        _acc_rows(dgate_ref, dx1 * (mh * gp), st, lay.ctx_row)
        dgp_ref[...] += jnp.sum((dy * mh).reshape(st * TB, D), axis=0, keepdims=True)

    tok = _tok_spec(lay, st)
    vsh = jax.ShapeDtypeStruct((1, D), F32)
    return pl.pallas_call(
        body, grid=(lay.bl // st, lay.nr),
        in_specs=[tok, tok, tok, _vec_spec(), _mod_spec(st, ksc), tok, _vec_spec(), _mod_spec(st, kgate)],
        out_specs=[tok, tok, acc_spec, acc_spec, _vec_spec(), acc_spec, _vec_spec()],
        out_shape=[jax.ShapeDtypeStruct((lay.nt, D), F32), jax.ShapeDtypeStruct((lay.nt, D), MXU_DTYPE),
                   acc_shape, acc_shape, vsh, acc_shape, vsh],
        compiler_params=_cp(("arbitrary", "arbitrary")), name=name)(x1, dh, dx_in, gpre, modt, m, gpost, modt)


def _resnorm_loss(lay, x, f, gain, modt, kgate, tgt):
    st = ST_FWD

    def body(x_ref, f_ref, g_ref, gate_ref, t_ref, dx_ref, l_ref):
        r = pl.program_id(1)

        @pl.when(_first_step())
        def _():
            l_ref[...] = jnp.zeros_like(l_ref)

        @pl.when(r == 0)
        def _():
            dx_ref[...] = jnp.zeros_like(dx_ref)

        @pl.when(r > 0)
        def _():
            fv = _rows3(f_ref[...].astype(F32), st)
            y = _rows3(x_ref[...], st) + gate_ref[0] * (fv * _rms(fv) * g_ref[...])
            e = y - t_ref[...].reshape(st, TB, D)
            dx_ref[...] = (e * (1.0 / D)).reshape(st * TB, D)
            l_ref[...] += jnp.sum(e * e) * (0.5 / D)

    tok = _tok_spec(lay, st)
    return pl.pallas_call(
        body, grid=(lay.bl // st, lay.nr),
        in_specs=[tok, tok, _vec_spec(), _mod_spec(st, kgate), _x_spec(lay, st)],
        out_specs=[tok, pl.BlockSpec((8, 128), lambda c, r: (0, 0))],
        out_shape=[jax.ShapeDtypeStruct((lay.nt, D), F32), jax.ShapeDtypeStruct((8, 128), F32)],
        compiler_params=_cp(("arbitrary", "arbitrary")), name="loss")(
            x, f, gain, modt, tgt.reshape(lay.bl, lay.nlb, TB, D))


FF_TN = D_FF // 2
FF_CHUNKS = ((0, 512), (512, 512), (1024, 384))


def _ffn_up(h, wgt, wut, name, side=None):
    m = h.shape[0]
    tm, tn = min(512, m), FF_TN
    ni, nj = m // tm, D_FF // tn

    def body(*refs):
        (h_ref, wg_ref, wu_ref, g_ref, u_ref, a_ref), srefs = _side_split(refs, 3, 3, 0, side)
        j, i = pl.program_id(0), pl.program_id(1)
        _side_start(side, srefs, jnp.logical_and(i == 0, j == 0))
        hv = h_ref[...]
        for c0, cw in FF_CHUNKS:
            g = _nt(hv, wg_ref[c0:c0 + cw, :])
            u = _nt(hv, wu_ref[c0:c0 + cw, :])
            g_ref[:, c0:c0 + cw] = g.astype(g_ref.dtype)
            u_ref[:, c0:c0 + cw] = u.astype(u_ref.dtype)
            a_ref[:, c0:c0 + cw] = (g * _sigmoid(g) * u).astype(a_ref.dtype)
        _side_wait(side, srefs, jnp.logical_and(i == ni - 1, j == nj - 1))

    sd = side if side is not None else _Side([])
    osp = pl.BlockSpec((tm, tn), lambda j, i: (i, j))
    osh = jax.ShapeDtypeStruct((m, D_FF), MXU_DTYPE)
    outs = pl.pallas_call(
        body, grid=(nj, ni),
        in_specs=[pl.BlockSpec((tm, D), lambda j, i: (i, 0)), pl.BlockSpec((tn, D), lambda j, i: (j, 0)),
                  pl.BlockSpec((tn, D), lambda j, i: (j, 0))] + sd.in_specs,
        out_specs=[osp, osp, osp] + sd.out_specs, out_shape=[osh, osh, osh] + sd.out_shape,
        scratch_shapes=sd.scratch if side is not None else [],
        compiler_params=_cp(("arbitrary", "arbitrary") if side is not None else ("parallel", "parallel")),
        name=name)(h, wgt, wut, *sd.arrays)
    return outs[0], outs[1], outs[2], list(outs[3:])


def _ffn_down_bwd(df, wd, g, u, name, side=None):
    m = df.shape[0]
    tm, tn = min(512, m), FF_TN
    ni, nj = m // tm, D_FF // tn

    def body(*refs):
        (df_ref, wd_ref, g_ref, u_ref, dg_ref, du_ref), srefs = _side_split(refs, 4, 2, 0, side)
        j, i = pl.program_id(0), pl.program_id(1)
        _side_start(side, srefs, jnp.logical_and(i == 0, j == 0))
        dfv = df_ref[...]
        for c0, cw in FF_CHUNKS:
            da = _nt(dfv, wd_ref[c0:c0 + cw, :])
            gv = g_ref[:, c0:c0 + cw].astype(F32)
            uv = u_ref[:, c0:c0 + cw].astype(F32)
            s = _sigmoid(gv)
            dg_ref[:, c0:c0 + cw] = (da * uv * (s * (1.0 + gv * (1.0 - s)))).astype(dg_ref.dtype)
            du_ref[:, c0:c0 + cw] = (da * gv * s).astype(du_ref.dtype)
        _side_wait(side, srefs, jnp.logical_and(i == ni - 1, j == nj - 1))

    sd = side if side is not None else _Side([])
    osp = pl.BlockSpec((tm, tn), lambda j, i: (i, j))
    osh = jax.ShapeDtypeStruct((m, D_FF), MXU_DTYPE)
    outs = pl.pallas_call(
        body, grid=(nj, ni),
        in_specs=[pl.BlockSpec((tm, D), lambda j, i: (i, 0)), pl.BlockSpec((tn, D), lambda j, i: (j, 0)), osp, osp]
        + sd.in_specs,
        out_specs=[osp, osp] + sd.out_specs, out_shape=[osh, osh] + sd.out_shape,
        scratch_shapes=sd.scratch if side is not None else [],
        compiler_params=_cp(("arbitrary", "arbitrary") if side is not None else ("parallel", "parallel")),
        name=name)(df, wd, g, u, *sd.arrays)
    return outs[0], outs[1], list(outs[2:])


def _head_masks(shape):
    lane = lax.broadcasted_iota(jnp.int32, shape, 1)
    return [jnp.logical_and(lane >= 64 * h, lane < 64 * h + 64) for h in range(4)]


def _head_mean(x, masks):
    out = jnp.zeros_like(x)
    for mk in masks:
        s = jnp.sum(jnp.where(mk, x, 0.0), axis=-1, keepdims=True) * (1.0 / 64.0)
        out = jnp.where(mk, s, out)
    return out


def _gate_common(z, masks):
    zg = _gelu(z)
    u = zg[:, :A_W]
    v = zg[:, A_W:]
    mu = _head_mean(v, masks)
    vc = v - mu
    rstd = lax.rsqrt(_head_mean(vc * vc, masks) + EPS)
    return u, vc * rstd, rstd


def _gate_s(vn, ws_ref, bias, masks):
    parts = []
    for c in range(TB // CHUNK):
        vc = vn[c * CHUNK:(c + 1) * CHUNK]
        s = bias
        for h in range(4):
            s = s + _nn(ws_ref[h], jnp.where(masks[h][:CHUNK], vc, 0.0).astype(MXU_DTYPE))
        parts.append(s)
    return jnp.concatenate(parts, axis=0)


MT = 4


def _blocks():
    return [pl.ds(s * TB, TB) for s in range(MT)]


def _gate_fwd(lay, z, ws, bias, name):
    def body(z_ref, ws_ref, b_ref, o_ref):
        masks = _head_masks((TB, A_W))
        for sl in _blocks():
            u, vn, _ = _gate_common(z_ref[sl, :].astype(F32), masks)
            o_ref[sl, :] = (u * _gate_s(vn, ws_ref, b_ref[...], masks)).astype(o_ref.dtype)

    return pl.pallas_call(
        body, grid=(lay.nb // MT,),
        in_specs=[pl.BlockSpec((MT * TB, 2 * A_W), lambda j: (j, 0)), pl.BlockSpec((4, CHUNK, CHUNK), lambda j: (0, 0, 0)),
                  pl.BlockSpec((CHUNK, A_W), lambda j: (0, 0))],
        out_specs=pl.BlockSpec((MT * TB, A_W), lambda j: (j, 0)),
        out_shape=jax.ShapeDtypeStruct((lay.nt, A_W), MXU_DTYPE),
        compiler_params=_cp(("parallel",)), name=name)(z, ws, bias)


def _gate_bwd(lay, z, da, ws, wst, bias, name):
    def body(z_ref, da_ref, ws_ref, wst_ref, b_ref, dz_ref, dws_ref, db_ref):
        j = pl.program_id(0)

        @pl.when(j == 0)
        def _():
            dws_ref[...] = jnp.zeros_like(dws_ref)
            db_ref[...] = jnp.zeros_like(db_ref)

        masks = _head_masks((TB, A_W))
        for blk in _blocks():
            zv = z_ref[blk, :].astype(F32)
            u, vn, rstd = _gate_common(zv, masks)
            s = _gate_s(vn, ws_ref, b_ref[...], masks)
            dav = da_ref[blk, :].astype(F32)
            du = dav * s
            ds = dav * u
            dvn_parts = []
            for c in range(TB // CHUNK):
                sl = slice(c * CHUNK, (c + 1) * CHUNK)
                ds_c = ds[sl]
                vn_c = vn[sl].astype(MXU_DTYPE)
                db_ref[...] += ds_c
                ds_b = ds_c.astype(MXU_DTYPE)
                dvn_c = jnp.zeros((CHUNK, A_W), F32)
                for h in range(4):
                    mk = masks[h][:CHUNK]
                    dws_ref[h] += _nt(jnp.where(mk, ds_c, 0.0).astype(MXU_DTYPE), vn_c)
                    dvn_c = dvn_c + jnp.where(mk, _nn(wst_ref[h], ds_b), 0.0)
                dvn_parts.append(dvn_c)
            dvn = jnp.concatenate(dvn_parts, axis=0)
            dv = rstd * (dvn - _head_mean(dvn, masks) - vn * _head_mean(dvn * vn, masks))
            gg = _gelu_grad(zv)
            dz_ref[blk, :A_W] = (du * gg[:, :A_W]).astype(dz_ref.dtype)
            dz_ref[blk, A_W:] = (dv * gg[:, A_W:]).astype(dz_ref.dtype)

    return pl.pallas_call(
        body, grid=(lay.nb // MT,),
        in_specs=[pl.BlockSpec((MT * TB, 2 * A_W), lambda j: (j, 0)), pl.BlockSpec((MT * TB, A_W), lambda j: (j, MCAT_A)),
                  pl.BlockSpec((4, CHUNK, CHUNK), lambda j: (0, 0, 0)), pl.BlockSpec((4, CHUNK, CHUNK), lambda j: (0, 0, 0)),
                  pl.BlockSpec((CHUNK, A_W), lambda j: (0, 0))],
        out_specs=[pl.BlockSpec((MT * TB, 2 * A_W), lambda j: (j, 0)), pl.BlockSpec((4, CHUNK, CHUNK), lambda j: (0, 0, 0)),
                   pl.BlockSpec((CHUNK, A_W), lambda j: (0, 0))],
        out_shape=[jax.ShapeDtypeStruct((lay.nt, 2 * A_W), MXU_DTYPE), jax.ShapeDtypeStruct((4, CHUNK, CHUNK), F32),
                   jax.ShapeDtypeStruct((CHUNK, A_W), F32)],
        compiler_params=_cp(("arbitrary",)), name=name)(z, da, ws, wst, bias)


def _band_constants():
    bands = np.zeros((2, 4, TB, TB), np.float32)
    inv = np.zeros((2, 4, TB, 1), np.float32)
    for kind, n in ((0, GRID_W), (1, TB)):
        for i, w in enumerate(POOL_WINDOWS):
            for t in range(TB):
                base, tt = (t // n) * n, t % n
                lo = min(max(tt - w // 2, 0), n)
                hi = min(max(tt - w // 2 + w, 0), n)
                bands[kind, i, t, base + lo:base + hi] = 1.0
                inv[kind, i, t, 0] = 1.0 / (hi - lo)
    return bands, inv


def _split3(x):
    a = x.astype(MXU_DTYPE)
    r1 = x - a.astype(F32)
    b = r1.astype(MXU_DTYPE)
    c = (r1 - b.astype(F32)).astype(MXU_DTYPE)
    return a, b, c


def _window_apply(band_ref, inv_ref, x, masks, transpose, mxu_exact=False):
    out = jnp.zeros_like(x)
    for i in range(4):
        xi = x * inv_ref[0, i] if transpose else x
        acc = None
        for part in ((xi.astype(MXU_DTYPE),) if mxu_exact else _split3(xi)):
            r = _tn(band_ref[0, i], part) if transpose else _nn(band_ref[0, i], part)
            acc = r if acc is None else acc + r
        if not transpose:
            acc = acc * inv_ref[0, i]
        out = jnp.where(masks[i], acc, out)
    return out


def _pool_specs(lay):
    kind = lambda j: jnp.where(j < lay.nctx // MT, 1, 0)
    return [pl.BlockSpec((1, 4, TB, TB), lambda j: (kind(j), 0, 0, 0)), pl.BlockSpec((1, 4, TB, 1), lambda j: (kind(j), 0, 0, 0))]


def _pool_fwd(lay, z, bands, inv, pw, scale, name):
    def body(p_ref, band_ref, inv_ref, pw_ref, sc_ref, o_ref):
        masks = _head_masks((TB, C_W))
        for blk in _blocks():
            p = p_ref[blk, :].astype(F32)
            diff = _window_apply(band_ref, inv_ref, p, masks, False, mxu_exact=True) - p
            o_ref[blk, :] = (_nn(diff.astype(MXU_DTYPE), pw_ref[...]) * sc_ref[...]).astype(o_ref.dtype)

    return pl.pallas_call(
        body, grid=(lay.nb // MT,),
        in_specs=[pl.BlockSpec((MT * TB, C_W), lambda j: (j, 4))] + _pool_specs(lay)
        + [pl.BlockSpec((C_W, C_W), lambda j: (0, 0)), pl.BlockSpec((1, C_W), lambda j: (0, 0))],
        out_specs=pl.BlockSpec((MT * TB, C_W), lambda j: (j, 0)),
        out_shape=jax.ShapeDtypeStruct((lay.nt, C_W), MXU_DTYPE),
        compiler_params=_cp(("parallel",)), name=name)(z, bands, inv, pw, scale)


def _pool_bwd(lay, z, dc, bands, inv, pw, scale, name):
    def body(p_ref, dc_ref, band_ref, inv_ref, pw_ref, sc_ref, dp_ref, dpw_ref, dsc_ref):
        j = pl.program_id(0)

        @pl.when(j == 0)
        def _():
            dpw_ref[...] = jnp.zeros_like(dpw_ref)
            dsc_ref[...] = jnp.zeros_like(dsc_ref)

        masks = _head_masks((TB, C_W))
        for blk in _blocks():
            p = p_ref[blk, :].astype(F32)
            dcv = dc_ref[blk, :].astype(F32)
            diff = _window_apply(band_ref, inv_ref, p, masks, False, mxu_exact=True) - p
            diff_b = diff.astype(MXU_DTYPE)
            pre = _nn(diff_b, pw_ref[...])
            dsc_ref[...] += jnp.sum(dcv * pre, axis=0, keepdims=True)
            dpre = dcv * sc_ref[...]
            dpre_b = dpre.astype(MXU_DTYPE)
            dpw_ref[...] += _tn(diff_b, dpre_b)
            ddiff = _nt(dpre_b, pw_ref[...])
            dp_ref[blk, :] = (_window_apply(band_ref, inv_ref, ddiff, masks, True) - ddiff).astype(dp_ref.dtype)

    return pl.pallas_call(
        body, grid=(lay.nb // MT,),
        in_specs=[pl.BlockSpec((MT * TB, C_W), lambda j: (j, 4)), pl.BlockSpec((MT * TB, C_W), lambda j: (j, MCAT_C))]
        + _pool_specs(lay)
        + [pl.BlockSpec((C_W, C_W), lambda j: (0, 0)), pl.BlockSpec((1, C_W), lambda j: (0, 0))],
        out_specs=[pl.BlockSpec((MT * TB, C_W), lambda j: (j, 0)), pl.BlockSpec((C_W, C_W), lambda j: (0, 0)),
                   pl.BlockSpec((1, C_W), lambda j: (0, 0))],
        out_shape=[jax.ShapeDtypeStruct((lay.nt, C_W), MXU_DTYPE), jax.ShapeDtypeStruct((C_W, C_W), F32),
                   jax.ShapeDtypeStruct((1, C_W), F32)],
        compiler_params=_cp(("arbitrary",)), name=name)(z, dc, bands, inv, pw, scale)


def _disc_math(lr, li, ldt, br, bi):
    dt = jnp.exp(ldt)
    e = jnp.exp(lr * dt)
    ar = e * jnp.cos(li * dt)
    ai = e * jnp.sin(li * dt)
    nr, ni = ar - 1.0, ai
    den = lr * lr + li * li
    qr = (nr * lr + ni * li) / den
    qi = (ni * lr - nr * li) / den
    return ar, ai, qr * br - qi * bi, qr * bi + qi * br


def _disc_fwd(lrx, lix, ldtx, brt, bit, name):
    def body(lr_ref, li_ref, ldt_ref, br_ref, bi_ref, ar_ref, ai_ref, obr_ref, obi_ref):
        ar, ai, obr, obi = _disc_math(lr_ref[...], li_ref[...], ldt_ref[...], br_ref[...], bi_ref[...])
        ar_ref[...] = ar
        ai_ref[...] = ai
        obr_ref[...] = obr
        obi_ref[...] = obi

    sh = jax.ShapeDtypeStruct(lrx.shape, F32)
    return pl.pallas_call(body, out_shape=[sh, sh, sh, sh], name=name)(lrx, lix, ldtx, brt, bit)


def _disc_bwd(lrx, lix, ldtx, brt, bit, dar, dai, dbr, dbi, name):
    nrow = lrx.shape[0] // SSM_H

    def body(lr_ref, li_ref, ldt_ref, br_ref, bi_ref, dar_ref, dai_ref, dbr_ref, dbi_ref,
             glr_ref, gli_ref, gdt_ref, gbr_ref, gbi_ref):
        _, vjp = jax.vjp(_disc_math, lr_ref[...], li_ref[...], ldt_ref[...], br_ref[...], bi_ref[...])
        glr, gli, gdt, gbr, gbi = vjp((dar_ref[...], dai_ref[...], dbr_ref[...], dbi_ref[...]))
        glr_ref[...] = jnp.sum(glr.reshape(nrow, SSM_H, SSM_P), axis=1)
        gli_ref[...] = jnp.sum(gli.reshape(nrow, SSM_H, SSM_P), axis=1)
        gdt_ref[...] = jnp.sum(jnp.sum(gdt.reshape(nrow, SSM_H, SSM_P), axis=1), axis=-1, keepdims=True)
        gbr_ref[...] = gbr
        gbi_ref[...] = gbi

    small = jax.ShapeDtypeStruct((nrow, SSM_P), F32)
    big = jax.ShapeDtypeStruct(lrx.shape, F32)
    return pl.pallas_call(body, out_shape=[small, small, jax.ShapeDtypeStruct((nrow, 1), F32), big, big],
                          name=name)(lrx, lix, ldtx, brt, bit, dar, dai, dbr, dbi)


HS = 1024
GQ, QC, QS = 8, 128, 512
LC = QS
SCAN_UNROLL = ST


def _scan_steps(step, carry):
    if SCAN_UNROLL >= ST:
        for s in range(ST):
            carry = step(s, carry)
        return carry

    def body(i, c):
        for j in range(SCAN_UNROLL):
            c = step(i * SCAN_UNROLL + j, c)
        return c

    return lax.fori_loop(0, ST // SCAN_UNROLL, body, carry)


def _tile_row(s):
    return s * 8 if isinstance(s, int) else pl.multiple_of(s * 8, 8)


def _dir_cat(x, d0, qq):
    xq = x[:, QC * qq:QC * qq + QC]
    zero = jnp.zeros_like(xq)
    return jnp.concatenate([jnp.where(d0, xq, zero), jnp.where(d0, zero, xq)], axis=1)


def _dir_pick(x, d0):
    return jnp.where(d0, x[:, :QC], x[:, QC:])


def _d0_rows(n):
    row = lax.broadcasted_iota(jnp.int32, (n, 1), 0)
    return jnp.bitwise_and(row, 4) == 0


def _scan_perm(bl):
    n = 2 * bl * ST
    p = np.zeros((n, n), np.float32)
    for s in range(ST):
        for d in range(2):
            for b in range(bl):
                t = s if d == 0 else ST - 1 - s
                p[s * 2 * bl + d * bl + b, d * bl * ST + b * ST + t] = 1.0
    return p


def _scan_maps(lay):
    spc = TB // ST
    nlc = lay.nlb * spc

    def fwd(k):
        return k // spc, k % spc

    def rev(k):
        cpos = nlc - 1 - jnp.maximum(k - spc, 0)
        return jnp.where(k < spc, 0, 1 + cpos // spc), jnp.where(k < spc, spc - 1 - k, cpos % spc)

    return fwd, rev


def _pack_rows(f_ref, r_ref, p_ref, rc):
    st = jnp.concatenate([f_ref[0].reshape(rc // 2, 256), r_ref[0].reshape(rc // 2, 256)], axis=0).astype(MXU_DTYPE)
    return _nn(p_ref[...], st).astype(MXU_DTYPE)


def _side_split(refs, n_in, n_out, n_scr, side):
    ns = side.n if side is not None else 0
    ins, sin = refs[:n_in], refs[n_in:n_in + ns]
    o0 = n_in + ns
    outs, sout = refs[o0:o0 + n_out], refs[o0 + n_out:o0 + n_out + ns]
    s0 = o0 + n_out + ns
    return ins + outs + refs[s0:s0 + n_scr], (sin, sout, refs[s0 + n_scr:])


def _side_start(side, srefs, first):
    if side is not None:
        @pl.when(first)
        def _():
            side.start(*srefs)


def _side_wait(side, srefs, last):
    if side is not None:
        @pl.when(last)
        def _():
            side.wait(*srefs)


def _ssm_fwd(lay, z, perm, bh, ch, ar8, ai8, name, side=None):
    bl = lay.bl
    rc = ST * 2 * bl
    nch = lay.nr * (TB // ST)
    fwd, rev = _scan_maps(lay)
    z4 = z.reshape(lay.nr, bl, TB, z.shape[1])

    def body(*refs):
        own, srefs = _side_split(refs, 7, 4, 2, side)
        uf_ref, ur_ref, p_ref, bh_ref, ch_ref, ar_ref, ai_ref, yf_ref, yr_ref, hst_ref, hsv_ref, hs, hc = own
        f, k = pl.program_id(0), pl.program_id(1)
        _side_start(side, srefs, jnp.logical_and(f == 0, k == 0))

        @pl.when(k == 0)
        def _():
            hc[...] = jnp.zeros_like(hc)

        hst_ref[0] = hc[...]
        d0 = _d0_rows(rc)
        uv = _pack_rows(uf_ref, ur_ref, p_ref, rc)
        for q in range(2):
            cr, ci = 2 * QS * q, 2 * QS * q + QS
            hs[:, cr:cr + 2 * QS] = _nn(_dir_cat(uv, d0, q), bh_ref[q])
            ar = ar_ref[:, QS * q:QS * q + QS]
            ai = ai_ref[:, QS * q:QS * q + QS]

            def step(s, carry, cr=cr, ci=ci, ar=ar, ai=ai):
                hr, hi = carry
                base = _tile_row(s)
                nr = ar * hr - ai * hi + hs[pl.ds(base, 8), cr:cr + LC]
                ni = ar * hi + ai * hr + hs[pl.ds(base, 8), ci:ci + LC]
                hs[pl.ds(base, 8), cr:cr + LC] = nr
                hs[pl.ds(base, 8), ci:ci + LC] = ni
                return nr, ni

            hr, hi = _scan_steps(step, (hc[:, cr:cr + LC], hc[:, ci:ci + LC]))
            hc[:, cr:cr + LC] = hr
            hc[:, ci:ci + LC] = hi
        hsv_ref[0] = hs[...].astype(hsv_ref.dtype)
        yi = jnp.concatenate(
            [_dir_pick(_nn(hsv_ref[0, :, 2 * QS * q:2 * QS * (q + 1)], ch_ref[q]), d0) for q in range(2)], axis=1)
        yd = _tn(p_ref[...], yi.astype(MXU_DTYPE))
        yf_ref[0] = yd[:rc // 2].reshape(bl, ST, 256).astype(yf_ref.dtype)
        yr_ref[0] = yd[rc // 2:].reshape(bl, ST, 256).astype(yr_ref.dtype)
        _side_wait(side, srefs, jnp.logical_and(f == 1, k == nch - 1))

    sd = side if side is not None else _Side([])
    blk = (1, bl, ST, 256)
    ysh = jax.ShapeDtypeStruct((lay.nr, bl, TB, B_W), MXU_DTYPE)
    outs = pl.pallas_call(
        body, grid=(2, nch),
        in_specs=[pl.BlockSpec(blk, lambda f, k: (fwd(k)[0], 0, fwd(k)[1], 2 + f)),
                  pl.BlockSpec(blk, lambda f, k: (rev(k)[0], 0, rev(k)[1], 2 + f)),
                  pl.BlockSpec((rc, rc), lambda f, k: (0, 0)),
                  pl.BlockSpec((2, 2 * QC, 2 * QS), lambda f, k: (f, 0, 0)),
                  pl.BlockSpec((2, 2 * QS, 2 * QC), lambda f, k: (f, 0, 0)),
                  pl.BlockSpec((8, HS), lambda f, k: (0, f)), pl.BlockSpec((8, HS), lambda f, k: (0, f))] + sd.in_specs,
        out_specs=[pl.BlockSpec(blk, lambda f, k: (fwd(k)[0], 0, fwd(k)[1], f)),
                   pl.BlockSpec(blk, lambda f, k: (rev(k)[0], 0, rev(k)[1], f)),
                   pl.BlockSpec((1, 8, 2 * HS), lambda f, k: (k, 0, f)),
                   pl.BlockSpec((1, rc, 2 * HS), lambda f, k: (k, 0, f))] + sd.out_specs,
        out_shape=[ysh, ysh, jax.ShapeDtypeStruct((nch, 8, 4 * HS), F32),
                   jax.ShapeDtypeStruct((nch, rc, 4 * HS), MXU_DTYPE)] + sd.out_shape,
        scratch_shapes=[pltpu.VMEM((rc, 2 * HS), F32), pltpu.VMEM((8, 2 * HS), F32)] + (sd.scratch if side is not None else []),
        compiler_params=_cp(("arbitrary", "arbitrary")), name=name)(z4, z4, perm, bh, ch, ar8, ai8, *sd.arrays)
    yf, yr, hst, hsv = outs[:4]
    return yf.reshape(lay.nt, B_W), yr.reshape(lay.nt, B_W), (hst, hsv), list(outs[4:])


def _ssm_bwd(lay, z, dy, perm, hst, bh, ch, ar8, ai8, name, side=None):
    bl = lay.bl
    rc = ST * 2 * bl
    nch = lay.nr * (TB // ST)
    fwd, rev = _scan_maps(lay)
    z4 = z.reshape(lay.nr, bl, TB, z.shape[1])
    dy4 = dy.reshape(lay.nr, bl, TB, B_W)

    hst, hsv = hst

    def body(*refs):
        own, srefs = _side_split(refs, 11, 6, 5, side)
        (uf_ref, ur_ref, dyf_ref, dyr_ref, p_ref, hst_ref, hsv_ref, bh_ref, ch_ref, ar_ref, ai_ref,
         duf_ref, dur_ref, dbh_ref, dch_ref, dar_ref, dai_ref, hs, es, ec, accr, acci) = own
        f, k = pl.program_id(0), pl.program_id(1)
        _side_start(side, srefs, jnp.logical_and(f == 0, k == 0))

        @pl.when(k == 0)
        def _():
            ec[...] = jnp.zeros_like(ec)
            accr[...] = jnp.zeros_like(accr)
            acci[...] = jnp.zeros_like(acci)
            dbh_ref[...] = jnp.zeros_like(dbh_ref)
            dch_ref[...] = jnp.zeros_like(dch_ref)

        d0 = _d0_rows(rc)
        uv = _pack_rows(uf_ref, ur_ref, p_ref, rc)
        dyv = _pack_rows(dyf_ref, dyr_ref, p_ref, rc)

        hs[0:8, :] = hst_ref[0]
        hs[8:, :] = hsv_ref[0].astype(F32)
        ucat, dycat = [], []
        for q in range(2):
            cr = 2 * QS * q
            ucat.append(_dir_cat(uv, d0, q))
            dycat.append(_dir_cat(dyv, d0, q))
            dch_ref[q] += _tn(hsv_ref[0, :, cr:cr + 2 * QS], dycat[q])
            es[:, cr:cr + 2 * QS] = _nt(dycat[q], ch_ref[q])

        dui = []
        for q in range(2):
            cr, ci = 2 * QS * q, 2 * QS * q + QS
            ar = ar_ref[:, QS * q:QS * q + QS]
            ai = ai_ref[:, QS * q:QS * q + QS]

            def bstep(i, carry, cr=cr, ci=ci, ar=ar, ai=ai):
                er, ei, sr, si = carry
                base = _tile_row(ST - 1 - i)
                ner = es[pl.ds(base, 8), cr:cr + LC] + ar * er + ai * ei
                nei = es[pl.ds(base, 8), ci:ci + LC] - ai * er + ar * ei
                es[pl.ds(base, 8), cr:cr + LC] = ner
                es[pl.ds(base, 8), ci:ci + LC] = nei
                hpr = hs[pl.ds(base, 8), cr:cr + LC]
                hpi = hs[pl.ds(base, 8), ci:ci + LC]
                return ner, nei, sr + ner * hpr + nei * hpi, si - ner * hpi + nei * hpr

            lo = QS * q
            er, ei, sr, si = _scan_steps(
                bstep, (ec[:, cr:cr + LC], ec[:, ci:ci + LC], accr[:, lo:lo + LC], acci[:, lo:lo + LC]))
            ec[:, cr:cr + LC] = er
            ec[:, ci:ci + LC] = ei
            accr[:, lo:lo + LC] = sr
            acci[:, lo:lo + LC] = si
            eb = es[:, cr:cr + 2 * QS].astype(MXU_DTYPE)
            dui.append(_dir_pick(_nt(eb, bh_ref[q]), d0))
            dbh_ref[q] += _tn(ucat[q], eb)

        dud = _tn(p_ref[...], jnp.concatenate(dui, axis=1).astype(MXU_DTYPE))
        duf_ref[0] = dud[:rc // 2].reshape(bl, ST, 256).astype(duf_ref.dtype)
        dur_ref[0] = dud[rc // 2:].reshape(bl, ST, 256).astype(dur_ref.dtype)

        @pl.when(k == nch - 1)
        def _():
            for d in range(2):
                dar_ref[d:d + 1, :] = jnp.sum(accr[4 * d:4 * d + 4, :], axis=0, keepdims=True)
                dai_ref[d:d + 1, :] = jnp.sum(acci[4 * d:4 * d + 4, :], axis=0, keepdims=True)

        _side_wait(side, srefs, jnp.logical_and(f == 1, k == nch - 1))

    sd = side if side is not None else _Side([])
    last = lambda k: nch - 1 - k
    blk = (1, bl, ST, 256)
    fspec = lambda c0: pl.BlockSpec(blk, lambda f, k: (fwd(last(k))[0], 0, fwd(last(k))[1], c0 + f))
    rspec = lambda c0: pl.BlockSpec(blk, lambda f, k: (rev(last(k))[0], 0, rev(last(k))[1], c0 + f))
    dush = jax.ShapeDtypeStruct((lay.nr, bl, TB, B_W), MXU_DTYPE)
    outs = pl.pallas_call(
        body, grid=(2, nch),
        in_specs=[fspec(2), rspec(2), fspec(0), rspec(0),
                  pl.BlockSpec((rc, rc), lambda f, k: (0, 0)),
                  pl.BlockSpec((1, 8, 2 * HS), lambda f, k: (last(k), 0, f)),
                  pl.BlockSpec((1, rc, 2 * HS), lambda f, k: (last(k), 0, f)),
                  pl.BlockSpec((2, 2 * QC, 2 * QS), lambda f, k: (f, 0, 0)),
                  pl.BlockSpec((2, 2 * QS, 2 * QC), lambda f, k: (f, 0, 0)),
                  pl.BlockSpec((8, HS), lambda f, k: (0, f)), pl.BlockSpec((8, HS), lambda f, k: (0, f))] + sd.in_specs,
        out_specs=[fspec(0), rspec(0),
                   pl.BlockSpec((2, 2 * QC, 2 * QS), lambda f, k: (f, 0, 0)),
                   pl.BlockSpec((2, 2 * QS, 2 * QC), lambda f, k: (f, 0, 0)),
                   pl.BlockSpec((2, HS), lambda f, k: (0, f)), pl.BlockSpec((2, HS), lambda f, k: (0, f))] + sd.out_specs,
        out_shape=[dush, dush, jax.ShapeDtypeStruct((4, 2 * QC, 2 * QS), F32),
                   jax.ShapeDtypeStruct((4, 2 * QS, 2 * QC), F32), jax.ShapeDtypeStruct((2, 2 * HS), F32),
                   jax.ShapeDtypeStruct((2, 2 * HS), F32)] + sd.out_shape,
        scratch_shapes=[pltpu.VMEM((rc + 8, 2 * HS), F32), pltpu.VMEM((rc, 2 * HS), F32), pltpu.VMEM((8, 2 * HS), F32),
                        pltpu.VMEM((8, HS), F32), pltpu.VMEM((8, HS), F32)] + (sd.scratch if side is not None else []),
        compiler_params=_cp(("arbitrary", "arbitrary")), name=name)(
            z4, z4, dy4, dy4, perm, hst, hsv, bh, ch, ar8, ai8, *sd.arrays)
    duf, dur, dbh, dch, dar, dai = outs[:6]
    return duf.reshape(lay.nt, B_W), dur.reshape(lay.nt, B_W), dbh, dch, dar, dai, list(outs[6:])


def _glu_fwd(lay, z, yf, yr, dvec, wglu, bglu, name):
    def body(u_ref, yf_ref, yr_ref, d_ref, w_ref, b_ref, o_ref, y_ref):
        y = yf_ref[...].astype(F32) + yr_ref[...].astype(F32) + d_ref[...] * u_ref[...].astype(F32)
        y_ref[...] = y
        g = _gelu(y)
        pre = _nn(g.astype(MXU_DTYPE), w_ref[...]) + b_ref[...]
        o_ref[...] = (g * _sigmoid(pre)).astype(o_ref.dtype)

    tok = pl.BlockSpec((MT * TB, B_W), lambda j: (j, 0))
    vec = pl.BlockSpec((1, B_W), lambda j: (0, 0))
    return pl.pallas_call(
        body, grid=(lay.nb // MT,),
        in_specs=[pl.BlockSpec((MT * TB, B_W), lambda j: (j, 1)), tok, tok, vec,
                  pl.BlockSpec((B_W, B_W), lambda j: (0, 0)), vec],
        out_specs=[tok, tok],
        out_shape=[jax.ShapeDtypeStruct((lay.nt, B_W), MXU_DTYPE), jax.ShapeDtypeStruct((lay.nt, B_W), F32)],
        compiler_params=_cp(("parallel",)), name=name)(z, yf, yr, dvec, wglu, bglu)


def _glu_bwd(lay, z, y, ds, dvec, wglu, bglu, name):
    def body(u_ref, y_ref, ds_ref, d_ref, w_ref, b_ref, dy_ref, dud_ref, dw_ref, db_ref, dd_ref):
        j = pl.program_id(0)

        @pl.when(j == 0)
        def _():
            dw_ref[...] = jnp.zeros_like(dw_ref)
            db_ref[...] = jnp.zeros_like(db_ref)
            dd_ref[...] = jnp.zeros_like(dd_ref)

        yv = y_ref[...]
        g = _gelu(yv)
        gb = g.astype(MXU_DTYPE)
        sg = _sigmoid(_nn(gb, w_ref[...]) + b_ref[...])
        dsv = ds_ref[...].astype(F32)
        dpre = dsv * g * sg * (1.0 - sg)
        dpre_b = dpre.astype(MXU_DTYPE)
        dg = dsv * sg + _nt(dpre_b, w_ref[...])
        dw_ref[...] += _tn(gb, dpre_b)
        db_ref[...] += jnp.sum(dpre, axis=0, keepdims=True)
        dy = dg * _gelu_grad(yv)
        dy_ref[...] = dy.astype(dy_ref.dtype)
        dd_ref[...] += jnp.sum(dy * u_ref[...].astype(F32), axis=0, keepdims=True)
        dud_ref[...] = (dy * d_ref[...]).astype(dud_ref.dtype)

    tok = pl.BlockSpec((MT * TB, B_W), lambda j: (j, 0))
    vec = pl.BlockSpec((1, B_W), lambda j: (0, 0))
    mat = pl.BlockSpec((B_W, B_W), lambda j: (0, 0))
    vsh = jax.ShapeDtypeStruct((1, B_W), F32)
    return pl.pallas_call(
        body, grid=(lay.nb // MT,),
        in_specs=[pl.BlockSpec((MT * TB, B_W), lambda j: (j, 1)), tok, tok, vec, mat, vec],
        out_specs=[tok, tok, mat, vec, vec],
        out_shape=[jax.ShapeDtypeStruct((lay.nt, B_W), MXU_DTYPE), jax.ShapeDtypeStruct((lay.nt, B_W), F32),
                   jax.ShapeDtypeStruct((B_W, B_W), F32), vsh, vsh],
        compiler_params=_cp(("arbitrary",)), name=name)(z, y, ds, dvec, wglu, bglu)


def _dz_assemble(lay, dz_a, duf, dur, dud, dz_p, name):
    def body(a_ref, f_ref, r_ref, d_ref, p_ref, o_ref):
        o_ref[:, :2 * A_W] = a_ref[...].astype(o_ref.dtype)
        o_ref[:, 2 * A_W:2 * A_W + B_W] = (f_ref[...].astype(F32) + r_ref[...].astype(F32) + d_ref[...]).astype(o_ref.dtype)
        o_ref[:, 2 * A_W + B_W:] = p_ref[...].astype(o_ref.dtype)

    spec = lambda w: pl.BlockSpec((MT * TB, w), lambda j: (j, 0))
    return pl.pallas_call(
        body, grid=(lay.nb // MT,), in_specs=[spec(2 * A_W), spec(B_W), spec(B_W), spec(B_W), spec(C_W)],
        out_specs=spec(D_IN), out_shape=jax.ShapeDtypeStruct((lay.nt, D_IN), MXU_DTYPE),
        compiler_params=_cp(("parallel",)), name=name)(dz_a, duf, dur, dud, dz_p)


def _expand_rows(a):
    return jnp.broadcast_to(a[:, :, None, :], (2, SSM_G, SSM_H, SSM_P)).reshape(-1, SSM_P)


def _ssm_params(lam_re, lam_im, log_dt, b_re, b_im, c_re, c_im, name):
    lrx, lix = _expand_rows(lam_re), _expand_rows(lam_im)
    ldtx = _expand_rows(jnp.broadcast_to(log_dt[:, :, None], (2, SSM_G, SSM_P)))
    brt = jnp.transpose(b_re, (0, 1, 3, 2)).reshape(-1, SSM_P)
    bit = jnp.transpose(b_im, (0, 1, 3, 2)).reshape(-1, SSM_P)
    arx, aix, bbr, bbi = _disc_fwd(lrx, lix, ldtx, brt, bit, name)
    ar = arx.reshape(2, SSM_G, SSM_H, SSM_P)[:, :, 0].reshape(2, SSM_G * SSM_P)
    ai = aix.reshape(2, SSM_G, SSM_H, SSM_P)[:, :, 0].reshape(2, SSM_G * SSM_P)
    eye = jnp.eye(GQ, dtype=F32)

    def bmat(bt):
        t = bt.reshape(2, 4, GQ, SSM_H, SSM_P)
        return jnp.einsum('dqghp,gk->qdghkp', t, eye).reshape(4, 2 * QC, QS)

    bh = jnp.concatenate([bmat(bbr), bmat(bbi)], axis=-1).astype(MXU_DTYPE)

    def cmat(c):
        t = c.reshape(2, 4, GQ, SSM_H, SSM_P)
        return jnp.einsum('dqghp,gk->qgpdkh', t, eye).reshape(4, QS, 2 * QC)

    ch = jnp.concatenate([cmat(c_re), -cmat(c_im)], axis=1).astype(MXU_DTYPE)

    def rows8(a):
        return jnp.repeat(a, 4, axis=0)

    return dict(lrx=lrx, lix=lix, ldtx=ldtx, brt=brt, bit=bit, bh=bh, ch=ch, ar8=rows8(ar), ai8=rows8(ai))


def _ssm_param_grads(sp, dbh, dch, dar, dai, name):
    def bdiag(m):
        t = m.reshape(4, 2, GQ, SSM_H, GQ, SSM_P)
        return jnp.einsum('qdghgp->dqghp', t).reshape(-1, SSM_P)

    dbr, dbi = bdiag(dbh[..., :QS]), bdiag(dbh[..., QS:])

    def cdiag(m):
        t = m.reshape(4, GQ, SSM_P, 2, GQ, SSM_H)
        return jnp.einsum('qgpdgh->dqghp', t).reshape(2, SSM_G, SSM_H, SSM_P)

    dc_re, dc_im = cdiag(dch[:, :QS]), -cdiag(dch[:, QS:])

    def hrow(a):
        t = a.reshape(2, SSM_G, 1, SSM_P)
        return jnp.concatenate([t, jnp.zeros((2, SSM_G, SSM_H - 1, SSM_P), F32)], axis=2).reshape(-1, SSM_P)

    glr, gli, gdt, gbr, gbi = _disc_bwd(sp["lrx"], sp["lix"], sp["ldtx"], sp["brt"], sp["bit"],
                                        hrow(dar), hrow(dai), dbr, dbi, name)
    to_b = lambda g: jnp.transpose(g.reshape(2, SSM_G, SSM_H, SSM_P), (0, 1, 3, 2))
    return dict(ssm_lam_re=glr.reshape(2, SSM_G, SSM_P), ssm_lam_im=gli.reshape(2, SSM_G, SSM_P),
                ssm_log_dt=gdt.reshape(2, SSM_G), ssm_b_re=to_b(gbr), ssm_b_im=to_b(gbi),
                ssm_c_re=dc_re, ssm_c_im=dc_im)


def _layer_consts(p):
    c = {}
    c["ws"] = p["sgu_w"].astype(MXU_DTYPE)
    c["wst"] = jnp.transpose(p["sgu_w"], (0, 2, 1)).astype(MXU_DTYPE)
    c["gbias"] = jnp.repeat(p["sgu_b"].T, 64, axis=1)
    pw = jnp.zeros((C_W, C_W), F32)
    for i in range(4):
        pw = pw.at[64 * i:64 * i + 64, 64 * i:64 * i + 64].set(p["pool_w"][i])
    c["pw"] = pw.astype(MXU_DTYPE)
    c["pscale"] = p["pool_scale"].reshape(1, C_W)
    c["dvec"] = p["ssm_d"].reshape(1, B_W)
    c["bglu"] = p["glu_b"].reshape(1, B_W)
    return c


def _layer_fwd(lay, i, x, modarr, p, w, cst, sp, bands, inv, perm, sides=None, last=False):
    n = f"l{i}_"
    sides = sides or {}
    win_side, win_fill = sides.get("win", (None, None))
    ssm_side, ssm_fill = sides.get("ssm", (None, None))
    ffn_side, ffn_fill = sides.get("ffn", (None, None))
    res = {"x0": x}
    h = _normmod_fwd(lay, x, p["norm_mix_pre"].reshape(1, D), modarr, 0, 1, n + "nm1")
    z = _mm([(h, w["win_t"])], True, MXU_DTYPE, n + "win", side=win_side)
    if win_side is not None:
        z, extra = z
        win_fill(extra)
    a = _gate_fwd(lay, z, cst["ws"], cst["gbias"], n + "gate")
    yf, yr, hst, extra = _ssm_fwd(lay, z, perm, sp["bh"], sp["ch"], sp["ar8"], sp["ai8"], n + "ssm", ssm_side)
    if ssm_side is not None:
        ssm_fill(extra)
    s, y = _glu_fwd(lay, z, yf, yr, cst["dvec"], w["wglu"], cst["bglu"], n + "glu")
    c = _pool_fwd(lay, z, bands, inv, cst["pw"], cst["pscale"], n + "pool")
    mcat = jnp.concatenate([s, a, c], axis=1)
    res["wout_p"] = _perm_wout(w["wout"])
    m = _mm([(mcat, res["wout_p"])], False, MXU_DTYPE, n + "wout")
    x1, h2 = _resnorm_normmod_fwd(lay, x, m, p["norm_mix_post"].reshape(1, D), p["norm_ffn_pre"].reshape(1, D),
                                  modarr, 2, 3, 4, n + "rn1nm2")
    g, u, act, extra = _ffn_up(h2, w["wg_t"], w["wu_t"], n + "ffn_up", ffn_side)
    if ffn_side is not None:
        ffn_fill(extra)
    f = _mm([(act, w["wd"])], False, MXU_DTYPE, n + "ffn_down")
    res.update(h=h, z=z, hst=hst, y=y, mcat=mcat, m=m, x1=x1, h2=h2, g=g, u=u, act=act, f=f)
    if last:
        return None, res
    x2 = _resnorm_fwd(lay, x1, f, p["norm_ffn_post"].reshape(1, D), modarr, 5, n + "rn2")
    return x2, res


def _layer_bwd(lay, i, dx2, modarr, p, w, cst, sp, bands, inv, perm, res, side_fns=None):
    n = f"l{i}b_"
    big, small = {}, {}
    side_fns = side_fns or {}
    side_of = lambda key: side_fns[key](big) if key in side_fns else None
    df, dg2, gpost2 = _resnorm_bwd(lay, dx2, res["f"], p["norm_ffn_post"].reshape(1, D), modarr, 5, n + "rn2")
    big["wd"] = _mm_tn(res["act"], df, MXU_DTYPE, n + "dwd")
    dg, du, early = _ffn_down_bwd(df, w["wd"], res["g"], res["u"], n + "ffn_down", side_of("ffn_down"))
    dh2_side = side_of("dh2")
    dh2 = _mm([(dg, w["wg_t"]), (du, w["wu_t"])], False, MXU_DTYPE, n + "dh2", side=dh2_side)
    if dh2_side is not None:
        dh2, ex = dh2
        early = early + ex
    big["wg_t"] = _mm_tn(dg, res["h2"], MXU_DTYPE, n + "dwg")
    big["wu_t"] = _mm_tn(du, res["h2"], MXU_DTYPE, n + "dwu")
    dx1, dm, dsh2, dsc2, gpre2, dg1, gpost1 = _normmod_resnorm_bwd(
        lay, res["x1"], dh2, dx2, p["norm_ffn_pre"].reshape(1, D), res["m"], p["norm_mix_post"].reshape(1, D), modarr,
        4, 2, n + "nm2rn1")
    big["wout"] = _unperm_wout(_mm_tn(res["mcat"], dm, MXU_DTYPE, n + "dwout"))
    dmcat = _mm([(dm, res["wout_p"])], True, MXU_DTYPE, n + "dmcat")
    z = res["z"]
    dz_a, dws, dgb = _gate_bwd(lay, z, dmcat, cst["ws"], cst["wst"], cst["gbias"], n + "gate")
    dy, dud, dwglu, dbglu, ddvec = _glu_bwd(lay, z, res["y"], dmcat, cst["dvec"], w["wglu"], cst["bglu"], n + "glu")
    big["wglu"] = dwglu.astype(MXU_DTYPE)
    duf, dur, dbh, dch, dar, dai, ex = _ssm_bwd(lay, z, dy, perm, res["hst"], sp["bh"], sp["ch"], sp["ar8"],
                                                sp["ai8"], n + "ssm", side_of("ssm"))
    early = early + ex
    dz_p, dpw, dpsc = _pool_bwd(lay, z, dmcat, bands, inv, cst["pw"], cst["pscale"], n + "pool")
    dz = _dz_assemble(lay, dz_a, duf, dur, dud, dz_p, n + "dz")
    big["win_t"] = _mm_tn(dz, res["h"], MXU_DTYPE, n + "dwin")
    dh_side = side_of("dh")
    dh = _mm([(dz, w["win_t"])], False, MXU_DTYPE, n + "dh", side=dh_side)
    if dh_side is not None:
        dh, ex = dh
        early = early + ex
    dx, dsh1, dsc1, gpre1 = _normmod_bwd(lay, res["x0"], dh, dx1, p["norm_mix_pre"].reshape(1, D), modarr, 1, n + "nm1",
                                         latent_only=(i == 0))

    small.update(norm_mix_pre=gpre1[0], norm_mix_post=gpost1[0], norm_ffn_pre=gpre2[0], norm_ffn_post=gpost2[0])
    small["sgu_w"] = dws
    small["sgu_b"] = jnp.sum(dgb.reshape(CHUNK, 4, 64), axis=-1).T
    small.update(_ssm_param_grads(sp, dbh, dch, dar, dai, n + "disc"))
    small["ssm_d"] = ddvec.reshape(SSM_G, SSM_H)
    small["glu_b"] = dbglu[0]
    small["pool_w"] = jnp.stack([dpw[64 * k:64 * k + 64, 64 * k:64 * k + 64] for k in range(4)])
    small["pool_scale"] = dpsc[0]
    dmod = jnp.concatenate([dsh1, dsc1, dg1, dsh2, dsc2, dg2], axis=1)[:lay.bl + 1]
    dmod = jnp.concatenate([dmod, jnp.zeros((8 - lay.bl - 1, 6, D), F32)], axis=0)
    return dx, big, small, dmod, early


def _perm_wout(w):
    return w.reshape(4, D // 4, D)[np.array(WOUT_PERM)].reshape(D, D)


def _unperm_wout(g):
    return g.reshape(4, D // 4, D)[np.array(WOUT_INV)].reshape(D, D)


SMALL_NAMES = ["norm_mix_pre", "norm_mix_post", "norm_ffn_pre", "norm_ffn_post", "sgu_w", "sgu_b", "ssm_lam_re",
               "ssm_lam_im", "ssm_log_dt", "ssm_b_re", "ssm_b_im", "ssm_c_re", "ssm_c_im", "ssm_d", "glu_b", "pool_w",
               "pool_scale"]
BIG_NAMES = ["win_t", "wout", "wglu", "wg_t", "wu_t", "wd"]


def _sincos_2d(rows, cols, dim):
    quarter = dim // 4
    omega = 1.0 / (10000.0 ** (jnp.arange(quarter, dtype=F32) / quarter))
    r = jnp.arange(rows, dtype=F32)[:, None] * omega
    cc = jnp.arange(cols, dtype=F32)[:, None] * omega
    er = jnp.concatenate([jnp.sin(r), jnp.cos(r)], axis=-1)
    ec = jnp.concatenate([jnp.sin(cc), jnp.cos(cc)], axis=-1)
    pe = jnp.concatenate([jnp.broadcast_to(er[:, None, :], (rows, cols, dim // 2)),
                          jnp.broadcast_to(ec[None, :, :], (rows, cols, dim // 2))], axis=-1)
    return pe.reshape(rows * cols, dim)


def _core(x, ctx, target, mods_local, params, weights, w_sides=None, g_side_fns=None):
    bl, lat, _ = x.shape
    assert bl == 4 and lat % TB == 0, "the scan fills 8 sublanes with 2 directions x 4 sequences"
    lay = _Layout(bl, lat)
    pe = _sincos_2d(lat // GRID_W, GRID_W, D)
    bands_np, inv_np = _band_constants()
    bands, inv = jnp.asarray(bands_np, MXU_DTYPE), jnp.asarray(inv_np, F32)
    perm = jnp.asarray(_scan_perm(bl), MXU_DTYPE)
    csts, sps, ress, wls = [], [], [], []
    for i in range(2):
        csts.append(_layer_consts(params[i]))
        p = params[i]
        sps.append(_ssm_params(p["ssm_lam_re"], p["ssm_lam_im"], p["ssm_log_dt"], p["ssm_b_re"], p["ssm_b_im"],
                               p["ssm_c_re"], p["ssm_c_im"], f"l{i}_disc"))
        wls.append(dict(weights[i]))

    embed_side, embed_fill = (w_sides[0].get("embed") if w_sides else None) or (None, None)
    xt, extra = _embed(lay, x, ctx, pe, embed_side)
    if embed_side is not None:
        embed_fill(wls, extra)
    if callable(mods_local):
        mods_local = mods_local()
    modarrs = [lay.mod_tiles(mods_local[i]) for i in range(2)]
    for i in range(2):
        sides = {}
        for key, (side, fill) in ((w_sides or [{}, {}])[i]).items():
            sides[key] = (side, functools.partial(fill, wls))
        xt, res = _layer_fwd(lay, i, xt, modarrs[i], params[i], wls[i], csts[i], sps[i], bands, inv, perm, sides,
                             last=(i == 1))
        ress.append(res)
    dx, lossv = _resnorm_loss(lay, ress[1]["x1"], ress[1]["f"], params[1]["norm_ffn_post"].reshape(1, D), modarrs[1], 5,
                              target)
    bigs, smalls, dmods, early = [None, None], [None, None], [None, None], []
    for i in (1, 0):
        fns = {}
        if i == 0 and g_side_fns is not None:
            fns = {key: functools.partial(fn, bigs[1]) for key, fn in g_side_fns.items()}
        dx, bigs[i], smalls[i], dmods[i], ex = _layer_bwd(lay, i, dx, modarrs[i], params[i], wls[i], csts[i], sps[i],
                                                           bands, inv, perm, ress[i], fns)
        early += ex
    return lossv[0, 0], dx.reshape(bl, lat, D), bigs, smalls, dmods, early


def _my_index():
    return 4 * lax.axis_index("x") + 2 * lax.axis_index("y") + lax.axis_index("c")


def _peer(k):
    x, y, c = lax.axis_index("x"), lax.axis_index("y"), lax.axis_index("c")
    kx, ky, kc = (k >> 2) & 1, (k >> 1) & 1, k & 1
    px = 1 - x if kx else x
    py = 1 - y if ky else y
    pc = 1 - c if kc else c
    return (px, py, pc), 4 * px + 2 * py + pc


class _Side:
    def __init__(self, items):
        self.items = items
        self.n = len(items)
        self.ncopies = sum(len(it[2]) for it in items)
        self.arrays = [it[0] for it in items]
        anyspec = pl.BlockSpec(memory_space=pl.ANY)
        self.in_specs = [anyspec] * self.n
        self.out_specs = [anyspec] * self.n
        self.out_shape = [jax.ShapeDtypeStruct((slots,) + tuple(a.shape) if mode == "gather" else tuple(a.shape), a.dtype)
                          for a, mode, ks, slots in items]
        self.scratch = [pltpu.SemaphoreType.DMA((self.ncopies,)), pltpu.SemaphoreType.DMA((self.ncopies,)),
                        pltpu.SemaphoreType.DMA((self.n,))]

    def _copies(self, ins, outs, sems):
        send_sems, recv_sems, local_sems = sems
        slot_of = lambda idx, slots: idx if slots == 8 else (idx // 2 if slots == 4 else idx % 2)
        me = _my_index()
        local, sends, recvs = [], [], []
        q = 0
        for t, (arr, mode, ks, slots) in enumerate(self.items):
            src_own = ins[t] if mode == "gather" else ins[t].at[me]
            local.append(pltpu.make_async_copy(src_own, outs[t].at[slot_of(me, slots)], local_sems.at[t]))
            for k in ks:
                peer, pidx = _peer(k)
                src = ins[t] if mode == "gather" else ins[t].at[pidx]
                sends.append(pltpu.make_async_remote_copy(
                    src_ref=src, dst_ref=outs[t].at[slot_of(me, slots)], send_sem=send_sems.at[q], recv_sem=recv_sems.at[q],
                    device_id=peer, device_id_type=pl.DeviceIdType.MESH))
                recvs.append(pltpu.make_async_remote_copy(
                    src_ref=src, dst_ref=outs[t].at[slot_of(pidx, slots)], send_sem=send_sems.at[q], recv_sem=recv_sems.at[q],
                    device_id=peer, device_id_type=pl.DeviceIdType.MESH))
                q += 1
        return local, sends, recvs

    def start(self, ins, outs, sems):
        local, sends, _ = self._copies(ins, outs, sems)
        for cp in sends + local:
            cp.start()

    def wait(self, ins, outs, sems):
        local, sends, recvs = self._copies(ins, outs, sems)
        for cp in recvs:
            cp.wait_recv()
        for cp in sends:
            cp.wait_send()
        for cp in local:
            cp.wait()


def _comm(items, name):
    side = _Side(items)
    n = side.n

    def body(*refs):
        ins, outs, sems = refs[:n], refs[n:2 * n], refs[2 * n:]
        side.start(ins, outs, sems)
        side.wait(ins, outs, sems)

    return pl.pallas_call(
        body, in_specs=side.in_specs, out_specs=side.out_specs, out_shape=side.out_shape, scratch_shapes=side.scratch,
        compiler_params=pltpu.CompilerParams(has_side_effects=True), name=name)(*side.arrays)


def _spread(items, name):
    n = len(items)
    ncopies = sum(len(it[1]) for it in items)

    def slot_of(idx, slots):
        return idx if slots == 8 else (idx // 2 if slots == 4 else idx % 2)

    def body(*refs):
        ins, outs, bufs = refs[:n], refs[n:2 * n], refs[2 * n:3 * n]
        load_sems, store_sems, send_sems, recv_sems = refs[3 * n:]
        me = _my_index()
        loads = [pltpu.make_async_copy(ins[t], bufs[t], load_sems.at[t]) for t in range(n)]
        for cp in loads:
            cp.start()
        stores, sends, recvs = [], [], []
        q = 0
        for t, (arr, ks, slots) in enumerate(items):
            loads[t].wait()
            own = outs[t].at[slot_of(me, slots)]
            stores.append(pltpu.make_async_copy(bufs[t], own, store_sems.at[t]))
            stores[-1].start()
            for k in ks:
                peer, pidx = _peer(k)
                sends.append(pltpu.make_async_remote_copy(
                    src_ref=bufs[t], dst_ref=own, send_sem=send_sems.at[q], recv_sem=recv_sems.at[q],
                    device_id=peer, device_id_type=pl.DeviceIdType.MESH))
                recvs.append(pltpu.make_async_remote_copy(
                    src_ref=bufs[t], dst_ref=outs[t].at[slot_of(pidx, slots)], send_sem=send_sems.at[q],
                    recv_sem=recv_sems.at[q], device_id=peer, device_id_type=pl.DeviceIdType.MESH))
                sends[-1].start()
                q += 1
        for cp in recvs:
            cp.wait_recv()
        for cp in sends:
            cp.wait_send()
        for cp in stores:
            cp.wait()

    anyspec = pl.BlockSpec(memory_space=pl.ANY)
    return pl.pallas_call(
        body, in_specs=[anyspec] * n, out_specs=[anyspec] * n,
        out_shape=[jax.ShapeDtypeStruct((slots,) + tuple(arr.shape), arr.dtype) for arr, ks, slots in items],
        scratch_shapes=[pltpu.VMEM(tuple(arr.shape), arr.dtype) for arr, ks, slots in items]
        + [pltpu.SemaphoreType.DMA((n,)), pltpu.SemaphoreType.DMA((n,)), pltpu.SemaphoreType.DMA((ncopies,)),
           pltpu.SemaphoreType.DMA((ncopies,))],
        compiler_params=pltpu.CompilerParams(has_side_effects=True, vmem_limit_bytes=VMEM_LIMIT),
        name=name)(*[it[0] for it in items])


ALL7 = (1, 2, 3, 4, 5, 6, 7)
CHIPS3 = (2, 4, 6)


def _sum8(parts, name):
    def one(a, nm):
        _, r, c = a.shape
        tr = r if r <= 512 else _pick_rows(r)

        def body(a_ref, o_ref):
            acc = a_ref[0].astype(F32)
            for q in range(1, a_ref.shape[0]):
                acc = acc + a_ref[q].astype(F32)
            o_ref[...] = acc

        return pl.pallas_call(
            body, grid=(r // tr,), in_specs=[pl.BlockSpec((a.shape[0], tr, c), lambda i: (0, i, 0))],
            out_specs=pl.BlockSpec((tr, c), lambda i: (i, 0)), out_shape=jax.ShapeDtypeStruct((r, c), F32),
            compiler_params=_cp(("parallel",)), name=nm)(a)

    return [one(a, f"{name}{i}") for i, a in enumerate(parts)]


def _pick_rows(r, cap=512):
    for t in (512, 352, 256, 176, 128, 64, 32, 16, 8):
        if r % t == 0 and t <= cap:
            return t
    return r


def _adam(w, g, m, v, name):
    shape = w.shape
    nel = int(np.prod(shape))
    c1 = 1.0 / (1.0 - ADAM_B1 ** ADAM_STEP)
    c2 = 1.0 / (1.0 - ADAM_B2 ** ADAM_STEP)

    def body(w_ref, g_ref, m_ref, v_ref, d_ref, nm_ref, nv_ref):
        gv = g_ref[...]
        nm = ADAM_B1 * m_ref[...] + (1.0 - ADAM_B1) * gv
        nv = ADAM_B2 * v_ref[...] + (1.0 - ADAM_B2) * (gv * gv)
        d_ref[...] = -ADAM_LR * ((nm * c1) / (jnp.sqrt(nv * c2) + ADAM_EPS) + ADAM_WD * w_ref[...])
        nm_ref[...] = nm
        nv_ref[...] = nv

    padded = int(np.prod(shape[:-2])) * (-(-shape[-2] // 8) * 8) * (-(-shape[-1] // 128) * 128) if len(shape) >= 2 else nel
    if len(shape) >= 2 and padded <= 1024 * 1024:
        sh = jax.ShapeDtypeStruct(shape, F32)
        return pl.pallas_call(body, out_shape=[sh] * 3, compiler_params=_cp(None), name=name)(w, g, m, v)

    if len(shape) >= 2 and shape[-1] >= 128:
        lanes = shape[-1]
    else:
        lanes = 512 if nel % 512 == 0 else 128
    r = nel // lanes
    tr = r if r * lanes <= 384 * 1024 else _pick_rows(r, 384 * 1024 // lanes)

    spec = pl.BlockSpec((tr, lanes), lambda i: (i, 0))
    sh = jax.ShapeDtypeStruct((r, lanes), F32)
    outs = pl.pallas_call(
        body, grid=(r // tr,), in_specs=[spec] * 4, out_specs=[spec] * 3, out_shape=[sh] * 3,
        compiler_params=_cp(("parallel",)), name=name)(*[a.reshape(r, lanes) for a in (w, g, m, v)])
    return [o.reshape(shape) for o in outs]


def _silu(x):
    return x * _sigmoid(x)


def _mod_fwd(c_rows, w_mod, b_cols, name):
    def body(c_ref, w_ref, b_ref, o_ref):
        s = _silu(c_ref[...])
        for l in range(2):
            o_ref[l] = jnp.dot(s, w_ref[l], preferred_element_type=F32, precision=lax.Precision.HIGHEST) + b_ref[l]

    nc = w_mod.shape[2]
    return pl.pallas_call(body, out_shape=jax.ShapeDtypeStruct((2, c_rows.shape[0], nc), F32),
                          compiler_params=_cp(None), name=name)(c_rows, w_mod, b_cols)


def _mod_bwd(c_rows, w_mod, dlat, dctx8, name):
    nrow = c_rows.shape[0]
    nb = nrow - 8

    def body(c_ref, w_ref, dl_ref, dc_ref, gw_ref, gc_ref):
        s = _silu(c_ref[...])
        ctx_row = lax.broadcasted_iota(jnp.int32, (nrow, 1), 0) == nb
        gc = jnp.zeros((1, D), F32)
        for l in range(2):
            dctx = dc_ref[0, l]
            for q in range(1, 8):
                dctx = dctx + dc_ref[q, l]
            dm = dl_ref[l] + jnp.where(ctx_row, dctx, 0.0)
            gw_ref[l] = lax.dot_general(s, dm, (((0,), (0,)), ((), ())), preferred_element_type=F32,
                                        precision=lax.Precision.HIGHEST)
            gc = gc + lax.dot_general(dctx, w_ref[l], (((1,), (1,)), ((), ())), preferred_element_type=F32,
                                      precision=lax.Precision.HIGHEST)
        gc_ref[...] = gc

    nc = w_mod.shape[2]
    return pl.pallas_call(body, out_shape=[jax.ShapeDtypeStruct((2, D, nc), F32), jax.ShapeDtypeStruct((1, D), F32)],
                          compiler_params=_cp(None), name=name)(c_rows, w_mod, dlat, dctx8)


def _bmod_cctx(dmod_all, gc4, c_ctx, name):
    def body(dm_ref, gc_ref, cc_ref, gb_ref, gcc_ref):
        for l in range(2):
            acc = jnp.sum(dm_ref[0, l], axis=0, keepdims=True)
            for q in range(1, 8):
                acc = acc + jnp.sum(dm_ref[q, l], axis=0, keepdims=True)
            gb_ref[l:l + 1, :] = acc
        g = gc_ref[0] + gc_ref[1] + gc_ref[2] + gc_ref[3]
        cv = cc_ref[...]
        sg = _sigmoid(cv)
        gcc_ref[...] = g * (sg * (1.0 + cv * (1.0 - sg)))

    return pl.pallas_call(body, out_shape=[jax.ShapeDtypeStruct((2, 6 * D), F32), jax.ShapeDtypeStruct((1, D), F32)],
                          compiler_params=_cp(None), name=name)(dmod_all, gc4, c_ctx)


def kernel(x, c, ctx, c_ctx, w_mod, b_mod, norm_mix_pre, norm_mix_post, norm_ffn_pre, norm_ffn_post, w_in, w_out, sgu_w, sgu_b, ssm_lam_re, ssm_lam_im, ssm_log_dt, ssm_b_re, ssm_b_im, ssm_c_re, ssm_c_im, ssm_d, glu_w, glu_b, pool_w, pool_scale, ffn_w_gate, ffn_w_up, ffn_w_down, loss_target, m_c_ctx, m_w_mod, m_b_mod, m_norm_mix_pre, m_norm_mix_post, m_norm_ffn_pre, m_norm_ffn_post, m_w_in, m_w_out, m_sgu_w, m_sgu_b, m_ssm_lam_re, m_ssm_lam_im, m_ssm_log_dt, m_ssm_b_re, m_ssm_b_im, m_ssm_c_re, m_ssm_c_im, m_ssm_d, m_glu_w, m_glu_b, m_pool_w, m_pool_scale, m_ffn_w_gate, m_ffn_w_up, m_ffn_w_down, v_c_ctx, v_w_mod, v_b_mod, v_norm_mix_pre, v_norm_mix_post, v_norm_ffn_pre, v_norm_ffn_post, v_w_in, v_w_out, v_sgu_w, v_sgu_b, v_ssm_lam_re, v_ssm_lam_im, v_ssm_log_dt, v_ssm_b_re, v_ssm_b_im, v_ssm_c_re, v_ssm_c_im, v_ssm_d, v_glu_w, v_glu_b, v_pool_w, v_pool_scale, v_ffn_w_gate, v_ffn_w_up, v_ffn_w_down):
    wts = dict(c_ctx=c_ctx, w_mod=w_mod, b_mod=b_mod, norm_mix_pre=norm_mix_pre, norm_mix_post=norm_mix_post,
               norm_ffn_pre=norm_ffn_pre, norm_ffn_post=norm_ffn_post, w_in=w_in, w_out=w_out, sgu_w=sgu_w, sgu_b=sgu_b,
               ssm_lam_re=ssm_lam_re, ssm_lam_im=ssm_lam_im, ssm_log_dt=ssm_log_dt, ssm_b_re=ssm_b_re, ssm_b_im=ssm_b_im,
               ssm_c_re=ssm_c_re, ssm_c_im=ssm_c_im, ssm_d=ssm_d, glu_w=glu_w, glu_b=glu_b, pool_w=pool_w,
               pool_scale=pool_scale, ffn_w_gate=ffn_w_gate, ffn_w_up=ffn_w_up, ffn_w_down=ffn_w_down)
    ms = dict(c_ctx=m_c_ctx, w_mod=m_w_mod, b_mod=m_b_mod, norm_mix_pre=m_norm_mix_pre, norm_mix_post=m_norm_mix_post,
              norm_ffn_pre=m_norm_ffn_pre, norm_ffn_post=m_norm_ffn_post, w_in=m_w_in, w_out=m_w_out, sgu_w=m_sgu_w,
              sgu_b=m_sgu_b, ssm_lam_re=m_ssm_lam_re, ssm_lam_im=m_ssm_lam_im, ssm_log_dt=m_ssm_log_dt,
              ssm_b_re=m_ssm_b_re, ssm_b_im=m_ssm_b_im, ssm_c_re=m_ssm_c_re, ssm_c_im=m_ssm_c_im, ssm_d=m_ssm_d,
              glu_w=m_glu_w, glu_b=m_glu_b, pool_w=m_pool_w, pool_scale=m_pool_scale, ffn_w_gate=m_ffn_w_gate,
              ffn_w_up=m_ffn_w_up, ffn_w_down=m_ffn_w_down)
    vs = dict(c_ctx=v_c_ctx, w_mod=v_w_mod, b_mod=v_b_mod, norm_mix_pre=v_norm_mix_pre, norm_mix_post=v_norm_mix_post,
              norm_ffn_pre=v_norm_ffn_pre, norm_ffn_post=v_norm_ffn_post, w_in=v_w_in, w_out=v_w_out, sgu_w=v_sgu_w,
              sgu_b=v_sgu_b, ssm_lam_re=v_ssm_lam_re, ssm_lam_im=v_ssm_lam_im, ssm_log_dt=v_ssm_log_dt,
              ssm_b_re=v_ssm_b_re, ssm_b_im=v_ssm_b_im, ssm_c_re=v_ssm_c_re, ssm_c_im=v_ssm_c_im, ssm_d=v_ssm_d,
              glu_w=v_glu_w, glu_b=v_glu_b, pool_w=v_pool_w, pool_scale=v_pool_scale, ffn_w_gate=v_ffn_w_gate,
              ffn_w_up=v_ffn_w_up, ffn_w_down=v_ffn_w_down)
    order = list(wts.keys())
    bl = x.shape[0]
    nseq = bl * N_DEV
    me = _my_index()
    chip = me // 2
    ncol = w_mod.shape[2]

    (c_all,) = _spread([(c, ALL7, 8)], "ag_c")
    nrow = nseq + 8
    c_rows = jnp.concatenate([c_all.reshape(nseq, D), c_ctx[None], jnp.zeros((7, D), F32)], axis=0)
    b_cols = lax.dynamic_slice_in_dim(b_mod, chip * ncol, ncol, axis=1)[:, None, :]
    mod_cols = _mod_fwd(c_rows, w_mod, b_cols, "mod_fwd")
    stash = {}

    def mods_local():
        mods = jnp.transpose(stash["mod4"], (1, 2, 0, 3)).reshape(2, nrow, 6 * D)
        return jnp.concatenate([lax.dynamic_slice_in_dim(mods, me * bl, bl, axis=1), mods[:, nseq:nseq + 1],
                                jnp.zeros((2, 8 - bl - 1, 6 * D), F32)], axis=1)

    shards = {}
    for i in range(2):
        for nme, s in zip(BIG_NAMES, [w_in[i].T, w_out[i], glu_w[i], ffn_w_gate[i].T, ffn_w_up[i].T, ffn_w_down[i]]):
            shards[(i, nme)] = s.astype(MXU_DTYPE)
    weights = [{}, {}]
    ffn_names = ("wg_t", "wu_t", "wd")
    w_plan = [{"embed": [(0, "win_t")], "win": [(0, "wout"), (0, "wglu")], "ssm": [(0, "wg_t"), (0, "wu_t")],
               "ffn": [(0, "wd"), (1, "win_t"), (1, "wout"), (1, "wglu")]},
              {"ssm": [(1, "wg_t"), (1, "wu_t")], "ffn": [(1, "wd")]}]

    def w_entry(keys, more=()):
        def fill(wls, gathered):
            for (i, nme), g in zip(keys, gathered):
                wls[i][nme] = g.reshape(-1, g.shape[-1])
            for (nme, _), g in zip(more, gathered[len(keys):]):
                stash[nme] = g
        return _Side([(shards[k2], "gather", CHIPS3, 4) for k2 in keys] + [(a, "gather", CHIPS3, 4) for _, a in more]), fill

    w_sides = [{key: w_entry(keys) for key, keys in plan.items()} for plan in w_plan]
    w_sides[0]["embed"] = w_entry(w_plan[0]["embed"], more=[("mod4", mod_cols)])

    eighths = lambda g: g.reshape(8, g.shape[0] // 8, g.shape[1])
    g_plan = {"ffn_down": [(1, "win_t"), (1, "wg_t")], "dh2": [(1, "wu_t"), (1, "wout"), (1, "wglu")],
              "ssm": [(0, k) for k in BIG_NAMES if k != "win_t"] + [(1, "wd")], "dh": [(0, "win_t")]}
    early_g = g_plan["ffn_down"] + g_plan["dh2"] + g_plan["ssm"] + g_plan["dh"]

    def g_entry(keys):
        return lambda big1, big0: _Side([(eighths((big1 if i == 1 else big0)[k]), "a2a", ALL7, 8) for i, k in keys])

    g_side_fns = {key: g_entry(keys) for key, keys in g_plan.items()}

    params = [{k: wts[k][i] for k in SMALL_NAMES} for i in range(2)]
    loss_part, grad_x, bigs, smalls, dmods, early = _core(x, ctx, loss_target, mods_local, params, weights,
                                                           w_sides, g_side_fns)
    loss = lax.psum(loss_part, ("x", "y", "c"))

    dmod_local = jnp.stack([dmods[i].reshape(8, 6 * D) for i in range(2)])
    (dmod_all,) = _spread([(dmod_local, ALL7, 8)], "ag_dmod")
    dcols = lax.dynamic_slice_in_dim(dmod_all, chip * ncol, ncol, axis=3)
    dlat = jnp.transpose(dcols[:, :, :bl], (1, 0, 2, 3)).reshape(2, nseq, ncol)
    dlat = jnp.concatenate([dlat, jnp.zeros((2, 8, ncol), F32)], axis=1)
    dctx8 = dcols[:, :, bl:bl + 1]
    g_w_mod, gc_part = _mod_bwd(c_rows, w_mod, dlat, dctx8, "mod_bwd")
    (gc4,) = _spread([(gc_part, CHIPS3, 4)], "ag_cctx")
    g_b_mod, g_c_ctx = _bmod_cctx(dmod_all, gc4, c_ctx[None], "bmod_cctx")

    small_flat = jnp.concatenate([jnp.stack([smalls[i][k] for i in range(2)]).reshape(-1) for k in SMALL_NAMES])
    npad = (-small_flat.shape[0]) % (8 * 1024)
    small_flat = jnp.concatenate([small_flat, jnp.zeros((npad,), F32)])
    late = _comm([(small_flat.reshape(8, -1, 1024), "a2a", ALL7, 8)], "a2a_grads")
    sums = _sum8(list(early) + list(late), "gsum")
    fin = _spread([(s, (1,), 2) for s in sums[:-1]] + [(sums[-1], ALL7, 8)], "ag_grads")
    big_g = [{}, {}]
    for (i, k), g in zip(early_g, fin[:-1]):
        big_g[i][k] = g.reshape(-1, g.shape[-1])
    small_red = fin[-1].reshape(-1)

    grads = {}
    off = 0
    for k in SMALL_NAMES:
        shp = wts[k].shape
        nel = int(np.prod(shp))
        grads[k] = small_red[off:off + nel].reshape(shp)
        off += nel
    grads["c_ctx"] = g_c_ctx[0]
    grads["w_mod"] = g_w_mod
    grads["b_mod"] = g_b_mod
    grads["w_in"] = jnp.stack([big_g[i]["win_t"].T for i in range(2)])
    grads["w_out"] = jnp.stack([big_g[i]["wout"] for i in range(2)])
    grads["glu_w"] = jnp.stack([big_g[i]["wglu"] for i in range(2)])
    grads["ffn_w_gate"] = jnp.stack([big_g[i]["wg_t"].T for i in range(2)])
    grads["ffn_w_up"] = jnp.stack([big_g[i]["wu_t"].T for i in range(2)])
    grads["ffn_w_down"] = jnp.stack([big_g[i]["wd"] for i in range(2)])

    deltas, new_m, new_v = {}, {}, {}
    for k in order:
        deltas[k], new_m[k], new_v[k] = _adam(wts[k], grads[k], ms[k], vs[k], "adam_" + k)
    return (loss, grad_x, *[grads[k] for k in order], *[deltas[k] for k in order],
            *[new_m[k] for k in order], *[new_v[k] for k in order])
```

```python
import functools
import math

import numpy as np
import jax
import jax.numpy as jnp
from jax import lax
from jax.experimental import pallas as pl
from jax.experimental.pallas import tpu as pltpu

F32 = jnp.float32
BF16 = jnp.bfloat16
MXU_DTYPE = jnp.bfloat16
MCAT_A, MCAT_C = 2, 3
WOUT_PERM, WOUT_INV = (1, 2, 0, 3), (2, 0, 1, 3)

D = 1024
EPS = 1e-6
TB = 256
CTX = 256
CHUNK = 128
GRID_W = 64
A_W, B_W, C_W = 256, 512, 256
D_IN = 1280
D_FF = 2816
SSM_G, SSM_P, SSM_H = 32, 64, 16
ST = 64
POOL_WINDOWS = (2, 4, 8, 16)
N_DEV = 8
VMEM_LIMIT = 52 * 1024 * 1024
GELU_C = math.sqrt(2.0 / math.pi)

ADAM_LR, ADAM_B1, ADAM_B2, ADAM_EPS, ADAM_WD, ADAM_STEP = 0.001, 0.9, 0.999, 1e-08, 0.01, 10


def _cp(sem=None, vmem=VMEM_LIMIT, **kw):
    return pltpu.CompilerParams(dimension_semantics=sem, vmem_limit_bytes=vmem, **kw)


def _pick(n, cap):
    if n <= cap:
        return n
    best = None
    for t in range(128, cap + 1, 128):
        if n % t == 0:
            best = t
    assert best is not None, (n, cap)
    return best


def _gelu(x):
    return 0.5 * x * (1.0 + jnp.tanh(GELU_C * (x + 0.044715 * x * x * x)))


def _gelu_grad(x):
    t = jnp.tanh(GELU_C * (x + 0.044715 * x * x * x))
    return 0.5 * (1.0 + t) + 0.5 * x * (1.0 - t * t) * GELU_C * (1.0 + 3.0 * 0.044715 * x * x)


def _sigmoid(x):
    return 1.0 / (1.0 + jnp.exp(-x))


def _dot(a, b, dims):
    return lax.dot_general(a, b, (dims, ((), ())), preferred_element_type=F32)


def _nn(a, b):
    return _dot(a, b, ((1,), (0,)))


def _nt(a, b):
    return _dot(a, b, ((1,), (1,)))


def _tn(a, b):
    return _dot(a, b, ((0,), (0,)))


def _mm(pairs, nt, out_dtype, name, tm=512, side=None):
    m = pairs[0][0].shape[0]
    n = pairs[0][1].shape[0] if nt else pairs[0][1].shape[1]
    tn = _pick(n, 1408)
    tm = min(tm, m)
    npairs = len(pairs)
    ni, nj = m // tm, n // tn

    def body(*refs):
        own, srefs = _side_split(refs, 2 * npairs, 1, 0, side)
        o_ref = own[-1]
        i, j = pl.program_id(0), pl.program_id(1)
        _side_start(side, srefs, jnp.logical_and(i == 0, j == 0))
        acc = None
        for t in range(npairs):
            a = own[2 * t][...].astype(MXU_DTYPE)
            b = own[2 * t + 1][...].astype(MXU_DTYPE)
            r = _nt(a, b) if nt else _nn(a, b)
            acc = r if acc is None else acc + r
        o_ref[...] = acc.astype(o_ref.dtype)
        _side_wait(side, srefs, jnp.logical_and(i == ni - 1, j == nj - 1))

    sd = side if side is not None else _Side([])
    in_specs, flat = [], []
    for a, b in pairs:
        k = a.shape[1]
        in_specs.append(pl.BlockSpec((tm, k), lambda i, j: (i, 0)))
        in_specs.append(pl.BlockSpec((tn, k), lambda i, j: (j, 0)) if nt else pl.BlockSpec((k, tn), lambda i, j: (0, j)))
        flat += [a, b]
    outs = pl.pallas_call(
        body, grid=(ni, nj), in_specs=in_specs + sd.in_specs,
        out_specs=[pl.BlockSpec((tm, tn), lambda i, j: (i, j))] + sd.out_specs,
        out_shape=[jax.ShapeDtypeStruct((m, n), out_dtype)] + sd.out_shape,
        scratch_shapes=sd.scratch if side is not None else [],
        compiler_params=_cp(("arbitrary", "arbitrary") if side is not None else ("parallel", "parallel")),
        name=name)(*flat, *sd.arrays)
    return outs[0] if side is None else (outs[0], list(outs[1:]))


def _mm_tn(a, b, out_dtype, name):
    m, k1 = a.shape
    n = b.shape[1]
    t1 = _pick(k1, 1408)
    tn = _pick(n, 1024)
    tm = max(t for t in (512, 1024, 1536) if m % t == 0)
    nsteps = m // tm

    def body(a_ref, b_ref, o_ref, acc_ref):
        t = pl.program_id(2)

        @pl.when(t == 0)
        def _():
            acc_ref[...] = jnp.zeros_like(acc_ref)

        acc_ref[...] += _tn(a_ref[...].astype(MXU_DTYPE), b_ref[...].astype(MXU_DTYPE))

        @pl.when(t == nsteps - 1)
        def _():
            o_ref[...] = acc_ref[...].astype(o_ref.dtype)

    return pl.pallas_call(
        body, grid=(k1 // t1, n // tn, nsteps),
        in_specs=[pl.BlockSpec((tm, t1), lambda i, j, t: (t, i)), pl.BlockSpec((tm, tn), lambda i, j, t: (t, j))],
        out_specs=pl.BlockSpec((t1, tn), lambda i, j, t: (i, j)),
        out_shape=jax.ShapeDtypeStruct((k1, n), out_dtype),
        scratch_shapes=[pltpu.VMEM((t1, tn), F32)],
        compiler_params=_cp(("parallel", "parallel", "arbitrary")), name=name)(a, b)


class _Layout:
    def __init__(self, bl, lat):
        self.bl, self.lat = bl, lat
        self.nlb = lat // TB
        self.nr = 1 + self.nlb
        self.nctx = bl
        self.nb = self.nr * bl
        self.nt = self.nb * TB
        self.ctx_row = bl

    def mod_tiles(self, mods):
        rows = np.array([[self.ctx_row if r == 0 else b for b in range(self.bl)] for r in range(self.nr)], np.int32)
        t = mods[rows].reshape(self.nr, self.bl, 6, D)
        return jnp.transpose(t, (0, 2, 1, 3)).reshape(self.nr * 6, self.bl, 1, D)


ST_FWD, ST_BWD = 4, 2


def _tok_spec(lay, st):
    nc = lay.bl // st
    return pl.BlockSpec((st * TB, D), lambda c, r: (r * nc + c, 0))


def _vec_spec():
    return pl.BlockSpec((1, D), lambda c, r: (0, 0))


def _mod_spec(st, k):
    return pl.BlockSpec((1, st, 1, D), lambda c, r: (r * 6 + k, c, 0, 0))


def _x_spec(lay, st):
    return pl.BlockSpec((st, 1, TB, D), lambda c, r: (c, jnp.maximum(r - 1, 0), 0, 0))


def _rows3(ref_or_val, st):
    return ref_or_val.reshape(st, TB, D)


def _acc_rows(acc_ref, val3, st, ctx_row):
    c, r = pl.program_id(0), pl.program_id(1)
    s = jnp.sum(val3, axis=1, keepdims=True)

    @pl.when(r == 0)
    def _():
        acc_ref[ctx_row:ctx_row + 1] += jnp.sum(s, axis=0, keepdims=True)

    @pl.when(r > 0)
    def _():
        acc_ref[pl.ds(c * st, st)] += s


def _first_step():
    return jnp.logical_and(pl.program_id(0) == 0, pl.program_id(1) == 0)


def _embed(lay, x, ctx, pe, side=None):
    st = ST_FWD
    bl, nlb = lay.bl, lay.nlb
    nc = bl // st

    def body(*refs):
        (x_ref, c_ref, pe_ref, o_ref), srefs = _side_split(refs, 3, 1, 0, side)
        c, r = pl.program_id(0), pl.program_id(1)
        _side_start(side, srefs, jnp.logical_and(c == 0, r == 0))

        @pl.when(r == 0)
        def _():
            o_ref[...] = c_ref[...].reshape(st * TB, D)

        @pl.when(r > 0)
        def _():
            o_ref[...] = (x_ref[...].reshape(st, TB, D) + pe_ref[...]).reshape(st * TB, D)

        _side_wait(side, srefs, jnp.logical_and(c == nc - 1, r == lay.nr - 1))

    sd = side if side is not None else _Side([])
    outs = pl.pallas_call(
        body, grid=(nc, lay.nr),
        in_specs=[_x_spec(lay, st), pl.BlockSpec((st, CTX, D), lambda c, r: (c, 0, 0)),
                  pl.BlockSpec((1, TB, D), lambda c, r: (jnp.maximum(r - 1, 0), 0, 0))] + sd.in_specs,
        out_specs=[_tok_spec(lay, st)] + sd.out_specs,
        out_shape=[jax.ShapeDtypeStruct((lay.nt, D), F32)] + sd.out_shape,
        scratch_shapes=sd.scratch if side is not None else [],
        compiler_params=_cp(("arbitrary", "arbitrary") if side is not None else ("parallel", "parallel")),
        name="embed")(x.reshape(bl, nlb, TB, D), ctx, pe.reshape(nlb, TB, D), *sd.arrays)
    return outs[0], list(outs[1:])


def _normmod_fwd(lay, x, gain, modt, ksh, ksc, name):
    st = ST_FWD

    def body(x_ref, g_ref, sh_ref, sc_ref, o_ref):
        xv = _rows3(x_ref[...], st)
        r = lax.rsqrt(jnp.mean(xv * xv, axis=-1, keepdims=True) + EPS)
        o_ref[...] = ((xv * r * g_ref[...]) * (1.0 + sc_ref[0]) + sh_ref[0]).reshape(st * TB, D).astype(o_ref.dtype)

    return pl.pallas_call(
        body, grid=(lay.bl // st, lay.nr),
        in_specs=[_tok_spec(lay, st), _vec_spec(), _mod_spec(st, ksh), _mod_spec(st, ksc)],
        out_specs=_tok_spec(lay, st), out_shape=jax.ShapeDtypeStruct((lay.nt, D), MXU_DTYPE),
        compiler_params=_cp(("parallel", "parallel")), name=name)(x, gain, modt, modt)


def _acc_out():
    return pl.BlockSpec((8, 1, D), lambda c, r: (0, 0, 0)), jax.ShapeDtypeStruct((8, 1, D), F32)


def _normmod_bwd(lay, x, dh, dx_in, gain, modt, ksc, name, latent_only=False):
    st = ST_BWD
    acc_spec, acc_shape = _acc_out()
    if latent_only:
        dx_spec, dx_shape = _x_spec(lay, st), jax.ShapeDtypeStruct((lay.bl, lay.nlb, TB, D), F32)
    else:
        dx_spec, dx_shape = _tok_spec(lay, st), jax.ShapeDtypeStruct((lay.nt, D), F32)

    def body(x_ref, dh_ref, dxi_ref, g_ref, sc_ref, dx_ref, dsh_ref, dsc_ref, dg_ref):
        xv = _rows3(x_ref[...], st)
        dhv = _rows3(dh_ref[...].astype(F32), st)
        g = g_ref[...]
        sc1 = 1.0 + sc_ref[0]
        r = lax.rsqrt(jnp.mean(xv * xv, axis=-1, keepdims=True) + EPS)
        xh = xv * r
        dxh = dhv * (g * sc1)
        dx = _rows3(dxi_ref[...], st) + r * (dxh - xh * jnp.mean(dxh * xh, axis=-1, keepdims=True))
        dx_ref[...] = dx.reshape(dx_ref.shape)

        @pl.when(_first_step())
        def _():
            dsh_ref[...] = jnp.zeros_like(dsh_ref)
            dsc_ref[...] = jnp.zeros_like(dsc_ref)
            dg_ref[...] = jnp.zeros_like(dg_ref)

        _acc_rows(dsh_ref, dhv, st, lay.ctx_row)
        _acc_rows(dsc_ref, dhv * (xh * g), st, lay.ctx_row)
        dg_ref[...] += jnp.sum((dhv * sc1 * xh).reshape(st * TB, D), axis=0, keepdims=True)

    return pl.pallas_call(
        body, grid=(lay.bl // st, lay.nr),
        in_specs=[_tok_spec(lay, st), _tok_spec(lay, st), _tok_spec(lay, st), _vec_spec(), _mod_spec(st, ksc)],
        out_specs=[dx_spec, acc_spec, acc_spec, _vec_spec()],
        out_shape=[dx_shape, acc_shape, acc_shape, jax.ShapeDtypeStruct((1, D), F32)],
        compiler_params=_cp(("arbitrary", "arbitrary")), name=name)(x, dh, dx_in, gain, modt)


def _resnorm_fwd(lay, x, m, gain, modt, kgate, name):
    st = ST_FWD

    def body(x_ref, m_ref, g_ref, gate_ref, o_ref):
        mv = _rows3(m_ref[...].astype(F32), st)
        r = lax.rsqrt(jnp.mean(mv * mv, axis=-1, keepdims=True) + EPS)
        o_ref[...] = x_ref[...] + (gate_ref[0] * (mv * r * g_ref[...])).reshape(st * TB, D)

    return pl.pallas_call(
        body, grid=(lay.bl // st, lay.nr),
        in_specs=[_tok_spec(lay, st), _tok_spec(lay, st), _vec_spec(), _mod_spec(st, kgate)],
        out_specs=_tok_spec(lay, st), out_shape=jax.ShapeDtypeStruct((lay.nt, D), F32),
        compiler_params=_cp(("parallel", "parallel")), name=name)(x, m, gain, modt)


def _resnorm_bwd(lay, dxn, m, gain, modt, kgate, name):
    st = ST_FWD
    acc_spec, acc_shape = _acc_out()

    def body(d_ref, m_ref, g_ref, gate_ref, dm_ref, dgate_ref, dg_ref):
        dv = _rows3(d_ref[...], st)
        mv = _rows3(m_ref[...].astype(F32), st)
        g = g_ref[...]
        r = lax.rsqrt(jnp.mean(mv * mv, axis=-1, keepdims=True) + EPS)
        xh = mv * r
        dy = dv * gate_ref[0]
        dxh = dy * g
        dm = r * (dxh - xh * jnp.mean(dxh * xh, axis=-1, keepdims=True))
        dm_ref[...] = dm.reshape(st * TB, D).astype(dm_ref.dtype)

        @pl.when(_first_step())
        def _():
            dgate_ref[...] = jnp.zeros_like(dgate_ref)
            dg_ref[...] = jnp.zeros_like(dg_ref)

        _acc_rows(dgate_ref, dv * (xh * g), st, lay.ctx_row)
        dg_ref[...] += jnp.sum((dy * xh).reshape(st * TB, D), axis=0, keepdims=True)

    return pl.pallas_call(
        body, grid=(lay.bl // st, lay.nr),
        in_specs=[_tok_spec(lay, st), _tok_spec(lay, st), _vec_spec(), _mod_spec(st, kgate)],
        out_specs=[_tok_spec(lay, st), acc_spec, _vec_spec()],
        out_shape=[jax.ShapeDtypeStruct((lay.nt, D), MXU_DTYPE), acc_shape, jax.ShapeDtypeStruct((1, D), F32)],
        compiler_params=_cp(("arbitrary", "arbitrary")), name=name)(dxn, m, gain, modt)


def _rms(v):
    return lax.rsqrt(jnp.mean(v * v, axis=-1, keepdims=True) + EPS)


def _resnorm_normmod_fwd(lay, x, m, gpost, gpre, modt, kgate, ksh, ksc, name):
    st = ST_FWD

    def body(x_ref, m_ref, gp_ref, gq_ref, gate_ref, sh_ref, sc_ref, x1_ref, h_ref):
        mv = _rows3(m_ref[...].astype(F32), st)
        x1 = _rows3(x_ref[...], st) + gate_ref[0] * (mv * _rms(mv) * gp_ref[...])
        x1_ref[...] = x1.reshape(st * TB, D)
        h = (x1 * _rms(x1) * gq_ref[...]) * (1.0 + sc_ref[0]) + sh_ref[0]
        h_ref[...] = h.reshape(st * TB, D).astype(h_ref.dtype)

    tok = _tok_spec(lay, st)
    return pl.pallas_call(
        body, grid=(lay.bl // st, lay.nr),
        in_specs=[tok, tok, _vec_spec(), _vec_spec(), _mod_spec(st, kgate), _mod_spec(st, ksh), _mod_spec(st, ksc)],
        out_specs=[tok, tok],
        out_shape=[jax.ShapeDtypeStruct((lay.nt, D), F32), jax.ShapeDtypeStruct((lay.nt, D), MXU_DTYPE)],
        compiler_params=_cp(("parallel", "parallel")), name=name)(x, m, gpost, gpre, modt, modt, modt)


def _normmod_resnorm_bwd(lay, x1, dh, dx_in, gpre, m, gpost, modt, ksc, kgate, name):
    st = ST_BWD
    acc_spec, acc_shape = _acc_out()

    def body(x_ref, dh_ref, dxi_ref, gq_ref, sc_ref, m_ref, gp_ref, gate_ref,
             dx_ref, dm_ref, dsh_ref, dsc_ref, dgq_ref, dgate_ref, dgp_ref):
        xv = _rows3(x_ref[...], st)
        dhv = _rows3(dh_ref[...].astype(F32), st)
        gq = gq_ref[...]
        sc1 = 1.0 + sc_ref[0]
        r = _rms(xv)
        xh = xv * r
        dxh = dhv * (gq * sc1)
        dx1 = _rows3(dxi_ref[...], st) + r * (dxh - xh * jnp.mean(dxh * xh, axis=-1, keepdims=True))
        dx_ref[...] = dx1.reshape(st * TB, D)
        mv = _rows3(m_ref[...].astype(F32), st)
        gp = gp_ref[...]
        rm = _rms(mv)
        mh = mv * rm
        dy = dx1 * gate_ref[0]
        dmh = dy * gp
        dm = rm * (dmh - mh * jnp.mean(dmh * mh, axis=-1, keepdims=True))
        dm_ref[...] = dm.reshape(st * TB, D).astype(dm_ref.dtype)

        @pl.when(_first_step())
        def _():
            for ref in (dsh_ref, dsc_ref, dgq_ref, dgate_ref, dgp_ref):
                ref[...] = jnp.zeros_like(ref)

        _acc_rows(dsh_ref, dhv, st, lay.ctx_row)
        _acc_rows(dsc_ref, dhv * (xh * gq), st, lay.ctx_row)
        dgq_ref[...] += jnp.sum((dhv * sc1 * xh).reshape(st * TB, D), axis=0, keepdims=True)
        _acc_rows(dgate_ref, dx1 * (mh * gp), st, lay.ctx_row)
        dgp_ref[...] += jnp.sum((dy * mh).reshape(st * TB, D), axis=0, keepdims=True)

    tok = _tok_spec(lay, st)
    vsh = jax.ShapeDtypeStruct((1, D), F32)
    return pl.pallas_call(
        body, grid=(lay.bl // st, lay.nr),
        in_specs=[tok, tok, tok, _vec_spec(), _mod_spec(st, ksc), tok, _vec_spec(), _mod_spec(st, kgate)],
        out_specs=[tok, tok, acc_spec, acc_spec, _vec_spec(), acc_spec, _vec_spec()],
        out_shape=[jax.ShapeDtypeStruct((lay.nt, D), F32), jax.ShapeDtypeStruct((lay.nt, D), MXU_DTYPE),
                   acc_shape, acc_shape, vsh, acc_shape, vsh],
        compiler_params=_cp(("arbitrary", "arbitrary")), name=name)(x1, dh, dx_in, gpre, modt, m, gpost, modt)


def _resnorm_loss(lay, x, f, gain, modt, kgate, tgt):
    st = ST_BWD

    def body(x_ref, f_ref, g_ref, gate_ref, t_ref, dx_ref, l_ref):
        r = pl.program_id(1)

        @pl.when(_first_step())
        def _():
            l_ref[...] = jnp.zeros_like(l_ref)

        @pl.when(r == 0)
        def _():
            dx_ref[...] = jnp.zeros_like(dx_ref)

        @pl.when(r > 0)
        def _():
            fv = _rows3(f_ref[...].astype(F32), st)
            y = _rows3(x_ref[...], st) + gate_ref[0] * (fv * _rms(fv) * g_ref[...])
            e = y - t_ref[...].reshape(st, TB, D)
            dx_ref[...] = (e * (1.0 / D)).reshape(st * TB, D)
            l_ref[...] += jnp.sum(e * e) * (0.5 / D)

    tok = _tok_spec(lay, st)
    return pl.pallas_call(
        body, grid=(lay.bl // st, lay.nr),
        in_specs=[tok, tok, _vec_spec(), _mod_spec(st, kgate), _x_spec(lay, st)],
        out_specs=[tok, pl.BlockSpec((8, 128), lambda c, r: (0, 0))],
        out_shape=[jax.ShapeDtypeStruct((lay.nt, D), F32), jax.ShapeDtypeStruct((8, 128), F32)],
        compiler_params=_cp(("arbitrary", "arbitrary")), name="loss")(
            x, f, gain, modt, tgt.reshape(lay.bl, lay.nlb, TB, D))


FF_TN = D_FF // 2
FF_CHUNKS = ((0, 512), (512, 512), (1024, 384))


def _ffn_up(h, wgt, wut, name, side=None):
    m = h.shape[0]
    tm, tn = min(512, m), FF_TN
    ni, nj = m // tm, D_FF // tn

    def body(*refs):
        (h_ref, wg_ref, wu_ref, g_ref, u_ref, a_ref), srefs = _side_split(refs, 3, 3, 0, side)
        j, i = pl.program_id(0), pl.program_id(1)
        _side_start(side, srefs, jnp.logical_and(i == 0, j == 0))
        hv = h_ref[...]
        for c0, cw in FF_CHUNKS:
            g = _nt(hv, wg_ref[c0:c0 + cw, :])
            u = _nt(hv, wu_ref[c0:c0 + cw, :])
            g_ref[:, c0:c0 + cw] = g.astype(g_ref.dtype)
            u_ref[:, c0:c0 + cw] = u.astype(u_ref.dtype)
            a_ref[:, c0:c0 + cw] = (g * _sigmoid(g) * u).astype(a_ref.dtype)
        _side_wait(side, srefs, jnp.logical_and(i == ni - 1, j == nj - 1))

    sd = side if side is not None else _Side([])
    osp = pl.BlockSpec((tm, tn), lambda j, i: (i, j))
    osh = jax.ShapeDtypeStruct((m, D_FF), MXU_DTYPE)
    outs = pl.pallas_call(
        body, grid=(nj, ni),
        in_specs=[pl.BlockSpec((tm, D), lambda j, i: (i, 0)), pl.BlockSpec((tn, D), lambda j, i: (j, 0)),
                  pl.BlockSpec((tn, D), lambda j, i: (j, 0))] + sd.in_specs,
        out_specs=[osp, osp, osp] + sd.out_specs, out_shape=[osh, osh, osh] + sd.out_shape,
        scratch_shapes=sd.scratch if side is not None else [],
        compiler_params=_cp(("arbitrary", "arbitrary") if side is not None else ("parallel", "parallel")),
        name=name)(h, wgt, wut, *sd.arrays)
    return outs[0], outs[1], outs[2], list(outs[3:])


def _ffn_down_bwd(df, wd, g, u, name, side=None):
    m = df.shape[0]
    tm, tn = min(512, m), FF_TN
    ni, nj = m // tm, D_FF // tn

    def body(*refs):
        (df_ref, wd_ref, g_ref, u_ref, dg_ref, du_ref), srefs = _side_split(refs, 4, 2, 0, side)
        j, i = pl.program_id(0), pl.program_id(1)
        _side_start(side, srefs, jnp.logical_and(i == 0, j == 0))
        dfv = df_ref[...]
        for c0, cw in FF_CHUNKS:
            da = _nt(dfv, wd_ref[c0:c0 + cw, :])
            gv = g_ref[:, c0:c0 + cw].astype(F32)
            uv = u_ref[:, c0:c0 + cw].astype(F32)
            s = _sigmoid(gv)
            dg_ref[:, c0:c0 + cw] = (da * uv * (s * (1.0 + gv * (1.0 - s)))).astype(dg_ref.dtype)
            du_ref[:, c0:c0 + cw] = (da * gv * s).astype(du_ref.dtype)
        _side_wait(side, srefs, jnp.logical_and(i == ni - 1, j == nj - 1))

    sd = side if side is not None else _Side([])
    osp = pl.BlockSpec((tm, tn), lambda j, i: (i, j))
    osh = jax.ShapeDtypeStruct((m, D_FF), MXU_DTYPE)
    outs = pl.pallas_call(
        body, grid=(nj, ni),
        in_specs=[pl.BlockSpec((tm, D), lambda j, i: (i, 0)), pl.BlockSpec((tn, D), lambda j, i: (j, 0)), osp, osp]
        + sd.in_specs,
        out_specs=[osp, osp] + sd.out_specs, out_shape=[osh, osh] + sd.out_shape,
        scratch_shapes=sd.scratch if side is not None else [],
        compiler_params=_cp(("arbitrary", "arbitrary") if side is not None else ("parallel", "parallel")),
        name=name)(df, wd, g, u, *sd.arrays)
    return outs[0], outs[1], list(outs[2:])


def _head_masks(shape):
    lane = lax.broadcasted_iota(jnp.int32, shape, 1)
    return [jnp.logical_and(lane >= 64 * h, lane < 64 * h + 64) for h in range(4)]


def _head_mean(x, masks):
    out = jnp.zeros_like(x)
    for mk in masks:
        s = jnp.sum(jnp.where(mk, x, 0.0), axis=-1, keepdims=True) * (1.0 / 64.0)
        out = jnp.where(mk, s, out)
    return out


def _gate_common(z, masks):
    zg = _gelu(z)
    u = zg[:, :A_W]
    v = zg[:, A_W:]
    mu = _head_mean(v, masks)
    vc = v - mu
    rstd = lax.rsqrt(_head_mean(vc * vc, masks) + EPS)
    return u, vc * rstd, rstd


def _gate_s(vn, ws_ref, bias, masks):
    parts = []
    for c in range(TB // CHUNK):
        vc = vn[c * CHUNK:(c + 1) * CHUNK]
        s = bias
        for h in range(4):
            s = s + _nn(ws_ref[h], jnp.where(masks[h][:CHUNK], vc, 0.0).astype(MXU_DTYPE))
        parts.append(s)
    return jnp.concatenate(parts, axis=0)


MT = 4


def _blocks():
    return [pl.ds(s * TB, TB) for s in range(MT)]


def _gate_fwd(lay, z, ws, bias, name):
    def body(z_ref, ws_ref, b_ref, o_ref):
        masks = _head_masks((TB, A_W))
        for sl in _blocks():
            u, vn, _ = _gate_common(z_ref[sl, :].astype(F32), masks)
            o_ref[sl, :] = (u * _gate_s(vn, ws_ref, b_ref[...], masks)).astype(o_ref.dtype)

    return pl.pallas_call(
        body, grid=(lay.nb // MT,),
        in_specs=[pl.BlockSpec((MT * TB, 2 * A_W), lambda j: (j, 0)), pl.BlockSpec((4, CHUNK, CHUNK), lambda j: (0, 0, 0)),
                  pl.BlockSpec((CHUNK, A_W), lambda j: (0, 0))],
        out_specs=pl.BlockSpec((MT * TB, A_W), lambda j: (j, 0)),
        out_shape=jax.ShapeDtypeStruct((lay.nt, A_W), MXU_DTYPE),
        compiler_params=_cp(("parallel",)), name=name)(z, ws, bias)


def _gate_bwd(lay, z, da, ws, wst, bias, name):
    def body(z_ref, da_ref, ws_ref, wst_ref, b_ref, dz_ref, dws_ref, db_ref):
        j = pl.program_id(0)

        @pl.when(j == 0)
        def _():
            dws_ref[...] = jnp.zeros_like(dws_ref)
            db_ref[...] = jnp.zeros_like(db_ref)

        masks = _head_masks((TB, A_W))
        for blk in _blocks():
            zv = z_ref[blk, :].astype(F32)
            u, vn, rstd = _gate_common(zv, masks)
            s = _gate_s(vn, ws_ref, b_ref[...], masks)
            dav = da_ref[blk, :].astype(F32)
            du = dav * s
            ds = dav * u
            dvn_parts = []
            for c in range(TB // CHUNK):
                sl = slice(c * CHUNK, (c + 1) * CHUNK)
                ds_c = ds[sl]
                vn_c = vn[sl].astype(MXU_DTYPE)
                db_ref[...] += ds_c
                ds_b = ds_c.astype(MXU_DTYPE)
                dvn_c = jnp.zeros((CHUNK, A_W), F32)
                for h in range(4):
                    mk = masks[h][:CHUNK]
                    dws_ref[h] += _nt(jnp.where(mk, ds_c, 0.0).astype(MXU_DTYPE), vn_c)
                    dvn_c = dvn_c + jnp.where(mk, _nn(wst_ref[h], ds_b), 0.0)
                dvn_parts.append(dvn_c)
            dvn = jnp.concatenate(dvn_parts, axis=0)
            dv = rstd * (dvn - _head_mean(dvn, masks) - vn * _head_mean(dvn * vn, masks))
            gg = _gelu_grad(zv)
            dz_ref[blk, :A_W] = (du * gg[:, :A_W]).astype(dz_ref.dtype)
            dz_ref[blk, A_W:] = (dv * gg[:, A_W:]).astype(dz_ref.dtype)

    return pl.pallas_call(
        body, grid=(lay.nb // MT,),
        in_specs=[pl.BlockSpec((MT * TB, 2 * A_W), lambda j: (j, 0)), pl.BlockSpec((MT * TB, A_W), lambda j: (j, MCAT_A)),
                  pl.BlockSpec((4, CHUNK, CHUNK), lambda j: (0, 0, 0)), pl.BlockSpec((4, CHUNK, CHUNK), lambda j: (0, 0, 0)),
                  pl.BlockSpec((CHUNK, A_W), lambda j: (0, 0))],
        out_specs=[pl.BlockSpec((MT * TB, 2 * A_W), lambda j: (j, 0)), pl.BlockSpec((4, CHUNK, CHUNK), lambda j: (0, 0, 0)),
                   pl.BlockSpec((CHUNK, A_W), lambda j: (0, 0))],
        out_shape=[jax.ShapeDtypeStruct((lay.nt, 2 * A_W), MXU_DTYPE), jax.ShapeDtypeStruct((4, CHUNK, CHUNK), F32),
                   jax.ShapeDtypeStruct((CHUNK, A_W), F32)],
        compiler_params=_cp(("arbitrary",)), name=name)(z, da, ws, wst, bias)


def _band_constants():
    bands = np.zeros((2, 4, TB, TB), np.float32)
    inv = np.zeros((2, 4, TB, 1), np.float32)
    for kind, n in ((0, GRID_W), (1, TB)):
        for i, w in enumerate(POOL_WINDOWS):
            for t in range(TB):
                base, tt = (t // n) * n, t % n
                lo = min(max(tt - w // 2, 0), n)
                hi = min(max(tt - w // 2 + w, 0), n)
                bands[kind, i, t, base + lo:base + hi] = 1.0
                inv[kind, i, t, 0] = 1.0 / (hi - lo)
    return bands, inv


def _split3(x):
    a = x.astype(MXU_DTYPE)
    r1 = x - a.astype(F32)
    b = r1.astype(MXU_DTYPE)
    c = (r1 - b.astype(F32)).astype(MXU_DTYPE)
    return a, b, c


def _window_apply(band_ref, inv_ref, x, masks, transpose, mxu_exact=False):
    out = jnp.zeros_like(x)
    for i in range(4):
        xi = x * inv_ref[0, i] if transpose else x
        acc = None
        for part in ((xi.astype(MXU_DTYPE),) if mxu_exact else _split3(xi)):
            r = _tn(band_ref[0, i], part) if transpose else _nn(band_ref[0, i], part)
            acc = r if acc is None else acc + r
        if not transpose:
            acc = acc * inv_ref[0, i]
        out = jnp.where(masks[i], acc, out)
    return out


def _pool_specs(lay):
    kind = lambda j: jnp.where(j < lay.nctx // MT, 1, 0)
    return [pl.BlockSpec((1, 4, TB, TB), lambda j: (kind(j), 0, 0, 0)), pl.BlockSpec((1, 4, TB, 1), lambda j: (kind(j), 0, 0, 0))]


def _pool_fwd(lay, z, bands, inv, pw, scale, name):
    def body(p_ref, band_ref, inv_ref, pw_ref, sc_ref, o_ref):
        masks = _head_masks((TB, C_W))
        for blk in _blocks():
            p = p_ref[blk, :].astype(F32)
            diff = _window_apply(band_ref, inv_ref, p, masks, False, mxu_exact=True) - p
            o_ref[blk, :] = (_nn(diff.astype(MXU_DTYPE), pw_ref[...]) * sc_ref[...]).astype(o_ref.dtype)

    return pl.pallas_call(
        body, grid=(lay.nb // MT,),
        in_specs=[pl.BlockSpec((MT * TB, C_W), lambda j: (j, 4))] + _pool_specs(lay)
        + [pl.BlockSpec((C_W, C_W), lambda j: (0, 0)), pl.BlockSpec((1, C_W), lambda j: (0, 0))],
        out_specs=pl.BlockSpec((MT * TB, C_W), lambda j: (j, 0)),
        out_shape=jax.ShapeDtypeStruct((lay.nt, C_W), MXU_DTYPE),
        compiler_params=_cp(("parallel",)), name=name)(z, bands, inv, pw, scale)


def _pool_bwd(lay, z, dc, bands, inv, pw, scale, name):
    def body(p_ref, dc_ref, band_ref, inv_ref, pw_ref, sc_ref, dp_ref, dpw_ref, dsc_ref):
        j = pl.program_id(0)

        @pl.when(j == 0)
        def _():
            dpw_ref[...] = jnp.zeros_like(dpw_ref)
            dsc_ref[...] = jnp.zeros_like(dsc_ref)

        masks = _head_masks((TB, C_W))
        for blk in _blocks():
            p = p_ref[blk, :].astype(F32)
            dcv = dc_ref[blk, :].astype(F32)
            diff = _window_apply(band_ref, inv_ref, p, masks, False, mxu_exact=True) - p
            diff_b = diff.astype(MXU_DTYPE)
            pre = _nn(diff_b, pw_ref[...])
            dsc_ref[...] += jnp.sum(dcv * pre, axis=0, keepdims=True)
            dpre = dcv * sc_ref[...]
            dpre_b = dpre.astype(MXU_DTYPE)
            dpw_ref[...] += _tn(diff_b, dpre_b)
            ddiff = _nt(dpre_b, pw_ref[...])
            dp_ref[blk, :] = (_window_apply(band_ref, inv_ref, ddiff, masks, True) - ddiff).astype(dp_ref.dtype)

    return pl.pallas_call(
        body, grid=(lay.nb // MT,),
        in_specs=[pl.BlockSpec((MT * TB, C_W), lambda j: (j, 4)), pl.BlockSpec((MT * TB, C_W), lambda j: (j, MCAT_C))]
        + _pool_specs(lay)
        + [pl.BlockSpec((C_W, C_W), lambda j: (0, 0)), pl.BlockSpec((1, C_W), lambda j: (0, 0))],
        out_specs=[pl.BlockSpec((MT * TB, C_W), lambda j: (j, 0)), pl.BlockSpec((C_W, C_W), lambda j: (0, 0)),
                   pl.BlockSpec((1, C_W), lambda j: (0, 0))],
        out_shape=[jax.ShapeDtypeStruct((lay.nt, C_W), MXU_DTYPE), jax.ShapeDtypeStruct((C_W, C_W), F32),
                   jax.ShapeDtypeStruct((1, C_W), F32)],
        compiler_params=_cp(("arbitrary",)), name=name)(z, dc, bands, inv, pw, scale)


def _disc_math(lr, li, ldt, br, bi):
    dt = jnp.exp(ldt)
    e = jnp.exp(lr * dt)
    ar = e * jnp.cos(li * dt)
    ai = e * jnp.sin(li * dt)
    nr, ni = ar - 1.0, ai
    den = lr * lr + li * li
    qr = (nr * lr + ni * li) / den
    qi = (ni * lr - nr * li) / den
    return ar, ai, qr * br - qi * bi, qr * bi + qi * br


def _disc_fwd(lrx, lix, ldtx, brt, bit, name):
    def body(lr_ref, li_ref, ldt_ref, br_ref, bi_ref, ar_ref, ai_ref, obr_ref, obi_ref):
        ar, ai, obr, obi = _disc_math(lr_ref[...], li_ref[...], ldt_ref[...], br_ref[...], bi_ref[...])
        ar_ref[...] = ar
        ai_ref[...] = ai
        obr_ref[...] = obr
        obi_ref[...] = obi

    sh = jax.ShapeDtypeStruct(lrx.shape, F32)
    return pl.pallas_call(body, out_shape=[sh, sh, sh, sh], name=name)(lrx, lix, ldtx, brt, bit)


def _disc_bwd(lrx, lix, ldtx, brt, bit, dar, dai, dbr, dbi, name):
    nrow = lrx.shape[0] // SSM_H

    def body(lr_ref, li_ref, ldt_ref, br_ref, bi_ref, dar_ref, dai_ref, dbr_ref, dbi_ref,
             glr_ref, gli_ref, gdt_ref, gbr_ref, gbi_ref):
        _, vjp = jax.vjp(_disc_math, lr_ref[...], li_ref[...], ldt_ref[...], br_ref[...], bi_ref[...])
        glr, gli, gdt, gbr, gbi = vjp((dar_ref[...], dai_ref[...], dbr_ref[...], dbi_ref[...]))
        glr_ref[...] = jnp.sum(glr.reshape(nrow, SSM_H, SSM_P), axis=1)
        gli_ref[...] = jnp.sum(gli.reshape(nrow, SSM_H, SSM_P), axis=1)
        gdt_ref[...] = jnp.sum(jnp.sum(gdt.reshape(nrow, SSM_H, SSM_P), axis=1), axis=-1, keepdims=True)
        gbr_ref[...] = gbr
        gbi_ref[...] = gbi

    small = jax.ShapeDtypeStruct((nrow, SSM_P), F32)
    big = jax.ShapeDtypeStruct(lrx.shape, F32)
    return pl.pallas_call(body, out_shape=[small, small, jax.ShapeDtypeStruct((nrow, 1), F32), big, big],
                          name=name)(lrx, lix, ldtx, brt, bit, dar, dai, dbr, dbi)


HS = 1024
GQ, QC, QS = 8, 128, 512
LC = QS
LCB = QS // 2
SCAN_UNROLL = ST


def _scan_steps(step, carry):
    if SCAN_UNROLL >= ST:
        for s in range(ST):
            carry = step(s, carry)
        return carry

    def body(i, c):
        for j in range(SCAN_UNROLL):
            c = step(i * SCAN_UNROLL + j, c)
        return c

    return lax.fori_loop(0, ST // SCAN_UNROLL, body, carry)


def _tile_row(s):
    return s * 8 if isinstance(s, int) else pl.multiple_of(s * 8, 8)


def _dir_cat(x, d0, qq):
    xq = x[:, QC * qq:QC * qq + QC]
    zero = jnp.zeros_like(xq)
    return jnp.concatenate([jnp.where(d0, xq, zero), jnp.where(d0, zero, xq)], axis=1)


def _dir_pick(x, d0):
    return jnp.where(d0, x[:, :QC], x[:, QC:])


def _d0_rows(n):
    row = lax.broadcasted_iota(jnp.int32, (n, 1), 0)
    return jnp.bitwise_and(row, 4) == 0


def _scan_perm(bl):
    n = 2 * bl * ST
    p = np.zeros((n, n), np.float32)
    for s in range(ST):
        for d in range(2):
            for b in range(bl):
                t = s if d == 0 else ST - 1 - s
                p[s * 2 * bl + d * bl + b, d * bl * ST + b * ST + t] = 1.0
    return p


def _scan_maps(lay):
    spc = TB // ST
    nlc = lay.nlb * spc

    def fwd(k):
        return k // spc, k % spc

    def rev(k):
        cpos = nlc - 1 - jnp.maximum(k - spc, 0)
        return jnp.where(k < spc, 0, 1 + cpos // spc), jnp.where(k < spc, spc - 1 - k, cpos % spc)

    return fwd, rev


def _pack_rows(f_ref, r_ref, p_ref, rc):
    st = jnp.concatenate([f_ref[0].reshape(rc // 2, 256), r_ref[0].reshape(rc // 2, 256)], axis=0).astype(MXU_DTYPE)
    return _nn(p_ref[...], st).astype(MXU_DTYPE)


def _side_split(refs, n_in, n_out, n_scr, side):
    ns = side.n if side is not None else 0
    ins, sin = refs[:n_in], refs[n_in:n_in + ns]
    o0 = n_in + ns
    outs, sout = refs[o0:o0 + n_out], refs[o0 + n_out:o0 + n_out + ns]
    s0 = o0 + n_out + ns
    return ins + outs + refs[s0:s0 + n_scr], (sin, sout, refs[s0 + n_scr:])


def _side_start(side, srefs, first):
    if side is not None:
        @pl.when(first)
        def _():
            side.start(*srefs)


def _side_wait(side, srefs, last):
    if side is not None:
        @pl.when(last)
        def _():
            side.wait(*srefs)


def _ssm_fwd(lay, z, perm, bh, ch, ar8, ai8, name, side=None):
    bl = lay.bl
    rc = ST * 2 * bl
    nch = lay.nr * (TB // ST)
    fwd, rev = _scan_maps(lay)
    z4 = z.reshape(lay.nr, bl, TB, z.shape[1])

    def body(*refs):
        own, srefs = _side_split(refs, 7, 4, 2, side)
        uf_ref, ur_ref, p_ref, bh_ref, ch_ref, ar_ref, ai_ref, yf_ref, yr_ref, hst_ref, hsv_ref, hs, hc = own
        f, k = pl.program_id(0), pl.program_id(1)
        _side_start(side, srefs, jnp.logical_and(f == 0, k == 0))

        @pl.when(k == 0)
        def _():
            hc[...] = jnp.zeros_like(hc)

        hst_ref[0] = hc[...]
        d0 = _d0_rows(rc)
        uv = _pack_rows(uf_ref, ur_ref, p_ref, rc)
        for q in range(2):
            cr, ci = 2 * QS * q, 2 * QS * q + QS
            hs[:, cr:cr + 2 * QS] = _nn(_dir_cat(uv, d0, q), bh_ref[q])
            ar = ar_ref[:, QS * q:QS * q + QS]
            ai = ai_ref[:, QS * q:QS * q + QS]

            def step(s, carry, cr=cr, ci=ci, ar=ar, ai=ai):
                hr, hi = carry
                base = _tile_row(s)
                nr = ar * hr - ai * hi + hs[pl.ds(base, 8), cr:cr + LC]
                ni = ar * hi + ai * hr + hs[pl.ds(base, 8), ci:ci + LC]
                hs[pl.ds(base, 8), cr:cr + LC] = nr
                hs[pl.ds(base, 8), ci:ci + LC] = ni
                return nr, ni

            hr, hi = _scan_steps(step, (hc[:, cr:cr + LC], hc[:, ci:ci + LC]))
            hc[:, cr:cr + LC] = hr
            hc[:, ci:ci + LC] = hi
        hsv_ref[0] = hs[...].astype(hsv_ref.dtype)
        yi = jnp.concatenate(
            [_dir_pick(_nn(hsv_ref[0, :, 2 * QS * q:2 * QS * (q + 1)], ch_ref[q]), d0) for q in range(2)], axis=1)
        yd = _tn(p_ref[...], yi.astype(MXU_DTYPE))
        yf_ref[0] = yd[:rc // 2].reshape(bl, ST, 256).astype(yf_ref.dtype)
        yr_ref[0] = yd[rc // 2:].reshape(bl, ST, 256).astype(yr_ref.dtype)
        _side_wait(side, srefs, jnp.logical_and(f == 1, k == nch - 1))

    sd = side if side is not None else _Side([])
    blk = (1, bl, ST, 256)
    ysh = jax.ShapeDtypeStruct((lay.nr, bl, TB, B_W), MXU_DTYPE)
    outs = pl.pallas_call(
        body, grid=(2, nch),
        in_specs=[pl.BlockSpec(blk, lambda f, k: (fwd(k)[0], 0, fwd(k)[1], 2 + f)),
                  pl.BlockSpec(blk, lambda f, k: (rev(k)[0], 0, rev(k)[1], 2 + f)),
                  pl.BlockSpec((rc, rc), lambda f, k: (0, 0)),
                  pl.BlockSpec((2, 2 * QC, 2 * QS), lambda f, k: (f, 0, 0)),
                  pl.BlockSpec((2, 2 * QS, 2 * QC), lambda f, k: (f, 0, 0)),
                  pl.BlockSpec((8, HS), lambda f, k: (0, f)), pl.BlockSpec((8, HS), lambda f, k: (0, f))] + sd.in_specs,
        out_specs=[pl.BlockSpec(blk, lambda f, k: (fwd(k)[0], 0, fwd(k)[1], f)),
                   pl.BlockSpec(blk, lambda f, k: (rev(k)[0], 0, rev(k)[1], f)),
                   pl.BlockSpec((1, 8, 2 * HS), lambda f, k: (k, 0, f)),
                   pl.BlockSpec((1, rc, 2 * HS), lambda f, k: (k, 0, f))] + sd.out_specs,
        out_shape=[ysh, ysh, jax.ShapeDtypeStruct((nch, 8, 4 * HS), F32),
                   jax.ShapeDtypeStruct((nch, rc, 4 * HS), MXU_DTYPE)] + sd.out_shape,
        scratch_shapes=[pltpu.VMEM((rc, 2 * HS), F32), pltpu.VMEM((8, 2 * HS), F32)] + (sd.scratch if side is not None else []),
        compiler_params=_cp(("arbitrary", "arbitrary")), name=name)(z4, z4, perm, bh, ch, ar8, ai8, *sd.arrays)
    yf, yr, hst, hsv = outs[:4]
    return yf.reshape(lay.nt, B_W), yr.reshape(lay.nt, B_W), (hst, hsv), list(outs[4:])


def _ssm_bwd(lay, z, dy, perm, hst, bh, ch, ar8, ai8, name, side=None):
    bl = lay.bl
    rc = ST * 2 * bl
    nch = lay.nr * (TB // ST)
    fwd, rev = _scan_maps(lay)
    z4 = z.reshape(lay.nr, bl, TB, z.shape[1])
    dy4 = dy.reshape(lay.nr, bl, TB, B_W)

    hst, hsv = hst

    def body(*refs):
        own, srefs = _side_split(refs, 11, 6, 5, side)
        (uf_ref, ur_ref, dyf_ref, dyr_ref, p_ref, hst_ref, hsv_ref, bh_ref, ch_ref, ar_ref, ai_ref,
         duf_ref, dur_ref, dbh_ref, dch_ref, dar_ref, dai_ref, hs, es, ec, accr, acci) = own
        f, k = pl.program_id(0), pl.program_id(1)
        _side_start(side, srefs, jnp.logical_and(f == 0, k == 0))

        @pl.when(k == 0)
        def _():
            ec[...] = jnp.zeros_like(ec)
            accr[...] = jnp.zeros_like(accr)
            acci[...] = jnp.zeros_like(acci)
            dbh_ref[...] = jnp.zeros_like(dbh_ref)
            dch_ref[...] = jnp.zeros_like(dch_ref)

        d0 = _d0_rows(rc)
        uv = _pack_rows(uf_ref, ur_ref, p_ref, rc)
        dyv = _pack_rows(dyf_ref, dyr_ref, p_ref, rc)

        hs[0:8, :] = hst_ref[0]
        hs[8:, :] = hsv_ref[0].astype(F32)
        ucat, dycat = [], []
        for q in range(2):
            cr = 2 * QS * q
            ucat.append(_dir_cat(uv, d0, q))
            dycat.append(_dir_cat(dyv, d0, q))
            dch_ref[q] += _tn(hsv_ref[0, :, cr:cr + 2 * QS], dycat[q])
            es[:, cr:cr + 2 * QS] = _nt(dycat[q], ch_ref[q])

        dui = []
        for q in range(2):
            for j in range(QS // LCB):
                cr, ci = 2 * QS * q + LCB * j, 2 * QS * q + QS + LCB * j
                lo = QS * q + LCB * j
                ar = ar_ref[:, lo:lo + LCB]
                ai = ai_ref[:, lo:lo + LCB]

                def bstep(i, carry, cr=cr, ci=ci, ar=ar, ai=ai):
                    er, ei, sr, si = carry
                    base = _tile_row(ST - 1 - i)
                    ner = es[pl.ds(base, 8), cr:cr + LCB] + ar * er + ai * ei
                    nei = es[pl.ds(base, 8), ci:ci + LCB] - ai * er + ar * ei
                    es[pl.ds(base, 8), cr:cr + LCB] = ner
                    es[pl.ds(base, 8), ci:ci + LCB] = nei
                    hpr = hs[pl.ds(base, 8), cr:cr + LCB]
                    hpi = hs[pl.ds(base, 8), ci:ci + LCB]
                    return ner, nei, sr + ner * hpr + nei * hpi, si - ner * hpi + nei * hpr

                er, ei, sr, si = _scan_steps(
                    bstep, (ec[:, cr:cr + LCB], ec[:, ci:ci + LCB], accr[:, lo:lo + LCB], acci[:, lo:lo + LCB]))
                ec[:, cr:cr + LCB] = er
                ec[:, ci:ci + LCB] = ei
                accr[:, lo:lo + LCB] = sr
                acci[:, lo:lo + LCB] = si
            cr = 2 * QS * q
            eb = es[:, cr:cr + 2 * QS].astype(MXU_DTYPE)
            dui.append(_dir_pick(_nt(eb, bh_ref[q]), d0))
            dbh_ref[q] += _tn(ucat[q], eb)

        dud = _tn(p_ref[...], jnp.concatenate(dui, axis=1).astype(MXU_DTYPE))
        duf_ref[0] = dud[:rc // 2].reshape(bl, ST, 256).astype(duf_ref.dtype)
        dur_ref[0] = dud[rc // 2:].reshape(bl, ST, 256).astype(dur_ref.dtype)

        @pl.when(k == nch - 1)
        def _():
            for d in range(2):
                dar_ref[d:d + 1, :] = jnp.sum(accr[4 * d:4 * d + 4, :], axis=0, keepdims=True)
                dai_ref[d:d + 1, :] = jnp.sum(acci[4 * d:4 * d + 4, :], axis=0, keepdims=True)

        _side_wait(side, srefs, jnp.logical_and(f == 1, k == nch - 1))

    sd = side if side is not None else _Side([])
    last = lambda k: nch - 1 - k
    blk = (1, bl, ST, 256)
    fspec = lambda c0: pl.BlockSpec(blk, lambda f, k: (fwd(last(k))[0], 0, fwd(last(k))[1], c0 + f))
    rspec = lambda c0: pl.BlockSpec(blk, lambda f, k: (rev(last(k))[0], 0, rev(last(k))[1], c0 + f))
    dush = jax.ShapeDtypeStruct((lay.nr, bl, TB, B_W), MXU_DTYPE)
    outs = pl.pallas_call(
        body, grid=(2, nch),
        in_specs=[fspec(2), rspec(2), fspec(0), rspec(0),
                  pl.BlockSpec((rc, rc), lambda f, k: (0, 0)),
                  pl.BlockSpec((1, 8, 2 * HS), lambda f, k: (last(k), 0, f)),
                  pl.BlockSpec((1, rc, 2 * HS), lambda f, k: (last(k), 0, f)),
                  pl.BlockSpec((2, 2 * QC, 2 * QS), lambda f, k: (f, 0, 0)),
                  pl.BlockSpec((2, 2 * QS, 2 * QC), lambda f, k: (f, 0, 0)),
                  pl.BlockSpec((8, HS), lambda f, k: (0, f)), pl.BlockSpec((8, HS), lambda f, k: (0, f))] + sd.in_specs,
        out_specs=[fspec(0), rspec(0),
                   pl.BlockSpec((2, 2 * QC, 2 * QS), lambda f, k: (f, 0, 0)),
                   pl.BlockSpec((2, 2 * QS, 2 * QC), lambda f, k: (f, 0, 0)),
                   pl.BlockSpec((2, HS), lambda f, k: (0, f)), pl.BlockSpec((2, HS), lambda f, k: (0, f))] + sd.out_specs,
        out_shape=[dush, dush, jax.ShapeDtypeStruct((4, 2 * QC, 2 * QS), F32),
                   jax.ShapeDtypeStruct((4, 2 * QS, 2 * QC), F32), jax.ShapeDtypeStruct((2, 2 * HS), F32),
                   jax.ShapeDtypeStruct((2, 2 * HS), F32)] + sd.out_shape,
        scratch_shapes=[pltpu.VMEM((rc + 8, 2 * HS), F32), pltpu.VMEM((rc, 2 * HS), F32), pltpu.VMEM((8, 2 * HS), F32),
                        pltpu.VMEM((8, HS), F32), pltpu.VMEM((8, HS), F32)] + (sd.scratch if side is not None else []),
        compiler_params=_cp(("arbitrary", "arbitrary")), name=name)(
            z4, z4, dy4, dy4, perm, hst, hsv, bh, ch, ar8, ai8, *sd.arrays)
    duf, dur, dbh, dch, dar, dai = outs[:6]
    return duf.reshape(lay.nt, B_W), dur.reshape(lay.nt, B_W), dbh, dch, dar, dai, list(outs[6:])


def _glu_fwd(lay, z, yf, yr, dvec, wglu, bglu, name):
    def body(u_ref, yf_ref, yr_ref, d_ref, w_ref, b_ref, o_ref, y_ref):
        y = yf_ref[...].astype(F32) + yr_ref[...].astype(F32) + d_ref[...] * u_ref[...].astype(F32)
        y_ref[...] = y
        g = _gelu(y)
        pre = _nn(g.astype(MXU_DTYPE), w_ref[...]) + b_ref[...]
        o_ref[...] = (g * _sigmoid(pre)).astype(o_ref.dtype)

    tok = pl.BlockSpec((MT * TB, B_W), lambda j: (j, 0))
    vec = pl.BlockSpec((1, B_W), lambda j: (0, 0))
    return pl.pallas_call(
        body, grid=(lay.nb // MT,),
        in_specs=[pl.BlockSpec((MT * TB, B_W), lambda j: (j, 1)), tok, tok, vec,
                  pl.BlockSpec((B_W, B_W), lambda j: (0, 0)), vec],
        out_specs=[tok, tok],
        out_shape=[jax.ShapeDtypeStruct((lay.nt, B_W), MXU_DTYPE), jax.ShapeDtypeStruct((lay.nt, B_W), F32)],
        compiler_params=_cp(("parallel",)), name=name)(z, yf, yr, dvec, wglu, bglu)


def _glu_bwd(lay, z, y, ds, dvec, wglu, bglu, name):
    def body(u_ref, y_ref, ds_ref, d_ref, w_ref, b_ref, dy_ref, dud_ref, dw_ref, db_ref, dd_ref):
        j = pl.program_id(0)

        @pl.when(j == 0)
        def _():
            dw_ref[...] = jnp.zeros_like(dw_ref)
            db_ref[...] = jnp.zeros_like(db_ref)
            dd_ref[...] = jnp.zeros_like(dd_ref)

        yv = y_ref[...]
        g = _gelu(yv)
        gb = g.astype(MXU_DTYPE)
        sg = _sigmoid(_nn(gb, w_ref[...]) + b_ref[...])
        dsv = ds_ref[...].astype(F32)
        dpre = dsv * g * sg * (1.0 - sg)
        dpre_b = dpre.astype(MXU_DTYPE)
        dg = dsv * sg + _nt(dpre_b, w_ref[...])
        dw_ref[...] += _tn(gb, dpre_b)
        db_ref[...] += jnp.sum(dpre, axis=0, keepdims=True)
        dy = dg * _gelu_grad(yv)
        dy_ref[...] = dy.astype(dy_ref.dtype)
        dd_ref[...] += jnp.sum(dy * u_ref[...].astype(F32), axis=0, keepdims=True)
        dud_ref[...] = (dy * d_ref[...]).astype(dud_ref.dtype)

    tok = pl.BlockSpec((MT * TB, B_W), lambda j: (j, 0))
    vec = pl.BlockSpec((1, B_W), lambda j: (0, 0))
    mat = pl.BlockSpec((B_W, B_W), lambda j: (0, 0))
    vsh = jax.ShapeDtypeStruct((1, B_W), F32)
    return pl.pallas_call(
        body, grid=(lay.nb // MT,),
        in_specs=[pl.BlockSpec((MT * TB, B_W), lambda j: (j, 1)), tok, tok, vec, mat, vec],
        out_specs=[tok, tok, mat, vec, vec],
        out_shape=[jax.ShapeDtypeStruct((lay.nt, B_W), MXU_DTYPE), jax.ShapeDtypeStruct((lay.nt, B_W), F32),
                   jax.ShapeDtypeStruct((B_W, B_W), F32), vsh, vsh],
        compiler_params=_cp(("arbitrary",)), name=name)(z, y, ds, dvec, wglu, bglu)


def _dz_assemble(lay, dz_a, duf, dur, dud, dz_p, name):
    def body(a_ref, f_ref, r_ref, d_ref, p_ref, o_ref):
        o_ref[:, :2 * A_W] = a_ref[...].astype(o_ref.dtype)
        o_ref[:, 2 * A_W:2 * A_W + B_W] = (f_ref[...].astype(F32) + r_ref[...].astype(F32) + d_ref[...]).astype(o_ref.dtype)
        o_ref[:, 2 * A_W + B_W:] = p_ref[...].astype(o_ref.dtype)

    spec = lambda w: pl.BlockSpec((MT * TB, w), lambda j: (j, 0))
    return pl.pallas_call(
        body, grid=(lay.nb // MT,), in_specs=[spec(2 * A_W), spec(B_W), spec(B_W), spec(B_W), spec(C_W)],
        out_specs=spec(D_IN), out_shape=jax.ShapeDtypeStruct((lay.nt, D_IN), MXU_DTYPE),
        compiler_params=_cp(("parallel",)), name=name)(dz_a, duf, dur, dud, dz_p)


def _expand_rows(a):
    return jnp.broadcast_to(a[:, :, None, :], (2, SSM_G, SSM_H, SSM_P)).reshape(-1, SSM_P)


def _ssm_params(lam_re, lam_im, log_dt, b_re, b_im, c_re, c_im, name):
    lrx, lix = _expand_rows(lam_re), _expand_rows(lam_im)
    ldtx = _expand_rows(jnp.broadcast_to(log_dt[:, :, None], (2, SSM_G, SSM_P)))
    brt = jnp.transpose(b_re, (0, 1, 3, 2)).reshape(-1, SSM_P)
    bit = jnp.transpose(b_im, (0, 1, 3, 2)).reshape(-1, SSM_P)
    arx, aix, bbr, bbi = _disc_fwd(lrx, lix, ldtx, brt, bit, name)
    ar = arx.reshape(2, SSM_G, SSM_H, SSM_P)[:, :, 0].reshape(2, SSM_G * SSM_P)
    ai = aix.reshape(2, SSM_G, SSM_H, SSM_P)[:, :, 0].reshape(2, SSM_G * SSM_P)
    eye = jnp.eye(GQ, dtype=F32)

    def bmat(bt):
        t = bt.reshape(2, 4, GQ, SSM_H, SSM_P)
        return jnp.einsum('dqghp,gk->qdghkp', t, eye).reshape(4, 2 * QC, QS)

    bh = jnp.concatenate([bmat(bbr), bmat(bbi)], axis=-1).astype(MXU_DTYPE)

    def cmat(c):
        t = c.reshape(2, 4, GQ, SSM_H, SSM_P)
        return jnp.einsum('dqghp,gk->qgpdkh', t, eye).reshape(4, QS, 2 * QC)

    ch = jnp.concatenate([cmat(c_re), -cmat(c_im)], axis=1).astype(MXU_DTYPE)

    def rows8(a):
        return jnp.repeat(a, 4, axis=0)

    return dict(lrx=lrx, lix=lix, ldtx=ldtx, brt=brt, bit=bit, bh=bh, ch=ch, ar8=rows8(ar), ai8=rows8(ai))


def _ssm_param_grads(sp, dbh, dch, dar, dai, name):
    def bdiag(m):
        t = m.reshape(4, 2, GQ, SSM_H, GQ, SSM_P)
        return jnp.einsum('qdghgp->dqghp', t).reshape(-1, SSM_P)

    dbr, dbi = bdiag(dbh[..., :QS]), bdiag(dbh[..., QS:])

    def cdiag(m):
        t = m.reshape(4, GQ, SSM_P, 2, GQ, SSM_H)
        return jnp.einsum('qgpdgh->dqghp', t).reshape(2, SSM_G, SSM_H, SSM_P)

    dc_re, dc_im = cdiag(dch[:, :QS]), -cdiag(dch[:, QS:])

    def hrow(a):
        t = a.reshape(2, SSM_G, 1, SSM_P)
        return jnp.concatenate([t, jnp.zeros((2, SSM_G, SSM_H - 1, SSM_P), F32)], axis=2).reshape(-1, SSM_P)

    glr, gli, gdt, gbr, gbi = _disc_bwd(sp["lrx"], sp["lix"], sp["ldtx"], sp["brt"], sp["bit"],
                                        hrow(dar), hrow(dai), dbr, dbi, name)
    to_b = lambda g: jnp.transpose(g.reshape(2, SSM_G, SSM_H, SSM_P), (0, 1, 3, 2))
    return dict(ssm_lam_re=glr.reshape(2, SSM_G, SSM_P), ssm_lam_im=gli.reshape(2, SSM_G, SSM_P),
                ssm_log_dt=gdt.reshape(2, SSM_G), ssm_b_re=to_b(gbr), ssm_b_im=to_b(gbi),
                ssm_c_re=dc_re, ssm_c_im=dc_im)


def _layer_consts(p):
    c = {}
    c["ws"] = p["sgu_w"].astype(MXU_DTYPE)
    c["wst"] = jnp.transpose(p["sgu_w"], (0, 2, 1)).astype(MXU_DTYPE)
    c["gbias"] = jnp.repeat(p["sgu_b"].T, 64, axis=1)
    pw = jnp.zeros((C_W, C_W), F32)
    for i in range(4):
        pw = pw.at[64 * i:64 * i + 64, 64 * i:64 * i + 64].set(p["pool_w"][i])
    c["pw"] = pw.astype(MXU_DTYPE)
    c["pscale"] = p["pool_scale"].reshape(1, C_W)
    c["dvec"] = p["ssm_d"].reshape(1, B_W)
    c["bglu"] = p["glu_b"].reshape(1, B_W)
    return c


def _layer_fwd(lay, i, x, modarr, p, w, cst, sp, bands, inv, perm, sides=None, last=False):
    n = f"l{i}_"
    sides = sides or {}
    win_side, win_fill = sides.get("win", (None, None))
    ssm_side, ssm_fill = sides.get("ssm", (None, None))
    ffn_side, ffn_fill = sides.get("ffn", (None, None))
    res = {"x0": x}
    h = _normmod_fwd(lay, x, p["norm_mix_pre"].reshape(1, D), modarr, 0, 1, n + "nm1")
    z = _mm([(h, w["win_t"])], True, MXU_DTYPE, n + "win", side=win_side)
    if win_side is not None:
        z, extra = z
        win_fill(extra)
    a = _gate_fwd(lay, z, cst["ws"], cst["gbias"], n + "gate")
    yf, yr, hst, extra = _ssm_fwd(lay, z, perm, sp["bh"], sp["ch"], sp["ar8"], sp["ai8"], n + "ssm", ssm_side)
    if ssm_side is not None:
        ssm_fill(extra)
    s, y = _glu_fwd(lay, z, yf, yr, cst["dvec"], w["wglu"], cst["bglu"], n + "glu")
    c = _pool_fwd(lay, z, bands, inv, cst["pw"], cst["pscale"], n + "pool")
    mcat = jnp.concatenate([s, a, c], axis=1)
    res["wout_p"] = _perm_wout(w["wout"])
    m = _mm([(mcat, res["wout_p"])], False, MXU_DTYPE, n + "wout")
    x1, h2 = _resnorm_normmod_fwd(lay, x, m, p["norm_mix_post"].reshape(1, D), p["norm_ffn_pre"].reshape(1, D),
                                  modarr, 2, 3, 4, n + "rn1nm2")
    g, u, act, extra = _ffn_up(h2, w["wg_t"], w["wu_t"], n + "ffn_up", ffn_side)
    if ffn_side is not None:
        ffn_fill(extra)
    f = _mm([(act, w["wd"])], False, MXU_DTYPE, n + "ffn_down")
    res.update(h=h, z=z, hst=hst, y=y, mcat=mcat, m=m, x1=x1, h2=h2, g=g, u=u, act=act, f=f)
    if last:
        return None, res
    x2 = _resnorm_fwd(lay, x1, f, p["norm_ffn_post"].reshape(1, D), modarr, 5, n + "rn2")
    return x2, res


def _layer_bwd(lay, i, dx2, modarr, p, w, cst, sp, bands, inv, perm, res, side_fns=None):
    n = f"l{i}b_"
    big, small = {}, {}
    side_fns = side_fns or {}
    side_of = lambda key: side_fns[key](big) if key in side_fns else None
    df, dg2, gpost2 = _resnorm_bwd(lay, dx2, res["f"], p["norm_ffn_post"].reshape(1, D), modarr, 5, n + "rn2")
    big["wd"] = _mm_tn(res["act"], df, MXU_DTYPE, n + "dwd")
    dg, du, early = _ffn_down_bwd(df, w["wd"], res["g"], res["u"], n + "ffn_down", side_of("ffn_down"))
    dh2_side = side_of("dh2")
    dh2 = _mm([(dg, w["wg_t"]), (du, w["wu_t"])], False, MXU_DTYPE, n + "dh2", side=dh2_side)
    if dh2_side is not None:
        dh2, ex = dh2
        early = early + ex
    big["wg_t"] = _mm_tn(dg, res["h2"], MXU_DTYPE, n + "dwg")
    big["wu_t"] = _mm_tn(du, res["h2"], MXU_DTYPE, n + "dwu")
    dx1, dm, dsh2, dsc2, gpre2, dg1, gpost1 = _normmod_resnorm_bwd(
        lay, res["x1"], dh2, dx2, p["norm_ffn_pre"].reshape(1, D), res["m"], p["norm_mix_post"].reshape(1, D), modarr,
        4, 2, n + "nm2rn1")
    big["wout"] = _unperm_wout(_mm_tn(res["mcat"], dm, MXU_DTYPE, n + "dwout"))
    dmcat = _mm([(dm, res["wout_p"])], True, MXU_DTYPE, n + "dmcat")
    z = res["z"]
    dz_a, dws, dgb = _gate_bwd(lay, z, dmcat, cst["ws"], cst["wst"], cst["gbias"], n + "gate")
    dy, dud, dwglu, dbglu, ddvec = _glu_bwd(lay, z, res["y"], dmcat, cst["dvec"], w["wglu"], cst["bglu"], n + "glu")
    big["wglu"] = dwglu.astype(MXU_DTYPE)
    duf, dur, dbh, dch, dar, dai, ex = _ssm_bwd(lay, z, dy, perm, res["hst"], sp["bh"], sp["ch"], sp["ar8"],
                                                sp["ai8"], n + "ssm", side_of("ssm"))
    early = early + ex
    dz_p, dpw, dpsc = _pool_bwd(lay, z, dmcat, bands, inv, cst["pw"], cst["pscale"], n + "pool")
    dz = _dz_assemble(lay, dz_a, duf, dur, dud, dz_p, n + "dz")
    big["win_t"] = _mm_tn(dz, res["h"], MXU_DTYPE, n + "dwin")
    dh_side = side_of("dh")
    dh = _mm([(dz, w["win_t"])], False, MXU_DTYPE, n + "dh", side=dh_side)
    if dh_side is not None:
        dh, ex = dh
        early = early + ex
    dx, dsh1, dsc1, gpre1 = _normmod_bwd(lay, res["x0"], dh, dx1, p["norm_mix_pre"].reshape(1, D), modarr, 1, n + "nm1",
                                         latent_only=(i == 0))

    small.update(norm_mix_pre=gpre1[0], norm_mix_post=gpost1[0], norm_ffn_pre=gpre2[0], norm_ffn_post=gpost2[0])
    small["sgu_w"] = dws
    small["sgu_b"] = jnp.sum(dgb.reshape(CHUNK, 4, 64), axis=-1).T
    small.update(_ssm_param_grads(sp, dbh, dch, dar, dai, n + "disc"))
    small["ssm_d"] = ddvec.reshape(SSM_G, SSM_H)
    small["glu_b"] = dbglu[0]
    small["pool_w"] = jnp.stack([dpw[64 * k:64 * k + 64, 64 * k:64 * k + 64] for k in range(4)])
    small["pool_scale"] = dpsc[0]
    dmod = jnp.concatenate([dsh1, dsc1, dg1, dsh2, dsc2, dg2], axis=1)[:lay.bl + 1]
    dmod = jnp.concatenate([dmod, jnp.zeros((8 - lay.bl - 1, 6, D), F32)], axis=0)
    return dx, big, small, dmod, early


def _perm_wout(w):
    return w.reshape(4, D // 4, D)[np.array(WOUT_PERM)].reshape(D, D)


def _unperm_wout(g):
    return g.reshape(4, D // 4, D)[np.array(WOUT_INV)].reshape(D, D)


SMALL_NAMES = ["norm_mix_pre", "norm_mix_post", "norm_ffn_pre", "norm_ffn_post", "sgu_w", "sgu_b", "ssm_lam_re",
               "ssm_lam_im", "ssm_log_dt", "ssm_b_re", "ssm_b_im", "ssm_c_re", "ssm_c_im", "ssm_d", "glu_b", "pool_w",
               "pool_scale"]
BIG_NAMES = ["win_t", "wout", "wglu", "wg_t", "wu_t", "wd"]


def _sincos_2d(rows, cols, dim):
    quarter = dim // 4
    omega = 1.0 / (10000.0 ** (jnp.arange(quarter, dtype=F32) / quarter))
    r = jnp.arange(rows, dtype=F32)[:, None] * omega
    cc = jnp.arange(cols, dtype=F32)[:, None] * omega
    er = jnp.concatenate([jnp.sin(r), jnp.cos(r)], axis=-1)
    ec = jnp.concatenate([jnp.sin(cc), jnp.cos(cc)], axis=-1)
    pe = jnp.concatenate([jnp.broadcast_to(er[:, None, :], (rows, cols, dim // 2)),
                          jnp.broadcast_to(ec[None, :, :], (rows, cols, dim // 2))], axis=-1)
    return pe.reshape(rows * cols, dim)


def _core(x, ctx, target, mods_local, params, weights, w_sides=None, g_side_fns=None):
    bl, lat, _ = x.shape
    assert bl == 4 and lat % TB == 0, "the scan fills 8 sublanes with 2 directions x 4 sequences"
    lay = _Layout(bl, lat)
    pe = _sincos_2d(lat // GRID_W, GRID_W, D)
    bands_np, inv_np = _band_constants()
    bands, inv = jnp.asarray(bands_np, MXU_DTYPE), jnp.asarray(inv_np, F32)
    perm = jnp.asarray(_scan_perm(bl), MXU_DTYPE)
    csts, sps, ress, wls = [], [], [], []
    for i in range(2):
        csts.append(_layer_consts(params[i]))
        p = params[i]
        sps.append(_ssm_params(p["ssm_lam_re"], p["ssm_lam_im"], p["ssm_log_dt"], p["ssm_b_re"], p["ssm_b_im"],
                               p["ssm_c_re"], p["ssm_c_im"], f"l{i}_disc"))
        wls.append(dict(weights[i]))

    embed_side, embed_fill = (w_sides[0].get("embed") if w_sides else None) or (None, None)
    xt, extra = _embed(lay, x, ctx, pe, embed_side)
    if embed_side is not None:
        embed_fill(wls, extra)
    if callable(mods_local):
        mods_local = mods_local()
    modarrs = [lay.mod_tiles(mods_local[i]) for i in range(2)]
    for i in range(2):
        sides = {}
        for key, (side, fill) in ((w_sides or [{}, {}])[i]).items():
            sides[key] = (side, functools.partial(fill, wls))
        xt, res = _layer_fwd(lay, i, xt, modarrs[i], params[i], wls[i], csts[i], sps[i], bands, inv, perm, sides,
                             last=(i == 1))
        ress.append(res)
    dx, lossv = _resnorm_loss(lay, ress[1]["x1"], ress[1]["f"], params[1]["norm_ffn_post"].reshape(1, D), modarrs[1], 5,
                              target)
    bigs, smalls, dmods, early = [None, None], [None, None], [None, None], []
    for i in (1, 0):
        fns = {}
        if i == 0 and g_side_fns is not None:
            fns = {key: functools.partial(fn, bigs[1]) for key, fn in g_side_fns.items()}
        dx, bigs[i], smalls[i], dmods[i], ex = _layer_bwd(lay, i, dx, modarrs[i], params[i], wls[i], csts[i], sps[i],
                                                           bands, inv, perm, ress[i], fns)
        early += ex
    return lossv[0, 0], dx.reshape(bl, lat, D), bigs, smalls, dmods, early


def _my_index():
    return 4 * lax.axis_index("x") + 2 * lax.axis_index("y") + lax.axis_index("c")


def _peer(k):
    x, y, c = lax.axis_index("x"), lax.axis_index("y"), lax.axis_index("c")
    kx, ky, kc = (k >> 2) & 1, (k >> 1) & 1, k & 1
    px = 1 - x if kx else x
    py = 1 - y if ky else y
    pc = 1 - c if kc else c
    return (px, py, pc), 4 * px + 2 * py + pc


class _Side:
    def __init__(self, items):
        self.items = items
        self.n = len(items)
        self.ncopies = sum(len(it[2]) for it in items)
        self.arrays = [it[0] for it in items]
        anyspec = pl.BlockSpec(memory_space=pl.ANY)
        self.in_specs = [anyspec] * self.n
        self.out_specs = [anyspec] * self.n
        self.out_shape = [jax.ShapeDtypeStruct((slots,) + tuple(a.shape) if mode == "gather" else tuple(a.shape), a.dtype)
                          for a, mode, ks, slots in items]
        self.scratch = [pltpu.SemaphoreType.DMA((self.ncopies,)), pltpu.SemaphoreType.DMA((self.ncopies,)),
                        pltpu.SemaphoreType.DMA((self.n,))]

    def _copies(self, ins, outs, sems):
        send_sems, recv_sems, local_sems = sems
        slot_of = lambda idx, slots: idx if slots == 8 else (idx // 2 if slots == 4 else idx % 2)
        me = _my_index()
        local, sends, recvs = [], [], []
        q = 0
        for t, (arr, mode, ks, slots) in enumerate(self.items):
            src_own = ins[t] if mode == "gather" else ins[t].at[me]
            local.append(pltpu.make_async_copy(src_own, outs[t].at[slot_of(me, slots)], local_sems.at[t]))
            for k in ks:
                peer, pidx = _peer(k)
                src = ins[t] if mode == "gather" else ins[t].at[pidx]
                sends.append(pltpu.make_async_remote_copy(
                    src_ref=src, dst_ref=outs[t].at[slot_of(me, slots)], send_sem=send_sems.at[q], recv_sem=recv_sems.at[q],
                    device_id=peer, device_id_type=pl.DeviceIdType.MESH))
                recvs.append(pltpu.make_async_remote_copy(
                    src_ref=src, dst_ref=outs[t].at[slot_of(pidx, slots)], send_sem=send_sems.at[q], recv_sem=recv_sems.at[q],
                    device_id=peer, device_id_type=pl.DeviceIdType.MESH))
                q += 1
        return local, sends, recvs

    def start(self, ins, outs, sems):
        local, sends, _ = self._copies(ins, outs, sems)
        for cp in sends + local:
            cp.start()

    def wait(self, ins, outs, sems):
        local, sends, recvs = self._copies(ins, outs, sems)
        for cp in recvs:
            cp.wait_recv()
        for cp in sends:
            cp.wait_send()
        for cp in local:
            cp.wait()


def _comm(items, name):
    side = _Side(items)
    n = side.n

    def body(*refs):
        ins, outs, sems = refs[:n], refs[n:2 * n], refs[2 * n:]
        side.start(ins, outs, sems)
        side.wait(ins, outs, sems)

    return pl.pallas_call(
        body, in_specs=side.in_specs, out_specs=side.out_specs, out_shape=side.out_shape, scratch_shapes=side.scratch,
        compiler_params=pltpu.CompilerParams(has_side_effects=True), name=name)(*side.arrays)


def _spread(items, name):
    n = len(items)
    ncopies = sum(len(it[1]) for it in items)

    def slot_of(idx, slots):
        return idx if slots == 8 else (idx // 2 if slots == 4 else idx % 2)

    def body(*refs):
        ins, outs, bufs = refs[:n], refs[n:2 * n], refs[2 * n:3 * n]
        load_sems, store_sems, send_sems, recv_sems = refs[3 * n:]
        me = _my_index()
        loads = [pltpu.make_async_copy(ins[t], bufs[t], load_sems.at[t]) for t in range(n)]
        for cp in loads:
            cp.start()
        stores, sends, recvs = [], [], []
        q = 0
        for t, (arr, ks, slots) in enumerate(items):
            loads[t].wait()
            own = outs[t].at[slot_of(me, slots)]
            stores.append(pltpu.make_async_copy(bufs[t], own, store_sems.at[t]))
            stores[-1].start()
            for k in ks:
                peer, pidx = _peer(k)
                sends.append(pltpu.make_async_remote_copy(
                    src_ref=bufs[t], dst_ref=own, send_sem=send_sems.at[q], recv_sem=recv_sems.at[q],
                    device_id=peer, device_id_type=pl.DeviceIdType.MESH))
                recvs.append(pltpu.make_async_remote_copy(
                    src_ref=bufs[t], dst_ref=outs[t].at[slot_of(pidx, slots)], send_sem=send_sems.at[q],
                    recv_sem=recv_sems.at[q], device_id=peer, device_id_type=pl.DeviceIdType.MESH))
                sends[-1].start()
                q += 1
        for cp in recvs:
            cp.wait_recv()
        for cp in sends:
            cp.wait_send()
        for cp in stores:
            cp.wait()

    anyspec = pl.BlockSpec(memory_space=pl.ANY)
    return pl.pallas_call(
        body, in_specs=[anyspec] * n, out_specs=[anyspec] * n,
        out_shape=[jax.ShapeDtypeStruct((slots,) + tuple(arr.shape), arr.dtype) for arr, ks, slots in items],
        scratch_shapes=[pltpu.VMEM(tuple(arr.shape), arr.dtype) for arr, ks, slots in items]
        + [pltpu.SemaphoreType.DMA((n,)), pltpu.SemaphoreType.DMA((n,)), pltpu.SemaphoreType.DMA((ncopies,)),
           pltpu.SemaphoreType.DMA((ncopies,))],
        compiler_params=pltpu.CompilerParams(has_side_effects=True, vmem_limit_bytes=VMEM_LIMIT),
        name=name)(*[it[0] for it in items])


ALL7 = (1, 2, 3, 4, 5, 6, 7)
CHIPS3 = (2, 4, 6)


def _sum8(parts, name):
    def one(a, nm):
        _, r, c = a.shape
        tr = r if r <= 512 else _pick_rows(r)

        def body(a_ref, o_ref):
            acc = a_ref[0].astype(F32)
            for q in range(1, a_ref.shape[0]):
                acc = acc + a_ref[q].astype(F32)
            o_ref[...] = acc

        return pl.pallas_call(
            body, grid=(r // tr,), in_specs=[pl.BlockSpec((a.shape[0], tr, c), lambda i: (0, i, 0))],
            out_specs=pl.BlockSpec((tr, c), lambda i: (i, 0)), out_shape=jax.ShapeDtypeStruct((r, c), F32),
            compiler_params=_cp(("parallel",)), name=nm)(a)

    return [one(a, f"{name}{i}") for i, a in enumerate(parts)]


def _pick_rows(r, cap=512):
    for t in (512, 352, 256, 176, 128, 64, 32, 16, 8):
        if r % t == 0 and t <= cap:
            return t
    return r


def _adam(w, g, m, v, name):
    shape = w.shape
    nel = int(np.prod(shape))
    c1 = 1.0 / (1.0 - ADAM_B1 ** ADAM_STEP)
    c2 = 1.0 / (1.0 - ADAM_B2 ** ADAM_STEP)

    def body(w_ref, g_ref, m_ref, v_ref, d_ref, nm_ref, nv_ref):
        gv = g_ref[...]
        nm = ADAM_B1 * m_ref[...] + (1.0 - ADAM_B1) * gv
        nv = ADAM_B2 * v_ref[...] + (1.0 - ADAM_B2) * (gv * gv)
        d_ref[...] = -ADAM_LR * ((nm * c1) / (jnp.sqrt(nv * c2) + ADAM_EPS) + ADAM_WD * w_ref[...])
        nm_ref[...] = nm
        nv_ref[...] = nv

    padded = int(np.prod(shape[:-2])) * (-(-shape[-2] // 8) * 8) * (-(-shape[-1] // 128) * 128) if len(shape) >= 2 else nel
    if len(shape) >= 2 and padded <= 1024 * 1024:
        sh = jax.ShapeDtypeStruct(shape, F32)
        return pl.pallas_call(body, out_shape=[sh] * 3, compiler_params=_cp(None), name=name)(w, g, m, v)

    if len(shape) >= 2 and shape[-1] >= 128:
        lanes = shape[-1]
    else:
        lanes = 512 if nel % 512 == 0 else 128
    r = nel // lanes
    tr = r if r * lanes <= 384 * 1024 else _pick_rows(r, 384 * 1024 // lanes)

    spec = pl.BlockSpec((tr, lanes), lambda i: (i, 0))
    sh = jax.ShapeDtypeStruct((r, lanes), F32)
    outs = pl.pallas_call(
        body, grid=(r // tr,), in_specs=[spec] * 4, out_specs=[spec] * 3, out_shape=[sh] * 3,
        compiler_params=_cp(("parallel",)), name=name)(*[a.reshape(r, lanes) for a in (w, g, m, v)])
    return [o.reshape(shape) for o in outs]


def _silu(x):
    return x * _sigmoid(x)


def _mod_fwd(c_rows, w_mod, b_cols, name):
    def body(c_ref, w_ref, b_ref, o_ref):
        s = _silu(c_ref[...])
        for l in range(2):
            o_ref[l] = jnp.dot(s, w_ref[l], preferred_element_type=F32, precision=lax.Precision.HIGHEST) + b_ref[l]

    nc = w_mod.shape[2]
    return pl.pallas_call(body, out_shape=jax.ShapeDtypeStruct((2, c_rows.shape[0], nc), F32),
                          compiler_params=_cp(None), name=name)(c_rows, w_mod, b_cols)


def _mod_bwd(c_rows, w_mod, dlat, dctx8, name):
    nrow = c_rows.shape[0]
    nb = nrow - 8

    def body(c_ref, w_ref, dl_ref, dc_ref, gw_ref, gc_ref):
        s = _silu(c_ref[...])
        ctx_row = lax.broadcasted_iota(jnp.int32, (nrow, 1), 0) == nb
        gc = jnp.zeros((1, D), F32)
        for l in range(2):
            dctx = dc_ref[0, l]
            for q in range(1, 8):
                dctx = dctx + dc_ref[q, l]
            dm = dl_ref[l] + jnp.where(ctx_row, dctx, 0.0)
            gw_ref[l] = lax.dot_general(s, dm, (((0,), (0,)), ((), ())), preferred_element_type=F32,
                                        precision=lax.Precision.HIGHEST)
            gc = gc + lax.dot_general(dctx, w_ref[l], (((1,), (1,)), ((), ())), preferred_element_type=F32,
                                      precision=lax.Precision.HIGHEST)
        gc_ref[...] = gc

    nc = w_mod.shape[2]
    return pl.pallas_call(body, out_shape=[jax.ShapeDtypeStruct((2, D, nc), F32), jax.ShapeDtypeStruct((1, D), F32)],
                          compiler_params=_cp(None), name=name)(c_rows, w_mod, dlat, dctx8)


def _bmod_cctx(dmod_all, gc4, c_ctx, name):
    def body(dm_ref, gc_ref, cc_ref, gb_ref, gcc_ref):
        for l in range(2):
            acc = jnp.sum(dm_ref[0, l], axis=0, keepdims=True)
            for q in range(1, 8):
                acc = acc + jnp.sum(dm_ref[q, l], axis=0, keepdims=True)
            gb_ref[l:l + 1, :] = acc
        g = gc_ref[0] + gc_ref[1] + gc_ref[2] + gc_ref[3]
        cv = cc_ref[...]
        sg = _sigmoid(cv)
        gcc_ref[...] = g * (sg * (1.0 + cv * (1.0 - sg)))

    return pl.pallas_call(body, out_shape=[jax.ShapeDtypeStruct((2, 6 * D), F32), jax.ShapeDtypeStruct((1, D), F32)],
                          compiler_params=_cp(None), name=name)(dmod_all, gc4, c_ctx)


def kernel(x, c, ctx, c_ctx, w_mod, b_mod, norm_mix_pre, norm_mix_post, norm_ffn_pre, norm_ffn_post, w_in, w_out, sgu_w, sgu_b, ssm_lam_re, ssm_lam_im, ssm_log_dt, ssm_b_re, ssm_b_im, ssm_c_re, ssm_c_im, ssm_d, glu_w, glu_b, pool_w, pool_scale, ffn_w_gate, ffn_w_up, ffn_w_down, loss_target, m_c_ctx, m_w_mod, m_b_mod, m_norm_mix_pre, m_norm_mix_post, m_norm_ffn_pre, m_norm_ffn_post, m_w_in, m_w_out, m_sgu_w, m_sgu_b, m_ssm_lam_re, m_ssm_lam_im, m_ssm_log_dt, m_ssm_b_re, m_ssm_b_im, m_ssm_c_re, m_ssm_c_im, m_ssm_d, m_glu_w, m_glu_b, m_pool_w, m_pool_scale, m_ffn_w_gate, m_ffn_w_up, m_ffn_w_down, v_c_ctx, v_w_mod, v_b_mod, v_norm_mix_pre, v_norm_mix_post, v_norm_ffn_pre, v_norm_ffn_post, v_w_in, v_w_out, v_sgu_w, v_sgu_b, v_ssm_lam_re, v_ssm_lam_im, v_ssm_log_dt, v_ssm_b_re, v_ssm_b_im, v_ssm_c_re, v_ssm_c_im, v_ssm_d, v_glu_w, v_glu_b, v_pool_w, v_pool_scale, v_ffn_w_gate, v_ffn_w_up, v_ffn_w_down):
    wts = dict(c_ctx=c_ctx, w_mod=w_mod, b_mod=b_mod, norm_mix_pre=norm_mix_pre, norm_mix_post=norm_mix_post,
               norm_ffn_pre=norm_ffn_pre, norm_ffn_post=norm_ffn_post, w_in=w_in, w_out=w_out, sgu_w=sgu_w, sgu_b=sgu_b,
               ssm_lam_re=ssm_lam_re, ssm_lam_im=ssm_lam_im, ssm_log_dt=ssm_log_dt, ssm_b_re=ssm_b_re, ssm_b_im=ssm_b_im,
               ssm_c_re=ssm_c_re, ssm_c_im=ssm_c_im, ssm_d=ssm_d, glu_w=glu_w, glu_b=glu_b, pool_w=pool_w,
               pool_scale=pool_scale, ffn_w_gate=ffn_w_gate, ffn_w_up=ffn_w_up, ffn_w_down=ffn_w_down)
    ms = dict(c_ctx=m_c_ctx, w_mod=m_w_mod, b_mod=m_b_mod, norm_mix_pre=m_norm_mix_pre, norm_mix_post=m_norm_mix_post,
              norm_ffn_pre=m_norm_ffn_pre, norm_ffn_post=m_norm_ffn_post, w_in=m_w_in, w_out=m_w_out, sgu_w=m_sgu_w,
              sgu_b=m_sgu_b, ssm_lam_re=m_ssm_lam_re, ssm_lam_im=m_ssm_lam_im, ssm_log_dt=m_ssm_log_dt,
              ssm_b_re=m_ssm_b_re, ssm_b_im=m_ssm_b_im, ssm_c_re=m_ssm_c_re, ssm_c_im=m_ssm_c_im, ssm_d=m_ssm_d,
              glu_w=m_glu_w, glu_b=m_glu_b, pool_w=m_pool_w, pool_scale=m_pool_scale, ffn_w_gate=m_ffn_w_gate,
              ffn_w_up=m_ffn_w_up, ffn_w_down=m_ffn_w_down)
    vs = dict(c_ctx=v_c_ctx, w_mod=v_w_mod, b_mod=v_b_mod, norm_mix_pre=v_norm_mix_pre, norm_mix_post=v_norm_mix_post,
              norm_ffn_pre=v_norm_ffn_pre, norm_ffn_post=v_norm_ffn_post, w_in=v_w_in, w_out=v_w_out, sgu_w=v_sgu_w,
              sgu_b=v_sgu_b, ssm_lam_re=v_ssm_lam_re, ssm_lam_im=v_ssm_lam_im, ssm_log_dt=v_ssm_log_dt,
              ssm_b_re=v_ssm_b_re, ssm_b_im=v_ssm_b_im, ssm_c_re=v_ssm_c_re, ssm_c_im=v_ssm_c_im, ssm_d=v_ssm_d,
              glu_w=v_glu_w, glu_b=v_glu_b, pool_w=v_pool_w, pool_scale=v_pool_scale, ffn_w_gate=v_ffn_w_gate,
              ffn_w_up=v_ffn_w_up, ffn_w_down=v_ffn_w_down)
    order = list(wts.keys())
    bl = x.shape[0]
    nseq = bl * N_DEV
    me = _my_index()
    chip = me // 2
    ncol = w_mod.shape[2]

    (c_all,) = _spread([(c, ALL7, 8)], "ag_c")
    nrow = nseq + 8
    c_rows = jnp.concatenate([c_all.reshape(nseq, D), c_ctx[None], jnp.zeros((7, D), F32)], axis=0)
    b_cols = lax.dynamic_slice_in_dim(b_mod, chip * ncol, ncol, axis=1)[:, None, :]
    mod_cols = _mod_fwd(c_rows, w_mod, b_cols, "mod_fwd")
    stash = {}

    def mods_local():
        mods = jnp.transpose(stash["mod4"], (1, 2, 0, 3)).reshape(2, nrow, 6 * D)
        return jnp.concatenate([lax.dynamic_slice_in_dim(mods, me * bl, bl, axis=1), mods[:, nseq:nseq + 1],
                                jnp.zeros((2, 8 - bl - 1, 6 * D), F32)], axis=1)

    shards = {}
    for i in range(2):
        for nme, s in zip(BIG_NAMES, [w_in[i].T, w_out[i], glu_w[i], ffn_w_gate[i].T, ffn_w_up[i].T, ffn_w_down[i]]):
            shards[(i, nme)] = s.astype(MXU_DTYPE)
    weights = [{}, {}]
    ffn_names = ("wg_t", "wu_t", "wd")
    w_plan = [{"embed": [(0, "win_t")], "win": [(0, "wout"), (0, "wglu")], "ssm": [(0, "wg_t"), (0, "wu_t")],
               "ffn": [(0, "wd"), (1, "win_t"), (1, "wout"), (1, "wglu")]},
              {"ssm": [(1, "wg_t"), (1, "wu_t")], "ffn": [(1, "wd")]}]

    def w_entry(keys, more=()):
        def fill(wls, gathered):
            for (i, nme), g in zip(keys, gathered):
                wls[i][nme] = g.reshape(-1, g.shape[-1])
            for (nme, _), g in zip(more, gathered[len(keys):]):
                stash[nme] = g
        return _Side([(shards[k2], "gather", CHIPS3, 4) for k2 in keys] + [(a, "gather", CHIPS3, 4) for _, a in more]), fill

    w_sides = [{key: w_entry(keys) for key, keys in plan.items()} for plan in w_plan]
    w_sides[0]["embed"] = w_entry(w_plan[0]["embed"], more=[("mod4", mod_cols)])

    eighths = lambda g: g.reshape(8, g.shape[0] // 8, g.shape[1])
    g_plan = {"ffn_down": [(1, "win_t"), (1, "wg_t")], "dh2": [(1, "wu_t"), (1, "wout"), (1, "wglu")],
              "ssm": [(0, k) for k in BIG_NAMES if k != "win_t"] + [(1, "wd")], "dh": [(0, "win_t")]}
    early_g = g_plan["ffn_down"] + g_plan["dh2"] + g_plan["ssm"] + g_plan["dh"]

    def g_entry(keys):
        return lambda big1, big0: _Side([(eighths((big1 if i == 1 else big0)[k]), "a2a", ALL7, 8) for i, k in keys])

    g_side_fns = {key: g_entry(keys) for key, keys in g_plan.items()}

    params = [{k: wts[k][i] for k in SMALL_NAMES} for i in range(2)]
    loss_part, grad_x, bigs, smalls, dmods, early = _core(x, ctx, loss_target, mods_local, params, weights,
                                                           w_sides, g_side_fns)
    loss = lax.psum(loss_part, ("x", "y", "c"))

    dmod_local = jnp.stack([dmods[i].reshape(8, 6 * D) for i in range(2)])
    (dmod_all,) = _spread([(dmod_local, ALL7, 8)], "ag_dmod")
    dcols = lax.dynamic_slice_in_dim(dmod_all, chip * ncol, ncol, axis=3)
    dlat = jnp.transpose(dcols[:, :, :bl], (1, 0, 2, 3)).reshape(2, nseq, ncol)
    dlat = jnp.concatenate([dlat, jnp.zeros((2, 8, ncol), F32)], axis=1)
    dctx8 = dcols[:, :, bl:bl + 1]
    g_w_mod, gc_part = _mod_bwd(c_rows, w_mod, dlat, dctx8, "mod_bwd")
    (gc4,) = _spread([(gc_part, CHIPS3, 4)], "ag_cctx")
    g_b_mod, g_c_ctx = _bmod_cctx(dmod_all, gc4, c_ctx[None], "bmod_cctx")

    small_flat = jnp.concatenate([jnp.stack([smalls[i][k] for i in range(2)]).reshape(-1) for k in SMALL_NAMES])
    npad = (-small_flat.shape[0]) % (8 * 1024)
    small_flat = jnp.concatenate([small_flat, jnp.zeros((npad,), F32)])
    late = _comm([(small_flat.reshape(8, -1, 1024), "a2a", ALL7, 8)], "a2a_grads")
    sums = _sum8(list(early) + list(late), "gsum")
    fin = _spread([(s, (1,), 2) for s in sums[:-1]] + [(sums[-1], ALL7, 8)], "ag_grads")
    big_g = [{}, {}]
    for (i, k), g in zip(early_g, fin[:-1]):
        big_g[i][k] = g.reshape(-1, g.shape[-1])
    small_red = fin[-1].reshape(-1)

    grads = {}
    off = 0
    for k in SMALL_NAMES:
        shp = wts[k].shape
        nel = int(np.prod(shp))
        grads[k] = small_red[off:off + nel].reshape(shp)
        off += nel
    grads["c_ctx"] = g_c_ctx[0]
    grads["w_mod"] = g_w_mod
    grads["b_mod"] = g_b_mod
    grads["w_in"] = jnp.stack([big_g[i]["win_t"].T for i in range(2)])
    grads["w_out"] = jnp.stack([big_g[i]["wout"] for i in range(2)])
    grads["glu_w"] = jnp.stack([big_g[i]["wglu"] for i in range(2)])
    grads["ffn_w_gate"] = jnp.stack([big_g[i]["wg_t"].T for i in range(2)])
    grads["ffn_w_up"] = jnp.stack([big_g[i]["wu_t"].T for i in range(2)])
    grads["ffn_w_down"] = jnp.stack([big_g[i]["wd"] for i in range(2)])

    deltas, new_m, new_v = {}, {}, {}
    for k in order:
        deltas[k], new_m[k], new_v[k] = _adam(wts[k], grads[k], ms[k], vs[k], "adam_" + k)
    return (loss, grad_x, *[grads[k] for k in order], *[deltas[k] for k in order],
            *[new_m[k] for k in order], *[new_v[k] for k in order])
```
